```python
import jax, jax.numpy as jnp
from jax import lax
import numpy as np

D_MODEL = 1024
BATCH = 16
SEQ = 4096
DEPTH = 4

N_MIXERS = 4
HEAD_DIM = 64
FOX_HEADS = D_MODEL // HEAD_DIM
SB_HEADS = D_MODEL // HEAD_DIM
Q_BLOCK = 128
GM_CHUNK = 128
GM_WIDTH = D_MODEL
GM_GROUPS = 8
GM_GROUP_DIM = GM_WIDTH // GM_GROUPS
CONV_WIDTH = 31
FFN_HIDDEN = -(-8 * D_MODEL // (3 * 256)) * 256
DN_ALPHA = (2.0 * DEPTH) ** 0.25
DN_BETA = (8.0 * DEPTH) ** -0.25
N_GM = (DEPTH + 3) // N_MIXERS
N_FOX = (DEPTH + 2) // N_MIXERS
N_SB = (DEPTH + 1) // N_MIXERS
N_CV = DEPTH // N_MIXERS
LN_EPS = 1e-5
NEG_INF = -1e30

kernel_name = "hybrid_interleaved_gmlp_fox_stickbreak_conformer"


def _layer_norm(x, g, b):
    xf = x.astype(jnp.float32)
    mu = jnp.mean(xf, axis=-1, keepdims=True)
    xc = xf - mu
    var = jnp.mean(xc * xc, axis=-1, keepdims=True)
    return (xc * lax.rsqrt(var + LN_EPS) * g + b).astype(x.dtype)


def _gmlp_mixer(h, w_in, b_in, ln_g, ln_b, w_s, b_s, w_out):
    B, S, _ = h.shape
    z = jax.nn.gelu(h @ w_in + b_in, approximate=False)
    u, v = jnp.split(z, 2, axis=-1)
    v = _layer_norm(v, ln_g, ln_b)
    nc = S // GM_CHUNK
    v5 = v.reshape(B, nc, GM_CHUNK, GM_GROUPS, GM_GROUP_DIM)
    causal = jnp.tril(jnp.ones((GM_CHUNK, GM_CHUNK), dtype=w_s.dtype))
    w_m = w_s * causal
    sv = jnp.einsum('gts,bnsgc->bntgc', w_m, v5) + b_s.T[None, None, :, :, None]
    y = u * sv.reshape(B, S, GM_WIDTH)
    return y @ w_out


def _fox_mixer(h, w_in, b_f, w_out):
    B, S, D = h.shape
    H = FOX_HEADS
    scale = HEAD_DIM ** -0.5
    proj = h @ w_in
    q, k, v, f_logit = jnp.split(proj, [D, 2 * D, 3 * D], axis=-1)
    to_heads = lambda t: t.reshape(B, S, H, HEAD_DIM).transpose(0, 2, 1, 3)
    q, k, v = to_heads(q), to_heads(k), to_heads(v)
    log_f = jax.nn.log_sigmoid((f_logit + b_f).astype(jnp.float32))
    F = jnp.cumsum(log_f, axis=1).transpose(0, 2, 1)
    nb = S // Q_BLOCK
    kpos = jnp.arange(S)
    qb = q.reshape(B, H, nb, Q_BLOCK, HEAD_DIM).transpose(2, 0, 1, 3, 4)
    Fb = F.reshape(B, H, nb, Q_BLOCK).transpose(2, 0, 1, 3)
    pb = kpos.reshape(nb, Q_BLOCK)

    def block(args):
        q_blk, F_blk, q_pos = args
        s = jnp.einsum('bhqd,bhkd->bhqk', q_blk, k).astype(jnp.float32) * scale
        s = s + F_blk[..., None] - F[:, :, None, :]
        s = jnp.where(kpos[None, :] <= q_pos[:, None], s, NEG_INF)
        p = jax.nn.softmax(s, axis=-1).astype(v.dtype)
        return jnp.einsum('bhqk,bhkd->bhqd', p, v)

    o = lax.map(block, (qb, Fb, pb))
    o = o.transpose(1, 0, 3, 2, 4).reshape(B, S, D)
    return o @ w_out


def _stick_breaking_mixer(h, w_in, w_out):
    B, S, D = h.shape
    H = SB_HEADS
    scale = HEAD_DIM ** -0.5
    q, k, v = jnp.split(h @ w_in, 3, axis=-1)
    to_heads = lambda t: t.reshape(B, S, H, HEAD_DIM).transpose(0, 2, 1, 3)
    q, k, v = to_heads(q), to_heads(k), to_heads(v)
    nb = S // Q_BLOCK
    kpos = jnp.arange(S)
    qb = q.reshape(B, H, nb, Q_BLOCK, HEAD_DIM).transpose(2, 0, 1, 3, 4)
    pb = kpos.reshape(nb, Q_BLOCK)

    def block(args):
        q_blk, q_pos = args
        z = jnp.einsum('bhqd,bhkd->bhqk', q_blk, k).astype(jnp.float32) * scale
        mask = kpos[None, :] < q_pos[:, None]
        log_beta = jax.nn.log_sigmoid(z)
        log_1m = jnp.where(mask, jax.nn.log_sigmoid(-z), 0.0)
        rest = lax.cumsum(log_1m, axis=3, reverse=True) - log_1m
        a = jnp.where(mask, jnp.exp(log_beta + rest), 0.0).astype(v.dtype)
        return jnp.einsum('bhqk,bhkd->bhqd', a, v)

    o = lax.map(block, (qb, pb))
    o = o.transpose(1, 0, 3, 2, 4).reshape(B, S, D)
    return o @ w_out


def _conformer_conv_mixer(h, w_in, b_in, dw, dw_b, ln_g, ln_b, w_out, b_out):
    D = h.shape[-1]
    a, g = jnp.split(h @ w_in + b_in, 2, axis=-1)
    y = a * jax.nn.sigmoid(g)
    y = lax.conv_general_dilated(
        y, dw[:, None, :], window_strides=(1,), padding=[(CONV_WIDTH - 1, 0)],
        dimension_numbers=('NWC', 'WIO', 'NWC'), feature_group_count=D) + dw_b
    y = jax.nn.silu(_layer_norm(y, ln_g, ln_b))
    return y @ w_out + b_out


def _swiglu(h, w_in, w_out):
    g, u = jnp.split(h @ w_in, 2, axis=-1)
    return (jax.nn.silu(g) * u) @ w_out


def _fwd_setup_inputs(seed: int = 0) -> dict:
    key = jax.random.key(seed)
    ks = iter(jax.random.split(key, 40))
    D = D_MODEL
    f32 = jnp.float32
    nrm = lambda shape, s: jax.random.normal(next(ks), shape, f32) * s
    gain = lambda shape: 1.0 + nrm(shape, 0.02)
    return {
        "x": nrm((BATCH, SEQ, D), 1.0),
        "c": nrm((BATCH, D), 1.0),
        "mod_w": nrm((DEPTH, D, 6 * D), 0.1 * D ** -0.5),
        "mod_b": nrm((DEPTH, 6 * D), 0.01),
        "ln1_g": gain((DEPTH, D)),
        "ln1_b": nrm((DEPTH, D), 0.02),
        "ln2_g": gain((DEPTH, D)),
        "ln2_b": nrm((DEPTH, D), 0.02),
        "ffn_w_in": nrm((DEPTH, D, 2 * FFN_HIDDEN), D ** -0.5),
        "ffn_w_out": nrm((DEPTH, FFN_HIDDEN, D), FFN_HIDDEN ** -0.5 * DN_BETA),
        "gm_w_in": nrm((N_GM, D, 2 * GM_WIDTH), D ** -0.5),
        "gm_b_in": nrm((N_GM, 2 * GM_WIDTH), 0.02),
        "gm_ln_g": gain((N_GM, GM_WIDTH)),
        "gm_ln_b": nrm((N_GM, GM_WIDTH), 0.02),
        "gm_w_s": nrm((N_GM, GM_GROUPS, GM_CHUNK, GM_CHUNK), 0.5 * GM_CHUNK ** -0.5),
        "gm_b_s": gain((N_GM, GM_GROUPS, GM_CHUNK)),
        "gm_w_out": nrm((N_GM, GM_WIDTH, D), GM_WIDTH ** -0.5 * DN_BETA),
        "fox_w_in": nrm((N_FOX, D, 3 * D + FOX_HEADS), D ** -0.5),
        "fox_b_f": jax.random.uniform(next(ks), (N_FOX, FOX_HEADS), f32, 1.0, 4.0),
        "fox_w_out": nrm((N_FOX, D, D), D ** -0.5 * DN_BETA),
        "sb_w_in": nrm((N_SB, D, 3 * D), D ** -0.5),
        "sb_w_out": nrm((N_SB, D, D), D ** -0.5 * DN_BETA),
        "cv_w_in": nrm((N_CV, D, 2 * D), D ** -0.5),
        "cv_b_in": nrm((N_CV, 2 * D), 0.02),
        "cv_dw": nrm((N_CV, CONV_WIDTH, D), CONV_WIDTH ** -0.5),
        "cv_dw_b": nrm((N_CV, D), 0.02),
        "cv_ln_g": gain((N_CV, D)),
        "cv_ln_b": nrm((N_CV, D), 0.02),
        "cv_w_out": nrm((N_CV, D, D), D ** -0.5 * DN_BETA),
        "cv_b_out": nrm((N_CV, D), 0.02),
    }


def _fwd_reference(x, c, mod_w, mod_b, ln1_g, ln1_b, ln2_g, ln2_b, ffn_w_in, ffn_w_out,
              gm_w_in, gm_b_in, gm_ln_g, gm_ln_b, gm_w_s, gm_b_s, gm_w_out,
              fox_w_in, fox_b_f, fox_w_out,
              sb_w_in, sb_w_out,
              cv_w_in, cv_b_in, cv_dw, cv_dw_b, cv_ln_g, cv_ln_b, cv_w_out, cv_b_out):
    c_act = jax.nn.silu(c)
    for l in range(DEPTH):
        m, j = l % N_MIXERS, l // N_MIXERS
        mod = c_act @ mod_w[l] + mod_b[l]
        sh1, sc1, g1, sh2, sc2, g2 = [t[:, None, :] for t in jnp.split(mod, 6, axis=-1)]
        h = x * (1.0 + sc1) + sh1
        if m == 0:
            y = _gmlp_mixer(h, gm_w_in[j], gm_b_in[j], gm_ln_g[j], gm_ln_b[j],
                            gm_w_s[j], gm_b_s[j], gm_w_out[j])
        elif m == 1:
            y = _fox_mixer(h, fox_w_in[j], fox_b_f[j], fox_w_out[j])
        elif m == 2:
            y = _stick_breaking_mixer(h, sb_w_in[j], sb_w_out[j])
        else:
            y = _conformer_conv_mixer(h, cv_w_in[j], cv_b_in[j], cv_dw[j], cv_dw_b[j],
                                      cv_ln_g[j], cv_ln_b[j], cv_w_out[j], cv_b_out[j])
        x = _layer_norm(DN_ALPHA * x + (1.0 + g1) * y, ln1_g[l], ln1_b[l])
        h = x * (1.0 + sc2) + sh2
        y = _swiglu(h, ffn_w_in[l], ffn_w_out[l])
        x = _layer_norm(DN_ALPHA * x + (1.0 + g2) * y, ln2_g[l], ln2_b[l])
    return x


import jax as _jax
import jax.numpy as _jnp

TWIN_FORMAT = 'train_step'
FWD_PARAMS = ['x', 'c', 'mod_w', 'mod_b', 'ln1_g', 'ln1_b', 'ln2_g', 'ln2_b', 'ffn_w_in', 'ffn_w_out', 'gm_w_in', 'gm_b_in', 'gm_ln_g', 'gm_ln_b', 'gm_w_s', 'gm_b_s', 'gm_w_out', 'fox_w_in', 'fox_b_f', 'fox_w_out', 'sb_w_in', 'sb_w_out', 'cv_w_in', 'cv_b_in', 'cv_dw', 'cv_dw_b', 'cv_ln_g', 'cv_ln_b', 'cv_w_out', 'cv_b_out']
TWIN_WEIGHTS = ['mod_w', 'mod_b', 'ln1_g', 'ln1_b', 'ln2_g', 'ln2_b', 'ffn_w_in', 'ffn_w_out', 'gm_w_in', 'gm_b_in', 'gm_ln_g', 'gm_ln_b', 'gm_w_s', 'gm_b_s', 'gm_w_out', 'fox_w_in', 'fox_b_f', 'fox_w_out', 'sb_w_in', 'sb_w_out', 'cv_w_in', 'cv_b_in', 'cv_dw', 'cv_dw_b', 'cv_ln_g', 'cv_ln_b', 'cv_w_out', 'cv_b_out']
TWIN_DIFF_INPUT = 'x'
TWIN_INPUTS = ['x', 'c', 'mod_w', 'mod_b', 'ln1_g', 'ln1_b', 'ln2_g', 'ln2_b', 'ffn_w_in', 'ffn_w_out', 'gm_w_in', 'gm_b_in', 'gm_ln_g', 'gm_ln_b', 'gm_w_s', 'gm_b_s', 'gm_w_out', 'fox_w_in', 'fox_b_f', 'fox_w_out', 'sb_w_in', 'sb_w_out', 'cv_w_in', 'cv_b_in', 'cv_dw', 'cv_dw_b', 'cv_ln_g', 'cv_ln_b', 'cv_w_out', 'cv_b_out', 'loss_target', 'm_mod_w', 'm_mod_b', 'm_ln1_g', 'm_ln1_b', 'm_ln2_g', 'm_ln2_b', 'm_ffn_w_in', 'm_ffn_w_out', 'm_gm_w_in', 'm_gm_b_in', 'm_gm_ln_g', 'm_gm_ln_b', 'm_gm_w_s', 'm_gm_b_s', 'm_gm_w_out', 'm_fox_w_in', 'm_fox_b_f', 'm_fox_w_out', 'm_sb_w_in', 'm_sb_w_out', 'm_cv_w_in', 'm_cv_b_in', 'm_cv_dw', 'm_cv_dw_b', 'm_cv_ln_g', 'm_cv_ln_b', 'm_cv_w_out', 'm_cv_b_out', 'v_mod_w', 'v_mod_b', 'v_ln1_g', 'v_ln1_b', 'v_ln2_g', 'v_ln2_b', 'v_ffn_w_in', 'v_ffn_w_out', 'v_gm_w_in', 'v_gm_b_in', 'v_gm_ln_g', 'v_gm_ln_b', 'v_gm_w_s', 'v_gm_b_s', 'v_gm_w_out', 'v_fox_w_in', 'v_fox_b_f', 'v_fox_w_out', 'v_sb_w_in', 'v_sb_w_out', 'v_cv_w_in', 'v_cv_b_in', 'v_cv_dw', 'v_cv_dw_b', 'v_cv_ln_g', 'v_cv_ln_b', 'v_cv_w_out', 'v_cv_b_out']
TWIN_OUTPUTS = ['loss', 'grad_x', 'grad_mod_w', 'grad_mod_b', 'grad_ln1_g', 'grad_ln1_b', 'grad_ln2_g', 'grad_ln2_b', 'grad_ffn_w_in', 'grad_ffn_w_out', 'grad_gm_w_in', 'grad_gm_b_in', 'grad_gm_ln_g', 'grad_gm_ln_b', 'grad_gm_w_s', 'grad_gm_b_s', 'grad_gm_w_out', 'grad_fox_w_in', 'grad_fox_b_f', 'grad_fox_w_out', 'grad_sb_w_in', 'grad_sb_w_out', 'grad_cv_w_in', 'grad_cv_b_in', 'grad_cv_dw', 'grad_cv_dw_b', 'grad_cv_ln_g', 'grad_cv_ln_b', 'grad_cv_w_out', 'grad_cv_b_out', 'delta_mod_w', 'delta_mod_b', 'delta_ln1_g', 'delta_ln1_b', 'delta_ln2_g', 'delta_ln2_b', 'delta_ffn_w_in', 'delta_ffn_w_out', 'delta_gm_w_in', 'delta_gm_b_in', 'delta_gm_ln_g', 'delta_gm_ln_b', 'delta_gm_w_s', 'delta_gm_b_s', 'delta_gm_w_out', 'delta_fox_w_in', 'delta_fox_b_f', 'delta_fox_w_out', 'delta_sb_w_in', 'delta_sb_w_out', 'delta_cv_w_in', 'delta_cv_b_in', 'delta_cv_dw', 'delta_cv_dw_b', 'delta_cv_ln_g', 'delta_cv_ln_b', 'delta_cv_w_out', 'delta_cv_b_out', 'new_m_mod_w', 'new_m_mod_b', 'new_m_ln1_g', 'new_m_ln1_b', 'new_m_ln2_g', 'new_m_ln2_b', 'new_m_ffn_w_in', 'new_m_ffn_w_out', 'new_m_gm_w_in', 'new_m_gm_b_in', 'new_m_gm_ln_g', 'new_m_gm_ln_b', 'new_m_gm_w_s', 'new_m_gm_b_s', 'new_m_gm_w_out', 'new_m_fox_w_in', 'new_m_fox_b_f', 'new_m_fox_w_out', 'new_m_sb_w_in', 'new_m_sb_w_out', 'new_m_cv_w_in', 'new_m_cv_b_in', 'new_m_cv_dw', 'new_m_cv_dw_b', 'new_m_cv_ln_g', 'new_m_cv_ln_b', 'new_m_cv_w_out', 'new_m_cv_b_out', 'new_v_mod_w', 'new_v_mod_b', 'new_v_ln1_g', 'new_v_ln1_b', 'new_v_ln2_g', 'new_v_ln2_b', 'new_v_ffn_w_in', 'new_v_ffn_w_out', 'new_v_gm_w_in', 'new_v_gm_b_in', 'new_v_gm_ln_g', 'new_v_gm_ln_b', 'new_v_gm_w_s', 'new_v_gm_b_s', 'new_v_gm_w_out', 'new_v_fox_w_in', 'new_v_fox_b_f', 'new_v_fox_w_out', 'new_v_sb_w_in', 'new_v_sb_w_out', 'new_v_cv_w_in', 'new_v_cv_b_in', 'new_v_cv_dw', 'new_v_cv_dw_b', 'new_v_cv_ln_g', 'new_v_cv_ln_b', 'new_v_cv_w_out', 'new_v_cv_b_out']
TWIN_LEAF_KINDS = {'loss': 'loss', 'grad_x': 'grad_x', 'grad_mod_w': 'grad_w', 'grad_mod_b': 'grad_w', 'grad_ln1_g': 'grad_w', 'grad_ln1_b': 'grad_w', 'grad_ln2_g': 'grad_w', 'grad_ln2_b': 'grad_w', 'grad_ffn_w_in': 'grad_w', 'grad_ffn_w_out': 'grad_w', 'grad_gm_w_in': 'grad_w', 'grad_gm_b_in': 'grad_w', 'grad_gm_ln_g': 'grad_w', 'grad_gm_ln_b': 'grad_w', 'grad_gm_w_s': 'grad_w', 'grad_gm_b_s': 'grad_w', 'grad_gm_w_out': 'grad_w', 'grad_fox_w_in': 'grad_w', 'grad_fox_b_f': 'grad_w', 'grad_fox_w_out': 'grad_w', 'grad_sb_w_in': 'grad_w', 'grad_sb_w_out': 'grad_w', 'grad_cv_w_in': 'grad_w', 'grad_cv_b_in': 'grad_w', 'grad_cv_dw': 'grad_w', 'grad_cv_dw_b': 'grad_w', 'grad_cv_ln_g': 'grad_w', 'grad_cv_ln_b': 'grad_w', 'grad_cv_w_out': 'grad_w', 'grad_cv_b_out': 'grad_w', 'delta_mod_w': 'delta_w', 'delta_mod_b': 'delta_w', 'delta_ln1_g': 'delta_w', 'delta_ln1_b': 'delta_w', 'delta_ln2_g': 'delta_w', 'delta_ln2_b': 'delta_w', 'delta_ffn_w_in': 'delta_w', 'delta_ffn_w_out': 'delta_w', 'delta_gm_w_in': 'delta_w', 'delta_gm_b_in': 'delta_w', 'delta_gm_ln_g': 'delta_w', 'delta_gm_ln_b': 'delta_w', 'delta_gm_w_s': 'delta_w', 'delta_gm_b_s': 'delta_w', 'delta_gm_w_out': 'delta_w', 'delta_fox_w_in': 'delta_w', 'delta_fox_b_f': 'delta_w', 'delta_fox_w_out': 'delta_w', 'delta_sb_w_in': 'delta_w', 'delta_sb_w_out': 'delta_w', 'delta_cv_w_in': 'delta_w', 'delta_cv_b_in': 'delta_w', 'delta_cv_dw': 'delta_w', 'delta_cv_dw_b': 'delta_w', 'delta_cv_ln_g': 'delta_w', 'delta_cv_ln_b': 'delta_w', 'delta_cv_w_out': 'delta_w', 'delta_cv_b_out': 'delta_w', 'new_m_mod_w': 'new_m', 'new_m_mod_b': 'new_m', 'new_m_ln1_g': 'new_m', 'new_m_ln1_b': 'new_m', 'new_m_ln2_g': 'new_m', 'new_m_ln2_b': 'new_m', 'new_m_ffn_w_in': 'new_m', 'new_m_ffn_w_out': 'new_m', 'new_m_gm_w_in': 'new_m', 'new_m_gm_b_in': 'new_m', 'new_m_gm_ln_g': 'new_m', 'new_m_gm_ln_b': 'new_m', 'new_m_gm_w_s': 'new_m', 'new_m_gm_b_s': 'new_m', 'new_m_gm_w_out': 'new_m', 'new_m_fox_w_in': 'new_m', 'new_m_fox_b_f': 'new_m', 'new_m_fox_w_out': 'new_m', 'new_m_sb_w_in': 'new_m', 'new_m_sb_w_out': 'new_m', 'new_m_cv_w_in': 'new_m', 'new_m_cv_b_in': 'new_m', 'new_m_cv_dw': 'new_m', 'new_m_cv_dw_b': 'new_m', 'new_m_cv_ln_g': 'new_m', 'new_m_cv_ln_b': 'new_m', 'new_m_cv_w_out': 'new_m', 'new_m_cv_b_out': 'new_m', 'new_v_mod_w': 'new_v', 'new_v_mod_b': 'new_v', 'new_v_ln1_g': 'new_v', 'new_v_ln1_b': 'new_v', 'new_v_ln2_g': 'new_v', 'new_v_ln2_b': 'new_v', 'new_v_ffn_w_in': 'new_v', 'new_v_ffn_w_out': 'new_v', 'new_v_gm_w_in': 'new_v', 'new_v_gm_b_in': 'new_v', 'new_v_gm_ln_g': 'new_v', 'new_v_gm_ln_b': 'new_v', 'new_v_gm_w_s': 'new_v', 'new_v_gm_b_s': 'new_v', 'new_v_gm_w_out': 'new_v', 'new_v_fox_w_in': 'new_v', 'new_v_fox_b_f': 'new_v', 'new_v_fox_w_out': 'new_v', 'new_v_sb_w_in': 'new_v', 'new_v_sb_w_out': 'new_v', 'new_v_cv_w_in': 'new_v', 'new_v_cv_b_in': 'new_v', 'new_v_cv_dw': 'new_v', 'new_v_cv_dw_b': 'new_v', 'new_v_cv_ln_g': 'new_v', 'new_v_cv_ln_b': 'new_v', 'new_v_cv_w_out': 'new_v', 'new_v_cv_b_out': 'new_v'}


def _forward(args):
    return _fwd_reference(*[args[k] for k in FWD_PARAMS])


def _output_shape():
    out = _jax.eval_shape(lambda: _forward(_fwd_setup_inputs(0)))
    return out.shape, out.dtype

N_MICROBATCH = 1
ADAM_LR = 0.001
ADAM_B1 = 0.9
ADAM_B2 = 0.999
ADAM_EPS = 1e-08
ADAM_WD = 0.01
ADAM_STEP = 10
PER_EXAMPLE_BATCH_AXIS = {'x': 0, 'c': 0, 'loss_target': 0}
SHARED_INPUTS = []
_WEIGHT_DTYPES = {'mod_w': _jnp.float32, 'mod_b': _jnp.float32, 'ln1_g': _jnp.float32, 'ln1_b': _jnp.float32, 'ln2_g': _jnp.float32, 'ln2_b': _jnp.float32, 'ffn_w_in': _jnp.float32, 'ffn_w_out': _jnp.float32, 'gm_w_in': _jnp.float32, 'gm_b_in': _jnp.float32, 'gm_ln_g': _jnp.float32, 'gm_ln_b': _jnp.float32, 'gm_w_s': _jnp.float32, 'gm_b_s': _jnp.float32, 'gm_w_out': _jnp.float32, 'fox_w_in': _jnp.float32, 'fox_b_f': _jnp.float32, 'fox_w_out': _jnp.float32, 'sb_w_in': _jnp.float32, 'sb_w_out': _jnp.float32, 'cv_w_in': _jnp.float32, 'cv_b_in': _jnp.float32, 'cv_dw': _jnp.float32, 'cv_dw_b': _jnp.float32, 'cv_ln_g': _jnp.float32, 'cv_ln_b': _jnp.float32, 'cv_w_out': _jnp.float32, 'cv_b_out': _jnp.float32}
MOMENT_SCALE = {'mod_w': 3.459196e-02, 'mod_b': 6.939312e-02, 'ln1_g': 1.802375e+00, 'ln1_b': 7.286494e-01, 'ln2_g': 3.210272e+01, 'ln2_b': 3.007130e+00, 'ffn_w_in': 2.344609e-02, 'ffn_w_out': 9.099600e-02, 'gm_w_in': 3.565224e-02, 'gm_b_in': 6.202021e-02, 'gm_ln_g': 1.517913e-02, 'gm_ln_b': 1.600112e-02, 'gm_w_s': 3.051157e-02, 'gm_b_s': 4.342382e-02, 'gm_w_out': 1.419016e-01, 'fox_w_in': 2.018700e-02, 'fox_b_f': 1.325942e-01, 'fox_w_out': 5.682457e-02, 'sb_w_in': 2.916888e-02, 'sb_w_out': 1.020364e-01, 'cv_w_in': 2.953491e-02, 'cv_b_in': 4.611928e-02, 'cv_dw': 4.037264e-02, 'cv_dw_b': 1.180540e-01, 'cv_ln_g': 5.725099e-02, 'cv_ln_b': 7.718186e-02, 'cv_w_out': 1.106624e-01, 'cv_b_out': 4.346971e-01}


def _to_microbatches(a, axis):
    t = _jnp.moveaxis(a, axis, 0)
    t = t.reshape((N_MICROBATCH, t.shape[0] // N_MICROBATCH) + t.shape[1:])
    return _jnp.moveaxis(t, 1, axis + 1)


def setup_inputs(seed: int = 0) -> dict:
    inp = _fwd_setup_inputs(seed)
    key = _jax.random.fold_in(_jax.random.key(seed), 7919)
    shape, _ = _output_shape()
    out = dict(inp)
    out["loss_target"] = _jax.random.normal(_jax.random.fold_in(key, 0), shape, _jnp.float32)
    for i, name in enumerate(TWIN_WEIGHTS):
        w = inp[name].astype(_jnp.float32)
        if MOMENT_SCALE is None:
            s = _jnp.sqrt(_jnp.mean(_jnp.square(w)) + 1e-30)
        else:
            s = MOMENT_SCALE[name]
        km, kv = _jax.random.split(_jax.random.fold_in(key, i + 1))
        out[name] = w
        out["m_" + name] = s * _jax.random.normal(km, w.shape, _jnp.float32)
        out["v_" + name] = (s * s) * _jax.random.uniform(kv, w.shape, _jnp.float32, 0.5, 1.5)
    if N_MICROBATCH > 1:
        for name, axis in PER_EXAMPLE_BATCH_AXIS.items():
            out[name] = _to_microbatches(out[name], axis)
    return {'x': out['x'], 'c': out['c'], 'mod_w': out['mod_w'], 'mod_b': out['mod_b'], 'ln1_g': out['ln1_g'], 'ln1_b': out['ln1_b'], 'ln2_g': out['ln2_g'], 'ln2_b': out['ln2_b'], 'ffn_w_in': out['ffn_w_in'], 'ffn_w_out': out['ffn_w_out'], 'gm_w_in': out['gm_w_in'], 'gm_b_in': out['gm_b_in'], 'gm_ln_g': out['gm_ln_g'], 'gm_ln_b': out['gm_ln_b'], 'gm_w_s': out['gm_w_s'], 'gm_b_s': out['gm_b_s'], 'gm_w_out': out['gm_w_out'], 'fox_w_in': out['fox_w_in'], 'fox_b_f': out['fox_b_f'], 'fox_w_out': out['fox_w_out'], 'sb_w_in': out['sb_w_in'], 'sb_w_out': out['sb_w_out'], 'cv_w_in': out['cv_w_in'], 'cv_b_in': out['cv_b_in'], 'cv_dw': out['cv_dw'], 'cv_dw_b': out['cv_dw_b'], 'cv_ln_g': out['cv_ln_g'], 'cv_ln_b': out['cv_ln_b'], 'cv_w_out': out['cv_w_out'], 'cv_b_out': out['cv_b_out'], 'loss_target': out['loss_target'], 'm_mod_w': out['m_mod_w'], 'm_mod_b': out['m_mod_b'], 'm_ln1_g': out['m_ln1_g'], 'm_ln1_b': out['m_ln1_b'], 'm_ln2_g': out['m_ln2_g'], 'm_ln2_b': out['m_ln2_b'], 'm_ffn_w_in': out['m_ffn_w_in'], 'm_ffn_w_out': out['m_ffn_w_out'], 'm_gm_w_in': out['m_gm_w_in'], 'm_gm_b_in': out['m_gm_b_in'], 'm_gm_ln_g': out['m_gm_ln_g'], 'm_gm_ln_b': out['m_gm_ln_b'], 'm_gm_w_s': out['m_gm_w_s'], 'm_gm_b_s': out['m_gm_b_s'], 'm_gm_w_out': out['m_gm_w_out'], 'm_fox_w_in': out['m_fox_w_in'], 'm_fox_b_f': out['m_fox_b_f'], 'm_fox_w_out': out['m_fox_w_out'], 'm_sb_w_in': out['m_sb_w_in'], 'm_sb_w_out': out['m_sb_w_out'], 'm_cv_w_in': out['m_cv_w_in'], 'm_cv_b_in': out['m_cv_b_in'], 'm_cv_dw': out['m_cv_dw'], 'm_cv_dw_b': out['m_cv_dw_b'], 'm_cv_ln_g': out['m_cv_ln_g'], 'm_cv_ln_b': out['m_cv_ln_b'], 'm_cv_w_out': out['m_cv_w_out'], 'm_cv_b_out': out['m_cv_b_out'], 'v_mod_w': out['v_mod_w'], 'v_mod_b': out['v_mod_b'], 'v_ln1_g': out['v_ln1_g'], 'v_ln1_b': out['v_ln1_b'], 'v_ln2_g': out['v_ln2_g'], 'v_ln2_b': out['v_ln2_b'], 'v_ffn_w_in': out['v_ffn_w_in'], 'v_ffn_w_out': out['v_ffn_w_out'], 'v_gm_w_in': out['v_gm_w_in'], 'v_gm_b_in': out['v_gm_b_in'], 'v_gm_ln_g': out['v_gm_ln_g'], 'v_gm_ln_b': out['v_gm_ln_b'], 'v_gm_w_s': out['v_gm_w_s'], 'v_gm_b_s': out['v_gm_b_s'], 'v_gm_w_out': out['v_gm_w_out'], 'v_fox_w_in': out['v_fox_w_in'], 'v_fox_b_f': out['v_fox_b_f'], 'v_fox_w_out': out['v_fox_w_out'], 'v_sb_w_in': out['v_sb_w_in'], 'v_sb_w_out': out['v_sb_w_out'], 'v_cv_w_in': out['v_cv_w_in'], 'v_cv_b_in': out['v_cv_b_in'], 'v_cv_dw': out['v_cv_dw'], 'v_cv_dw_b': out['v_cv_dw_b'], 'v_cv_ln_g': out['v_cv_ln_g'], 'v_cv_ln_b': out['v_cv_ln_b'], 'v_cv_w_out': out['v_cv_w_out'], 'v_cv_b_out': out['v_cv_b_out']}


def _loss(weights, diff, rest, loss_target):
    with _jax.named_scope("forward"):
        args = {**rest, TWIN_DIFF_INPUT: diff, **{k: w.astype(_WEIGHT_DTYPES[k]) for k, w in weights.items()}}
        y = _forward(args)
    with _jax.named_scope("loss_head"):
        err = _jnp.square(y.astype(_jnp.float32) - loss_target)
        return 0.5 * _jnp.sum(_jnp.mean(err, axis=-1)) if err.ndim else 0.5 * err


def _adamw(w, g, m, v):
    m = ADAM_B1 * m + (1.0 - ADAM_B1) * g
    v = ADAM_B2 * v + (1.0 - ADAM_B2) * _jnp.square(g)
    m_hat = m / (1.0 - ADAM_B1 ** ADAM_STEP)
    v_hat = v / (1.0 - ADAM_B2 ** ADAM_STEP)
    delta = -ADAM_LR * (m_hat / (_jnp.sqrt(v_hat) + ADAM_EPS) + ADAM_WD * w)
    return delta, m, v


def reference(x, c, mod_w, mod_b, ln1_g, ln1_b, ln2_g, ln2_b, ffn_w_in, ffn_w_out, gm_w_in, gm_b_in, gm_ln_g, gm_ln_b, gm_w_s, gm_b_s, gm_w_out, fox_w_in, fox_b_f, fox_w_out, sb_w_in, sb_w_out, cv_w_in, cv_b_in, cv_dw, cv_dw_b, cv_ln_g, cv_ln_b, cv_w_out, cv_b_out, loss_target, m_mod_w, m_mod_b, m_ln1_g, m_ln1_b, m_ln2_g, m_ln2_b, m_ffn_w_in, m_ffn_w_out, m_gm_w_in, m_gm_b_in, m_gm_ln_g, m_gm_ln_b, m_gm_w_s, m_gm_b_s, m_gm_w_out, m_fox_w_in, m_fox_b_f, m_fox_w_out, m_sb_w_in, m_sb_w_out, m_cv_w_in, m_cv_b_in, m_cv_dw, m_cv_dw_b, m_cv_ln_g, m_cv_ln_b, m_cv_w_out, m_cv_b_out, v_mod_w, v_mod_b, v_ln1_g, v_ln1_b, v_ln2_g, v_ln2_b, v_ffn_w_in, v_ffn_w_out, v_gm_w_in, v_gm_b_in, v_gm_ln_g, v_gm_ln_b, v_gm_w_s, v_gm_b_s, v_gm_w_out, v_fox_w_in, v_fox_b_f, v_fox_w_out, v_sb_w_in, v_sb_w_out, v_cv_w_in, v_cv_b_in, v_cv_dw, v_cv_dw_b, v_cv_ln_g, v_cv_ln_b, v_cv_w_out, v_cv_b_out):
    given = dict(x=x, c=c, mod_w=mod_w, mod_b=mod_b, ln1_g=ln1_g, ln1_b=ln1_b, ln2_g=ln2_g, ln2_b=ln2_b, ffn_w_in=ffn_w_in, ffn_w_out=ffn_w_out, gm_w_in=gm_w_in, gm_b_in=gm_b_in, gm_ln_g=gm_ln_g, gm_ln_b=gm_ln_b, gm_w_s=gm_w_s, gm_b_s=gm_b_s, gm_w_out=gm_w_out, fox_w_in=fox_w_in, fox_b_f=fox_b_f, fox_w_out=fox_w_out, sb_w_in=sb_w_in, sb_w_out=sb_w_out, cv_w_in=cv_w_in, cv_b_in=cv_b_in, cv_dw=cv_dw, cv_dw_b=cv_dw_b, cv_ln_g=cv_ln_g, cv_ln_b=cv_ln_b, cv_w_out=cv_w_out, cv_b_out=cv_b_out, loss_target=loss_target, m_mod_w=m_mod_w, m_mod_b=m_mod_b, m_ln1_g=m_ln1_g, m_ln1_b=m_ln1_b, m_ln2_g=m_ln2_g, m_ln2_b=m_ln2_b, m_ffn_w_in=m_ffn_w_in, m_ffn_w_out=m_ffn_w_out, m_gm_w_in=m_gm_w_in, m_gm_b_in=m_gm_b_in, m_gm_ln_g=m_gm_ln_g, m_gm_ln_b=m_gm_ln_b, m_gm_w_s=m_gm_w_s, m_gm_b_s=m_gm_b_s, m_gm_w_out=m_gm_w_out, m_fox_w_in=m_fox_w_in, m_fox_b_f=m_fox_b_f, m_fox_w_out=m_fox_w_out, m_sb_w_in=m_sb_w_in, m_sb_w_out=m_sb_w_out, m_cv_w_in=m_cv_w_in, m_cv_b_in=m_cv_b_in, m_cv_dw=m_cv_dw, m_cv_dw_b=m_cv_dw_b, m_cv_ln_g=m_cv_ln_g, m_cv_ln_b=m_cv_ln_b, m_cv_w_out=m_cv_w_out, m_cv_b_out=m_cv_b_out, v_mod_w=v_mod_w, v_mod_b=v_mod_b, v_ln1_g=v_ln1_g, v_ln1_b=v_ln1_b, v_ln2_g=v_ln2_g, v_ln2_b=v_ln2_b, v_ffn_w_in=v_ffn_w_in, v_ffn_w_out=v_ffn_w_out, v_gm_w_in=v_gm_w_in, v_gm_b_in=v_gm_b_in, v_gm_ln_g=v_gm_ln_g, v_gm_ln_b=v_gm_ln_b, v_gm_w_s=v_gm_w_s, v_gm_b_s=v_gm_b_s, v_gm_w_out=v_gm_w_out, v_fox_w_in=v_fox_w_in, v_fox_b_f=v_fox_b_f, v_fox_w_out=v_fox_w_out, v_sb_w_in=v_sb_w_in, v_sb_w_out=v_sb_w_out, v_cv_w_in=v_cv_w_in, v_cv_b_in=v_cv_b_in, v_cv_dw=v_cv_dw, v_cv_dw_b=v_cv_dw_b, v_cv_ln_g=v_cv_ln_g, v_cv_ln_b=v_cv_ln_b, v_cv_w_out=v_cv_w_out, v_cv_b_out=v_cv_b_out)
    weights = {n: given[n] for n in TWIN_WEIGHTS}
    shared = {n: given[n] for n in SHARED_INPUTS}
    per_example = {n: given[n] for n in ['x', 'c']}
    grad_fn = _jax.value_and_grad(_loss, argnums=(0, 1))

    def one_microbatch(ex, loss_target):
        ex = dict(ex)
        diff = ex.pop(TWIN_DIFF_INPUT)
        return grad_fn(weights, diff, {**shared, **ex}, loss_target)

    if N_MICROBATCH == 1:
        loss, (grad_w, grad_x) = one_microbatch(per_example, given["loss_target"])
    else:
        def body(carry, xs):
            loss_sum, grad_sum = carry
            l_k, (gw_k, gx_k) = one_microbatch(xs[0], xs[1])
            with _jax.named_scope("update"):
                return (loss_sum + l_k, _jax.tree.map(_jnp.add, grad_sum, gw_k)), gx_k

        init = (_jnp.zeros((), _jnp.float32), _jax.tree.map(_jnp.zeros_like, weights))
        (loss, grad_w), grad_x = _jax.lax.scan(body, init, (per_example, given["loss_target"]))
    with _jax.named_scope("update"):
        delta_w, new_m, new_v = {}, {}, {}
        for n in TWIN_WEIGHTS:
            delta_w[n], new_m[n], new_v[n] = _adamw(weights[n], grad_w[n], given["m_" + n], given["v_" + n])
    return (loss, grad_x, *[grad_w[n] for n in TWIN_WEIGHTS], *[delta_w[n] for n in TWIN_WEIGHTS],
            *[new_m[n] for n in TWIN_WEIGHTS], *[new_v[n] for n in TWIN_WEIGHTS])
```

```python
import math

import jax
import jax.numpy as jnp
import numpy as np
from jax import lax
from jax.experimental import pallas as pl
from jax.experimental.pallas import tpu as pltpu

F32, BF16 = jnp.float32, jnp.bfloat16

HEAD_DIM = 64
CONV_WIDTH = 31
LN_EPS = 1e-5
NEG_INF = -1e30
ADAM_LR, ADAM_B1, ADAM_B2, ADAM_EPS, ADAM_WD, ADAM_STEP = 0.001, 0.9, 0.999, 1e-08, 0.01, 10

LANES = 128
SUBLANES = 8
VMEM_LIMIT_BYTES = 56 * 1024 * 1024
N_CHIPS = 4
N_DEV = 8

WEIGHTS = ['mod_w', 'mod_b', 'ln1_g', 'ln1_b', 'ln2_g', 'ln2_b', 'ffn_w_in', 'ffn_w_out', 'gm_w_in', 'gm_b_in',
           'gm_ln_g', 'gm_ln_b', 'gm_w_s', 'gm_b_s', 'gm_w_out', 'fox_w_in', 'fox_b_f', 'fox_w_out', 'sb_w_in',
           'sb_w_out', 'cv_w_in', 'cv_b_in', 'cv_dw', 'cv_dw_b', 'cv_ln_g', 'cv_ln_b', 'cv_w_out', 'cv_b_out']
ARGS = ['x', 'c'] + WEIGHTS + ['loss_target'] + ['m_' + n for n in WEIGHTS] + ['v_' + n for n in WEIGHTS]
BIG = {'ffn_w_in': 2, 'ffn_w_out': 1, 'gm_w_in': 2, 'gm_w_out': 1, 'fox_w_in': 2, 'fox_w_out': 1,
       'sb_w_in': 2, 'sb_w_out': 1, 'cv_w_in': 2, 'cv_w_out': 1}
SMALL_SPLIT = ['cv_b_in', 'cv_dw', 'cv_dw_b', 'cv_ln_g', 'cv_ln_b', 'cv_b_out']
SMALL_REPL = ['mod_b', 'ln1_g', 'ln1_b', 'ln2_g', 'ln2_b', 'gm_b_in', 'gm_ln_g', 'gm_ln_b', 'gm_w_s', 'gm_b_s', 'fox_b_f']
PACK_COLS = 1024


_names_used = {}


def _unique(name):
    k = _names_used.get(name, 0)
    _names_used[name] = k + 1
    return name if k == 0 else f"{name}_{k}"


def _params(sem):
    return pltpu.CompilerParams(dimension_semantics=sem, vmem_limit_bytes=VMEM_LIMIT_BYTES)


def _pick(dim, pref, mult=LANES):
    if dim <= pref:
        return dim
    best = 0
    for t in range(mult, pref + 1, mult):
        if dim % t == 0:
            best = t
    assert best, (dim, pref)
    return best


def _mesh_pos():
    return lax.axis_index("x"), lax.axis_index("y"), lax.axis_index("c")


def _all_gather8(name, blk):
    def body(x_ref, out_ref, send_sems, recv_sems, local_sem):
        x, y, c = _mesh_pos()
        me, sibling = (x, y, c), (x, y, 1 - c)
        chips = [(1 - x, y), (x, 1 - y), (1 - x, 1 - y)]

        def slot(px, py, pc):
            return out_ref.at[4 * px + 2 * py + pc]

        def copy(k, block, to, src=None):
            return pltpu.make_async_remote_copy(
                src_ref=slot(*block) if src is None else src, dst_ref=slot(*block),
                send_sem=send_sems.at[k], recv_sem=recv_sems.at[k],
                device_id=to, device_id_type=pl.DeviceIdType.MESH)

        mine = pltpu.make_async_copy(x_ref, slot(*me), local_sem)
        mine.start()
        first = [copy(0, me, sibling, src=x_ref)]
        first += [copy(1 + j, me, (*chip, c), src=x_ref) for j, chip in enumerate(chips)]
        for cp in first:
            cp.start()
        passed = [copy(4 + j, (*chip, c), sibling) for j, chip in enumerate(chips)]
        for j, chip in enumerate(chips):
            copy(1 + j, (*chip, c), me).wait_recv()
            passed[j].start()
        copy(0, sibling, me).wait_recv()
        for j, chip in enumerate(chips):
            copy(4 + j, (*chip, 1 - c), me).wait_recv()
        for cp in first + passed:
            cp.wait_send()
        mine.wait()

    return pl.pallas_call(
        body, name=_unique(name),
        out_shape=jax.ShapeDtypeStruct((N_DEV,) + blk.shape, blk.dtype),
        in_specs=[pl.BlockSpec(memory_space=pl.ANY)],
        out_specs=pl.BlockSpec(memory_space=pl.ANY),
        scratch_shapes=[pltpu.SemaphoreType.DMA((7,)), pltpu.SemaphoreType.DMA((7,)), pltpu.SemaphoreType.DMA],
    )(blk)


def _sibling_exchange(name, blk):
    def body(x_ref, out_ref, send_sem, recv_sem):
        x, y, c = _mesh_pos()
        cp = pltpu.make_async_remote_copy(src_ref=x_ref, dst_ref=out_ref, send_sem=send_sem, recv_sem=recv_sem,
                                          device_id=(x, y, 1 - c), device_id_type=pl.DeviceIdType.MESH)
        cp.start()
        cp.wait()

    return pl.pallas_call(
        body, name=_unique(name),
        out_shape=jax.ShapeDtypeStruct(blk.shape, blk.dtype),
        in_specs=[pl.BlockSpec(memory_space=pl.ANY)],
        out_specs=pl.BlockSpec(memory_space=pl.ANY),
        scratch_shapes=[pltpu.SemaphoreType.DMA, pltpu.SemaphoreType.DMA],
    )(blk)


def _chip_all_to_all(name, blk):
    def body(x_ref, out_ref, send_sems, recv_sems, local_sem):
        x, y, c = _mesh_pos()
        chips = [(1 - x, y), (x, 1 - y), (1 - x, 1 - y)]
        my_q = 2 * x + y
        mine = pltpu.make_async_copy(x_ref.at[my_q], out_ref.at[my_q], local_sem)
        mine.start()
        cps = [pltpu.make_async_remote_copy(
            src_ref=x_ref.at[2 * px + py], dst_ref=out_ref.at[my_q],
            send_sem=send_sems.at[j], recv_sem=recv_sems.at[j],
            device_id=(px, py, c), device_id_type=pl.DeviceIdType.MESH) for j, (px, py) in enumerate(chips)]
        for cp in cps:
            cp.start()
        for j, (px, py) in enumerate(chips):
            pltpu.make_async_remote_copy(
                src_ref=x_ref.at[my_q], dst_ref=out_ref.at[2 * px + py],
                send_sem=send_sems.at[j], recv_sem=recv_sems.at[j],
                device_id=(px, py, c), device_id_type=pl.DeviceIdType.MESH).wait_recv()
        for cp in cps:
            cp.wait_send()
        mine.wait()

    return pl.pallas_call(
        body, name=_unique(name),
        out_shape=jax.ShapeDtypeStruct(blk.shape, blk.dtype),
        in_specs=[pl.BlockSpec(memory_space=pl.ANY)],
        out_specs=pl.BlockSpec(memory_space=pl.ANY),
        scratch_shapes=[pltpu.SemaphoreType.DMA((3,)), pltpu.SemaphoreType.DMA((3,)), pltpu.SemaphoreType.DMA],
    )(blk)


def _rowwise(name, fn, rows, bvecs=(), cvecs=(), out_rows=(), out_bsums=(), out_tsums=(), tm=256):
    B, S = rows[0].shape[:2]
    tm = _pick(S, tm, SUBLANES)
    n_r, n_b, n_c = len(rows), len(bvecs), len(cvecs)
    n_or, n_ob = len(out_rows), len(out_bsums)

    def body(*refs):
        ins, outs = refs[:n_r + n_b + n_c], refs[n_r + n_b + n_c:]
        r = [ref[0] for ref in ins[:n_r]]
        bv = [ref[0] for ref in ins[n_r:n_r + n_b]]
        cv = [ref[...] for ref in ins[n_r + n_b:]]
        o_rows, o_bsums, o_tsums = fn(r, bv, cv)
        b, i = pl.program_id(0), pl.program_id(1)
        for ref, val in zip(outs[:n_or], o_rows):
            ref[0] = val.astype(ref.dtype)
        for ref, val in zip(outs[n_or:n_or + n_ob], o_bsums):
            @pl.when(i == 0)
            def _(ref=ref, val=val):
                ref[0] = val

            @pl.when(i > 0)
            def _(ref=ref, val=val):
                ref[0] += val
        for ref, val in zip(outs[n_or + n_ob:], o_tsums):
            first = jnp.logical_and(b == 0, i == 0)

            @pl.when(first)
            def _(ref=ref, val=val):
                ref[...] = val

            @pl.when(jnp.logical_not(first))
            def _(ref=ref, val=val):
                ref[...] += val

    in_specs = [pl.BlockSpec((1, tm, a.shape[2]), lambda b, i: (b, i, 0)) for a in rows]
    in_specs += [pl.BlockSpec((1, 1, a.shape[2]), lambda b, i: (b, 0, 0)) for a in bvecs]
    in_specs += [pl.BlockSpec((1, a.shape[1]), lambda b, i: (0, 0)) for a in cvecs]
    out_shape = [jax.ShapeDtypeStruct((B, S, cdim), dt) for cdim, dt in out_rows]
    out_specs = [pl.BlockSpec((1, tm, cdim), lambda b, i: (b, i, 0)) for cdim, _ in out_rows]
    out_shape += [jax.ShapeDtypeStruct((B, 1, cdim), F32) for cdim in out_bsums]
    out_specs += [pl.BlockSpec((1, 1, cdim), lambda b, i: (b, 0, 0)) for cdim in out_bsums]
    out_shape += [jax.ShapeDtypeStruct((1, cdim), F32) for cdim in out_tsums]
    out_specs += [pl.BlockSpec((1, cdim), lambda b, i: (0, 0)) for cdim in out_tsums]
    sem = ("arbitrary", "arbitrary") if out_tsums else ("parallel", "arbitrary")
    res = pl.pallas_call(body, name=_unique(name), grid=(B, S // tm), in_specs=in_specs, out_specs=out_specs,
                         out_shape=out_shape, compiler_params=_params(sem))(*rows, *bvecs, *cvecs)
    return list(res)


def _mm(name, a, b, ta=False, tb=False, out_dtype=F32, tm=512, tn=1536, tk=1536):
    M, K = (a.shape[1], a.shape[0]) if ta else a.shape
    N = b.shape[0] if tb else b.shape[1]
    assert (b.shape[1] if tb else b.shape[0]) == K, (a.shape, b.shape, ta, tb)
    tm, tn, tk = _pick(M, tm, SUBLANES if not ta else LANES), _pick(N, tn), _pick(K, tk, LANES if (not ta or tb) else SUBLANES)
    nk = K // tk
    dims = (((0 if ta else 1,), (1 if tb else 0,)), ((), ()))

    def body(a_ref, b_ref, o_ref, acc_ref):
        k = pl.program_id(2)
        p = lax.dot_general(a_ref[...].astype(BF16), b_ref[...].astype(BF16), dims, preferred_element_type=F32)
        if nk == 1:
            o_ref[...] = p.astype(o_ref.dtype)
        else:
            @pl.when(k == 0)
            def _():
                acc_ref[...] = p

            @pl.when(k > 0)
            def _():
                acc_ref[...] += p

            @pl.when(k == nk - 1)
            def _():
                o_ref[...] = acc_ref[...].astype(o_ref.dtype)

    a_spec = pl.BlockSpec((tk, tm), lambda i, j, k: (k, i)) if ta else pl.BlockSpec((tm, tk), lambda i, j, k: (i, k))
    b_spec = pl.BlockSpec((tn, tk), lambda i, j, k: (j, k)) if tb else pl.BlockSpec((tk, tn), lambda i, j, k: (k, j))
    return pl.pallas_call(
        body, name=_unique(name), grid=(M // tm, N // tn, nk), in_specs=[a_spec, b_spec],
        out_specs=pl.BlockSpec((tm, tn), lambda i, j, k: (i, j)),
        out_shape=jax.ShapeDtypeStruct((M, N), out_dtype),
        scratch_shapes=[pltpu.VMEM((tm, tn) if nk > 1 else (SUBLANES, LANES), F32)],
        compiler_params=_params(("parallel", "parallel", "arbitrary")))(a, b)


def _silu(x):
    return x * _sigmoid(x)


def _sigmoid(x):
    return 1.0 / (1.0 + jnp.exp(-x))


def _dsilu(x):
    s = _sigmoid(x)
    return s * (1.0 + x * (1.0 - s))


def _gelu(x):
    return 0.5 * x * (1.0 + lax.erf(x * np.float32(math.sqrt(0.5))))


def _dgelu(x):
    cdf = 0.5 * (1.0 + lax.erf(x * np.float32(math.sqrt(0.5))))
    pdf = jnp.exp(-0.5 * x * x) * np.float32(1.0 / math.sqrt(2.0 * math.pi))
    return cdf + x * pdf


def _ln_stats(r):
    mu = jnp.mean(r, axis=-1, keepdims=True)
    xc = r - mu
    var = jnp.mean(xc * xc, axis=-1, keepdims=True)
    rstd = lax.rsqrt(var + LN_EPS)
    return xc * rstd, rstd


def _ln_bwd(dxhat, xhat, rstd):
    m1 = jnp.mean(dxhat, axis=-1, keepdims=True)
    m2 = jnp.mean(dxhat * xhat, axis=-1, keepdims=True)
    return rstd * (dxhat - m1 - xhat * m2)


def _csum(v):
    return jnp.sum(v, axis=0, keepdims=True)


def _split_dot(x, m01, lhs01=False, terms=2):
    acc, rem = None, x
    for _ in range(terms):
        part = rem.astype(BF16)
        rem = rem - part.astype(F32)
        d = jnp.dot(m01, part, preferred_element_type=F32) if lhs01 else jnp.dot(part, m01, preferred_element_type=F32)
        acc = d if acc is None else acc + d
    return acc


def _iota2(shape, dim):
    return lax.broadcasted_iota(jnp.int32, shape, dim)


def _modulate(name, x, sc, sh):
    D = x.shape[2]
    return _rowwise(name, lambda r, bv, cv: ([r[0] * (1.0 + bv[0]) + bv[1]], [], []),
                    [x], [sc, sh], [], out_rows=[(D, BF16)])[0]


def _resid_ln(name, alpha, x, y, g, ln_g, ln_b, ybias=None):
    D = x.shape[2]

    def fn(r, bv, cv):
        yy = r[1] if ybias is None else r[1] + cv[2]
        xhat, _ = _ln_stats(alpha * r[0] + (1.0 + bv[0]) * yy)
        return [xhat * cv[0] + cv[1]], [], []
    cvecs = [ln_g, ln_b] + ([] if ybias is None else [ybias])
    return _rowwise(name, fn, [x, y], [g], cvecs, out_rows=[(D, F32)])[0]


def _resid_ln_bwd(name, alpha, dxn, x, y, g, ln_g, ybias=None):
    D = x.shape[2]

    def fn(r, bv, cv):
        yy = r[2] if ybias is None else r[2] + cv[1]
        xhat, rstd = _ln_stats(alpha * r[1] + (1.0 + bv[0]) * yy)
        dr = _ln_bwd(r[0] * cv[0], xhat, rstd)
        dy = (1.0 + bv[0]) * dr
        return [dr, dy], [_csum(dr * yy)], [_csum(r[0] * xhat), _csum(r[0]), _csum(dy)]
    cvecs = [ln_g] + ([] if ybias is None else [ybias])
    return _rowwise(name, fn, [dxn, x, y], [g], cvecs, out_rows=[(D, F32), (D, BF16)], out_bsums=[D], out_tsums=[D, D, D])


def _modulate_bwd(name, alpha, dh, dr, x, sc):
    D = x.shape[2]

    def fn(r, bv, cv):
        return [alpha * r[1] + r[0] * (1.0 + bv[0])], [_csum(r[0] * r[2]), _csum(r[0])], []
    return _rowwise(name, fn, [dh, dr, x], [sc], [], out_rows=[(D, F32)], out_bsums=[D, D])


def _loss_head(name, y, target):
    D = y.shape[2]

    def fn(r, bv, cv):
        e = r[0] - r[1]
        return [e * np.float32(1.0 / D)], [_csum(e * e)], []
    return _rowwise(name, fn, [y, target], [], [], out_rows=[(D, F32)], out_bsums=[D])


def _swiglu_act(name, z):
    Hd = z.shape[2] // 2
    return _rowwise(name, lambda r, bv, cv: ([_silu(r[0][:, :Hd]) * r[0][:, Hd:]], [], []), [z], out_rows=[(Hd, BF16)])[0]


def _swiglu_act_bwd(name, da, z):
    Hd = z.shape[2] // 2

    def fn(r, bv, cv):
        gg, u = r[1][:, :Hd], r[1][:, Hd:]
        return [jnp.concatenate([r[0] * u * _dsilu(gg), r[0] * _silu(gg)], axis=1)], [], []
    return _rowwise(name, fn, [da, z], out_rows=[(2 * Hd, BF16)])[0]


def _gm_act(name, zin, b_in, ln_g, ln_b):
    W = zin.shape[2] // 2

    def fn(r, bv, cv):
        z = _gelu(r[0] + cv[0])
        vhat, _ = _ln_stats(z[:, W:])
        return [z[:, :W], vhat * cv[1] + cv[2]], [], []
    return _rowwise(name, fn, [zin], [], [b_in, ln_g, ln_b], out_rows=[(W, F32), (W, BF16)])


def _gm_act_bwd(name, zin, du, dvn, b_in, ln_g):
    W = zin.shape[2] // 2

    def fn(r, bv, cv):
        zz = r[0] + cv[0]
        z = _gelu(zz)
        vhat, rstd = _ln_stats(z[:, W:])
        dv = _ln_bwd(r[2] * cv[1], vhat, rstd)
        dzin = jnp.concatenate([r[1], dv], axis=1) * _dgelu(zz)
        return [dzin], [], [_csum(dzin), _csum(r[2] * vhat), _csum(r[2])]
    return _rowwise(name, fn, [zin, du, dvn], [], [b_in, ln_g], out_rows=[(2 * W, BF16)], out_tsums=[2 * W, W, W])


def _gm_causal_w(ws_ref, g):
    T = ws_ref.shape[1]
    return jnp.where(_iota2((T, T), 1) <= _iota2((T, T), 0), ws_ref[g], 0.0).astype(BF16)


def _gm_spatial(name, u, vn, w_s, b_sT):
    B, S, W = u.shape
    G, T = w_s.shape[0], w_s.shape[1]
    assert W == G * T, "a head group is as wide as a chunk is long"

    def body(u_ref, vn_ref, ws_ref, bs_ref, y_ref):
        for g in range(G):
            cs = slice(g * T, (g + 1) * T)
            sv = jnp.dot(_gm_causal_w(ws_ref, g), vn_ref[0, :, cs], preferred_element_type=F32) + bs_ref[:, g:g + 1]
            y_ref[0, :, cs] = (u_ref[0, :, cs] * sv).astype(y_ref.dtype)

    row = pl.BlockSpec((1, T, W), lambda b, i: (b, i, 0))
    return pl.pallas_call(
        body, name=_unique(name), grid=(B, S // T),
        in_specs=[row, row, pl.BlockSpec((G, T, T), lambda b, i: (0, 0, 0)), pl.BlockSpec((T, G), lambda b, i: (0, 0))],
        out_specs=row, out_shape=jax.ShapeDtypeStruct((B, S, W), BF16),
        compiler_params=_params(("parallel", "parallel")))(u, vn, w_s, b_sT)


def _gm_spatial_bwd(name, dyg, u, vn, w_s, b_sT):
    B, S, W = u.shape
    G, T = w_s.shape[0], w_s.shape[1]
    assert W == G * T, "a head group is as wide as a chunk is long"

    def body(dy_ref, u_ref, vn_ref, ws_ref, bs_ref, du_ref, dvn_ref, dws_ref, dbs_ref):
        first = jnp.logical_and(pl.program_id(0) == 0, pl.program_id(1) == 0)

        @pl.when(first)
        def _():
            dws_ref[...] = jnp.zeros_like(dws_ref)
            dbs_ref[...] = jnp.zeros_like(dbs_ref)

        tril = _iota2((T, T), 1) <= _iota2((T, T), 0)
        for g in range(G):
            cs = slice(g * T, (g + 1) * T)
            wm = _gm_causal_w(ws_ref, g)
            vng = vn_ref[0, :, cs]
            sv = jnp.dot(wm, vng, preferred_element_type=F32) + bs_ref[:, g:g + 1]
            dy = dy_ref[0, :, cs]
            du_ref[0, :, cs] = dy * sv
            dsv = dy * u_ref[0, :, cs]
            dsv16 = dsv.astype(BF16)
            dvn_ref[0, :, cs] = lax.dot_general(wm, dsv16, (((0,), (0,)), ((), ())), preferred_element_type=F32)
            dw = lax.dot_general(dsv16, vng, (((1,), (1,)), ((), ())), preferred_element_type=F32)
            dws_ref[g] += jnp.where(tril, dw, 0.0)
            dbs_ref[:, g:g + 1] += jnp.sum(dsv, axis=1, keepdims=True)

    row = pl.BlockSpec((1, T, W), lambda b, i: (b, i, 0))
    return pl.pallas_call(
        body, name=_unique(name), grid=(B, S // T),
        in_specs=[row, row, row, pl.BlockSpec((G, T, T), lambda b, i: (0, 0, 0)), pl.BlockSpec((T, G), lambda b, i: (0, 0))],
        out_specs=[row, row, pl.BlockSpec((G, T, T), lambda b, i: (0, 0, 0)), pl.BlockSpec((T, G), lambda b, i: (0, 0))],
        out_shape=[jax.ShapeDtypeStruct((B, S, W), F32), jax.ShapeDtypeStruct((B, S, W), F32),
                   jax.ShapeDtypeStruct((G, T, T), F32), jax.ShapeDtypeStruct((T, G), F32)],
        compiler_params=_params(("arbitrary", "arbitrary")))(dyg, u, vn, w_s, b_sT)


def _cv_glu(name, pw, b_in):
    W = pw.shape[2] // 2

    def fn(r, bv, cv):
        z = r[0] + cv[0]
        return [z[:, :W] * _sigmoid(z[:, W:])], [], []
    return _rowwise(name, fn, [pw], [], [b_in], out_rows=[(W, F32)])[0]


def _cv_glu_bwd(name, pw, dyg, b_in):
    W = pw.shape[2] // 2

    def fn(r, bv, cv):
        z = r[0] + cv[0]
        a, s = z[:, :W], _sigmoid(z[:, W:])
        dpw = jnp.concatenate([r[1] * s, r[1] * a * s * (1.0 - s)], axis=1)
        return [dpw], [], [_csum(dpw)]
    return _rowwise(name, fn, [pw, dyg], [], [b_in], out_rows=[(2 * W, BF16)], out_tsums=[2 * W])


def _cv_ln_act(name, yc, ln_g, ln_b):
    D = yc.shape[2]

    def fn(r, bv, cv):
        xhat, _ = _ln_stats(r[0])
        return [_silu(xhat * cv[0] + cv[1])], [], []
    return _rowwise(name, fn, [yc], [], [ln_g, ln_b], out_rows=[(D, BF16)])[0]


def _cv_ln_act_bwd(name, yc, dys, ln_g, ln_b):
    D = yc.shape[2]

    def fn(r, bv, cv):
        xhat, rstd = _ln_stats(r[0])
        dyn = r[1] * _dsilu(xhat * cv[0] + cv[1])
        dyc = _ln_bwd(dyn * cv[0], xhat, rstd)
        return [dyc], [], [_csum(dyn * xhat), _csum(dyn), _csum(dyc)]
    return _rowwise(name, fn, [yc, dys], [], [ln_g, ln_b], out_rows=[(D, F32)], out_tsums=[D, D, D])


CONV_HALO = 32
CONV_TS, CONV_TC = 256, 128


def _dwconv(name, y, dw, dw_b):
    B, S, D = y.shape
    ts, tc, halo, K = _pick(S, CONV_TS, SUBLANES), _pick(D, CONV_TC), CONV_HALO, CONV_WIDTH

    def body(cur_ref, prev_ref, dw_ref, b_ref, o_ref, buf):
        i = pl.program_id(1)
        buf[pl.ds(0, halo), :] = jnp.where(i > 0, prev_ref[0, pl.ds(ts - halo, halo), :], 0.0)
        buf[pl.ds(halo, ts), :] = cur_ref[0]
        acc = jnp.zeros((ts, tc), F32) + b_ref[...]
        for k in range(K):
            acc = acc + dw_ref[k:k + 1, :] * buf[pl.ds(halo - (K - 1) + k, ts), :]
        o_ref[0] = acc

    return pl.pallas_call(
        body, name=_unique(name), grid=(B, S // ts, D // tc),
        in_specs=[pl.BlockSpec((1, ts, tc), lambda b, i, j: (b, i, j)),
                  pl.BlockSpec((1, ts, tc), lambda b, i, j: (b, jnp.maximum(i - 1, 0), j)),
                  pl.BlockSpec((halo, tc), lambda b, i, j: (0, j)), pl.BlockSpec((1, tc), lambda b, i, j: (0, j))],
        out_specs=pl.BlockSpec((1, ts, tc), lambda b, i, j: (b, i, j)),
        out_shape=jax.ShapeDtypeStruct((B, S, D), F32),
        scratch_shapes=[pltpu.VMEM((halo + ts, tc), F32)],
        compiler_params=_params(("parallel", "parallel", "parallel")))(y, y, dw, dw_b)


def _dwconv_bwd(name, dyc, y, dw):
    B, S, D = y.shape
    ts, tc, halo, K = _pick(S, CONV_TS, SUBLANES), _pick(D, CONV_TC), CONV_HALO, CONV_WIDTH
    nt = S // ts

    def body(g_ref, gnext_ref, y_ref, yprev_ref, dw_ref, dy_ref, ddw_ref, gbuf, ybuf):
        b, i = pl.program_id(1), pl.program_id(2)
        first = jnp.logical_and(b == 0, i == 0)

        @pl.when(first)
        def _():
            ddw_ref[...] = jnp.zeros_like(ddw_ref)

        g = g_ref[0]
        gbuf[pl.ds(0, ts), :] = g
        gbuf[pl.ds(ts, halo), :] = jnp.where(i < nt - 1, gnext_ref[0, pl.ds(0, halo), :], 0.0)
        ybuf[pl.ds(0, halo), :] = jnp.where(i > 0, yprev_ref[0, pl.ds(ts - halo, halo), :], 0.0)
        ybuf[pl.ds(halo, ts), :] = y_ref[0]
        acc = jnp.zeros((ts, tc), F32)
        for k in range(K):
            acc = acc + dw_ref[k:k + 1, :] * gbuf[pl.ds(K - 1 - k, ts), :]
            ddw_ref[k:k + 1, :] += _csum(g * ybuf[pl.ds(halo - (K - 1) + k, ts), :])
        dy_ref[0] = acc

    tile = lambda f: pl.BlockSpec((1, ts, tc), f)
    return pl.pallas_call(
        body, name=_unique(name), grid=(D // tc, B, nt),
        in_specs=[tile(lambda j, b, i: (b, i, j)), tile(lambda j, b, i: (b, jnp.minimum(i + 1, nt - 1), j)),
                  tile(lambda j, b, i: (b, i, j)), tile(lambda j, b, i: (b, jnp.maximum(i - 1, 0), j)),
                  pl.BlockSpec((halo, tc), lambda j, b, i: (0, j))],
        out_specs=[tile(lambda j, b, i: (b, i, j)), pl.BlockSpec((halo, tc), lambda j, b, i: (0, j))],
        out_shape=[jax.ShapeDtypeStruct((B, S, D), F32), jax.ShapeDtypeStruct((halo, D), F32)],
        scratch_shapes=[pltpu.VMEM((ts + halo, tc), F32), pltpu.VMEM((halo + ts, tc), F32)],
        compiler_params=_params(("parallel", "arbitrary", "arbitrary")))(dyc, dyc, y, y, dw)


ATT_BLOCK = 128
FOX_GATE_COLS = 128


def _log_sigmoid(x):
    return jnp.minimum(x, 0.0) - jnp.log(1.0 + jnp.exp(-jnp.abs(x)))


def _fox_gate_cumsum(name, fl, b_f):
    B, S, C = fl.shape
    T = _pick(S, ATT_BLOCK, SUBLANES)

    def body(fl_ref, bf_ref, f_ref, carry):
        @pl.when(pl.program_id(1) == 0)
        def _():
            carry[...] = jnp.zeros_like(carry)
        lf = _log_sigmoid(fl_ref[0] + bf_ref[...])
        lower = (_iota2((T, T), 1) <= _iota2((T, T), 0)).astype(BF16)
        f = _split_dot(lf, lower, lhs01=True, terms=3) + carry[...]
        f_ref[0] = f
        carry[...] = f[T - 1:T, :]

    return pl.pallas_call(
        body, name=_unique(name), grid=(B, S // T),
        in_specs=[pl.BlockSpec((1, T, C), lambda b, i: (b, i, 0)), pl.BlockSpec((1, C), lambda b, i: (0, 0))],
        out_specs=pl.BlockSpec((1, T, C), lambda b, i: (b, i, 0)),
        out_shape=jax.ShapeDtypeStruct((B, S, C), F32),
        scratch_shapes=[pltpu.VMEM((1, C), F32)],
        compiler_params=_params(("arbitrary", "arbitrary")))(fl, b_f)


def _fox_gate_bwd(name, dF, fl, b_f, n_heads):
    B, S, C = fl.shape
    T = _pick(S, ATT_BLOCK, SUBLANES)
    nt = S // T

    def body(df_ref, fl_ref, bf_ref, dfl_ref, dbf_ref, carry):
        first = jnp.logical_and(pl.program_id(0) == 0, pl.program_id(1) == 0)

        @pl.when(pl.program_id(1) == 0)
        def _():
            carry[...] = jnp.zeros_like(carry)

        @pl.when(first)
        def _():
            dbf_ref[...] = jnp.zeros_like(dbf_ref)

        upper = (_iota2((T, T), 1) >= _iota2((T, T), 0)).astype(BF16)
        dlf = _split_dot(df_ref[0], upper, lhs01=True, terms=3) + carry[...]
        carry[...] = dlf[0:1, :]
        x = fl_ref[0] + bf_ref[...]
        dfl = jnp.where(_iota2((T, C), 1) < n_heads, dlf * _sigmoid(-x), 0.0)
        dfl_ref[0] = dfl
        dbf_ref[...] += _csum(dfl)

    rev = lambda b, i: (b, nt - 1 - i, 0)
    return pl.pallas_call(
        body, name=_unique(name), grid=(B, nt),
        in_specs=[pl.BlockSpec((1, T, C), rev), pl.BlockSpec((1, T, C), rev), pl.BlockSpec((1, C), lambda b, i: (0, 0))],
        out_specs=[pl.BlockSpec((1, T, C), rev), pl.BlockSpec((1, C), lambda b, i: (0, 0))],
        out_shape=[jax.ShapeDtypeStruct((B, S, C), F32), jax.ShapeDtypeStruct((1, C), F32)],
        scratch_shapes=[pltpu.VMEM((1, C), F32)],
        compiler_params=_params(("arbitrary", "arbitrary")))(dF, fl, b_f)


_NT = (((1,), (1,)), ((), ()))
_TN = (((0,), (0,)), ((), ()))


def _fox_fwd(name, q, k, v, fq, fk):
    B, H, S, dh = q.shape
    T = _pick(S, ATT_BLOCK, SUBLANES)
    scale = np.float32(dh ** -0.5)

    def body(q_ref, k_ref, v_ref, fq_ref, fk_ref, o_ref, lse_ref):
        i = pl.program_id(2)
        qb, fqb = q_ref[0, 0], fq_ref[0, 0]
        rows = i * T + _iota2((T, T), 0)

        def step(j, carry):
            m, l, acc = carry
            ks = pl.ds(pl.multiple_of(j * T, T), T)
            s = lax.dot_general(qb, k_ref[0, 0, ks, :], _NT, preferred_element_type=F32) * scale
            s = s + fqb - fk_ref[0, 0, pl.ds(j, 1), :]
            s = jnp.where(j * T + _iota2((T, T), 1) <= rows, s, NEG_INF)
            m_new = jnp.maximum(m, jnp.max(s, axis=1, keepdims=True))
            p = jnp.exp(s - m_new)
            a = jnp.exp(m - m_new)
            l = a * l + jnp.sum(p, axis=1, keepdims=True)
            acc = a * acc + jnp.dot(p.astype(BF16), v_ref[0, 0, ks, :], preferred_element_type=F32)
            return m_new, l, acc

        init = (jnp.full((T, 1), NEG_INF, F32), jnp.zeros((T, 1), F32), jnp.zeros((T, dh), F32))
        m, l, acc = lax.fori_loop(0, i + 1, step, init)
        o_ref[0, 0] = acc / l
        lse_ref[0, 0] = m + jnp.log(l)

    full = lambda w: pl.BlockSpec((1, 1, S, w), lambda b, h, i: (b, h, 0, 0))
    blk = lambda w: pl.BlockSpec((1, 1, T, w), lambda b, h, i: (b, h, i, 0))
    return pl.pallas_call(
        body, name=_unique(name), grid=(B, H, S // T),
        in_specs=[blk(dh), full(dh), full(dh), blk(1), pl.BlockSpec((1, 1, S // T, T), lambda b, h, i: (b, h, 0, 0))],
        out_specs=[blk(dh), blk(1)],
        out_shape=[jax.ShapeDtypeStruct((B, H, S, dh), F32), jax.ShapeDtypeStruct((B, H, S, 1), F32)],
        compiler_params=_params(("parallel", "parallel", "parallel")))(q, k, v, fq, fk)


def _fox_bwd(name, q, k, v, fq, fk, do, lse):
    B, H, S, dh = q.shape
    T = _pick(S, ATT_BLOCK, SUBLANES)
    nt = S // T
    scale = np.float32(dh ** -0.5)

    def body(q_ref, k_ref, v_ref, fq_ref, fk_ref, do_ref, lse_ref, dq_ref, dk_ref, dv_ref, dfk_ref, p_buf, dp_buf):
        dk_ref[...] = jnp.zeros_like(dk_ref)
        dv_ref[...] = jnp.zeros_like(dv_ref)
        dfk_ref[...] = jnp.zeros_like(dfk_ref)

        def qloop(i, _):
            qs = pl.ds(pl.multiple_of(i * T, T), T)
            qb, fqb, lse_b = q_ref[0, 0, qs, :], fq_ref[0, 0, qs, :], lse_ref[0, 0, qs, :]
            dob16 = do_ref[0, 0, qs, :].astype(BF16)
            rows = i * T + _iota2((T, T), 0)

            def sweep1(j, delta):
                ks = pl.ds(pl.multiple_of(j * T, T), T)
                s = lax.dot_general(qb, k_ref[0, 0, ks, :], _NT, preferred_element_type=F32) * scale
                s = s + fqb - fk_ref[0, 0, pl.ds(j, 1), :]
                p = jnp.where(j * T + _iota2((T, T), 1) <= rows, jnp.exp(s - lse_b), 0.0)
                dp = lax.dot_general(dob16, v_ref[0, 0, ks, :], _NT, preferred_element_type=F32)
                p_buf[j] = p
                dp_buf[j] = dp
                return delta + jnp.sum(p * dp, axis=1, keepdims=True)

            delta = lax.fori_loop(0, i + 1, sweep1, jnp.zeros((T, 1), F32))

            def sweep2(j, dq):
                ks = pl.ds(pl.multiple_of(j * T, T), T)
                p = p_buf[j]
                ds = p * (dp_buf[j] - delta)
                ds16 = ds.astype(BF16)
                dk_ref[0, 0, ks, :] += lax.dot_general(ds16, qb, _TN, preferred_element_type=F32)
                dv_ref[0, 0, ks, :] += lax.dot_general(p.astype(BF16), dob16, _TN, preferred_element_type=F32)
                dfk_ref[0, 0, pl.ds(j, 1), :] -= _csum(ds)
                return dq + jnp.dot(ds16, k_ref[0, 0, ks, :], preferred_element_type=F32)

            dq_ref[0, 0, qs, :] = lax.fori_loop(0, i + 1, sweep2, jnp.zeros((T, dh), F32)) * scale
            return 0

        lax.fori_loop(0, nt, qloop, 0)
        dk_ref[...] = dk_ref[...] * scale

    full = lambda w: pl.BlockSpec((1, 1, S, w), lambda b, h: (b, h, 0, 0))
    fks = pl.BlockSpec((1, 1, nt, T), lambda b, h: (b, h, 0, 0))
    return pl.pallas_call(
        body, name=_unique(name), grid=(B, H),
        in_specs=[full(dh), full(dh), full(dh), full(1), fks, full(dh), full(1)],
        out_specs=[full(dh), full(dh), full(dh), fks],
        out_shape=[jax.ShapeDtypeStruct((B, H, S, dh), F32)] * 3 + [jax.ShapeDtypeStruct((B, H, nt, T), F32)],
        scratch_shapes=[pltpu.VMEM((nt, T, T), F32), pltpu.VMEM((nt, T, T), F32)],
        compiler_params=_params(("parallel", "parallel")))(q, k, v, fq, fk, do, lse)


def _sb_terms(z):
    t = jnp.exp(-jnp.abs(z))
    lp = jnp.log(1.0 + t)
    inv = 1.0 / (1.0 + t)
    pos = z >= 0.0
    return jnp.minimum(z, 0.0) - lp, jnp.minimum(-z, 0.0) - lp, jnp.where(pos, 1.0, t) * inv, jnp.where(pos, t, 1.0) * inv


def _sb_fwd(name, q, k, v):
    B, H, S, dh = q.shape
    T = _pick(S, ATT_BLOCK, SUBLANES)
    scale = np.float32(dh ** -0.5)

    def body(q_ref, k_ref, v_ref, o_ref, lt_ref):
        i = pl.program_id(2)
        qb = q_ref[0, 0]
        rows = i * T + _iota2((T, T), 0)
        after = (_iota2((T, T), 0) > _iota2((T, T), 1)).astype(BF16)

        def step(jj, carry):
            run, acc = carry
            j = i - jj
            ks = pl.ds(pl.multiple_of(j * T, T), T)
            z = lax.dot_general(qb, k_ref[0, 0, ks, :], _NT, preferred_element_type=F32) * scale
            mask = j * T + _iota2((T, T), 1) < rows
            lb, l1, _, _ = _sb_terms(z)
            l1 = jnp.where(mask, l1, 0.0)
            rest = _split_dot(l1, after) + run
            a = jnp.where(mask, jnp.exp(lb + rest), 0.0)
            acc = acc + jnp.dot(a.astype(BF16), v_ref[0, 0, ks, :], preferred_element_type=F32)
            return run + jnp.sum(l1, axis=1, keepdims=True), acc

        run, acc = lax.fori_loop(0, i + 1, step, (jnp.zeros((T, 1), F32), jnp.zeros((T, dh), F32)))
        o_ref[0, 0] = acc
        lt_ref[0, 0] = run

    full = lambda w: pl.BlockSpec((1, 1, S, w), lambda b, h, i: (b, h, 0, 0))
    blk = lambda w: pl.BlockSpec((1, 1, T, w), lambda b, h, i: (b, h, i, 0))
    return pl.pallas_call(
        body, name=_unique(name), grid=(B, H, S // T),
        in_specs=[blk(dh), full(dh), full(dh)], out_specs=[blk(dh), blk(1)],
        out_shape=[jax.ShapeDtypeStruct((B, H, S, dh), F32), jax.ShapeDtypeStruct((B, H, S, 1), F32)],
        compiler_params=_params(("parallel", "parallel", "parallel")))(q, k, v)


def _sb_bwd(name, q, k, v, do, lt):
    B, H, S, dh = q.shape
    T = _pick(S, ATT_BLOCK, SUBLANES)
    nt = S // T
    scale = np.float32(dh ** -0.5)

    def body(q_ref, k_ref, v_ref, do_ref, lt_ref, dq_ref, dk_ref, dv_ref):
        dk_ref[...] = jnp.zeros_like(dk_ref)
        dv_ref[...] = jnp.zeros_like(dv_ref)
        upto = (_iota2((T, T), 0) <= _iota2((T, T), 1)).astype(BF16)
        before = (_iota2((T, T), 0) < _iota2((T, T), 1)).astype(BF16)

        def qloop(i, _):
            qs = pl.ds(pl.multiple_of(i * T, T), T)
            qb, dob16, lt_b = q_ref[0, 0, qs, :], do_ref[0, 0, qs, :].astype(BF16), lt_ref[0, 0, qs, :]
            rows = i * T + _iota2((T, T), 0)

            def kloop(j, carry):
                pl1, pe, dq = carry
                ks = pl.ds(pl.multiple_of(j * T, T), T)
                kb, vb = k_ref[0, 0, ks, :], v_ref[0, 0, ks, :]
                z = lax.dot_general(qb, kb, _NT, preferred_element_type=F32) * scale
                mask = j * T + _iota2((T, T), 1) < rows
                lb, l1, beta, omb = _sb_terms(z)
                l1 = jnp.where(mask, l1, 0.0)
                rest = lt_b - (_split_dot(l1, upto) + pl1)
                a = jnp.where(mask, jnp.exp(lb + rest), 0.0)
                e = lax.dot_general(dob16, vb, _NT, preferred_element_type=F32) * a
                pre = _split_dot(e, before) + pe
                dz16 = jnp.where(mask, e * omb - pre * beta, 0.0).astype(BF16)
                dk_ref[0, 0, ks, :] += lax.dot_general(dz16, qb, _TN, preferred_element_type=F32)
                dv_ref[0, 0, ks, :] += lax.dot_general(a.astype(BF16), dob16, _TN, preferred_element_type=F32)
                return (pl1 + jnp.sum(l1, axis=1, keepdims=True), pe + jnp.sum(e, axis=1, keepdims=True),
                        dq + jnp.dot(dz16, kb, preferred_element_type=F32))

            init = (jnp.zeros((T, 1), F32), jnp.zeros((T, 1), F32), jnp.zeros((T, dh), F32))
            dq_ref[0, 0, qs, :] = lax.fori_loop(0, i + 1, kloop, init)[2] * scale
            return 0

        lax.fori_loop(0, nt, qloop, 0)
        dk_ref[...] = dk_ref[...] * scale

    full = lambda w: pl.BlockSpec((1, 1, S, w), lambda b, h: (b, h, 0, 0))
    return pl.pallas_call(
        body, name=_unique(name), grid=(B, H),
        in_specs=[full(dh), full(dh), full(dh), full(dh), full(1)], out_specs=[full(dh)] * 3,
        out_shape=[jax.ShapeDtypeStruct((B, H, S, dh), F32)] * 3,
        compiler_params=_params(("parallel", "parallel")))(q, k, v, do, lt)


def _adamw(name, w, g, m, v):
    shape = w.shape
    cols = shape[-1] if (w.ndim >= 2 and shape[-1] % LANES == 0) else 0
    n = w.size
    if cols:
        prep = lambda t: t.reshape(1, n // cols, cols)
    else:
        cols = LANES
        pad = (-n) % (SUBLANES * LANES)
        prep = lambda t: jnp.pad(t.reshape(-1), (0, pad), constant_values=1.0).reshape(1, (n + pad) // cols, cols)

    def fn(r, bv, cv):
        w_, g_, m_, v_ = r
        m2 = ADAM_B1 * m_ + (1.0 - ADAM_B1) * g_
        v2 = ADAM_B2 * v_ + (1.0 - ADAM_B2) * (g_ * g_)
        m_hat = m2 / (1.0 - ADAM_B1 ** ADAM_STEP)
        v_hat = v2 / (1.0 - ADAM_B2 ** ADAM_STEP)
        return [-ADAM_LR * (m_hat / (jnp.sqrt(v_hat) + ADAM_EPS) + ADAM_WD * w_), m2, v2], [], []
    outs = _rowwise(name, fn, [prep(w), prep(g), prep(m), prep(v)], out_rows=[(cols, F32)] * 3, tm=512)
    return [o.reshape(-1)[:n].reshape(shape) for o in outs]


def _sum8(name, parts):
    def fn(r, bv, cv):
        s = r[0]
        for t in r[1:]:
            s = s + t
        return [s], [], []
    rows = [parts[i][None] for i in range(parts.shape[0])]
    return _rowwise(name, fn, rows, out_rows=[(parts.shape[2], F32)])[0][0]


def _pack(arrs, cols, dtype, row_mult):
    flat = jnp.concatenate([a.reshape(-1).astype(dtype) for a in arrs])
    pad = (-flat.size) % (cols * row_mult)
    return jnp.pad(flat, (0, pad)).reshape(-1, cols)


def _unpack(flat, shapes):
    out, off = [], 0
    for s in shapes:
        n = int(np.prod(s))
        out.append(flat[off:off + n].reshape(s))
        off += n
    return out


def _heads(t, H):
    B, S, W = t.shape
    return t.reshape(B, S, H, W // H).transpose(0, 2, 1, 3)


def _unheads(t):
    B, H, S, dh = t.shape
    return t.transpose(0, 2, 1, 3).reshape(B, S, H * dh)


def kernel(*args):
    _names_used.clear()
    p = dict(zip(ARGS, args))
    x, target = p['x'], p['loss_target']
    B, S, D = x.shape
    T = B * S
    depth = p['ln1_g'].shape[0]
    H = D // HEAD_DIM
    alpha = np.float32((2.0 * depth) ** 0.25)
    cx, cy, cc = _mesh_pos()
    my_q = 2 * cx + cy
    axes = ("x", "y", "c")
    two = lambda t: t.reshape(T, t.shape[-1])
    three = lambda t: t.reshape(B, S, t.shape[-1])

    small_in = [p['c']] + [p[n] for n in SMALL_SPLIT]
    g1 = _all_gather8("ag_small", _pack(small_in, LANES, F32, SUBLANES))
    g1 = g1.reshape(N_DEV, -1)
    c_all = g1[:, :B * D].reshape(N_DEV * B, D)
    per_chip = [_unpack(g1[2 * q], [a.shape for a in small_in])[1:] for q in range(N_CHIPS)]
    small = {n: jnp.concatenate([per_chip[q][i] for q in range(N_CHIPS)], axis=-1) for i, n in enumerate(SMALL_SPLIT)}
    for n in SMALL_REPL:
        small[n] = p[n]

    n_seq = N_DEV * B
    seq_pad = -(-n_seq // LANES) * LANES
    c_act = _rowwise("c_act", lambda r, bv, cv: ([_silu(r[0])], [], []),
                     [jnp.pad(c_all, ((0, seq_pad - n_seq), (0, 0)))[None]], out_rows=[(D, F32)])[0][0]
    mod_cols = p['mod_w'].shape[2]
    mod_part = jnp.stack([_mm(f"mod_fwd{l}", c_act, p['mod_w'][l])[:n_seq] for l in range(depth)])
    half_layers = depth // 2
    mod_half = lax.dynamic_slice_in_dim(mod_part, cc * half_layers, half_layers, axis=0)
    gm_ = _all_gather8("ag_mod", mod_half.reshape(half_layers * n_seq, mod_cols))
    mod_all = gm_.reshape(N_CHIPS, 2, half_layers, n_seq, mod_cols).transpose(1, 2, 3, 0, 4).reshape(depth, n_seq, 6 * D)
    mod_mine = lax.dynamic_slice_in_dim(mod_all, (2 * my_q + cc) * B, B, axis=1)
    mod = _rowwise("mod_bias", lambda r, bv, cv: ([r[0] + bv[0]], [], []), [mod_mine], [p['mod_b'][:, None, :]],
                   out_rows=[(6 * D, F32)])[0]
    mods = [[mod[l, :, None, i * D:(i + 1) * D] for i in range(6)] for l in range(depth)]

    big_names = list(BIG)
    shard_shapes = [p[n].shape for n in big_names]
    n_sh = sum(int(np.prod(s)) for s in shard_shapes)
    half_len = -(-n_sh // (2 * PACK_COLS * 16)) * (PACK_COLS * 16)
    wflat = jnp.concatenate([p[n].reshape(-1).astype(BF16) for n in big_names])
    wflat = jnp.pad(wflat, (0, 2 * half_len - n_sh)).reshape(2, half_len // PACK_COLS, PACK_COLS)
    wg = _all_gather8("ag_weights", lax.dynamic_index_in_dim(wflat, cc, 0, keepdims=False))
    wg = wg.reshape(N_CHIPS, 2 * half_len)
    W = {}
    off = 0
    for n, s in zip(big_names, shard_shapes):
        sz = int(np.prod(s))
        seg = wg[:, off:off + sz].reshape((N_CHIPS,) + s)
        W[n] = jnp.concatenate([seg[q] for q in range(N_CHIPS)], axis=BIG[n])
        off += sz

    def vec(n, j):
        return small[n][j][None, :]

    def attn_proj(h1, w_in, gate_cols):
        wp = jnp.pad(w_in, ((0, 0), (0, gate_cols))) if gate_cols else w_in
        proj = three(_mm("att_proj", two(h1), wp))
        q, k, v = [_heads(proj[..., i * D:(i + 1) * D].astype(BF16), H) for i in range(3)]
        return wp, proj, q, k, v

    def gm_fwd(j, h1):
        zin = three(_mm("gm_in", two(h1), W['gm_w_in'][j]))
        u, vn = _gm_act("gm_act", zin, vec('gm_b_in', j), vec('gm_ln_g', j), vec('gm_ln_b', j))
        b_sT = small['gm_b_s'][j].T
        yg = _gm_spatial("gm_spatial", u, vn, small['gm_w_s'][j], b_sT)
        return three(_mm("gm_out", two(yg), W['gm_w_out'][j])), (zin, u, vn, b_sT, yg)

    def gm_bwd(j, h1, dy1, cache):
        zin, u, vn, b_sT, yg = cache
        g = {'gm_w_out': _mm("gm_dwout", two(yg), two(dy1), ta=True)}
        dyg = three(_mm("gm_dyg", two(dy1), W['gm_w_out'][j], tb=True))
        du, dvn, dws, dbsT = _gm_spatial_bwd("gm_spatial_bwd", dyg, u, vn, small['gm_w_s'][j], b_sT)
        dzin, g['gm_b_in'], g['gm_ln_g'], g['gm_ln_b'] = _gm_act_bwd("gm_act_bwd", zin, du, dvn, vec('gm_b_in', j), vec('gm_ln_g', j))
        g['gm_w_s'], g['gm_b_s'] = dws, dbsT.T
        g['gm_w_in'] = _mm("gm_dwin", two(h1), two(dzin), ta=True)
        return _mm("gm_dh", two(dzin), W['gm_w_in'][j], tb=True), g

    def fox_fwd(j, h1):
        wp, proj, q, k, v = attn_proj(h1, W['fox_w_in'][j], 3 * D + FOX_GATE_COLS - W['fox_w_in'].shape[2])
        fl = proj[..., 3 * D:]
        bf = jnp.pad(small['fox_b_f'][j][None, :], ((0, 0), (0, FOX_GATE_COLS - H)))
        Fh = _fox_gate_cumsum("fox_gate", fl, bf)[..., :H].transpose(0, 2, 1)
        fq, fk = Fh[..., None], Fh.reshape(B, H, S // _pick(S, ATT_BLOCK, SUBLANES), -1)
        o, lse = _fox_fwd("fox_fwd", q, k, v, fq, fk)
        o2 = _unheads(o)
        return three(_mm("fox_out", two(o2), W['fox_w_out'][j])), (wp, q, k, v, fl, bf, fq, fk, lse, o2)

    def fox_bwd(j, h1, dy1, cache):
        wp, q, k, v, fl, bf, fq, fk, lse, o2 = cache
        g = {'fox_w_out': _mm("fox_dwout", two(o2), two(dy1), ta=True)}
        do = _heads(three(_mm("fox_do", two(dy1), W['fox_w_out'][j], tb=True)), H)
        dq, dk, dv, dfk = _fox_bwd("fox_bwd", q, k, v, fq, fk, do, lse)
        dF = jnp.pad(dfk.reshape(B, H, S).transpose(0, 2, 1), ((0, 0), (0, 0), (0, FOX_GATE_COLS - H)))
        dfl, dbf = _fox_gate_bwd("fox_gate_bwd", dF, fl, bf, H)
        dproj = jnp.concatenate([_unheads(dq).astype(BF16), _unheads(dk).astype(BF16), _unheads(dv).astype(BF16),
                                 dfl.astype(BF16)], axis=-1)
        g['fox_w_in'] = _mm("fox_dwin", two(h1), two(dproj), ta=True)[:, :W['fox_w_in'].shape[2]]
        g['fox_b_f'] = dbf[0, :H]
        return _mm("fox_dh", two(dproj), wp, tb=True), g

    def sb_fwd(j, h1):
        wp, proj, q, k, v = attn_proj(h1, W['sb_w_in'][j], 0)
        o, lt = _sb_fwd("sb_fwd", q, k, v)
        o2 = _unheads(o)
        return three(_mm("sb_out", two(o2), W['sb_w_out'][j])), (q, k, v, lt, o2)

    def sb_bwd(j, h1, dy1, cache):
        q, k, v, lt, o2 = cache
        g = {'sb_w_out': _mm("sb_dwout", two(o2), two(dy1), ta=True)}
        do = _heads(three(_mm("sb_do", two(dy1), W['sb_w_out'][j], tb=True)), H)
        dq, dk, dv = _sb_bwd("sb_bwd", q, k, v, do, lt)
        dproj = jnp.concatenate([_unheads(dq).astype(BF16), _unheads(dk).astype(BF16), _unheads(dv).astype(BF16)], axis=-1)
        g['sb_w_in'] = _mm("sb_dwin", two(h1), two(dproj), ta=True)
        return _mm("sb_dh", two(dproj), W['sb_w_in'][j], tb=True), g

    def cv_fwd(j, h1):
        pw = three(_mm("cv_in", two(h1), W['cv_w_in'][j]))
        ygl = _cv_glu("cv_glu", pw, vec('cv_b_in', j))
        dw = jnp.pad(small['cv_dw'][j], ((0, CONV_HALO - CONV_WIDTH), (0, 0)))
        yc = _dwconv("cv_dwconv", ygl, dw, vec('cv_dw_b', j))
        ys = _cv_ln_act("cv_ln_act", yc, vec('cv_ln_g', j), vec('cv_ln_b', j))
        return three(_mm("cv_out", two(ys), W['cv_w_out'][j])), (pw, ygl, dw, yc, ys)

    def cv_bwd(j, h1, dy1, cache):
        pw, ygl, dw, yc, ys = cache
        g = {'cv_w_out': _mm("cv_dwout", two(ys), two(dy1), ta=True)}
        dys = three(_mm("cv_dys", two(dy1), W['cv_w_out'][j], tb=True))
        dyc, g['cv_ln_g'], g['cv_ln_b'], g['cv_dw_b'] = _cv_ln_act_bwd("cv_ln_act_bwd", yc, dys, vec('cv_ln_g', j), vec('cv_ln_b', j))
        dygl, ddw = _dwconv_bwd("cv_dwconv_bwd", dyc, ygl, dw)
        g['cv_dw'] = ddw[:CONV_WIDTH]
        dpw, g['cv_b_in'] = _cv_glu_bwd("cv_glu_bwd", pw, dygl, vec('cv_b_in', j))
        g['cv_w_in'] = _mm("cv_dwin", two(h1), two(dpw), ta=True)
        return _mm("cv_dh", two(dpw), W['cv_w_in'][j], tb=True), g

    mixers = [(gm_fwd, gm_bwd), (fox_fwd, fox_bwd), (sb_fwd, sb_bwd), (cv_fwd, cv_bwd)]
    n_mix = len(mixers)

    saved = []
    for l in range(depth):
        m, j = l % n_mix, l // n_mix
        sh1, sc1, g1_, sh2, sc2, g2_ = mods[l]
        ybias = vec('cv_b_out', j) if m == 3 else None
        h1 = _modulate("mod1", x, sc1, sh1)
        y1, cache = mixers[m][0](j, h1)
        xm = _resid_ln("resid_ln1", alpha, x, y1, g1_, small['ln1_g'][l][None], small['ln1_b'][l][None], ybias)
        h2 = _modulate("mod2", xm, sc2, sh2)
        z = three(_mm("ffn_in", two(h2), W['ffn_w_in'][l]))
        a = _swiglu_act("ffn_act", z)
        y2 = three(_mm("ffn_out", two(a), W['ffn_w_out'][l]))
        xo = _resid_ln("resid_ln2", alpha, xm, y2, g2_, small['ln2_g'][l][None], small['ln2_b'][l][None])
        saved.append((x, h1, y1, cache, xm, h2, z, a, y2, ybias))
        x = xo

    dx, sq = _loss_head("loss_head", x, target)
    loss = lax.psum(jnp.sum(sq) * np.float32(0.5 / D), axes)

    grads = {n: [None] * p[n].shape[0] for n in WEIGHTS}
    dmod = [None] * depth
    for l in reversed(range(depth)):
        m, j = l % n_mix, l // n_mix
        sh1, sc1, g1_, sh2, sc2, g2_ = mods[l]
        x_in, h1, y1, cache, xm, h2, z, a, y2, ybias = saved[l]
        dr2, dy2, dg2, grads['ln2_g'][l], grads['ln2_b'][l], _ = _resid_ln_bwd("resid_ln2_bwd", alpha, dx, xm, y2, g2_, small['ln2_g'][l][None])
        grads['ffn_w_out'][l] = _mm("ffn_dwout", two(a), two(dy2), ta=True)
        da = three(_mm("ffn_da", two(dy2), W['ffn_w_out'][l], tb=True))
        dz = _swiglu_act_bwd("ffn_act_bwd", da, z)
        grads['ffn_w_in'][l] = _mm("ffn_dwin", two(h2), two(dz), ta=True)
        dh2 = three(_mm("ffn_dh", two(dz), W['ffn_w_in'][l], tb=True))
        dxm, dsc2, dsh2 = _modulate_bwd("mod2_bwd", alpha, dh2, dr2, xm, sc2)
        dr1, dy1, dg1, grads['ln1_g'][l], grads['ln1_b'][l], dyb = _resid_ln_bwd("resid_ln1_bwd", alpha, dxm, x_in, y1, g1_, small['ln1_g'][l][None], ybias)
        dh1, mg = mixers[m][1](j, h1, dy1, cache)
        if m == 3:
            mg['cv_b_out'] = dyb
        for n, gval in mg.items():
            grads[n][j] = gval
        dx, dsc1, dsh1 = _modulate_bwd("mod1_bwd", alpha, three(dh1), dr1, x_in, sc1)
        dmod[l] = jnp.concatenate([dsh1, dsc1, dg1, dsh2, dsc2, dg2], axis=-1)[:, 0, :]
    grad_x = dx
    dmod = jnp.stack(dmod)
    grads['mod_b'] = [jnp.sum(dmod[l], axis=0) for l in range(depth)]
    full_shape = {n: tuple(t.shape) for n, t in small.items()}

    small_names = SMALL_REPL + SMALL_SPLIT
    small_parts = [jnp.stack([gv.reshape(full_shape[n][1:]) for gv in grads[n]]) for n in small_names]
    pack_a = _pack([dmod], LANES, F32, SUBLANES)
    pack_b = _pack(small_parts, LANES, F32, SUBLANES)
    g2 = _all_gather8("ag_grads_small", jnp.concatenate([pack_a, pack_b], axis=0))
    rows_a = pack_a.shape[0]
    dmod_all = g2[:, :rows_a].reshape(N_DEV, -1)[:, :dmod.size].reshape(N_DEV, depth, B, 6 * D)
    dmod_all = dmod_all.transpose(1, 0, 2, 3).reshape(depth, n_seq, 6 * D)
    small_sum = _sum8("sum_grads_small", g2[:, rows_a:]).reshape(-1)
    g_small = dict(zip(small_names, _unpack(small_sum, [full_shape[n] for n in small_names])))
    for n in SMALL_SPLIT:
        w = p[n].shape[-1]
        g_small[n] = lax.dynamic_slice_in_dim(g_small[n], my_q * w, w, axis=g_small[n].ndim - 1)

    dm_cols = lax.dynamic_slice_in_dim(dmod_all, my_q * mod_cols, mod_cols, axis=2)
    dm_cols = jnp.pad(dm_cols, ((0, 0), (0, seq_pad - n_seq), (0, 0)))
    g_mod_w = jnp.stack([_mm(f"mod_dw{l}", c_act, dm_cols[l], ta=True) for l in range(depth)])

    gparts = []
    for n, s in zip(big_names, shard_shapes):
        gfull = jnp.stack(grads[n])
        gparts.append(jnp.stack(jnp.split(gfull, N_CHIPS, axis=BIG[n])).reshape(N_CHIPS, -1))
    gflat = jnp.pad(jnp.concatenate(gparts, axis=1), ((0, 0), (0, 2 * half_len - n_sh)))
    gflat = gflat.reshape(N_CHIPS, 2, half_len // PACK_COLS, PACK_COLS)
    keep = lax.dynamic_index_in_dim(gflat, cc, 1, keepdims=False)
    give = lax.dynamic_index_in_dim(gflat, 1 - cc, 1, keepdims=False)
    got = _sibling_exchange("rs_sibling", give)
    rows4 = lambda t: t.reshape(1, -1, PACK_COLS)
    chip_sum = _rowwise("rs_add_sibling", lambda r, bv, cv: ([r[0] + r[1]], [], []), [rows4(keep), rows4(got)],
                        out_rows=[(PACK_COLS, BF16)], tm=512)[0].reshape(keep.shape)
    from_chips = _chip_all_to_all("rs_chips", chip_sum)
    half_sum = _rowwise("rs_add_chips", lambda r, bv, cv: ([((r[0] + r[1]) + r[2]) + r[3]], [], []),
                        [from_chips[q][None] for q in range(N_CHIPS)], out_rows=[(PACK_COLS, F32)], tm=512)[0][0]
    other = _sibling_exchange("rs_share", half_sum)
    lo = jnp.where(cc == 0, half_sum, other)
    hi = jnp.where(cc == 0, other, half_sum)
    g_big = dict(zip(big_names, _unpack(jnp.concatenate([lo, hi], axis=0).reshape(-1), shard_shapes)))

    g_out = {**g_small, **g_big, 'mod_w': g_mod_w}
    upd = {n: _adamw("adamw_" + n, p[n], g_out[n], p['m_' + n], p['v_' + n]) for n in WEIGHTS}
    return (loss, grad_x, *[g_out[n] for n in WEIGHTS], *[upd[n][0] for n in WEIGHTS],
            *[upd[n][1] for n in WEIGHTS], *[upd[n][2] for n in WEIGHTS])
```

```python
import math

import jax
import jax.numpy as jnp
import numpy as np
from jax import lax
from jax.experimental import pallas as pl
from jax.experimental.pallas import tpu as pltpu

F32, BF16 = jnp.float32, jnp.bfloat16

HEAD_DIM = 64
CONV_WIDTH = 31
LN_EPS = 1e-5
NEG_INF = -1e30
ADAM_LR, ADAM_B1, ADAM_B2, ADAM_EPS, ADAM_WD, ADAM_STEP = 0.001, 0.9, 0.999, 1e-08, 0.01, 10

LANES = 128
SUBLANES = 8
VMEM_LIMIT_BYTES = 56 * 1024 * 1024
N_CHIPS = 4
N_DEV = 8

WEIGHTS = ['mod_w', 'mod_b', 'ln1_g', 'ln1_b', 'ln2_g', 'ln2_b', 'ffn_w_in', 'ffn_w_out', 'gm_w_in', 'gm_b_in',
           'gm_ln_g', 'gm_ln_b', 'gm_w_s', 'gm_b_s', 'gm_w_out', 'fox_w_in', 'fox_b_f', 'fox_w_out', 'sb_w_in',
           'sb_w_out', 'cv_w_in', 'cv_b_in', 'cv_dw', 'cv_dw_b', 'cv_ln_g', 'cv_ln_b', 'cv_w_out', 'cv_b_out']
ARGS = ['x', 'c'] + WEIGHTS + ['loss_target'] + ['m_' + n for n in WEIGHTS] + ['v_' + n for n in WEIGHTS]
BIG = {'ffn_w_in': 2, 'ffn_w_out': 1, 'gm_w_in': 2, 'gm_w_out': 1, 'fox_w_in': 2, 'fox_w_out': 1,
       'sb_w_in': 2, 'sb_w_out': 1, 'cv_w_in': 2, 'cv_w_out': 1}
SMALL_SPLIT = ['cv_b_in', 'cv_dw', 'cv_dw_b', 'cv_ln_g', 'cv_ln_b', 'cv_b_out']
SMALL_REPL = ['mod_b', 'ln1_g', 'ln1_b', 'ln2_g', 'ln2_b', 'gm_b_in', 'gm_ln_g', 'gm_ln_b', 'gm_w_s', 'gm_b_s', 'fox_b_f']
PACK_COLS = 1024


_names_used = {}


def _unique(name):
    k = _names_used.get(name, 0)
    _names_used[name] = k + 1
    return name if k == 0 else f"{name}_{k}"


def _params(sem):
    return pltpu.CompilerParams(dimension_semantics=sem, vmem_limit_bytes=VMEM_LIMIT_BYTES)


def _pick(dim, pref, mult=LANES):
    if dim <= pref:
        return dim
    best = 0
    for t in range(mult, pref + 1, mult):
        if dim % t == 0:
            best = t
    assert best, (dim, pref)
    return best


def _mesh_pos():
    return lax.axis_index("x"), lax.axis_index("y"), lax.axis_index("c")


AG_COPIES = 7
A2A_COPIES = 3


def _comm_call(name, body, blks, out_shapes, n_sems):
    n = len(blks)
    hbm = pl.BlockSpec(memory_space=pl.ANY)
    return pl.pallas_call(
        body, name=_unique(name), out_shape=out_shapes, in_specs=[hbm] * n, out_specs=[hbm] * n,
        scratch_shapes=[pltpu.SemaphoreType.DMA((n_sems * n,)), pltpu.SemaphoreType.DMA((n_sems * n,)),
                        pltpu.SemaphoreType.DMA((n,))],
    )(*blks)


def _all_gather8(name, blks):
    n = len(blks)

    def body(*refs):
        x_refs, out_refs, (send_sems, recv_sems, local_sems) = refs[:n], refs[n:2 * n], refs[2 * n:]
        x, y, c = _mesh_pos()
        me, sibling = (x, y, c), (x, y, 1 - c)
        chips = [(1 - x, y), (x, 1 - y), (1 - x, 1 - y)]

        def copy(a, k, block, to, from_input=False):
            px, py, pc = block
            slot = out_refs[a].at[4 * px + 2 * py + pc]
            return pltpu.make_async_remote_copy(
                src_ref=x_refs[a] if from_input else slot, dst_ref=slot,
                send_sem=send_sems.at[AG_COPIES * a + k], recv_sem=recv_sems.at[AG_COPIES * a + k],
                device_id=to, device_id_type=pl.DeviceIdType.MESH)

        local, sent = [], []
        for a in range(n):
            local.append(pltpu.make_async_copy(x_refs[a], out_refs[a].at[4 * x + 2 * y + c], local_sems.at[a]))
            local[-1].start()
            first = [copy(a, 0, me, sibling, True)] + [copy(a, 1 + j, me, (*chip, c), True) for j, chip in enumerate(chips)]
            for cp in first:
                cp.start()
            sent += first
        for a in range(n):
            for j, chip in enumerate(chips):
                copy(a, 1 + j, (*chip, c), me).wait_recv()
                sent.append(copy(a, 4 + j, (*chip, c), sibling))
                sent[-1].start()
        for a in range(n):
            copy(a, 0, sibling, me).wait_recv()
            for j, chip in enumerate(chips):
                copy(a, 4 + j, (*chip, 1 - c), me).wait_recv()
        for cp in sent:
            cp.wait_send()
        for cp in local:
            cp.wait()

    return _comm_call(name, body, blks, [jax.ShapeDtypeStruct((N_DEV,) + b.shape, b.dtype) for b in blks], AG_COPIES)


def _sibling_exchange(name, blks):
    n = len(blks)

    def body(*refs):
        x_refs, out_refs, (send_sems, recv_sems, _) = refs[:n], refs[n:2 * n], refs[2 * n:]
        x, y, c = _mesh_pos()
        cps = [pltpu.make_async_remote_copy(src_ref=x_refs[a], dst_ref=out_refs[a], send_sem=send_sems.at[a],
                                            recv_sem=recv_sems.at[a], device_id=(x, y, 1 - c),
                                            device_id_type=pl.DeviceIdType.MESH) for a in range(n)]
        for cp in cps:
            cp.start()
        for cp in cps:
            cp.wait()

    return _comm_call(name, body, blks, [jax.ShapeDtypeStruct(b.shape, b.dtype) for b in blks], 1)


def _chip_all_to_all(name, blks):
    n = len(blks)

    def body(*refs):
        x_refs, out_refs, (send_sems, recv_sems, local_sems) = refs[:n], refs[n:2 * n], refs[2 * n:]
        x, y, c = _mesh_pos()
        chips = [(1 - x, y), (x, 1 - y), (1 - x, 1 - y)]
        my_q = 2 * x + y

        def copy(a, j, src_q, dst_q):
            px, py = chips[j]
            return pltpu.make_async_remote_copy(
                src_ref=x_refs[a].at[src_q], dst_ref=out_refs[a].at[dst_q],
                send_sem=send_sems.at[A2A_COPIES * a + j], recv_sem=recv_sems.at[A2A_COPIES * a + j],
                device_id=(px, py, c), device_id_type=pl.DeviceIdType.MESH)

        local, sent = [], []
        for a in range(n):
            local.append(pltpu.make_async_copy(x_refs[a].at[my_q], out_refs[a].at[my_q], local_sems.at[a]))
            local[-1].start()
            sent += [copy(a, j, 2 * px + py, my_q) for j, (px, py) in enumerate(chips)]
            for cp in sent[-A2A_COPIES:]:
                cp.start()
        for a in range(n):
            for j, (px, py) in enumerate(chips):
                copy(a, j, my_q, 2 * px + py).wait_recv()
        for cp in sent:
            cp.wait_send()
        for cp in local:
            cp.wait()

    return _comm_call(name, body, blks, [jax.ShapeDtypeStruct(b.shape, b.dtype) for b in blks], A2A_COPIES)


def _rowwise(name, fn, rows, bvecs=(), cvecs=(), out_rows=(), out_bsums=(), out_tsums=(), tm=256):
    B, S = rows[0].shape[:2]
    tm = _pick(S, tm, SUBLANES)
    n_r, n_b, n_c = len(rows), len(bvecs), len(cvecs)
    n_or, n_ob = len(out_rows), len(out_bsums)

    def body(*refs):
        ins, outs = refs[:n_r + n_b + n_c], refs[n_r + n_b + n_c:]
        r = [ref[0] for ref in ins[:n_r]]
        bv = [ref[0] for ref in ins[n_r:n_r + n_b]]
        cv = [ref[...] for ref in ins[n_r + n_b:]]
        o_rows, o_bsums, o_tsums = fn(r, bv, cv)
        b, i = pl.program_id(0), pl.program_id(1)
        for ref, val in zip(outs[:n_or], o_rows):
            ref[0] = val.astype(ref.dtype)
        for ref, val in zip(outs[n_or:n_or + n_ob], o_bsums):
            @pl.when(i == 0)
            def _(ref=ref, val=val):
                ref[0] = val

            @pl.when(i > 0)
            def _(ref=ref, val=val):
                ref[0] += val
        for ref, val in zip(outs[n_or + n_ob:], o_tsums):
            first = jnp.logical_and(b == 0, i == 0)

            @pl.when(first)
            def _(ref=ref, val=val):
                ref[...] = val

            @pl.when(jnp.logical_not(first))
            def _(ref=ref, val=val):
                ref[...] += val

    in_specs = [pl.BlockSpec((1, tm, a.shape[2]), lambda b, i: (b, i, 0)) for a in rows]
    in_specs += [pl.BlockSpec((1, 1, a.shape[2]), lambda b, i: (b, 0, 0)) for a in bvecs]
    in_specs += [pl.BlockSpec((1, a.shape[1]), lambda b, i: (0, 0)) for a in cvecs]
    out_shape = [jax.ShapeDtypeStruct((B, S, cdim), dt) for cdim, dt in out_rows]
    out_specs = [pl.BlockSpec((1, tm, cdim), lambda b, i: (b, i, 0)) for cdim, _ in out_rows]
    out_shape += [jax.ShapeDtypeStruct((B, 1, cdim), F32) for cdim in out_bsums]
    out_specs += [pl.BlockSpec((1, 1, cdim), lambda b, i: (b, 0, 0)) for cdim in out_bsums]
    out_shape += [jax.ShapeDtypeStruct((1, cdim), F32) for cdim in out_tsums]
    out_specs += [pl.BlockSpec((1, cdim), lambda b, i: (0, 0)) for cdim in out_tsums]
    sem = ("arbitrary", "arbitrary") if out_tsums else ("parallel", "arbitrary")
    res = pl.pallas_call(body, name=_unique(name), grid=(B, S // tm), in_specs=in_specs, out_specs=out_specs,
                         out_shape=out_shape, compiler_params=_params(sem))(*rows, *bvecs, *cvecs)
    return list(res)


def _mm(name, a, b, ta=False, tb=False, out_dtype=F32, tm=512, tn=1536, tk=1536):
    M, K = (a.shape[1], a.shape[0]) if ta else a.shape
    N = b.shape[0] if tb else b.shape[1]
    assert (b.shape[1] if tb else b.shape[0]) == K, (a.shape, b.shape, ta, tb)
    tm, tn, tk = _pick(M, tm, SUBLANES if not ta else LANES), _pick(N, tn), _pick(K, tk, LANES if (not ta or tb) else SUBLANES)
    nk = K // tk
    dims = (((0 if ta else 1,), (1 if tb else 0,)), ((), ()))

    def body(a_ref, b_ref, o_ref, acc_ref):
        k = pl.program_id(2)
        p = lax.dot_general(a_ref[...].astype(BF16), b_ref[...].astype(BF16), dims, preferred_element_type=F32)
        if nk == 1:
            o_ref[...] = p.astype(o_ref.dtype)
        else:
            @pl.when(k == 0)
            def _():
                acc_ref[...] = p

            @pl.when(k > 0)
            def _():
                acc_ref[...] += p

            @pl.when(k == nk - 1)
            def _():
                o_ref[...] = acc_ref[...].astype(o_ref.dtype)

    a_spec = pl.BlockSpec((tk, tm), lambda i, j, k: (k, i)) if ta else pl.BlockSpec((tm, tk), lambda i, j, k: (i, k))
    b_spec = pl.BlockSpec((tn, tk), lambda i, j, k: (j, k)) if tb else pl.BlockSpec((tk, tn), lambda i, j, k: (k, j))
    return pl.pallas_call(
        body, name=_unique(name), grid=(M // tm, N // tn, nk), in_specs=[a_spec, b_spec],
        out_specs=pl.BlockSpec((tm, tn), lambda i, j, k: (i, j)),
        out_shape=jax.ShapeDtypeStruct((M, N), out_dtype),
        scratch_shapes=[pltpu.VMEM((tm, tn) if nk > 1 else (SUBLANES, LANES), F32)],
        compiler_params=_params(("parallel", "parallel", "arbitrary")))(a, b)


def _silu(x):
    return x * _sigmoid(x)


def _sigmoid(x):
    return 1.0 / (1.0 + jnp.exp(-x))


def _dsilu(x):
    s = _sigmoid(x)
    return s * (1.0 + x * (1.0 - s))


def _gelu(x):
    return 0.5 * x * (1.0 + lax.erf(x * np.float32(math.sqrt(0.5))))


def _dgelu(x):
    cdf = 0.5 * (1.0 + lax.erf(x * np.float32(math.sqrt(0.5))))
    pdf = jnp.exp(-0.5 * x * x) * np.float32(1.0 / math.sqrt(2.0 * math.pi))
    return cdf + x * pdf


def _ln_stats(r):
    mu = jnp.mean(r, axis=-1, keepdims=True)
    xc = r - mu
    var = jnp.mean(xc * xc, axis=-1, keepdims=True)
    rstd = lax.rsqrt(var + LN_EPS)
    return xc * rstd, rstd


def _ln_bwd(dxhat, xhat, rstd):
    m1 = jnp.mean(dxhat, axis=-1, keepdims=True)
    m2 = jnp.mean(dxhat * xhat, axis=-1, keepdims=True)
    return rstd * (dxhat - m1 - xhat * m2)


def _csum(v):
    return jnp.sum(v, axis=0, keepdims=True)


def _split_dot(x, m01, lhs01=False, terms=2):
    acc, rem = None, x
    for _ in range(terms):
        part = rem.astype(BF16)
        rem = rem - part.astype(F32)
        d = jnp.dot(m01, part, preferred_element_type=F32) if lhs01 else jnp.dot(part, m01, preferred_element_type=F32)
        acc = d if acc is None else acc + d
    return acc


def _iota2(shape, dim):
    return lax.broadcasted_iota(jnp.int32, shape, dim)


def _modulate(name, x, sc, sh):
    D = x.shape[2]
    return _rowwise(name, lambda r, bv, cv: ([r[0] * (1.0 + bv[0]) + bv[1]], [], []),
                    [x], [sc, sh], [], out_rows=[(D, BF16)])[0]


def _resid_ln(name, alpha, x, y, g, ln_g, ln_b, ybias=None):
    D = x.shape[2]

    def fn(r, bv, cv):
        yy = r[1] if ybias is None else r[1] + cv[2]
        xhat, _ = _ln_stats(alpha * r[0] + (1.0 + bv[0]) * yy)
        return [xhat * cv[0] + cv[1]], [], []
    cvecs = [ln_g, ln_b] + ([] if ybias is None else [ybias])
    return _rowwise(name, fn, [x, y], [g], cvecs, out_rows=[(D, F32)])[0]


def _resid_ln_bwd(name, alpha, dxn, x, y, g, ln_g, ybias=None):
    D = x.shape[2]

    def fn(r, bv, cv):
        yy = r[2] if ybias is None else r[2] + cv[1]
        xhat, rstd = _ln_stats(alpha * r[1] + (1.0 + bv[0]) * yy)
        dr = _ln_bwd(r[0] * cv[0], xhat, rstd)
        dy = (1.0 + bv[0]) * dr
        return [dr, dy], [_csum(dr * yy)], [_csum(r[0] * xhat), _csum(r[0]), _csum(dy)]
    cvecs = [ln_g] + ([] if ybias is None else [ybias])
    return _rowwise(name, fn, [dxn, x, y], [g], cvecs, out_rows=[(D, F32), (D, BF16)], out_bsums=[D], out_tsums=[D, D, D])


def _modulate_bwd(name, alpha, dh, dr, x, sc):
    D = x.shape[2]

    def fn(r, bv, cv):
        return [alpha * r[1] + r[0] * (1.0 + bv[0])], [_csum(r[0] * r[2]), _csum(r[0])], []
    return _rowwise(name, fn, [dh, dr, x], [sc], [], out_rows=[(D, F32)], out_bsums=[D, D])


def _loss_head(name, y, target):
    D = y.shape[2]

    def fn(r, bv, cv):
        e = r[0] - r[1]
        return [e * np.float32(1.0 / D)], [_csum(e * e)], []
    return _rowwise(name, fn, [y, target], [], [], out_rows=[(D, F32)], out_bsums=[D])


def _swiglu_act(name, z):
    Hd = z.shape[2] // 2
    return _rowwise(name, lambda r, bv, cv: ([_silu(r[0][:, :Hd]) * r[0][:, Hd:]], [], []), [z], out_rows=[(Hd, BF16)])[0]


def _swiglu_act_bwd(name, da, z):
    Hd = z.shape[2] // 2

    def fn(r, bv, cv):
        gg, u = r[1][:, :Hd], r[1][:, Hd:]
        return [jnp.concatenate([r[0] * u * _dsilu(gg), r[0] * _silu(gg)], axis=1)], [], []
    return _rowwise(name, fn, [da, z], out_rows=[(2 * Hd, BF16)])[0]


def _gm_act(name, zin, b_in, ln_g, ln_b):
    W = zin.shape[2] // 2

    def fn(r, bv, cv):
        z = _gelu(r[0] + cv[0])
        vhat, _ = _ln_stats(z[:, W:])
        return [z[:, :W], vhat * cv[1] + cv[2]], [], []
    return _rowwise(name, fn, [zin], [], [b_in, ln_g, ln_b], out_rows=[(W, F32), (W, BF16)])


def _gm_act_bwd(name, zin, du, dvn, b_in, ln_g):
    W = zin.shape[2] // 2

    def fn(r, bv, cv):
        zz = r[0] + cv[0]
        z = _gelu(zz)
        vhat, rstd = _ln_stats(z[:, W:])
        dv = _ln_bwd(r[2] * cv[1], vhat, rstd)
        dzin = jnp.concatenate([r[1], dv], axis=1) * _dgelu(zz)
        return [dzin], [], [_csum(dzin), _csum(r[2] * vhat), _csum(r[2])]
    return _rowwise(name, fn, [zin, du, dvn], [], [b_in, ln_g], out_rows=[(2 * W, BF16)], out_tsums=[2 * W, W, W])


def _gm_causal_w(ws_ref, g):
    T = ws_ref.shape[1]
    return jnp.where(_iota2((T, T), 1) <= _iota2((T, T), 0), ws_ref[g], 0.0).astype(BF16)


def _gm_spatial(name, u, vn, w_s, b_sT):
    B, S, W = u.shape
    G, T = w_s.shape[0], w_s.shape[1]
    assert W == G * T, "a head group is as wide as a chunk is long"

    def body(u_ref, vn_ref, ws_ref, bs_ref, y_ref):
        for g in range(G):
            cs = slice(g * T, (g + 1) * T)
            sv = jnp.dot(_gm_causal_w(ws_ref, g), vn_ref[0, :, cs], preferred_element_type=F32) + bs_ref[:, g:g + 1]
            y_ref[0, :, cs] = (u_ref[0, :, cs] * sv).astype(y_ref.dtype)

    row = pl.BlockSpec((1, T, W), lambda b, i: (b, i, 0))
    return pl.pallas_call(
        body, name=_unique(name), grid=(B, S // T),
        in_specs=[row, row, pl.BlockSpec((G, T, T), lambda b, i: (0, 0, 0)), pl.BlockSpec((T, G), lambda b, i: (0, 0))],
        out_specs=row, out_shape=jax.ShapeDtypeStruct((B, S, W), BF16),
        compiler_params=_params(("parallel", "parallel")))(u, vn, w_s, b_sT)


def _gm_spatial_bwd(name, dyg, u, vn, w_s, b_sT):
    B, S, W = u.shape
    G, T = w_s.shape[0], w_s.shape[1]
    assert W == G * T, "a head group is as wide as a chunk is long"

    def body(dy_ref, u_ref, vn_ref, ws_ref, bs_ref, du_ref, dvn_ref, dws_ref, dbs_ref):
        first = jnp.logical_and(pl.program_id(0) == 0, pl.program_id(1) == 0)

        @pl.when(first)
        def _():
            dws_ref[...] = jnp.zeros_like(dws_ref)
            dbs_ref[...] = jnp.zeros_like(dbs_ref)

        tril = _iota2((T, T), 1) <= _iota2((T, T), 0)
        for g in range(G):
            cs = slice(g * T, (g + 1) * T)
            wm = _gm_causal_w(ws_ref, g)
            vng = vn_ref[0, :, cs]
            sv = jnp.dot(wm, vng, preferred_element_type=F32) + bs_ref[:, g:g + 1]
            dy = dy_ref[0, :, cs]
            du_ref[0, :, cs] = dy * sv
            dsv = dy * u_ref[0, :, cs]
            dsv16 = dsv.astype(BF16)
            dvn_ref[0, :, cs] = lax.dot_general(wm, dsv16, (((0,), (0,)), ((), ())), preferred_element_type=F32)
            dw = lax.dot_general(dsv16, vng, (((1,), (1,)), ((), ())), preferred_element_type=F32)
            dws_ref[g] += jnp.where(tril, dw, 0.0)
            dbs_ref[:, g:g + 1] += jnp.sum(dsv, axis=1, keepdims=True)

    row = pl.BlockSpec((1, T, W), lambda b, i: (b, i, 0))
    return pl.pallas_call(
        body, name=_unique(name), grid=(B, S // T),
        in_specs=[row, row, row, pl.BlockSpec((G, T, T), lambda b, i: (0, 0, 0)), pl.BlockSpec((T, G), lambda b, i: (0, 0))],
        out_specs=[row, row, pl.BlockSpec((G, T, T), lambda b, i: (0, 0, 0)), pl.BlockSpec((T, G), lambda b, i: (0, 0))],
        out_shape=[jax.ShapeDtypeStruct((B, S, W), F32), jax.ShapeDtypeStruct((B, S, W), F32),
                   jax.ShapeDtypeStruct((G, T, T), F32), jax.ShapeDtypeStruct((T, G), F32)],
        compiler_params=_params(("arbitrary", "arbitrary")))(dyg, u, vn, w_s, b_sT)


def _cv_glu(name, pw, b_in):
    W = pw.shape[2] // 2

    def fn(r, bv, cv):
        z = r[0] + cv[0]
        return [z[:, :W] * _sigmoid(z[:, W:])], [], []
    return _rowwise(name, fn, [pw], [], [b_in], out_rows=[(W, F32)])[0]


def _cv_glu_bwd(name, pw, dyg, b_in):
    W = pw.shape[2] // 2

    def fn(r, bv, cv):
        z = r[0] + cv[0]
        a, s = z[:, :W], _sigmoid(z[:, W:])
        dpw = jnp.concatenate([r[1] * s, r[1] * a * s * (1.0 - s)], axis=1)
        return [dpw], [], [_csum(dpw)]
    return _rowwise(name, fn, [pw, dyg], [], [b_in], out_rows=[(2 * W, BF16)], out_tsums=[2 * W])


def _cv_ln_act(name, yc, ln_g, ln_b):
    D = yc.shape[2]

    def fn(r, bv, cv):
        xhat, _ = _ln_stats(r[0])
        return [_silu(xhat * cv[0] + cv[1])], [], []
    return _rowwise(name, fn, [yc], [], [ln_g, ln_b], out_rows=[(D, BF16)])[0]


def _cv_ln_act_bwd(name, yc, dys, ln_g, ln_b):
    D = yc.shape[2]

    def fn(r, bv, cv):
        xhat, rstd = _ln_stats(r[0])
        dyn = r[1] * _dsilu(xhat * cv[0] + cv[1])
        dyc = _ln_bwd(dyn * cv[0], xhat, rstd)
        return [dyc], [], [_csum(dyn * xhat), _csum(dyn), _csum(dyc)]
    return _rowwise(name, fn, [yc, dys], [], [ln_g, ln_b], out_rows=[(D, F32)], out_tsums=[D, D, D])


CONV_HALO = 32
CONV_TS, CONV_TC = 256, 128


def _dwconv(name, y, dw, dw_b):
    B, S, D = y.shape
    ts, tc, halo, K = _pick(S, CONV_TS, SUBLANES), _pick(D, CONV_TC), CONV_HALO, CONV_WIDTH

    def body(cur_ref, prev_ref, dw_ref, b_ref, o_ref, buf):
        i = pl.program_id(1)
        buf[pl.ds(0, halo), :] = jnp.where(i > 0, prev_ref[0, pl.ds(ts - halo, halo), :], 0.0)
        buf[pl.ds(halo, ts), :] = cur_ref[0]
        acc = jnp.zeros((ts, tc), F32) + b_ref[...]
        for k in range(K):
            acc = acc + dw_ref[k:k + 1, :] * buf[pl.ds(halo - (K - 1) + k, ts), :]
        o_ref[0] = acc

    return pl.pallas_call(
        body, name=_unique(name), grid=(B, S // ts, D // tc),
        in_specs=[pl.BlockSpec((1, ts, tc), lambda b, i, j: (b, i, j)),
                  pl.BlockSpec((1, ts, tc), lambda b, i, j: (b, jnp.maximum(i - 1, 0), j)),
                  pl.BlockSpec((halo, tc), lambda b, i, j: (0, j)), pl.BlockSpec((1, tc), lambda b, i, j: (0, j))],
        out_specs=pl.BlockSpec((1, ts, tc), lambda b, i, j: (b, i, j)),
        out_shape=jax.ShapeDtypeStruct((B, S, D), F32),
        scratch_shapes=[pltpu.VMEM((halo + ts, tc), F32)],
        compiler_params=_params(("parallel", "parallel", "parallel")))(y, y, dw, dw_b)


def _dwconv_bwd(name, dyc, y, dw):
    B, S, D = y.shape
    ts, tc, halo, K = _pick(S, CONV_TS, SUBLANES), _pick(D, CONV_TC), CONV_HALO, CONV_WIDTH
    nt = S // ts

    def body(g_ref, gnext_ref, y_ref, yprev_ref, dw_ref, dy_ref, ddw_ref, gbuf, ybuf):
        b, i = pl.program_id(1), pl.program_id(2)
        first = jnp.logical_and(b == 0, i == 0)

        @pl.when(first)
        def _():
            ddw_ref[...] = jnp.zeros_like(ddw_ref)

        g = g_ref[0]
        gbuf[pl.ds(0, ts), :] = g
        gbuf[pl.ds(ts, halo), :] = jnp.where(i < nt - 1, gnext_ref[0, pl.ds(0, halo), :], 0.0)
        ybuf[pl.ds(0, halo), :] = jnp.where(i > 0, yprev_ref[0, pl.ds(ts - halo, halo), :], 0.0)
        ybuf[pl.ds(halo, ts), :] = y_ref[0]
        acc = jnp.zeros((ts, tc), F32)
        for k in range(K):
            acc = acc + dw_ref[k:k + 1, :] * gbuf[pl.ds(K - 1 - k, ts), :]
            ddw_ref[k:k + 1, :] += _csum(g * ybuf[pl.ds(halo - (K - 1) + k, ts), :])
        dy_ref[0] = acc

    tile = lambda f: pl.BlockSpec((1, ts, tc), f)
    return pl.pallas_call(
        body, name=_unique(name), grid=(D // tc, B, nt),
        in_specs=[tile(lambda j, b, i: (b, i, j)), tile(lambda j, b, i: (b, jnp.minimum(i + 1, nt - 1), j)),
                  tile(lambda j, b, i: (b, i, j)), tile(lambda j, b, i: (b, jnp.maximum(i - 1, 0), j)),
                  pl.BlockSpec((halo, tc), lambda j, b, i: (0, j))],
        out_specs=[tile(lambda j, b, i: (b, i, j)), pl.BlockSpec((halo, tc), lambda j, b, i: (0, j))],
        out_shape=[jax.ShapeDtypeStruct((B, S, D), F32), jax.ShapeDtypeStruct((halo, D), F32)],
        scratch_shapes=[pltpu.VMEM((ts + halo, tc), F32), pltpu.VMEM((halo + ts, tc), F32)],
        compiler_params=_params(("parallel", "arbitrary", "arbitrary")))(dyc, dyc, y, y, dw)


ATT_BLOCK = 128
ATT_KEY_BLOCK = 512


def _att_tiles(S):
    return _pick(S, ATT_BLOCK, SUBLANES), _pick(S, ATT_KEY_BLOCK, LANES)


def _causal_tiles(i, tq, tk):
    return (i * tq + tq + tk - 1) // tk
FOX_GATE_COLS = 128


def _log_sigmoid(x):
    return jnp.minimum(x, 0.0) - jnp.log(1.0 + jnp.exp(-jnp.abs(x)))


def _fox_gate_cumsum(name, fl, b_f):
    B, S, C = fl.shape
    T = _pick(S, ATT_BLOCK, SUBLANES)

    def body(fl_ref, bf_ref, f_ref, carry):
        @pl.when(pl.program_id(1) == 0)
        def _():
            carry[...] = jnp.zeros_like(carry)
        lf = _log_sigmoid(fl_ref[0] + bf_ref[...])
        lower = (_iota2((T, T), 1) <= _iota2((T, T), 0)).astype(BF16)
        f = _split_dot(lf, lower, lhs01=True, terms=3) + carry[...]
        f_ref[0] = f
        carry[...] = f[T - 1:T, :]

    return pl.pallas_call(
        body, name=_unique(name), grid=(B, S // T),
        in_specs=[pl.BlockSpec((1, T, C), lambda b, i: (b, i, 0)), pl.BlockSpec((1, C), lambda b, i: (0, 0))],
        out_specs=pl.BlockSpec((1, T, C), lambda b, i: (b, i, 0)),
        out_shape=jax.ShapeDtypeStruct((B, S, C), F32),
        scratch_shapes=[pltpu.VMEM((1, C), F32)],
        compiler_params=_params(("arbitrary", "arbitrary")))(fl, b_f)


def _fox_gate_bwd(name, dF, fl, b_f, n_heads):
    B, S, C = fl.shape
    T = _pick(S, ATT_BLOCK, SUBLANES)
    nt = S // T

    def body(df_ref, fl_ref, bf_ref, dfl_ref, dbf_ref, carry):
        first = jnp.logical_and(pl.program_id(0) == 0, pl.program_id(1) == 0)

        @pl.when(pl.program_id(1) == 0)
        def _():
            carry[...] = jnp.zeros_like(carry)

        @pl.when(first)
        def _():
            dbf_ref[...] = jnp.zeros_like(dbf_ref)

        upper = (_iota2((T, T), 1) >= _iota2((T, T), 0)).astype(BF16)
        dlf = _split_dot(df_ref[0], upper, lhs01=True, terms=3) + carry[...]
        carry[...] = dlf[0:1, :]
        x = fl_ref[0] + bf_ref[...]
        dfl = jnp.where(_iota2((T, C), 1) < n_heads, dlf * _sigmoid(-x), 0.0)
        dfl_ref[0] = dfl
        dbf_ref[...] += _csum(dfl)

    rev = lambda b, i: (b, nt - 1 - i, 0)
    return pl.pallas_call(
        body, name=_unique(name), grid=(B, nt),
        in_specs=[pl.BlockSpec((1, T, C), rev), pl.BlockSpec((1, T, C), rev), pl.BlockSpec((1, C), lambda b, i: (0, 0))],
        out_specs=[pl.BlockSpec((1, T, C), rev), pl.BlockSpec((1, C), lambda b, i: (0, 0))],
        out_shape=[jax.ShapeDtypeStruct((B, S, C), F32), jax.ShapeDtypeStruct((1, C), F32)],
        scratch_shapes=[pltpu.VMEM((1, C), F32)],
        compiler_params=_params(("arbitrary", "arbitrary")))(dF, fl, b_f)


_NT = (((1,), (1,)), ((), ()))
_TN = (((0,), (0,)), ((), ()))


def _fox_fwd(name, q, k, v, fq, fk):
    B, H, S, dh = q.shape
    T, TK = _att_tiles(S)
    scale = np.float32(dh ** -0.5)

    def body(q_ref, k_ref, v_ref, fq_ref, fk_ref, o_ref, lse_ref):
        i = pl.program_id(2)
        qb, fqb = q_ref[0, 0], fq_ref[0, 0]
        rows = i * T + _iota2((T, TK), 0)

        def step(j, carry):
            m, l, acc = carry
            ks = pl.ds(pl.multiple_of(j * TK, TK), TK)
            s = lax.dot_general(qb, k_ref[0, 0, ks, :], _NT, preferred_element_type=F32) * scale
            s = s + fqb - fk_ref[0, 0, pl.ds(j, 1), :]
            s = jnp.where(j * TK + _iota2((T, TK), 1) <= rows, s, NEG_INF)
            m_new = jnp.maximum(m, jnp.max(s, axis=1, keepdims=True))
            p = jnp.exp(s - m_new)
            a = jnp.exp(m - m_new)
            l = a * l + jnp.sum(p, axis=1, keepdims=True)
            acc = a * acc + jnp.dot(p.astype(BF16), v_ref[0, 0, ks, :], preferred_element_type=F32)
            return m_new, l, acc

        init = (jnp.full((T, 1), NEG_INF, F32), jnp.zeros((T, 1), F32), jnp.zeros((T, dh), F32))
        m, l, acc = lax.fori_loop(0, _causal_tiles(i, T, TK),step, init)
        o_ref[0, 0] = acc / l
        lse_ref[0, 0] = m + jnp.log(l)

    full = lambda w: pl.BlockSpec((1, 1, S, w), lambda b, h, i: (b, h, 0, 0))
    blk = lambda w: pl.BlockSpec((1, 1, T, w), lambda b, h, i: (b, h, i, 0))
    return pl.pallas_call(
        body, name=_unique(name), grid=(B, H, S // T),
        in_specs=[blk(dh), full(dh), full(dh), blk(1), pl.BlockSpec((1, 1, S // TK, TK), lambda b, h, i: (b, h, 0, 0))],
        out_specs=[blk(dh), blk(1)],
        out_shape=[jax.ShapeDtypeStruct((B, H, S, dh), F32), jax.ShapeDtypeStruct((B, H, S, 1), F32)],
        compiler_params=_params(("parallel", "parallel", "parallel")))(q, k, v, fq, fk)


def _fox_bwd(name, q, k, v, fq, fk, do, lse):
    B, H, S, dh = q.shape
    T, TK = _att_tiles(S)
    nt, nkt = S // T, S // TK
    scale = np.float32(dh ** -0.5)

    def body(q_ref, k_ref, v_ref, fq_ref, fk_ref, do_ref, lse_ref, dq_ref, dk_ref, dv_ref, dfk_ref, p_buf, dp_buf):
        dk_ref[...] = jnp.zeros_like(dk_ref)
        dv_ref[...] = jnp.zeros_like(dv_ref)
        dfk_ref[...] = jnp.zeros_like(dfk_ref)

        def qloop(i, _):
            qs = pl.ds(pl.multiple_of(i * T, T), T)
            qb, fqb, lse_b = q_ref[0, 0, qs, :], fq_ref[0, 0, qs, :], lse_ref[0, 0, qs, :]
            dob16 = do_ref[0, 0, qs, :].astype(BF16)
            rows = i * T + _iota2((T, TK), 0)

            def sweep1(j, delta):
                ks = pl.ds(pl.multiple_of(j * TK, TK), TK)
                s = lax.dot_general(qb, k_ref[0, 0, ks, :], _NT, preferred_element_type=F32) * scale
                s = s + fqb - fk_ref[0, 0, pl.ds(j, 1), :]
                p = jnp.where(j * TK + _iota2((T, TK), 1) <= rows, jnp.exp(s - lse_b), 0.0)
                dp = lax.dot_general(dob16, v_ref[0, 0, ks, :], _NT, preferred_element_type=F32)
                p_buf[j] = p
                dp_buf[j] = dp
                return delta + jnp.sum(p * dp, axis=1, keepdims=True)

            delta = lax.fori_loop(0, _causal_tiles(i, T, TK),sweep1, jnp.zeros((T, 1), F32))

            def sweep2(j, dq):
                ks = pl.ds(pl.multiple_of(j * TK, TK), TK)
                p = p_buf[j]
                ds = p * (dp_buf[j] - delta)
                ds16 = ds.astype(BF16)
                dk_ref[0, 0, ks, :] += lax.dot_general(ds16, qb, _TN, preferred_element_type=F32)
                dv_ref[0, 0, ks, :] += lax.dot_general(p.astype(BF16), dob16, _TN, preferred_element_type=F32)
                dfk_ref[0, 0, pl.ds(j, 1), :] -= _csum(ds)
                return dq + jnp.dot(ds16, k_ref[0, 0, ks, :], preferred_element_type=F32)

            dq_ref[0, 0, qs, :] = lax.fori_loop(0, _causal_tiles(i, T, TK),sweep2, jnp.zeros((T, dh), F32)) * scale
            return 0

        lax.fori_loop(0, nt, qloop, 0)
        dk_ref[...] = dk_ref[...] * scale

    full = lambda w: pl.BlockSpec((1, 1, S, w), lambda b, h: (b, h, 0, 0))
    fks = pl.BlockSpec((1, 1, nkt, TK), lambda b, h: (b, h, 0, 0))
    return pl.pallas_call(
        body, name=_unique(name), grid=(B, H),
        in_specs=[full(dh), full(dh), full(dh), full(1), fks, full(dh), full(1)],
        out_specs=[full(dh), full(dh), full(dh), fks],
        out_shape=[jax.ShapeDtypeStruct((B, H, S, dh), F32)] * 3 + [jax.ShapeDtypeStruct((B, H, nkt, TK), F32)],
        scratch_shapes=[pltpu.VMEM((nkt, T, TK), F32), pltpu.VMEM((nkt, T, TK), F32)],
        compiler_params=_params(("parallel", "parallel")))(q, k, v, fq, fk, do, lse)


def _sb_terms(z):
    t = jnp.exp(-jnp.abs(z))
    lp = jnp.log(1.0 + t)
    inv = 1.0 / (1.0 + t)
    pos = z >= 0.0
    return jnp.minimum(z, 0.0) - lp, jnp.minimum(-z, 0.0) - lp, jnp.where(pos, 1.0, t) * inv, jnp.where(pos, t, 1.0) * inv


def _lane_scan(x, reverse):
    lane = _iota2(x.shape, 1)
    y, d = x, 1
    while d < LANES:
        if reverse:
            y = y + jnp.where(lane + d < LANES, pltpu.roll(y, LANES - d, 1), 0.0)
        else:
            y = y + jnp.where(lane >= d, pltpu.roll(y, d, 1), 0.0)
        d *= 2
    return y


def _tile_scan(x, run, reverse):
    nseg = x.shape[1] // LANES
    parts = [None] * nseg
    for sgm in (range(nseg - 1, -1, -1) if reverse else range(nseg)):
        incl = _lane_scan(x[:, sgm * LANES:(sgm + 1) * LANES], reverse)
        parts[sgm] = incl + run
        run = run + (incl[:, 0:1] if reverse else incl[:, LANES - 1:LANES])
    return jnp.concatenate(parts, axis=1), run


def _sb_fwd(name, q, k, v):
    B, H, S, dh = q.shape
    T, TK = _att_tiles(S)
    scale = np.float32(dh ** -0.5)

    def body(q_ref, k_ref, v_ref, o_ref, lt_ref):
        i = pl.program_id(2)
        qb = q_ref[0, 0]
        rows = i * T + _iota2((T, TK), 0)
        n_tiles = _causal_tiles(i, T, TK)

        def step(jj, carry):
            run, acc = carry
            j = n_tiles - 1 - jj
            ks = pl.ds(pl.multiple_of(j * TK, TK), TK)
            z = lax.dot_general(qb, k_ref[0, 0, ks, :], _NT, preferred_element_type=F32) * scale
            mask = j * TK + _iota2((T, TK), 1) < rows
            lb, l1, _, _ = _sb_terms(z)
            l1 = jnp.where(mask, l1, 0.0)
            incl, run = _tile_scan(l1, run, reverse=True)
            a = jnp.where(mask, jnp.exp(lb + (incl - l1)), 0.0)
            acc = acc + jnp.dot(a.astype(BF16), v_ref[0, 0, ks, :], preferred_element_type=F32)
            return run, acc

        run, acc = lax.fori_loop(0, n_tiles, step, (jnp.zeros((T, 1), F32), jnp.zeros((T, dh), F32)))
        o_ref[0, 0] = acc
        lt_ref[0, 0] = run

    full = lambda w: pl.BlockSpec((1, 1, S, w), lambda b, h, i: (b, h, 0, 0))
    blk = lambda w: pl.BlockSpec((1, 1, T, w), lambda b, h, i: (b, h, i, 0))
    return pl.pallas_call(
        body, name=_unique(name), grid=(B, H, S // T),
        in_specs=[blk(dh), full(dh), full(dh)], out_specs=[blk(dh), blk(1)],
        out_shape=[jax.ShapeDtypeStruct((B, H, S, dh), F32), jax.ShapeDtypeStruct((B, H, S, 1), F32)],
        compiler_params=_params(("parallel", "parallel", "parallel")))(q, k, v)


def _sb_bwd(name, q, k, v, do, lt):
    B, H, S, dh = q.shape
    T, TK = _att_tiles(S)
    nt = S // T
    scale = np.float32(dh ** -0.5)

    def body(q_ref, k_ref, v_ref, do_ref, lt_ref, dq_ref, dk_ref, dv_ref):
        dk_ref[...] = jnp.zeros_like(dk_ref)
        dv_ref[...] = jnp.zeros_like(dv_ref)

        def qloop(i, _):
            qs = pl.ds(pl.multiple_of(i * T, T), T)
            qb, dob16, lt_b = q_ref[0, 0, qs, :], do_ref[0, 0, qs, :].astype(BF16), lt_ref[0, 0, qs, :]
            rows = i * T + _iota2((T, TK), 0)

            def kloop(j, carry):
                pl1, pe, dq = carry
                ks = pl.ds(pl.multiple_of(j * TK, TK), TK)
                kb, vb = k_ref[0, 0, ks, :], v_ref[0, 0, ks, :]
                z = lax.dot_general(qb, kb, _NT, preferred_element_type=F32) * scale
                mask = j * TK + _iota2((T, TK), 1) < rows
                lb, l1, beta, omb = _sb_terms(z)
                l1 = jnp.where(mask, l1, 0.0)
                upto, pl1 = _tile_scan(l1, pl1, reverse=False)
                a = jnp.where(mask, jnp.exp(lb + (lt_b - upto)), 0.0)
                e = lax.dot_general(dob16, vb, _NT, preferred_element_type=F32) * a
                e_upto, pe = _tile_scan(e, pe, reverse=False)
                dz16 = jnp.where(mask, e * omb - (e_upto - e) * beta, 0.0).astype(BF16)
                dk_ref[0, 0, ks, :] += lax.dot_general(dz16, qb, _TN, preferred_element_type=F32)
                dv_ref[0, 0, ks, :] += lax.dot_general(a.astype(BF16), dob16, _TN, preferred_element_type=F32)
                return pl1, pe, dq + jnp.dot(dz16, kb, preferred_element_type=F32)

            init = (jnp.zeros((T, 1), F32), jnp.zeros((T, 1), F32), jnp.zeros((T, dh), F32))
            dq_ref[0, 0, qs, :] = lax.fori_loop(0, _causal_tiles(i, T, TK),kloop, init)[2] * scale
            return 0

        lax.fori_loop(0, nt, qloop, 0)
        dk_ref[...] = dk_ref[...] * scale

    full = lambda w: pl.BlockSpec((1, 1, S, w), lambda b, h: (b, h, 0, 0))
    return pl.pallas_call(
        body, name=_unique(name), grid=(B, H),
        in_specs=[full(dh), full(dh), full(dh), full(dh), full(1)], out_specs=[full(dh)] * 3,
        out_shape=[jax.ShapeDtypeStruct((B, H, S, dh), F32)] * 3,
        compiler_params=_params(("parallel", "parallel")))(q, k, v, do, lt)


def _adamw(name, w, g, m, v):
    shape = w.shape
    cols = shape[-1] if (w.ndim >= 2 and shape[-1] % LANES == 0) else 0
    n = w.size
    if cols:
        prep = lambda t: t.reshape(1, n // cols, cols)
    else:
        cols = LANES
        pad = (-n) % (SUBLANES * LANES)
        prep = lambda t: jnp.pad(t.reshape(-1), (0, pad), constant_values=1.0).reshape(1, (n + pad) // cols, cols)

    def fn(r, bv, cv):
        w_, g_, m_, v_ = r
        m2 = ADAM_B1 * m_ + (1.0 - ADAM_B1) * g_
        v2 = ADAM_B2 * v_ + (1.0 - ADAM_B2) * (g_ * g_)
        m_hat = m2 / (1.0 - ADAM_B1 ** ADAM_STEP)
        v_hat = v2 / (1.0 - ADAM_B2 ** ADAM_STEP)
        return [-ADAM_LR * (m_hat / (jnp.sqrt(v_hat) + ADAM_EPS) + ADAM_WD * w_), m2, v2], [], []
    outs = _rowwise(name, fn, [prep(w), prep(g), prep(m), prep(v)], out_rows=[(cols, F32)] * 3, tm=512)
    return [o.reshape(-1)[:n].reshape(shape) for o in outs]


def _sum8(name, parts):
    def fn(r, bv, cv):
        s = r[0]
        for t in r[1:]:
            s = s + t
        return [s], [], []
    rows = [parts[i][None] for i in range(parts.shape[0])]
    return _rowwise(name, fn, rows, out_rows=[(parts.shape[2], F32)])[0][0]


def _pack(arrs, cols, dtype, row_mult):
    flat = jnp.concatenate([a.reshape(-1).astype(dtype) for a in arrs])
    pad = (-flat.size) % (cols * row_mult)
    return jnp.pad(flat, (0, pad)).reshape(-1, cols)


def _unpack(flat, shapes):
    out, off = [], 0
    for s in shapes:
        n = int(np.prod(s))
        out.append(flat[off:off + n].reshape(s))
        off += n
    return out


def _heads(t, H):
    B, S, W = t.shape
    return t.reshape(B, S, H, W // H).transpose(0, 2, 1, 3)


def _unheads(t):
    B, H, S, dh = t.shape
    return t.transpose(0, 2, 1, 3).reshape(B, S, H * dh)


def kernel(*args):
    _names_used.clear()
    p = dict(zip(ARGS, args))
    x, target = p['x'], p['loss_target']
    B, S, D = x.shape
    T = B * S
    depth = p['ln1_g'].shape[0]
    H = D // HEAD_DIM
    alpha = np.float32((2.0 * depth) ** 0.25)
    cx, cy, cc = _mesh_pos()
    my_q = 2 * cx + cy
    axes = ("x", "y", "c")
    two = lambda t: t.reshape(T, t.shape[-1])
    three = lambda t: t.reshape(B, S, t.shape[-1])

    small_in = [p['c']] + [p[n] for n in SMALL_SPLIT]
    g1 = _all_gather8("ag_small", [_pack(small_in, LANES, F32, SUBLANES)])[0]
    g1 = g1.reshape(N_DEV, -1)
    c_all = g1[:, :B * D].reshape(N_DEV * B, D)
    per_chip = [_unpack(g1[2 * q], [a.shape for a in small_in])[1:] for q in range(N_CHIPS)]
    small = {n: jnp.concatenate([per_chip[q][i] for q in range(N_CHIPS)], axis=-1) for i, n in enumerate(SMALL_SPLIT)}
    for n in SMALL_REPL:
        small[n] = p[n]

    n_seq = N_DEV * B
    seq_pad = -(-n_seq // LANES) * LANES
    c_act = _rowwise("c_act", lambda r, bv, cv: ([_silu(r[0])], [], []),
                     [jnp.pad(c_all, ((0, seq_pad - n_seq), (0, 0)))[None]], out_rows=[(D, F32)])[0][0]
    mod_cols = p['mod_w'].shape[2]
    mod_part = jnp.stack([_mm(f"mod_fwd{l}", c_act, p['mod_w'][l])[:n_seq] for l in range(depth)])
    half_layers = depth // 2
    mod_half = lax.dynamic_slice_in_dim(mod_part, cc * half_layers, half_layers, axis=0)
    gm_ = _all_gather8("ag_mod", [mod_half.reshape(half_layers * n_seq, mod_cols)])[0]
    mod_all = gm_.reshape(N_CHIPS, 2, half_layers, n_seq, mod_cols).transpose(1, 2, 3, 0, 4).reshape(depth, n_seq, 6 * D)
    mod_mine = lax.dynamic_slice_in_dim(mod_all, (2 * my_q + cc) * B, B, axis=1)
    mod = _rowwise("mod_bias", lambda r, bv, cv: ([r[0] + bv[0]], [], []), [mod_mine], [p['mod_b'][:, None, :]],
                   out_rows=[(6 * D, F32)])[0]
    mods = [[mod[l, :, None, i * D:(i + 1) * D] for i in range(6)] for l in range(depth)]

    big_names = list(BIG)
    shard_shapes = [p[n].shape for n in big_names]
    half_rows = [s[0] * s[1] // 2 for s in shard_shapes]
    w_halves = [lax.dynamic_slice_in_dim(p[n].reshape(-1, s[2]), cc * hr, hr, axis=0).astype(BF16)
                for n, s, hr in zip(big_names, shard_shapes, half_rows)]
    W = {}
    for n, s, g in zip(big_names, shard_shapes, _all_gather8("ag_weights", w_halves)):
        seg = g.reshape((N_CHIPS,) + s)
        W[n] = jnp.concatenate([seg[q] for q in range(N_CHIPS)], axis=BIG[n])

    def vec(n, j):
        return small[n][j][None, :]

    def attn_proj(h1, w_in, gate_cols):
        wp = jnp.pad(w_in, ((0, 0), (0, gate_cols))) if gate_cols else w_in
        proj = three(_mm("att_proj", two(h1), wp))
        q, k, v = [_heads(proj[..., i * D:(i + 1) * D].astype(BF16), H) for i in range(3)]
        return wp, proj, q, k, v

    def gm_fwd(j, h1):
        zin = three(_mm("gm_in", two(h1), W['gm_w_in'][j]))
        u, vn = _gm_act("gm_act", zin, vec('gm_b_in', j), vec('gm_ln_g', j), vec('gm_ln_b', j))
        b_sT = small['gm_b_s'][j].T
        yg = _gm_spatial("gm_spatial", u, vn, small['gm_w_s'][j], b_sT)
        return three(_mm("gm_out", two(yg), W['gm_w_out'][j])), (zin, u, vn, b_sT, yg)

    def gm_bwd(j, h1, dy1, cache):
        zin, u, vn, b_sT, yg = cache
        g = {'gm_w_out': _mm("gm_dwout", two(yg), two(dy1), ta=True)}
        dyg = three(_mm("gm_dyg", two(dy1), W['gm_w_out'][j], tb=True))
        du, dvn, dws, dbsT = _gm_spatial_bwd("gm_spatial_bwd", dyg, u, vn, small['gm_w_s'][j], b_sT)
        dzin, g['gm_b_in'], g['gm_ln_g'], g['gm_ln_b'] = _gm_act_bwd("gm_act_bwd", zin, du, dvn, vec('gm_b_in', j), vec('gm_ln_g', j))
        g['gm_w_s'], g['gm_b_s'] = dws, dbsT.T
        g['gm_w_in'] = _mm("gm_dwin", two(h1), two(dzin), ta=True)
        return _mm("gm_dh", two(dzin), W['gm_w_in'][j], tb=True), g

    def fox_fwd(j, h1):
        wp, proj, q, k, v = attn_proj(h1, W['fox_w_in'][j], 3 * D + FOX_GATE_COLS - W['fox_w_in'].shape[2])
        fl = proj[..., 3 * D:]
        bf = jnp.pad(small['fox_b_f'][j][None, :], ((0, 0), (0, FOX_GATE_COLS - H)))
        Fh = _fox_gate_cumsum("fox_gate", fl, bf)[..., :H].transpose(0, 2, 1)
        fq, fk = Fh[..., None], Fh.reshape(B, H, -1, _att_tiles(S)[1])
        o, lse = _fox_fwd("fox_fwd", q, k, v, fq, fk)
        o2 = _unheads(o)
        return three(_mm("fox_out", two(o2), W['fox_w_out'][j])), (wp, q, k, v, fl, bf, fq, fk, lse, o2)

    def fox_bwd(j, h1, dy1, cache):
        wp, q, k, v, fl, bf, fq, fk, lse, o2 = cache
        g = {'fox_w_out': _mm("fox_dwout", two(o2), two(dy1), ta=True)}
        do = _heads(three(_mm("fox_do", two(dy1), W['fox_w_out'][j], tb=True)), H)
        dq, dk, dv, dfk = _fox_bwd("fox_bwd", q, k, v, fq, fk, do, lse)
        dF = jnp.pad(dfk.reshape(B, H, S).transpose(0, 2, 1), ((0, 0), (0, 0), (0, FOX_GATE_COLS - H)))
        dfl, dbf = _fox_gate_bwd("fox_gate_bwd", dF, fl, bf, H)
        dproj = jnp.concatenate([_unheads(dq).astype(BF16), _unheads(dk).astype(BF16), _unheads(dv).astype(BF16),
                                 dfl.astype(BF16)], axis=-1)
        g['fox_w_in'] = _mm("fox_dwin", two(h1), two(dproj), ta=True)[:, :W['fox_w_in'].shape[2]]
        g['fox_b_f'] = dbf[0, :H]
        return _mm("fox_dh", two(dproj), wp, tb=True), g

    def sb_fwd(j, h1):
        wp, proj, q, k, v = attn_proj(h1, W['sb_w_in'][j], 0)
        o, lt = _sb_fwd("sb_fwd", q, k, v)
        o2 = _unheads(o)
        return three(_mm("sb_out", two(o2), W['sb_w_out'][j])), (q, k, v, lt, o2)

    def sb_bwd(j, h1, dy1, cache):
        q, k, v, lt, o2 = cache
        g = {'sb_w_out': _mm("sb_dwout", two(o2), two(dy1), ta=True)}
        do = _heads(three(_mm("sb_do", two(dy1), W['sb_w_out'][j], tb=True)), H)
        dq, dk, dv = _sb_bwd("sb_bwd", q, k, v, do, lt)
        dproj = jnp.concatenate([_unheads(dq).astype(BF16), _unheads(dk).astype(BF16), _unheads(dv).astype(BF16)], axis=-1)
        g['sb_w_in'] = _mm("sb_dwin", two(h1), two(dproj), ta=True)
        return _mm("sb_dh", two(dproj), W['sb_w_in'][j], tb=True), g

    def cv_fwd(j, h1):
        pw = three(_mm("cv_in", two(h1), W['cv_w_in'][j]))
        ygl = _cv_glu("cv_glu", pw, vec('cv_b_in', j))
        dw = jnp.pad(small['cv_dw'][j], ((0, CONV_HALO - CONV_WIDTH), (0, 0)))
        yc = _dwconv("cv_dwconv", ygl, dw, vec('cv_dw_b', j))
        ys = _cv_ln_act("cv_ln_act", yc, vec('cv_ln_g', j), vec('cv_ln_b', j))
        return three(_mm("cv_out", two(ys), W['cv_w_out'][j])), (pw, ygl, dw, yc, ys)

    def cv_bwd(j, h1, dy1, cache):
        pw, ygl, dw, yc, ys = cache
        g = {'cv_w_out': _mm("cv_dwout", two(ys), two(dy1), ta=True)}
        dys = three(_mm("cv_dys", two(dy1), W['cv_w_out'][j], tb=True))
        dyc, g['cv_ln_g'], g['cv_ln_b'], g['cv_dw_b'] = _cv_ln_act_bwd("cv_ln_act_bwd", yc, dys, vec('cv_ln_g', j), vec('cv_ln_b', j))
        dygl, ddw = _dwconv_bwd("cv_dwconv_bwd", dyc, ygl, dw)
        g['cv_dw'] = ddw[:CONV_WIDTH]
        dpw, g['cv_b_in'] = _cv_glu_bwd("cv_glu_bwd", pw, dygl, vec('cv_b_in', j))
        g['cv_w_in'] = _mm("cv_dwin", two(h1), two(dpw), ta=True)
        return _mm("cv_dh", two(dpw), W['cv_w_in'][j], tb=True), g

    mixers = [(gm_fwd, gm_bwd), (fox_fwd, fox_bwd), (sb_fwd, sb_bwd), (cv_fwd, cv_bwd)]
    n_mix = len(mixers)

    saved = []
    for l in range(depth):
        m, j = l % n_mix, l // n_mix
        sh1, sc1, g1_, sh2, sc2, g2_ = mods[l]
        ybias = vec('cv_b_out', j) if m == 3 else None
        h1 = _modulate("mod1", x, sc1, sh1)
        y1, cache = mixers[m][0](j, h1)
        xm = _resid_ln("resid_ln1", alpha, x, y1, g1_, small['ln1_g'][l][None], small['ln1_b'][l][None], ybias)
        h2 = _modulate("mod2", xm, sc2, sh2)
        z = three(_mm("ffn_in", two(h2), W['ffn_w_in'][l]))
        a = _swiglu_act("ffn_act", z)
        y2 = three(_mm("ffn_out", two(a), W['ffn_w_out'][l]))
        xo = _resid_ln("resid_ln2", alpha, xm, y2, g2_, small['ln2_g'][l][None], small['ln2_b'][l][None])
        saved.append((x, h1, y1, cache, xm, h2, z, a, y2, ybias))
        x = xo

    dx, sq = _loss_head("loss_head", x, target)
    loss = lax.psum(jnp.sum(sq) * np.float32(0.5 / D), axes)

    grads = {n: [None] * p[n].shape[0] for n in WEIGHTS}
    dmod = [None] * depth
    for l in reversed(range(depth)):
        m, j = l % n_mix, l // n_mix
        sh1, sc1, g1_, sh2, sc2, g2_ = mods[l]
        x_in, h1, y1, cache, xm, h2, z, a, y2, ybias = saved[l]
        dr2, dy2, dg2, grads['ln2_g'][l], grads['ln2_b'][l], _ = _resid_ln_bwd("resid_ln2_bwd", alpha, dx, xm, y2, g2_, small['ln2_g'][l][None])
        grads['ffn_w_out'][l] = _mm("ffn_dwout", two(a), two(dy2), ta=True)
        da = three(_mm("ffn_da", two(dy2), W['ffn_w_out'][l], tb=True))
        dz = _swiglu_act_bwd("ffn_act_bwd", da, z)
        grads['ffn_w_in'][l] = _mm("ffn_dwin", two(h2), two(dz), ta=True)
        dh2 = three(_mm("ffn_dh", two(dz), W['ffn_w_in'][l], tb=True))
        dxm, dsc2, dsh2 = _modulate_bwd("mod2_bwd", alpha, dh2, dr2, xm, sc2)
        dr1, dy1, dg1, grads['ln1_g'][l], grads['ln1_b'][l], dyb = _resid_ln_bwd("resid_ln1_bwd", alpha, dxm, x_in, y1, g1_, small['ln1_g'][l][None], ybias)
        dh1, mg = mixers[m][1](j, h1, dy1, cache)
        if m == 3:
            mg['cv_b_out'] = dyb
        for n, gval in mg.items():
            grads[n][j] = gval
        dx, dsc1, dsh1 = _modulate_bwd("mod1_bwd", alpha, three(dh1), dr1, x_in, sc1)
        dmod[l] = jnp.concatenate([dsh1, dsc1, dg1, dsh2, dsc2, dg2], axis=-1)[:, 0, :]
    grad_x = dx
    dmod = jnp.stack(dmod)
    grads['mod_b'] = [jnp.sum(dmod[l], axis=0) for l in range(depth)]
    full_shape = {n: tuple(t.shape) for n, t in small.items()}

    small_names = SMALL_REPL + SMALL_SPLIT
    small_parts = [jnp.stack([gv.reshape(full_shape[n][1:]) for gv in grads[n]]) for n in small_names]
    pack_a = _pack([dmod], LANES, F32, SUBLANES)
    pack_b = _pack(small_parts, LANES, F32, SUBLANES)
    g2 = _all_gather8("ag_grads_small", [jnp.concatenate([pack_a, pack_b], axis=0)])[0]
    rows_a = pack_a.shape[0]
    dmod_all = g2[:, :rows_a].reshape(N_DEV, -1)[:, :dmod.size].reshape(N_DEV, depth, B, 6 * D)
    dmod_all = dmod_all.transpose(1, 0, 2, 3).reshape(depth, n_seq, 6 * D)
    small_sum = _sum8("sum_grads_small", g2[:, rows_a:]).reshape(-1)
    g_small = dict(zip(small_names, _unpack(small_sum, [full_shape[n] for n in small_names])))
    for n in SMALL_SPLIT:
        w = p[n].shape[-1]
        g_small[n] = lax.dynamic_slice_in_dim(g_small[n], my_q * w, w, axis=g_small[n].ndim - 1)

    dm_cols = lax.dynamic_slice_in_dim(dmod_all, my_q * mod_cols, mod_cols, axis=2)
    dm_cols = jnp.pad(dm_cols, ((0, 0), (0, seq_pad - n_seq), (0, 0)))
    g_mod_w = jnp.stack([_mm(f"mod_dw{l}", c_act, dm_cols[l], ta=True) for l in range(depth)])

    keep, give = [], []
    for n, s, hr in zip(big_names, shard_shapes, half_rows):
        gfull = jnp.stack(grads[n])
        g4 = jnp.stack(jnp.split(gfull, N_CHIPS, axis=BIG[n])).reshape(N_CHIPS, 2 * hr, s[2])
        keep.append(lax.dynamic_slice_in_dim(g4, cc * hr, hr, axis=1))
        give.append(lax.dynamic_slice_in_dim(g4, (1 - cc) * hr, hr, axis=1))
    got = _sibling_exchange("rs_sibling", give)
    chip_sum = [_rowwise("rs_add_sibling", lambda r, bv, cv: ([r[0] + r[1]], [], []),
                         [a.reshape(1, -1, a.shape[2]), b.reshape(1, -1, a.shape[2])],
                         out_rows=[(a.shape[2], BF16)], tm=512)[0].reshape(a.shape) for a, b in zip(keep, got)]
    from_chips = _chip_all_to_all("rs_chips", chip_sum)
    half_sum = [_rowwise("rs_add_chips", lambda r, bv, cv: ([((r[0] + r[1]) + r[2]) + r[3]], [], []),
                         [t[q][None] for q in range(N_CHIPS)], out_rows=[(t.shape[2], F32)], tm=512)[0][0]
                for t in from_chips]
    other = _sibling_exchange("rs_share", half_sum)
    g_big = {n: jnp.concatenate([jnp.where(cc == 0, a, b), jnp.where(cc == 0, b, a)], axis=0).reshape(s)
             for n, s, a, b in zip(big_names, shard_shapes, half_sum, other)}

    g_out = {**g_small, **g_big, 'mod_w': g_mod_w}
    upd = {n: _adamw("adamw_" + n, p[n], g_out[n], p['m_' + n], p['v_' + n]) for n in WEIGHTS}
    return (loss, grad_x, *[g_out[n] for n in WEIGHTS], *[upd[n][0] for n in WEIGHTS],
            *[upd[n][1] for n in WEIGHTS], *[upd[n][2] for n in WEIGHTS])
```

```python
import math

import jax
import jax.numpy as jnp
import numpy as np
from jax import lax
from jax.experimental import pallas as pl
from jax.experimental.pallas import tpu as pltpu

F32, BF16 = jnp.float32, jnp.bfloat16

HEAD_DIM = 64
CONV_WIDTH = 31
LN_EPS = 1e-5
NEG_INF = -1e30
ADAM_LR, ADAM_B1, ADAM_B2, ADAM_EPS, ADAM_WD, ADAM_STEP = 0.001, 0.9, 0.999, 1e-08, 0.01, 10

LANES = 128
SUBLANES = 8
VMEM_LIMIT_BYTES = 56 * 1024 * 1024
N_CHIPS = 4
N_DEV = 8

WEIGHTS = ['mod_w', 'mod_b', 'ln1_g', 'ln1_b', 'ln2_g', 'ln2_b', 'ffn_w_in', 'ffn_w_out', 'gm_w_in', 'gm_b_in',
           'gm_ln_g', 'gm_ln_b', 'gm_w_s', 'gm_b_s', 'gm_w_out', 'fox_w_in', 'fox_b_f', 'fox_w_out', 'sb_w_in',
           'sb_w_out', 'cv_w_in', 'cv_b_in', 'cv_dw', 'cv_dw_b', 'cv_ln_g', 'cv_ln_b', 'cv_w_out', 'cv_b_out']
ARGS = ['x', 'c'] + WEIGHTS + ['loss_target'] + ['m_' + n for n in WEIGHTS] + ['v_' + n for n in WEIGHTS]
BIG = {'ffn_w_in': 2, 'ffn_w_out': 1, 'gm_w_in': 2, 'gm_w_out': 1, 'fox_w_in': 2, 'fox_w_out': 1,
       'sb_w_in': 2, 'sb_w_out': 1, 'cv_w_in': 2, 'cv_w_out': 1}
SMALL_SPLIT = ['cv_b_in', 'cv_dw', 'cv_dw_b', 'cv_ln_g', 'cv_ln_b', 'cv_b_out']
SMALL_REPL = ['mod_b', 'ln1_g', 'ln1_b', 'ln2_g', 'ln2_b', 'gm_b_in', 'gm_ln_g', 'gm_ln_b', 'gm_w_s', 'gm_b_s', 'fox_b_f']
PACK_COLS = 1024


_names_used = {}


def _unique(name):
    k = _names_used.get(name, 0)
    _names_used[name] = k + 1
    return name if k == 0 else f"{name}_{k}"


def _params(sem):
    return pltpu.CompilerParams(dimension_semantics=sem, vmem_limit_bytes=VMEM_LIMIT_BYTES)


def _pick(dim, pref, mult=LANES):
    if dim <= pref:
        return dim
    best = 0
    for t in range(mult, pref + 1, mult):
        if dim % t == 0:
            best = t
    assert best, (dim, pref)
    return best


def _mesh_pos():
    return lax.axis_index("x"), lax.axis_index("y"), lax.axis_index("c")


AG_COPIES = 7
A2A_COPIES = 3


def _comm_call(name, body, blks, out_shapes, n_sems):
    n = len(blks)
    hbm = pl.BlockSpec(memory_space=pl.ANY)
    return pl.pallas_call(
        body, name=_unique(name), out_shape=out_shapes, in_specs=[hbm] * n, out_specs=[hbm] * n,
        scratch_shapes=[pltpu.SemaphoreType.DMA((n_sems * n,)), pltpu.SemaphoreType.DMA((n_sems * n,)),
                        pltpu.SemaphoreType.DMA((n,))],
    )(*blks)


def _all_gather8(name, blks):
    n = len(blks)

    def body(*refs):
        x_refs, out_refs, (send_sems, recv_sems, local_sems) = refs[:n], refs[n:2 * n], refs[2 * n:]
        x, y, c = _mesh_pos()
        me, sibling = (x, y, c), (x, y, 1 - c)
        chips = [(1 - x, y), (x, 1 - y), (1 - x, 1 - y)]

        def copy(a, k, block, to, from_input=False):
            px, py, pc = block
            slot = out_refs[a].at[4 * px + 2 * py + pc]
            return pltpu.make_async_remote_copy(
                src_ref=x_refs[a] if from_input else slot, dst_ref=slot,
                send_sem=send_sems.at[AG_COPIES * a + k], recv_sem=recv_sems.at[AG_COPIES * a + k],
                device_id=to, device_id_type=pl.DeviceIdType.MESH)

        local, sent = [], []
        for a in range(n):
            local.append(pltpu.make_async_copy(x_refs[a], out_refs[a].at[4 * x + 2 * y + c], local_sems.at[a]))
            local[-1].start()
            first = [copy(a, 0, me, sibling, True)] + [copy(a, 1 + j, me, (*chip, c), True) for j, chip in enumerate(chips)]
            for cp in first:
                cp.start()
            sent += first
        for a in range(n):
            for j, chip in enumerate(chips):
                copy(a, 1 + j, (*chip, c), me).wait_recv()
                sent.append(copy(a, 4 + j, (*chip, c), sibling))
                sent[-1].start()
        for a in range(n):
            copy(a, 0, sibling, me).wait_recv()
            for j, chip in enumerate(chips):
                copy(a, 4 + j, (*chip, 1 - c), me).wait_recv()
        for cp in sent:
            cp.wait_send()
        for cp in local:
            cp.wait()

    return _comm_call(name, body, blks, [jax.ShapeDtypeStruct((N_DEV,) + b.shape, b.dtype) for b in blks], AG_COPIES)


def _sibling_exchange(name, blks):
    n = len(blks)

    def body(*refs):
        x_refs, out_refs, (send_sems, recv_sems, _) = refs[:n], refs[n:2 * n], refs[2 * n:]
        x, y, c = _mesh_pos()
        cps = [pltpu.make_async_remote_copy(src_ref=x_refs[a], dst_ref=out_refs[a], send_sem=send_sems.at[a],
                                            recv_sem=recv_sems.at[a], device_id=(x, y, 1 - c),
                                            device_id_type=pl.DeviceIdType.MESH) for a in range(n)]
        for cp in cps:
            cp.start()
        for cp in cps:
            cp.wait()

    return _comm_call(name, body, blks, [jax.ShapeDtypeStruct(b.shape, b.dtype) for b in blks], 1)


def _chip_all_to_all(name, blks):
    n = len(blks)

    def body(*refs):
        x_refs, out_refs, (send_sems, recv_sems, local_sems) = refs[:n], refs[n:2 * n], refs[2 * n:]
        x, y, c = _mesh_pos()
        chips = [(1 - x, y), (x, 1 - y), (1 - x, 1 - y)]
        my_q = 2 * x + y

        def copy(a, j, src_q, dst_q):
            px, py = chips[j]
            return pltpu.make_async_remote_copy(
                src_ref=x_refs[a].at[src_q], dst_ref=out_refs[a].at[dst_q],
                send_sem=send_sems.at[A2A_COPIES * a + j], recv_sem=recv_sems.at[A2A_COPIES * a + j],
                device_id=(px, py, c), device_id_type=pl.DeviceIdType.MESH)

        local, sent = [], []
        for a in range(n):
            local.append(pltpu.make_async_copy(x_refs[a].at[my_q], out_refs[a].at[my_q], local_sems.at[a]))
            local[-1].start()
            sent += [copy(a, j, 2 * px + py, my_q) for j, (px, py) in enumerate(chips)]
            for cp in sent[-A2A_COPIES:]:
                cp.start()
        for a in range(n):
            for j, (px, py) in enumerate(chips):
                copy(a, j, my_q, 2 * px + py).wait_recv()
        for cp in sent:
            cp.wait_send()
        for cp in local:
            cp.wait()

    return _comm_call(name, body, blks, [jax.ShapeDtypeStruct(b.shape, b.dtype) for b in blks], A2A_COPIES)


def _rowwise(name, fn, rows, bvecs=(), cvecs=(), out_rows=(), out_bsums=(), out_tsums=(), tm=256):
    B, S = rows[0].shape[:2]
    tm = _pick(S, tm, SUBLANES)
    n_r, n_b, n_c = len(rows), len(bvecs), len(cvecs)
    n_or, n_ob = len(out_rows), len(out_bsums)

    def body(*refs):
        ins, outs = refs[:n_r + n_b + n_c], refs[n_r + n_b + n_c:]
        r = [ref[0] for ref in ins[:n_r]]
        bv = [ref[0] for ref in ins[n_r:n_r + n_b]]
        cv = [ref[...] for ref in ins[n_r + n_b:]]
        o_rows, o_bsums, o_tsums = fn(r, bv, cv)
        b, i = pl.program_id(0), pl.program_id(1)
        for ref, val in zip(outs[:n_or], o_rows):
            ref[0] = val.astype(ref.dtype)
        for ref, val in zip(outs[n_or:n_or + n_ob], o_bsums):
            @pl.when(i == 0)
            def _(ref=ref, val=val):
                ref[0] = val

            @pl.when(i > 0)
            def _(ref=ref, val=val):
                ref[0] += val
        for ref, val in zip(outs[n_or + n_ob:], o_tsums):
            first = jnp.logical_and(b == 0, i == 0)

            @pl.when(first)
            def _(ref=ref, val=val):
                ref[...] = val

            @pl.when(jnp.logical_not(first))
            def _(ref=ref, val=val):
                ref[...] += val

    in_specs = [pl.BlockSpec((1, tm, a.shape[2]), lambda b, i: (b, i, 0)) for a in rows]
    in_specs += [pl.BlockSpec((1, 1, a.shape[2]), lambda b, i: (b, 0, 0)) for a in bvecs]
    in_specs += [pl.BlockSpec((1, a.shape[1]), lambda b, i: (0, 0)) for a in cvecs]
    out_shape = [jax.ShapeDtypeStruct((B, S, cdim), dt) for cdim, dt in out_rows]
    out_specs = [pl.BlockSpec((1, tm, cdim), lambda b, i: (b, i, 0)) for cdim, _ in out_rows]
    out_shape += [jax.ShapeDtypeStruct((B, 1, cdim), F32) for cdim in out_bsums]
    out_specs += [pl.BlockSpec((1, 1, cdim), lambda b, i: (b, 0, 0)) for cdim in out_bsums]
    out_shape += [jax.ShapeDtypeStruct((1, cdim), F32) for cdim in out_tsums]
    out_specs += [pl.BlockSpec((1, cdim), lambda b, i: (0, 0)) for cdim in out_tsums]
    sem = ("arbitrary", "arbitrary") if out_tsums else ("parallel", "arbitrary")
    res = pl.pallas_call(body, name=_unique(name), grid=(B, S // tm), in_specs=in_specs, out_specs=out_specs,
                         out_shape=out_shape, compiler_params=_params(sem))(*rows, *bvecs, *cvecs)
    return list(res)


def _mm(name, a, b, ta=False, tb=False, out_dtype=F32, tm=512, tn=1536, tk=1536):
    M, K = (a.shape[1], a.shape[0]) if ta else a.shape
    N = b.shape[0] if tb else b.shape[1]
    assert (b.shape[1] if tb else b.shape[0]) == K, (a.shape, b.shape, ta, tb)
    tm, tn, tk = _pick(M, tm, SUBLANES if not ta else LANES), _pick(N, tn), _pick(K, tk, LANES if (not ta or tb) else SUBLANES)
    nk = K // tk
    dims = (((0 if ta else 1,), (1 if tb else 0,)), ((), ()))

    def body(a_ref, b_ref, o_ref, acc_ref):
        k = pl.program_id(2)
        p = lax.dot_general(a_ref[...].astype(BF16), b_ref[...].astype(BF16), dims, preferred_element_type=F32)
        if nk == 1:
            o_ref[...] = p.astype(o_ref.dtype)
        else:
            @pl.when(k == 0)
            def _():
                acc_ref[...] = p

            @pl.when(k > 0)
            def _():
                acc_ref[...] += p

            @pl.when(k == nk - 1)
            def _():
                o_ref[...] = acc_ref[...].astype(o_ref.dtype)

    a_spec = pl.BlockSpec((tk, tm), lambda i, j, k: (k, i)) if ta else pl.BlockSpec((tm, tk), lambda i, j, k: (i, k))
    b_spec = pl.BlockSpec((tn, tk), lambda i, j, k: (j, k)) if tb else pl.BlockSpec((tk, tn), lambda i, j, k: (k, j))
    return pl.pallas_call(
        body, name=_unique(name), grid=(M // tm, N // tn, nk), in_specs=[a_spec, b_spec],
        out_specs=pl.BlockSpec((tm, tn), lambda i, j, k: (i, j)),
        out_shape=jax.ShapeDtypeStruct((M, N), out_dtype),
        scratch_shapes=[pltpu.VMEM((tm, tn) if nk > 1 else (SUBLANES, LANES), F32)],
        compiler_params=_params(("parallel", "parallel", "arbitrary")))(a, b)


def _silu(x):
    return x * _sigmoid(x)


def _sigmoid(x):
    return 1.0 / (1.0 + jnp.exp(-x))


def _dsilu(x):
    s = _sigmoid(x)
    return s * (1.0 + x * (1.0 - s))


def _gelu(x):
    return 0.5 * x * (1.0 + lax.erf(x * np.float32(math.sqrt(0.5))))


def _dgelu(x):
    cdf = 0.5 * (1.0 + lax.erf(x * np.float32(math.sqrt(0.5))))
    pdf = jnp.exp(-0.5 * x * x) * np.float32(1.0 / math.sqrt(2.0 * math.pi))
    return cdf + x * pdf


def _ln_stats(r):
    mu = jnp.mean(r, axis=-1, keepdims=True)
    xc = r - mu
    var = jnp.mean(xc * xc, axis=-1, keepdims=True)
    rstd = lax.rsqrt(var + LN_EPS)
    return xc * rstd, rstd


def _ln_bwd(dxhat, xhat, rstd):
    m1 = jnp.mean(dxhat, axis=-1, keepdims=True)
    m2 = jnp.mean(dxhat * xhat, axis=-1, keepdims=True)
    return rstd * (dxhat - m1 - xhat * m2)


def _csum(v):
    return jnp.sum(v, axis=0, keepdims=True)


def _split_dot(x, m01, lhs01=False, terms=2):
    acc, rem = None, x
    for _ in range(terms):
        part = rem.astype(BF16)
        rem = rem - part.astype(F32)
        d = jnp.dot(m01, part, preferred_element_type=F32) if lhs01 else jnp.dot(part, m01, preferred_element_type=F32)
        acc = d if acc is None else acc + d
    return acc


def _iota2(shape, dim):
    return lax.broadcasted_iota(jnp.int32, shape, dim)


def _modulate(name, x, sc, sh):
    D = x.shape[2]
    return _rowwise(name, lambda r, bv, cv: ([r[0] * (1.0 + bv[0]) + bv[1]], [], []),
                    [x], [sc, sh], [], out_rows=[(D, BF16)])[0]


def _resid_ln(name, alpha, x, y, g, ln_g, ln_b, ybias=None):
    D = x.shape[2]

    def fn(r, bv, cv):
        yy = r[1] if ybias is None else r[1] + cv[2]
        xhat, _ = _ln_stats(alpha * r[0] + (1.0 + bv[0]) * yy)
        return [xhat * cv[0] + cv[1]], [], []
    cvecs = [ln_g, ln_b] + ([] if ybias is None else [ybias])
    return _rowwise(name, fn, [x, y], [g], cvecs, out_rows=[(D, F32)])[0]


def _resid_ln_bwd(name, alpha, dxn, x, y, g, ln_g, ybias=None):
    D = x.shape[2]

    def fn(r, bv, cv):
        yy = r[2] if ybias is None else r[2] + cv[1]
        xhat, rstd = _ln_stats(alpha * r[1] + (1.0 + bv[0]) * yy)
        dr = _ln_bwd(r[0] * cv[0], xhat, rstd)
        dy = (1.0 + bv[0]) * dr
        return [dr, dy], [_csum(dr * yy)], [_csum(r[0] * xhat), _csum(r[0]), _csum(dy)]
    cvecs = [ln_g] + ([] if ybias is None else [ybias])
    return _rowwise(name, fn, [dxn, x, y], [g], cvecs, out_rows=[(D, F32), (D, BF16)], out_bsums=[D], out_tsums=[D, D, D])


def _modulate_bwd(name, alpha, dh, dr, x, sc):
    D = x.shape[2]

    def fn(r, bv, cv):
        return [alpha * r[1] + r[0] * (1.0 + bv[0])], [_csum(r[0] * r[2]), _csum(r[0])], []
    return _rowwise(name, fn, [dh, dr, x], [sc], [], out_rows=[(D, F32)], out_bsums=[D, D])


def _loss_head(name, y, target):
    D = y.shape[2]

    def fn(r, bv, cv):
        e = r[0] - r[1]
        return [e * np.float32(1.0 / D)], [_csum(e * e)], []
    return _rowwise(name, fn, [y, target], [], [], out_rows=[(D, F32)], out_bsums=[D])


def _swiglu_act(name, z):
    Hd = z.shape[2] // 2
    return _rowwise(name, lambda r, bv, cv: ([_silu(r[0][:, :Hd]) * r[0][:, Hd:]], [], []), [z], out_rows=[(Hd, BF16)])[0]


def _swiglu_act_bwd(name, da, z):
    Hd = z.shape[2] // 2

    def fn(r, bv, cv):
        gg, u = r[1][:, :Hd], r[1][:, Hd:]
        return [jnp.concatenate([r[0] * u * _dsilu(gg), r[0] * _silu(gg)], axis=1)], [], []
    return _rowwise(name, fn, [da, z], out_rows=[(2 * Hd, BF16)])[0]


def _gm_act(name, zin, b_in, ln_g, ln_b):
    W = zin.shape[2] // 2

    def fn(r, bv, cv):
        z = _gelu(r[0] + cv[0])
        vhat, _ = _ln_stats(z[:, W:])
        return [z[:, :W], vhat * cv[1] + cv[2]], [], []
    return _rowwise(name, fn, [zin], [], [b_in, ln_g, ln_b], out_rows=[(W, F32), (W, BF16)])


def _gm_act_bwd(name, zin, du, dvn, b_in, ln_g):
    W = zin.shape[2] // 2

    def fn(r, bv, cv):
        zz = r[0] + cv[0]
        z = _gelu(zz)
        vhat, rstd = _ln_stats(z[:, W:])
        dv = _ln_bwd(r[2] * cv[1], vhat, rstd)
        dzin = jnp.concatenate([r[1], dv], axis=1) * _dgelu(zz)
        return [dzin], [], [_csum(dzin), _csum(r[2] * vhat), _csum(r[2])]
    return _rowwise(name, fn, [zin, du, dvn], [], [b_in, ln_g], out_rows=[(2 * W, BF16)], out_tsums=[2 * W, W, W])


def _gm_causal_w(ws_ref, g):
    T = ws_ref.shape[1]
    return jnp.where(_iota2((T, T), 1) <= _iota2((T, T), 0), ws_ref[g], 0.0).astype(BF16)


def _gm_spatial(name, u, vn, w_s, b_sT):
    B, S, W = u.shape
    G, T = w_s.shape[0], w_s.shape[1]
    assert W == G * T, "a head group is as wide as a chunk is long"

    def body(u_ref, vn_ref, ws_ref, bs_ref, y_ref):
        for g in range(G):
            cs = slice(g * T, (g + 1) * T)
            sv = jnp.dot(_gm_causal_w(ws_ref, g), vn_ref[0, :, cs], preferred_element_type=F32) + bs_ref[:, g:g + 1]
            y_ref[0, :, cs] = (u_ref[0, :, cs] * sv).astype(y_ref.dtype)

    row = pl.BlockSpec((1, T, W), lambda b, i: (b, i, 0))
    return pl.pallas_call(
        body, name=_unique(name), grid=(B, S // T),
        in_specs=[row, row, pl.BlockSpec((G, T, T), lambda b, i: (0, 0, 0)), pl.BlockSpec((T, G), lambda b, i: (0, 0))],
        out_specs=row, out_shape=jax.ShapeDtypeStruct((B, S, W), BF16),
        compiler_params=_params(("parallel", "parallel")))(u, vn, w_s, b_sT)


def _gm_spatial_bwd(name, dyg, u, vn, w_s, b_sT):
    B, S, W = u.shape
    G, T = w_s.shape[0], w_s.shape[1]
    assert W == G * T, "a head group is as wide as a chunk is long"

    def body(dy_ref, u_ref, vn_ref, ws_ref, bs_ref, du_ref, dvn_ref, dws_ref, dbs_ref):
        first = jnp.logical_and(pl.program_id(0) == 0, pl.program_id(1) == 0)

        @pl.when(first)
        def _():
            dws_ref[...] = jnp.zeros_like(dws_ref)
            dbs_ref[...] = jnp.zeros_like(dbs_ref)

        tril = _iota2((T, T), 1) <= _iota2((T, T), 0)
        for g in range(G):
            cs = slice(g * T, (g + 1) * T)
            wm = _gm_causal_w(ws_ref, g)
            vng = vn_ref[0, :, cs]
            sv = jnp.dot(wm, vng, preferred_element_type=F32) + bs_ref[:, g:g + 1]
            dy = dy_ref[0, :, cs]
            du_ref[0, :, cs] = dy * sv
            dsv = dy * u_ref[0, :, cs]
            dsv16 = dsv.astype(BF16)
            dvn_ref[0, :, cs] = lax.dot_general(wm, dsv16, (((0,), (0,)), ((), ())), preferred_element_type=F32)
            dw = lax.dot_general(dsv16, vng, (((1,), (1,)), ((), ())), preferred_element_type=F32)
            dws_ref[g] += jnp.where(tril, dw, 0.0)
            dbs_ref[:, g:g + 1] += jnp.sum(dsv, axis=1, keepdims=True)

    row = pl.BlockSpec((1, T, W), lambda b, i: (b, i, 0))
    return pl.pallas_call(
        body, name=_unique(name), grid=(B, S // T),
        in_specs=[row, row, row, pl.BlockSpec((G, T, T), lambda b, i: (0, 0, 0)), pl.BlockSpec((T, G), lambda b, i: (0, 0))],
        out_specs=[row, row, pl.BlockSpec((G, T, T), lambda b, i: (0, 0, 0)), pl.BlockSpec((T, G), lambda b, i: (0, 0))],
        out_shape=[jax.ShapeDtypeStruct((B, S, W), F32), jax.ShapeDtypeStruct((B, S, W), F32),
                   jax.ShapeDtypeStruct((G, T, T), F32), jax.ShapeDtypeStruct((T, G), F32)],
        compiler_params=_params(("arbitrary", "arbitrary")))(dyg, u, vn, w_s, b_sT)


def _cv_glu(name, pw, b_in):
    W = pw.shape[2] // 2

    def fn(r, bv, cv):
        z = r[0] + cv[0]
        return [z[:, :W] * _sigmoid(z[:, W:])], [], []
    return _rowwise(name, fn, [pw], [], [b_in], out_rows=[(W, F32)])[0]


def _cv_glu_bwd(name, pw, dyg, b_in):
    W = pw.shape[2] // 2

    def fn(r, bv, cv):
        z = r[0] + cv[0]
        a, s = z[:, :W], _sigmoid(z[:, W:])
        dpw = jnp.concatenate([r[1] * s, r[1] * a * s * (1.0 - s)], axis=1)
        return [dpw], [], [_csum(dpw)]
    return _rowwise(name, fn, [pw, dyg], [], [b_in], out_rows=[(2 * W, BF16)], out_tsums=[2 * W])


def _cv_ln_act(name, yc, ln_g, ln_b):
    D = yc.shape[2]

    def fn(r, bv, cv):
        xhat, _ = _ln_stats(r[0])
        return [_silu(xhat * cv[0] + cv[1])], [], []
    return _rowwise(name, fn, [yc], [], [ln_g, ln_b], out_rows=[(D, BF16)])[0]


def _cv_ln_act_bwd(name, yc, dys, ln_g, ln_b):
    D = yc.shape[2]

    def fn(r, bv, cv):
        xhat, rstd = _ln_stats(r[0])
        dyn = r[1] * _dsilu(xhat * cv[0] + cv[1])
        dyc = _ln_bwd(dyn * cv[0], xhat, rstd)
        return [dyc], [], [_csum(dyn * xhat), _csum(dyn), _csum(dyc)]
    return _rowwise(name, fn, [yc, dys], [], [ln_g, ln_b], out_rows=[(D, F32)], out_tsums=[D, D, D])


CONV_HALO = 32
CONV_TS, CONV_TC = 256, 128


def _dwconv(name, y, dw, dw_b):
    B, S, D = y.shape
    ts, tc, halo, K = _pick(S, CONV_TS, SUBLANES), _pick(D, CONV_TC), CONV_HALO, CONV_WIDTH

    def body(cur_ref, prev_ref, dw_ref, b_ref, o_ref, buf):
        i = pl.program_id(1)
        buf[pl.ds(0, halo), :] = jnp.where(i > 0, prev_ref[0, pl.ds(ts - halo, halo), :], 0.0)
        buf[pl.ds(halo, ts), :] = cur_ref[0]
        acc = jnp.zeros((ts, tc), F32) + b_ref[...]
        for k in range(K):
            acc = acc + dw_ref[k:k + 1, :] * buf[pl.ds(halo - (K - 1) + k, ts), :]
        o_ref[0] = acc

    return pl.pallas_call(
        body, name=_unique(name), grid=(B, S // ts, D // tc),
        in_specs=[pl.BlockSpec((1, ts, tc), lambda b, i, j: (b, i, j)),
                  pl.BlockSpec((1, ts, tc), lambda b, i, j: (b, jnp.maximum(i - 1, 0), j)),
                  pl.BlockSpec((halo, tc), lambda b, i, j: (0, j)), pl.BlockSpec((1, tc), lambda b, i, j: (0, j))],
        out_specs=pl.BlockSpec((1, ts, tc), lambda b, i, j: (b, i, j)),
        out_shape=jax.ShapeDtypeStruct((B, S, D), F32),
        scratch_shapes=[pltpu.VMEM((halo + ts, tc), F32)],
        compiler_params=_params(("parallel", "parallel", "parallel")))(y, y, dw, dw_b)


def _dwconv_bwd(name, dyc, y, dw):
    B, S, D = y.shape
    ts, tc, halo, K = _pick(S, CONV_TS, SUBLANES), _pick(D, CONV_TC), CONV_HALO, CONV_WIDTH
    nt = S // ts

    def body(g_ref, gnext_ref, y_ref, yprev_ref, dw_ref, dy_ref, ddw_ref, gbuf, ybuf):
        b, i = pl.program_id(1), pl.program_id(2)
        first = jnp.logical_and(b == 0, i == 0)

        @pl.when(first)
        def _():
            ddw_ref[...] = jnp.zeros_like(ddw_ref)

        g = g_ref[0]
        gbuf[pl.ds(0, ts), :] = g
        gbuf[pl.ds(ts, halo), :] = jnp.where(i < nt - 1, gnext_ref[0, pl.ds(0, halo), :], 0.0)
        ybuf[pl.ds(0, halo), :] = jnp.where(i > 0, yprev_ref[0, pl.ds(ts - halo, halo), :], 0.0)
        ybuf[pl.ds(halo, ts), :] = y_ref[0]
        acc = jnp.zeros((ts, tc), F32)
        for k in range(K):
            acc = acc + dw_ref[k:k + 1, :] * gbuf[pl.ds(K - 1 - k, ts), :]
            ddw_ref[k:k + 1, :] += _csum(g * ybuf[pl.ds(halo - (K - 1) + k, ts), :])
        dy_ref[0] = acc

    tile = lambda f: pl.BlockSpec((1, ts, tc), f)
    return pl.pallas_call(
        body, name=_unique(name), grid=(D // tc, B, nt),
        in_specs=[tile(lambda j, b, i: (b, i, j)), tile(lambda j, b, i: (b, jnp.minimum(i + 1, nt - 1), j)),
                  tile(lambda j, b, i: (b, i, j)), tile(lambda j, b, i: (b, jnp.maximum(i - 1, 0), j)),
                  pl.BlockSpec((halo, tc), lambda j, b, i: (0, j))],
        out_specs=[tile(lambda j, b, i: (b, i, j)), pl.BlockSpec((halo, tc), lambda j, b, i: (0, j))],
        out_shape=[jax.ShapeDtypeStruct((B, S, D), F32), jax.ShapeDtypeStruct((halo, D), F32)],
        scratch_shapes=[pltpu.VMEM((ts + halo, tc), F32), pltpu.VMEM((halo + ts, tc), F32)],
        compiler_params=_params(("parallel", "arbitrary", "arbitrary")))(dyc, dyc, y, y, dw)


ATT_BLOCK = 128
ATT_QUERY_BLOCK = 256
ATT_KEY_BLOCK = 512
ATT_PIECE_ROWS = 32
FOX_GATE_COLS = 128


def _att_tiles(S):
    return _pick(S, ATT_QUERY_BLOCK, SUBLANES), _pick(S, ATT_KEY_BLOCK, LANES)


def _pieces(T, TK):
    R = min(T, ATT_PIECE_ROWS)
    segs = [slice(c, c + LANES) for c in range(0, TK, LANES)]
    return [(slice(r, r + R), segs) for r in range(0, T, R)]


def _piece_keep(row0, col0, rs, cs, strict, lane_major_of=0):
    shape = (rs.stop - rs.start, cs.stop - cs.start)
    lane = _iota2(shape, 1)
    key = col0 + (lane * lane_major_of + cs.start // LANES if lane_major_of else cs.start + lane)
    qry = row0 + rs.start + _iota2(shape, 0)
    return key < qry if strict else key <= qry


def _causal_tiles(i, tq, tk):
    return (i * tq + tq + tk - 1) // tk


def _lane_major(t, tk):
    B, H, S, dh = t.shape
    return t.reshape(B, H, S // tk, LANES, tk // LANES, dh).swapaxes(3, 4).reshape(B, H, S, dh)


def _lane_major_inverse(t, tk):
    B, H, S, dh = t.shape
    return t.reshape(B, H, S // tk, tk // LANES, LANES, dh).swapaxes(3, 4).reshape(B, H, S, dh)


def _log_sigmoid(x):
    return jnp.minimum(x, 0.0) - jnp.log(1.0 + jnp.exp(-jnp.abs(x)))


def _fox_gate_cumsum(name, fl, b_f):
    B, S, C = fl.shape
    T = _pick(S, ATT_BLOCK, SUBLANES)

    def body(fl_ref, bf_ref, f_ref, carry):
        @pl.when(pl.program_id(1) == 0)
        def _():
            carry[...] = jnp.zeros_like(carry)
        lf = _log_sigmoid(fl_ref[0] + bf_ref[...])
        lower = (_iota2((T, T), 1) <= _iota2((T, T), 0)).astype(BF16)
        f = _split_dot(lf, lower, lhs01=True, terms=3) + carry[...]
        f_ref[0] = f
        carry[...] = f[T - 1:T, :]

    return pl.pallas_call(
        body, name=_unique(name), grid=(B, S // T),
        in_specs=[pl.BlockSpec((1, T, C), lambda b, i: (b, i, 0)), pl.BlockSpec((1, C), lambda b, i: (0, 0))],
        out_specs=pl.BlockSpec((1, T, C), lambda b, i: (b, i, 0)),
        out_shape=jax.ShapeDtypeStruct((B, S, C), F32),
        scratch_shapes=[pltpu.VMEM((1, C), F32)],
        compiler_params=_params(("arbitrary", "arbitrary")))(fl, b_f)


def _fox_gate_bwd(name, dF, fl, b_f, n_heads):
    B, S, C = fl.shape
    T = _pick(S, ATT_BLOCK, SUBLANES)
    nt = S // T

    def body(df_ref, fl_ref, bf_ref, dfl_ref, dbf_ref, carry):
        first = jnp.logical_and(pl.program_id(0) == 0, pl.program_id(1) == 0)

        @pl.when(pl.program_id(1) == 0)
        def _():
            carry[...] = jnp.zeros_like(carry)

        @pl.when(first)
        def _():
            dbf_ref[...] = jnp.zeros_like(dbf_ref)

        upper = (_iota2((T, T), 1) >= _iota2((T, T), 0)).astype(BF16)
        dlf = _split_dot(df_ref[0], upper, lhs01=True, terms=3) + carry[...]
        carry[...] = dlf[0:1, :]
        x = fl_ref[0] + bf_ref[...]
        dfl = jnp.where(_iota2((T, C), 1) < n_heads, dlf * _sigmoid(-x), 0.0)
        dfl_ref[0] = dfl
        dbf_ref[...] += _csum(dfl)

    rev = lambda b, i: (b, nt - 1 - i, 0)
    return pl.pallas_call(
        body, name=_unique(name), grid=(B, nt),
        in_specs=[pl.BlockSpec((1, T, C), rev), pl.BlockSpec((1, T, C), rev), pl.BlockSpec((1, C), lambda b, i: (0, 0))],
        out_specs=[pl.BlockSpec((1, T, C), rev), pl.BlockSpec((1, C), lambda b, i: (0, 0))],
        out_shape=[jax.ShapeDtypeStruct((B, S, C), F32), jax.ShapeDtypeStruct((1, C), F32)],
        scratch_shapes=[pltpu.VMEM((1, C), F32)],
        compiler_params=_params(("arbitrary", "arbitrary")))(dF, fl, b_f)


_NT = (((1,), (1,)), ((), ()))
_TN = (((0,), (0,)), ((), ()))


def _fox_fwd(name, q, k, v, fq, fk):
    B, H, S, dh = q.shape
    T, TK = _att_tiles(S)
    scale = np.float32(dh ** -0.5)

    pieces = _pieces(T, TK)

    def body(q_ref, k_ref, v_ref, fq_ref, fk_ref, o_ref, lse_ref, s_scr, p_scr):
        i = pl.program_id(2)
        qb = q_ref[0, 0]
        n_tiles = _causal_tiles(i, T, TK)

        def tile(j, carry, diag):
            m, l, acc = carry
            ks = pl.ds(pl.multiple_of(j * TK, TK), TK)
            s_scr[...] = lax.dot_general(qb, k_ref[0, 0, ks, :], _NT, preferred_element_type=F32) * scale
            fkj = fk_ref[0, 0, pl.ds(j, 1), :]
            m_new = []
            for rc, (rs, segs) in enumerate(pieces):
                fq_c, mx = fq_ref[0, 0, rs, :], m[rc]
                for cs in segs:
                    s = s_scr[rs, cs] + fq_c - fkj[:, cs]
                    if diag:
                        s = jnp.where(_piece_keep(i * T, j * TK, rs, cs, False), s, NEG_INF)
                    s_scr[rs, cs] = s
                    mx = jnp.maximum(mx, jnp.max(s, axis=1, keepdims=True))
                m_new.append(mx)
            alpha, l_new = [], []
            for rc, (rs, segs) in enumerate(pieces):
                alpha.append(jnp.exp(m[rc] - m_new[rc]))
                lsum = alpha[rc] * l[rc]
                for cs in segs:
                    p = jnp.exp(s_scr[rs, cs] - m_new[rc])
                    p_scr[rs, cs] = p.astype(BF16)
                    lsum = lsum + jnp.sum(p, axis=1, keepdims=True)
                l_new.append(lsum)
            acc = jnp.concatenate(alpha, axis=0) * acc + jnp.dot(p_scr[...], v_ref[0, 0, ks, :], preferred_element_type=F32)
            return tuple(m_new), tuple(l_new), acc

        init = (tuple(jnp.full((rs.stop - rs.start, 1), NEG_INF, F32) for rs, _ in pieces),
                tuple(jnp.zeros((rs.stop - rs.start, 1), F32) for rs, _ in pieces), jnp.zeros((T, dh), F32))
        carry = lax.fori_loop(0, n_tiles - 1, lambda j, c: tile(j, c, False), init)
        m, l, acc = tile(n_tiles - 1, carry, True)
        m, l = jnp.concatenate(m, axis=0), jnp.concatenate(l, axis=0)
        o_ref[0, 0] = acc / l
        lse_ref[0, 0] = m + jnp.log(l)

    full = lambda w: pl.BlockSpec((1, 1, S, w), lambda b, h, i: (b, h, 0, 0))
    blk = lambda w: pl.BlockSpec((1, 1, T, w), lambda b, h, i: (b, h, i, 0))
    return pl.pallas_call(
        body, name=_unique(name), grid=(B, H, S // T),
        in_specs=[blk(dh), full(dh), full(dh), blk(1), pl.BlockSpec((1, 1, S // TK, TK), lambda b, h, i: (b, h, 0, 0))],
        out_specs=[blk(dh), blk(1)],
        out_shape=[jax.ShapeDtypeStruct((B, H, S, dh), F32), jax.ShapeDtypeStruct((B, H, S, 1), F32)],
        scratch_shapes=[pltpu.VMEM((T, TK), F32), pltpu.VMEM((T, TK), BF16)],
        compiler_params=_params(("parallel", "parallel", "parallel")))(q, k, v, fq, fk)


def _fox_bwd(name, q, k, v, fq, fk, do, lse):
    B, H, S, dh = q.shape
    T, TK = _att_tiles(S)
    nt, nkt = S // T, S // TK
    scale = np.float32(dh ** -0.5)

    pieces = _pieces(T, TK)

    def body(q_ref, k_ref, v_ref, fq_ref, fk_ref, do_ref, lse_ref, dq_ref, dk_ref, dv_ref, dfk_ref,
             p_buf, dp_buf, s_scr, ds_scr, p16_scr):
        dk_ref[...] = jnp.zeros_like(dk_ref)
        dv_ref[...] = jnp.zeros_like(dv_ref)
        dfk_ref[...] = jnp.zeros_like(dfk_ref)

        def qloop(i, _):
            qs = pl.ds(pl.multiple_of(i * T, T), T)
            qb, dob16 = q_ref[0, 0, qs, :], do_ref[0, 0, qs, :].astype(BF16)
            n_tiles = _causal_tiles(i, T, TK)
            row_at = lambda rs: pl.ds(pl.multiple_of(i * T + rs.start, SUBLANES), rs.stop - rs.start)
            fq_c = [fq_ref[0, 0, row_at(rs), :] for rs, _ in pieces]
            lse_c = [lse_ref[0, 0, row_at(rs), :] for rs, _ in pieces]

            def sweep1(j, delta, diag):
                ks = pl.ds(pl.multiple_of(j * TK, TK), TK)
                s_scr[...] = lax.dot_general(qb, k_ref[0, 0, ks, :], _NT, preferred_element_type=F32) * scale
                dp_buf[j] = lax.dot_general(dob16, v_ref[0, 0, ks, :], _NT, preferred_element_type=F32)
                fkj = fk_ref[0, 0, pl.ds(j, 1), :]
                out = []
                for rc, (rs, segs) in enumerate(pieces):
                    d = delta[rc]
                    for cs in segs:
                        p = jnp.exp(((s_scr[rs, cs] + fq_c[rc]) - fkj[:, cs]) - lse_c[rc])
                        if diag:
                            p = jnp.where(_piece_keep(i * T, j * TK, rs, cs, False), p, 0.0)
                        p_buf[j, rs, cs] = p
                        d = d + jnp.sum(p * dp_buf[j, rs, cs], axis=1, keepdims=True)
                    out.append(d)
                return tuple(out)

            zeros = tuple(jnp.zeros((rs.stop - rs.start, 1), F32) for rs, _ in pieces)
            delta = lax.fori_loop(0, n_tiles - 1, lambda j, d: sweep1(j, d, False), zeros)
            delta = sweep1(n_tiles - 1, delta, True)

            def sweep2(j, dq):
                ks = pl.ds(pl.multiple_of(j * TK, TK), TK)
                col = [None] * len(pieces[0][1])
                for rc, (rs, segs) in enumerate(pieces):
                    for sg, cs in enumerate(segs):
                        p = p_buf[j, rs, cs]
                        ds = p * (dp_buf[j, rs, cs] - delta[rc])
                        ds_scr[rs, cs] = ds.astype(BF16)
                        p16_scr[rs, cs] = p.astype(BF16)
                        col[sg] = _csum(ds) if col[sg] is None else col[sg] + _csum(ds)
                dfk_ref[0, 0, pl.ds(j, 1), :] -= jnp.concatenate(col, axis=1)
                ds16 = ds_scr[...]
                dk_ref[0, 0, ks, :] += lax.dot_general(ds16, qb, _TN, preferred_element_type=F32)
                dv_ref[0, 0, ks, :] += lax.dot_general(p16_scr[...], dob16, _TN, preferred_element_type=F32)
                return dq + jnp.dot(ds16, k_ref[0, 0, ks, :], preferred_element_type=F32)

            dq_ref[0, 0, qs, :] = lax.fori_loop(0, n_tiles, sweep2, jnp.zeros((T, dh), F32)) * scale
            return 0

        lax.fori_loop(0, nt, qloop, 0)
        dk_ref[...] = dk_ref[...] * scale

    full = lambda w: pl.BlockSpec((1, 1, S, w), lambda b, h: (b, h, 0, 0))
    fks = pl.BlockSpec((1, 1, nkt, TK), lambda b, h: (b, h, 0, 0))
    return pl.pallas_call(
        body, name=_unique(name), grid=(B, H),
        in_specs=[full(dh), full(dh), full(dh), full(1), fks, full(dh), full(1)],
        out_specs=[full(dh), full(dh), full(dh), fks],
        out_shape=[jax.ShapeDtypeStruct((B, H, S, dh), F32)] * 3 + [jax.ShapeDtypeStruct((B, H, nkt, TK), F32)],
        scratch_shapes=[pltpu.VMEM((nkt, T, TK), F32), pltpu.VMEM((nkt, T, TK), F32), pltpu.VMEM((T, TK), F32),
                        pltpu.VMEM((T, TK), BF16), pltpu.VMEM((T, TK), BF16)],
        compiler_params=_params(("parallel", "parallel")))(q, k, v, fq, fk, do, lse)


def _sb_terms(z, with_sigmoids=True):
    t = jnp.exp(-jnp.abs(z))
    lp = jnp.log(1.0 + t)
    lb, l1 = jnp.minimum(z, 0.0) - lp, jnp.minimum(-z, 0.0) - lp
    if not with_sigmoids:
        return lb, l1, None, None
    inv = 1.0 / (1.0 + t)
    pos = z >= 0.0
    return lb, l1, jnp.where(pos, 1.0, t) * inv, jnp.where(pos, t, 1.0) * inv


def _lane_scan(x, reverse):
    lane = _iota2(x.shape, 1)
    y, d = x, 1
    while d < LANES:
        if reverse:
            y = y + jnp.where(lane + d < LANES, pltpu.roll(y, LANES - d, 1), 0.0)
        else:
            y = y + jnp.where(lane >= d, pltpu.roll(y, d, 1), 0.0)
        d *= 2
    return y


def _chunk_scan(xs, reverse):
    n = len(xs)
    within, acc = [None] * n, None
    for s in (range(n - 1, -1, -1) if reverse else range(n)):
        acc = xs[s] if acc is None else acc + xs[s]
        within[s] = acc
    lanes = _lane_scan(acc, reverse)
    beyond = lanes - acc
    return [w + beyond for w in within], (lanes[:, 0:1] if reverse else lanes[:, LANES - 1:LANES])


def _sb_fwd(name, q, k, v):
    B, H, S, dh = q.shape
    T, TK = _att_tiles(S)
    scale = np.float32(dh ** -0.5)
    pieces = _pieces(T, TK)

    def body(q_ref, k_ref, v_ref, o_ref, lt_ref, z_scr, a_scr):
        i = pl.program_id(2)
        qb = q_ref[0, 0]
        n_tiles = _causal_tiles(i, T, TK)

        def tile(j, carry, diag):
            runs, acc = carry
            ks = pl.ds(pl.multiple_of(j * TK, TK), TK)
            z_scr[...] = lax.dot_general(qb, k_ref[0, 0, ks, :], _NT, preferred_element_type=F32) * scale
            new_runs = []
            for rc, (rs, segs) in enumerate(pieces):
                terms = [_sb_terms(z_scr[rs, cs], False) for cs in segs]
                keep = [_piece_keep(i * T, j * TK, rs, cs, True, len(segs)) if diag else None for cs in segs]
                l1 = [jnp.where(kp, t[1], 0.0) if diag else t[1] for kp, t in zip(keep, terms)]
                right_of, total = _chunk_scan(l1, True)
                for cs, kp, t, x, r in zip(segs, keep, terms, l1, right_of):
                    a = jnp.exp(t[0] + ((r - x) + runs[rc]))
                    a_scr[rs, cs] = (jnp.where(kp, a, 0.0) if diag else a).astype(BF16)
                new_runs.append(runs[rc] + total)
            return tuple(new_runs), acc + jnp.dot(a_scr[...], v_ref[0, 0, ks, :], preferred_element_type=F32)

        init = (tuple(jnp.zeros((rs.stop - rs.start, 1), F32) for rs, _ in pieces), jnp.zeros((T, dh), F32))
        carry = tile(n_tiles - 1, init, True)
        runs, acc = lax.fori_loop(1, n_tiles, lambda jj, c: tile(n_tiles - 1 - jj, c, False), carry)
        o_ref[0, 0] = acc
        lt_ref[0, 0] = jnp.concatenate(runs, axis=0)

    full = lambda w: pl.BlockSpec((1, 1, S, w), lambda b, h, i: (b, h, 0, 0))
    blk = lambda w: pl.BlockSpec((1, 1, T, w), lambda b, h, i: (b, h, i, 0))
    return pl.pallas_call(
        body, name=_unique(name), grid=(B, H, S // T),
        in_specs=[blk(dh), full(dh), full(dh)], out_specs=[blk(dh), blk(1)],
        out_shape=[jax.ShapeDtypeStruct((B, H, S, dh), F32), jax.ShapeDtypeStruct((B, H, S, 1), F32)],
        scratch_shapes=[pltpu.VMEM((T, TK), F32), pltpu.VMEM((T, TK), BF16)],
        compiler_params=_params(("parallel", "parallel", "parallel")))(q, k, v)


def _sb_bwd(name, q, k, v, do, lt):
    B, H, S, dh = q.shape
    T, TK = _att_tiles(S)
    nt = S // T
    scale = np.float32(dh ** -0.5)

    pieces = _pieces(T, TK)

    def body(q_ref, k_ref, v_ref, do_ref, lt_ref, dq_ref, dk_ref, dv_ref, z_scr, da_scr, dz_scr, a_scr):
        dk_ref[...] = jnp.zeros_like(dk_ref)
        dv_ref[...] = jnp.zeros_like(dv_ref)

        def qloop(i, _):
            qs = pl.ds(pl.multiple_of(i * T, T), T)
            qb, dob16 = q_ref[0, 0, qs, :], do_ref[0, 0, qs, :].astype(BF16)
            n_tiles = _causal_tiles(i, T, TK)
            lt_c = [lt_ref[0, 0, pl.ds(pl.multiple_of(i * T + rs.start, SUBLANES), rs.stop - rs.start), :] for rs, _ in pieces]

            def tile(j, carry, diag):
                sums_l, sums_e, dq = carry
                ks = pl.ds(pl.multiple_of(j * TK, TK), TK)
                kb, vb = k_ref[0, 0, ks, :], v_ref[0, 0, ks, :]
                z_scr[...] = lax.dot_general(qb, kb, _NT, preferred_element_type=F32) * scale
                da_scr[...] = lax.dot_general(dob16, vb, _NT, preferred_element_type=F32)
                new_l, new_e = [], []
                for rc, (rs, segs) in enumerate(pieces):
                    terms = [_sb_terms(z_scr[rs, cs]) for cs in segs]
                    keep = [_piece_keep(i * T, j * TK, rs, cs, True, len(segs)) if diag else None for cs in segs]
                    l1 = [jnp.where(kp, t[1], 0.0) if diag else t[1] for kp, t in zip(keep, terms)]
                    upto, total_l = _chunk_scan(l1, False)
                    es = []
                    for cs, kp, t, u in zip(segs, keep, terms, upto):
                        a = jnp.exp(t[0] + (lt_c[rc] - (u + sums_l[rc])))
                        if diag:
                            a = jnp.where(kp, a, 0.0)
                        a_scr[rs, cs] = a.astype(BF16)
                        es.append(da_scr[rs, cs] * a)
                    e_upto, total_e = _chunk_scan(es, False)
                    for cs, kp, t, e, eu in zip(segs, keep, terms, es, e_upto):
                        dz = e * t[3] - ((eu - e) + sums_e[rc]) * t[2]
                        dz_scr[rs, cs] = (jnp.where(kp, dz, 0.0) if diag else dz).astype(BF16)
                    new_l.append(sums_l[rc] + total_l)
                    new_e.append(sums_e[rc] + total_e)
                dz16 = dz_scr[...]
                dk_ref[0, 0, ks, :] += lax.dot_general(dz16, qb, _TN, preferred_element_type=F32)
                dv_ref[0, 0, ks, :] += lax.dot_general(a_scr[...], dob16, _TN, preferred_element_type=F32)
                return tuple(new_l), tuple(new_e), dq + jnp.dot(dz16, kb, preferred_element_type=F32)

            zeros = tuple(jnp.zeros((rs.stop - rs.start, 1), F32) for rs, _ in pieces)
            carry = lax.fori_loop(0, n_tiles - 1, lambda j, c: tile(j, c, False), (zeros, zeros, jnp.zeros((T, dh), F32)))
            dq_ref[0, 0, qs, :] = tile(n_tiles - 1, carry, True)[2] * scale
            return 0

        lax.fori_loop(0, nt, qloop, 0)
        dk_ref[...] = dk_ref[...] * scale

    full = lambda w: pl.BlockSpec((1, 1, S, w), lambda b, h: (b, h, 0, 0))
    return pl.pallas_call(
        body, name=_unique(name), grid=(B, H),
        in_specs=[full(dh), full(dh), full(dh), full(dh), full(1)], out_specs=[full(dh)] * 3,
        out_shape=[jax.ShapeDtypeStruct((B, H, S, dh), F32)] * 3,
        scratch_shapes=[pltpu.VMEM((T, TK), F32), pltpu.VMEM((T, TK), F32), pltpu.VMEM((T, TK), BF16), pltpu.VMEM((T, TK), BF16)],
        compiler_params=_params(("parallel", "parallel")))(q, k, v, do, lt)


def _adamw(name, w, g, m, v):
    shape = w.shape
    cols = shape[-1] if (w.ndim >= 2 and shape[-1] % LANES == 0) else 0
    n = w.size
    if cols:
        prep = lambda t: t.reshape(1, n // cols, cols)
    else:
        cols = LANES
        pad = (-n) % (SUBLANES * LANES)
        prep = lambda t: jnp.pad(t.reshape(-1), (0, pad), constant_values=1.0).reshape(1, (n + pad) // cols, cols)

    def fn(r, bv, cv):
        w_, g_, m_, v_ = r
        m2 = ADAM_B1 * m_ + (1.0 - ADAM_B1) * g_
        v2 = ADAM_B2 * v_ + (1.0 - ADAM_B2) * (g_ * g_)
        m_hat = m2 / (1.0 - ADAM_B1 ** ADAM_STEP)
        v_hat = v2 / (1.0 - ADAM_B2 ** ADAM_STEP)
        return [-ADAM_LR * (m_hat / (jnp.sqrt(v_hat) + ADAM_EPS) + ADAM_WD * w_), m2, v2], [], []
    outs = _rowwise(name, fn, [prep(w), prep(g), prep(m), prep(v)], out_rows=[(cols, F32)] * 3, tm=512)
    return [o.reshape(-1)[:n].reshape(shape) for o in outs]


def _sum8(name, parts):
    def fn(r, bv, cv):
        s = r[0]
        for t in r[1:]:
            s = s + t
        return [s], [], []
    rows = [parts[i][None] for i in range(parts.shape[0])]
    return _rowwise(name, fn, rows, out_rows=[(parts.shape[2], F32)])[0][0]


def _pack(arrs, cols, dtype, row_mult):
    flat = jnp.concatenate([a.reshape(-1).astype(dtype) for a in arrs])
    pad = (-flat.size) % (cols * row_mult)
    return jnp.pad(flat, (0, pad)).reshape(-1, cols)


def _unpack(flat, shapes):
    out, off = [], 0
    for s in shapes:
        n = int(np.prod(s))
        out.append(flat[off:off + n].reshape(s))
        off += n
    return out


def _heads(t, H):
    B, S, W = t.shape
    return t.reshape(B, S, H, W // H).transpose(0, 2, 1, 3)


def _unheads(t):
    B, H, S, dh = t.shape
    return t.transpose(0, 2, 1, 3).reshape(B, S, H * dh)


def kernel(*args):
    _names_used.clear()
    p = dict(zip(ARGS, args))
    x, target = p['x'], p['loss_target']
    B, S, D = x.shape
    T = B * S
    depth = p['ln1_g'].shape[0]
    H = D // HEAD_DIM
    alpha = np.float32((2.0 * depth) ** 0.25)
    cx, cy, cc = _mesh_pos()
    my_q = 2 * cx + cy
    axes = ("x", "y", "c")
    two = lambda t: t.reshape(T, t.shape[-1])
    three = lambda t: t.reshape(B, S, t.shape[-1])

    small_in = [p['c']] + [p[n] for n in SMALL_SPLIT]
    g1 = _all_gather8("ag_small", [_pack(small_in, LANES, F32, SUBLANES)])[0]
    g1 = g1.reshape(N_DEV, -1)
    c_all = g1[:, :B * D].reshape(N_DEV * B, D)
    per_chip = [_unpack(g1[2 * q], [a.shape for a in small_in])[1:] for q in range(N_CHIPS)]
    small = {n: jnp.concatenate([per_chip[q][i] for q in range(N_CHIPS)], axis=-1) for i, n in enumerate(SMALL_SPLIT)}
    for n in SMALL_REPL:
        small[n] = p[n]

    n_seq = N_DEV * B
    seq_pad = -(-n_seq // LANES) * LANES
    c_act = _rowwise("c_act", lambda r, bv, cv: ([_silu(r[0])], [], []),
                     [jnp.pad(c_all, ((0, seq_pad - n_seq), (0, 0)))[None]], out_rows=[(D, F32)])[0][0]
    mod_cols = p['mod_w'].shape[2]
    mod_part = jnp.stack([_mm(f"mod_fwd{l}", c_act, p['mod_w'][l])[:n_seq] for l in range(depth)])
    half_layers = depth // 2
    mod_half = lax.dynamic_slice_in_dim(mod_part, cc * half_layers, half_layers, axis=0)
    gm_ = _all_gather8("ag_mod", [mod_half.reshape(half_layers * n_seq, mod_cols)])[0]
    mod_all = gm_.reshape(N_CHIPS, 2, half_layers, n_seq, mod_cols).transpose(1, 2, 3, 0, 4).reshape(depth, n_seq, 6 * D)
    mod_mine = lax.dynamic_slice_in_dim(mod_all, (2 * my_q + cc) * B, B, axis=1)
    mod = _rowwise("mod_bias", lambda r, bv, cv: ([r[0] + bv[0]], [], []), [mod_mine], [p['mod_b'][:, None, :]],
                   out_rows=[(6 * D, F32)])[0]
    mods = [[mod[l, :, None, i * D:(i + 1) * D] for i in range(6)] for l in range(depth)]

    big_names = list(BIG)
    shard_shapes = [p[n].shape for n in big_names]
    half_rows = [s[0] * s[1] // 2 for s in shard_shapes]
    w_halves = [lax.dynamic_slice_in_dim(p[n].reshape(-1, s[2]), cc * hr, hr, axis=0).astype(BF16)
                for n, s, hr in zip(big_names, shard_shapes, half_rows)]
    W = {}
    for n, s, g in zip(big_names, shard_shapes, _all_gather8("ag_weights", w_halves)):
        seg = g.reshape((N_CHIPS,) + s)
        W[n] = jnp.concatenate([seg[q] for q in range(N_CHIPS)], axis=BIG[n])

    def vec(n, j):
        return small[n][j][None, :]

    def attn_proj(h1, w_in, gate_cols):
        wp = jnp.pad(w_in, ((0, 0), (0, gate_cols))) if gate_cols else w_in
        proj = three(_mm("att_proj", two(h1), wp))
        q, k, v = [_heads(proj[..., i * D:(i + 1) * D].astype(BF16), H) for i in range(3)]
        return wp, proj, q, k, v

    def gm_fwd(j, h1):
        zin = three(_mm("gm_in", two(h1), W['gm_w_in'][j]))
        u, vn = _gm_act("gm_act", zin, vec('gm_b_in', j), vec('gm_ln_g', j), vec('gm_ln_b', j))
        b_sT = small['gm_b_s'][j].T
        yg = _gm_spatial("gm_spatial", u, vn, small['gm_w_s'][j], b_sT)
        return three(_mm("gm_out", two(yg), W['gm_w_out'][j])), (zin, u, vn, b_sT, yg)

    def gm_bwd(j, h1, dy1, cache):
        zin, u, vn, b_sT, yg = cache
        g = {'gm_w_out': _mm("gm_dwout", two(yg), two(dy1), ta=True)}
        dyg = three(_mm("gm_dyg", two(dy1), W['gm_w_out'][j], tb=True))
        du, dvn, dws, dbsT = _gm_spatial_bwd("gm_spatial_bwd", dyg, u, vn, small['gm_w_s'][j], b_sT)
        dzin, g['gm_b_in'], g['gm_ln_g'], g['gm_ln_b'] = _gm_act_bwd("gm_act_bwd", zin, du, dvn, vec('gm_b_in', j), vec('gm_ln_g', j))
        g['gm_w_s'], g['gm_b_s'] = dws, dbsT.T
        g['gm_w_in'] = _mm("gm_dwin", two(h1), two(dzin), ta=True)
        return _mm("gm_dh", two(dzin), W['gm_w_in'][j], tb=True), g

    def fox_fwd(j, h1):
        wp, proj, q, k, v = attn_proj(h1, W['fox_w_in'][j], 3 * D + FOX_GATE_COLS - W['fox_w_in'].shape[2])
        fl = proj[..., 3 * D:]
        bf = jnp.pad(small['fox_b_f'][j][None, :], ((0, 0), (0, FOX_GATE_COLS - H)))
        Fh = _fox_gate_cumsum("fox_gate", fl, bf)[..., :H].transpose(0, 2, 1)
        fq, fk = Fh[..., None], Fh.reshape(B, H, -1, _att_tiles(S)[1])
        o, lse = _fox_fwd("fox_fwd", q, k, v, fq, fk)
        o2 = _unheads(o)
        return three(_mm("fox_out", two(o2), W['fox_w_out'][j])), (wp, q, k, v, fl, bf, fq, fk, lse, o2)

    def fox_bwd(j, h1, dy1, cache):
        wp, q, k, v, fl, bf, fq, fk, lse, o2 = cache
        g = {'fox_w_out': _mm("fox_dwout", two(o2), two(dy1), ta=True)}
        do = _heads(three(_mm("fox_do", two(dy1), W['fox_w_out'][j], tb=True)), H)
        dq, dk, dv, dfk = _fox_bwd("fox_bwd", q, k, v, fq, fk, do, lse)
        dF = jnp.pad(dfk.reshape(B, H, S).transpose(0, 2, 1), ((0, 0), (0, 0), (0, FOX_GATE_COLS - H)))
        dfl, dbf = _fox_gate_bwd("fox_gate_bwd", dF, fl, bf, H)
        dproj = jnp.concatenate([_unheads(dq).astype(BF16), _unheads(dk).astype(BF16), _unheads(dv).astype(BF16),
                                 dfl.astype(BF16)], axis=-1)
        g['fox_w_in'] = _mm("fox_dwin", two(h1), two(dproj), ta=True)[:, :W['fox_w_in'].shape[2]]
        g['fox_b_f'] = dbf[0, :H]
        return _mm("fox_dh", two(dproj), wp, tb=True), g

    def sb_fwd(j, h1):
        wp, proj, q, k, v = attn_proj(h1, W['sb_w_in'][j], 0)
        k, v = _lane_major(k, _att_tiles(S)[1]), _lane_major(v, _att_tiles(S)[1])
        o, lt = _sb_fwd("sb_fwd", q, k, v)
        o2 = _unheads(o)
        return three(_mm("sb_out", two(o2), W['sb_w_out'][j])), (q, k, v, lt, o2)

    def sb_bwd(j, h1, dy1, cache):
        q, k, v, lt, o2 = cache
        g = {'sb_w_out': _mm("sb_dwout", two(o2), two(dy1), ta=True)}
        do = _heads(three(_mm("sb_do", two(dy1), W['sb_w_out'][j], tb=True)), H)
        dq, dk, dv = _sb_bwd("sb_bwd", q, k, v, do, lt)
        dk, dv = _lane_major_inverse(dk, _att_tiles(S)[1]), _lane_major_inverse(dv, _att_tiles(S)[1])
        dproj = jnp.concatenate([_unheads(dq).astype(BF16), _unheads(dk).astype(BF16), _unheads(dv).astype(BF16)], axis=-1)
        g['sb_w_in'] = _mm("sb_dwin", two(h1), two(dproj), ta=True)
        return _mm("sb_dh", two(dproj), W['sb_w_in'][j], tb=True), g

    def cv_fwd(j, h1):
        pw = three(_mm("cv_in", two(h1), W['cv_w_in'][j]))
        ygl = _cv_glu("cv_glu", pw, vec('cv_b_in', j))
        dw = jnp.pad(small['cv_dw'][j], ((0, CONV_HALO - CONV_WIDTH), (0, 0)))
        yc = _dwconv("cv_dwconv", ygl, dw, vec('cv_dw_b', j))
        ys = _cv_ln_act("cv_ln_act", yc, vec('cv_ln_g', j), vec('cv_ln_b', j))
        return three(_mm("cv_out", two(ys), W['cv_w_out'][j])), (pw, ygl, dw, yc, ys)

    def cv_bwd(j, h1, dy1, cache):
        pw, ygl, dw, yc, ys = cache
        g = {'cv_w_out': _mm("cv_dwout", two(ys), two(dy1), ta=True)}
        dys = three(_mm("cv_dys", two(dy1), W['cv_w_out'][j], tb=True))
        dyc, g['cv_ln_g'], g['cv_ln_b'], g['cv_dw_b'] = _cv_ln_act_bwd("cv_ln_act_bwd", yc, dys, vec('cv_ln_g', j), vec('cv_ln_b', j))
        dygl, ddw = _dwconv_bwd("cv_dwconv_bwd", dyc, ygl, dw)
        g['cv_dw'] = ddw[:CONV_WIDTH]
        dpw, g['cv_b_in'] = _cv_glu_bwd("cv_glu_bwd", pw, dygl, vec('cv_b_in', j))
        g['cv_w_in'] = _mm("cv_dwin", two(h1), two(dpw), ta=True)
        return _mm("cv_dh", two(dpw), W['cv_w_in'][j], tb=True), g

    mixers = [(gm_fwd, gm_bwd), (fox_fwd, fox_bwd), (sb_fwd, sb_bwd), (cv_fwd, cv_bwd)]
    n_mix = len(mixers)

    saved = []
    for l in range(depth):
        m, j = l % n_mix, l // n_mix
        sh1, sc1, g1_, sh2, sc2, g2_ = mods[l]
        ybias = vec('cv_b_out', j) if m == 3 else None
        h1 = _modulate("mod1", x, sc1, sh1)
        y1, cache = mixers[m][0](j, h1)
        xm = _resid_ln("resid_ln1", alpha, x, y1, g1_, small['ln1_g'][l][None], small['ln1_b'][l][None], ybias)
        h2 = _modulate("mod2", xm, sc2, sh2)
        z = three(_mm("ffn_in", two(h2), W['ffn_w_in'][l]))
        a = _swiglu_act("ffn_act", z)
        y2 = three(_mm("ffn_out", two(a), W['ffn_w_out'][l]))
        xo = _resid_ln("resid_ln2", alpha, xm, y2, g2_, small['ln2_g'][l][None], small['ln2_b'][l][None])
        saved.append((x, h1, y1, cache, xm, h2, z, a, y2, ybias))
        x = xo

    dx, sq = _loss_head("loss_head", x, target)
    loss = lax.psum(jnp.sum(sq) * np.float32(0.5 / D), axes)

    grads = {n: [None] * p[n].shape[0] for n in WEIGHTS}
    dmod = [None] * depth
    for l in reversed(range(depth)):
        m, j = l % n_mix, l // n_mix
        sh1, sc1, g1_, sh2, sc2, g2_ = mods[l]
        x_in, h1, y1, cache, xm, h2, z, a, y2, ybias = saved[l]
        dr2, dy2, dg2, grads['ln2_g'][l], grads['ln2_b'][l], _ = _resid_ln_bwd("resid_ln2_bwd", alpha, dx, xm, y2, g2_, small['ln2_g'][l][None])
        grads['ffn_w_out'][l] = _mm("ffn_dwout", two(a), two(dy2), ta=True)
        da = three(_mm("ffn_da", two(dy2), W['ffn_w_out'][l], tb=True))
        dz = _swiglu_act_bwd("ffn_act_bwd", da, z)
        grads['ffn_w_in'][l] = _mm("ffn_dwin", two(h2), two(dz), ta=True)
        dh2 = three(_mm("ffn_dh", two(dz), W['ffn_w_in'][l], tb=True))
        dxm, dsc2, dsh2 = _modulate_bwd("mod2_bwd", alpha, dh2, dr2, xm, sc2)
        dr1, dy1, dg1, grads['ln1_g'][l], grads['ln1_b'][l], dyb = _resid_ln_bwd("resid_ln1_bwd", alpha, dxm, x_in, y1, g1_, small['ln1_g'][l][None], ybias)
        dh1, mg = mixers[m][1](j, h1, dy1, cache)
        if m == 3:
            mg['cv_b_out'] = dyb
        for n, gval in mg.items():
            grads[n][j] = gval
        dx, dsc1, dsh1 = _modulate_bwd("mod1_bwd", alpha, three(dh1), dr1, x_in, sc1)
        dmod[l] = jnp.concatenate([dsh1, dsc1, dg1, dsh2, dsc2, dg2], axis=-1)[:, 0, :]
    grad_x = dx
    dmod = jnp.stack(dmod)
    grads['mod_b'] = [jnp.sum(dmod[l], axis=0) for l in range(depth)]
    full_shape = {n: tuple(t.shape) for n, t in small.items()}

    small_names = SMALL_REPL + SMALL_SPLIT
    small_parts = [jnp.stack([gv.reshape(full_shape[n][1:]) for gv in grads[n]]) for n in small_names]
    pack_a = _pack([dmod], LANES, F32, SUBLANES)
    pack_b = _pack(small_parts, LANES, F32, SUBLANES)
    g2 = _all_gather8("ag_grads_small", [jnp.concatenate([pack_a, pack_b], axis=0)])[0]
    rows_a = pack_a.shape[0]
    dmod_all = g2[:, :rows_a].reshape(N_DEV, -1)[:, :dmod.size].reshape(N_DEV, depth, B, 6 * D)
    dmod_all = dmod_all.transpose(1, 0, 2, 3).reshape(depth, n_seq, 6 * D)
    small_sum = _sum8("sum_grads_small", g2[:, rows_a:]).reshape(-1)
    g_small = dict(zip(small_names, _unpack(small_sum, [full_shape[n] for n in small_names])))
    for n in SMALL_SPLIT:
        w = p[n].shape[-1]
        g_small[n] = lax.dynamic_slice_in_dim(g_small[n], my_q * w, w, axis=g_small[n].ndim - 1)

    dm_cols = lax.dynamic_slice_in_dim(dmod_all, my_q * mod_cols, mod_cols, axis=2)
    dm_cols = jnp.pad(dm_cols, ((0, 0), (0, seq_pad - n_seq), (0, 0)))
    g_mod_w = jnp.stack([_mm(f"mod_dw{l}", c_act, dm_cols[l], ta=True) for l in range(depth)])

    keep, give = [], []
    for n, s, hr in zip(big_names, shard_shapes, half_rows):
        gfull = jnp.stack(grads[n])
        g4 = jnp.stack(jnp.split(gfull, N_CHIPS, axis=BIG[n])).reshape(N_CHIPS, 2 * hr, s[2])
        keep.append(lax.dynamic_slice_in_dim(g4, cc * hr, hr, axis=1))
        give.append(lax.dynamic_slice_in_dim(g4, (1 - cc) * hr, hr, axis=1))
    got = _sibling_exchange("rs_sibling", give)
    chip_sum = [_rowwise("rs_add_sibling", lambda r, bv, cv: ([r[0] + r[1]], [], []),
                         [a.reshape(1, -1, a.shape[2]), b.reshape(1, -1, a.shape[2])],
                         out_rows=[(a.shape[2], BF16)], tm=512)[0].reshape(a.shape) for a, b in zip(keep, got)]
    from_chips = _chip_all_to_all("rs_chips", chip_sum)
    half_sum = [_rowwise("rs_add_chips", lambda r, bv, cv: ([((r[0] + r[1]) + r[2]) + r[3]], [], []),
                         [t[q][None] for q in range(N_CHIPS)], out_rows=[(t.shape[2], F32)], tm=512)[0][0]
                for t in from_chips]
    other = _sibling_exchange("rs_share", half_sum)
    g_big = {n: jnp.concatenate([jnp.where(cc == 0, a, b), jnp.where(cc == 0, b, a)], axis=0).reshape(s)
             for n, s, a, b in zip(big_names, shard_shapes, half_sum, other)}

    g_out = {**g_small, **g_big, 'mod_w': g_mod_w}
    upd = {n: _adamw("adamw_" + n, p[n], g_out[n], p['m_' + n], p['v_' + n]) for n in WEIGHTS}
    return (loss, grad_x, *[g_out[n] for n in WEIGHTS], *[upd[n][0] for n in WEIGHTS],
            *[upd[n][1] for n in WEIGHTS], *[upd[n][2] for n in WEIGHTS])
```

```python
import math

import jax
import jax.numpy as jnp
import numpy as np
from jax import lax
from jax.experimental import pallas as pl
from jax.experimental.pallas import tpu as pltpu

F32, BF16 = jnp.float32, jnp.bfloat16

HEAD_DIM = 64
CONV_WIDTH = 31
LN_EPS = 1e-5
NEG_INF = -1e30
ADAM_LR, ADAM_B1, ADAM_B2, ADAM_EPS, ADAM_WD, ADAM_STEP = 0.001, 0.9, 0.999, 1e-08, 0.01, 10

LANES = 128
SUBLANES = 8
VMEM_LIMIT_BYTES = 56 * 1024 * 1024
N_CHIPS = 4
N_DEV = 8

WEIGHTS = ['mod_w', 'mod_b', 'ln1_g', 'ln1_b', 'ln2_g', 'ln2_b', 'ffn_w_in', 'ffn_w_out', 'gm_w_in', 'gm_b_in',
           'gm_ln_g', 'gm_ln_b', 'gm_w_s', 'gm_b_s', 'gm_w_out', 'fox_w_in', 'fox_b_f', 'fox_w_out', 'sb_w_in',
           'sb_w_out', 'cv_w_in', 'cv_b_in', 'cv_dw', 'cv_dw_b', 'cv_ln_g', 'cv_ln_b', 'cv_w_out', 'cv_b_out']
ARGS = ['x', 'c'] + WEIGHTS + ['loss_target'] + ['m_' + n for n in WEIGHTS] + ['v_' + n for n in WEIGHTS]
BIG = {'ffn_w_in': 2, 'ffn_w_out': 1, 'gm_w_in': 2, 'gm_w_out': 1, 'fox_w_in': 2, 'fox_w_out': 1,
       'sb_w_in': 2, 'sb_w_out': 1, 'cv_w_in': 2, 'cv_w_out': 1}
SMALL_SPLIT = ['cv_b_in', 'cv_dw', 'cv_dw_b', 'cv_ln_g', 'cv_ln_b', 'cv_b_out']
SMALL_REPL = ['mod_b', 'ln1_g', 'ln1_b', 'ln2_g', 'ln2_b', 'gm_b_in', 'gm_ln_g', 'gm_ln_b', 'gm_w_s', 'gm_b_s', 'fox_b_f']
PACK_COLS = 1024


_names_used = {}


def _unique(name):
    k = _names_used.get(name, 0)
    _names_used[name] = k + 1
    return name if k == 0 else f"{name}_{k}"


def _params(sem):
    return pltpu.CompilerParams(dimension_semantics=sem, vmem_limit_bytes=VMEM_LIMIT_BYTES)


def _pick(dim, pref, mult=LANES):
    if dim <= pref:
        return dim
    best = 0
    for t in range(mult, pref + 1, mult):
        if dim % t == 0:
            best = t
    assert best, (dim, pref)
    return best


def _mesh_pos():
    return lax.axis_index("x"), lax.axis_index("y"), lax.axis_index("c")


AG_COPIES = 7
A2A_COPIES = 3


def _comm_call(name, body, blks, out_shapes, n_sems):
    n = len(blks)
    hbm = pl.BlockSpec(memory_space=pl.ANY)
    return pl.pallas_call(
        body, name=_unique(name), out_shape=out_shapes, in_specs=[hbm] * n, out_specs=[hbm] * n,
        scratch_shapes=[pltpu.SemaphoreType.DMA((n_sems * n,)), pltpu.SemaphoreType.DMA((n_sems * n,)),
                        pltpu.SemaphoreType.DMA((n,))],
    )(*blks)


def _all_gather8(name, blks):
    n = len(blks)

    def body(*refs):
        x_refs, out_refs, (send_sems, recv_sems, local_sems) = refs[:n], refs[n:2 * n], refs[2 * n:]
        x, y, c = _mesh_pos()
        me, sibling = (x, y, c), (x, y, 1 - c)
        chips = [(1 - x, y), (x, 1 - y), (1 - x, 1 - y)]

        def copy(a, k, block, to, from_input=False):
            px, py, pc = block
            slot = out_refs[a].at[4 * px + 2 * py + pc]
            return pltpu.make_async_remote_copy(
                src_ref=x_refs[a] if from_input else slot, dst_ref=slot,
                send_sem=send_sems.at[AG_COPIES * a + k], recv_sem=recv_sems.at[AG_COPIES * a + k],
                device_id=to, device_id_type=pl.DeviceIdType.MESH)

        local, sent = [], []
        for a in range(n):
            local.append(pltpu.make_async_copy(x_refs[a], out_refs[a].at[4 * x + 2 * y + c], local_sems.at[a]))
            local[-1].start()
            first = [copy(a, 0, me, sibling, True)] + [copy(a, 1 + j, me, (*chip, c), True) for j, chip in enumerate(chips)]
            for cp in first:
                cp.start()
            sent += first
        for a in range(n):
            for j, chip in enumerate(chips):
                copy(a, 1 + j, (*chip, c), me).wait_recv()
                sent.append(copy(a, 4 + j, (*chip, c), sibling))
                sent[-1].start()
        for a in range(n):
            copy(a, 0, sibling, me).wait_recv()
            for j, chip in enumerate(chips):
                copy(a, 4 + j, (*chip, 1 - c), me).wait_recv()
        for cp in sent:
            cp.wait_send()
        for cp in local:
            cp.wait()

    return _comm_call(name, body, blks, [jax.ShapeDtypeStruct((N_DEV,) + b.shape, b.dtype) for b in blks], AG_COPIES)


def _sibling_exchange(name, blks):
    n = len(blks)

    def body(*refs):
        x_refs, out_refs, (send_sems, recv_sems, _) = refs[:n], refs[n:2 * n], refs[2 * n:]
        x, y, c = _mesh_pos()
        cps = [pltpu.make_async_remote_copy(src_ref=x_refs[a], dst_ref=out_refs[a], send_sem=send_sems.at[a],
                                            recv_sem=recv_sems.at[a], device_id=(x, y, 1 - c),
                                            device_id_type=pl.DeviceIdType.MESH) for a in range(n)]
        for cp in cps:
            cp.start()
        for cp in cps:
            cp.wait()

    return _comm_call(name, body, blks, [jax.ShapeDtypeStruct(b.shape, b.dtype) for b in blks], 1)


def _chip_all_to_all(name, blks):
    n = len(blks)

    def body(*refs):
        x_refs, out_refs, (send_sems, recv_sems, local_sems) = refs[:n], refs[n:2 * n], refs[2 * n:]
        x, y, c = _mesh_pos()
        chips = [(1 - x, y), (x, 1 - y), (1 - x, 1 - y)]
        my_q = 2 * x + y

        def copy(a, j, src_q, dst_q):
            px, py = chips[j]
            return pltpu.make_async_remote_copy(
                src_ref=x_refs[a].at[src_q], dst_ref=out_refs[a].at[dst_q],
                send_sem=send_sems.at[A2A_COPIES * a + j], recv_sem=recv_sems.at[A2A_COPIES * a + j],
                device_id=(px, py, c), device_id_type=pl.DeviceIdType.MESH)

        local, sent = [], []
        for a in range(n):
            local.append(pltpu.make_async_copy(x_refs[a].at[my_q], out_refs[a].at[my_q], local_sems.at[a]))
            local[-1].start()
            sent += [copy(a, j, 2 * px + py, my_q) for j, (px, py) in enumerate(chips)]
            for cp in sent[-A2A_COPIES:]:
                cp.start()
        for a in range(n):
            for j, (px, py) in enumerate(chips):
                copy(a, j, my_q, 2 * px + py).wait_recv()
        for cp in sent:
            cp.wait_send()
        for cp in local:
            cp.wait()

    return _comm_call(name, body, blks, [jax.ShapeDtypeStruct(b.shape, b.dtype) for b in blks], A2A_COPIES)


def _rowwise(name, fn, rows, bvecs=(), cvecs=(), out_rows=(), out_bsums=(), out_tsums=(), tm=256):
    B, S = rows[0].shape[:2]
    tm = _pick(S, tm, SUBLANES)
    n_r, n_b, n_c = len(rows), len(bvecs), len(cvecs)
    n_or, n_ob = len(out_rows), len(out_bsums)

    def body(*refs):
        ins, outs = refs[:n_r + n_b + n_c], refs[n_r + n_b + n_c:]
        r = [ref[0] for ref in ins[:n_r]]
        bv = [ref[0] for ref in ins[n_r:n_r + n_b]]
        cv = [ref[...] for ref in ins[n_r + n_b:]]
        o_rows, o_bsums, o_tsums = fn(r, bv, cv)
        b, i = pl.program_id(0), pl.program_id(1)
        for ref, val in zip(outs[:n_or], o_rows):
            ref[0] = val.astype(ref.dtype)
        for ref, val in zip(outs[n_or:n_or + n_ob], o_bsums):
            @pl.when(i == 0)
            def _(ref=ref, val=val):
                ref[0] = val

            @pl.when(i > 0)
            def _(ref=ref, val=val):
                ref[0] += val
        for ref, val in zip(outs[n_or + n_ob:], o_tsums):
            first = jnp.logical_and(b == 0, i == 0)

            @pl.when(first)
            def _(ref=ref, val=val):
                ref[...] = val

            @pl.when(jnp.logical_not(first))
            def _(ref=ref, val=val):
                ref[...] += val

    in_specs = [pl.BlockSpec((1, tm, a.shape[2]), lambda b, i: (b, i, 0)) for a in rows]
    in_specs += [pl.BlockSpec((1, 1, a.shape[2]), lambda b, i: (b, 0, 0)) for a in bvecs]
    in_specs += [pl.BlockSpec((1, a.shape[1]), lambda b, i: (0, 0)) for a in cvecs]
    out_shape = [jax.ShapeDtypeStruct((B, S, cdim), dt) for cdim, dt in out_rows]
    out_specs = [pl.BlockSpec((1, tm, cdim), lambda b, i: (b, i, 0)) for cdim, _ in out_rows]
    out_shape += [jax.ShapeDtypeStruct((B, 1, cdim), F32) for cdim in out_bsums]
    out_specs += [pl.BlockSpec((1, 1, cdim), lambda b, i: (b, 0, 0)) for cdim in out_bsums]
    out_shape += [jax.ShapeDtypeStruct((1, cdim), F32) for cdim in out_tsums]
    out_specs += [pl.BlockSpec((1, cdim), lambda b, i: (0, 0)) for cdim in out_tsums]
    sem = ("arbitrary", "arbitrary") if out_tsums else ("parallel", "arbitrary")
    res = pl.pallas_call(body, name=_unique(name), grid=(B, S // tm), in_specs=in_specs, out_specs=out_specs,
                         out_shape=out_shape, compiler_params=_params(sem))(*rows, *bvecs, *cvecs)
    return list(res)


def _mm(name, a, b, ta=False, tb=False, out_dtype=F32, tm=512, tn=1536, tk=1536):
    M, K = (a.shape[1], a.shape[0]) if ta else a.shape
    N = b.shape[0] if tb else b.shape[1]
    assert (b.shape[1] if tb else b.shape[0]) == K, (a.shape, b.shape, ta, tb)
    tm, tn, tk = _pick(M, tm, SUBLANES if not ta else LANES), _pick(N, tn), _pick(K, tk, LANES if (not ta or tb) else SUBLANES)
    nk = K // tk
    dims = (((0 if ta else 1,), (1 if tb else 0,)), ((), ()))

    def body(a_ref, b_ref, o_ref, acc_ref):
        k = pl.program_id(2)
        p = lax.dot_general(a_ref[...].astype(BF16), b_ref[...].astype(BF16), dims, preferred_element_type=F32)
        if nk == 1:
            o_ref[...] = p.astype(o_ref.dtype)
        else:
            @pl.when(k == 0)
            def _():
                acc_ref[...] = p

            @pl.when(k > 0)
            def _():
                acc_ref[...] += p

            @pl.when(k == nk - 1)
            def _():
                o_ref[...] = acc_ref[...].astype(o_ref.dtype)

    a_spec = pl.BlockSpec((tk, tm), lambda i, j, k: (k, i)) if ta else pl.BlockSpec((tm, tk), lambda i, j, k: (i, k))
    b_spec = pl.BlockSpec((tn, tk), lambda i, j, k: (j, k)) if tb else pl.BlockSpec((tk, tn), lambda i, j, k: (k, j))
    return pl.pallas_call(
        body, name=_unique(name), grid=(M // tm, N // tn, nk), in_specs=[a_spec, b_spec],
        out_specs=pl.BlockSpec((tm, tn), lambda i, j, k: (i, j)),
        out_shape=jax.ShapeDtypeStruct((M, N), out_dtype),
        scratch_shapes=[pltpu.VMEM((tm, tn) if nk > 1 else (SUBLANES, LANES), F32)],
        compiler_params=_params(("parallel", "parallel", "arbitrary")))(a, b)


def _silu(x):
    return x * _sigmoid(x)


def _sigmoid(x):
    return 1.0 / (1.0 + jnp.exp(-x))


def _dsilu(x):
    s = _sigmoid(x)
    return s * (1.0 + x * (1.0 - s))


def _gelu(x):
    return 0.5 * x * (1.0 + lax.erf(x * np.float32(math.sqrt(0.5))))


def _dgelu(x):
    cdf = 0.5 * (1.0 + lax.erf(x * np.float32(math.sqrt(0.5))))
    pdf = jnp.exp(-0.5 * x * x) * np.float32(1.0 / math.sqrt(2.0 * math.pi))
    return cdf + x * pdf


def _ln_stats(r):
    mu = jnp.mean(r, axis=-1, keepdims=True)
    xc = r - mu
    var = jnp.mean(xc * xc, axis=-1, keepdims=True)
    rstd = lax.rsqrt(var + LN_EPS)
    return xc * rstd, rstd


def _ln_bwd(dxhat, xhat, rstd):
    m1 = jnp.mean(dxhat, axis=-1, keepdims=True)
    m2 = jnp.mean(dxhat * xhat, axis=-1, keepdims=True)
    return rstd * (dxhat - m1 - xhat * m2)


def _csum(v):
    return jnp.sum(v, axis=0, keepdims=True)


def _split_dot(x, m01, lhs01=False, terms=2):
    acc, rem = None, x
    for _ in range(terms):
        part = rem.astype(BF16)
        rem = rem - part.astype(F32)
        d = jnp.dot(m01, part, preferred_element_type=F32) if lhs01 else jnp.dot(part, m01, preferred_element_type=F32)
        acc = d if acc is None else acc + d
    return acc


def _iota2(shape, dim):
    return lax.broadcasted_iota(jnp.int32, shape, dim)


def _modulate(name, x, sc, sh):
    D = x.shape[2]
    return _rowwise(name, lambda r, bv, cv: ([r[0] * (1.0 + bv[0]) + bv[1]], [], []),
                    [x], [sc, sh], [], out_rows=[(D, BF16)])[0]


def _resid_ln(name, alpha, x, y, g, ln_g, ln_b, ybias=None):
    D = x.shape[2]

    def fn(r, bv, cv):
        yy = r[1] if ybias is None else r[1] + cv[2]
        xhat, _ = _ln_stats(alpha * r[0] + (1.0 + bv[0]) * yy)
        return [xhat * cv[0] + cv[1]], [], []
    cvecs = [ln_g, ln_b] + ([] if ybias is None else [ybias])
    return _rowwise(name, fn, [x, y], [g], cvecs, out_rows=[(D, F32)])[0]


def _resid_ln_bwd(name, alpha, dxn, x, y, g, ln_g, ybias=None):
    D = x.shape[2]

    def fn(r, bv, cv):
        yy = r[2] if ybias is None else r[2] + cv[1]
        xhat, rstd = _ln_stats(alpha * r[1] + (1.0 + bv[0]) * yy)
        dr = _ln_bwd(r[0] * cv[0], xhat, rstd)
        dy = (1.0 + bv[0]) * dr
        return [dr, dy], [_csum(dr * yy)], [_csum(r[0] * xhat), _csum(r[0]), _csum(dy)]
    cvecs = [ln_g] + ([] if ybias is None else [ybias])
    return _rowwise(name, fn, [dxn, x, y], [g], cvecs, out_rows=[(D, F32), (D, BF16)], out_bsums=[D], out_tsums=[D, D, D])


def _modulate_bwd(name, alpha, dh, dr, x, sc):
    D = x.shape[2]

    def fn(r, bv, cv):
        return [alpha * r[1] + r[0] * (1.0 + bv[0])], [_csum(r[0] * r[2]), _csum(r[0])], []
    return _rowwise(name, fn, [dh, dr, x], [sc], [], out_rows=[(D, F32)], out_bsums=[D, D])


def _loss_head(name, y, target):
    D = y.shape[2]

    def fn(r, bv, cv):
        e = r[0] - r[1]
        return [e * np.float32(1.0 / D)], [_csum(e * e)], []
    return _rowwise(name, fn, [y, target], [], [], out_rows=[(D, F32)], out_bsums=[D])


def _swiglu_act(name, z):
    Hd = z.shape[2] // 2
    return _rowwise(name, lambda r, bv, cv: ([_silu(r[0][:, :Hd]) * r[0][:, Hd:]], [], []), [z], out_rows=[(Hd, BF16)])[0]


def _swiglu_act_bwd(name, da, z):
    Hd = z.shape[2] // 2

    def fn(r, bv, cv):
        gg, u = r[1][:, :Hd], r[1][:, Hd:]
        return [jnp.concatenate([r[0] * u * _dsilu(gg), r[0] * _silu(gg)], axis=1)], [], []
    return _rowwise(name, fn, [da, z], out_rows=[(2 * Hd, BF16)])[0]


def _gm_act(name, zin, b_in, ln_g, ln_b):
    W = zin.shape[2] // 2

    def fn(r, bv, cv):
        z = _gelu(r[0] + cv[0])
        vhat, _ = _ln_stats(z[:, W:])
        return [z[:, :W], vhat * cv[1] + cv[2]], [], []
    return _rowwise(name, fn, [zin], [], [b_in, ln_g, ln_b], out_rows=[(W, F32), (W, BF16)])


def _gm_act_bwd(name, zin, du, dvn, b_in, ln_g):
    W = zin.shape[2] // 2

    def fn(r, bv, cv):
        zz = r[0] + cv[0]
        z = _gelu(zz)
        vhat, rstd = _ln_stats(z[:, W:])
        dv = _ln_bwd(r[2] * cv[1], vhat, rstd)
        dzin = jnp.concatenate([r[1], dv], axis=1) * _dgelu(zz)
        return [dzin], [], [_csum(dzin), _csum(r[2] * vhat), _csum(r[2])]
    return _rowwise(name, fn, [zin, du, dvn], [], [b_in, ln_g], out_rows=[(2 * W, BF16)], out_tsums=[2 * W, W, W])


def _gm_causal_w(ws_ref, g):
    T = ws_ref.shape[1]
    return jnp.where(_iota2((T, T), 1) <= _iota2((T, T), 0), ws_ref[g], 0.0).astype(BF16)


def _gm_spatial(name, u, vn, w_s, b_sT):
    B, S, W = u.shape
    G, T = w_s.shape[0], w_s.shape[1]
    assert W == G * T, "a head group is as wide as a chunk is long"

    def body(u_ref, vn_ref, ws_ref, bs_ref, y_ref):
        for g in range(G):
            cs = slice(g * T, (g + 1) * T)
            sv = jnp.dot(_gm_causal_w(ws_ref, g), vn_ref[0, :, cs], preferred_element_type=F32) + bs_ref[:, g:g + 1]
            y_ref[0, :, cs] = (u_ref[0, :, cs] * sv).astype(y_ref.dtype)

    row = pl.BlockSpec((1, T, W), lambda b, i: (b, i, 0))
    return pl.pallas_call(
        body, name=_unique(name), grid=(B, S // T),
        in_specs=[row, row, pl.BlockSpec((G, T, T), lambda b, i: (0, 0, 0)), pl.BlockSpec((T, G), lambda b, i: (0, 0))],
        out_specs=row, out_shape=jax.ShapeDtypeStruct((B, S, W), BF16),
        compiler_params=_params(("parallel", "parallel")))(u, vn, w_s, b_sT)


def _gm_spatial_bwd(name, dyg, u, vn, w_s, b_sT):
    B, S, W = u.shape
    G, T = w_s.shape[0], w_s.shape[1]
    assert W == G * T, "a head group is as wide as a chunk is long"

    def body(dy_ref, u_ref, vn_ref, ws_ref, bs_ref, du_ref, dvn_ref, dws_ref, dbs_ref):
        first = jnp.logical_and(pl.program_id(0) == 0, pl.program_id(1) == 0)

        @pl.when(first)
        def _():
            dws_ref[...] = jnp.zeros_like(dws_ref)
            dbs_ref[...] = jnp.zeros_like(dbs_ref)

        tril = _iota2((T, T), 1) <= _iota2((T, T), 0)
        for g in range(G):
            cs = slice(g * T, (g + 1) * T)
            wm = _gm_causal_w(ws_ref, g)
            vng = vn_ref[0, :, cs]
            sv = jnp.dot(wm, vng, preferred_element_type=F32) + bs_ref[:, g:g + 1]
            dy = dy_ref[0, :, cs]
            du_ref[0, :, cs] = dy * sv
            dsv = dy * u_ref[0, :, cs]
            dsv16 = dsv.astype(BF16)
            dvn_ref[0, :, cs] = lax.dot_general(wm, dsv16, (((0,), (0,)), ((), ())), preferred_element_type=F32)
            dw = lax.dot_general(dsv16, vng, (((1,), (1,)), ((), ())), preferred_element_type=F32)
            dws_ref[g] += jnp.where(tril, dw, 0.0)
            dbs_ref[:, g:g + 1] += jnp.sum(dsv, axis=1, keepdims=True)

    row = pl.BlockSpec((1, T, W), lambda b, i: (b, i, 0))
    return pl.pallas_call(
        body, name=_unique(name), grid=(B, S // T),
        in_specs=[row, row, row, pl.BlockSpec((G, T, T), lambda b, i: (0, 0, 0)), pl.BlockSpec((T, G), lambda b, i: (0, 0))],
        out_specs=[row, row, pl.BlockSpec((G, T, T), lambda b, i: (0, 0, 0)), pl.BlockSpec((T, G), lambda b, i: (0, 0))],
        out_shape=[jax.ShapeDtypeStruct((B, S, W), F32), jax.ShapeDtypeStruct((B, S, W), F32),
                   jax.ShapeDtypeStruct((G, T, T), F32), jax.ShapeDtypeStruct((T, G), F32)],
        compiler_params=_params(("arbitrary", "arbitrary")))(dyg, u, vn, w_s, b_sT)


def _cv_glu(name, pw, b_in):
    W = pw.shape[2] // 2

    def fn(r, bv, cv):
        z = r[0] + cv[0]
        return [z[:, :W] * _sigmoid(z[:, W:])], [], []
    return _rowwise(name, fn, [pw], [], [b_in], out_rows=[(W, F32)])[0]


def _cv_glu_bwd(name, pw, dyg, b_in):
    W = pw.shape[2] // 2

    def fn(r, bv, cv):
        z = r[0] + cv[0]
        a, s = z[:, :W], _sigmoid(z[:, W:])
        dpw = jnp.concatenate([r[1] * s, r[1] * a * s * (1.0 - s)], axis=1)
        return [dpw], [], [_csum(dpw)]
    return _rowwise(name, fn, [pw, dyg], [], [b_in], out_rows=[(2 * W, BF16)], out_tsums=[2 * W])


def _cv_ln_act(name, yc, ln_g, ln_b):
    D = yc.shape[2]

    def fn(r, bv, cv):
        xhat, _ = _ln_stats(r[0])
        return [_silu(xhat * cv[0] + cv[1])], [], []
    return _rowwise(name, fn, [yc], [], [ln_g, ln_b], out_rows=[(D, BF16)])[0]


def _cv_ln_act_bwd(name, yc, dys, ln_g, ln_b):
    D = yc.shape[2]

    def fn(r, bv, cv):
        xhat, rstd = _ln_stats(r[0])
        dyn = r[1] * _dsilu(xhat * cv[0] + cv[1])
        dyc = _ln_bwd(dyn * cv[0], xhat, rstd)
        return [dyc], [], [_csum(dyn * xhat), _csum(dyn), _csum(dyc)]
    return _rowwise(name, fn, [yc, dys], [], [ln_g, ln_b], out_rows=[(D, F32)], out_tsums=[D, D, D])


CONV_HALO = 32
CONV_TS, CONV_TC = 256, 128


def _dwconv(name, y, dw, dw_b):
    B, S, D = y.shape
    ts, tc, halo, K = _pick(S, CONV_TS, SUBLANES), _pick(D, CONV_TC), CONV_HALO, CONV_WIDTH

    def body(cur_ref, prev_ref, dw_ref, b_ref, o_ref, buf):
        i = pl.program_id(1)
        buf[pl.ds(0, halo), :] = jnp.where(i > 0, prev_ref[0, pl.ds(ts - halo, halo), :], 0.0)
        buf[pl.ds(halo, ts), :] = cur_ref[0]
        acc = jnp.zeros((ts, tc), F32) + b_ref[...]
        for k in range(K):
            acc = acc + dw_ref[k:k + 1, :] * buf[pl.ds(halo - (K - 1) + k, ts), :]
        o_ref[0] = acc

    return pl.pallas_call(
        body, name=_unique(name), grid=(B, S // ts, D // tc),
        in_specs=[pl.BlockSpec((1, ts, tc), lambda b, i, j: (b, i, j)),
                  pl.BlockSpec((1, ts, tc), lambda b, i, j: (b, jnp.maximum(i - 1, 0), j)),
                  pl.BlockSpec((halo, tc), lambda b, i, j: (0, j)), pl.BlockSpec((1, tc), lambda b, i, j: (0, j))],
        out_specs=pl.BlockSpec((1, ts, tc), lambda b, i, j: (b, i, j)),
        out_shape=jax.ShapeDtypeStruct((B, S, D), F32),
        scratch_shapes=[pltpu.VMEM((halo + ts, tc), F32)],
        compiler_params=_params(("parallel", "parallel", "parallel")))(y, y, dw, dw_b)


def _dwconv_bwd(name, dyc, y, dw):
    B, S, D = y.shape
    ts, tc, halo, K = _pick(S, CONV_TS, SUBLANES), _pick(D, CONV_TC), CONV_HALO, CONV_WIDTH
    nt = S // ts

    def body(g_ref, gnext_ref, y_ref, yprev_ref, dw_ref, dy_ref, ddw_ref, gbuf, ybuf):
        b, i = pl.program_id(1), pl.program_id(2)
        first = jnp.logical_and(b == 0, i == 0)

        @pl.when(first)
        def _():
            ddw_ref[...] = jnp.zeros_like(ddw_ref)

        g = g_ref[0]
        gbuf[pl.ds(0, ts), :] = g
        gbuf[pl.ds(ts, halo), :] = jnp.where(i < nt - 1, gnext_ref[0, pl.ds(0, halo), :], 0.0)
        ybuf[pl.ds(0, halo), :] = jnp.where(i > 0, yprev_ref[0, pl.ds(ts - halo, halo), :], 0.0)
        ybuf[pl.ds(halo, ts), :] = y_ref[0]
        acc = jnp.zeros((ts, tc), F32)
        for k in range(K):
            acc = acc + dw_ref[k:k + 1, :] * gbuf[pl.ds(K - 1 - k, ts), :]
            ddw_ref[k:k + 1, :] += _csum(g * ybuf[pl.ds(halo - (K - 1) + k, ts), :])
        dy_ref[0] = acc

    tile = lambda f: pl.BlockSpec((1, ts, tc), f)
    return pl.pallas_call(
        body, name=_unique(name), grid=(D // tc, B, nt),
        in_specs=[tile(lambda j, b, i: (b, i, j)), tile(lambda j, b, i: (b, jnp.minimum(i + 1, nt - 1), j)),
                  tile(lambda j, b, i: (b, i, j)), tile(lambda j, b, i: (b, jnp.maximum(i - 1, 0), j)),
                  pl.BlockSpec((halo, tc), lambda j, b, i: (0, j))],
        out_specs=[tile(lambda j, b, i: (b, i, j)), pl.BlockSpec((halo, tc), lambda j, b, i: (0, j))],
        out_shape=[jax.ShapeDtypeStruct((B, S, D), F32), jax.ShapeDtypeStruct((halo, D), F32)],
        scratch_shapes=[pltpu.VMEM((ts + halo, tc), F32), pltpu.VMEM((halo + ts, tc), F32)],
        compiler_params=_params(("parallel", "arbitrary", "arbitrary")))(dyc, dyc, y, y, dw)


ATT_BLOCK = 128
ATT_QUERY_BLOCK = 256
ATT_KEY_BLOCK = 512
ATT_PIECE_ROWS = 32
FOX_GATE_COLS = 128


def _att_tiles(S):
    return _pick(S, ATT_QUERY_BLOCK, SUBLANES), _pick(S, ATT_KEY_BLOCK, LANES)


def _pieces(T, TK):
    R = min(T, ATT_PIECE_ROWS)
    segs = [slice(c, c + LANES) for c in range(0, TK, LANES)]
    return [(slice(r, r + R), segs) for r in range(0, T, R)]


def _piece_keep(row0, col0, rs, cs, strict, lane_major_of=0):
    shape = (rs.stop - rs.start, cs.stop - cs.start)
    lane = _iota2(shape, 1)
    key = col0 + (lane * lane_major_of + cs.start // LANES if lane_major_of else cs.start + lane)
    qry = row0 + rs.start + _iota2(shape, 0)
    return key < qry if strict else key <= qry


def _causal_tiles(i, tq, tk):
    return (i * tq + tq + tk - 1) // tk


def _lane_major(t, tk):
    B, H, S, dh = t.shape
    return t.reshape(B, H, S // tk, LANES, tk // LANES, dh).swapaxes(3, 4).reshape(B, H, S, dh)


def _lane_major_inverse(t, tk):
    B, H, S, dh = t.shape
    return t.reshape(B, H, S // tk, tk // LANES, LANES, dh).swapaxes(3, 4).reshape(B, H, S, dh)


def _log_sigmoid(x):
    return jnp.minimum(x, 0.0) - jnp.log(1.0 + jnp.exp(-jnp.abs(x)))


def _fox_gate_cumsum(name, fl, b_f):
    B, S, C = fl.shape
    T = _pick(S, ATT_BLOCK, SUBLANES)

    def body(fl_ref, bf_ref, f_ref, carry):
        @pl.when(pl.program_id(1) == 0)
        def _():
            carry[...] = jnp.zeros_like(carry)
        lf = _log_sigmoid(fl_ref[0] + bf_ref[...])
        lower = (_iota2((T, T), 1) <= _iota2((T, T), 0)).astype(BF16)
        f = _split_dot(lf, lower, lhs01=True, terms=3) + carry[...]
        f_ref[0] = f
        carry[...] = f[T - 1:T, :]

    return pl.pallas_call(
        body, name=_unique(name), grid=(B, S // T),
        in_specs=[pl.BlockSpec((1, T, C), lambda b, i: (b, i, 0)), pl.BlockSpec((1, C), lambda b, i: (0, 0))],
        out_specs=pl.BlockSpec((1, T, C), lambda b, i: (b, i, 0)),
        out_shape=jax.ShapeDtypeStruct((B, S, C), F32),
        scratch_shapes=[pltpu.VMEM((1, C), F32)],
        compiler_params=_params(("arbitrary", "arbitrary")))(fl, b_f)


def _fox_gate_bwd(name, dF, fl, b_f, n_heads):
    B, S, C = fl.shape
    T = _pick(S, ATT_BLOCK, SUBLANES)
    nt = S // T

    def body(df_ref, fl_ref, bf_ref, dfl_ref, dbf_ref, carry):
        first = jnp.logical_and(pl.program_id(0) == 0, pl.program_id(1) == 0)

        @pl.when(pl.program_id(1) == 0)
        def _():
            carry[...] = jnp.zeros_like(carry)

        @pl.when(first)
        def _():
            dbf_ref[...] = jnp.zeros_like(dbf_ref)

        upper = (_iota2((T, T), 1) >= _iota2((T, T), 0)).astype(BF16)
        dlf = _split_dot(df_ref[0], upper, lhs01=True, terms=3) + carry[...]
        carry[...] = dlf[0:1, :]
        x = fl_ref[0] + bf_ref[...]
        dfl = jnp.where(_iota2((T, C), 1) < n_heads, dlf * _sigmoid(-x), 0.0)
        dfl_ref[0] = dfl
        dbf_ref[...] += _csum(dfl)

    rev = lambda b, i: (b, nt - 1 - i, 0)
    return pl.pallas_call(
        body, name=_unique(name), grid=(B, nt),
        in_specs=[pl.BlockSpec((1, T, C), rev), pl.BlockSpec((1, T, C), rev), pl.BlockSpec((1, C), lambda b, i: (0, 0))],
        out_specs=[pl.BlockSpec((1, T, C), rev), pl.BlockSpec((1, C), lambda b, i: (0, 0))],
        out_shape=[jax.ShapeDtypeStruct((B, S, C), F32), jax.ShapeDtypeStruct((1, C), F32)],
        scratch_shapes=[pltpu.VMEM((1, C), F32)],
        compiler_params=_params(("arbitrary", "arbitrary")))(dF, fl, b_f)


_NT = (((1,), (1,)), ((), ()))
_TN = (((0,), (0,)), ((), ()))


def _fox_fwd(name, q, k, v, fq, fk):
    B, H, S, dh = q.shape
    T, TK = _att_tiles(S)
    scale = np.float32(dh ** -0.5)

    pieces = _pieces(T, TK)

    def body(q_ref, k_ref, v_ref, fq_ref, fk_ref, o_ref, lse_ref, s_scr, p_scr):
        i = pl.program_id(2)
        qb = q_ref[0, 0]
        n_tiles = _causal_tiles(i, T, TK)

        def tile(j, carry, diag):
            m, l, acc = carry
            ks = pl.ds(pl.multiple_of(j * TK, TK), TK)
            s_scr[...] = lax.dot_general(qb, k_ref[0, 0, ks, :], _NT, preferred_element_type=F32) * scale
            fkj = fk_ref[0, 0, pl.ds(j, 1), :]
            m_new = []
            for rc, (rs, segs) in enumerate(pieces):
                fq_c, mx = fq_ref[0, 0, rs, :], m[rc]
                for cs in segs:
                    s = s_scr[rs, cs] + fq_c - fkj[:, cs]
                    if diag:
                        s = jnp.where(_piece_keep(i * T, j * TK, rs, cs, False), s, NEG_INF)
                    s_scr[rs, cs] = s
                    mx = jnp.maximum(mx, jnp.max(s, axis=1, keepdims=True))
                m_new.append(mx)
            alpha, l_new = [], []
            for rc, (rs, segs) in enumerate(pieces):
                alpha.append(jnp.exp(m[rc] - m_new[rc]))
                lsum = alpha[rc] * l[rc]
                for cs in segs:
                    p = jnp.exp(s_scr[rs, cs] - m_new[rc])
                    p_scr[rs, cs] = p.astype(BF16)
                    lsum = lsum + jnp.sum(p, axis=1, keepdims=True)
                l_new.append(lsum)
            acc = jnp.concatenate(alpha, axis=0) * acc + jnp.dot(p_scr[...], v_ref[0, 0, ks, :], preferred_element_type=F32)
            return tuple(m_new), tuple(l_new), acc

        init = (tuple(jnp.full((rs.stop - rs.start, 1), NEG_INF, F32) for rs, _ in pieces),
                tuple(jnp.zeros((rs.stop - rs.start, 1), F32) for rs, _ in pieces), jnp.zeros((T, dh), F32))
        carry = lax.fori_loop(0, n_tiles - 1, lambda j, c: tile(j, c, False), init)
        m, l, acc = tile(n_tiles - 1, carry, True)
        m, l = jnp.concatenate(m, axis=0), jnp.concatenate(l, axis=0)
        o_ref[0, 0] = acc / l
        lse_ref[0, 0] = m + jnp.log(l)

    full = lambda w: pl.BlockSpec((1, 1, S, w), lambda b, h, i: (b, h, 0, 0))
    blk = lambda w: pl.BlockSpec((1, 1, T, w), lambda b, h, i: (b, h, i, 0))
    return pl.pallas_call(
        body, name=_unique(name), grid=(B, H, S // T),
        in_specs=[blk(dh), full(dh), full(dh), blk(1), pl.BlockSpec((1, 1, S // TK, TK), lambda b, h, i: (b, h, 0, 0))],
        out_specs=[blk(dh), blk(1)],
        out_shape=[jax.ShapeDtypeStruct((B, H, S, dh), F32), jax.ShapeDtypeStruct((B, H, S, 1), F32)],
        scratch_shapes=[pltpu.VMEM((T, TK), F32), pltpu.VMEM((T, TK), BF16)],
        compiler_params=_params(("parallel", "parallel", "parallel")))(q, k, v, fq, fk)


def _fox_bwd(name, q, k, v, fq, fk, do, lse):
    B, H, S, dh = q.shape
    T, TK = _att_tiles(S)
    nt, nkt = S // T, S // TK
    scale = np.float32(dh ** -0.5)

    pieces = _pieces(T, TK)

    def body(q_ref, k_ref, v_ref, fq_ref, fk_ref, do_ref, lse_ref, dq_ref, dk_ref, dv_ref, dfk_ref,
             p_buf, dp_buf, s_scr, ds_scr, p16_scr):
        dk_ref[...] = jnp.zeros_like(dk_ref)
        dv_ref[...] = jnp.zeros_like(dv_ref)
        dfk_ref[...] = jnp.zeros_like(dfk_ref)

        def qloop(i, _):
            qs = pl.ds(pl.multiple_of(i * T, T), T)
            qb, dob16 = q_ref[0, 0, qs, :], do_ref[0, 0, qs, :].astype(BF16)
            n_tiles = _causal_tiles(i, T, TK)
            row_at = lambda rs: pl.ds(pl.multiple_of(i * T + rs.start, SUBLANES), rs.stop - rs.start)
            fq_c = [fq_ref[0, 0, row_at(rs), :] for rs, _ in pieces]
            lse_c = [lse_ref[0, 0, row_at(rs), :] for rs, _ in pieces]

            def sweep1(j, delta, diag):
                ks = pl.ds(pl.multiple_of(j * TK, TK), TK)
                s_scr[...] = lax.dot_general(qb, k_ref[0, 0, ks, :], _NT, preferred_element_type=F32) * scale
                dp_buf[j] = lax.dot_general(dob16, v_ref[0, 0, ks, :], _NT, preferred_element_type=F32)
                fkj = fk_ref[0, 0, pl.ds(j, 1), :]
                out = []
                for rc, (rs, segs) in enumerate(pieces):
                    d = delta[rc]
                    for cs in segs:
                        p = jnp.exp(((s_scr[rs, cs] + fq_c[rc]) - fkj[:, cs]) - lse_c[rc])
                        if diag:
                            p = jnp.where(_piece_keep(i * T, j * TK, rs, cs, False), p, 0.0)
                        p_buf[j, rs, cs] = p
                        d = d + jnp.sum(p * dp_buf[j, rs, cs], axis=1, keepdims=True)
                    out.append(d)
                return tuple(out)

            zeros = tuple(jnp.zeros((rs.stop - rs.start, 1), F32) for rs, _ in pieces)
            delta = lax.fori_loop(0, n_tiles - 1, lambda j, d: sweep1(j, d, False), zeros)
            delta = sweep1(n_tiles - 1, delta, True)

            def sweep2(j, dq):
                ks = pl.ds(pl.multiple_of(j * TK, TK), TK)
                col = [None] * len(pieces[0][1])
                for rc, (rs, segs) in enumerate(pieces):
                    for sg, cs in enumerate(segs):
                        p = p_buf[j, rs, cs]
                        ds = p * (dp_buf[j, rs, cs] - delta[rc])
                        ds_scr[rs, cs] = ds.astype(BF16)
                        p16_scr[rs, cs] = p.astype(BF16)
                        col[sg] = _csum(ds) if col[sg] is None else col[sg] + _csum(ds)
                dfk_ref[0, 0, pl.ds(j, 1), :] -= jnp.concatenate(col, axis=1)
                ds16 = ds_scr[...]
                dk_ref[0, 0, ks, :] += lax.dot_general(ds16, qb, _TN, preferred_element_type=F32)
                dv_ref[0, 0, ks, :] += lax.dot_general(p16_scr[...], dob16, _TN, preferred_element_type=F32)
                return dq + jnp.dot(ds16, k_ref[0, 0, ks, :], preferred_element_type=F32)

            dq_ref[0, 0, qs, :] = lax.fori_loop(0, n_tiles, sweep2, jnp.zeros((T, dh), F32)) * scale
            return 0

        lax.fori_loop(0, nt, qloop, 0)
        dk_ref[...] = dk_ref[...] * scale

    full = lambda w: pl.BlockSpec((1, 1, S, w), lambda b, h: (b, h, 0, 0))
    fks = pl.BlockSpec((1, 1, nkt, TK), lambda b, h: (b, h, 0, 0))
    return pl.pallas_call(
        body, name=_unique(name), grid=(B, H),
        in_specs=[full(dh), full(dh), full(dh), full(1), fks, full(dh), full(1)],
        out_specs=[full(dh), full(dh), full(dh), fks],
        out_shape=[jax.ShapeDtypeStruct((B, H, S, dh), F32)] * 3 + [jax.ShapeDtypeStruct((B, H, nkt, TK), F32)],
        scratch_shapes=[pltpu.VMEM((nkt, T, TK), F32), pltpu.VMEM((nkt, T, TK), F32), pltpu.VMEM((T, TK), F32),
                        pltpu.VMEM((T, TK), BF16), pltpu.VMEM((T, TK), BF16)],
        compiler_params=_params(("parallel", "parallel")))(q, k, v, fq, fk, do, lse)


def _sb_terms(z, with_sigmoids=True):
    t = jnp.exp(-jnp.abs(z))
    lp = jnp.log(1.0 + t)
    lb, l1 = jnp.minimum(z, 0.0) - lp, jnp.minimum(-z, 0.0) - lp
    if not with_sigmoids:
        return lb, l1, None, None
    inv = 1.0 / (1.0 + t)
    pos = z >= 0.0
    return lb, l1, jnp.where(pos, 1.0, t) * inv, jnp.where(pos, t, 1.0) * inv


def _lane_scan(x, reverse):
    lane = _iota2(x.shape, 1)
    y, d = x, 1
    while d < LANES:
        if reverse:
            y = y + jnp.where(lane + d < LANES, pltpu.roll(y, LANES - d, 1), 0.0)
        else:
            y = y + jnp.where(lane >= d, pltpu.roll(y, d, 1), 0.0)
        d *= 2
    return y


def _chunk_scan(xs, reverse):
    n = len(xs)
    within, acc = [None] * n, None
    for s in (range(n - 1, -1, -1) if reverse else range(n)):
        acc = xs[s] if acc is None else acc + xs[s]
        within[s] = acc
    lanes = _lane_scan(acc, reverse)
    beyond = lanes - acc
    return [w + beyond for w in within], (lanes[:, 0:1] if reverse else lanes[:, LANES - 1:LANES])


SB_DEAD = 110.0


def _sb_fwd(name, q, k, v):
    B, H, S, dh = q.shape
    T, TK = _att_tiles(S)
    scale = np.float32(dh ** -0.5)
    pieces = _pieces(T, TK)

    def body(q_ref, k_ref, v_ref, o_ref, lt_ref, first_ref, z_scr, a_scr):
        i = pl.program_id(2)
        qb = q_ref[0, 0]
        n_tiles = _causal_tiles(i, T, TK)

        def tile(j, carry, diag):
            runs, acc = carry
            ks = pl.ds(pl.multiple_of(j * TK, TK), TK)
            z_scr[...] = lax.dot_general(qb, k_ref[0, 0, ks, :], _NT, preferred_element_type=F32) * scale
            new_runs = []
            for rc, (rs, segs) in enumerate(pieces):
                terms = [_sb_terms(z_scr[rs, cs], False) for cs in segs]
                keep = [_piece_keep(i * T, j * TK, rs, cs, True, len(segs)) if diag else None for cs in segs]
                l1 = [jnp.where(kp, t[1], 0.0) if diag else t[1] for kp, t in zip(keep, terms)]
                right_of, total = _chunk_scan(l1, True)
                for cs, kp, t, x, r in zip(segs, keep, terms, l1, right_of):
                    a = jnp.exp(t[0] + ((r - x) + runs[rc]))
                    a_scr[rs, cs] = (jnp.where(kp, a, 0.0) if diag else a).astype(BF16)
                new_runs.append(runs[rc] + total)
            return tuple(new_runs), acc + jnp.dot(a_scr[...], v_ref[0, 0, ks, :], preferred_element_type=F32)

        init = (tuple(jnp.zeros((rs.stop - rs.start, 1), F32) for rs, _ in pieces), jnp.zeros((T, dh), F32))
        def some_row_alive(runs):
            worst = runs[0]
            for r in runs[1:]:
                worst = jnp.maximum(worst, r)
            return jnp.max(worst) > -SB_DEAD

        def step(c):
            runs, acc = tile(n_tiles - 1 - c[0], (c[1], c[2]), False)
            return c[0] + 1, runs, acc

        visited, runs, acc = lax.while_loop(lambda c: jnp.logical_and(c[0] < n_tiles, some_row_alive(c[1])), step,
                                            (jnp.int32(1), *tile(n_tiles - 1, init, True)))
        o_ref[0, 0] = acc
        lt_ref[0, 0] = jnp.concatenate(runs, axis=0)
        first_ref[pl.program_id(0), pl.program_id(1), i] = (n_tiles - visited).astype(F32)

    full = lambda w: pl.BlockSpec((1, 1, S, w), lambda b, h, i: (b, h, 0, 0))
    blk = lambda w: pl.BlockSpec((1, 1, T, w), lambda b, h, i: (b, h, i, 0))
    return pl.pallas_call(
        body, name=_unique(name), grid=(B, H, S // T),
        in_specs=[blk(dh), full(dh), full(dh)], out_specs=[blk(dh), blk(1), pl.BlockSpec(memory_space=pltpu.SMEM)],
        out_shape=[jax.ShapeDtypeStruct((B, H, S, dh), F32), jax.ShapeDtypeStruct((B, H, S, 1), F32),
                   jax.ShapeDtypeStruct((B, H, S // T), F32)],
        scratch_shapes=[pltpu.VMEM((T, TK), F32), pltpu.VMEM((T, TK), BF16)],
        compiler_params=_params(("arbitrary", "arbitrary", "arbitrary")))(q, k, v)


def _sb_bwd(name, q, k, v, do, lt, first):
    B, H, S, dh = q.shape
    T, TK = _att_tiles(S)
    nt = S // T
    scale = np.float32(dh ** -0.5)

    pieces = _pieces(T, TK)

    def body(first_ref, q_ref, k_ref, v_ref, do_ref, lt_ref, dq_ref, dk_ref, dv_ref, z_scr, da_scr, dz_scr, a_scr):
        dk_ref[...] = jnp.zeros_like(dk_ref)
        dv_ref[...] = jnp.zeros_like(dv_ref)
        b, h = pl.program_id(0), pl.program_id(1)

        def qloop(i, _):
            qs = pl.ds(pl.multiple_of(i * T, T), T)
            qb, dob16 = q_ref[0, 0, qs, :], do_ref[0, 0, qs, :].astype(BF16)
            n_tiles = _causal_tiles(i, T, TK)
            first_tile = jnp.clip(first_ref[b, h, i].astype(jnp.int32), 0, n_tiles - 1)
            lt_c = [lt_ref[0, 0, pl.ds(pl.multiple_of(i * T + rs.start, SUBLANES), rs.stop - rs.start), :] for rs, _ in pieces]

            def tile(j, carry, diag):
                sums_l, sums_e, dq = carry
                ks = pl.ds(pl.multiple_of(j * TK, TK), TK)
                kb, vb = k_ref[0, 0, ks, :], v_ref[0, 0, ks, :]
                z_scr[...] = lax.dot_general(qb, kb, _NT, preferred_element_type=F32) * scale
                da_scr[...] = lax.dot_general(dob16, vb, _NT, preferred_element_type=F32)
                new_l, new_e = [], []
                for rc, (rs, segs) in enumerate(pieces):
                    terms = [_sb_terms(z_scr[rs, cs]) for cs in segs]
                    keep = [_piece_keep(i * T, j * TK, rs, cs, True, len(segs)) if diag else None for cs in segs]
                    l1 = [jnp.where(kp, t[1], 0.0) if diag else t[1] for kp, t in zip(keep, terms)]
                    upto, total_l = _chunk_scan(l1, False)
                    es = []
                    for cs, kp, t, u in zip(segs, keep, terms, upto):
                        a = jnp.exp(t[0] + (lt_c[rc] - (u + sums_l[rc])))
                        if diag:
                            a = jnp.where(kp, a, 0.0)
                        a_scr[rs, cs] = a.astype(BF16)
                        es.append(da_scr[rs, cs] * a)
                    e_upto, total_e = _chunk_scan(es, False)
                    for cs, kp, t, e, eu in zip(segs, keep, terms, es, e_upto):
                        dz = e * t[3] - ((eu - e) + sums_e[rc]) * t[2]
                        dz_scr[rs, cs] = (jnp.where(kp, dz, 0.0) if diag else dz).astype(BF16)
                    new_l.append(sums_l[rc] + total_l)
                    new_e.append(sums_e[rc] + total_e)
                dz16 = dz_scr[...]
                dk_ref[0, 0, ks, :] += lax.dot_general(dz16, qb, _TN, preferred_element_type=F32)
                dv_ref[0, 0, ks, :] += lax.dot_general(a_scr[...], dob16, _TN, preferred_element_type=F32)
                return tuple(new_l), tuple(new_e), dq + jnp.dot(dz16, kb, preferred_element_type=F32)

            zeros = tuple(jnp.zeros((rs.stop - rs.start, 1), F32) for rs, _ in pieces)
            carry = lax.fori_loop(first_tile, n_tiles - 1, lambda j, c: tile(j, c, False), (zeros, zeros, jnp.zeros((T, dh), F32)))
            dq_ref[0, 0, qs, :] = tile(n_tiles - 1, carry, True)[2] * scale
            return 0

        lax.fori_loop(0, nt, qloop, 0)
        dk_ref[...] = dk_ref[...] * scale

    full = lambda w: pl.BlockSpec((1, 1, S, w), lambda b, h: (b, h, 0, 0))
    return pl.pallas_call(
        body, name=_unique(name), grid=(B, H),
        in_specs=[pl.BlockSpec(memory_space=pltpu.SMEM), full(dh), full(dh), full(dh), full(dh), full(1)], out_specs=[full(dh)] * 3,
        out_shape=[jax.ShapeDtypeStruct((B, H, S, dh), F32)] * 3,
        scratch_shapes=[pltpu.VMEM((T, TK), F32), pltpu.VMEM((T, TK), F32), pltpu.VMEM((T, TK), BF16), pltpu.VMEM((T, TK), BF16)],
        compiler_params=_params(("parallel", "parallel")))(first, q, k, v, do, lt)


def _adamw(name, w, g, m, v):
    shape = w.shape
    cols = shape[-1] if (w.ndim >= 2 and shape[-1] % LANES == 0) else 0
    n = w.size
    if cols:
        prep = lambda t: t.reshape(1, n // cols, cols)
    else:
        cols = LANES
        pad = (-n) % (SUBLANES * LANES)
        prep = lambda t: jnp.pad(t.reshape(-1), (0, pad), constant_values=1.0).reshape(1, (n + pad) // cols, cols)

    def fn(r, bv, cv):
        w_, g_, m_, v_ = r
        m2 = ADAM_B1 * m_ + (1.0 - ADAM_B1) * g_
        v2 = ADAM_B2 * v_ + (1.0 - ADAM_B2) * (g_ * g_)
        m_hat = m2 / (1.0 - ADAM_B1 ** ADAM_STEP)
        v_hat = v2 / (1.0 - ADAM_B2 ** ADAM_STEP)
        return [-ADAM_LR * (m_hat / (jnp.sqrt(v_hat) + ADAM_EPS) + ADAM_WD * w_), m2, v2], [], []
    outs = _rowwise(name, fn, [prep(w), prep(g), prep(m), prep(v)], out_rows=[(cols, F32)] * 3, tm=512)
    return [o.reshape(-1)[:n].reshape(shape) for o in outs]


def _sum8(name, parts):
    def fn(r, bv, cv):
        s = r[0]
        for t in r[1:]:
            s = s + t
        return [s], [], []
    rows = [parts[i][None] for i in range(parts.shape[0])]
    return _rowwise(name, fn, rows, out_rows=[(parts.shape[2], F32)])[0][0]


def _pack(arrs, cols, dtype, row_mult):
    flat = jnp.concatenate([a.reshape(-1).astype(dtype) for a in arrs])
    pad = (-flat.size) % (cols * row_mult)
    return jnp.pad(flat, (0, pad)).reshape(-1, cols)


def _unpack(flat, shapes):
    out, off = [], 0
    for s in shapes:
        n = int(np.prod(s))
        out.append(flat[off:off + n].reshape(s))
        off += n
    return out


def _heads(t, H):
    B, S, W = t.shape
    return t.reshape(B, S, H, W // H).transpose(0, 2, 1, 3)


def _unheads(t):
    B, H, S, dh = t.shape
    return t.transpose(0, 2, 1, 3).reshape(B, S, H * dh)


def kernel(*args):
    _names_used.clear()
    p = dict(zip(ARGS, args))
    x, target = p['x'], p['loss_target']
    B, S, D = x.shape
    T = B * S
    depth = p['ln1_g'].shape[0]
    H = D // HEAD_DIM
    alpha = np.float32((2.0 * depth) ** 0.25)
    cx, cy, cc = _mesh_pos()
    my_q = 2 * cx + cy
    axes = ("x", "y", "c")
    two = lambda t: t.reshape(T, t.shape[-1])
    three = lambda t: t.reshape(B, S, t.shape[-1])

    small_in = [p['c']] + [p[n] for n in SMALL_SPLIT]
    g1 = _all_gather8("ag_small", [_pack(small_in, LANES, F32, SUBLANES)])[0]
    g1 = g1.reshape(N_DEV, -1)
    c_all = g1[:, :B * D].reshape(N_DEV * B, D)
    per_chip = [_unpack(g1[2 * q], [a.shape for a in small_in])[1:] for q in range(N_CHIPS)]
    small = {n: jnp.concatenate([per_chip[q][i] for q in range(N_CHIPS)], axis=-1) for i, n in enumerate(SMALL_SPLIT)}
    for n in SMALL_REPL:
        small[n] = p[n]

    n_seq = N_DEV * B
    seq_pad = -(-n_seq // LANES) * LANES
    c_act = _rowwise("c_act", lambda r, bv, cv: ([_silu(r[0])], [], []),
                     [jnp.pad(c_all, ((0, seq_pad - n_seq), (0, 0)))[None]], out_rows=[(D, F32)])[0][0]
    mod_cols = p['mod_w'].shape[2]
    mod_part = jnp.stack([_mm(f"mod_fwd{l}", c_act, p['mod_w'][l])[:n_seq] for l in range(depth)])
    half_layers = depth // 2
    mod_half = lax.dynamic_slice_in_dim(mod_part, cc * half_layers, half_layers, axis=0)
    gm_ = _all_gather8("ag_mod", [mod_half.reshape(half_layers * n_seq, mod_cols)])[0]
    mod_all = gm_.reshape(N_CHIPS, 2, half_layers, n_seq, mod_cols).transpose(1, 2, 3, 0, 4).reshape(depth, n_seq, 6 * D)
    mod_mine = lax.dynamic_slice_in_dim(mod_all, (2 * my_q + cc) * B, B, axis=1)
    mod = _rowwise("mod_bias", lambda r, bv, cv: ([r[0] + bv[0]], [], []), [mod_mine], [p['mod_b'][:, None, :]],
                   out_rows=[(6 * D, F32)])[0]
    mods = [[mod[l, :, None, i * D:(i + 1) * D] for i in range(6)] for l in range(depth)]

    big_names = list(BIG)
    shard_shapes = [p[n].shape for n in big_names]
    half_rows = [s[0] * s[1] // 2 for s in shard_shapes]
    w_halves = [lax.dynamic_slice_in_dim(p[n].reshape(-1, s[2]), cc * hr, hr, axis=0).astype(BF16)
                for n, s, hr in zip(big_names, shard_shapes, half_rows)]
    W = {}
    for n, s, g in zip(big_names, shard_shapes, _all_gather8("ag_weights", w_halves)):
        seg = g.reshape((N_CHIPS,) + s)
        W[n] = jnp.concatenate([seg[q] for q in range(N_CHIPS)], axis=BIG[n])

    def vec(n, j):
        return small[n][j][None, :]

    def attn_proj(h1, w_in, gate_cols):
        wp = jnp.pad(w_in, ((0, 0), (0, gate_cols))) if gate_cols else w_in
        proj = three(_mm("att_proj", two(h1), wp))
        q, k, v = [_heads(proj[..., i * D:(i + 1) * D].astype(BF16), H) for i in range(3)]
        return wp, proj, q, k, v

    def gm_fwd(j, h1):
        zin = three(_mm("gm_in", two(h1), W['gm_w_in'][j]))
        u, vn = _gm_act("gm_act", zin, vec('gm_b_in', j), vec('gm_ln_g', j), vec('gm_ln_b', j))
        b_sT = small['gm_b_s'][j].T
        yg = _gm_spatial("gm_spatial", u, vn, small['gm_w_s'][j], b_sT)
        return three(_mm("gm_out", two(yg), W['gm_w_out'][j])), (zin, u, vn, b_sT, yg)

    def gm_bwd(j, h1, dy1, cache):
        zin, u, vn, b_sT, yg = cache
        g = {'gm_w_out': _mm("gm_dwout", two(yg), two(dy1), ta=True)}
        dyg = three(_mm("gm_dyg", two(dy1), W['gm_w_out'][j], tb=True))
        du, dvn, dws, dbsT = _gm_spatial_bwd("gm_spatial_bwd", dyg, u, vn, small['gm_w_s'][j], b_sT)
        dzin, g['gm_b_in'], g['gm_ln_g'], g['gm_ln_b'] = _gm_act_bwd("gm_act_bwd", zin, du, dvn, vec('gm_b_in', j), vec('gm_ln_g', j))
        g['gm_w_s'], g['gm_b_s'] = dws, dbsT.T
        g['gm_w_in'] = _mm("gm_dwin", two(h1), two(dzin), ta=True)
        return _mm("gm_dh", two(dzin), W['gm_w_in'][j], tb=True), g

    def fox_fwd(j, h1):
        wp, proj, q, k, v = attn_proj(h1, W['fox_w_in'][j], 3 * D + FOX_GATE_COLS - W['fox_w_in'].shape[2])
        fl = proj[..., 3 * D:]
        bf = jnp.pad(small['fox_b_f'][j][None, :], ((0, 0), (0, FOX_GATE_COLS - H)))
        Fh = _fox_gate_cumsum("fox_gate", fl, bf)[..., :H].transpose(0, 2, 1)
        fq, fk = Fh[..., None], Fh.reshape(B, H, -1, _att_tiles(S)[1])
        o, lse = _fox_fwd("fox_fwd", q, k, v, fq, fk)
        o2 = _unheads(o)
        return three(_mm("fox_out", two(o2), W['fox_w_out'][j])), (wp, q, k, v, fl, bf, fq, fk, lse, o2)

    def fox_bwd(j, h1, dy1, cache):
        wp, q, k, v, fl, bf, fq, fk, lse, o2 = cache
        g = {'fox_w_out': _mm("fox_dwout", two(o2), two(dy1), ta=True)}
        do = _heads(three(_mm("fox_do", two(dy1), W['fox_w_out'][j], tb=True)), H)
        dq, dk, dv, dfk = _fox_bwd("fox_bwd", q, k, v, fq, fk, do, lse)
        dF = jnp.pad(dfk.reshape(B, H, S).transpose(0, 2, 1), ((0, 0), (0, 0), (0, FOX_GATE_COLS - H)))
        dfl, dbf = _fox_gate_bwd("fox_gate_bwd", dF, fl, bf, H)
        dproj = jnp.concatenate([_unheads(dq).astype(BF16), _unheads(dk).astype(BF16), _unheads(dv).astype(BF16),
                                 dfl.astype(BF16)], axis=-1)
        g['fox_w_in'] = _mm("fox_dwin", two(h1), two(dproj), ta=True)[:, :W['fox_w_in'].shape[2]]
        g['fox_b_f'] = dbf[0, :H]
        return _mm("fox_dh", two(dproj), wp, tb=True), g

    def sb_fwd(j, h1):
        wp, proj, q, k, v = attn_proj(h1, W['sb_w_in'][j], 0)
        k, v = _lane_major(k, _att_tiles(S)[1]), _lane_major(v, _att_tiles(S)[1])
        o, lt, first = _sb_fwd("sb_fwd", q, k, v)
        o2 = _unheads(o)
        return three(_mm("sb_out", two(o2), W['sb_w_out'][j])), (q, k, v, lt, first, o2)

    def sb_bwd(j, h1, dy1, cache):
        q, k, v, lt, first, o2 = cache
        g = {'sb_w_out': _mm("sb_dwout", two(o2), two(dy1), ta=True)}
        do = _heads(three(_mm("sb_do", two(dy1), W['sb_w_out'][j], tb=True)), H)
        dq, dk, dv = _sb_bwd("sb_bwd", q, k, v, do, lt, first)
        dk, dv = _lane_major_inverse(dk, _att_tiles(S)[1]), _lane_major_inverse(dv, _att_tiles(S)[1])
        dproj = jnp.concatenate([_unheads(dq).astype(BF16), _unheads(dk).astype(BF16), _unheads(dv).astype(BF16)], axis=-1)
        g['sb_w_in'] = _mm("sb_dwin", two(h1), two(dproj), ta=True)
        return _mm("sb_dh", two(dproj), W['sb_w_in'][j], tb=True), g

    def cv_fwd(j, h1):
        pw = three(_mm("cv_in", two(h1), W['cv_w_in'][j]))
        ygl = _cv_glu("cv_glu", pw, vec('cv_b_in', j))
        dw = jnp.pad(small['cv_dw'][j], ((0, CONV_HALO - CONV_WIDTH), (0, 0)))
        yc = _dwconv("cv_dwconv", ygl, dw, vec('cv_dw_b', j))
        ys = _cv_ln_act("cv_ln_act", yc, vec('cv_ln_g', j), vec('cv_ln_b', j))
        return three(_mm("cv_out", two(ys), W['cv_w_out'][j])), (pw, ygl, dw, yc, ys)

    def cv_bwd(j, h1, dy1, cache):
        pw, ygl, dw, yc, ys = cache
        g = {'cv_w_out': _mm("cv_dwout", two(ys), two(dy1), ta=True)}
        dys = three(_mm("cv_dys", two(dy1), W['cv_w_out'][j], tb=True))
        dyc, g['cv_ln_g'], g['cv_ln_b'], g['cv_dw_b'] = _cv_ln_act_bwd("cv_ln_act_bwd", yc, dys, vec('cv_ln_g', j), vec('cv_ln_b', j))
        dygl, ddw = _dwconv_bwd("cv_dwconv_bwd", dyc, ygl, dw)
        g['cv_dw'] = ddw[:CONV_WIDTH]
        dpw, g['cv_b_in'] = _cv_glu_bwd("cv_glu_bwd", pw, dygl, vec('cv_b_in', j))
        g['cv_w_in'] = _mm("cv_dwin", two(h1), two(dpw), ta=True)
        return _mm("cv_dh", two(dpw), W['cv_w_in'][j], tb=True), g

    mixers = [(gm_fwd, gm_bwd), (fox_fwd, fox_bwd), (sb_fwd, sb_bwd), (cv_fwd, cv_bwd)]
    n_mix = len(mixers)

    saved = []
    for l in range(depth):
        m, j = l % n_mix, l // n_mix
        sh1, sc1, g1_, sh2, sc2, g2_ = mods[l]
        ybias = vec('cv_b_out', j) if m == 3 else None
        h1 = _modulate("mod1", x, sc1, sh1)
        y1, cache = mixers[m][0](j, h1)
        xm = _resid_ln("resid_ln1", alpha, x, y1, g1_, small['ln1_g'][l][None], small['ln1_b'][l][None], ybias)
        h2 = _modulate("mod2", xm, sc2, sh2)
        z = three(_mm("ffn_in", two(h2), W['ffn_w_in'][l]))
        a = _swiglu_act("ffn_act", z)
        y2 = three(_mm("ffn_out", two(a), W['ffn_w_out'][l]))
        xo = _resid_ln("resid_ln2", alpha, xm, y2, g2_, small['ln2_g'][l][None], small['ln2_b'][l][None])
        saved.append((x, h1, y1, cache, xm, h2, z, a, y2, ybias))
        x = xo

    dx, sq = _loss_head("loss_head", x, target)
    loss = lax.psum(jnp.sum(sq) * np.float32(0.5 / D), axes)

    grads = {n: [None] * p[n].shape[0] for n in WEIGHTS}
    dmod = [None] * depth
    for l in reversed(range(depth)):
        m, j = l % n_mix, l // n_mix
        sh1, sc1, g1_, sh2, sc2, g2_ = mods[l]
        x_in, h1, y1, cache, xm, h2, z, a, y2, ybias = saved[l]
        dr2, dy2, dg2, grads['ln2_g'][l], grads['ln2_b'][l], _ = _resid_ln_bwd("resid_ln2_bwd", alpha, dx, xm, y2, g2_, small['ln2_g'][l][None])
        grads['ffn_w_out'][l] = _mm("ffn_dwout", two(a), two(dy2), ta=True)
        da = three(_mm("ffn_da", two(dy2), W['ffn_w_out'][l], tb=True))
        dz = _swiglu_act_bwd("ffn_act_bwd", da, z)
        grads['ffn_w_in'][l] = _mm("ffn_dwin", two(h2), two(dz), ta=True)
        dh2 = three(_mm("ffn_dh", two(dz), W['ffn_w_in'][l], tb=True))
        dxm, dsc2, dsh2 = _modulate_bwd("mod2_bwd", alpha, dh2, dr2, xm, sc2)
        dr1, dy1, dg1, grads['ln1_g'][l], grads['ln1_b'][l], dyb = _resid_ln_bwd("resid_ln1_bwd", alpha, dxm, x_in, y1, g1_, small['ln1_g'][l][None], ybias)
        dh1, mg = mixers[m][1](j, h1, dy1, cache)
        if m == 3:
            mg['cv_b_out'] = dyb
        for n, gval in mg.items():
            grads[n][j] = gval
        dx, dsc1, dsh1 = _modulate_bwd("mod1_bwd", alpha, three(dh1), dr1, x_in, sc1)
        dmod[l] = jnp.concatenate([dsh1, dsc1, dg1, dsh2, dsc2, dg2], axis=-1)[:, 0, :]
    grad_x = dx
    dmod = jnp.stack(dmod)
    grads['mod_b'] = [jnp.sum(dmod[l], axis=0) for l in range(depth)]
    full_shape = {n: tuple(t.shape) for n, t in small.items()}

    small_names = SMALL_REPL + SMALL_SPLIT
    small_parts = [jnp.stack([gv.reshape(full_shape[n][1:]) for gv in grads[n]]) for n in small_names]
    pack_a = _pack([dmod], LANES, F32, SUBLANES)
    pack_b = _pack(small_parts, LANES, F32, SUBLANES)
    g2 = _all_gather8("ag_grads_small", [jnp.concatenate([pack_a, pack_b], axis=0)])[0]
    rows_a = pack_a.shape[0]
    dmod_all = g2[:, :rows_a].reshape(N_DEV, -1)[:, :dmod.size].reshape(N_DEV, depth, B, 6 * D)
    dmod_all = dmod_all.transpose(1, 0, 2, 3).reshape(depth, n_seq, 6 * D)
    small_sum = _sum8("sum_grads_small", g2[:, rows_a:]).reshape(-1)
    g_small = dict(zip(small_names, _unpack(small_sum, [full_shape[n] for n in small_names])))
    for n in SMALL_SPLIT:
        w = p[n].shape[-1]
        g_small[n] = lax.dynamic_slice_in_dim(g_small[n], my_q * w, w, axis=g_small[n].ndim - 1)

    dm_cols = lax.dynamic_slice_in_dim(dmod_all, my_q * mod_cols, mod_cols, axis=2)
    dm_cols = jnp.pad(dm_cols, ((0, 0), (0, seq_pad - n_seq), (0, 0)))
    g_mod_w = jnp.stack([_mm(f"mod_dw{l}", c_act, dm_cols[l], ta=True) for l in range(depth)])

    keep, give = [], []
    for n, s, hr in zip(big_names, shard_shapes, half_rows):
        gfull = jnp.stack(grads[n])
        g4 = jnp.stack(jnp.split(gfull, N_CHIPS, axis=BIG[n])).reshape(N_CHIPS, 2 * hr, s[2])
        keep.append(lax.dynamic_slice_in_dim(g4, cc * hr, hr, axis=1))
        give.append(lax.dynamic_slice_in_dim(g4, (1 - cc) * hr, hr, axis=1))
    got = _sibling_exchange("rs_sibling", give)
    chip_sum = [_rowwise("rs_add_sibling", lambda r, bv, cv: ([r[0] + r[1]], [], []),
                         [a.reshape(1, -1, a.shape[2]), b.reshape(1, -1, a.shape[2])],
                         out_rows=[(a.shape[2], BF16)], tm=512)[0].reshape(a.shape) for a, b in zip(keep, got)]
    from_chips = _chip_all_to_all("rs_chips", chip_sum)
    half_sum = [_rowwise("rs_add_chips", lambda r, bv, cv: ([((r[0] + r[1]) + r[2]) + r[3]], [], []),
                         [t[q][None] for q in range(N_CHIPS)], out_rows=[(t.shape[2], F32)], tm=512)[0][0]
                for t in from_chips]
    other = _sibling_exchange("rs_share", half_sum)
    g_big = {n: jnp.concatenate([jnp.where(cc == 0, a, b), jnp.where(cc == 0, b, a)], axis=0).reshape(s)
             for n, s, a, b in zip(big_names, shard_shapes, half_sum, other)}

    g_out = {**g_small, **g_big, 'mod_w': g_mod_w}
    upd = {n: _adamw("adamw_" + n, p[n], g_out[n], p['m_' + n], p['v_' + n]) for n in WEIGHTS}
    return (loss, grad_x, *[g_out[n] for n in WEIGHTS], *[upd[n][0] for n in WEIGHTS],
            *[upd[n][1] for n in WEIGHTS], *[upd[n][2] for n in WEIGHTS])
```

```python
import math

import jax
import jax.numpy as jnp
import numpy as np
from jax import lax
from jax.experimental import pallas as pl
from jax.experimental.pallas import tpu as pltpu

F32, BF16 = jnp.float32, jnp.bfloat16

HEAD_DIM = 64
CONV_WIDTH = 31
LN_EPS = 1e-5
NEG_INF = -1e30
ADAM_LR, ADAM_B1, ADAM_B2, ADAM_EPS, ADAM_WD, ADAM_STEP = 0.001, 0.9, 0.999, 1e-08, 0.01, 10

LANES = 128
SUBLANES = 8
VMEM_LIMIT_BYTES = 56 * 1024 * 1024
N_CHIPS = 4
N_DEV = 8

WEIGHTS = ['mod_w', 'mod_b', 'ln1_g', 'ln1_b', 'ln2_g', 'ln2_b', 'ffn_w_in', 'ffn_w_out', 'gm_w_in', 'gm_b_in',
           'gm_ln_g', 'gm_ln_b', 'gm_w_s', 'gm_b_s', 'gm_w_out', 'fox_w_in', 'fox_b_f', 'fox_w_out', 'sb_w_in',
           'sb_w_out', 'cv_w_in', 'cv_b_in', 'cv_dw', 'cv_dw_b', 'cv_ln_g', 'cv_ln_b', 'cv_w_out', 'cv_b_out']
ARGS = ['x', 'c'] + WEIGHTS + ['loss_target'] + ['m_' + n for n in WEIGHTS] + ['v_' + n for n in WEIGHTS]
BIG = {'ffn_w_in': 2, 'ffn_w_out': 1, 'gm_w_in': 2, 'gm_w_out': 1, 'fox_w_in': 2, 'fox_w_out': 1,
       'sb_w_in': 2, 'sb_w_out': 1, 'cv_w_in': 2, 'cv_w_out': 1}
SMALL_SPLIT = ['cv_b_in', 'cv_dw', 'cv_dw_b', 'cv_ln_g', 'cv_ln_b', 'cv_b_out']
SMALL_REPL = ['mod_b', 'ln1_g', 'ln1_b', 'ln2_g', 'ln2_b', 'gm_b_in', 'gm_ln_g', 'gm_ln_b', 'gm_w_s', 'gm_b_s', 'fox_b_f']
PACK_COLS = 1024


_names_used = {}


def _unique(name):
    k = _names_used.get(name, 0)
    _names_used[name] = k + 1
    return name if k == 0 else f"{name}_{k}"


def _params(sem):
    return pltpu.CompilerParams(dimension_semantics=sem, vmem_limit_bytes=VMEM_LIMIT_BYTES)


def _pick(dim, pref, mult=LANES):
    if dim <= pref:
        return dim
    best = 0
    for t in range(mult, pref + 1, mult):
        if dim % t == 0:
            best = t
    assert best, (dim, pref)
    return best


def _mesh_pos():
    return lax.axis_index("x"), lax.axis_index("y"), lax.axis_index("c")


AG_COPIES = 7
A2A_COPIES = 3


def _comm_call(name, body, blks, out_shapes, n_sems):
    n = len(blks)
    hbm = pl.BlockSpec(memory_space=pl.ANY)
    return pl.pallas_call(
        body, name=_unique(name), out_shape=out_shapes, in_specs=[hbm] * n, out_specs=[hbm] * n,
        scratch_shapes=[pltpu.SemaphoreType.DMA((n_sems * n,)), pltpu.SemaphoreType.DMA((n_sems * n,)),
                        pltpu.SemaphoreType.DMA((n,))],
    )(*blks)


def _all_gather8(name, blks):
    n = len(blks)

    def body(*refs):
        x_refs, out_refs, (send_sems, recv_sems, local_sems) = refs[:n], refs[n:2 * n], refs[2 * n:]
        x, y, c = _mesh_pos()
        me, sibling = (x, y, c), (x, y, 1 - c)
        chips = [(1 - x, y), (x, 1 - y), (1 - x, 1 - y)]

        def copy(a, k, block, to, from_input=False):
            px, py, pc = block
            slot = out_refs[a].at[4 * px + 2 * py + pc]
            return pltpu.make_async_remote_copy(
                src_ref=x_refs[a] if from_input else slot, dst_ref=slot,
                send_sem=send_sems.at[AG_COPIES * a + k], recv_sem=recv_sems.at[AG_COPIES * a + k],
                device_id=to, device_id_type=pl.DeviceIdType.MESH)

        local, sent = [], []
        for a in range(n):
            local.append(pltpu.make_async_copy(x_refs[a], out_refs[a].at[4 * x + 2 * y + c], local_sems.at[a]))
            local[-1].start()
            first = [copy(a, 0, me, sibling, True)] + [copy(a, 1 + j, me, (*chip, c), True) for j, chip in enumerate(chips)]
            for cp in first:
                cp.start()
            sent += first
        for a in range(n):
            for j, chip in enumerate(chips):
                copy(a, 1 + j, (*chip, c), me).wait_recv()
                sent.append(copy(a, 4 + j, (*chip, c), sibling))
                sent[-1].start()
        for a in range(n):
            copy(a, 0, sibling, me).wait_recv()
            for j, chip in enumerate(chips):
                copy(a, 4 + j, (*chip, 1 - c), me).wait_recv()
        for cp in sent:
            cp.wait_send()
        for cp in local:
            cp.wait()

    return _comm_call(name, body, blks, [jax.ShapeDtypeStruct((N_DEV,) + b.shape, b.dtype) for b in blks], AG_COPIES)


def _sibling_exchange(name, blks):
    n = len(blks)

    def body(*refs):
        x_refs, out_refs, (send_sems, recv_sems, _) = refs[:n], refs[n:2 * n], refs[2 * n:]
        x, y, c = _mesh_pos()
        cps = [pltpu.make_async_remote_copy(src_ref=x_refs[a], dst_ref=out_refs[a], send_sem=send_sems.at[a],
                                            recv_sem=recv_sems.at[a], device_id=(x, y, 1 - c),
                                            device_id_type=pl.DeviceIdType.MESH) for a in range(n)]
        for cp in cps:
            cp.start()
        for cp in cps:
            cp.wait()

    return _comm_call(name, body, blks, [jax.ShapeDtypeStruct(b.shape, b.dtype) for b in blks], 1)


def _chip_all_to_all(name, blks):
    n = len(blks)

    def body(*refs):
        x_refs, out_refs, (send_sems, recv_sems, local_sems) = refs[:n], refs[n:2 * n], refs[2 * n:]
        x, y, c = _mesh_pos()
        chips = [(1 - x, y), (x, 1 - y), (1 - x, 1 - y)]
        my_q = 2 * x + y

        def copy(a, j, src_q, dst_q):
            px, py = chips[j]
            return pltpu.make_async_remote_copy(
                src_ref=x_refs[a].at[src_q], dst_ref=out_refs[a].at[dst_q],
                send_sem=send_sems.at[A2A_COPIES * a + j], recv_sem=recv_sems.at[A2A_COPIES * a + j],
                device_id=(px, py, c), device_id_type=pl.DeviceIdType.MESH)

        local, sent = [], []
        for a in range(n):
            local.append(pltpu.make_async_copy(x_refs[a].at[my_q], out_refs[a].at[my_q], local_sems.at[a]))
            local[-1].start()
            sent += [copy(a, j, 2 * px + py, my_q) for j, (px, py) in enumerate(chips)]
            for cp in sent[-A2A_COPIES:]:
                cp.start()
        for a in range(n):
            for j, (px, py) in enumerate(chips):
                copy(a, j, my_q, 2 * px + py).wait_recv()
        for cp in sent:
            cp.wait_send()
        for cp in local:
            cp.wait()

    return _comm_call(name, body, blks, [jax.ShapeDtypeStruct(b.shape, b.dtype) for b in blks], A2A_COPIES)


def _rowwise(name, fn, rows, bvecs=(), cvecs=(), out_rows=(), out_bsums=(), out_tsums=(), tm=256):
    B, S = rows[0].shape[:2]
    tm = _pick(S, tm, SUBLANES)
    n_r, n_b, n_c = len(rows), len(bvecs), len(cvecs)
    n_or, n_ob = len(out_rows), len(out_bsums)

    def body(*refs):
        ins, outs = refs[:n_r + n_b + n_c], refs[n_r + n_b + n_c:]
        r = [ref[0] for ref in ins[:n_r]]
        bv = [ref[0] for ref in ins[n_r:n_r + n_b]]
        cv = [ref[...] for ref in ins[n_r + n_b:]]
        o_rows, o_bsums, o_tsums = fn(r, bv, cv)
        b, i = pl.program_id(0), pl.program_id(1)
        for ref, val in zip(outs[:n_or], o_rows):
            ref[0] = val.astype(ref.dtype)
        for ref, val in zip(outs[n_or:n_or + n_ob], o_bsums):
            @pl.when(i == 0)
            def _(ref=ref, val=val):
                ref[0] = val

            @pl.when(i > 0)
            def _(ref=ref, val=val):
                ref[0] += val
        for ref, val in zip(outs[n_or + n_ob:], o_tsums):
            first = jnp.logical_and(b == 0, i == 0)

            @pl.when(first)
            def _(ref=ref, val=val):
                ref[...] = val

            @pl.when(jnp.logical_not(first))
            def _(ref=ref, val=val):
                ref[...] += val

    in_specs = [pl.BlockSpec((1, tm, a.shape[2]), lambda b, i: (b, i, 0)) for a in rows]
    in_specs += [pl.BlockSpec((1, 1, a.shape[2]), lambda b, i: (b, 0, 0)) for a in bvecs]
    in_specs += [pl.BlockSpec((1, a.shape[1]), lambda b, i: (0, 0)) for a in cvecs]
    out_shape = [jax.ShapeDtypeStruct((B, S, cdim), dt) for cdim, dt in out_rows]
    out_specs = [pl.BlockSpec((1, tm, cdim), lambda b, i: (b, i, 0)) for cdim, _ in out_rows]
    out_shape += [jax.ShapeDtypeStruct((B, 1, cdim), F32) for cdim in out_bsums]
    out_specs += [pl.BlockSpec((1, 1, cdim), lambda b, i: (b, 0, 0)) for cdim in out_bsums]
    out_shape += [jax.ShapeDtypeStruct((1, cdim), F32) for cdim in out_tsums]
    out_specs += [pl.BlockSpec((1, cdim), lambda b, i: (0, 0)) for cdim in out_tsums]
    sem = ("arbitrary", "arbitrary") if out_tsums else ("parallel", "arbitrary")
    res = pl.pallas_call(body, name=_unique(name), grid=(B, S // tm), in_specs=in_specs, out_specs=out_specs,
                         out_shape=out_shape, compiler_params=_params(sem))(*rows, *bvecs, *cvecs)
    return list(res)


MM_TILE = 1536
MM_ROWS = 512
MM_WEIGHT_TILE_BYTES = 12 * 1024 * 1024


def _mm(name, a, b, ta=False, tb=False, out_dtype=F32):
    M, K = (a.shape[1], a.shape[0]) if ta else a.shape
    N = b.shape[0] if tb else b.shape[1]
    assert (b.shape[1] if tb else b.shape[0]) == K, (a.shape, b.shape, ta, tb)
    tn = _pick(N, MM_TILE)
    if ta:
        tm, tk = _pick(M, MM_TILE), _pick(K, 2 * MM_ROWS, LANES if tb else SUBLANES)
    else:
        tm = _pick(M, MM_ROWS, SUBLANES)
        tk = K if K * tn * 2 <= MM_WEIGHT_TILE_BYTES else _pick(K, MM_TILE)
    nk = K // tk
    dims = (((0 if ta else 1,), (1 if tb else 0,)), ((), ()))

    def body(a_ref, b_ref, o_ref, acc_ref):
        k = pl.program_id(2)
        p = lax.dot_general(a_ref[...].astype(BF16), b_ref[...].astype(BF16), dims, preferred_element_type=F32)
        if nk == 1:
            o_ref[...] = p.astype(o_ref.dtype)
        else:
            @pl.when(k == 0)
            def _():
                acc_ref[...] = p

            @pl.when(k > 0)
            def _():
                acc_ref[...] += p

            @pl.when(k == nk - 1)
            def _():
                o_ref[...] = acc_ref[...].astype(o_ref.dtype)

    a_spec = pl.BlockSpec((tk, tm), lambda j, i, k: (k, i)) if ta else pl.BlockSpec((tm, tk), lambda j, i, k: (i, k))
    b_spec = pl.BlockSpec((tn, tk), lambda j, i, k: (j, k)) if tb else pl.BlockSpec((tk, tn), lambda j, i, k: (k, j))
    return pl.pallas_call(
        body, name=_unique(name), grid=(N // tn, M // tm, nk), in_specs=[a_spec, b_spec],
        out_specs=pl.BlockSpec((tm, tn), lambda j, i, k: (i, j)),
        out_shape=jax.ShapeDtypeStruct((M, N), out_dtype),
        scratch_shapes=[pltpu.VMEM((tm, tn) if nk > 1 else (SUBLANES, LANES), F32)],
        compiler_params=_params(("parallel", "parallel", "arbitrary")))(a, b)


def _silu(x):
    return x * _sigmoid(x)


def _sigmoid(x):
    return 1.0 / (1.0 + jnp.exp(-x))


def _dsilu(x):
    s = _sigmoid(x)
    return s * (1.0 + x * (1.0 - s))


def _gelu(x):
    return 0.5 * x * (1.0 + lax.erf(x * np.float32(math.sqrt(0.5))))


def _dgelu(x):
    cdf = 0.5 * (1.0 + lax.erf(x * np.float32(math.sqrt(0.5))))
    pdf = jnp.exp(-0.5 * x * x) * np.float32(1.0 / math.sqrt(2.0 * math.pi))
    return cdf + x * pdf


def _ln_stats(r):
    mu = jnp.mean(r, axis=-1, keepdims=True)
    xc = r - mu
    var = jnp.mean(xc * xc, axis=-1, keepdims=True)
    rstd = lax.rsqrt(var + LN_EPS)
    return xc * rstd, rstd


def _ln_bwd(dxhat, xhat, rstd):
    m1 = jnp.mean(dxhat, axis=-1, keepdims=True)
    m2 = jnp.mean(dxhat * xhat, axis=-1, keepdims=True)
    return rstd * (dxhat - m1 - xhat * m2)


def _csum(v):
    return jnp.sum(v, axis=0, keepdims=True)


def _split_dot(x, m01, lhs01=False, terms=2):
    acc, rem = None, x
    for _ in range(terms):
        part = rem.astype(BF16)
        rem = rem - part.astype(F32)
        d = jnp.dot(m01, part, preferred_element_type=F32) if lhs01 else jnp.dot(part, m01, preferred_element_type=F32)
        acc = d if acc is None else acc + d
    return acc


def _iota2(shape, dim):
    return lax.broadcasted_iota(jnp.int32, shape, dim)


def _modulate(name, x, sc, sh):
    D = x.shape[2]
    return _rowwise(name, lambda r, bv, cv: ([r[0] * (1.0 + bv[0]) + bv[1]], [], []),
                    [x], [sc, sh], [], out_rows=[(D, BF16)])[0]


def _resid_ln(name, alpha, x, y, g, ln_g, ln_b, ybias=None):
    D = x.shape[2]

    def fn(r, bv, cv):
        yy = r[1] if ybias is None else r[1] + cv[2]
        xhat, _ = _ln_stats(alpha * r[0] + (1.0 + bv[0]) * yy)
        return [xhat * cv[0] + cv[1]], [], []
    cvecs = [ln_g, ln_b] + ([] if ybias is None else [ybias])
    return _rowwise(name, fn, [x, y], [g], cvecs, out_rows=[(D, F32)])[0]


def _resid_ln_bwd(name, alpha, dxn, x, y, g, ln_g, ybias=None):
    D = x.shape[2]

    def fn(r, bv, cv):
        yy = r[2] if ybias is None else r[2] + cv[1]
        xhat, rstd = _ln_stats(alpha * r[1] + (1.0 + bv[0]) * yy)
        dr = _ln_bwd(r[0] * cv[0], xhat, rstd)
        dy = (1.0 + bv[0]) * dr
        return [dr, dy], [_csum(dr * yy)], [_csum(r[0] * xhat), _csum(r[0]), _csum(dy)]
    cvecs = [ln_g] + ([] if ybias is None else [ybias])
    return _rowwise(name, fn, [dxn, x, y], [g], cvecs, out_rows=[(D, F32), (D, BF16)], out_bsums=[D], out_tsums=[D, D, D])


def _modulate_bwd(name, alpha, dh, dr, x, sc):
    D = x.shape[2]

    def fn(r, bv, cv):
        return [alpha * r[1] + r[0] * (1.0 + bv[0])], [_csum(r[0] * r[2]), _csum(r[0])], []
    return _rowwise(name, fn, [dh, dr, x], [sc], [], out_rows=[(D, F32)], out_bsums=[D, D])


def _loss_head(name, y, target):
    D = y.shape[2]

    def fn(r, bv, cv):
        e = r[0] - r[1]
        return [e * np.float32(1.0 / D)], [_csum(e * e)], []
    return _rowwise(name, fn, [y, target], [], [], out_rows=[(D, F32)], out_bsums=[D])


def _swiglu_act(name, z):
    Hd = z.shape[2] // 2
    return _rowwise(name, lambda r, bv, cv: ([_silu(r[0][:, :Hd]) * r[0][:, Hd:]], [], []), [z], out_rows=[(Hd, BF16)])[0]


def _swiglu_act_bwd(name, da, z):
    Hd = z.shape[2] // 2

    def fn(r, bv, cv):
        gg, u = r[1][:, :Hd], r[1][:, Hd:]
        return [jnp.concatenate([r[0] * u * _dsilu(gg), r[0] * _silu(gg)], axis=1)], [], []
    return _rowwise(name, fn, [da, z], out_rows=[(2 * Hd, BF16)])[0]


def _gm_act(name, zin, b_in, ln_g, ln_b):
    W = zin.shape[2] // 2

    def fn(r, bv, cv):
        z = _gelu(r[0] + cv[0])
        vhat, _ = _ln_stats(z[:, W:])
        return [z[:, :W], vhat * cv[1] + cv[2]], [], []
    return _rowwise(name, fn, [zin], [], [b_in, ln_g, ln_b], out_rows=[(W, F32), (W, BF16)])


def _gm_act_bwd(name, zin, du, dvn, b_in, ln_g):
    W = zin.shape[2] // 2

    def fn(r, bv, cv):
        zz = r[0] + cv[0]
        z = _gelu(zz)
        vhat, rstd = _ln_stats(z[:, W:])
        dv = _ln_bwd(r[2] * cv[1], vhat, rstd)
        dzin = jnp.concatenate([r[1], dv], axis=1) * _dgelu(zz)
        return [dzin], [], [_csum(dzin), _csum(r[2] * vhat), _csum(r[2])]
    return _rowwise(name, fn, [zin, du, dvn], [], [b_in, ln_g], out_rows=[(2 * W, BF16)], out_tsums=[2 * W, W, W])


def _gm_causal_w(ws_ref, g):
    T = ws_ref.shape[1]
    return jnp.where(_iota2((T, T), 1) <= _iota2((T, T), 0), ws_ref[g], 0.0).astype(BF16)


def _gm_spatial(name, u, vn, w_s, b_sT):
    B, S, W = u.shape
    G, T = w_s.shape[0], w_s.shape[1]
    assert W == G * T, "a head group is as wide as a chunk is long"

    def body(u_ref, vn_ref, ws_ref, bs_ref, y_ref):
        for g in range(G):
            cs = slice(g * T, (g + 1) * T)
            sv = jnp.dot(_gm_causal_w(ws_ref, g), vn_ref[0, :, cs], preferred_element_type=F32) + bs_ref[:, g:g + 1]
            y_ref[0, :, cs] = (u_ref[0, :, cs] * sv).astype(y_ref.dtype)

    row = pl.BlockSpec((1, T, W), lambda b, i: (b, i, 0))
    return pl.pallas_call(
        body, name=_unique(name), grid=(B, S // T),
        in_specs=[row, row, pl.BlockSpec((G, T, T), lambda b, i: (0, 0, 0)), pl.BlockSpec((T, G), lambda b, i: (0, 0))],
        out_specs=row, out_shape=jax.ShapeDtypeStruct((B, S, W), BF16),
        compiler_params=_params(("parallel", "parallel")))(u, vn, w_s, b_sT)


def _gm_spatial_bwd(name, dyg, u, vn, w_s, b_sT):
    B, S, W = u.shape
    G, T = w_s.shape[0], w_s.shape[1]
    assert W == G * T, "a head group is as wide as a chunk is long"

    def body(dy_ref, u_ref, vn_ref, ws_ref, bs_ref, du_ref, dvn_ref, dws_ref, dbs_ref):
        first = jnp.logical_and(pl.program_id(0) == 0, pl.program_id(1) == 0)

        @pl.when(first)
        def _():
            dws_ref[...] = jnp.zeros_like(dws_ref)
            dbs_ref[...] = jnp.zeros_like(dbs_ref)

        tril = _iota2((T, T), 1) <= _iota2((T, T), 0)
        for g in range(G):
            cs = slice(g * T, (g + 1) * T)
            wm = _gm_causal_w(ws_ref, g)
            vng = vn_ref[0, :, cs]
            sv = jnp.dot(wm, vng, preferred_element_type=F32) + bs_ref[:, g:g + 1]
            dy = dy_ref[0, :, cs]
            du_ref[0, :, cs] = dy * sv
            dsv = dy * u_ref[0, :, cs]
            dsv16 = dsv.astype(BF16)
            dvn_ref[0, :, cs] = lax.dot_general(wm, dsv16, (((0,), (0,)), ((), ())), preferred_element_type=F32)
            dw = lax.dot_general(dsv16, vng, (((1,), (1,)), ((), ())), preferred_element_type=F32)
            dws_ref[g] += jnp.where(tril, dw, 0.0)
            dbs_ref[:, g:g + 1] += jnp.sum(dsv, axis=1, keepdims=True)

    row = pl.BlockSpec((1, T, W), lambda b, i: (b, i, 0))
    return pl.pallas_call(
        body, name=_unique(name), grid=(B, S // T),
        in_specs=[row, row, row, pl.BlockSpec((G, T, T), lambda b, i: (0, 0, 0)), pl.BlockSpec((T, G), lambda b, i: (0, 0))],
        out_specs=[row, row, pl.BlockSpec((G, T, T), lambda b, i: (0, 0, 0)), pl.BlockSpec((T, G), lambda b, i: (0, 0))],
        out_shape=[jax.ShapeDtypeStruct((B, S, W), F32), jax.ShapeDtypeStruct((B, S, W), F32),
                   jax.ShapeDtypeStruct((G, T, T), F32), jax.ShapeDtypeStruct((T, G), F32)],
        compiler_params=_params(("arbitrary", "arbitrary")))(dyg, u, vn, w_s, b_sT)


def _cv_glu(name, pw, b_in):
    W = pw.shape[2] // 2

    def fn(r, bv, cv):
        z = r[0] + cv[0]
        return [z[:, :W] * _sigmoid(z[:, W:])], [], []
    return _rowwise(name, fn, [pw], [], [b_in], out_rows=[(W, F32)])[0]


def _cv_glu_bwd(name, pw, dyg, b_in):
    W = pw.shape[2] // 2

    def fn(r, bv, cv):
        z = r[0] + cv[0]
        a, s = z[:, :W], _sigmoid(z[:, W:])
        dpw = jnp.concatenate([r[1] * s, r[1] * a * s * (1.0 - s)], axis=1)
        return [dpw], [], [_csum(dpw)]
    return _rowwise(name, fn, [pw, dyg], [], [b_in], out_rows=[(2 * W, BF16)], out_tsums=[2 * W])


def _cv_ln_act(name, yc, ln_g, ln_b):
    D = yc.shape[2]

    def fn(r, bv, cv):
        xhat, _ = _ln_stats(r[0])
        return [_silu(xhat * cv[0] + cv[1])], [], []
    return _rowwise(name, fn, [yc], [], [ln_g, ln_b], out_rows=[(D, BF16)])[0]


def _cv_ln_act_bwd(name, yc, dys, ln_g, ln_b):
    D = yc.shape[2]

    def fn(r, bv, cv):
        xhat, rstd = _ln_stats(r[0])
        dyn = r[1] * _dsilu(xhat * cv[0] + cv[1])
        dyc = _ln_bwd(dyn * cv[0], xhat, rstd)
        return [dyc], [], [_csum(dyn * xhat), _csum(dyn), _csum(dyc)]
    return _rowwise(name, fn, [yc, dys], [], [ln_g, ln_b], out_rows=[(D, F32)], out_tsums=[D, D, D])


CONV_HALO = 32
CONV_TS, CONV_TC = 256, 128


def _dwconv(name, y, dw, dw_b):
    B, S, D = y.shape
    ts, tc, halo, K = _pick(S, CONV_TS, SUBLANES), _pick(D, CONV_TC), CONV_HALO, CONV_WIDTH

    def body(cur_ref, prev_ref, dw_ref, b_ref, o_ref, buf):
        i = pl.program_id(1)
        buf[pl.ds(0, halo), :] = jnp.where(i > 0, prev_ref[0, pl.ds(ts - halo, halo), :], 0.0)
        buf[pl.ds(halo, ts), :] = cur_ref[0]
        acc = jnp.zeros((ts, tc), F32) + b_ref[...]
        for k in range(K):
            acc = acc + dw_ref[k:k + 1, :] * buf[pl.ds(halo - (K - 1) + k, ts), :]
        o_ref[0] = acc

    return pl.pallas_call(
        body, name=_unique(name), grid=(B, S // ts, D // tc),
        in_specs=[pl.BlockSpec((1, ts, tc), lambda b, i, j: (b, i, j)),
                  pl.BlockSpec((1, ts, tc), lambda b, i, j: (b, jnp.maximum(i - 1, 0), j)),
                  pl.BlockSpec((halo, tc), lambda b, i, j: (0, j)), pl.BlockSpec((1, tc), lambda b, i, j: (0, j))],
        out_specs=pl.BlockSpec((1, ts, tc), lambda b, i, j: (b, i, j)),
        out_shape=jax.ShapeDtypeStruct((B, S, D), F32),
        scratch_shapes=[pltpu.VMEM((halo + ts, tc), F32)],
        compiler_params=_params(("parallel", "parallel", "parallel")))(y, y, dw, dw_b)


def _dwconv_bwd(name, dyc, y, dw):
    B, S, D = y.shape
    ts, tc, halo, K = _pick(S, CONV_TS, SUBLANES), _pick(D, CONV_TC), CONV_HALO, CONV_WIDTH
    nt = S // ts

    def body(g_ref, gnext_ref, y_ref, yprev_ref, dw_ref, dy_ref, ddw_ref, gbuf, ybuf):
        b, i = pl.program_id(1), pl.program_id(2)
        first = jnp.logical_and(b == 0, i == 0)

        @pl.when(first)
        def _():
            ddw_ref[...] = jnp.zeros_like(ddw_ref)

        g = g_ref[0]
        gbuf[pl.ds(0, ts), :] = g
        gbuf[pl.ds(ts, halo), :] = jnp.where(i < nt - 1, gnext_ref[0, pl.ds(0, halo), :], 0.0)
        ybuf[pl.ds(0, halo), :] = jnp.where(i > 0, yprev_ref[0, pl.ds(ts - halo, halo), :], 0.0)
        ybuf[pl.ds(halo, ts), :] = y_ref[0]
        acc = jnp.zeros((ts, tc), F32)
        for k in range(K):
            acc = acc + dw_ref[k:k + 1, :] * gbuf[pl.ds(K - 1 - k, ts), :]
            ddw_ref[k:k + 1, :] += _csum(g * ybuf[pl.ds(halo - (K - 1) + k, ts), :])
        dy_ref[0] = acc

    tile = lambda f: pl.BlockSpec((1, ts, tc), f)
    return pl.pallas_call(
        body, name=_unique(name), grid=(D // tc, B, nt),
        in_specs=[tile(lambda j, b, i: (b, i, j)), tile(lambda j, b, i: (b, jnp.minimum(i + 1, nt - 1), j)),
                  tile(lambda j, b, i: (b, i, j)), tile(lambda j, b, i: (b, jnp.maximum(i - 1, 0), j)),
                  pl.BlockSpec((halo, tc), lambda j, b, i: (0, j))],
        out_specs=[tile(lambda j, b, i: (b, i, j)), pl.BlockSpec((halo, tc), lambda j, b, i: (0, j))],
        out_shape=[jax.ShapeDtypeStruct((B, S, D), F32), jax.ShapeDtypeStruct((halo, D), F32)],
        scratch_shapes=[pltpu.VMEM((ts + halo, tc), F32), pltpu.VMEM((halo + ts, tc), F32)],
        compiler_params=_params(("parallel", "arbitrary", "arbitrary")))(dyc, dyc, y, y, dw)


ATT_BLOCK = 128
ATT_QUERY_BLOCK = 256
FOX_KEY_BLOCK = 1024
SB_KEY_BLOCK = 512
ATT_PIECE_ROWS = 32
FOX_GATE_COLS = 128


def _att_tiles(S, key_block):
    return _pick(S, ATT_QUERY_BLOCK, SUBLANES), _pick(S, key_block, LANES)


def _pieces(T, TK):
    R = min(T, ATT_PIECE_ROWS)
    segs = [slice(c, c + LANES) for c in range(0, TK, LANES)]
    return [(slice(r, r + R), segs) for r in range(0, T, R)]


def _piece_keep(row0, col0, rs, cs, strict, lane_major_of=0):
    shape = (rs.stop - rs.start, cs.stop - cs.start)
    lane = _iota2(shape, 1)
    key = col0 + (lane * lane_major_of + cs.start // LANES if lane_major_of else cs.start + lane)
    qry = row0 + rs.start + _iota2(shape, 0)
    return key < qry if strict else key <= qry


def _causal_tiles(i, tq, tk):
    return (i * tq + tq + tk - 1) // tk


def _lane_major(t, tk):
    B, H, S, dh = t.shape
    return t.reshape(B, H, S // tk, LANES, tk // LANES, dh).swapaxes(3, 4).reshape(B, H, S, dh)


def _lane_major_inverse(t, tk):
    B, H, S, dh = t.shape
    return t.reshape(B, H, S // tk, tk // LANES, LANES, dh).swapaxes(3, 4).reshape(B, H, S, dh)


def _log_sigmoid(x):
    return jnp.minimum(x, 0.0) - jnp.log(1.0 + jnp.exp(-jnp.abs(x)))


def _fox_gate_cumsum(name, fl, b_f):
    B, S, C = fl.shape
    T = _pick(S, ATT_BLOCK, SUBLANES)

    def body(fl_ref, bf_ref, f_ref, carry):
        @pl.when(pl.program_id(1) == 0)
        def _():
            carry[...] = jnp.zeros_like(carry)
        lf = _log_sigmoid(fl_ref[0] + bf_ref[...])
        lower = (_iota2((T, T), 1) <= _iota2((T, T), 0)).astype(BF16)
        f = _split_dot(lf, lower, lhs01=True, terms=3) + carry[...]
        f_ref[0] = f
        carry[...] = f[T - 1:T, :]

    return pl.pallas_call(
        body, name=_unique(name), grid=(B, S // T),
        in_specs=[pl.BlockSpec((1, T, C), lambda b, i: (b, i, 0)), pl.BlockSpec((1, C), lambda b, i: (0, 0))],
        out_specs=pl.BlockSpec((1, T, C), lambda b, i: (b, i, 0)),
        out_shape=jax.ShapeDtypeStruct((B, S, C), F32),
        scratch_shapes=[pltpu.VMEM((1, C), F32)],
        compiler_params=_params(("arbitrary", "arbitrary")))(fl, b_f)


def _fox_gate_bwd(name, dF, fl, b_f, n_heads):
    B, S, C = fl.shape
    T = _pick(S, ATT_BLOCK, SUBLANES)
    nt = S // T

    def body(df_ref, fl_ref, bf_ref, dfl_ref, dbf_ref, carry):
        first = jnp.logical_and(pl.program_id(0) == 0, pl.program_id(1) == 0)

        @pl.when(pl.program_id(1) == 0)
        def _():
            carry[...] = jnp.zeros_like(carry)

        @pl.when(first)
        def _():
            dbf_ref[...] = jnp.zeros_like(dbf_ref)

        upper = (_iota2((T, T), 1) >= _iota2((T, T), 0)).astype(BF16)
        dlf = _split_dot(df_ref[0], upper, lhs01=True, terms=3) + carry[...]
        carry[...] = dlf[0:1, :]
        x = fl_ref[0] + bf_ref[...]
        dfl = jnp.where(_iota2((T, C), 1) < n_heads, dlf * _sigmoid(-x), 0.0)
        dfl_ref[0] = dfl
        dbf_ref[...] += _csum(dfl)

    rev = lambda b, i: (b, nt - 1 - i, 0)
    return pl.pallas_call(
        body, name=_unique(name), grid=(B, nt),
        in_specs=[pl.BlockSpec((1, T, C), rev), pl.BlockSpec((1, T, C), rev), pl.BlockSpec((1, C), lambda b, i: (0, 0))],
        out_specs=[pl.BlockSpec((1, T, C), rev), pl.BlockSpec((1, C), lambda b, i: (0, 0))],
        out_shape=[jax.ShapeDtypeStruct((B, S, C), F32), jax.ShapeDtypeStruct((1, C), F32)],
        scratch_shapes=[pltpu.VMEM((1, C), F32)],
        compiler_params=_params(("arbitrary", "arbitrary")))(dF, fl, b_f)


_NT = (((1,), (1,)), ((), ()))
_TN = (((0,), (0,)), ((), ()))


def _fox_fwd(name, q, k, v, fq, fk):
    B, H, S, dh = q.shape
    T, TK = _att_tiles(S, FOX_KEY_BLOCK)
    scale = np.float32(dh ** -0.5)

    pieces = _pieces(T, TK)

    def body(q_ref, k_ref, v_ref, fq_ref, fk_ref, o_ref, lse_ref, s_scr, p_scr):
        i = pl.program_id(2)
        qb = q_ref[0, 0]
        n_tiles = _causal_tiles(i, T, TK)

        def tile(j, carry, diag):
            m, l, acc = carry
            ks = pl.ds(pl.multiple_of(j * TK, TK), TK)
            s_scr[...] = lax.dot_general(qb, k_ref[0, 0, ks, :], _NT, preferred_element_type=F32) * scale
            fkj = fk_ref[0, 0, pl.ds(j, 1), :]
            m_new = []
            for rc, (rs, segs) in enumerate(pieces):
                fq_c, mx = fq_ref[0, 0, rs, :], m[rc]
                for cs in segs:
                    s = s_scr[rs, cs] + fq_c - fkj[:, cs]
                    if diag:
                        s = jnp.where(_piece_keep(i * T, j * TK, rs, cs, False), s, NEG_INF)
                    s_scr[rs, cs] = s
                    mx = jnp.maximum(mx, jnp.max(s, axis=1, keepdims=True))
                m_new.append(mx)
            alpha, l_new = [], []
            for rc, (rs, segs) in enumerate(pieces):
                alpha.append(jnp.exp(m[rc] - m_new[rc]))
                lsum = alpha[rc] * l[rc]
                for cs in segs:
                    p = jnp.exp(s_scr[rs, cs] - m_new[rc])
                    p_scr[rs, cs] = p.astype(BF16)
                    lsum = lsum + jnp.sum(p, axis=1, keepdims=True)
                l_new.append(lsum)
            acc = jnp.concatenate(alpha, axis=0) * acc + jnp.dot(p_scr[...], v_ref[0, 0, ks, :], preferred_element_type=F32)
            return tuple(m_new), tuple(l_new), acc

        init = (tuple(jnp.full((rs.stop - rs.start, 1), NEG_INF, F32) for rs, _ in pieces),
                tuple(jnp.zeros((rs.stop - rs.start, 1), F32) for rs, _ in pieces), jnp.zeros((T, dh), F32))
        carry = lax.fori_loop(0, n_tiles - 1, lambda j, c: tile(j, c, False), init)
        m, l, acc = tile(n_tiles - 1, carry, True)
        m, l = jnp.concatenate(m, axis=0), jnp.concatenate(l, axis=0)
        o_ref[0, 0] = acc / l
        lse_ref[0, 0] = m + jnp.log(l)

    full = lambda w: pl.BlockSpec((1, 1, S, w), lambda b, h, i: (b, h, 0, 0))
    blk = lambda w: pl.BlockSpec((1, 1, T, w), lambda b, h, i: (b, h, i, 0))
    return pl.pallas_call(
        body, name=_unique(name), grid=(B, H, S // T),
        in_specs=[blk(dh), full(dh), full(dh), blk(1), pl.BlockSpec((1, 1, S // TK, TK), lambda b, h, i: (b, h, 0, 0))],
        out_specs=[blk(dh), blk(1)],
        out_shape=[jax.ShapeDtypeStruct((B, H, S, dh), F32), jax.ShapeDtypeStruct((B, H, S, 1), F32)],
        scratch_shapes=[pltpu.VMEM((T, TK), F32), pltpu.VMEM((T, TK), BF16)],
        compiler_params=_params(("parallel", "parallel", "parallel")))(q, k, v, fq, fk)


def _fox_bwd(name, q, k, v, fq, fk, do, lse):
    B, H, S, dh = q.shape
    T, TK = _att_tiles(S, FOX_KEY_BLOCK)
    nt, nkt = S // T, S // TK
    scale = np.float32(dh ** -0.5)

    pieces = _pieces(T, TK)

    def body(q_ref, k_ref, v_ref, fq_ref, fk_ref, do_ref, lse_ref, dq_ref, dk_ref, dv_ref, dfk_ref,
             p_buf, dp_buf, s_scr, ds_scr, p16_scr):
        dk_ref[...] = jnp.zeros_like(dk_ref)
        dv_ref[...] = jnp.zeros_like(dv_ref)
        dfk_ref[...] = jnp.zeros_like(dfk_ref)

        def qloop(i, _):
            qs = pl.ds(pl.multiple_of(i * T, T), T)
            qb, dob16 = q_ref[0, 0, qs, :], do_ref[0, 0, qs, :].astype(BF16)
            n_tiles = _causal_tiles(i, T, TK)
            row_at = lambda rs: pl.ds(pl.multiple_of(i * T + rs.start, SUBLANES), rs.stop - rs.start)
            fq_c = [fq_ref[0, 0, row_at(rs), :] for rs, _ in pieces]
            lse_c = [lse_ref[0, 0, row_at(rs), :] for rs, _ in pieces]

            def sweep1(j, delta, diag):
                ks = pl.ds(pl.multiple_of(j * TK, TK), TK)
                s_scr[...] = lax.dot_general(qb, k_ref[0, 0, ks, :], _NT, preferred_element_type=F32) * scale
                dp_buf[j] = lax.dot_general(dob16, v_ref[0, 0, ks, :], _NT, preferred_element_type=F32)
                fkj = fk_ref[0, 0, pl.ds(j, 1), :]
                out = []
                for rc, (rs, segs) in enumerate(pieces):
                    d = delta[rc]
                    for cs in segs:
                        p = jnp.exp(((s_scr[rs, cs] + fq_c[rc]) - fkj[:, cs]) - lse_c[rc])
                        if diag:
                            p = jnp.where(_piece_keep(i * T, j * TK, rs, cs, False), p, 0.0)
                        p_buf[j, rs, cs] = p
                        d = d + jnp.sum(p * dp_buf[j, rs, cs], axis=1, keepdims=True)
                    out.append(d)
                return tuple(out)

            zeros = tuple(jnp.zeros((rs.stop - rs.start, 1), F32) for rs, _ in pieces)
            delta = lax.fori_loop(0, n_tiles - 1, lambda j, d: sweep1(j, d, False), zeros)
            delta = sweep1(n_tiles - 1, delta, True)

            def sweep2(j, dq):
                ks = pl.ds(pl.multiple_of(j * TK, TK), TK)
                col = [None] * len(pieces[0][1])
                for rc, (rs, segs) in enumerate(pieces):
                    for sg, cs in enumerate(segs):
                        p = p_buf[j, rs, cs]
                        ds = p * (dp_buf[j, rs, cs] - delta[rc])
                        ds_scr[rs, cs] = ds.astype(BF16)
                        p16_scr[rs, cs] = p.astype(BF16)
                        col[sg] = _csum(ds) if col[sg] is None else col[sg] + _csum(ds)
                dfk_ref[0, 0, pl.ds(j, 1), :] -= jnp.concatenate(col, axis=1)
                ds16 = ds_scr[...]
                dk_ref[0, 0, ks, :] += lax.dot_general(ds16, qb, _TN, preferred_element_type=F32)
                dv_ref[0, 0, ks, :] += lax.dot_general(p16_scr[...], dob16, _TN, preferred_element_type=F32)
                return dq + jnp.dot(ds16, k_ref[0, 0, ks, :], preferred_element_type=F32)

            dq_ref[0, 0, qs, :] = lax.fori_loop(0, n_tiles, sweep2, jnp.zeros((T, dh), F32)) * scale
            return 0

        lax.fori_loop(0, nt, qloop, 0)
        dk_ref[...] = dk_ref[...] * scale

    full = lambda w: pl.BlockSpec((1, 1, S, w), lambda b, h: (b, h, 0, 0))
    fks = pl.BlockSpec((1, 1, nkt, TK), lambda b, h: (b, h, 0, 0))
    return pl.pallas_call(
        body, name=_unique(name), grid=(B, H),
        in_specs=[full(dh), full(dh), full(dh), full(1), fks, full(dh), full(1)],
        out_specs=[full(dh), full(dh), full(dh), fks],
        out_shape=[jax.ShapeDtypeStruct((B, H, S, dh), F32)] * 3 + [jax.ShapeDtypeStruct((B, H, nkt, TK), F32)],
        scratch_shapes=[pltpu.VMEM((nkt, T, TK), F32), pltpu.VMEM((nkt, T, TK), F32), pltpu.VMEM((T, TK), F32),
                        pltpu.VMEM((T, TK), BF16), pltpu.VMEM((T, TK), BF16)],
        compiler_params=_params(("parallel", "parallel")))(q, k, v, fq, fk, do, lse)


def _sb_terms(z, with_sigmoids=True):
    t = jnp.exp(-jnp.abs(z))
    lp = jnp.log(1.0 + t)
    lb, l1 = jnp.minimum(z, 0.0) - lp, jnp.minimum(-z, 0.0) - lp
    if not with_sigmoids:
        return lb, l1, None, None
    inv = 1.0 / (1.0 + t)
    pos = z >= 0.0
    return lb, l1, jnp.where(pos, 1.0, t) * inv, jnp.where(pos, t, 1.0) * inv


def _lane_scan(x, reverse):
    lane = _iota2(x.shape, 1)
    y, d = x, 1
    while d < LANES:
        if reverse:
            y = y + jnp.where(lane + d < LANES, pltpu.roll(y, LANES - d, 1), 0.0)
        else:
            y = y + jnp.where(lane >= d, pltpu.roll(y, d, 1), 0.0)
        d *= 2
    return y


def _chunk_scan(xs, reverse):
    n = len(xs)
    within, acc = [None] * n, None
    for s in (range(n - 1, -1, -1) if reverse else range(n)):
        acc = xs[s] if acc is None else acc + xs[s]
        within[s] = acc
    lanes = _lane_scan(acc, reverse)
    beyond = lanes - acc
    return [w + beyond for w in within], (lanes[:, 0:1] if reverse else lanes[:, LANES - 1:LANES])


SB_DEAD = 110.0


def _sb_fwd(name, q, k, v):
    B, H, S, dh = q.shape
    T, TK = _att_tiles(S, SB_KEY_BLOCK)
    scale = np.float32(dh ** -0.5)
    pieces = _pieces(T, TK)

    def body(q_ref, k_ref, v_ref, o_ref, lt_ref, first_ref, z_scr, a_scr):
        i = pl.program_id(2)
        qb = q_ref[0, 0]
        n_tiles = _causal_tiles(i, T, TK)

        def tile(j, carry, diag):
            runs, acc = carry
            ks = pl.ds(pl.multiple_of(j * TK, TK), TK)
            z_scr[...] = lax.dot_general(qb, k_ref[0, 0, ks, :], _NT, preferred_element_type=F32) * scale
            new_runs = []
            for rc, (rs, segs) in enumerate(pieces):
                terms = [_sb_terms(z_scr[rs, cs], False) for cs in segs]
                keep = [_piece_keep(i * T, j * TK, rs, cs, True, len(segs)) if diag else None for cs in segs]
                l1 = [jnp.where(kp, t[1], 0.0) if diag else t[1] for kp, t in zip(keep, terms)]
                right_of, total = _chunk_scan(l1, True)
                for cs, kp, t, x, r in zip(segs, keep, terms, l1, right_of):
                    a = jnp.exp(t[0] + ((r - x) + runs[rc]))
                    a_scr[rs, cs] = (jnp.where(kp, a, 0.0) if diag else a).astype(BF16)
                new_runs.append(runs[rc] + total)
            return tuple(new_runs), acc + jnp.dot(a_scr[...], v_ref[0, 0, ks, :], preferred_element_type=F32)

        init = (tuple(jnp.zeros((rs.stop - rs.start, 1), F32) for rs, _ in pieces), jnp.zeros((T, dh), F32))
        def some_row_alive(runs):
            worst = runs[0]
            for r in runs[1:]:
                worst = jnp.maximum(worst, r)
            return jnp.max(worst) > -SB_DEAD

        def step(c):
            runs, acc = tile(n_tiles - 1 - c[0], (c[1], c[2]), False)
            return c[0] + 1, runs, acc

        visited, runs, acc = lax.while_loop(lambda c: jnp.logical_and(c[0] < n_tiles, some_row_alive(c[1])), step,
                                            (jnp.int32(1), *tile(n_tiles - 1, init, True)))
        o_ref[0, 0] = acc
        lt_ref[0, 0] = jnp.concatenate(runs, axis=0)
        first_ref[pl.program_id(0), pl.program_id(1), i] = (n_tiles - visited).astype(F32)

    full = lambda w: pl.BlockSpec((1, 1, S, w), lambda b, h, i: (b, h, 0, 0))
    blk = lambda w: pl.BlockSpec((1, 1, T, w), lambda b, h, i: (b, h, i, 0))
    return pl.pallas_call(
        body, name=_unique(name), grid=(B, H, S // T),
        in_specs=[blk(dh), full(dh), full(dh)], out_specs=[blk(dh), blk(1), pl.BlockSpec(memory_space=pltpu.SMEM)],
        out_shape=[jax.ShapeDtypeStruct((B, H, S, dh), F32), jax.ShapeDtypeStruct((B, H, S, 1), F32),
                   jax.ShapeDtypeStruct((B, H, S // T), F32)],
        scratch_shapes=[pltpu.VMEM((T, TK), F32), pltpu.VMEM((T, TK), BF16)],
        compiler_params=_params(("arbitrary", "arbitrary", "arbitrary")))(q, k, v)


def _sb_bwd(name, q, k, v, do, lt, first):
    B, H, S, dh = q.shape
    T, TK = _att_tiles(S, SB_KEY_BLOCK)
    nt = S // T
    scale = np.float32(dh ** -0.5)

    pieces = _pieces(T, TK)

    def body(first_ref, q_ref, k_ref, v_ref, do_ref, lt_ref, dq_ref, dk_ref, dv_ref, z_scr, da_scr, dz_scr, a_scr):
        dk_ref[...] = jnp.zeros_like(dk_ref)
        dv_ref[...] = jnp.zeros_like(dv_ref)
        b, h = pl.program_id(0), pl.program_id(1)

        def qloop(i, _):
            qs = pl.ds(pl.multiple_of(i * T, T), T)
            qb, dob16 = q_ref[0, 0, qs, :], do_ref[0, 0, qs, :].astype(BF16)
            n_tiles = _causal_tiles(i, T, TK)
            first_tile = jnp.clip(first_ref[b, h, i].astype(jnp.int32), 0, n_tiles - 1)
            lt_c = [lt_ref[0, 0, pl.ds(pl.multiple_of(i * T + rs.start, SUBLANES), rs.stop - rs.start), :] for rs, _ in pieces]

            def tile(j, carry, diag):
                sums_l, sums_e, dq = carry
                ks = pl.ds(pl.multiple_of(j * TK, TK), TK)
                kb, vb = k_ref[0, 0, ks, :], v_ref[0, 0, ks, :]
                z_scr[...] = lax.dot_general(qb, kb, _NT, preferred_element_type=F32) * scale
                da_scr[...] = lax.dot_general(dob16, vb, _NT, preferred_element_type=F32)
                new_l, new_e = [], []
                for rc, (rs, segs) in enumerate(pieces):
                    terms = [_sb_terms(z_scr[rs, cs]) for cs in segs]
                    keep = [_piece_keep(i * T, j * TK, rs, cs, True, len(segs)) if diag else None for cs in segs]
                    l1 = [jnp.where(kp, t[1], 0.0) if diag else t[1] for kp, t in zip(keep, terms)]
                    upto, total_l = _chunk_scan(l1, False)
                    es = []
                    for cs, kp, t, u in zip(segs, keep, terms, upto):
                        a = jnp.exp(t[0] + (lt_c[rc] - (u + sums_l[rc])))
                        if diag:
                            a = jnp.where(kp, a, 0.0)
                        a_scr[rs, cs] = a.astype(BF16)
                        es.append(da_scr[rs, cs] * a)
                    e_upto, total_e = _chunk_scan(es, False)
                    for cs, kp, t, e, eu in zip(segs, keep, terms, es, e_upto):
                        dz = e * t[3] - ((eu - e) + sums_e[rc]) * t[2]
                        dz_scr[rs, cs] = (jnp.where(kp, dz, 0.0) if diag else dz).astype(BF16)
                    new_l.append(sums_l[rc] + total_l)
                    new_e.append(sums_e[rc] + total_e)
                dz16 = dz_scr[...]
                dk_ref[0, 0, ks, :] += lax.dot_general(dz16, qb, _TN, preferred_element_type=F32)
                dv_ref[0, 0, ks, :] += lax.dot_general(a_scr[...], dob16, _TN, preferred_element_type=F32)
                return tuple(new_l), tuple(new_e), dq + jnp.dot(dz16, kb, preferred_element_type=F32)

            zeros = tuple(jnp.zeros((rs.stop - rs.start, 1), F32) for rs, _ in pieces)
            carry = lax.fori_loop(first_tile, n_tiles - 1, lambda j, c: tile(j, c, False), (zeros, zeros, jnp.zeros((T, dh), F32)))
            dq_ref[0, 0, qs, :] = tile(n_tiles - 1, carry, True)[2] * scale
            return 0

        lax.fori_loop(0, nt, qloop, 0)
        dk_ref[...] = dk_ref[...] * scale

    full = lambda w: pl.BlockSpec((1, 1, S, w), lambda b, h: (b, h, 0, 0))
    return pl.pallas_call(
        body, name=_unique(name), grid=(B, H),
        in_specs=[pl.BlockSpec(memory_space=pltpu.SMEM), full(dh), full(dh), full(dh), full(dh), full(1)], out_specs=[full(dh)] * 3,
        out_shape=[jax.ShapeDtypeStruct((B, H, S, dh), F32)] * 3,
        scratch_shapes=[pltpu.VMEM((T, TK), F32), pltpu.VMEM((T, TK), F32), pltpu.VMEM((T, TK), BF16), pltpu.VMEM((T, TK), BF16)],
        compiler_params=_params(("parallel", "parallel")))(first, q, k, v, do, lt)


def _adamw(name, w, g, m, v):
    shape = w.shape
    cols = shape[-1] if (w.ndim >= 2 and shape[-1] % LANES == 0) else 0
    n = w.size
    if cols:
        prep = lambda t: t.reshape(1, n // cols, cols)
    else:
        cols = LANES
        pad = (-n) % (SUBLANES * LANES)
        prep = lambda t: jnp.pad(t.reshape(-1), (0, pad), constant_values=1.0).reshape(1, (n + pad) // cols, cols)

    def fn(r, bv, cv):
        w_, g_, m_, v_ = r
        m2 = ADAM_B1 * m_ + (1.0 - ADAM_B1) * g_
        v2 = ADAM_B2 * v_ + (1.0 - ADAM_B2) * (g_ * g_)
        m_hat = m2 / (1.0 - ADAM_B1 ** ADAM_STEP)
        v_hat = v2 / (1.0 - ADAM_B2 ** ADAM_STEP)
        return [-ADAM_LR * (m_hat / (jnp.sqrt(v_hat) + ADAM_EPS) + ADAM_WD * w_), m2, v2], [], []
    outs = _rowwise(name, fn, [prep(w), prep(g), prep(m), prep(v)], out_rows=[(cols, F32)] * 3, tm=512)
    return [o.reshape(-1)[:n].reshape(shape) for o in outs]


def _sum8(name, parts):
    def fn(r, bv, cv):
        s = r[0]
        for t in r[1:]:
            s = s + t
        return [s], [], []
    rows = [parts[i][None] for i in range(parts.shape[0])]
    return _rowwise(name, fn, rows, out_rows=[(parts.shape[2], F32)])[0][0]


def _pack(arrs, cols, dtype, row_mult):
    flat = jnp.concatenate([a.reshape(-1).astype(dtype) for a in arrs])
    pad = (-flat.size) % (cols * row_mult)
    return jnp.pad(flat, (0, pad)).reshape(-1, cols)


def _unpack(flat, shapes):
    out, off = [], 0
    for s in shapes:
        n = int(np.prod(s))
        out.append(flat[off:off + n].reshape(s))
        off += n
    return out


def _heads(t, H):
    B, S, W = t.shape
    return t.reshape(B, S, H, W // H).transpose(0, 2, 1, 3)


def _unheads(t):
    B, H, S, dh = t.shape
    return t.transpose(0, 2, 1, 3).reshape(B, S, H * dh)


def kernel(*args):
    _names_used.clear()
    p = dict(zip(ARGS, args))
    x, target = p['x'], p['loss_target']
    B, S, D = x.shape
    T = B * S
    depth = p['ln1_g'].shape[0]
    H = D // HEAD_DIM
    alpha = np.float32((2.0 * depth) ** 0.25)
    cx, cy, cc = _mesh_pos()
    my_q = 2 * cx + cy
    axes = ("x", "y", "c")
    two = lambda t: t.reshape(T, t.shape[-1])
    three = lambda t: t.reshape(B, S, t.shape[-1])

    small_in = [p['c']] + [p[n] for n in SMALL_SPLIT]
    g1 = _all_gather8("ag_small", [_pack(small_in, LANES, F32, SUBLANES)])[0]
    g1 = g1.reshape(N_DEV, -1)
    c_all = g1[:, :B * D].reshape(N_DEV * B, D)
    per_chip = [_unpack(g1[2 * q], [a.shape for a in small_in])[1:] for q in range(N_CHIPS)]
    small = {n: jnp.concatenate([per_chip[q][i] for q in range(N_CHIPS)], axis=-1) for i, n in enumerate(SMALL_SPLIT)}
    for n in SMALL_REPL:
        small[n] = p[n]

    n_seq = N_DEV * B
    seq_pad = -(-n_seq // LANES) * LANES
    c_act = _rowwise("c_act", lambda r, bv, cv: ([_silu(r[0])], [], []),
                     [jnp.pad(c_all, ((0, seq_pad - n_seq), (0, 0)))[None]], out_rows=[(D, F32)])[0][0]
    mod_cols = p['mod_w'].shape[2]
    mod_part = jnp.stack([_mm(f"mod_fwd{l}", c_act, p['mod_w'][l])[:n_seq] for l in range(depth)])
    half_layers = depth // 2
    mod_half = lax.dynamic_slice_in_dim(mod_part, cc * half_layers, half_layers, axis=0)
    gm_ = _all_gather8("ag_mod", [mod_half.reshape(half_layers * n_seq, mod_cols)])[0]
    mod_all = gm_.reshape(N_CHIPS, 2, half_layers, n_seq, mod_cols).transpose(1, 2, 3, 0, 4).reshape(depth, n_seq, 6 * D)
    mod_mine = lax.dynamic_slice_in_dim(mod_all, (2 * my_q + cc) * B, B, axis=1)
    mod = _rowwise("mod_bias", lambda r, bv, cv: ([r[0] + bv[0]], [], []), [mod_mine], [p['mod_b'][:, None, :]],
                   out_rows=[(6 * D, F32)])[0]
    mods = [[mod[l, :, None, i * D:(i + 1) * D] for i in range(6)] for l in range(depth)]

    big_names = list(BIG)
    shard_shapes = [p[n].shape for n in big_names]
    half_rows = [s[0] * s[1] // 2 for s in shard_shapes]
    w_halves = [lax.dynamic_slice_in_dim(p[n].reshape(-1, s[2]), cc * hr, hr, axis=0).astype(BF16)
                for n, s, hr in zip(big_names, shard_shapes, half_rows)]
    W = {}
    for n, s, g in zip(big_names, shard_shapes, _all_gather8("ag_weights", w_halves)):
        seg = g.reshape((N_CHIPS,) + s)
        W[n] = jnp.concatenate([seg[q] for q in range(N_CHIPS)], axis=BIG[n])

    def vec(n, j):
        return small[n][j][None, :]

    def attn_proj(h1, w_in, gate_cols):
        wp = jnp.pad(w_in, ((0, 0), (0, gate_cols))) if gate_cols else w_in
        proj = three(_mm("att_proj", two(h1), wp))
        q, k, v = [_heads(proj[..., i * D:(i + 1) * D].astype(BF16), H) for i in range(3)]
        return wp, proj, q, k, v

    def gm_fwd(j, h1):
        zin = three(_mm("gm_in", two(h1), W['gm_w_in'][j]))
        u, vn = _gm_act("gm_act", zin, vec('gm_b_in', j), vec('gm_ln_g', j), vec('gm_ln_b', j))
        b_sT = small['gm_b_s'][j].T
        yg = _gm_spatial("gm_spatial", u, vn, small['gm_w_s'][j], b_sT)
        return three(_mm("gm_out", two(yg), W['gm_w_out'][j])), (zin, u, vn, b_sT, yg)

    def gm_bwd(j, h1, dy1, cache):
        zin, u, vn, b_sT, yg = cache
        g = {'gm_w_out': _mm("gm_dwout", two(yg), two(dy1), ta=True)}
        dyg = three(_mm("gm_dyg", two(dy1), W['gm_w_out'][j], tb=True))
        du, dvn, dws, dbsT = _gm_spatial_bwd("gm_spatial_bwd", dyg, u, vn, small['gm_w_s'][j], b_sT)
        dzin, g['gm_b_in'], g['gm_ln_g'], g['gm_ln_b'] = _gm_act_bwd("gm_act_bwd", zin, du, dvn, vec('gm_b_in', j), vec('gm_ln_g', j))
        g['gm_w_s'], g['gm_b_s'] = dws, dbsT.T
        g['gm_w_in'] = _mm("gm_dwin", two(h1), two(dzin), ta=True)
        return _mm("gm_dh", two(dzin), W['gm_w_in'][j], tb=True), g

    def fox_fwd(j, h1):
        wp, proj, q, k, v = attn_proj(h1, W['fox_w_in'][j], 3 * D + FOX_GATE_COLS - W['fox_w_in'].shape[2])
        fl = proj[..., 3 * D:]
        bf = jnp.pad(small['fox_b_f'][j][None, :], ((0, 0), (0, FOX_GATE_COLS - H)))
        Fh = _fox_gate_cumsum("fox_gate", fl, bf)[..., :H].transpose(0, 2, 1)
        fq, fk = Fh[..., None], Fh.reshape(B, H, -1, _att_tiles(S, FOX_KEY_BLOCK)[1])
        o, lse = _fox_fwd("fox_fwd", q, k, v, fq, fk)
        o2 = _unheads(o)
        return three(_mm("fox_out", two(o2), W['fox_w_out'][j])), (wp, q, k, v, fl, bf, fq, fk, lse, o2)

    def fox_bwd(j, h1, dy1, cache):
        wp, q, k, v, fl, bf, fq, fk, lse, o2 = cache
        g = {'fox_w_out': _mm("fox_dwout", two(o2), two(dy1), ta=True)}
        do = _heads(three(_mm("fox_do", two(dy1), W['fox_w_out'][j], tb=True)), H)
        dq, dk, dv, dfk = _fox_bwd("fox_bwd", q, k, v, fq, fk, do, lse)
        dF = jnp.pad(dfk.reshape(B, H, S).transpose(0, 2, 1), ((0, 0), (0, 0), (0, FOX_GATE_COLS - H)))
        dfl, dbf = _fox_gate_bwd("fox_gate_bwd", dF, fl, bf, H)
        dproj = jnp.concatenate([_unheads(dq).astype(BF16), _unheads(dk).astype(BF16), _unheads(dv).astype(BF16),
                                 dfl.astype(BF16)], axis=-1)
        g['fox_w_in'] = _mm("fox_dwin", two(h1), two(dproj), ta=True)[:, :W['fox_w_in'].shape[2]]
        g['fox_b_f'] = dbf[0, :H]
        return _mm("fox_dh", two(dproj), wp, tb=True), g

    def sb_fwd(j, h1):
        wp, proj, q, k, v = attn_proj(h1, W['sb_w_in'][j], 0)
        k, v = _lane_major(k, _att_tiles(S, SB_KEY_BLOCK)[1]), _lane_major(v, _att_tiles(S, SB_KEY_BLOCK)[1])
        o, lt, first = _sb_fwd("sb_fwd", q, k, v)
        o2 = _unheads(o)
        return three(_mm("sb_out", two(o2), W['sb_w_out'][j])), (q, k, v, lt, first, o2)

    def sb_bwd(j, h1, dy1, cache):
        q, k, v, lt, first, o2 = cache
        g = {'sb_w_out': _mm("sb_dwout", two(o2), two(dy1), ta=True)}
        do = _heads(three(_mm("sb_do", two(dy1), W['sb_w_out'][j], tb=True)), H)
        dq, dk, dv = _sb_bwd("sb_bwd", q, k, v, do, lt, first)
        dk, dv = _lane_major_inverse(dk, _att_tiles(S, SB_KEY_BLOCK)[1]), _lane_major_inverse(dv, _att_tiles(S, SB_KEY_BLOCK)[1])
        dproj = jnp.concatenate([_unheads(dq).astype(BF16), _unheads(dk).astype(BF16), _unheads(dv).astype(BF16)], axis=-1)
        g['sb_w_in'] = _mm("sb_dwin", two(h1), two(dproj), ta=True)
        return _mm("sb_dh", two(dproj), W['sb_w_in'][j], tb=True), g

    def cv_fwd(j, h1):
        pw = three(_mm("cv_in", two(h1), W['cv_w_in'][j]))
        ygl = _cv_glu("cv_glu", pw, vec('cv_b_in', j))
        dw = jnp.pad(small['cv_dw'][j], ((0, CONV_HALO - CONV_WIDTH), (0, 0)))
        yc = _dwconv("cv_dwconv", ygl, dw, vec('cv_dw_b', j))
        ys = _cv_ln_act("cv_ln_act", yc, vec('cv_ln_g', j), vec('cv_ln_b', j))
        return three(_mm("cv_out", two(ys), W['cv_w_out'][j])), (pw, ygl, dw, yc, ys)

    def cv_bwd(j, h1, dy1, cache):
        pw, ygl, dw, yc, ys = cache
        g = {'cv_w_out': _mm("cv_dwout", two(ys), two(dy1), ta=True)}
        dys = three(_mm("cv_dys", two(dy1), W['cv_w_out'][j], tb=True))
        dyc, g['cv_ln_g'], g['cv_ln_b'], g['cv_dw_b'] = _cv_ln_act_bwd("cv_ln_act_bwd", yc, dys, vec('cv_ln_g', j), vec('cv_ln_b', j))
        dygl, ddw = _dwconv_bwd("cv_dwconv_bwd", dyc, ygl, dw)
        g['cv_dw'] = ddw[:CONV_WIDTH]
        dpw, g['cv_b_in'] = _cv_glu_bwd("cv_glu_bwd", pw, dygl, vec('cv_b_in', j))
        g['cv_w_in'] = _mm("cv_dwin", two(h1), two(dpw), ta=True)
        return _mm("cv_dh", two(dpw), W['cv_w_in'][j], tb=True), g

    mixers = [(gm_fwd, gm_bwd), (fox_fwd, fox_bwd), (sb_fwd, sb_bwd), (cv_fwd, cv_bwd)]
    n_mix = len(mixers)

    saved = []
    for l in range(depth):
        m, j = l % n_mix, l // n_mix
        sh1, sc1, g1_, sh2, sc2, g2_ = mods[l]
        ybias = vec('cv_b_out', j) if m == 3 else None
        h1 = _modulate("mod1", x, sc1, sh1)
        y1, cache = mixers[m][0](j, h1)
        xm = _resid_ln("resid_ln1", alpha, x, y1, g1_, small['ln1_g'][l][None], small['ln1_b'][l][None], ybias)
        h2 = _modulate("mod2", xm, sc2, sh2)
        z = three(_mm("ffn_in", two(h2), W['ffn_w_in'][l]))
        a = _swiglu_act("ffn_act", z)
        y2 = three(_mm("ffn_out", two(a), W['ffn_w_out'][l]))
        xo = _resid_ln("resid_ln2", alpha, xm, y2, g2_, small['ln2_g'][l][None], small['ln2_b'][l][None])
        saved.append((x, h1, y1, cache, xm, h2, z, a, y2, ybias))
        x = xo

    dx, sq = _loss_head("loss_head", x, target)
    loss = lax.psum(jnp.sum(sq) * np.float32(0.5 / D), axes)

    grads = {n: [None] * p[n].shape[0] for n in WEIGHTS}
    dmod = [None] * depth
    for l in reversed(range(depth)):
        m, j = l % n_mix, l // n_mix
        sh1, sc1, g1_, sh2, sc2, g2_ = mods[l]
        x_in, h1, y1, cache, xm, h2, z, a, y2, ybias = saved[l]
        dr2, dy2, dg2, grads['ln2_g'][l], grads['ln2_b'][l], _ = _resid_ln_bwd("resid_ln2_bwd", alpha, dx, xm, y2, g2_, small['ln2_g'][l][None])
        grads['ffn_w_out'][l] = _mm("ffn_dwout", two(a), two(dy2), ta=True)
        da = three(_mm("ffn_da", two(dy2), W['ffn_w_out'][l], tb=True))
        dz = _swiglu_act_bwd("ffn_act_bwd", da, z)
        grads['ffn_w_in'][l] = _mm("ffn_dwin", two(h2), two(dz), ta=True)
        dh2 = three(_mm("ffn_dh", two(dz), W['ffn_w_in'][l], tb=True))
        dxm, dsc2, dsh2 = _modulate_bwd("mod2_bwd", alpha, dh2, dr2, xm, sc2)
        dr1, dy1, dg1, grads['ln1_g'][l], grads['ln1_b'][l], dyb = _resid_ln_bwd("resid_ln1_bwd", alpha, dxm, x_in, y1, g1_, small['ln1_g'][l][None], ybias)
        dh1, mg = mixers[m][1](j, h1, dy1, cache)
        if m == 3:
            mg['cv_b_out'] = dyb
        for n, gval in mg.items():
            grads[n][j] = gval
        dx, dsc1, dsh1 = _modulate_bwd("mod1_bwd", alpha, three(dh1), dr1, x_in, sc1)
        dmod[l] = jnp.concatenate([dsh1, dsc1, dg1, dsh2, dsc2, dg2], axis=-1)[:, 0, :]
    grad_x = dx
    dmod = jnp.stack(dmod)
    grads['mod_b'] = [jnp.sum(dmod[l], axis=0) for l in range(depth)]
    full_shape = {n: tuple(t.shape) for n, t in small.items()}

    small_names = SMALL_REPL + SMALL_SPLIT
    small_parts = [jnp.stack([gv.reshape(full_shape[n][1:]) for gv in grads[n]]) for n in small_names]
    pack_a = _pack([dmod], LANES, F32, SUBLANES)
    pack_b = _pack(small_parts, LANES, F32, SUBLANES)
    g2 = _all_gather8("ag_grads_small", [jnp.concatenate([pack_a, pack_b], axis=0)])[0]
    rows_a = pack_a.shape[0]
    dmod_all = g2[:, :rows_a].reshape(N_DEV, -1)[:, :dmod.size].reshape(N_DEV, depth, B, 6 * D)
    dmod_all = dmod_all.transpose(1, 0, 2, 3).reshape(depth, n_seq, 6 * D)
    small_sum = _sum8("sum_grads_small", g2[:, rows_a:]).reshape(-1)
    g_small = dict(zip(small_names, _unpack(small_sum, [full_shape[n] for n in small_names])))
    for n in SMALL_SPLIT:
        w = p[n].shape[-1]
        g_small[n] = lax.dynamic_slice_in_dim(g_small[n], my_q * w, w, axis=g_small[n].ndim - 1)

    dm_cols = lax.dynamic_slice_in_dim(dmod_all, my_q * mod_cols, mod_cols, axis=2)
    dm_cols = jnp.pad(dm_cols, ((0, 0), (0, seq_pad - n_seq), (0, 0)))
    g_mod_w = jnp.stack([_mm(f"mod_dw{l}", c_act, dm_cols[l], ta=True) for l in range(depth)])

    keep, give = [], []
    for n, s, hr in zip(big_names, shard_shapes, half_rows):
        gfull = jnp.stack(grads[n])
        g4 = jnp.stack(jnp.split(gfull, N_CHIPS, axis=BIG[n])).reshape(N_CHIPS, 2 * hr, s[2])
        keep.append(lax.dynamic_slice_in_dim(g4, cc * hr, hr, axis=1))
        give.append(lax.dynamic_slice_in_dim(g4, (1 - cc) * hr, hr, axis=1))
    got = _sibling_exchange("rs_sibling", give)
    chip_sum = [_rowwise("rs_add_sibling", lambda r, bv, cv: ([r[0] + r[1]], [], []),
                         [a.reshape(1, -1, a.shape[2]), b.reshape(1, -1, a.shape[2])],
                         out_rows=[(a.shape[2], BF16)], tm=512)[0].reshape(a.shape) for a, b in zip(keep, got)]
    from_chips = _chip_all_to_all("rs_chips", chip_sum)
    half_sum = [_rowwise("rs_add_chips", lambda r, bv, cv: ([((r[0] + r[1]) + r[2]) + r[3]], [], []),
                         [t[q][None] for q in range(N_CHIPS)], out_rows=[(t.shape[2], F32)], tm=512)[0][0]
                for t in from_chips]
    other = _sibling_exchange("rs_share", half_sum)
    g_big = {n: jnp.concatenate([jnp.where(cc == 0, a, b), jnp.where(cc == 0, b, a)], axis=0).reshape(s)
             for n, s, a, b in zip(big_names, shard_shapes, half_sum, other)}

    g_out = {**g_small, **g_big, 'mod_w': g_mod_w}
    upd = {n: _adamw("adamw_" + n, p[n], g_out[n], p['m_' + n], p['v_' + n]) for n in WEIGHTS}
    return (loss, grad_x, *[g_out[n] for n in WEIGHTS], *[upd[n][0] for n in WEIGHTS],
            *[upd[n][1] for n in WEIGHTS], *[upd[n][2] for n in WEIGHTS])
```

```python
import math

import jax
import jax.numpy as jnp
import numpy as np
from jax import lax
from jax.experimental import pallas as pl
from jax.experimental.pallas import tpu as pltpu

F32, BF16 = jnp.float32, jnp.bfloat16

HEAD_DIM = 64
CONV_WIDTH = 31
LN_EPS = 1e-5
NEG_INF = -1e30
ADAM_LR, ADAM_B1, ADAM_B2, ADAM_EPS, ADAM_WD, ADAM_STEP = 0.001, 0.9, 0.999, 1e-08, 0.01, 10

LANES = 128
SUBLANES = 8
VMEM_LIMIT_BYTES = 56 * 1024 * 1024
N_CHIPS = 4
N_DEV = 8

WEIGHTS = ['mod_w', 'mod_b', 'ln1_g', 'ln1_b', 'ln2_g', 'ln2_b', 'ffn_w_in', 'ffn_w_out', 'gm_w_in', 'gm_b_in',
           'gm_ln_g', 'gm_ln_b', 'gm_w_s', 'gm_b_s', 'gm_w_out', 'fox_w_in', 'fox_b_f', 'fox_w_out', 'sb_w_in',
           'sb_w_out', 'cv_w_in', 'cv_b_in', 'cv_dw', 'cv_dw_b', 'cv_ln_g', 'cv_ln_b', 'cv_w_out', 'cv_b_out']
ARGS = ['x', 'c'] + WEIGHTS + ['loss_target'] + ['m_' + n for n in WEIGHTS] + ['v_' + n for n in WEIGHTS]
BIG = {'ffn_w_in': 2, 'ffn_w_out': 1, 'gm_w_in': 2, 'gm_w_out': 1, 'fox_w_in': 2, 'fox_w_out': 1,
       'sb_w_in': 2, 'sb_w_out': 1, 'cv_w_in': 2, 'cv_w_out': 1}
SMALL_SPLIT = ['cv_b_in', 'cv_dw', 'cv_dw_b', 'cv_ln_g', 'cv_ln_b', 'cv_b_out']
SMALL_REPL = ['mod_b', 'ln1_g', 'ln1_b', 'ln2_g', 'ln2_b', 'gm_b_in', 'gm_ln_g', 'gm_ln_b', 'gm_w_s', 'gm_b_s', 'fox_b_f']
PACK_COLS = 1024


_names_used = {}


def _unique(name):
    k = _names_used.get(name, 0)
    _names_used[name] = k + 1
    return name if k == 0 else f"{name}_{k}"


def _params(sem):
    return pltpu.CompilerParams(dimension_semantics=sem, vmem_limit_bytes=VMEM_LIMIT_BYTES)


def _pick(dim, pref, mult=LANES):
    if dim <= pref:
        return dim
    best = 0
    for t in range(mult, pref + 1, mult):
        if dim % t == 0:
            best = t
    assert best, (dim, pref)
    return best


def _mesh_pos():
    return lax.axis_index("x"), lax.axis_index("y"), lax.axis_index("c")


AG_COPIES = 7
A2A_COPIES = 3


def _comm_call(name, body, blks, out_shapes, n_sems):
    n = len(blks)
    hbm = pl.BlockSpec(memory_space=pl.ANY)
    return pl.pallas_call(
        body, name=_unique(name), out_shape=out_shapes, in_specs=[hbm] * n, out_specs=[hbm] * n,
        scratch_shapes=[pltpu.SemaphoreType.DMA((n_sems * n,)), pltpu.SemaphoreType.DMA((n_sems * n,)),
                        pltpu.SemaphoreType.DMA((n,))],
    )(*blks)


def _all_gather8(name, blks):
    n = len(blks)

    def body(*refs):
        x_refs, out_refs, (send_sems, recv_sems, local_sems) = refs[:n], refs[n:2 * n], refs[2 * n:]
        x, y, c = _mesh_pos()
        me, sibling = (x, y, c), (x, y, 1 - c)
        chips = [(1 - x, y), (x, 1 - y), (1 - x, 1 - y)]

        def copy(a, k, block, to, from_input=False):
            px, py, pc = block
            slot = out_refs[a].at[4 * px + 2 * py + pc]
            return pltpu.make_async_remote_copy(
                src_ref=x_refs[a] if from_input else slot, dst_ref=slot,
                send_sem=send_sems.at[AG_COPIES * a + k], recv_sem=recv_sems.at[AG_COPIES * a + k],
                device_id=to, device_id_type=pl.DeviceIdType.MESH)

        local, sent = [], []
        for a in range(n):
            local.append(pltpu.make_async_copy(x_refs[a], out_refs[a].at[4 * x + 2 * y + c], local_sems.at[a]))
            local[-1].start()
            first = [copy(a, 0, me, sibling, True)] + [copy(a, 1 + j, me, (*chip, c), True) for j, chip in enumerate(chips)]
            for cp in first:
                cp.start()
            sent += first
        for a in range(n):
            for j, chip in enumerate(chips):
                copy(a, 1 + j, (*chip, c), me).wait_recv()
                sent.append(copy(a, 4 + j, (*chip, c), sibling))
                sent[-1].start()
        for a in range(n):
            copy(a, 0, sibling, me).wait_recv()
            for j, chip in enumerate(chips):
                copy(a, 4 + j, (*chip, 1 - c), me).wait_recv()
        for cp in sent:
            cp.wait_send()
        for cp in local:
            cp.wait()

    return _comm_call(name, body, blks, [jax.ShapeDtypeStruct((N_DEV,) + b.shape, b.dtype) for b in blks], AG_COPIES)


def _sibling_exchange(name, blks):
    n = len(blks)

    def body(*refs):
        x_refs, out_refs, (send_sems, recv_sems, _) = refs[:n], refs[n:2 * n], refs[2 * n:]
        x, y, c = _mesh_pos()
        cps = [pltpu.make_async_remote_copy(src_ref=x_refs[a], dst_ref=out_refs[a], send_sem=send_sems.at[a],
                                            recv_sem=recv_sems.at[a], device_id=(x, y, 1 - c),
                                            device_id_type=pl.DeviceIdType.MESH) for a in range(n)]
        for cp in cps:
            cp.start()
        for cp in cps:
            cp.wait()

    return _comm_call(name, body, blks, [jax.ShapeDtypeStruct(b.shape, b.dtype) for b in blks], 1)


def _chip_all_to_all(name, blks):
    n = len(blks)

    def body(*refs):
        x_refs, out_refs, (send_sems, recv_sems, local_sems) = refs[:n], refs[n:2 * n], refs[2 * n:]
        x, y, c = _mesh_pos()
        chips = [(1 - x, y), (x, 1 - y), (1 - x, 1 - y)]
        my_q = 2 * x + y

        def copy(a, j, src_q, dst_q):
            px, py = chips[j]
            return pltpu.make_async_remote_copy(
                src_ref=x_refs[a].at[src_q], dst_ref=out_refs[a].at[dst_q],
                send_sem=send_sems.at[A2A_COPIES * a + j], recv_sem=recv_sems.at[A2A_COPIES * a + j],
                device_id=(px, py, c), device_id_type=pl.DeviceIdType.MESH)

        local, sent = [], []
        for a in range(n):
            local.append(pltpu.make_async_copy(x_refs[a].at[my_q], out_refs[a].at[my_q], local_sems.at[a]))
            local[-1].start()
            sent += [copy(a, j, 2 * px + py, my_q) for j, (px, py) in enumerate(chips)]
            for cp in sent[-A2A_COPIES:]:
                cp.start()
        for a in range(n):
            for j, (px, py) in enumerate(chips):
                copy(a, j, my_q, 2 * px + py).wait_recv()
        for cp in sent:
            cp.wait_send()
        for cp in local:
            cp.wait()

    return _comm_call(name, body, blks, [jax.ShapeDtypeStruct(b.shape, b.dtype) for b in blks], A2A_COPIES)


def _rowwise(name, fn, rows, bvecs=(), cvecs=(), out_rows=(), out_bsums=(), out_tsums=(), tm=256):
    B, S = rows[0].shape[:2]
    tm = _pick(S, tm, SUBLANES)
    n_r, n_b, n_c = len(rows), len(bvecs), len(cvecs)
    n_or, n_ob = len(out_rows), len(out_bsums)

    def body(*refs):
        ins, outs = refs[:n_r + n_b + n_c], refs[n_r + n_b + n_c:]
        r = [ref[0] for ref in ins[:n_r]]
        bv = [ref[0] for ref in ins[n_r:n_r + n_b]]
        cv = [ref[...] for ref in ins[n_r + n_b:]]
        o_rows, o_bsums, o_tsums = fn(r, bv, cv)
        b, i = pl.program_id(0), pl.program_id(1)
        for ref, val in zip(outs[:n_or], o_rows):
            ref[0] = val.astype(ref.dtype)
        for ref, val in zip(outs[n_or:n_or + n_ob], o_bsums):
            @pl.when(i == 0)
            def _(ref=ref, val=val):
                ref[0] = val

            @pl.when(i > 0)
            def _(ref=ref, val=val):
                ref[0] += val
        for ref, val in zip(outs[n_or + n_ob:], o_tsums):
            first = jnp.logical_and(b == 0, i == 0)

            @pl.when(first)
            def _(ref=ref, val=val):
                ref[...] = val

            @pl.when(jnp.logical_not(first))
            def _(ref=ref, val=val):
                ref[...] += val

    in_specs = [pl.BlockSpec((1, tm, a.shape[2]), lambda b, i: (b, i, 0)) for a in rows]
    in_specs += [pl.BlockSpec((1, 1, a.shape[2]), lambda b, i: (b, 0, 0)) for a in bvecs]
    in_specs += [pl.BlockSpec((1, a.shape[1]), lambda b, i: (0, 0)) for a in cvecs]
    out_shape = [jax.ShapeDtypeStruct((B, S, cdim), dt) for cdim, dt in out_rows]
    out_specs = [pl.BlockSpec((1, tm, cdim), lambda b, i: (b, i, 0)) for cdim, _ in out_rows]
    out_shape += [jax.ShapeDtypeStruct((B, 1, cdim), F32) for cdim in out_bsums]
    out_specs += [pl.BlockSpec((1, 1, cdim), lambda b, i: (b, 0, 0)) for cdim in out_bsums]
    out_shape += [jax.ShapeDtypeStruct((1, cdim), F32) for cdim in out_tsums]
    out_specs += [pl.BlockSpec((1, cdim), lambda b, i: (0, 0)) for cdim in out_tsums]
    sem = ("arbitrary", "arbitrary") if out_tsums else ("parallel", "arbitrary")
    res = pl.pallas_call(body, name=_unique(name), grid=(B, S // tm), in_specs=in_specs, out_specs=out_specs,
                         out_shape=out_shape, compiler_params=_params(sem))(*rows, *bvecs, *cvecs)
    return list(res)


MM_TILE = 1536
MM_ROWS = 512
MM_WEIGHT_TILE_BYTES = 12 * 1024 * 1024


def _mm(name, a, b, ta=False, tb=False, out_dtype=F32):
    M, K = (a.shape[1], a.shape[0]) if ta else a.shape
    N = b.shape[0] if tb else b.shape[1]
    assert (b.shape[1] if tb else b.shape[0]) == K, (a.shape, b.shape, ta, tb)
    tn = _pick(N, MM_TILE)
    if ta:
        tm, tk = _pick(M, MM_TILE), _pick(K, 2 * MM_ROWS, LANES if tb else SUBLANES)
    else:
        tm = _pick(M, MM_ROWS, SUBLANES)
        tk = K if K * tn * 2 <= MM_WEIGHT_TILE_BYTES else _pick(K, MM_TILE)
    nk = K // tk
    dims = (((0 if ta else 1,), (1 if tb else 0,)), ((), ()))

    def body(a_ref, b_ref, o_ref, acc_ref):
        k = pl.program_id(2)
        p = lax.dot_general(a_ref[...].astype(BF16), b_ref[...].astype(BF16), dims, preferred_element_type=F32)
        if nk == 1:
            o_ref[...] = p.astype(o_ref.dtype)
        else:
            @pl.when(k == 0)
            def _():
                acc_ref[...] = p

            @pl.when(k > 0)
            def _():
                acc_ref[...] += p

            @pl.when(k == nk - 1)
            def _():
                o_ref[...] = acc_ref[...].astype(o_ref.dtype)

    a_spec = pl.BlockSpec((tk, tm), lambda j, i, k: (k, i)) if ta else pl.BlockSpec((tm, tk), lambda j, i, k: (i, k))
    b_spec = pl.BlockSpec((tn, tk), lambda j, i, k: (j, k)) if tb else pl.BlockSpec((tk, tn), lambda j, i, k: (k, j))
    return pl.pallas_call(
        body, name=_unique(name), grid=(N // tn, M // tm, nk), in_specs=[a_spec, b_spec],
        out_specs=pl.BlockSpec((tm, tn), lambda j, i, k: (i, j)),
        out_shape=jax.ShapeDtypeStruct((M, N), out_dtype),
        scratch_shapes=[pltpu.VMEM((tm, tn) if nk > 1 else (SUBLANES, LANES), F32)],
        compiler_params=_params(("parallel", "parallel", "arbitrary")))(a, b)


def _silu(x):
    return x * _sigmoid(x)


def _sigmoid(x):
    return 1.0 / (1.0 + jnp.exp(-x))


def _dsilu(x):
    s = _sigmoid(x)
    return s * (1.0 + x * (1.0 - s))


def _gelu(x):
    return 0.5 * x * (1.0 + lax.erf(x * np.float32(math.sqrt(0.5))))


def _dgelu(x):
    cdf = 0.5 * (1.0 + lax.erf(x * np.float32(math.sqrt(0.5))))
    pdf = jnp.exp(-0.5 * x * x) * np.float32(1.0 / math.sqrt(2.0 * math.pi))
    return cdf + x * pdf


def _ln_stats(r):
    mu = jnp.mean(r, axis=-1, keepdims=True)
    xc = r - mu
    var = jnp.mean(xc * xc, axis=-1, keepdims=True)
    rstd = lax.rsqrt(var + LN_EPS)
    return xc * rstd, rstd


def _ln_bwd(dxhat, xhat, rstd):
    m1 = jnp.mean(dxhat, axis=-1, keepdims=True)
    m2 = jnp.mean(dxhat * xhat, axis=-1, keepdims=True)
    return rstd * (dxhat - m1 - xhat * m2)


def _csum(v):
    return jnp.sum(v, axis=0, keepdims=True)


def _split_dot(x, m01, lhs01=False, terms=2):
    acc, rem = None, x
    for _ in range(terms):
        part = rem.astype(BF16)
        rem = rem - part.astype(F32)
        d = jnp.dot(m01, part, preferred_element_type=F32) if lhs01 else jnp.dot(part, m01, preferred_element_type=F32)
        acc = d if acc is None else acc + d
    return acc


def _iota2(shape, dim):
    return lax.broadcasted_iota(jnp.int32, shape, dim)


def _modulate(name, x, sc, sh):
    D = x.shape[2]
    return _rowwise(name, lambda r, bv, cv: ([r[0] * (1.0 + bv[0]) + bv[1]], [], []),
                    [x], [sc, sh], [], out_rows=[(D, BF16)])[0]


def _resid_ln(name, alpha, x, y, g, ln_g, ln_b, ybias=None):
    D = x.shape[2]

    def fn(r, bv, cv):
        yy = r[1] if ybias is None else r[1] + cv[2]
        xhat, _ = _ln_stats(alpha * r[0] + (1.0 + bv[0]) * yy)
        return [xhat * cv[0] + cv[1]], [], []
    cvecs = [ln_g, ln_b] + ([] if ybias is None else [ybias])
    return _rowwise(name, fn, [x, y], [g], cvecs, out_rows=[(D, F32)])[0]


def _resid_ln_bwd(name, alpha, dxn, x, y, g, ln_g, ybias=None):
    D = x.shape[2]

    def fn(r, bv, cv):
        yy = r[2] if ybias is None else r[2] + cv[1]
        xhat, rstd = _ln_stats(alpha * r[1] + (1.0 + bv[0]) * yy)
        dr = _ln_bwd(r[0] * cv[0], xhat, rstd)
        dy = (1.0 + bv[0]) * dr
        return [dr, dy], [_csum(dr * yy)], [_csum(r[0] * xhat), _csum(r[0]), _csum(dy)]
    cvecs = [ln_g] + ([] if ybias is None else [ybias])
    return _rowwise(name, fn, [dxn, x, y], [g], cvecs, out_rows=[(D, F32), (D, BF16)], out_bsums=[D], out_tsums=[D, D, D])


def _modulate_bwd(name, alpha, dh, dr, x, sc):
    D = x.shape[2]

    def fn(r, bv, cv):
        return [alpha * r[1] + r[0] * (1.0 + bv[0])], [_csum(r[0] * r[2]), _csum(r[0])], []
    return _rowwise(name, fn, [dh, dr, x], [sc], [], out_rows=[(D, F32)], out_bsums=[D, D])


def _loss_head(name, y, target):
    D = y.shape[2]

    def fn(r, bv, cv):
        e = r[0] - r[1]
        return [e * np.float32(1.0 / D)], [_csum(e * e)], []
    return _rowwise(name, fn, [y, target], [], [], out_rows=[(D, F32)], out_bsums=[D])


def _swiglu_act(name, z):
    Hd = z.shape[2] // 2
    return _rowwise(name, lambda r, bv, cv: ([_silu(r[0][:, :Hd]) * r[0][:, Hd:]], [], []), [z], out_rows=[(Hd, BF16)])[0]


def _swiglu_act_bwd(name, da, z):
    Hd = z.shape[2] // 2

    def fn(r, bv, cv):
        gg, u = r[1][:, :Hd], r[1][:, Hd:]
        return [jnp.concatenate([r[0] * u * _dsilu(gg), r[0] * _silu(gg)], axis=1)], [], []
    return _rowwise(name, fn, [da, z], out_rows=[(2 * Hd, BF16)])[0]


def _gm_act(name, zin, b_in, ln_g, ln_b):
    W = zin.shape[2] // 2

    def fn(r, bv, cv):
        z = _gelu(r[0] + cv[0])
        vhat, _ = _ln_stats(z[:, W:])
        return [z[:, :W], vhat * cv[1] + cv[2]], [], []
    return _rowwise(name, fn, [zin], [], [b_in, ln_g, ln_b], out_rows=[(W, F32), (W, BF16)])


def _gm_act_bwd(name, zin, du, dvn, b_in, ln_g):
    W = zin.shape[2] // 2

    def fn(r, bv, cv):
        zz = r[0] + cv[0]
        z = _gelu(zz)
        vhat, rstd = _ln_stats(z[:, W:])
        dv = _ln_bwd(r[2] * cv[1], vhat, rstd)
        dzin = jnp.concatenate([r[1], dv], axis=1) * _dgelu(zz)
        return [dzin], [], [_csum(dzin), _csum(r[2] * vhat), _csum(r[2])]
    return _rowwise(name, fn, [zin, du, dvn], [], [b_in, ln_g], out_rows=[(2 * W, BF16)], out_tsums=[2 * W, W, W])


def _gm_causal_w(ws_ref, g):
    T = ws_ref.shape[1]
    return jnp.where(_iota2((T, T), 1) <= _iota2((T, T), 0), ws_ref[g], 0.0).astype(BF16)


def _gm_spatial(name, u, vn, w_s, b_sT):
    B, S, W = u.shape
    G, T = w_s.shape[0], w_s.shape[1]
    assert W == G * T, "a head group is as wide as a chunk is long"

    def body(u_ref, vn_ref, ws_ref, bs_ref, y_ref):
        for g in range(G):
            cs = slice(g * T, (g + 1) * T)
            sv = jnp.dot(_gm_causal_w(ws_ref, g), vn_ref[0, :, cs], preferred_element_type=F32) + bs_ref[:, g:g + 1]
            y_ref[0, :, cs] = (u_ref[0, :, cs] * sv).astype(y_ref.dtype)

    row = pl.BlockSpec((1, T, W), lambda b, i: (b, i, 0))
    return pl.pallas_call(
        body, name=_unique(name), grid=(B, S // T),
        in_specs=[row, row, pl.BlockSpec((G, T, T), lambda b, i: (0, 0, 0)), pl.BlockSpec((T, G), lambda b, i: (0, 0))],
        out_specs=row, out_shape=jax.ShapeDtypeStruct((B, S, W), BF16),
        compiler_params=_params(("parallel", "parallel")))(u, vn, w_s, b_sT)


def _gm_spatial_bwd(name, dyg, u, vn, w_s, b_sT):
    B, S, W = u.shape
    G, T = w_s.shape[0], w_s.shape[1]
    assert W == G * T, "a head group is as wide as a chunk is long"

    def body(dy_ref, u_ref, vn_ref, ws_ref, bs_ref, du_ref, dvn_ref, dws_ref, dbs_ref):
        first = jnp.logical_and(pl.program_id(0) == 0, pl.program_id(1) == 0)

        @pl.when(first)
        def _():
            dws_ref[...] = jnp.zeros_like(dws_ref)
            dbs_ref[...] = jnp.zeros_like(dbs_ref)

        tril = _iota2((T, T), 1) <= _iota2((T, T), 0)
        for g in range(G):
            cs = slice(g * T, (g + 1) * T)
            wm = _gm_causal_w(ws_ref, g)
            vng = vn_ref[0, :, cs]
            sv = jnp.dot(wm, vng, preferred_element_type=F32) + bs_ref[:, g:g + 1]
            dy = dy_ref[0, :, cs]
            du_ref[0, :, cs] = dy * sv
            dsv = dy * u_ref[0, :, cs]
            dsv16 = dsv.astype(BF16)
            dvn_ref[0, :, cs] = lax.dot_general(wm, dsv16, (((0,), (0,)), ((), ())), preferred_element_type=F32)
            dw = lax.dot_general(dsv16, vng, (((1,), (1,)), ((), ())), preferred_element_type=F32)
            dws_ref[g] += jnp.where(tril, dw, 0.0)
            dbs_ref[:, g:g + 1] += jnp.sum(dsv, axis=1, keepdims=True)

    row = pl.BlockSpec((1, T, W), lambda b, i: (b, i, 0))
    return pl.pallas_call(
        body, name=_unique(name), grid=(B, S // T),
        in_specs=[row, row, row, pl.BlockSpec((G, T, T), lambda b, i: (0, 0, 0)), pl.BlockSpec((T, G), lambda b, i: (0, 0))],
        out_specs=[row, row, pl.BlockSpec((G, T, T), lambda b, i: (0, 0, 0)), pl.BlockSpec((T, G), lambda b, i: (0, 0))],
        out_shape=[jax.ShapeDtypeStruct((B, S, W), F32), jax.ShapeDtypeStruct((B, S, W), F32),
                   jax.ShapeDtypeStruct((G, T, T), F32), jax.ShapeDtypeStruct((T, G), F32)],
        compiler_params=_params(("arbitrary", "arbitrary")))(dyg, u, vn, w_s, b_sT)


def _cv_glu(name, pw, b_in):
    W = pw.shape[2] // 2

    def fn(r, bv, cv):
        z = r[0] + cv[0]
        return [z[:, :W] * _sigmoid(z[:, W:])], [], []
    return _rowwise(name, fn, [pw], [], [b_in], out_rows=[(W, F32)])[0]


def _cv_glu_bwd(name, pw, dyg, b_in):
    W = pw.shape[2] // 2

    def fn(r, bv, cv):
        z = r[0] + cv[0]
        a, s = z[:, :W], _sigmoid(z[:, W:])
        dpw = jnp.concatenate([r[1] * s, r[1] * a * s * (1.0 - s)], axis=1)
        return [dpw], [], [_csum(dpw)]
    return _rowwise(name, fn, [pw, dyg], [], [b_in], out_rows=[(2 * W, BF16)], out_tsums=[2 * W])


def _cv_ln_act(name, yc, ln_g, ln_b):
    D = yc.shape[2]

    def fn(r, bv, cv):
        xhat, _ = _ln_stats(r[0])
        return [_silu(xhat * cv[0] + cv[1])], [], []
    return _rowwise(name, fn, [yc], [], [ln_g, ln_b], out_rows=[(D, BF16)])[0]


def _cv_ln_act_bwd(name, yc, dys, ln_g, ln_b):
    D = yc.shape[2]

    def fn(r, bv, cv):
        xhat, rstd = _ln_stats(r[0])
        dyn = r[1] * _dsilu(xhat * cv[0] + cv[1])
        dyc = _ln_bwd(dyn * cv[0], xhat, rstd)
        return [dyc], [], [_csum(dyn * xhat), _csum(dyn), _csum(dyc)]
    return _rowwise(name, fn, [yc, dys], [], [ln_g, ln_b], out_rows=[(D, F32)], out_tsums=[D, D, D])


CONV_HALO = 32
CONV_TS, CONV_TC = 256, 128


def _dwconv(name, y, dw, dw_b):
    B, S, D = y.shape
    ts, tc, halo, K = _pick(S, CONV_TS, SUBLANES), _pick(D, CONV_TC), CONV_HALO, CONV_WIDTH

    def body(cur_ref, prev_ref, dw_ref, b_ref, o_ref, buf):
        i = pl.program_id(1)
        buf[pl.ds(0, halo), :] = jnp.where(i > 0, prev_ref[0, pl.ds(ts - halo, halo), :], 0.0)
        buf[pl.ds(halo, ts), :] = cur_ref[0]
        acc = jnp.zeros((ts, tc), F32) + b_ref[...]
        for k in range(K):
            acc = acc + dw_ref[k:k + 1, :] * buf[pl.ds(halo - (K - 1) + k, ts), :]
        o_ref[0] = acc

    return pl.pallas_call(
        body, name=_unique(name), grid=(B, S // ts, D // tc),
        in_specs=[pl.BlockSpec((1, ts, tc), lambda b, i, j: (b, i, j)),
                  pl.BlockSpec((1, ts, tc), lambda b, i, j: (b, jnp.maximum(i - 1, 0), j)),
                  pl.BlockSpec((halo, tc), lambda b, i, j: (0, j)), pl.BlockSpec((1, tc), lambda b, i, j: (0, j))],
        out_specs=pl.BlockSpec((1, ts, tc), lambda b, i, j: (b, i, j)),
        out_shape=jax.ShapeDtypeStruct((B, S, D), F32),
        scratch_shapes=[pltpu.VMEM((halo + ts, tc), F32)],
        compiler_params=_params(("parallel", "parallel", "parallel")))(y, y, dw, dw_b)


def _dwconv_bwd(name, dyc, y, dw):
    B, S, D = y.shape
    ts, tc, halo, K = _pick(S, CONV_TS, SUBLANES), _pick(D, CONV_TC), CONV_HALO, CONV_WIDTH
    nt = S // ts

    def body(g_ref, gnext_ref, y_ref, yprev_ref, dw_ref, dy_ref, ddw_ref, gbuf, ybuf):
        b, i = pl.program_id(1), pl.program_id(2)
        first = jnp.logical_and(b == 0, i == 0)

        @pl.when(first)
        def _():
            ddw_ref[...] = jnp.zeros_like(ddw_ref)

        g = g_ref[0]
        gbuf[pl.ds(0, ts), :] = g
        gbuf[pl.ds(ts, halo), :] = jnp.where(i < nt - 1, gnext_ref[0, pl.ds(0, halo), :], 0.0)
        ybuf[pl.ds(0, halo), :] = jnp.where(i > 0, yprev_ref[0, pl.ds(ts - halo, halo), :], 0.0)
        ybuf[pl.ds(halo, ts), :] = y_ref[0]
        acc = jnp.zeros((ts, tc), F32)
        for k in range(K):
            acc = acc + dw_ref[k:k + 1, :] * gbuf[pl.ds(K - 1 - k, ts), :]
            ddw_ref[k:k + 1, :] += _csum(g * ybuf[pl.ds(halo - (K - 1) + k, ts), :])
        dy_ref[0] = acc

    tile = lambda f: pl.BlockSpec((1, ts, tc), f)
    return pl.pallas_call(
        body, name=_unique(name), grid=(D // tc, B, nt),
        in_specs=[tile(lambda j, b, i: (b, i, j)), tile(lambda j, b, i: (b, jnp.minimum(i + 1, nt - 1), j)),
                  tile(lambda j, b, i: (b, i, j)), tile(lambda j, b, i: (b, jnp.maximum(i - 1, 0), j)),
                  pl.BlockSpec((halo, tc), lambda j, b, i: (0, j))],
        out_specs=[tile(lambda j, b, i: (b, i, j)), pl.BlockSpec((halo, tc), lambda j, b, i: (0, j))],
        out_shape=[jax.ShapeDtypeStruct((B, S, D), F32), jax.ShapeDtypeStruct((halo, D), F32)],
        scratch_shapes=[pltpu.VMEM((ts + halo, tc), F32), pltpu.VMEM((halo + ts, tc), F32)],
        compiler_params=_params(("parallel", "arbitrary", "arbitrary")))(dyc, dyc, y, y, dw)


ATT_BLOCK = 128
ATT_QUERY_BLOCK = 256
FOX_KEY_BLOCK = 1024
SB_KEY_BLOCK = 512
ATT_PIECE_ROWS = 32
FOX_GATE_COLS = 128


def _att_tiles(S, key_block):
    return _pick(S, ATT_QUERY_BLOCK, SUBLANES), _pick(S, key_block, LANES)


def _pieces(T, TK):
    R = min(T, ATT_PIECE_ROWS)
    segs = [slice(c, c + LANES) for c in range(0, TK, LANES)]
    return [(slice(r, r + R), segs) for r in range(0, T, R)]


def _piece_keep(row0, col0, rs, cs, strict, lane_major_of=0):
    shape = (rs.stop - rs.start, cs.stop - cs.start)
    lane = _iota2(shape, 1)
    key = col0 + (lane * lane_major_of + cs.start // LANES if lane_major_of else cs.start + lane)
    qry = row0 + rs.start + _iota2(shape, 0)
    return key < qry if strict else key <= qry


def _causal_tiles(i, tq, tk):
    return (i * tq + tq + tk - 1) // tk


def _lane_major(t, tk):
    B, H, S, dh = t.shape
    return t.reshape(B, H, S // tk, LANES, tk // LANES, dh).swapaxes(3, 4).reshape(B, H, S, dh)


def _lane_major_inverse(t, tk):
    B, H, S, dh = t.shape
    return t.reshape(B, H, S // tk, tk // LANES, LANES, dh).swapaxes(3, 4).reshape(B, H, S, dh)


def _log_sigmoid(x):
    return jnp.minimum(x, 0.0) - jnp.log(1.0 + jnp.exp(-jnp.abs(x)))


def _fox_gate_cumsum(name, fl, b_f):
    B, S, C = fl.shape
    T = _pick(S, ATT_BLOCK, SUBLANES)

    def body(fl_ref, bf_ref, f_ref, carry):
        @pl.when(pl.program_id(1) == 0)
        def _():
            carry[...] = jnp.zeros_like(carry)
        lf = _log_sigmoid(fl_ref[0] + bf_ref[...])
        lower = (_iota2((T, T), 1) <= _iota2((T, T), 0)).astype(BF16)
        f = _split_dot(lf, lower, lhs01=True, terms=3) + carry[...]
        f_ref[0] = f
        carry[...] = f[T - 1:T, :]

    return pl.pallas_call(
        body, name=_unique(name), grid=(B, S // T),
        in_specs=[pl.BlockSpec((1, T, C), lambda b, i: (b, i, 0)), pl.BlockSpec((1, C), lambda b, i: (0, 0))],
        out_specs=pl.BlockSpec((1, T, C), lambda b, i: (b, i, 0)),
        out_shape=jax.ShapeDtypeStruct((B, S, C), F32),
        scratch_shapes=[pltpu.VMEM((1, C), F32)],
        compiler_params=_params(("arbitrary", "arbitrary")))(fl, b_f)


def _fox_gate_bwd(name, dF, fl, b_f, n_heads):
    B, S, C = fl.shape
    T = _pick(S, ATT_BLOCK, SUBLANES)
    nt = S // T

    def body(df_ref, fl_ref, bf_ref, dfl_ref, dbf_ref, carry):
        first = jnp.logical_and(pl.program_id(0) == 0, pl.program_id(1) == 0)

        @pl.when(pl.program_id(1) == 0)
        def _():
            carry[...] = jnp.zeros_like(carry)

        @pl.when(first)
        def _():
            dbf_ref[...] = jnp.zeros_like(dbf_ref)

        upper = (_iota2((T, T), 1) >= _iota2((T, T), 0)).astype(BF16)
        dlf = _split_dot(df_ref[0], upper, lhs01=True, terms=3) + carry[...]
        carry[...] = dlf[0:1, :]
        x = fl_ref[0] + bf_ref[...]
        dfl = jnp.where(_iota2((T, C), 1) < n_heads, dlf * _sigmoid(-x), 0.0)
        dfl_ref[0] = dfl
        dbf_ref[...] += _csum(dfl)

    rev = lambda b, i: (b, nt - 1 - i, 0)
    return pl.pallas_call(
        body, name=_unique(name), grid=(B, nt),
        in_specs=[pl.BlockSpec((1, T, C), rev), pl.BlockSpec((1, T, C), rev), pl.BlockSpec((1, C), lambda b, i: (0, 0))],
        out_specs=[pl.BlockSpec((1, T, C), rev), pl.BlockSpec((1, C), lambda b, i: (0, 0))],
        out_shape=[jax.ShapeDtypeStruct((B, S, C), F32), jax.ShapeDtypeStruct((1, C), F32)],
        scratch_shapes=[pltpu.VMEM((1, C), F32)],
        compiler_params=_params(("arbitrary", "arbitrary")))(dF, fl, b_f)


_NT = (((1,), (1,)), ((), ()))
_TN = (((0,), (0,)), ((), ()))


def _fox_fwd(name, q, k, v, fq, fk):
    B, H, S, dh = q.shape
    T, TK = _att_tiles(S, FOX_KEY_BLOCK)
    scale = np.float32(dh ** -0.5)

    pieces = _pieces(T, TK)

    def body(q_ref, k_ref, v_ref, fq_ref, fk_ref, o_ref, lse_ref, s_scr, p_scr):
        i = pl.program_id(2)
        qb = q_ref[0, 0]
        n_tiles = _causal_tiles(i, T, TK)

        def tile(j, carry, diag):
            m, l, acc = carry
            ks = pl.ds(pl.multiple_of(j * TK, TK), TK)
            s_scr[...] = lax.dot_general(qb, k_ref[0, 0, ks, :], _NT, preferred_element_type=F32) * scale
            fkj = fk_ref[0, 0, pl.ds(j, 1), :]
            m_new = []
            for rc, (rs, segs) in enumerate(pieces):
                fq_c, mx = fq_ref[0, 0, rs, :], None
                for cs in segs:
                    s = s_scr[rs, cs] + fq_c - fkj[:, cs]
                    if diag:
                        s = jnp.where(_piece_keep(i * T, j * TK, rs, cs, False), s, NEG_INF)
                    s_scr[rs, cs] = s
                    mx = s if mx is None else jnp.maximum(mx, s)
                m_new.append(jnp.maximum(m[rc], jnp.max(mx, axis=1, keepdims=True)))
            alpha, l_new = [], []
            for rc, (rs, segs) in enumerate(pieces):
                alpha.append(jnp.exp(m[rc] - m_new[rc]))
                psum = None
                for cs in segs:
                    p = jnp.exp(s_scr[rs, cs] - m_new[rc])
                    p_scr[rs, cs] = p.astype(BF16)
                    psum = p if psum is None else psum + p
                l_new.append(alpha[rc] * l[rc] + jnp.sum(psum, axis=1, keepdims=True))
            acc = jnp.concatenate(alpha, axis=0) * acc + jnp.dot(p_scr[...], v_ref[0, 0, ks, :], preferred_element_type=F32)
            return tuple(m_new), tuple(l_new), acc

        init = (tuple(jnp.full((rs.stop - rs.start, 1), NEG_INF, F32) for rs, _ in pieces),
                tuple(jnp.zeros((rs.stop - rs.start, 1), F32) for rs, _ in pieces), jnp.zeros((T, dh), F32))
        carry = lax.fori_loop(0, n_tiles - 1, lambda j, c: tile(j, c, False), init)
        m, l, acc = tile(n_tiles - 1, carry, True)
        m, l = jnp.concatenate(m, axis=0), jnp.concatenate(l, axis=0)
        o_ref[0, 0] = acc / l
        lse_ref[0, 0] = m + jnp.log(l)

    full = lambda w: pl.BlockSpec((1, 1, S, w), lambda b, h, i: (b, h, 0, 0))
    blk = lambda w: pl.BlockSpec((1, 1, T, w), lambda b, h, i: (b, h, i, 0))
    return pl.pallas_call(
        body, name=_unique(name), grid=(B, H, S // T),
        in_specs=[blk(dh), full(dh), full(dh), blk(1), pl.BlockSpec((1, 1, S // TK, TK), lambda b, h, i: (b, h, 0, 0))],
        out_specs=[blk(dh), blk(1)],
        out_shape=[jax.ShapeDtypeStruct((B, H, S, dh), F32), jax.ShapeDtypeStruct((B, H, S, 1), F32)],
        scratch_shapes=[pltpu.VMEM((T, TK), F32), pltpu.VMEM((T, TK), BF16)],
        compiler_params=_params(("parallel", "parallel", "parallel")))(q, k, v, fq, fk)


def _fox_bwd(name, q, k, v, fq, fk, do, lse):
    B, H, S, dh = q.shape
    T, TK = _att_tiles(S, FOX_KEY_BLOCK)
    nt, nkt = S // T, S // TK
    scale = np.float32(dh ** -0.5)

    pieces = _pieces(T, TK)

    def body(q_ref, k_ref, v_ref, fq_ref, fk_ref, do_ref, lse_ref, dq_ref, dk_ref, dv_ref, dfk_ref,
             p_buf, dp_buf, s_scr, ds_scr, p16_scr):
        dk_ref[...] = jnp.zeros_like(dk_ref)
        dv_ref[...] = jnp.zeros_like(dv_ref)
        dfk_ref[...] = jnp.zeros_like(dfk_ref)

        def qloop(i, _):
            qs = pl.ds(pl.multiple_of(i * T, T), T)
            qb, dob16 = q_ref[0, 0, qs, :], do_ref[0, 0, qs, :].astype(BF16)
            n_tiles = _causal_tiles(i, T, TK)
            row_at = lambda rs: pl.ds(pl.multiple_of(i * T + rs.start, SUBLANES), rs.stop - rs.start)
            fq_c = [fq_ref[0, 0, row_at(rs), :] for rs, _ in pieces]
            lse_c = [lse_ref[0, 0, row_at(rs), :] for rs, _ in pieces]

            def sweep1(j, delta, diag):
                ks = pl.ds(pl.multiple_of(j * TK, TK), TK)
                s_scr[...] = lax.dot_general(qb, k_ref[0, 0, ks, :], _NT, preferred_element_type=F32) * scale
                dp_buf[j] = lax.dot_general(dob16, v_ref[0, 0, ks, :], _NT, preferred_element_type=F32)
                fkj = fk_ref[0, 0, pl.ds(j, 1), :]
                out = []
                for rc, (rs, segs) in enumerate(pieces):
                    pdp = None
                    for cs in segs:
                        p = jnp.exp(((s_scr[rs, cs] + fq_c[rc]) - fkj[:, cs]) - lse_c[rc])
                        if diag:
                            p = jnp.where(_piece_keep(i * T, j * TK, rs, cs, False), p, 0.0)
                        p_buf[j, rs, cs] = p
                        pdp = p * dp_buf[j, rs, cs] if pdp is None else pdp + p * dp_buf[j, rs, cs]
                    out.append(delta[rc] + jnp.sum(pdp, axis=1, keepdims=True))
                return tuple(out)

            zeros = tuple(jnp.zeros((rs.stop - rs.start, 1), F32) for rs, _ in pieces)
            delta = lax.fori_loop(0, n_tiles - 1, lambda j, d: sweep1(j, d, False), zeros)
            delta = sweep1(n_tiles - 1, delta, True)

            def sweep2(j, dq):
                ks = pl.ds(pl.multiple_of(j * TK, TK), TK)
                col = [None] * len(pieces[0][1])
                for rc, (rs, segs) in enumerate(pieces):
                    for sg, cs in enumerate(segs):
                        p = p_buf[j, rs, cs]
                        ds = p * (dp_buf[j, rs, cs] - delta[rc])
                        ds_scr[rs, cs] = ds.astype(BF16)
                        p16_scr[rs, cs] = p.astype(BF16)
                        col[sg] = ds if col[sg] is None else col[sg] + ds
                dfk_ref[0, 0, pl.ds(j, 1), :] -= jnp.concatenate([_csum(c) for c in col], axis=1)
                ds16 = ds_scr[...]
                dk_ref[0, 0, ks, :] += lax.dot_general(ds16, qb, _TN, preferred_element_type=F32)
                dv_ref[0, 0, ks, :] += lax.dot_general(p16_scr[...], dob16, _TN, preferred_element_type=F32)
                return dq + jnp.dot(ds16, k_ref[0, 0, ks, :], preferred_element_type=F32)

            dq_ref[0, 0, qs, :] = lax.fori_loop(0, n_tiles, sweep2, jnp.zeros((T, dh), F32)) * scale
            return 0

        lax.fori_loop(0, nt, qloop, 0)
        dk_ref[...] = dk_ref[...] * scale

    full = lambda w: pl.BlockSpec((1, 1, S, w), lambda b, h: (b, h, 0, 0))
    fks = pl.BlockSpec((1, 1, nkt, TK), lambda b, h: (b, h, 0, 0))
    return pl.pallas_call(
        body, name=_unique(name), grid=(B, H),
        in_specs=[full(dh), full(dh), full(dh), full(1), fks, full(dh), full(1)],
        out_specs=[full(dh), full(dh), full(dh), fks],
        out_shape=[jax.ShapeDtypeStruct((B, H, S, dh), F32)] * 3 + [jax.ShapeDtypeStruct((B, H, nkt, TK), F32)],
        scratch_shapes=[pltpu.VMEM((nkt, T, TK), F32), pltpu.VMEM((nkt, T, TK), F32), pltpu.VMEM((T, TK), F32),
                        pltpu.VMEM((T, TK), BF16), pltpu.VMEM((T, TK), BF16)],
        compiler_params=_params(("parallel", "parallel")))(q, k, v, fq, fk, do, lse)


def _sb_terms(z, with_sigmoids=True):
    t = jnp.exp(-jnp.abs(z))
    lp = jnp.log(1.0 + t)
    lb, l1 = jnp.minimum(z, 0.0) - lp, jnp.minimum(-z, 0.0) - lp
    if not with_sigmoids:
        return lb, l1, None, None
    inv = 1.0 / (1.0 + t)
    pos = z >= 0.0
    return lb, l1, jnp.where(pos, 1.0, t) * inv, jnp.where(pos, t, 1.0) * inv


SCAN_RADIX = 4


def _lane_scan(x, reverse):
    lane = _iota2(x.shape, 1)
    y, d = x, 1
    while d < LANES:
        step = y
        for m in range(1, SCAN_RADIX):
            if m * d < LANES:
                if reverse:
                    step = step + jnp.where(lane + m * d < LANES, pltpu.roll(y, LANES - m * d, 1), 0.0)
                else:
                    step = step + jnp.where(lane >= m * d, pltpu.roll(y, m * d, 1), 0.0)
        y, d = step, d * SCAN_RADIX
    return y


def _chunk_scan(xs, reverse):
    n = len(xs)
    within, acc = [None] * n, None
    for s in (range(n - 1, -1, -1) if reverse else range(n)):
        acc = xs[s] if acc is None else acc + xs[s]
        within[s] = acc
    lanes = _lane_scan(acc, reverse)
    beyond = lanes - acc
    return [w + beyond for w in within], (lanes[:, 0:1] if reverse else lanes[:, LANES - 1:LANES])


SB_DEAD = 110.0


def _sb_fwd(name, q, k, v):
    B, H, S, dh = q.shape
    T, TK = _att_tiles(S, SB_KEY_BLOCK)
    scale = np.float32(dh ** -0.5)
    pieces = _pieces(T, TK)

    def body(q_ref, k_ref, v_ref, o_ref, lt_ref, first_ref, z_scr, a_scr):
        i = pl.program_id(2)
        qb = q_ref[0, 0]
        n_tiles = _causal_tiles(i, T, TK)

        def tile(j, carry, diag):
            runs, acc = carry
            ks = pl.ds(pl.multiple_of(j * TK, TK), TK)
            z_scr[...] = lax.dot_general(qb, k_ref[0, 0, ks, :], _NT, preferred_element_type=F32) * scale
            new_runs = []
            for rc, (rs, segs) in enumerate(pieces):
                terms = [_sb_terms(z_scr[rs, cs], False) for cs in segs]
                keep = [_piece_keep(i * T, j * TK, rs, cs, True, len(segs)) if diag else None for cs in segs]
                l1 = [jnp.where(kp, t[1], 0.0) if diag else t[1] for kp, t in zip(keep, terms)]
                right_of, total = _chunk_scan(l1, True)
                for cs, kp, t, x, r in zip(segs, keep, terms, l1, right_of):
                    a = jnp.exp(t[0] + ((r - x) + runs[rc]))
                    a_scr[rs, cs] = (jnp.where(kp, a, 0.0) if diag else a).astype(BF16)
                new_runs.append(runs[rc] + total)
            return tuple(new_runs), acc + jnp.dot(a_scr[...], v_ref[0, 0, ks, :], preferred_element_type=F32)

        init = (tuple(jnp.zeros((rs.stop - rs.start, 1), F32) for rs, _ in pieces), jnp.zeros((T, dh), F32))
        def some_row_alive(runs):
            worst = runs[0]
            for r in runs[1:]:
                worst = jnp.maximum(worst, r)
            return jnp.max(worst) > -SB_DEAD

        def step(c):
            runs, acc = tile(n_tiles - 1 - c[0], (c[1], c[2]), False)
            return c[0] + 1, runs, acc

        visited, runs, acc = lax.while_loop(lambda c: jnp.logical_and(c[0] < n_tiles, some_row_alive(c[1])), step,
                                            (jnp.int32(1), *tile(n_tiles - 1, init, True)))
        o_ref[0, 0] = acc
        lt_ref[0, 0] = jnp.concatenate(runs, axis=0)
        first_ref[pl.program_id(0), pl.program_id(1), i] = (n_tiles - visited).astype(F32)

    full = lambda w: pl.BlockSpec((1, 1, S, w), lambda b, h, i: (b, h, 0, 0))
    blk = lambda w: pl.BlockSpec((1, 1, T, w), lambda b, h, i: (b, h, i, 0))
    return pl.pallas_call(
        body, name=_unique(name), grid=(B, H, S // T),
        in_specs=[blk(dh), full(dh), full(dh)], out_specs=[blk(dh), blk(1), pl.BlockSpec(memory_space=pltpu.SMEM)],
        out_shape=[jax.ShapeDtypeStruct((B, H, S, dh), F32), jax.ShapeDtypeStruct((B, H, S, 1), F32),
                   jax.ShapeDtypeStruct((B, H, S // T), F32)],
        scratch_shapes=[pltpu.VMEM((T, TK), F32), pltpu.VMEM((T, TK), BF16)],
        compiler_params=_params(("arbitrary", "arbitrary", "arbitrary")))(q, k, v)


def _sb_bwd(name, q, k, v, do, lt, first):
    B, H, S, dh = q.shape
    T, TK = _att_tiles(S, SB_KEY_BLOCK)
    nt = S // T
    scale = np.float32(dh ** -0.5)

    pieces = _pieces(T, TK)

    def body(first_ref, q_ref, k_ref, v_ref, do_ref, lt_ref, dq_ref, dk_ref, dv_ref, z_scr, da_scr, dz_scr, a_scr):
        dk_ref[...] = jnp.zeros_like(dk_ref)
        dv_ref[...] = jnp.zeros_like(dv_ref)
        b, h = pl.program_id(0), pl.program_id(1)

        def qloop(i, _):
            qs = pl.ds(pl.multiple_of(i * T, T), T)
            qb, dob16 = q_ref[0, 0, qs, :], do_ref[0, 0, qs, :].astype(BF16)
            n_tiles = _causal_tiles(i, T, TK)
            first_tile = jnp.clip(first_ref[b, h, i].astype(jnp.int32), 0, n_tiles - 1)
            lt_c = [lt_ref[0, 0, pl.ds(pl.multiple_of(i * T + rs.start, SUBLANES), rs.stop - rs.start), :] for rs, _ in pieces]

            def tile(j, carry, diag):
                sums_l, sums_e, dq = carry
                ks = pl.ds(pl.multiple_of(j * TK, TK), TK)
                kb, vb = k_ref[0, 0, ks, :], v_ref[0, 0, ks, :]
                z_scr[...] = lax.dot_general(qb, kb, _NT, preferred_element_type=F32) * scale
                da_scr[...] = lax.dot_general(dob16, vb, _NT, preferred_element_type=F32)
                new_l, new_e = [], []
                for rc, (rs, segs) in enumerate(pieces):
                    terms = [_sb_terms(z_scr[rs, cs]) for cs in segs]
                    keep = [_piece_keep(i * T, j * TK, rs, cs, True, len(segs)) if diag else None for cs in segs]
                    l1 = [jnp.where(kp, t[1], 0.0) if diag else t[1] for kp, t in zip(keep, terms)]
                    upto, total_l = _chunk_scan(l1, False)
                    es = []
                    for cs, kp, t, u in zip(segs, keep, terms, upto):
                        a = jnp.exp(t[0] + (lt_c[rc] - (u + sums_l[rc])))
                        if diag:
                            a = jnp.where(kp, a, 0.0)
                        a_scr[rs, cs] = a.astype(BF16)
                        es.append(da_scr[rs, cs] * a)
                    e_upto, total_e = _chunk_scan(es, False)
                    for cs, kp, t, e, eu in zip(segs, keep, terms, es, e_upto):
                        dz = e * t[3] - ((eu - e) + sums_e[rc]) * t[2]
                        dz_scr[rs, cs] = (jnp.where(kp, dz, 0.0) if diag else dz).astype(BF16)
                    new_l.append(sums_l[rc] + total_l)
                    new_e.append(sums_e[rc] + total_e)
                dz16 = dz_scr[...]
                dk_ref[0, 0, ks, :] += lax.dot_general(dz16, qb, _TN, preferred_element_type=F32)
                dv_ref[0, 0, ks, :] += lax.dot_general(a_scr[...], dob16, _TN, preferred_element_type=F32)
                return tuple(new_l), tuple(new_e), dq + jnp.dot(dz16, kb, preferred_element_type=F32)

            zeros = tuple(jnp.zeros((rs.stop - rs.start, 1), F32) for rs, _ in pieces)
            carry = lax.fori_loop(first_tile, n_tiles - 1, lambda j, c: tile(j, c, False), (zeros, zeros, jnp.zeros((T, dh), F32)))
            dq_ref[0, 0, qs, :] = tile(n_tiles - 1, carry, True)[2] * scale
            return 0

        lax.fori_loop(0, nt, qloop, 0)
        dk_ref[...] = dk_ref[...] * scale

    full = lambda w: pl.BlockSpec((1, 1, S, w), lambda b, h: (b, h, 0, 0))
    return pl.pallas_call(
        body, name=_unique(name), grid=(B, H),
        in_specs=[pl.BlockSpec(memory_space=pltpu.SMEM), full(dh), full(dh), full(dh), full(dh), full(1)], out_specs=[full(dh)] * 3,
        out_shape=[jax.ShapeDtypeStruct((B, H, S, dh), F32)] * 3,
        scratch_shapes=[pltpu.VMEM((T, TK), F32), pltpu.VMEM((T, TK), F32), pltpu.VMEM((T, TK), BF16), pltpu.VMEM((T, TK), BF16)],
        compiler_params=_params(("parallel", "parallel")))(first, q, k, v, do, lt)


def _adamw(name, w, g, m, v):
    shape = w.shape
    n = w.size
    cols = shape[-1] if (w.ndim >= 2 and (shape[-1] % LANES == 0 or n // shape[-1] >= LANES)) else 0
    if cols:
        prep = lambda t: t.reshape(1, n // cols, cols)
    else:
        cols = LANES
        pad = (-n) % (SUBLANES * LANES)
        prep = lambda t: jnp.pad(t.reshape(-1), (0, pad), constant_values=1.0).reshape(1, (n + pad) // cols, cols)

    def fn(r, bv, cv):
        w_, g_, m_, v_ = r
        m2 = ADAM_B1 * m_ + (1.0 - ADAM_B1) * g_
        v2 = ADAM_B2 * v_ + (1.0 - ADAM_B2) * (g_ * g_)
        m_hat = m2 / (1.0 - ADAM_B1 ** ADAM_STEP)
        v_hat = v2 / (1.0 - ADAM_B2 ** ADAM_STEP)
        return [-ADAM_LR * (m_hat / (jnp.sqrt(v_hat) + ADAM_EPS) + ADAM_WD * w_), m2, v2], [], []
    outs = _rowwise(name, fn, [prep(w), prep(g), prep(m), prep(v)], out_rows=[(cols, F32)] * 3, tm=512)
    return [o.reshape(-1)[:n].reshape(shape) for o in outs]


def _sum8(name, parts):
    def fn(r, bv, cv):
        s = r[0]
        for t in r[1:]:
            s = s + t
        return [s], [], []
    rows = [parts[i][None] for i in range(parts.shape[0])]
    return _rowwise(name, fn, rows, out_rows=[(parts.shape[2], F32)])[0][0]


def _pack(arrs, cols, dtype, row_mult):
    flat = jnp.concatenate([a.reshape(-1).astype(dtype) for a in arrs])
    pad = (-flat.size) % (cols * row_mult)
    return jnp.pad(flat, (0, pad)).reshape(-1, cols)


def _unpack(flat, shapes):
    out, off = [], 0
    for s in shapes:
        n = int(np.prod(s))
        out.append(flat[off:off + n].reshape(s))
        off += n
    return out


def _heads(t, H):
    B, S, W = t.shape
    return t.reshape(B, S, H, W // H).transpose(0, 2, 1, 3)


def _unheads(t):
    B, H, S, dh = t.shape
    return t.transpose(0, 2, 1, 3).reshape(B, S, H * dh)


def kernel(*args):
    _names_used.clear()
    p = dict(zip(ARGS, args))
    x, target = p['x'], p['loss_target']
    B, S, D = x.shape
    T = B * S
    depth = p['ln1_g'].shape[0]
    H = D // HEAD_DIM
    alpha = np.float32((2.0 * depth) ** 0.25)
    cx, cy, cc = _mesh_pos()
    my_q = 2 * cx + cy
    axes = ("x", "y", "c")
    two = lambda t: t.reshape(T, t.shape[-1])
    three = lambda t: t.reshape(B, S, t.shape[-1])

    small_in = [p['c']] + [p[n] for n in SMALL_SPLIT]
    g1 = _all_gather8("ag_small", [_pack(small_in, LANES, F32, SUBLANES)])[0]
    g1 = g1.reshape(N_DEV, -1)
    c_all = g1[:, :B * D].reshape(N_DEV * B, D)
    per_chip = [_unpack(g1[2 * q], [a.shape for a in small_in])[1:] for q in range(N_CHIPS)]
    small = {n: jnp.concatenate([per_chip[q][i] for q in range(N_CHIPS)], axis=-1) for i, n in enumerate(SMALL_SPLIT)}
    for n in SMALL_REPL:
        small[n] = p[n]

    n_seq = N_DEV * B
    seq_pad = -(-n_seq // LANES) * LANES
    c_act = _rowwise("c_act", lambda r, bv, cv: ([_silu(r[0])], [], []),
                     [jnp.pad(c_all, ((0, seq_pad - n_seq), (0, 0)))[None]], out_rows=[(D, F32)])[0][0]
    mod_cols = p['mod_w'].shape[2]
    mod_part = jnp.stack([_mm(f"mod_fwd{l}", c_act, p['mod_w'][l])[:n_seq] for l in range(depth)])
    half_layers = depth // 2
    mod_half = lax.dynamic_slice_in_dim(mod_part, cc * half_layers, half_layers, axis=0)
    gm_ = _all_gather8("ag_mod", [mod_half.reshape(half_layers * n_seq, mod_cols)])[0]
    mod_all = gm_.reshape(N_CHIPS, 2, half_layers, n_seq, mod_cols).transpose(1, 2, 3, 0, 4).reshape(depth, n_seq, 6 * D)
    mod_mine = lax.dynamic_slice_in_dim(mod_all, (2 * my_q + cc) * B, B, axis=1)
    mod = _rowwise("mod_bias", lambda r, bv, cv: ([r[0] + bv[0]], [], []), [mod_mine], [p['mod_b'][:, None, :]],
                   out_rows=[(6 * D, F32)])[0]
    mods = [[mod[l, :, None, i * D:(i + 1) * D] for i in range(6)] for l in range(depth)]

    big_names = list(BIG)
    shard_shapes = [p[n].shape for n in big_names]
    half_rows = [s[0] * s[1] // 2 for s in shard_shapes]
    w_halves = [lax.dynamic_slice_in_dim(p[n].reshape(-1, s[2]), cc * hr, hr, axis=0).astype(BF16)
                for n, s, hr in zip(big_names, shard_shapes, half_rows)]
    W = {}
    for n, s, g in zip(big_names, shard_shapes, _all_gather8("ag_weights", w_halves)):
        seg = g.reshape((N_CHIPS,) + s)
        W[n] = jnp.concatenate([seg[q] for q in range(N_CHIPS)], axis=BIG[n])

    def vec(n, j):
        return small[n][j][None, :]

    def attn_proj(h1, w_in, gate_cols):
        wp = jnp.pad(w_in, ((0, 0), (0, gate_cols))) if gate_cols else w_in
        proj = three(_mm("att_proj", two(h1), wp))
        q, k, v = [_heads(proj[..., i * D:(i + 1) * D].astype(BF16), H) for i in range(3)]
        return wp, proj, q, k, v

    def gm_fwd(j, h1):
        zin = three(_mm("gm_in", two(h1), W['gm_w_in'][j]))
        u, vn = _gm_act("gm_act", zin, vec('gm_b_in', j), vec('gm_ln_g', j), vec('gm_ln_b', j))
        b_sT = small['gm_b_s'][j].T
        yg = _gm_spatial("gm_spatial", u, vn, small['gm_w_s'][j], b_sT)
        return three(_mm("gm_out", two(yg), W['gm_w_out'][j])), (zin, u, vn, b_sT, yg)

    def gm_bwd(j, h1, dy1, cache):
        zin, u, vn, b_sT, yg = cache
        g = {'gm_w_out': _mm("gm_dwout", two(yg), two(dy1), ta=True)}
        dyg = three(_mm("gm_dyg", two(dy1), W['gm_w_out'][j], tb=True))
        du, dvn, dws, dbsT = _gm_spatial_bwd("gm_spatial_bwd", dyg, u, vn, small['gm_w_s'][j], b_sT)
        dzin, g['gm_b_in'], g['gm_ln_g'], g['gm_ln_b'] = _gm_act_bwd("gm_act_bwd", zin, du, dvn, vec('gm_b_in', j), vec('gm_ln_g', j))
        g['gm_w_s'], g['gm_b_s'] = dws, dbsT.T
        g['gm_w_in'] = _mm("gm_dwin", two(h1), two(dzin), ta=True)
        return _mm("gm_dh", two(dzin), W['gm_w_in'][j], tb=True), g

    def fox_fwd(j, h1):
        wp, proj, q, k, v = attn_proj(h1, W['fox_w_in'][j], 3 * D + FOX_GATE_COLS - W['fox_w_in'].shape[2])
        fl = proj[..., 3 * D:]
        bf = jnp.pad(small['fox_b_f'][j][None, :], ((0, 0), (0, FOX_GATE_COLS - H)))
        Fh = _fox_gate_cumsum("fox_gate", fl, bf)[..., :H].transpose(0, 2, 1)
        fq, fk = Fh[..., None], Fh.reshape(B, H, -1, _att_tiles(S, FOX_KEY_BLOCK)[1])
        o, lse = _fox_fwd("fox_fwd", q, k, v, fq, fk)
        o2 = _unheads(o)
        return three(_mm("fox_out", two(o2), W['fox_w_out'][j])), (wp, q, k, v, fl, bf, fq, fk, lse, o2)

    def fox_bwd(j, h1, dy1, cache):
        wp, q, k, v, fl, bf, fq, fk, lse, o2 = cache
        g = {'fox_w_out': _mm("fox_dwout", two(o2), two(dy1), ta=True)}
        do = _heads(three(_mm("fox_do", two(dy1), W['fox_w_out'][j], tb=True)), H)
        dq, dk, dv, dfk = _fox_bwd("fox_bwd", q, k, v, fq, fk, do, lse)
        dF = jnp.pad(dfk.reshape(B, H, S).transpose(0, 2, 1), ((0, 0), (0, 0), (0, FOX_GATE_COLS - H)))
        dfl, dbf = _fox_gate_bwd("fox_gate_bwd", dF, fl, bf, H)
        dproj = jnp.concatenate([_unheads(dq).astype(BF16), _unheads(dk).astype(BF16), _unheads(dv).astype(BF16),
                                 dfl.astype(BF16)], axis=-1)
        g['fox_w_in'] = _mm("fox_dwin", two(h1), two(dproj), ta=True)[:, :W['fox_w_in'].shape[2]]
        g['fox_b_f'] = dbf[0, :H]
        return _mm("fox_dh", two(dproj), wp, tb=True), g

    def sb_fwd(j, h1):
        wp, proj, q, k, v = attn_proj(h1, W['sb_w_in'][j], 0)
        k, v = _lane_major(k, _att_tiles(S, SB_KEY_BLOCK)[1]), _lane_major(v, _att_tiles(S, SB_KEY_BLOCK)[1])
        o, lt, first = _sb_fwd("sb_fwd", q, k, v)
        o2 = _unheads(o)
        return three(_mm("sb_out", two(o2), W['sb_w_out'][j])), (q, k, v, lt, first, o2)

    def sb_bwd(j, h1, dy1, cache):
        q, k, v, lt, first, o2 = cache
        g = {'sb_w_out': _mm("sb_dwout", two(o2), two(dy1), ta=True)}
        do = _heads(three(_mm("sb_do", two(dy1), W['sb_w_out'][j], tb=True)), H)
        dq, dk, dv = _sb_bwd("sb_bwd", q, k, v, do, lt, first)
        dk, dv = _lane_major_inverse(dk, _att_tiles(S, SB_KEY_BLOCK)[1]), _lane_major_inverse(dv, _att_tiles(S, SB_KEY_BLOCK)[1])
        dproj = jnp.concatenate([_unheads(dq).astype(BF16), _unheads(dk).astype(BF16), _unheads(dv).astype(BF16)], axis=-1)
        g['sb_w_in'] = _mm("sb_dwin", two(h1), two(dproj), ta=True)
        return _mm("sb_dh", two(dproj), W['sb_w_in'][j], tb=True), g

    def cv_fwd(j, h1):
        pw = three(_mm("cv_in", two(h1), W['cv_w_in'][j]))
        ygl = _cv_glu("cv_glu", pw, vec('cv_b_in', j))
        dw = jnp.pad(small['cv_dw'][j], ((0, CONV_HALO - CONV_WIDTH), (0, 0)))
        yc = _dwconv("cv_dwconv", ygl, dw, vec('cv_dw_b', j))
        ys = _cv_ln_act("cv_ln_act", yc, vec('cv_ln_g', j), vec('cv_ln_b', j))
        return three(_mm("cv_out", two(ys), W['cv_w_out'][j])), (pw, ygl, dw, yc, ys)

    def cv_bwd(j, h1, dy1, cache):
        pw, ygl, dw, yc, ys = cache
        g = {'cv_w_out': _mm("cv_dwout", two(ys), two(dy1), ta=True)}
        dys = three(_mm("cv_dys", two(dy1), W['cv_w_out'][j], tb=True))
        dyc, g['cv_ln_g'], g['cv_ln_b'], g['cv_dw_b'] = _cv_ln_act_bwd("cv_ln_act_bwd", yc, dys, vec('cv_ln_g', j), vec('cv_ln_b', j))
        dygl, ddw = _dwconv_bwd("cv_dwconv_bwd", dyc, ygl, dw)
        g['cv_dw'] = ddw[:CONV_WIDTH]
        dpw, g['cv_b_in'] = _cv_glu_bwd("cv_glu_bwd", pw, dygl, vec('cv_b_in', j))
        g['cv_w_in'] = _mm("cv_dwin", two(h1), two(dpw), ta=True)
        return _mm("cv_dh", two(dpw), W['cv_w_in'][j], tb=True), g

    mixers = [(gm_fwd, gm_bwd), (fox_fwd, fox_bwd), (sb_fwd, sb_bwd), (cv_fwd, cv_bwd)]
    n_mix = len(mixers)

    saved = []
    for l in range(depth):
        m, j = l % n_mix, l // n_mix
        sh1, sc1, g1_, sh2, sc2, g2_ = mods[l]
        ybias = vec('cv_b_out', j) if m == 3 else None
        h1 = _modulate("mod1", x, sc1, sh1)
        y1, cache = mixers[m][0](j, h1)
        xm = _resid_ln("resid_ln1", alpha, x, y1, g1_, small['ln1_g'][l][None], small['ln1_b'][l][None], ybias)
        h2 = _modulate("mod2", xm, sc2, sh2)
        z = three(_mm("ffn_in", two(h2), W['ffn_w_in'][l]))
        a = _swiglu_act("ffn_act", z)
        y2 = three(_mm("ffn_out", two(a), W['ffn_w_out'][l]))
        xo = _resid_ln("resid_ln2", alpha, xm, y2, g2_, small['ln2_g'][l][None], small['ln2_b'][l][None])
        saved.append((x, h1, y1, cache, xm, h2, z, a, y2, ybias))
        x = xo

    dx, sq = _loss_head("loss_head", x, target)
    loss = lax.psum(jnp.sum(sq) * np.float32(0.5 / D), axes)

    grads = {n: [None] * p[n].shape[0] for n in WEIGHTS}
    dmod = [None] * depth
    for l in reversed(range(depth)):
        m, j = l % n_mix, l // n_mix
        sh1, sc1, g1_, sh2, sc2, g2_ = mods[l]
        x_in, h1, y1, cache, xm, h2, z, a, y2, ybias = saved[l]
        dr2, dy2, dg2, grads['ln2_g'][l], grads['ln2_b'][l], _ = _resid_ln_bwd("resid_ln2_bwd", alpha, dx, xm, y2, g2_, small['ln2_g'][l][None])
        grads['ffn_w_out'][l] = _mm("ffn_dwout", two(a), two(dy2), ta=True)
        da = three(_mm("ffn_da", two(dy2), W['ffn_w_out'][l], tb=True))
        dz = _swiglu_act_bwd("ffn_act_bwd", da, z)
        grads['ffn_w_in'][l] = _mm("ffn_dwin", two(h2), two(dz), ta=True)
        dh2 = three(_mm("ffn_dh", two(dz), W['ffn_w_in'][l], tb=True))
        dxm, dsc2, dsh2 = _modulate_bwd("mod2_bwd", alpha, dh2, dr2, xm, sc2)
        dr1, dy1, dg1, grads['ln1_g'][l], grads['ln1_b'][l], dyb = _resid_ln_bwd("resid_ln1_bwd", alpha, dxm, x_in, y1, g1_, small['ln1_g'][l][None], ybias)
        dh1, mg = mixers[m][1](j, h1, dy1, cache)
        if m == 3:
            mg['cv_b_out'] = dyb
        for n, gval in mg.items():
            grads[n][j] = gval
        dx, dsc1, dsh1 = _modulate_bwd("mod1_bwd", alpha, three(dh1), dr1, x_in, sc1)
        dmod[l] = jnp.concatenate([dsh1, dsc1, dg1, dsh2, dsc2, dg2], axis=-1)[:, 0, :]
    grad_x = dx
    dmod = jnp.stack(dmod)
    grads['mod_b'] = [jnp.sum(dmod[l], axis=0) for l in range(depth)]
    full_shape = {n: tuple(t.shape) for n, t in small.items()}

    small_names = SMALL_REPL + SMALL_SPLIT
    small_parts = [jnp.stack([gv.reshape(full_shape[n][1:]) for gv in grads[n]]) for n in small_names]
    pack_a = _pack([dmod], LANES, F32, SUBLANES)
    pack_b = _pack(small_parts, LANES, F32, SUBLANES)
    g2 = _all_gather8("ag_grads_small", [jnp.concatenate([pack_a, pack_b], axis=0)])[0]
    rows_a = pack_a.shape[0]
    dmod_all = g2[:, :rows_a].reshape(N_DEV, -1)[:, :dmod.size].reshape(N_DEV, depth, B, 6 * D)
    dmod_all = dmod_all.transpose(1, 0, 2, 3).reshape(depth, n_seq, 6 * D)
    small_sum = _sum8("sum_grads_small", g2[:, rows_a:]).reshape(-1)
    g_small = dict(zip(small_names, _unpack(small_sum, [full_shape[n] for n in small_names])))
    for n in SMALL_SPLIT:
        w = p[n].shape[-1]
        g_small[n] = lax.dynamic_slice_in_dim(g_small[n], my_q * w, w, axis=g_small[n].ndim - 1)

    dm_cols = lax.dynamic_slice_in_dim(dmod_all, my_q * mod_cols, mod_cols, axis=2)
    dm_cols = jnp.pad(dm_cols, ((0, 0), (0, seq_pad - n_seq), (0, 0)))
    g_mod_w = jnp.stack([_mm(f"mod_dw{l}", c_act, dm_cols[l], ta=True) for l in range(depth)])

    keep, give = [], []
    for n, s, hr in zip(big_names, shard_shapes, half_rows):
        gfull = jnp.stack(grads[n])
        g4 = jnp.stack(jnp.split(gfull, N_CHIPS, axis=BIG[n])).reshape(N_CHIPS, 2 * hr, s[2])
        keep.append(lax.dynamic_slice_in_dim(g4, cc * hr, hr, axis=1))
        give.append(lax.dynamic_slice_in_dim(g4, (1 - cc) * hr, hr, axis=1))
    got = _sibling_exchange("rs_sibling", give)
    chip_sum = [_rowwise("rs_add_sibling", lambda r, bv, cv: ([r[0] + r[1]], [], []),
                         [a.reshape(1, -1, a.shape[2]), b.reshape(1, -1, a.shape[2])],
                         out_rows=[(a.shape[2], BF16)], tm=512)[0].reshape(a.shape) for a, b in zip(keep, got)]
    from_chips = _chip_all_to_all("rs_chips", chip_sum)
    half_sum = [_rowwise("rs_add_chips", lambda r, bv, cv: ([((r[0] + r[1]) + r[2]) + r[3]], [], []),
                         [t[q][None] for q in range(N_CHIPS)], out_rows=[(t.shape[2], F32)], tm=512)[0][0]
                for t in from_chips]
    other = _sibling_exchange("rs_share", half_sum)
    g_big = {n: jnp.concatenate([jnp.where(cc == 0, a, b), jnp.where(cc == 0, b, a)], axis=0).reshape(s)
             for n, s, a, b in zip(big_names, shard_shapes, half_sum, other)}

    g_out = {**g_small, **g_big, 'mod_w': g_mod_w}
    upd = {n: _adamw("adamw_" + n, p[n], g_out[n], p['m_' + n], p['v_' + n]) for n in WEIGHTS}
    return (loss, grad_x, *[g_out[n] for n in WEIGHTS], *[upd[n][0] for n in WEIGHTS],
            *[upd[n][1] for n in WEIGHTS], *[upd[n][2] for n in WEIGHTS])
```

```python
import math

import jax
import jax.numpy as jnp
import numpy as np
from jax import lax
from jax.experimental import pallas as pl
from jax.experimental.pallas import tpu as pltpu

F32, BF16 = jnp.float32, jnp.bfloat16

HEAD_DIM = 64
CONV_WIDTH = 31
LN_EPS = 1e-5
NEG_INF = -1e30
ADAM_LR, ADAM_B1, ADAM_B2, ADAM_EPS, ADAM_WD, ADAM_STEP = 0.001, 0.9, 0.999, 1e-08, 0.01, 10

LANES = 128
SUBLANES = 8
VMEM_LIMIT_BYTES = 56 * 1024 * 1024
N_CHIPS = 4
N_DEV = 8

WEIGHTS = ['mod_w', 'mod_b', 'ln1_g', 'ln1_b', 'ln2_g', 'ln2_b', 'ffn_w_in', 'ffn_w_out', 'gm_w_in', 'gm_b_in',
           'gm_ln_g', 'gm_ln_b', 'gm_w_s', 'gm_b_s', 'gm_w_out', 'fox_w_in', 'fox_b_f', 'fox_w_out', 'sb_w_in',
           'sb_w_out', 'cv_w_in', 'cv_b_in', 'cv_dw', 'cv_dw_b', 'cv_ln_g', 'cv_ln_b', 'cv_w_out', 'cv_b_out']
ARGS = ['x', 'c'] + WEIGHTS + ['loss_target'] + ['m_' + n for n in WEIGHTS] + ['v_' + n for n in WEIGHTS]
BIG = {'ffn_w_in': 2, 'ffn_w_out': 1, 'gm_w_in': 2, 'gm_w_out': 1, 'fox_w_in': 2, 'fox_w_out': 1,
       'sb_w_in': 2, 'sb_w_out': 1, 'cv_w_in': 2, 'cv_w_out': 1}
SMALL_SPLIT = ['cv_b_in', 'cv_dw', 'cv_dw_b', 'cv_ln_g', 'cv_ln_b', 'cv_b_out']
SMALL_REPL = ['mod_b', 'ln1_g', 'ln1_b', 'ln2_g', 'ln2_b', 'gm_b_in', 'gm_ln_g', 'gm_ln_b', 'gm_w_s', 'gm_b_s', 'fox_b_f']
PACK_COLS = 1024


_names_used = {}


def _unique(name):
    k = _names_used.get(name, 0)
    _names_used[name] = k + 1
    return name if k == 0 else f"{name}_{k}"


def _params(sem):
    return pltpu.CompilerParams(dimension_semantics=sem, vmem_limit_bytes=VMEM_LIMIT_BYTES)


def _pick(dim, pref, mult=LANES):
    if dim <= pref:
        return dim
    best = 0
    for t in range(mult, pref + 1, mult):
        if dim % t == 0:
            best = t
    assert best, (dim, pref)
    return best


def _mesh_pos():
    return lax.axis_index("x"), lax.axis_index("y"), lax.axis_index("c")


AG_COPIES = 7
A2A_COPIES = 3


def _comm_call(name, body, blks, out_shapes, n_sems):
    n = len(blks)
    hbm = pl.BlockSpec(memory_space=pl.ANY)
    return pl.pallas_call(
        body, name=_unique(name), out_shape=out_shapes, in_specs=[hbm] * n, out_specs=[hbm] * n,
        scratch_shapes=[pltpu.SemaphoreType.DMA((n_sems * n,)), pltpu.SemaphoreType.DMA((n_sems * n,)),
                        pltpu.SemaphoreType.DMA((n,))],
    )(*blks)


def _all_gather8(name, blks):
    n = len(blks)

    def body(*refs):
        x_refs, out_refs, (send_sems, recv_sems, local_sems) = refs[:n], refs[n:2 * n], refs[2 * n:]
        x, y, c = _mesh_pos()
        me, sibling = (x, y, c), (x, y, 1 - c)
        chips = [(1 - x, y), (x, 1 - y), (1 - x, 1 - y)]

        def copy(a, k, block, to, from_input=False):
            px, py, pc = block
            slot = out_refs[a].at[4 * px + 2 * py + pc]
            return pltpu.make_async_remote_copy(
                src_ref=x_refs[a] if from_input else slot, dst_ref=slot,
                send_sem=send_sems.at[AG_COPIES * a + k], recv_sem=recv_sems.at[AG_COPIES * a + k],
                device_id=to, device_id_type=pl.DeviceIdType.MESH)

        local, sent = [], []
        for a in range(n):
            local.append(pltpu.make_async_copy(x_refs[a], out_refs[a].at[4 * x + 2 * y + c], local_sems.at[a]))
            local[-1].start()
            first = [copy(a, 0, me, sibling, True)] + [copy(a, 1 + j, me, (*chip, c), True) for j, chip in enumerate(chips)]
            for cp in first:
                cp.start()
            sent += first
        for a in range(n):
            for j, chip in enumerate(chips):
                copy(a, 1 + j, (*chip, c), me).wait_recv()
                sent.append(copy(a, 4 + j, (*chip, c), sibling))
                sent[-1].start()
        for a in range(n):
            copy(a, 0, sibling, me).wait_recv()
            for j, chip in enumerate(chips):
                copy(a, 4 + j, (*chip, 1 - c), me).wait_recv()
        for cp in sent:
            cp.wait_send()
        for cp in local:
            cp.wait()

    return _comm_call(name, body, blks, [jax.ShapeDtypeStruct((N_DEV,) + b.shape, b.dtype) for b in blks], AG_COPIES)


def _sibling_exchange(name, blks):
    n = len(blks)

    def body(*refs):
        x_refs, out_refs, (send_sems, recv_sems, _) = refs[:n], refs[n:2 * n], refs[2 * n:]
        x, y, c = _mesh_pos()
        cps = [pltpu.make_async_remote_copy(src_ref=x_refs[a], dst_ref=out_refs[a], send_sem=send_sems.at[a],
                                            recv_sem=recv_sems.at[a], device_id=(x, y, 1 - c),
                                            device_id_type=pl.DeviceIdType.MESH) for a in range(n)]
        for cp in cps:
            cp.start()
        for cp in cps:
            cp.wait()

    return _comm_call(name, body, blks, [jax.ShapeDtypeStruct(b.shape, b.dtype) for b in blks], 1)


def _chip_all_to_all(name, blks):
    n = len(blks)

    def body(*refs):
        x_refs, out_refs, (send_sems, recv_sems, local_sems) = refs[:n], refs[n:2 * n], refs[2 * n:]
        x, y, c = _mesh_pos()
        chips = [(1 - x, y), (x, 1 - y), (1 - x, 1 - y)]
        my_q = 2 * x + y

        def copy(a, j, src_q, dst_q):
            px, py = chips[j]
            return pltpu.make_async_remote_copy(
                src_ref=x_refs[a].at[src_q], dst_ref=out_refs[a].at[dst_q],
                send_sem=send_sems.at[A2A_COPIES * a + j], recv_sem=recv_sems.at[A2A_COPIES * a + j],
                device_id=(px, py, c), device_id_type=pl.DeviceIdType.MESH)

        local, sent = [], []
        for a in range(n):
            local.append(pltpu.make_async_copy(x_refs[a].at[my_q], out_refs[a].at[my_q], local_sems.at[a]))
            local[-1].start()
            sent += [copy(a, j, 2 * px + py, my_q) for j, (px, py) in enumerate(chips)]
            for cp in sent[-A2A_COPIES:]:
                cp.start()
        for a in range(n):
            for j, (px, py) in enumerate(chips):
                copy(a, j, my_q, 2 * px + py).wait_recv()
        for cp in sent:
            cp.wait_send()
        for cp in local:
            cp.wait()

    return _comm_call(name, body, blks, [jax.ShapeDtypeStruct(b.shape, b.dtype) for b in blks], A2A_COPIES)


def _rowwise(name, fn, rows, bvecs=(), cvecs=(), out_rows=(), out_bsums=(), out_tsums=(), tm=256):
    B, S = rows[0].shape[:2]
    tm = _pick(S, tm, SUBLANES)
    n_r, n_b, n_c = len(rows), len(bvecs), len(cvecs)
    n_or, n_ob = len(out_rows), len(out_bsums)

    def body(*refs):
        ins, outs = refs[:n_r + n_b + n_c], refs[n_r + n_b + n_c:]
        r = [ref[0] for ref in ins[:n_r]]
        bv = [ref[0] for ref in ins[n_r:n_r + n_b]]
        cv = [ref[...] for ref in ins[n_r + n_b:]]
        o_rows, o_bsums, o_tsums = fn(r, bv, cv)
        b, i = pl.program_id(0), pl.program_id(1)
        for ref, val in zip(outs[:n_or], o_rows):
            ref[0] = val.astype(ref.dtype)
        for ref, val in zip(outs[n_or:n_or + n_ob], o_bsums):
            @pl.when(i == 0)
            def _(ref=ref, val=val):
                ref[0] = val

            @pl.when(i > 0)
            def _(ref=ref, val=val):
                ref[0] += val
        for ref, val in zip(outs[n_or + n_ob:], o_tsums):
            first = jnp.logical_and(b == 0, i == 0)

            @pl.when(first)
            def _(ref=ref, val=val):
                ref[...] = val

            @pl.when(jnp.logical_not(first))
            def _(ref=ref, val=val):
                ref[...] += val

    in_specs = [pl.BlockSpec((1, tm, a.shape[2]), lambda b, i: (b, i, 0)) for a in rows]
    in_specs += [pl.BlockSpec((1, 1, a.shape[2]), lambda b, i: (b, 0, 0)) for a in bvecs]
    in_specs += [pl.BlockSpec((1, a.shape[1]), lambda b, i: (0, 0)) for a in cvecs]
    out_shape = [jax.ShapeDtypeStruct((B, S, cdim), dt) for cdim, dt in out_rows]
    out_specs = [pl.BlockSpec((1, tm, cdim), lambda b, i: (b, i, 0)) for cdim, _ in out_rows]
    out_shape += [jax.ShapeDtypeStruct((B, 1, cdim), F32) for cdim in out_bsums]
    out_specs += [pl.BlockSpec((1, 1, cdim), lambda b, i: (b, 0, 0)) for cdim in out_bsums]
    out_shape += [jax.ShapeDtypeStruct((1, cdim), F32) for cdim in out_tsums]
    out_specs += [pl.BlockSpec((1, cdim), lambda b, i: (0, 0)) for cdim in out_tsums]
    sem = ("arbitrary", "arbitrary") if out_tsums else ("parallel", "arbitrary")
    res = pl.pallas_call(body, name=_unique(name), grid=(B, S // tm), in_specs=in_specs, out_specs=out_specs,
                         out_shape=out_shape, compiler_params=_params(sem))(*rows, *bvecs, *cvecs)
    return list(res)


MM_TILE = 1536
MM_ROWS = 512
MM_WEIGHT_TILE_BYTES = 12 * 1024 * 1024


def _mm(name, a, b, ta=False, tb=False, out_dtype=F32):
    M, K = (a.shape[1], a.shape[0]) if ta else a.shape
    N = b.shape[0] if tb else b.shape[1]
    assert (b.shape[1] if tb else b.shape[0]) == K, (a.shape, b.shape, ta, tb)
    tn = _pick(N, MM_TILE)
    if ta:
        tm, tk = _pick(M, MM_TILE), _pick(K, 2 * MM_ROWS, LANES if tb else SUBLANES)
    else:
        tm = _pick(M, MM_ROWS, SUBLANES)
        tk = K if K * tn * 2 <= MM_WEIGHT_TILE_BYTES else _pick(K, MM_TILE)
    nk = K // tk
    dims = (((0 if ta else 1,), (1 if tb else 0,)), ((), ()))

    def body(a_ref, b_ref, o_ref, acc_ref):
        k = pl.program_id(2)
        p = lax.dot_general(a_ref[...].astype(BF16), b_ref[...].astype(BF16), dims, preferred_element_type=F32)
        if nk == 1:
            o_ref[...] = p.astype(o_ref.dtype)
        else:
            @pl.when(k == 0)
            def _():
                acc_ref[...] = p

            @pl.when(k > 0)
            def _():
                acc_ref[...] += p

            @pl.when(k == nk - 1)
            def _():
                o_ref[...] = acc_ref[...].astype(o_ref.dtype)

    a_spec = pl.BlockSpec((tk, tm), lambda j, i, k: (k, i)) if ta else pl.BlockSpec((tm, tk), lambda j, i, k: (i, k))
    b_spec = pl.BlockSpec((tn, tk), lambda j, i, k: (j, k)) if tb else pl.BlockSpec((tk, tn), lambda j, i, k: (k, j))
    return pl.pallas_call(
        body, name=_unique(name), grid=(N // tn, M // tm, nk), in_specs=[a_spec, b_spec],
        out_specs=pl.BlockSpec((tm, tn), lambda j, i, k: (i, j)),
        out_shape=jax.ShapeDtypeStruct((M, N), out_dtype),
        scratch_shapes=[pltpu.VMEM((tm, tn) if nk > 1 else (SUBLANES, LANES), F32)],
        compiler_params=_params(("parallel", "parallel", "arbitrary")))(a, b)


def _silu(x):
    return x * _sigmoid(x)


def _sigmoid(x):
    return 1.0 / (1.0 + jnp.exp(-x))


def _dsilu(x):
    s = _sigmoid(x)
    return s * (1.0 + x * (1.0 - s))


def _gelu(x):
    return 0.5 * x * (1.0 + lax.erf(x * np.float32(math.sqrt(0.5))))


def _dgelu(x):
    cdf = 0.5 * (1.0 + lax.erf(x * np.float32(math.sqrt(0.5))))
    pdf = jnp.exp(-0.5 * x * x) * np.float32(1.0 / math.sqrt(2.0 * math.pi))
    return cdf + x * pdf


def _ln_stats(r):
    mu = jnp.mean(r, axis=-1, keepdims=True)
    xc = r - mu
    var = jnp.mean(xc * xc, axis=-1, keepdims=True)
    rstd = lax.rsqrt(var + LN_EPS)
    return xc * rstd, rstd


def _ln_bwd(dxhat, xhat, rstd):
    m1 = jnp.mean(dxhat, axis=-1, keepdims=True)
    m2 = jnp.mean(dxhat * xhat, axis=-1, keepdims=True)
    return rstd * (dxhat - m1 - xhat * m2)


def _csum(v):
    return jnp.sum(v, axis=0, keepdims=True)


def _split_dot(x, m01, lhs01=False, terms=2):
    acc, rem = None, x
    for _ in range(terms):
        part = rem.astype(BF16)
        rem = rem - part.astype(F32)
        d = jnp.dot(m01, part, preferred_element_type=F32) if lhs01 else jnp.dot(part, m01, preferred_element_type=F32)
        acc = d if acc is None else acc + d
    return acc


def _iota2(shape, dim):
    return lax.broadcasted_iota(jnp.int32, shape, dim)


def _modulate(name, x, sc, sh):
    D = x.shape[2]
    return _rowwise(name, lambda r, bv, cv: ([r[0] * (1.0 + bv[0]) + bv[1]], [], []),
                    [x], [sc, sh], [], out_rows=[(D, BF16)])[0]


def _resid_ln(name, alpha, x, y, g, ln_g, ln_b, ybias=None, then=None):
    D = x.shape[2]

    def fn(r, bv, cv):
        yy = r[1] if ybias is None else r[1] + cv[2]
        xhat, _ = _ln_stats(alpha * r[0] + (1.0 + bv[0]) * yy)
        xn = xhat * cv[0] + cv[1]
        return [xn] + ([xn * (1.0 + bv[1]) + bv[2]] if then else []), [], []
    cvecs = [ln_g, ln_b] + ([] if ybias is None else [ybias])
    return _rowwise(name, fn, [x, y], [g] + list(then or ()), cvecs, out_rows=[(D, F32)] + ([(D, BF16)] if then else []))


def _resid_ln_bwd(name, alpha, dxn, x, y, g, ln_g, ln_b, ybias=None, then=None):
    D = x.shape[2]

    def fn(r, bv, cv):
        yy = r[2] if ybias is None else r[2] + cv[2]
        xhat, rstd = _ln_stats(alpha * r[1] + (1.0 + bv[0]) * yy)
        d, sums = r[0], []
        if then:
            d = alpha * d + r[3] * (1.0 + bv[1])
            sums = [_csum(r[3] * (xhat * cv[0] + cv[1])), _csum(r[3])]
        dr = _ln_bwd(d * cv[0], xhat, rstd)
        dy = (1.0 + bv[0]) * dr
        return [dr, dy], [_csum(dr * yy)] + sums, [_csum(d * xhat), _csum(d), _csum(dy)]
    cvecs = [ln_g, ln_b] + ([] if ybias is None else [ybias])
    rows, bvecs = [dxn, x, y] + ([then[0]] if then else []), [g] + ([then[1]] if then else [])
    return _rowwise(name, fn, rows, bvecs, cvecs, out_rows=[(D, F32), (D, BF16)], out_bsums=[D] * (3 if then else 1),
                    out_tsums=[D, D, D])


def _modulate_bwd(name, alpha, dh, dr, x, sc):
    D = x.shape[2]

    def fn(r, bv, cv):
        return [alpha * r[1] + r[0] * (1.0 + bv[0])], [_csum(r[0] * r[2]), _csum(r[0])], []
    return _rowwise(name, fn, [dh, dr, x], [sc], [], out_rows=[(D, F32)], out_bsums=[D, D])


def _loss_head(name, y, target):
    D = y.shape[2]

    def fn(r, bv, cv):
        e = r[0] - r[1]
        return [e * np.float32(1.0 / D)], [_csum(e * e)], []
    return _rowwise(name, fn, [y, target], [], [], out_rows=[(D, F32)], out_bsums=[D])


def _swiglu_act(name, z):
    Hd = z.shape[2] // 2
    return _rowwise(name, lambda r, bv, cv: ([_silu(r[0][:, :Hd]) * r[0][:, Hd:]], [], []), [z], out_rows=[(Hd, BF16)])[0]


def _swiglu_act_bwd(name, da, z):
    Hd = z.shape[2] // 2

    def fn(r, bv, cv):
        gg, u = r[1][:, :Hd], r[1][:, Hd:]
        return [jnp.concatenate([r[0] * u * _dsilu(gg), r[0] * _silu(gg)], axis=1)], [], []
    return _rowwise(name, fn, [da, z], out_rows=[(2 * Hd, BF16)])[0]


def _gm_act(name, zin, b_in, ln_g, ln_b):
    W = zin.shape[2] // 2

    def fn(r, bv, cv):
        z = _gelu(r[0] + cv[0])
        vhat, _ = _ln_stats(z[:, W:])
        return [z[:, :W], vhat * cv[1] + cv[2]], [], []
    return _rowwise(name, fn, [zin], [], [b_in, ln_g, ln_b], out_rows=[(W, F32), (W, BF16)])


def _gm_act_bwd(name, zin, du, dvn, b_in, ln_g):
    W = zin.shape[2] // 2

    def fn(r, bv, cv):
        zz = r[0] + cv[0]
        z = _gelu(zz)
        vhat, rstd = _ln_stats(z[:, W:])
        dv = _ln_bwd(r[2] * cv[1], vhat, rstd)
        dzin = jnp.concatenate([r[1], dv], axis=1) * _dgelu(zz)
        return [dzin], [], [_csum(dzin), _csum(r[2] * vhat), _csum(r[2])]
    return _rowwise(name, fn, [zin, du, dvn], [], [b_in, ln_g], out_rows=[(2 * W, BF16)], out_tsums=[2 * W, W, W])


def _gm_causal_w(ws_ref, g):
    T = ws_ref.shape[1]
    return jnp.where(_iota2((T, T), 1) <= _iota2((T, T), 0), ws_ref[g], 0.0).astype(BF16)


def _gm_spatial(name, u, vn, w_s, b_sT):
    B, S, W = u.shape
    G, T = w_s.shape[0], w_s.shape[1]
    assert W == G * T, "a head group is as wide as a chunk is long"

    def body(u_ref, vn_ref, ws_ref, bs_ref, y_ref):
        for g in range(G):
            cs = slice(g * T, (g + 1) * T)
            sv = jnp.dot(_gm_causal_w(ws_ref, g), vn_ref[0, :, cs], preferred_element_type=F32) + bs_ref[:, g:g + 1]
            y_ref[0, :, cs] = (u_ref[0, :, cs] * sv).astype(y_ref.dtype)

    row = pl.BlockSpec((1, T, W), lambda b, i: (b, i, 0))
    return pl.pallas_call(
        body, name=_unique(name), grid=(B, S // T),
        in_specs=[row, row, pl.BlockSpec((G, T, T), lambda b, i: (0, 0, 0)), pl.BlockSpec((T, G), lambda b, i: (0, 0))],
        out_specs=row, out_shape=jax.ShapeDtypeStruct((B, S, W), BF16),
        compiler_params=_params(("parallel", "parallel")))(u, vn, w_s, b_sT)


def _gm_spatial_bwd(name, dyg, u, vn, w_s, b_sT):
    B, S, W = u.shape
    G, T = w_s.shape[0], w_s.shape[1]
    assert W == G * T, "a head group is as wide as a chunk is long"

    def body(dy_ref, u_ref, vn_ref, ws_ref, bs_ref, du_ref, dvn_ref, dws_ref, dbs_ref):
        first = jnp.logical_and(pl.program_id(0) == 0, pl.program_id(1) == 0)

        @pl.when(first)
        def _():
            dws_ref[...] = jnp.zeros_like(dws_ref)
            dbs_ref[...] = jnp.zeros_like(dbs_ref)

        tril = _iota2((T, T), 1) <= _iota2((T, T), 0)
        for g in range(G):
            cs = slice(g * T, (g + 1) * T)
            wm = _gm_causal_w(ws_ref, g)
            vng = vn_ref[0, :, cs]
            sv = jnp.dot(wm, vng, preferred_element_type=F32) + bs_ref[:, g:g + 1]
            dy = dy_ref[0, :, cs]
            du_ref[0, :, cs] = dy * sv
            dsv = dy * u_ref[0, :, cs]
            dsv16 = dsv.astype(BF16)
            dvn_ref[0, :, cs] = lax.dot_general(wm, dsv16, (((0,), (0,)), ((), ())), preferred_element_type=F32)
            dw = lax.dot_general(dsv16, vng, (((1,), (1,)), ((), ())), preferred_element_type=F32)
            dws_ref[g] += jnp.where(tril, dw, 0.0)
            dbs_ref[:, g:g + 1] += jnp.sum(dsv, axis=1, keepdims=True)

    row = pl.BlockSpec((1, T, W), lambda b, i: (b, i, 0))
    return pl.pallas_call(
        body, name=_unique(name), grid=(B, S // T),
        in_specs=[row, row, row, pl.BlockSpec((G, T, T), lambda b, i: (0, 0, 0)), pl.BlockSpec((T, G), lambda b, i: (0, 0))],
        out_specs=[row, row, pl.BlockSpec((G, T, T), lambda b, i: (0, 0, 0)), pl.BlockSpec((T, G), lambda b, i: (0, 0))],
        out_shape=[jax.ShapeDtypeStruct((B, S, W), F32), jax.ShapeDtypeStruct((B, S, W), F32),
                   jax.ShapeDtypeStruct((G, T, T), F32), jax.ShapeDtypeStruct((T, G), F32)],
        compiler_params=_params(("arbitrary", "arbitrary")))(dyg, u, vn, w_s, b_sT)


def _cv_glu(name, pw, b_in):
    W = pw.shape[2] // 2

    def fn(r, bv, cv):
        z = r[0] + cv[0]
        return [z[:, :W] * _sigmoid(z[:, W:])], [], []
    return _rowwise(name, fn, [pw], [], [b_in], out_rows=[(W, F32)])[0]


def _cv_glu_bwd(name, pw, dyg, b_in):
    W = pw.shape[2] // 2

    def fn(r, bv, cv):
        z = r[0] + cv[0]
        a, s = z[:, :W], _sigmoid(z[:, W:])
        dpw = jnp.concatenate([r[1] * s, r[1] * a * s * (1.0 - s)], axis=1)
        return [dpw], [], [_csum(dpw)]
    return _rowwise(name, fn, [pw, dyg], [], [b_in], out_rows=[(2 * W, BF16)], out_tsums=[2 * W])


def _cv_ln_act(name, yc, ln_g, ln_b):
    D = yc.shape[2]

    def fn(r, bv, cv):
        xhat, _ = _ln_stats(r[0])
        return [_silu(xhat * cv[0] + cv[1])], [], []
    return _rowwise(name, fn, [yc], [], [ln_g, ln_b], out_rows=[(D, BF16)])[0]


def _cv_ln_act_bwd(name, yc, dys, ln_g, ln_b):
    D = yc.shape[2]

    def fn(r, bv, cv):
        xhat, rstd = _ln_stats(r[0])
        dyn = r[1] * _dsilu(xhat * cv[0] + cv[1])
        dyc = _ln_bwd(dyn * cv[0], xhat, rstd)
        return [dyc], [], [_csum(dyn * xhat), _csum(dyn), _csum(dyc)]
    return _rowwise(name, fn, [yc, dys], [], [ln_g, ln_b], out_rows=[(D, F32)], out_tsums=[D, D, D])


CONV_HALO = 32
CONV_TS, CONV_TC = 256, 128


def _dwconv(name, y, dw, dw_b):
    B, S, D = y.shape
    ts, tc, halo, K = _pick(S, CONV_TS, SUBLANES), _pick(D, CONV_TC), CONV_HALO, CONV_WIDTH

    def body(cur_ref, prev_ref, dw_ref, b_ref, o_ref, buf):
        i = pl.program_id(1)
        buf[pl.ds(0, halo), :] = jnp.where(i > 0, prev_ref[0, pl.ds(ts - halo, halo), :], 0.0)
        buf[pl.ds(halo, ts), :] = cur_ref[0]
        acc = jnp.zeros((ts, tc), F32) + b_ref[...]
        for k in range(K):
            acc = acc + dw_ref[k:k + 1, :] * buf[pl.ds(halo - (K - 1) + k, ts), :]
        o_ref[0] = acc

    return pl.pallas_call(
        body, name=_unique(name), grid=(B, S // ts, D // tc),
        in_specs=[pl.BlockSpec((1, ts, tc), lambda b, i, j: (b, i, j)),
                  pl.BlockSpec((1, ts, tc), lambda b, i, j: (b, jnp.maximum(i - 1, 0), j)),
                  pl.BlockSpec((halo, tc), lambda b, i, j: (0, j)), pl.BlockSpec((1, tc), lambda b, i, j: (0, j))],
        out_specs=pl.BlockSpec((1, ts, tc), lambda b, i, j: (b, i, j)),
        out_shape=jax.ShapeDtypeStruct((B, S, D), F32),
        scratch_shapes=[pltpu.VMEM((halo + ts, tc), F32)],
        compiler_params=_params(("parallel", "parallel", "parallel")))(y, y, dw, dw_b)


def _dwconv_bwd(name, dyc, y, dw):
    B, S, D = y.shape
    ts, tc, halo, K = _pick(S, CONV_TS, SUBLANES), _pick(D, CONV_TC), CONV_HALO, CONV_WIDTH
    nt = S // ts

    def body(g_ref, gnext_ref, y_ref, yprev_ref, dw_ref, dy_ref, ddw_ref, gbuf, ybuf):
        b, i = pl.program_id(1), pl.program_id(2)
        first = jnp.logical_and(b == 0, i == 0)

        @pl.when(first)
        def _():
            ddw_ref[...] = jnp.zeros_like(ddw_ref)

        g = g_ref[0]
        gbuf[pl.ds(0, ts), :] = g
        gbuf[pl.ds(ts, halo), :] = jnp.where(i < nt - 1, gnext_ref[0, pl.ds(0, halo), :], 0.0)
        ybuf[pl.ds(0, halo), :] = jnp.where(i > 0, yprev_ref[0, pl.ds(ts - halo, halo), :], 0.0)
        ybuf[pl.ds(halo, ts), :] = y_ref[0]
        acc = jnp.zeros((ts, tc), F32)
        for k in range(K):
            acc = acc + dw_ref[k:k + 1, :] * gbuf[pl.ds(K - 1 - k, ts), :]
            ddw_ref[k:k + 1, :] += _csum(g * ybuf[pl.ds(halo - (K - 1) + k, ts), :])
        dy_ref[0] = acc

    tile = lambda f: pl.BlockSpec((1, ts, tc), f)
    return pl.pallas_call(
        body, name=_unique(name), grid=(D // tc, B, nt),
        in_specs=[tile(lambda j, b, i: (b, i, j)), tile(lambda j, b, i: (b, jnp.minimum(i + 1, nt - 1), j)),
                  tile(lambda j, b, i: (b, i, j)), tile(lambda j, b, i: (b, jnp.maximum(i - 1, 0), j)),
                  pl.BlockSpec((halo, tc), lambda j, b, i: (0, j))],
        out_specs=[tile(lambda j, b, i: (b, i, j)), pl.BlockSpec((halo, tc), lambda j, b, i: (0, j))],
        out_shape=[jax.ShapeDtypeStruct((B, S, D), F32), jax.ShapeDtypeStruct((halo, D), F32)],
        scratch_shapes=[pltpu.VMEM((ts + halo, tc), F32), pltpu.VMEM((halo + ts, tc), F32)],
        compiler_params=_params(("parallel", "arbitrary", "arbitrary")))(dyc, dyc, y, y, dw)


ATT_BLOCK = 128
ATT_QUERY_BLOCK = 256
FOX_KEY_BLOCK = 1024
SB_KEY_BLOCK = 512
ATT_PIECE_ROWS = 32
FOX_GATE_COLS = 128


def _att_tiles(S, key_block):
    return _pick(S, ATT_QUERY_BLOCK, SUBLANES), _pick(S, key_block, LANES)


def _pieces(T, TK):
    R = min(T, ATT_PIECE_ROWS)
    segs = [slice(c, c + LANES) for c in range(0, TK, LANES)]
    return [(slice(r, r + R), segs) for r in range(0, T, R)]


def _piece_keep(row0, col0, rs, cs, strict, lane_major_of=0):
    shape = (rs.stop - rs.start, cs.stop - cs.start)
    lane = _iota2(shape, 1)
    key = col0 + (lane * lane_major_of + cs.start // LANES if lane_major_of else cs.start + lane)
    qry = row0 + rs.start + _iota2(shape, 0)
    return key < qry if strict else key <= qry


def _causal_tiles(i, tq, tk):
    return (i * tq + tq + tk - 1) // tk


def _lane_major(t, tk):
    B, H, S, dh = t.shape
    return t.reshape(B, H, S // tk, LANES, tk // LANES, dh).swapaxes(3, 4).reshape(B, H, S, dh)


def _lane_major_inverse(t, tk):
    B, H, S, dh = t.shape
    return t.reshape(B, H, S // tk, tk // LANES, LANES, dh).swapaxes(3, 4).reshape(B, H, S, dh)


def _log_sigmoid(x):
    return jnp.minimum(x, 0.0) - jnp.log(1.0 + jnp.exp(-jnp.abs(x)))


def _fox_gate_cumsum(name, fl, b_f):
    B, S, C = fl.shape
    T = _pick(S, ATT_BLOCK, SUBLANES)

    def body(fl_ref, bf_ref, f_ref, carry):
        @pl.when(pl.program_id(1) == 0)
        def _():
            carry[...] = jnp.zeros_like(carry)
        lf = _log_sigmoid(fl_ref[0] + bf_ref[...])
        lower = (_iota2((T, T), 1) <= _iota2((T, T), 0)).astype(BF16)
        f = _split_dot(lf, lower, lhs01=True, terms=3) + carry[...]
        f_ref[0] = f
        carry[...] = f[T - 1:T, :]

    return pl.pallas_call(
        body, name=_unique(name), grid=(B, S // T),
        in_specs=[pl.BlockSpec((1, T, C), lambda b, i: (b, i, 0)), pl.BlockSpec((1, C), lambda b, i: (0, 0))],
        out_specs=pl.BlockSpec((1, T, C), lambda b, i: (b, i, 0)),
        out_shape=jax.ShapeDtypeStruct((B, S, C), F32),
        scratch_shapes=[pltpu.VMEM((1, C), F32)],
        compiler_params=_params(("arbitrary", "arbitrary")))(fl, b_f)


def _fox_gate_bwd(name, dF, fl, b_f, n_heads):
    B, S, C = fl.shape
    T = _pick(S, ATT_BLOCK, SUBLANES)
    nt = S // T

    def body(df_ref, fl_ref, bf_ref, dfl_ref, dbf_ref, carry):
        first = jnp.logical_and(pl.program_id(0) == 0, pl.program_id(1) == 0)

        @pl.when(pl.program_id(1) == 0)
        def _():
            carry[...] = jnp.zeros_like(carry)

        @pl.when(first)
        def _():
            dbf_ref[...] = jnp.zeros_like(dbf_ref)

        upper = (_iota2((T, T), 1) >= _iota2((T, T), 0)).astype(BF16)
        dlf = _split_dot(df_ref[0], upper, lhs01=True, terms=3) + carry[...]
        carry[...] = dlf[0:1, :]
        x = fl_ref[0] + bf_ref[...]
        dfl = jnp.where(_iota2((T, C), 1) < n_heads, dlf * _sigmoid(-x), 0.0)
        dfl_ref[0] = dfl
        dbf_ref[...] += _csum(dfl)

    rev = lambda b, i: (b, nt - 1 - i, 0)
    return pl.pallas_call(
        body, name=_unique(name), grid=(B, nt),
        in_specs=[pl.BlockSpec((1, T, C), rev), pl.BlockSpec((1, T, C), rev), pl.BlockSpec((1, C), lambda b, i: (0, 0))],
        out_specs=[pl.BlockSpec((1, T, C), rev), pl.BlockSpec((1, C), lambda b, i: (0, 0))],
        out_shape=[jax.ShapeDtypeStruct((B, S, C), F32), jax.ShapeDtypeStruct((1, C), F32)],
        scratch_shapes=[pltpu.VMEM((1, C), F32)],
        compiler_params=_params(("arbitrary", "arbitrary")))(dF, fl, b_f)


_NT = (((1,), (1,)), ((), ()))
_TN = (((0,), (0,)), ((), ()))


def _fox_fwd(name, q, k, v, fq, fk):
    B, H, S, dh = q.shape
    T, TK = _att_tiles(S, FOX_KEY_BLOCK)
    scale = np.float32(dh ** -0.5)

    pieces = _pieces(T, TK)

    def body(q_ref, k_ref, v_ref, fq_ref, fk_ref, o_ref, lse_ref, s_scr, p_scr):
        i = pl.program_id(2)
        qb = q_ref[0, 0]
        n_tiles = _causal_tiles(i, T, TK)

        def tile(j, carry, diag):
            m, l, acc = carry
            ks = pl.ds(pl.multiple_of(j * TK, TK), TK)
            s_scr[...] = lax.dot_general(qb, k_ref[0, 0, ks, :], _NT, preferred_element_type=F32) * scale
            fkj = fk_ref[0, 0, pl.ds(j, 1), :]
            m_new = []
            for rc, (rs, segs) in enumerate(pieces):
                fq_c, mx = fq_ref[0, 0, rs, :], None
                for cs in segs:
                    s = s_scr[rs, cs] + fq_c - fkj[:, cs]
                    if diag:
                        s = jnp.where(_piece_keep(i * T, j * TK, rs, cs, False), s, NEG_INF)
                    s_scr[rs, cs] = s
                    mx = s if mx is None else jnp.maximum(mx, s)
                m_new.append(jnp.maximum(m[rc], jnp.max(mx, axis=1, keepdims=True)))
            alpha, l_new = [], []
            for rc, (rs, segs) in enumerate(pieces):
                alpha.append(jnp.exp(m[rc] - m_new[rc]))
                psum = None
                for cs in segs:
                    p = jnp.exp(s_scr[rs, cs] - m_new[rc])
                    p_scr[rs, cs] = p.astype(BF16)
                    psum = p if psum is None else psum + p
                l_new.append(alpha[rc] * l[rc] + jnp.sum(psum, axis=1, keepdims=True))
            acc = jnp.concatenate(alpha, axis=0) * acc + jnp.dot(p_scr[...], v_ref[0, 0, ks, :], preferred_element_type=F32)
            return tuple(m_new), tuple(l_new), acc

        init = (tuple(jnp.full((rs.stop - rs.start, 1), NEG_INF, F32) for rs, _ in pieces),
                tuple(jnp.zeros((rs.stop - rs.start, 1), F32) for rs, _ in pieces), jnp.zeros((T, dh), F32))
        carry = lax.fori_loop(0, n_tiles - 1, lambda j, c: tile(j, c, False), init)
        m, l, acc = tile(n_tiles - 1, carry, True)
        m, l = jnp.concatenate(m, axis=0), jnp.concatenate(l, axis=0)
        o_ref[0, 0] = acc / l
        lse_ref[0, 0] = m + jnp.log(l)

    full = lambda w: pl.BlockSpec((1, 1, S, w), lambda b, h, i: (b, h, 0, 0))
    blk = lambda w: pl.BlockSpec((1, 1, T, w), lambda b, h, i: (b, h, i, 0))
    return pl.pallas_call(
        body, name=_unique(name), grid=(B, H, S // T),
        in_specs=[blk(dh), full(dh), full(dh), blk(1), pl.BlockSpec((1, 1, S // TK, TK), lambda b, h, i: (b, h, 0, 0))],
        out_specs=[blk(dh), blk(1)],
        out_shape=[jax.ShapeDtypeStruct((B, H, S, dh), F32), jax.ShapeDtypeStruct((B, H, S, 1), F32)],
        scratch_shapes=[pltpu.VMEM((T, TK), F32), pltpu.VMEM((T, TK), BF16)],
        compiler_params=_params(("parallel", "parallel", "parallel")))(q, k, v, fq, fk)


def _fox_bwd(name, q, k, v, fq, fk, do, lse):
    B, H, S, dh = q.shape
    T, TK = _att_tiles(S, FOX_KEY_BLOCK)
    nt, nkt = S // T, S // TK
    scale = np.float32(dh ** -0.5)

    pieces = _pieces(T, TK)

    def body(q_ref, k_ref, v_ref, fq_ref, fk_ref, do_ref, lse_ref, dq_ref, dk_ref, dv_ref, dfk_ref,
             p_buf, dp_buf, s_scr, ds_scr, p16_scr):
        dk_ref[...] = jnp.zeros_like(dk_ref)
        dv_ref[...] = jnp.zeros_like(dv_ref)
        dfk_ref[...] = jnp.zeros_like(dfk_ref)

        def qloop(i, _):
            qs = pl.ds(pl.multiple_of(i * T, T), T)
            qb, dob16 = q_ref[0, 0, qs, :], do_ref[0, 0, qs, :].astype(BF16)
            n_tiles = _causal_tiles(i, T, TK)
            row_at = lambda rs: pl.ds(pl.multiple_of(i * T + rs.start, SUBLANES), rs.stop - rs.start)
            fq_c = [fq_ref[0, 0, row_at(rs), :] for rs, _ in pieces]
            lse_c = [lse_ref[0, 0, row_at(rs), :] for rs, _ in pieces]

            def sweep1(j, delta, diag):
                ks = pl.ds(pl.multiple_of(j * TK, TK), TK)
                s_scr[...] = lax.dot_general(qb, k_ref[0, 0, ks, :], _NT, preferred_element_type=F32) * scale
                dp_buf[j] = lax.dot_general(dob16, v_ref[0, 0, ks, :], _NT, preferred_element_type=F32)
                fkj = fk_ref[0, 0, pl.ds(j, 1), :]
                out = []
                for rc, (rs, segs) in enumerate(pieces):
                    pdp = None
                    for cs in segs:
                        p = jnp.exp(((s_scr[rs, cs] + fq_c[rc]) - fkj[:, cs]) - lse_c[rc])
                        if diag:
                            p = jnp.where(_piece_keep(i * T, j * TK, rs, cs, False), p, 0.0)
                        p_buf[j, rs, cs] = p
                        pdp = p * dp_buf[j, rs, cs] if pdp is None else pdp + p * dp_buf[j, rs, cs]
                    out.append(delta[rc] + jnp.sum(pdp, axis=1, keepdims=True))
                return tuple(out)

            zeros = tuple(jnp.zeros((rs.stop - rs.start, 1), F32) for rs, _ in pieces)
            delta = lax.fori_loop(0, n_tiles - 1, lambda j, d: sweep1(j, d, False), zeros)
            delta = sweep1(n_tiles - 1, delta, True)

            def sweep2(j, dq):
                ks = pl.ds(pl.multiple_of(j * TK, TK), TK)
                col = [None] * len(pieces[0][1])
                for rc, (rs, segs) in enumerate(pieces):
                    for sg, cs in enumerate(segs):
                        p = p_buf[j, rs, cs]
                        ds = p * (dp_buf[j, rs, cs] - delta[rc])
                        ds_scr[rs, cs] = ds.astype(BF16)
                        p16_scr[rs, cs] = p.astype(BF16)
                        col[sg] = ds if col[sg] is None else col[sg] + ds
                dfk_ref[0, 0, pl.ds(j, 1), :] -= jnp.concatenate([_csum(c) for c in col], axis=1)
                ds16 = ds_scr[...]
                dk_ref[0, 0, ks, :] += lax.dot_general(ds16, qb, _TN, preferred_element_type=F32)
                dv_ref[0, 0, ks, :] += lax.dot_general(p16_scr[...], dob16, _TN, preferred_element_type=F32)
                return dq + jnp.dot(ds16, k_ref[0, 0, ks, :], preferred_element_type=F32)

            dq_ref[0, 0, qs, :] = lax.fori_loop(0, n_tiles, sweep2, jnp.zeros((T, dh), F32)) * scale
            return 0

        lax.fori_loop(0, nt, qloop, 0)
        dk_ref[...] = dk_ref[...] * scale

    full = lambda w: pl.BlockSpec((1, 1, S, w), lambda b, h: (b, h, 0, 0))
    fks = pl.BlockSpec((1, 1, nkt, TK), lambda b, h: (b, h, 0, 0))
    return pl.pallas_call(
        body, name=_unique(name), grid=(B, H),
        in_specs=[full(dh), full(dh), full(dh), full(1), fks, full(dh), full(1)],
        out_specs=[full(dh), full(dh), full(dh), fks],
        out_shape=[jax.ShapeDtypeStruct((B, H, S, dh), F32)] * 3 + [jax.ShapeDtypeStruct((B, H, nkt, TK), F32)],
        scratch_shapes=[pltpu.VMEM((nkt, T, TK), F32), pltpu.VMEM((nkt, T, TK), F32), pltpu.VMEM((T, TK), F32),
                        pltpu.VMEM((T, TK), BF16), pltpu.VMEM((T, TK), BF16)],
        compiler_params=_params(("parallel", "parallel")))(q, k, v, fq, fk, do, lse)


def _sb_terms(z, with_sigmoids=True):
    t = jnp.exp(-jnp.abs(z))
    lp = jnp.log(1.0 + t)
    lb, l1 = jnp.minimum(z, 0.0) - lp, jnp.minimum(-z, 0.0) - lp
    if not with_sigmoids:
        return lb, l1, None, None
    return lb, l1, jnp.exp(lb), jnp.exp(l1)


SCAN_RADIX = 4


def _lane_scan(x, reverse):
    lane = _iota2(x.shape, 1)
    y, d = x, 1
    while d < LANES:
        step = y
        for m in range(1, SCAN_RADIX):
            if m * d < LANES:
                if reverse:
                    step = step + jnp.where(lane + m * d < LANES, pltpu.roll(y, LANES - m * d, 1), 0.0)
                else:
                    step = step + jnp.where(lane >= m * d, pltpu.roll(y, m * d, 1), 0.0)
        y, d = step, d * SCAN_RADIX
    return y


def _chunk_scan(xs, reverse):
    n = len(xs)
    within, acc = [None] * n, None
    for s in (range(n - 1, -1, -1) if reverse else range(n)):
        acc = xs[s] if acc is None else acc + xs[s]
        within[s] = acc
    lanes = _lane_scan(acc, reverse)
    beyond = lanes - acc
    return [w + beyond for w in within], (lanes[:, 0:1] if reverse else lanes[:, LANES - 1:LANES])


SB_DEAD = 110.0


def _sb_fwd(name, q, k, v):
    B, H, S, dh = q.shape
    T, TK = _att_tiles(S, SB_KEY_BLOCK)
    scale = np.float32(dh ** -0.5)
    pieces = _pieces(T, TK)

    def body(q_ref, k_ref, v_ref, o_ref, lt_ref, first_ref, z_scr, a_scr):
        i = pl.program_id(2)
        qb = q_ref[0, 0]
        n_tiles = _causal_tiles(i, T, TK)

        def tile(j, carry, diag):
            runs, acc = carry
            ks = pl.ds(pl.multiple_of(j * TK, TK), TK)
            z_scr[...] = lax.dot_general(qb, k_ref[0, 0, ks, :], _NT, preferred_element_type=F32) * scale
            new_runs = []
            for rc, (rs, segs) in enumerate(pieces):
                terms = [_sb_terms(z_scr[rs, cs], False) for cs in segs]
                keep = [_piece_keep(i * T, j * TK, rs, cs, True, len(segs)) if diag else None for cs in segs]
                l1 = [jnp.where(kp, t[1], 0.0) if diag else t[1] for kp, t in zip(keep, terms)]
                right_of, total = _chunk_scan(l1, True)
                for cs, kp, t, x, r in zip(segs, keep, terms, l1, right_of):
                    a = jnp.exp(t[0] + ((r - x) + runs[rc]))
                    a_scr[rs, cs] = (jnp.where(kp, a, 0.0) if diag else a).astype(BF16)
                new_runs.append(runs[rc] + total)
            return tuple(new_runs), acc + jnp.dot(a_scr[...], v_ref[0, 0, ks, :], preferred_element_type=F32)

        init = (tuple(jnp.zeros((rs.stop - rs.start, 1), F32) for rs, _ in pieces), jnp.zeros((T, dh), F32))
        def some_row_alive(runs):
            worst = runs[0]
            for r in runs[1:]:
                worst = jnp.maximum(worst, r)
            return jnp.max(worst) > -SB_DEAD

        def step(c):
            runs, acc = tile(n_tiles - 1 - c[0], (c[1], c[2]), False)
            return c[0] + 1, runs, acc

        visited, runs, acc = lax.while_loop(lambda c: jnp.logical_and(c[0] < n_tiles, some_row_alive(c[1])), step,
                                            (jnp.int32(1), *tile(n_tiles - 1, init, True)))
        o_ref[0, 0] = acc
        lt_ref[0, 0] = jnp.concatenate(runs, axis=0)
        first_ref[pl.program_id(0), pl.program_id(1), i] = (n_tiles - visited).astype(F32)

    full = lambda w: pl.BlockSpec((1, 1, S, w), lambda b, h, i: (b, h, 0, 0))
    blk = lambda w: pl.BlockSpec((1, 1, T, w), lambda b, h, i: (b, h, i, 0))
    return pl.pallas_call(
        body, name=_unique(name), grid=(B, H, S // T),
        in_specs=[blk(dh), full(dh), full(dh)], out_specs=[blk(dh), blk(1), pl.BlockSpec(memory_space=pltpu.SMEM)],
        out_shape=[jax.ShapeDtypeStruct((B, H, S, dh), F32), jax.ShapeDtypeStruct((B, H, S, 1), F32),
                   jax.ShapeDtypeStruct((B, H, S // T), F32)],
        scratch_shapes=[pltpu.VMEM((T, TK), F32), pltpu.VMEM((T, TK), BF16)],
        compiler_params=_params(("arbitrary", "arbitrary", "arbitrary")))(q, k, v)


def _sb_bwd(name, q, k, v, do, lt, first):
    B, H, S, dh = q.shape
    T, TK = _att_tiles(S, SB_KEY_BLOCK)
    nt = S // T
    scale = np.float32(dh ** -0.5)

    pieces = _pieces(T, TK)

    def body(first_ref, q_ref, k_ref, v_ref, do_ref, lt_ref, dq_ref, dk_ref, dv_ref, z_scr, da_scr, dz_scr, a_scr):
        dk_ref[...] = jnp.zeros_like(dk_ref)
        dv_ref[...] = jnp.zeros_like(dv_ref)
        b, h = pl.program_id(0), pl.program_id(1)

        def qloop(i, _):
            qs = pl.ds(pl.multiple_of(i * T, T), T)
            qb, dob16 = q_ref[0, 0, qs, :], do_ref[0, 0, qs, :].astype(BF16)
            n_tiles = _causal_tiles(i, T, TK)
            first_tile = jnp.clip(first_ref[b, h, i].astype(jnp.int32), 0, n_tiles - 1)
            lt_c = [lt_ref[0, 0, pl.ds(pl.multiple_of(i * T + rs.start, SUBLANES), rs.stop - rs.start), :] for rs, _ in pieces]

            def tile(j, carry, diag):
                sums_l, sums_e, dq = carry
                ks = pl.ds(pl.multiple_of(j * TK, TK), TK)
                kb, vb = k_ref[0, 0, ks, :], v_ref[0, 0, ks, :]
                z_scr[...] = lax.dot_general(qb, kb, _NT, preferred_element_type=F32) * scale
                da_scr[...] = lax.dot_general(dob16, vb, _NT, preferred_element_type=F32)
                new_l, new_e = [], []
                for rc, (rs, segs) in enumerate(pieces):
                    terms = [_sb_terms(z_scr[rs, cs]) for cs in segs]
                    keep = [_piece_keep(i * T, j * TK, rs, cs, True, len(segs)) if diag else None for cs in segs]
                    l1 = [jnp.where(kp, t[1], 0.0) if diag else t[1] for kp, t in zip(keep, terms)]
                    upto, total_l = _chunk_scan(l1, False)
                    es = []
                    for cs, kp, t, u in zip(segs, keep, terms, upto):
                        a = jnp.exp(t[0] + (lt_c[rc] - (u + sums_l[rc])))
                        if diag:
                            a = jnp.where(kp, a, 0.0)
                        a_scr[rs, cs] = a.astype(BF16)
                        es.append(da_scr[rs, cs] * a)
                    e_upto, total_e = _chunk_scan(es, False)
                    for cs, kp, t, e, eu in zip(segs, keep, terms, es, e_upto):
                        dz = e * t[3] - ((eu - e) + sums_e[rc]) * t[2]
                        dz_scr[rs, cs] = (jnp.where(kp, dz, 0.0) if diag else dz).astype(BF16)
                    new_l.append(sums_l[rc] + total_l)
                    new_e.append(sums_e[rc] + total_e)
                dz16 = dz_scr[...]
                dk_ref[0, 0, ks, :] += lax.dot_general(dz16, qb, _TN, preferred_element_type=F32)
                dv_ref[0, 0, ks, :] += lax.dot_general(a_scr[...], dob16, _TN, preferred_element_type=F32)
                return tuple(new_l), tuple(new_e), dq + jnp.dot(dz16, kb, preferred_element_type=F32)

            zeros = tuple(jnp.zeros((rs.stop - rs.start, 1), F32) for rs, _ in pieces)
            carry = lax.fori_loop(first_tile, n_tiles - 1, lambda j, c: tile(j, c, False), (zeros, zeros, jnp.zeros((T, dh), F32)))
            dq_ref[0, 0, qs, :] = tile(n_tiles - 1, carry, True)[2] * scale
            return 0

        lax.fori_loop(0, nt, qloop, 0)
        dk_ref[...] = dk_ref[...] * scale

    full = lambda w: pl.BlockSpec((1, 1, S, w), lambda b, h: (b, h, 0, 0))
    return pl.pallas_call(
        body, name=_unique(name), grid=(B, H),
        in_specs=[pl.BlockSpec(memory_space=pltpu.SMEM), full(dh), full(dh), full(dh), full(dh), full(1)], out_specs=[full(dh)] * 3,
        out_shape=[jax.ShapeDtypeStruct((B, H, S, dh), F32)] * 3,
        scratch_shapes=[pltpu.VMEM((T, TK), F32), pltpu.VMEM((T, TK), F32), pltpu.VMEM((T, TK), BF16), pltpu.VMEM((T, TK), BF16)],
        compiler_params=_params(("parallel", "parallel")))(first, q, k, v, do, lt)


def _adamw(name, w, g, m, v):
    shape = w.shape
    n = w.size
    cols = shape[-1] if (w.ndim >= 2 and (shape[-1] % LANES == 0 or n // shape[-1] >= LANES)) else 0
    if cols:
        prep = lambda t: t.reshape(1, n // cols, cols)
    else:
        cols = LANES
        pad = (-n) % (SUBLANES * LANES)
        prep = lambda t: jnp.pad(t.reshape(-1), (0, pad), constant_values=1.0).reshape(1, (n + pad) // cols, cols)

    def fn(r, bv, cv):
        w_, g_, m_, v_ = r
        m2 = ADAM_B1 * m_ + (1.0 - ADAM_B1) * g_
        v2 = ADAM_B2 * v_ + (1.0 - ADAM_B2) * (g_ * g_)
        m_hat = m2 / (1.0 - ADAM_B1 ** ADAM_STEP)
        v_hat = v2 / (1.0 - ADAM_B2 ** ADAM_STEP)
        return [-ADAM_LR * (m_hat / (jnp.sqrt(v_hat) + ADAM_EPS) + ADAM_WD * w_), m2, v2], [], []
    outs = _rowwise(name, fn, [prep(w), prep(g), prep(m), prep(v)], out_rows=[(cols, F32)] * 3, tm=512)
    return [o.reshape(-1)[:n].reshape(shape) for o in outs]


def _sum8(name, parts):
    def fn(r, bv, cv):
        s = r[0]
        for t in r[1:]:
            s = s + t
        return [s], [], []
    rows = [parts[i][None] for i in range(parts.shape[0])]
    return _rowwise(name, fn, rows, out_rows=[(parts.shape[2], F32)])[0][0]


def _pack(arrs, cols, dtype, row_mult):
    flat = jnp.concatenate([a.reshape(-1).astype(dtype) for a in arrs])
    pad = (-flat.size) % (cols * row_mult)
    return jnp.pad(flat, (0, pad)).reshape(-1, cols)


def _unpack(flat, shapes):
    out, off = [], 0
    for s in shapes:
        n = int(np.prod(s))
        out.append(flat[off:off + n].reshape(s))
        off += n
    return out


def _heads(t, H):
    B, S, W = t.shape
    return t.reshape(B, S, H, W // H).transpose(0, 2, 1, 3)


def _unheads(t):
    B, H, S, dh = t.shape
    return t.transpose(0, 2, 1, 3).reshape(B, S, H * dh)


def kernel(*args):
    _names_used.clear()
    p = dict(zip(ARGS, args))
    x, target = p['x'], p['loss_target']
    B, S, D = x.shape
    T = B * S
    depth = p['ln1_g'].shape[0]
    H = D // HEAD_DIM
    alpha = np.float32((2.0 * depth) ** 0.25)
    cx, cy, cc = _mesh_pos()
    my_q = 2 * cx + cy
    axes = ("x", "y", "c")
    two = lambda t: t.reshape(T, t.shape[-1])
    three = lambda t: t.reshape(B, S, t.shape[-1])

    small_in = [p['c']] + [p[n] for n in SMALL_SPLIT]
    g1 = _all_gather8("ag_small", [_pack(small_in, LANES, F32, SUBLANES)])[0]
    g1 = g1.reshape(N_DEV, -1)
    c_all = g1[:, :B * D].reshape(N_DEV * B, D)
    per_chip = [_unpack(g1[2 * q], [a.shape for a in small_in])[1:] for q in range(N_CHIPS)]
    small = {n: jnp.concatenate([per_chip[q][i] for q in range(N_CHIPS)], axis=-1) for i, n in enumerate(SMALL_SPLIT)}
    for n in SMALL_REPL:
        small[n] = p[n]

    n_seq = N_DEV * B
    seq_pad = -(-n_seq // LANES) * LANES
    c_act = _rowwise("c_act", lambda r, bv, cv: ([_silu(r[0])], [], []),
                     [jnp.pad(c_all, ((0, seq_pad - n_seq), (0, 0)))[None]], out_rows=[(D, F32)])[0][0]
    mod_cols = p['mod_w'].shape[2]
    mod_part = jnp.stack([_mm(f"mod_fwd{l}", c_act, p['mod_w'][l])[:n_seq] for l in range(depth)])
    half_layers = depth // 2
    mod_half = lax.dynamic_slice_in_dim(mod_part, cc * half_layers, half_layers, axis=0)
    gm_ = _all_gather8("ag_mod", [mod_half.reshape(half_layers * n_seq, mod_cols)])[0]
    mod_all = gm_.reshape(N_CHIPS, 2, half_layers, n_seq, mod_cols).transpose(1, 2, 3, 0, 4).reshape(depth, n_seq, 6 * D)
    mod_mine = lax.dynamic_slice_in_dim(mod_all, (2 * my_q + cc) * B, B, axis=1)
    mod = _rowwise("mod_bias", lambda r, bv, cv: ([r[0] + bv[0]], [], []), [mod_mine], [p['mod_b'][:, None, :]],
                   out_rows=[(6 * D, F32)])[0]
    mods = [[mod[l, :, None, i * D:(i + 1) * D] for i in range(6)] for l in range(depth)]

    big_names = list(BIG)
    shard_shapes = [p[n].shape for n in big_names]
    half_rows = [s[0] * s[1] // 2 for s in shard_shapes]
    w_halves = [lax.dynamic_slice_in_dim(p[n].reshape(-1, s[2]), cc * hr, hr, axis=0).astype(BF16)
                for n, s, hr in zip(big_names, shard_shapes, half_rows)]
    W = {}
    for n, s, g in zip(big_names, shard_shapes, _all_gather8("ag_weights", w_halves)):
        seg = g.reshape((N_CHIPS,) + s)
        W[n] = jnp.concatenate([seg[q] for q in range(N_CHIPS)], axis=BIG[n])

    def vec(n, j):
        return small[n][j][None, :]

    def attn_proj(h1, w_in, gate_cols):
        wp = jnp.pad(w_in, ((0, 0), (0, gate_cols))) if gate_cols else w_in
        proj = three(_mm("att_proj", two(h1), wp))
        q, k, v = [_heads(proj[..., i * D:(i + 1) * D].astype(BF16), H) for i in range(3)]
        return wp, proj, q, k, v

    def gm_fwd(j, h1):
        zin = three(_mm("gm_in", two(h1), W['gm_w_in'][j]))
        u, vn = _gm_act("gm_act", zin, vec('gm_b_in', j), vec('gm_ln_g', j), vec('gm_ln_b', j))
        b_sT = small['gm_b_s'][j].T
        yg = _gm_spatial("gm_spatial", u, vn, small['gm_w_s'][j], b_sT)
        return three(_mm("gm_out", two(yg), W['gm_w_out'][j])), (zin, u, vn, b_sT, yg)

    def gm_bwd(j, h1, dy1, cache):
        zin, u, vn, b_sT, yg = cache
        g = {'gm_w_out': _mm("gm_dwout", two(yg), two(dy1), ta=True)}
        dyg = three(_mm("gm_dyg", two(dy1), W['gm_w_out'][j], tb=True))
        du, dvn, dws, dbsT = _gm_spatial_bwd("gm_spatial_bwd", dyg, u, vn, small['gm_w_s'][j], b_sT)
        dzin, g['gm_b_in'], g['gm_ln_g'], g['gm_ln_b'] = _gm_act_bwd("gm_act_bwd", zin, du, dvn, vec('gm_b_in', j), vec('gm_ln_g', j))
        g['gm_w_s'], g['gm_b_s'] = dws, dbsT.T
        g['gm_w_in'] = _mm("gm_dwin", two(h1), two(dzin), ta=True)
        return _mm("gm_dh", two(dzin), W['gm_w_in'][j], tb=True), g

    def fox_fwd(j, h1):
        wp, proj, q, k, v = attn_proj(h1, W['fox_w_in'][j], 3 * D + FOX_GATE_COLS - W['fox_w_in'].shape[2])
        fl = proj[..., 3 * D:]
        bf = jnp.pad(small['fox_b_f'][j][None, :], ((0, 0), (0, FOX_GATE_COLS - H)))
        Fh = _fox_gate_cumsum("fox_gate", fl, bf)[..., :H].transpose(0, 2, 1)
        fq, fk = Fh[..., None], Fh.reshape(B, H, -1, _att_tiles(S, FOX_KEY_BLOCK)[1])
        o, lse = _fox_fwd("fox_fwd", q, k, v, fq, fk)
        o2 = _unheads(o)
        return three(_mm("fox_out", two(o2), W['fox_w_out'][j])), (wp, q, k, v, fl, bf, fq, fk, lse, o2)

    def fox_bwd(j, h1, dy1, cache):
        wp, q, k, v, fl, bf, fq, fk, lse, o2 = cache
        g = {'fox_w_out': _mm("fox_dwout", two(o2), two(dy1), ta=True)}
        do = _heads(three(_mm("fox_do", two(dy1), W['fox_w_out'][j], tb=True)), H)
        dq, dk, dv, dfk = _fox_bwd("fox_bwd", q, k, v, fq, fk, do, lse)
        dF = jnp.pad(dfk.reshape(B, H, S).transpose(0, 2, 1), ((0, 0), (0, 0), (0, FOX_GATE_COLS - H)))
        dfl, dbf = _fox_gate_bwd("fox_gate_bwd", dF, fl, bf, H)
        dproj = jnp.concatenate([_unheads(dq).astype(BF16), _unheads(dk).astype(BF16), _unheads(dv).astype(BF16),
                                 dfl.astype(BF16)], axis=-1)
        g['fox_w_in'] = _mm("fox_dwin", two(h1), two(dproj), ta=True)[:, :W['fox_w_in'].shape[2]]
        g['fox_b_f'] = dbf[0, :H]
        return _mm("fox_dh", two(dproj), wp, tb=True), g

    def sb_fwd(j, h1):
        wp, proj, q, k, v = attn_proj(h1, W['sb_w_in'][j], 0)
        k, v = _lane_major(k, _att_tiles(S, SB_KEY_BLOCK)[1]), _lane_major(v, _att_tiles(S, SB_KEY_BLOCK)[1])
        o, lt, first = _sb_fwd("sb_fwd", q, k, v)
        o2 = _unheads(o)
        return three(_mm("sb_out", two(o2), W['sb_w_out'][j])), (q, k, v, lt, first, o2)

    def sb_bwd(j, h1, dy1, cache):
        q, k, v, lt, first, o2 = cache
        g = {'sb_w_out': _mm("sb_dwout", two(o2), two(dy1), ta=True)}
        do = _heads(three(_mm("sb_do", two(dy1), W['sb_w_out'][j], tb=True)), H)
        dq, dk, dv = _sb_bwd("sb_bwd", q, k, v, do, lt, first)
        dk, dv = _lane_major_inverse(dk, _att_tiles(S, SB_KEY_BLOCK)[1]), _lane_major_inverse(dv, _att_tiles(S, SB_KEY_BLOCK)[1])
        dproj = jnp.concatenate([_unheads(dq).astype(BF16), _unheads(dk).astype(BF16), _unheads(dv).astype(BF16)], axis=-1)
        g['sb_w_in'] = _mm("sb_dwin", two(h1), two(dproj), ta=True)
        return _mm("sb_dh", two(dproj), W['sb_w_in'][j], tb=True), g

    def cv_fwd(j, h1):
        pw = three(_mm("cv_in", two(h1), W['cv_w_in'][j]))
        ygl = _cv_glu("cv_glu", pw, vec('cv_b_in', j))
        dw = jnp.pad(small['cv_dw'][j], ((0, CONV_HALO - CONV_WIDTH), (0, 0)))
        yc = _dwconv("cv_dwconv", ygl, dw, vec('cv_dw_b', j))
        ys = _cv_ln_act("cv_ln_act", yc, vec('cv_ln_g', j), vec('cv_ln_b', j))
        return three(_mm("cv_out", two(ys), W['cv_w_out'][j])), (pw, ygl, dw, yc, ys)

    def cv_bwd(j, h1, dy1, cache):
        pw, ygl, dw, yc, ys = cache
        g = {'cv_w_out': _mm("cv_dwout", two(ys), two(dy1), ta=True)}
        dys = three(_mm("cv_dys", two(dy1), W['cv_w_out'][j], tb=True))
        dyc, g['cv_ln_g'], g['cv_ln_b'], g['cv_dw_b'] = _cv_ln_act_bwd("cv_ln_act_bwd", yc, dys, vec('cv_ln_g', j), vec('cv_ln_b', j))
        dygl, ddw = _dwconv_bwd("cv_dwconv_bwd", dyc, ygl, dw)
        g['cv_dw'] = ddw[:CONV_WIDTH]
        dpw, g['cv_b_in'] = _cv_glu_bwd("cv_glu_bwd", pw, dygl, vec('cv_b_in', j))
        g['cv_w_in'] = _mm("cv_dwin", two(h1), two(dpw), ta=True)
        return _mm("cv_dh", two(dpw), W['cv_w_in'][j], tb=True), g

    mixers = [(gm_fwd, gm_bwd), (fox_fwd, fox_bwd), (sb_fwd, sb_bwd), (cv_fwd, cv_bwd)]
    n_mix = len(mixers)

    saved = []
    h1 = _modulate("mod1", x, mods[0][1], mods[0][0])
    for l in range(depth):
        m, j = l % n_mix, l // n_mix
        sh1, sc1, g1_, sh2, sc2, g2_ = mods[l]
        ybias = vec('cv_b_out', j) if m == 3 else None
        y1, cache = mixers[m][0](j, h1)
        xm, h2 = _resid_ln("resid_ln1", alpha, x, y1, g1_, small['ln1_g'][l][None], small['ln1_b'][l][None], ybias, then=(sc2, sh2))
        z = three(_mm("ffn_in", two(h2), W['ffn_w_in'][l]))
        a = _swiglu_act("ffn_act", z)
        y2 = three(_mm("ffn_out", two(a), W['ffn_w_out'][l]))
        nxt = (mods[l + 1][1], mods[l + 1][0]) if l + 1 < depth else None
        xo, *h_next = _resid_ln("resid_ln2", alpha, xm, y2, g2_, small['ln2_g'][l][None], small['ln2_b'][l][None], then=nxt)
        saved.append((x, h1, y1, cache, xm, h2, z, a, y2, ybias))
        x, h1 = xo, (h_next[0] if h_next else None)

    dx, sq = _loss_head("loss_head", x, target)
    loss = lax.psum(jnp.sum(sq) * np.float32(0.5 / D), axes)

    grads = {n: [None] * p[n].shape[0] for n in WEIGHTS}
    parts = [dict() for _ in range(depth)]
    after = None
    for l in reversed(range(depth)):
        m, j = l % n_mix, l // n_mix
        sh1, sc1, g1_, sh2, sc2, g2_ = mods[l]
        x_in, h1, y1, cache, xm, h2, z, a, y2, ybias = saved[l]
        ln2 = (small['ln2_g'][l][None], small['ln2_b'][l][None])
        if after is None:
            dr2, dy2, parts[l]['g2'], grads['ln2_g'][l], grads['ln2_b'][l], _ = _resid_ln_bwd("resid_ln2_bwd", alpha, dx, xm, y2, g2_, *ln2)
        else:
            (dr2, dy2, parts[l]['g2'], parts[l + 1]['sc1'], parts[l + 1]['sh1'], grads['ln2_g'][l], grads['ln2_b'][l], _) = _resid_ln_bwd(
                "resid_ln2_bwd", alpha, after[0], xm, y2, g2_, *ln2, then=after[1:])
        grads['ffn_w_out'][l] = _mm("ffn_dwout", two(a), two(dy2), ta=True)
        da = three(_mm("ffn_da", two(dy2), W['ffn_w_out'][l], tb=True))
        dz = _swiglu_act_bwd("ffn_act_bwd", da, z)
        grads['ffn_w_in'][l] = _mm("ffn_dwin", two(h2), two(dz), ta=True)
        dh2 = three(_mm("ffn_dh", two(dz), W['ffn_w_in'][l], tb=True))
        (dr1, dy1, parts[l]['g1'], parts[l]['sc2'], parts[l]['sh2'], grads['ln1_g'][l], grads['ln1_b'][l], dyb) = _resid_ln_bwd(
            "resid_ln1_bwd", alpha, dr2, x_in, y1, g1_, small['ln1_g'][l][None], small['ln1_b'][l][None], ybias, then=(dh2, sc2))
        dh1, mg = mixers[m][1](j, h1, dy1, cache)
        if m == 3:
            mg['cv_b_out'] = dyb
        for n, gval in mg.items():
            grads[n][j] = gval
        after = (dr1, three(dh1), sc1)
    grad_x, parts[0]['sc1'], parts[0]['sh1'] = _modulate_bwd("mod1_bwd", alpha, after[1], after[0], saved[0][0], after[2])
    dmod = [jnp.concatenate([pt[k] for k in ('sh1', 'sc1', 'g1', 'sh2', 'sc2', 'g2')], axis=-1)[:, 0, :] for pt in parts]
    dmod = jnp.stack(dmod)
    grads['mod_b'] = [jnp.sum(dmod[l], axis=0) for l in range(depth)]
    full_shape = {n: tuple(t.shape) for n, t in small.items()}

    small_names = SMALL_REPL + SMALL_SPLIT
    small_parts = [jnp.stack([gv.reshape(full_shape[n][1:]) for gv in grads[n]]) for n in small_names]
    pack_a = _pack([dmod], LANES, F32, SUBLANES)
    pack_b = _pack(small_parts, LANES, F32, SUBLANES)
    g2 = _all_gather8("ag_grads_small", [jnp.concatenate([pack_a, pack_b], axis=0)])[0]
    rows_a = pack_a.shape[0]
    dmod_all = g2[:, :rows_a].reshape(N_DEV, -1)[:, :dmod.size].reshape(N_DEV, depth, B, 6 * D)
    dmod_all = dmod_all.transpose(1, 0, 2, 3).reshape(depth, n_seq, 6 * D)
    small_sum = _sum8("sum_grads_small", g2[:, rows_a:]).reshape(-1)
    g_small = dict(zip(small_names, _unpack(small_sum, [full_shape[n] for n in small_names])))
    for n in SMALL_SPLIT:
        w = p[n].shape[-1]
        g_small[n] = lax.dynamic_slice_in_dim(g_small[n], my_q * w, w, axis=g_small[n].ndim - 1)

    dm_cols = lax.dynamic_slice_in_dim(dmod_all, my_q * mod_cols, mod_cols, axis=2)
    dm_cols = jnp.pad(dm_cols, ((0, 0), (0, seq_pad - n_seq), (0, 0)))
    g_mod_w = jnp.stack([_mm(f"mod_dw{l}", c_act, dm_cols[l], ta=True) for l in range(depth)])

    keep, give = [], []
    for n, s, hr in zip(big_names, shard_shapes, half_rows):
        gfull = jnp.stack(grads[n])
        g4 = jnp.stack(jnp.split(gfull, N_CHIPS, axis=BIG[n])).reshape(N_CHIPS, 2 * hr, s[2])
        keep.append(lax.dynamic_slice_in_dim(g4, cc * hr, hr, axis=1))
        give.append(lax.dynamic_slice_in_dim(g4, (1 - cc) * hr, hr, axis=1))
    got = _sibling_exchange("rs_sibling", give)
    chip_sum = [_rowwise("rs_add_sibling", lambda r, bv, cv: ([r[0] + r[1]], [], []),
                         [a.reshape(1, -1, a.shape[2]), b.reshape(1, -1, a.shape[2])],
                         out_rows=[(a.shape[2], BF16)], tm=512)[0].reshape(a.shape) for a, b in zip(keep, got)]
    from_chips = _chip_all_to_all("rs_chips", chip_sum)
    half_sum = [_rowwise("rs_add_chips", lambda r, bv, cv: ([((r[0] + r[1]) + r[2]) + r[3]], [], []),
                         [t[q][None] for q in range(N_CHIPS)], out_rows=[(t.shape[2], F32)], tm=512)[0][0]
                for t in from_chips]
    other = _sibling_exchange("rs_share", half_sum)
    g_big = {n: jnp.concatenate([jnp.where(cc == 0, a, b), jnp.where(cc == 0, b, a)], axis=0).reshape(s)
             for n, s, a, b in zip(big_names, shard_shapes, half_sum, other)}

    g_out = {**g_small, **g_big, 'mod_w': g_mod_w}
    upd = {n: _adamw("adamw_" + n, p[n], g_out[n], p['m_' + n], p['v_' + n]) for n in WEIGHTS}
    return (loss, grad_x, *[g_out[n] for n in WEIGHTS], *[upd[n][0] for n in WEIGHTS],
            *[upd[n][1] for n in WEIGHTS], *[upd[n][2] for n in WEIGHTS])
```

```python
import math

import jax
import jax.numpy as jnp
import numpy as np
from jax import lax
from jax.experimental import pallas as pl
from jax.experimental.pallas import tpu as pltpu

F32, BF16 = jnp.float32, jnp.bfloat16

HEAD_DIM = 64
CONV_WIDTH = 31
LN_EPS = 1e-5
NEG_INF = -1e30
ADAM_LR, ADAM_B1, ADAM_B2, ADAM_EPS, ADAM_WD, ADAM_STEP = 0.001, 0.9, 0.999, 1e-08, 0.01, 10

LANES = 128
SUBLANES = 8
VMEM_LIMIT_BYTES = 56 * 1024 * 1024
N_CHIPS = 4
N_DEV = 8

WEIGHTS = ['mod_w', 'mod_b', 'ln1_g', 'ln1_b', 'ln2_g', 'ln2_b', 'ffn_w_in', 'ffn_w_out', 'gm_w_in', 'gm_b_in',
           'gm_ln_g', 'gm_ln_b', 'gm_w_s', 'gm_b_s', 'gm_w_out', 'fox_w_in', 'fox_b_f', 'fox_w_out', 'sb_w_in',
           'sb_w_out', 'cv_w_in', 'cv_b_in', 'cv_dw', 'cv_dw_b', 'cv_ln_g', 'cv_ln_b', 'cv_w_out', 'cv_b_out']
ARGS = ['x', 'c'] + WEIGHTS + ['loss_target'] + ['m_' + n for n in WEIGHTS] + ['v_' + n for n in WEIGHTS]
BIG = {'ffn_w_in': 2, 'ffn_w_out': 1, 'gm_w_in': 2, 'gm_w_out': 1, 'fox_w_in': 2, 'fox_w_out': 1,
       'sb_w_in': 2, 'sb_w_out': 1, 'cv_w_in': 2, 'cv_w_out': 1}
SMALL_SPLIT = ['cv_b_in', 'cv_dw', 'cv_dw_b', 'cv_ln_g', 'cv_ln_b', 'cv_b_out']
SMALL_REPL = ['mod_b', 'ln1_g', 'ln1_b', 'ln2_g', 'ln2_b', 'gm_b_in', 'gm_ln_g', 'gm_ln_b', 'gm_w_s', 'gm_b_s', 'fox_b_f']
PACK_COLS = 1024


_names_used = {}


def _unique(name):
    k = _names_used.get(name, 0)
    _names_used[name] = k + 1
    return name if k == 0 else f"{name}_{k}"


def _params(sem):
    return pltpu.CompilerParams(dimension_semantics=sem, vmem_limit_bytes=VMEM_LIMIT_BYTES)


def _pick(dim, pref, mult=LANES):
    if dim <= pref:
        return dim
    best = 0
    for t in range(mult, pref + 1, mult):
        if dim % t == 0:
            best = t
    assert best, (dim, pref)
    return best


def _mesh_pos():
    return lax.axis_index("x"), lax.axis_index("y"), lax.axis_index("c")


AG_COPIES = 7
A2A_COPIES = 3


def _comm_call(name, body, blks, out_shapes, n_sems):
    n = len(blks)
    hbm = pl.BlockSpec(memory_space=pl.ANY)
    return pl.pallas_call(
        body, name=_unique(name), out_shape=out_shapes, in_specs=[hbm] * n, out_specs=[hbm] * n,
        scratch_shapes=[pltpu.SemaphoreType.DMA((n_sems * n,)), pltpu.SemaphoreType.DMA((n_sems * n,)),
                        pltpu.SemaphoreType.DMA((n,))],
    )(*blks)


def _all_gather8(name, blks):
    n = len(blks)

    def body(*refs):
        x_refs, out_refs, (send_sems, recv_sems, local_sems) = refs[:n], refs[n:2 * n], refs[2 * n:]
        x, y, c = _mesh_pos()
        me, sibling = (x, y, c), (x, y, 1 - c)
        chips = [(1 - x, y), (x, 1 - y), (1 - x, 1 - y)]

        def copy(a, k, block, to, from_input=False):
            px, py, pc = block
            slot = out_refs[a].at[4 * px + 2 * py + pc]
            return pltpu.make_async_remote_copy(
                src_ref=x_refs[a] if from_input else slot, dst_ref=slot,
                send_sem=send_sems.at[AG_COPIES * a + k], recv_sem=recv_sems.at[AG_COPIES * a + k],
                device_id=to, device_id_type=pl.DeviceIdType.MESH)

        local, sent = [], []
        for a in range(n):
            local.append(pltpu.make_async_copy(x_refs[a], out_refs[a].at[4 * x + 2 * y + c], local_sems.at[a]))
            local[-1].start()
            first = [copy(a, 0, me, sibling, True)] + [copy(a, 1 + j, me, (*chip, c), True) for j, chip in enumerate(chips)]
            for cp in first:
                cp.start()
            sent += first
        for a in range(n):
            for j, chip in enumerate(chips):
                copy(a, 1 + j, (*chip, c), me).wait_recv()
                sent.append(copy(a, 4 + j, (*chip, c), sibling))
                sent[-1].start()
        for a in range(n):
            copy(a, 0, sibling, me).wait_recv()
            for j, chip in enumerate(chips):
                copy(a, 4 + j, (*chip, 1 - c), me).wait_recv()
        for cp in sent:
            cp.wait_send()
        for cp in local:
            cp.wait()

    return _comm_call(name, body, blks, [jax.ShapeDtypeStruct((N_DEV,) + b.shape, b.dtype) for b in blks], AG_COPIES)


def _sibling_exchange(name, blks):
    n = len(blks)

    def body(*refs):
        x_refs, out_refs, (send_sems, recv_sems, _) = refs[:n], refs[n:2 * n], refs[2 * n:]
        x, y, c = _mesh_pos()
        cps = [pltpu.make_async_remote_copy(src_ref=x_refs[a], dst_ref=out_refs[a], send_sem=send_sems.at[a],
                                            recv_sem=recv_sems.at[a], device_id=(x, y, 1 - c),
                                            device_id_type=pl.DeviceIdType.MESH) for a in range(n)]
        for cp in cps:
            cp.start()
        for cp in cps:
            cp.wait()

    return _comm_call(name, body, blks, [jax.ShapeDtypeStruct(b.shape, b.dtype) for b in blks], 1)


def _chip_all_to_all(name, blks):
    n = len(blks)

    def body(*refs):
        x_refs, out_refs, (send_sems, recv_sems, local_sems) = refs[:n], refs[n:2 * n], refs[2 * n:]
        x, y, c = _mesh_pos()
        chips = [(1 - x, y), (x, 1 - y), (1 - x, 1 - y)]
        my_q = 2 * x + y

        def copy(a, j, src_q, dst_q):
            px, py = chips[j]
            return pltpu.make_async_remote_copy(
                src_ref=x_refs[a].at[src_q], dst_ref=out_refs[a].at[dst_q],
                send_sem=send_sems.at[A2A_COPIES * a + j], recv_sem=recv_sems.at[A2A_COPIES * a + j],
                device_id=(px, py, c), device_id_type=pl.DeviceIdType.MESH)

        local, sent = [], []
        for a in range(n):
            local.append(pltpu.make_async_copy(x_refs[a].at[my_q], out_refs[a].at[my_q], local_sems.at[a]))
            local[-1].start()
            sent += [copy(a, j, 2 * px + py, my_q) for j, (px, py) in enumerate(chips)]
            for cp in sent[-A2A_COPIES:]:
                cp.start()
        for a in range(n):
            for j, (px, py) in enumerate(chips):
                copy(a, j, my_q, 2 * px + py).wait_recv()
        for cp in sent:
            cp.wait_send()
        for cp in local:
            cp.wait()

    return _comm_call(name, body, blks, [jax.ShapeDtypeStruct(b.shape, b.dtype) for b in blks], A2A_COPIES)


def _rowwise(name, fn, rows, bvecs=(), cvecs=(), out_rows=(), out_bsums=(), out_tsums=(), tm=256):
    B, S = rows[0].shape[:2]
    tm = _pick(S, tm, SUBLANES)
    n_r, n_b, n_c = len(rows), len(bvecs), len(cvecs)
    n_or, n_ob = len(out_rows), len(out_bsums)

    def body(*refs):
        ins, outs = refs[:n_r + n_b + n_c], refs[n_r + n_b + n_c:]
        r = [ref[0] for ref in ins[:n_r]]
        bv = [ref[0] for ref in ins[n_r:n_r + n_b]]
        cv = [ref[...] for ref in ins[n_r + n_b:]]
        o_rows, o_bsums, o_tsums = fn(r, bv, cv)
        b, i = pl.program_id(0), pl.program_id(1)
        for ref, val in zip(outs[:n_or], o_rows):
            ref[0] = val.astype(ref.dtype)
        for ref, val in zip(outs[n_or:n_or + n_ob], o_bsums):
            @pl.when(i == 0)
            def _(ref=ref, val=val):
                ref[0] = val

            @pl.when(i > 0)
            def _(ref=ref, val=val):
                ref[0] += val
        for ref, val in zip(outs[n_or + n_ob:], o_tsums):
            first = jnp.logical_and(b == 0, i == 0)

            @pl.when(first)
            def _(ref=ref, val=val):
                ref[...] = val

            @pl.when(jnp.logical_not(first))
            def _(ref=ref, val=val):
                ref[...] += val

    in_specs = [pl.BlockSpec((1, tm, a.shape[2]), lambda b, i: (b, i, 0)) for a in rows]
    in_specs += [pl.BlockSpec((1, 1, a.shape[2]), lambda b, i: (b, 0, 0)) for a in bvecs]
    in_specs += [pl.BlockSpec((1, a.shape[1]), lambda b, i: (0, 0)) for a in cvecs]
    out_shape = [jax.ShapeDtypeStruct((B, S, cdim), dt) for cdim, dt in out_rows]
    out_specs = [pl.BlockSpec((1, tm, cdim), lambda b, i: (b, i, 0)) for cdim, _ in out_rows]
    out_shape += [jax.ShapeDtypeStruct((B, 1, cdim), F32) for cdim in out_bsums]
    out_specs += [pl.BlockSpec((1, 1, cdim), lambda b, i: (b, 0, 0)) for cdim in out_bsums]
    out_shape += [jax.ShapeDtypeStruct((1, cdim), F32) for cdim in out_tsums]
    out_specs += [pl.BlockSpec((1, cdim), lambda b, i: (0, 0)) for cdim in out_tsums]
    sem = ("arbitrary", "arbitrary") if out_tsums else ("parallel", "arbitrary")
    res = pl.pallas_call(body, name=_unique(name), grid=(B, S // tm), in_specs=in_specs, out_specs=out_specs,
                         out_shape=out_shape, compiler_params=_params(sem))(*rows, *bvecs, *cvecs)
    return list(res)


MM_TILE = 1536
MM_ROWS = 512
MM_WEIGHT_TILE_BYTES = 12 * 1024 * 1024


def _mm(name, a, b, ta=False, tb=False, out_dtype=F32):
    M, K = (a.shape[1], a.shape[0]) if ta else a.shape
    N = b.shape[0] if tb else b.shape[1]
    assert (b.shape[1] if tb else b.shape[0]) == K, (a.shape, b.shape, ta, tb)
    tn = _pick(N, MM_TILE)
    if ta:
        tm, tk = _pick(M, MM_TILE), _pick(K, 2 * MM_ROWS, LANES if tb else SUBLANES)
    else:
        tm = _pick(M, MM_ROWS, SUBLANES)
        tk = K if K * tn * 2 <= MM_WEIGHT_TILE_BYTES else _pick(K, MM_TILE)
    nk = K // tk
    dims = (((0 if ta else 1,), (1 if tb else 0,)), ((), ()))

    def body(a_ref, b_ref, o_ref, acc_ref):
        k = pl.program_id(2)
        p = lax.dot_general(a_ref[...].astype(BF16), b_ref[...].astype(BF16), dims, preferred_element_type=F32)
        if nk == 1:
            o_ref[...] = p.astype(o_ref.dtype)
        else:
            @pl.when(k == 0)
            def _():
                acc_ref[...] = p

            @pl.when(k > 0)
            def _():
                acc_ref[...] += p

            @pl.when(k == nk - 1)
            def _():
                o_ref[...] = acc_ref[...].astype(o_ref.dtype)

    a_spec = pl.BlockSpec((tk, tm), lambda j, i, k: (k, i)) if ta else pl.BlockSpec((tm, tk), lambda j, i, k: (i, k))
    b_spec = pl.BlockSpec((tn, tk), lambda j, i, k: (j, k)) if tb else pl.BlockSpec((tk, tn), lambda j, i, k: (k, j))
    return pl.pallas_call(
        body, name=_unique(name), grid=(N // tn, M // tm, nk), in_specs=[a_spec, b_spec],
        out_specs=pl.BlockSpec((tm, tn), lambda j, i, k: (i, j)),
        out_shape=jax.ShapeDtypeStruct((M, N), out_dtype),
        scratch_shapes=[pltpu.VMEM((tm, tn) if nk > 1 else (SUBLANES, LANES), F32)],
        compiler_params=_params(("parallel", "parallel", "arbitrary")))(a, b)


def _silu(x):
    return x * _sigmoid(x)


def _sigmoid(x):
    return 1.0 / (1.0 + jnp.exp(-x))


def _dsilu(x):
    s = _sigmoid(x)
    return s * (1.0 + x * (1.0 - s))


def _gelu(x):
    return 0.5 * x * (1.0 + lax.erf(x * np.float32(math.sqrt(0.5))))


def _dgelu(x):
    cdf = 0.5 * (1.0 + lax.erf(x * np.float32(math.sqrt(0.5))))
    pdf = jnp.exp(-0.5 * x * x) * np.float32(1.0 / math.sqrt(2.0 * math.pi))
    return cdf + x * pdf


def _ln_stats(r):
    mu = jnp.mean(r, axis=-1, keepdims=True)
    xc = r - mu
    var = jnp.mean(xc * xc, axis=-1, keepdims=True)
    rstd = lax.rsqrt(var + LN_EPS)
    return xc * rstd, rstd


def _ln_bwd(dxhat, xhat, rstd):
    m1 = jnp.mean(dxhat, axis=-1, keepdims=True)
    m2 = jnp.mean(dxhat * xhat, axis=-1, keepdims=True)
    return rstd * (dxhat - m1 - xhat * m2)


def _csum(v):
    return jnp.sum(v, axis=0, keepdims=True)


def _split_dot(x, m01, lhs01=False, terms=2):
    acc, rem = None, x
    for _ in range(terms):
        part = rem.astype(BF16)
        rem = rem - part.astype(F32)
        d = jnp.dot(m01, part, preferred_element_type=F32) if lhs01 else jnp.dot(part, m01, preferred_element_type=F32)
        acc = d if acc is None else acc + d
    return acc


def _iota2(shape, dim):
    return lax.broadcasted_iota(jnp.int32, shape, dim)


def _modulate(name, x, sc, sh):
    D = x.shape[2]
    return _rowwise(name, lambda r, bv, cv: ([r[0] * (1.0 + bv[0]) + bv[1]], [], []),
                    [x], [sc, sh], [], out_rows=[(D, BF16)])[0]


def _resid_ln(name, alpha, x, y, g, ln_g, ln_b, ybias=None, then=None):
    D = x.shape[2]

    def fn(r, bv, cv):
        yy = r[1] if ybias is None else r[1] + cv[2]
        xhat, _ = _ln_stats(alpha * r[0] + (1.0 + bv[0]) * yy)
        xn = xhat * cv[0] + cv[1]
        return [xn] + ([xn * (1.0 + bv[1]) + bv[2]] if then else []), [], []
    cvecs = [ln_g, ln_b] + ([] if ybias is None else [ybias])
    return _rowwise(name, fn, [x, y], [g] + list(then or ()), cvecs, out_rows=[(D, F32)] + ([(D, BF16)] if then else []))


def _resid_ln_bwd(name, alpha, dxn, x, y, g, ln_g, ln_b, ybias=None, then=None):
    D = x.shape[2]

    def fn(r, bv, cv):
        yy = r[2] if ybias is None else r[2] + cv[2]
        xhat, rstd = _ln_stats(alpha * r[1] + (1.0 + bv[0]) * yy)
        d, sums = r[0], []
        if then:
            d = alpha * d + r[3] * (1.0 + bv[1])
            sums = [_csum(r[3] * (xhat * cv[0] + cv[1])), _csum(r[3])]
        dr = _ln_bwd(d * cv[0], xhat, rstd)
        dy = (1.0 + bv[0]) * dr
        return [dr, dy], [_csum(dr * yy)] + sums, [_csum(d * xhat), _csum(d), _csum(dy)]
    cvecs = [ln_g, ln_b] + ([] if ybias is None else [ybias])
    rows, bvecs = [dxn, x, y] + ([then[0]] if then else []), [g] + ([then[1]] if then else [])
    return _rowwise(name, fn, rows, bvecs, cvecs, out_rows=[(D, F32), (D, BF16)], out_bsums=[D] * (3 if then else 1),
                    out_tsums=[D, D, D])


def _modulate_bwd(name, alpha, dh, dr, x, sc):
    D = x.shape[2]

    def fn(r, bv, cv):
        return [alpha * r[1] + r[0] * (1.0 + bv[0])], [_csum(r[0] * r[2]), _csum(r[0])], []
    return _rowwise(name, fn, [dh, dr, x], [sc], [], out_rows=[(D, F32)], out_bsums=[D, D])


def _loss_head(name, y, target):
    D = y.shape[2]

    def fn(r, bv, cv):
        e = r[0] - r[1]
        return [e * np.float32(1.0 / D)], [_csum(e * e)], []
    return _rowwise(name, fn, [y, target], [], [], out_rows=[(D, F32)], out_bsums=[D])


def _ffn_in_act(name, h, w_in):
    M, K = h.shape
    Hd = w_in.shape[1] // 2
    tm, tn = _pick(M, MM_ROWS, SUBLANES), _pick(Hd, MM_TILE)
    nj = Hd // tn

    def body(h_ref, wg_ref, wu_ref, zg_ref, zu_ref, act_ref):
        hv = h_ref[...].astype(BF16)
        zg = jnp.dot(hv, wg_ref[...].astype(BF16), preferred_element_type=F32)
        zu = jnp.dot(hv, wu_ref[...].astype(BF16), preferred_element_type=F32)
        zg_ref[...] = zg
        zu_ref[...] = zu
        act_ref[...] = (_silu(zg) * zu).astype(act_ref.dtype)

    tile = pl.BlockSpec((tm, tn), lambda j, i: (i, j))
    return pl.pallas_call(
        body, name=_unique(name), grid=(nj, M // tm),
        in_specs=[pl.BlockSpec((tm, K), lambda j, i: (i, 0)), pl.BlockSpec((K, tn), lambda j, i: (0, j)),
                  pl.BlockSpec((K, tn), lambda j, i: (0, j + nj))],
        out_specs=[tile, tile, tile],
        out_shape=[jax.ShapeDtypeStruct((M, Hd), F32), jax.ShapeDtypeStruct((M, Hd), F32), jax.ShapeDtypeStruct((M, Hd), BF16)],
        compiler_params=_params(("parallel", "parallel")))(h, w_in, w_in)


def _ffn_out_bwd_act(name, dy, w_out, zg, zu):
    M, K = dy.shape
    Hd = w_out.shape[0]
    tm, tn = _pick(M, MM_ROWS, SUBLANES), _pick(Hd, MM_TILE)

    def body(dy_ref, w_ref, zg_ref, zu_ref, dzg_ref, dzu_ref):
        da = lax.dot_general(dy_ref[...].astype(BF16), w_ref[...].astype(BF16), _NT, preferred_element_type=F32)
        gg = zg_ref[...]
        dzg_ref[...] = (da * zu_ref[...] * _dsilu(gg)).astype(dzg_ref.dtype)
        dzu_ref[...] = (da * _silu(gg)).astype(dzu_ref.dtype)

    tile = pl.BlockSpec((tm, tn), lambda j, i: (i, j))
    return pl.pallas_call(
        body, name=_unique(name), grid=(Hd // tn, M // tm),
        in_specs=[pl.BlockSpec((tm, K), lambda j, i: (i, 0)), pl.BlockSpec((tn, K), lambda j, i: (j, 0)), tile, tile],
        out_specs=[tile, tile],
        out_shape=[jax.ShapeDtypeStruct((M, Hd), BF16), jax.ShapeDtypeStruct((M, Hd), BF16)],
        compiler_params=_params(("parallel", "parallel")))(dy, w_out, zg, zu)


def _ffn_in_bwd(name, dzg, dzu, w_in):
    M, Hd = dzg.shape
    N = w_in.shape[0]
    tm, tn = _pick(M, MM_ROWS, SUBLANES), _pick(N, MM_TILE)

    def body(g_ref, u_ref, wg_ref, wu_ref, o_ref):
        o_ref[...] = (lax.dot_general(g_ref[...], wg_ref[...].astype(BF16), _NT, preferred_element_type=F32)
                      + lax.dot_general(u_ref[...], wu_ref[...].astype(BF16), _NT, preferred_element_type=F32))

    rows = pl.BlockSpec((tm, Hd), lambda j, i: (i, 0))
    return pl.pallas_call(
        body, name=_unique(name), grid=(N // tn, M // tm),
        in_specs=[rows, rows, pl.BlockSpec((tn, Hd), lambda j, i: (j, 0)), pl.BlockSpec((tn, Hd), lambda j, i: (j, 1))],
        out_specs=pl.BlockSpec((tm, tn), lambda j, i: (i, j)),
        out_shape=jax.ShapeDtypeStruct((M, N), F32),
        compiler_params=_params(("parallel", "parallel")))(dzg, dzu, w_in, w_in)


def _gm_act(name, zin, b_in, ln_g, ln_b):
    W = zin.shape[2] // 2

    def fn(r, bv, cv):
        z = _gelu(r[0] + cv[0])
        vhat, _ = _ln_stats(z[:, W:])
        return [z[:, :W], vhat * cv[1] + cv[2]], [], []
    return _rowwise(name, fn, [zin], [], [b_in, ln_g, ln_b], out_rows=[(W, F32), (W, BF16)])


def _gm_act_bwd(name, zin, du, dvn, b_in, ln_g):
    W = zin.shape[2] // 2

    def fn(r, bv, cv):
        zz = r[0] + cv[0]
        z = _gelu(zz)
        vhat, rstd = _ln_stats(z[:, W:])
        dv = _ln_bwd(r[2] * cv[1], vhat, rstd)
        dzin = jnp.concatenate([r[1], dv], axis=1) * _dgelu(zz)
        return [dzin], [], [_csum(dzin), _csum(r[2] * vhat), _csum(r[2])]
    return _rowwise(name, fn, [zin, du, dvn], [], [b_in, ln_g], out_rows=[(2 * W, BF16)], out_tsums=[2 * W, W, W])


def _gm_causal_w(ws_ref, g):
    T = ws_ref.shape[1]
    return jnp.where(_iota2((T, T), 1) <= _iota2((T, T), 0), ws_ref[g], 0.0).astype(BF16)


def _gm_spatial(name, u, vn, w_s, b_sT):
    B, S, W = u.shape
    G, T = w_s.shape[0], w_s.shape[1]
    assert W == G * T, "a head group is as wide as a chunk is long"

    def body(u_ref, vn_ref, ws_ref, bs_ref, y_ref):
        for g in range(G):
            cs = slice(g * T, (g + 1) * T)
            sv = jnp.dot(_gm_causal_w(ws_ref, g), vn_ref[0, :, cs], preferred_element_type=F32) + bs_ref[:, g:g + 1]
            y_ref[0, :, cs] = (u_ref[0, :, cs] * sv).astype(y_ref.dtype)

    row = pl.BlockSpec((1, T, W), lambda b, i: (b, i, 0))
    return pl.pallas_call(
        body, name=_unique(name), grid=(B, S // T),
        in_specs=[row, row, pl.BlockSpec((G, T, T), lambda b, i: (0, 0, 0)), pl.BlockSpec((T, G), lambda b, i: (0, 0))],
        out_specs=row, out_shape=jax.ShapeDtypeStruct((B, S, W), BF16),
        compiler_params=_params(("parallel", "parallel")))(u, vn, w_s, b_sT)


def _gm_spatial_bwd(name, dyg, u, vn, w_s, b_sT):
    B, S, W = u.shape
    G, T = w_s.shape[0], w_s.shape[1]
    assert W == G * T, "a head group is as wide as a chunk is long"

    def body(dy_ref, u_ref, vn_ref, ws_ref, bs_ref, du_ref, dvn_ref, dws_ref, dbs_ref):
        first = jnp.logical_and(pl.program_id(0) == 0, pl.program_id(1) == 0)

        @pl.when(first)
        def _():
            dws_ref[...] = jnp.zeros_like(dws_ref)
            dbs_ref[...] = jnp.zeros_like(dbs_ref)

        tril = _iota2((T, T), 1) <= _iota2((T, T), 0)
        for g in range(G):
            cs = slice(g * T, (g + 1) * T)
            wm = _gm_causal_w(ws_ref, g)
            vng = vn_ref[0, :, cs]
            sv = jnp.dot(wm, vng, preferred_element_type=F32) + bs_ref[:, g:g + 1]
            dy = dy_ref[0, :, cs]
            du_ref[0, :, cs] = dy * sv
            dsv = dy * u_ref[0, :, cs]
            dsv16 = dsv.astype(BF16)
            dvn_ref[0, :, cs] = lax.dot_general(wm, dsv16, (((0,), (0,)), ((), ())), preferred_element_type=F32)
            dw = lax.dot_general(dsv16, vng, (((1,), (1,)), ((), ())), preferred_element_type=F32)
            dws_ref[g] += jnp.where(tril, dw, 0.0)
            dbs_ref[:, g:g + 1] += jnp.sum(dsv, axis=1, keepdims=True)

    row = pl.BlockSpec((1, T, W), lambda b, i: (b, i, 0))
    return pl.pallas_call(
        body, name=_unique(name), grid=(B, S // T),
        in_specs=[row, row, row, pl.BlockSpec((G, T, T), lambda b, i: (0, 0, 0)), pl.BlockSpec((T, G), lambda b, i: (0, 0))],
        out_specs=[row, row, pl.BlockSpec((G, T, T), lambda b, i: (0, 0, 0)), pl.BlockSpec((T, G), lambda b, i: (0, 0))],
        out_shape=[jax.ShapeDtypeStruct((B, S, W), F32), jax.ShapeDtypeStruct((B, S, W), F32),
                   jax.ShapeDtypeStruct((G, T, T), F32), jax.ShapeDtypeStruct((T, G), F32)],
        compiler_params=_params(("arbitrary", "arbitrary")))(dyg, u, vn, w_s, b_sT)


def _cv_glu(name, pw, b_in):
    W = pw.shape[2] // 2

    def fn(r, bv, cv):
        z = r[0] + cv[0]
        return [z[:, :W] * _sigmoid(z[:, W:])], [], []
    return _rowwise(name, fn, [pw], [], [b_in], out_rows=[(W, F32)])[0]


def _cv_glu_bwd(name, pw, dyg, b_in):
    W = pw.shape[2] // 2

    def fn(r, bv, cv):
        z = r[0] + cv[0]
        a, s = z[:, :W], _sigmoid(z[:, W:])
        dpw = jnp.concatenate([r[1] * s, r[1] * a * s * (1.0 - s)], axis=1)
        return [dpw], [], [_csum(dpw)]
    return _rowwise(name, fn, [pw, dyg], [], [b_in], out_rows=[(2 * W, BF16)], out_tsums=[2 * W])


def _cv_ln_act(name, yc, ln_g, ln_b):
    D = yc.shape[2]

    def fn(r, bv, cv):
        xhat, _ = _ln_stats(r[0])
        return [_silu(xhat * cv[0] + cv[1])], [], []
    return _rowwise(name, fn, [yc], [], [ln_g, ln_b], out_rows=[(D, BF16)])[0]


def _cv_ln_act_bwd(name, yc, dys, ln_g, ln_b):
    D = yc.shape[2]

    def fn(r, bv, cv):
        xhat, rstd = _ln_stats(r[0])
        dyn = r[1] * _dsilu(xhat * cv[0] + cv[1])
        dyc = _ln_bwd(dyn * cv[0], xhat, rstd)
        return [dyc], [], [_csum(dyn * xhat), _csum(dyn), _csum(dyc)]
    return _rowwise(name, fn, [yc, dys], [], [ln_g, ln_b], out_rows=[(D, F32)], out_tsums=[D, D, D])


CONV_HALO = 32
CONV_TS, CONV_TC = 256, 128


def _dwconv(name, y, dw, dw_b):
    B, S, D = y.shape
    ts, tc, halo, K = _pick(S, CONV_TS, SUBLANES), _pick(D, CONV_TC), CONV_HALO, CONV_WIDTH

    def body(cur_ref, prev_ref, dw_ref, b_ref, o_ref, buf):
        i = pl.program_id(1)
        buf[pl.ds(0, halo), :] = jnp.where(i > 0, prev_ref[0, pl.ds(ts - halo, halo), :], 0.0)
        buf[pl.ds(halo, ts), :] = cur_ref[0]
        acc = jnp.zeros((ts, tc), F32) + b_ref[...]
        for k in range(K):
            acc = acc + dw_ref[k:k + 1, :] * buf[pl.ds(halo - (K - 1) + k, ts), :]
        o_ref[0] = acc

    return pl.pallas_call(
        body, name=_unique(name), grid=(B, S // ts, D // tc),
        in_specs=[pl.BlockSpec((1, ts, tc), lambda b, i, j: (b, i, j)),
                  pl.BlockSpec((1, ts, tc), lambda b, i, j: (b, jnp.maximum(i - 1, 0), j)),
                  pl.BlockSpec((halo, tc), lambda b, i, j: (0, j)), pl.BlockSpec((1, tc), lambda b, i, j: (0, j))],
        out_specs=pl.BlockSpec((1, ts, tc), lambda b, i, j: (b, i, j)),
        out_shape=jax.ShapeDtypeStruct((B, S, D), F32),
        scratch_shapes=[pltpu.VMEM((halo + ts, tc), F32)],
        compiler_params=_params(("parallel", "parallel", "parallel")))(y, y, dw, dw_b)


def _dwconv_bwd(name, dyc, y, dw):
    B, S, D = y.shape
    ts, tc, halo, K = _pick(S, CONV_TS, SUBLANES), _pick(D, CONV_TC), CONV_HALO, CONV_WIDTH
    nt = S // ts

    def body(g_ref, gnext_ref, y_ref, yprev_ref, dw_ref, dy_ref, ddw_ref, gbuf, ybuf):
        b, i = pl.program_id(1), pl.program_id(2)
        first = jnp.logical_and(b == 0, i == 0)

        @pl.when(first)
        def _():
            ddw_ref[...] = jnp.zeros_like(ddw_ref)

        g = g_ref[0]
        gbuf[pl.ds(0, ts), :] = g
        gbuf[pl.ds(ts, halo), :] = jnp.where(i < nt - 1, gnext_ref[0, pl.ds(0, halo), :], 0.0)
        ybuf[pl.ds(0, halo), :] = jnp.where(i > 0, yprev_ref[0, pl.ds(ts - halo, halo), :], 0.0)
        ybuf[pl.ds(halo, ts), :] = y_ref[0]
        acc = jnp.zeros((ts, tc), F32)
        for k in range(K):
            acc = acc + dw_ref[k:k + 1, :] * gbuf[pl.ds(K - 1 - k, ts), :]
            ddw_ref[k:k + 1, :] += _csum(g * ybuf[pl.ds(halo - (K - 1) + k, ts), :])
        dy_ref[0] = acc

    tile = lambda f: pl.BlockSpec((1, ts, tc), f)
    return pl.pallas_call(
        body, name=_unique(name), grid=(D // tc, B, nt),
        in_specs=[tile(lambda j, b, i: (b, i, j)), tile(lambda j, b, i: (b, jnp.minimum(i + 1, nt - 1), j)),
                  tile(lambda j, b, i: (b, i, j)), tile(lambda j, b, i: (b, jnp.maximum(i - 1, 0), j)),
                  pl.BlockSpec((halo, tc), lambda j, b, i: (0, j))],
        out_specs=[tile(lambda j, b, i: (b, i, j)), pl.BlockSpec((halo, tc), lambda j, b, i: (0, j))],
        out_shape=[jax.ShapeDtypeStruct((B, S, D), F32), jax.ShapeDtypeStruct((halo, D), F32)],
        scratch_shapes=[pltpu.VMEM((ts + halo, tc), F32), pltpu.VMEM((halo + ts, tc), F32)],
        compiler_params=_params(("parallel", "arbitrary", "arbitrary")))(dyc, dyc, y, y, dw)


ATT_BLOCK = 128
ATT_QUERY_BLOCK = 256
FOX_KEY_BLOCK = 1024
SB_KEY_BLOCK = 512
ATT_PIECE_ROWS = 32
FOX_GATE_COLS = 128


def _att_tiles(S, key_block):
    return _pick(S, ATT_QUERY_BLOCK, SUBLANES), _pick(S, key_block, LANES)


def _pieces(T, TK):
    R = min(T, ATT_PIECE_ROWS)
    segs = [slice(c, c + LANES) for c in range(0, TK, LANES)]
    return [(slice(r, r + R), segs) for r in range(0, T, R)]


def _piece_keep(row0, col0, rs, cs, strict, lane_major_of=0):
    shape = (rs.stop - rs.start, cs.stop - cs.start)
    lane = _iota2(shape, 1)
    key = col0 + (lane * lane_major_of + cs.start // LANES if lane_major_of else cs.start + lane)
    qry = row0 + rs.start + _iota2(shape, 0)
    return key < qry if strict else key <= qry


def _causal_tiles(i, tq, tk):
    return (i * tq + tq + tk - 1) // tk


def _lane_major(t, tk):
    B, H, S, dh = t.shape
    return t.reshape(B, H, S // tk, LANES, tk // LANES, dh).swapaxes(3, 4).reshape(B, H, S, dh)


def _lane_major_inverse(t, tk):
    B, H, S, dh = t.shape
    return t.reshape(B, H, S // tk, tk // LANES, LANES, dh).swapaxes(3, 4).reshape(B, H, S, dh)


def _log_sigmoid(x):
    return jnp.minimum(x, 0.0) - jnp.log(1.0 + jnp.exp(-jnp.abs(x)))


def _fox_gate_cumsum(name, fl, b_f):
    B, S, C = fl.shape
    T = _pick(S, ATT_BLOCK, SUBLANES)

    def body(fl_ref, bf_ref, f_ref, carry):
        @pl.when(pl.program_id(1) == 0)
        def _():
            carry[...] = jnp.zeros_like(carry)
        lf = _log_sigmoid(fl_ref[0] + bf_ref[...])
        lower = (_iota2((T, T), 1) <= _iota2((T, T), 0)).astype(BF16)
        f = _split_dot(lf, lower, lhs01=True, terms=3) + carry[...]
        f_ref[0] = f
        carry[...] = f[T - 1:T, :]

    return pl.pallas_call(
        body, name=_unique(name), grid=(B, S // T),
        in_specs=[pl.BlockSpec((1, T, C), lambda b, i: (b, i, 0)), pl.BlockSpec((1, C), lambda b, i: (0, 0))],
        out_specs=pl.BlockSpec((1, T, C), lambda b, i: (b, i, 0)),
        out_shape=jax.ShapeDtypeStruct((B, S, C), F32),
        scratch_shapes=[pltpu.VMEM((1, C), F32)],
        compiler_params=_params(("arbitrary", "arbitrary")))(fl, b_f)


def _fox_gate_bwd(name, dF, fl, b_f, n_heads):
    B, S, C = fl.shape
    T = _pick(S, ATT_BLOCK, SUBLANES)
    nt = S // T

    def body(df_ref, fl_ref, bf_ref, dfl_ref, dbf_ref, carry):
        first = jnp.logical_and(pl.program_id(0) == 0, pl.program_id(1) == 0)

        @pl.when(pl.program_id(1) == 0)
        def _():
            carry[...] = jnp.zeros_like(carry)

        @pl.when(first)
        def _():
            dbf_ref[...] = jnp.zeros_like(dbf_ref)

        upper = (_iota2((T, T), 1) >= _iota2((T, T), 0)).astype(BF16)
        dlf = _split_dot(df_ref[0], upper, lhs01=True, terms=3) + carry[...]
        carry[...] = dlf[0:1, :]
        x = fl_ref[0] + bf_ref[...]
        dfl = jnp.where(_iota2((T, C), 1) < n_heads, dlf * _sigmoid(-x), 0.0)
        dfl_ref[0] = dfl
        dbf_ref[...] += _csum(dfl)

    rev = lambda b, i: (b, nt - 1 - i, 0)
    return pl.pallas_call(
        body, name=_unique(name), grid=(B, nt),
        in_specs=[pl.BlockSpec((1, T, C), rev), pl.BlockSpec((1, T, C), rev), pl.BlockSpec((1, C), lambda b, i: (0, 0))],
        out_specs=[pl.BlockSpec((1, T, C), rev), pl.BlockSpec((1, C), lambda b, i: (0, 0))],
        out_shape=[jax.ShapeDtypeStruct((B, S, C), F32), jax.ShapeDtypeStruct((1, C), F32)],
        scratch_shapes=[pltpu.VMEM((1, C), F32)],
        compiler_params=_params(("arbitrary", "arbitrary")))(dF, fl, b_f)


_NT = (((1,), (1,)), ((), ()))
_TN = (((0,), (0,)), ((), ()))


def _fox_fwd(name, q, k, v, fq, fk):
    B, H, S, dh = q.shape
    T, TK = _att_tiles(S, FOX_KEY_BLOCK)
    scale = np.float32(dh ** -0.5)

    pieces = _pieces(T, TK)

    def body(q_ref, k_ref, v_ref, fq_ref, fk_ref, o_ref, lse_ref, s_scr, p_scr):
        i = pl.program_id(2)
        qb = q_ref[0, 0]
        n_tiles = _causal_tiles(i, T, TK)

        def tile(j, carry, diag):
            m, l, acc = carry
            ks = pl.ds(pl.multiple_of(j * TK, TK), TK)
            s_scr[...] = lax.dot_general(qb, k_ref[0, 0, ks, :], _NT, preferred_element_type=F32) * scale
            fkj = fk_ref[0, 0, pl.ds(j, 1), :]
            m_new = []
            for rc, (rs, segs) in enumerate(pieces):
                fq_c, mx = fq_ref[0, 0, rs, :], None
                for cs in segs:
                    s = s_scr[rs, cs] + fq_c - fkj[:, cs]
                    if diag:
                        s = jnp.where(_piece_keep(i * T, j * TK, rs, cs, False), s, NEG_INF)
                    s_scr[rs, cs] = s
                    mx = s if mx is None else jnp.maximum(mx, s)
                m_new.append(jnp.maximum(m[rc], jnp.max(mx, axis=1, keepdims=True)))
            alpha, l_new = [], []
            for rc, (rs, segs) in enumerate(pieces):
                alpha.append(jnp.exp(m[rc] - m_new[rc]))
                psum = None
                for cs in segs:
                    p = jnp.exp(s_scr[rs, cs] - m_new[rc])
                    p_scr[rs, cs] = p.astype(BF16)
                    psum = p if psum is None else psum + p
                l_new.append(alpha[rc] * l[rc] + jnp.sum(psum, axis=1, keepdims=True))
            acc = jnp.concatenate(alpha, axis=0) * acc + jnp.dot(p_scr[...], v_ref[0, 0, ks, :], preferred_element_type=F32)
            return tuple(m_new), tuple(l_new), acc

        init = (tuple(jnp.full((rs.stop - rs.start, 1), NEG_INF, F32) for rs, _ in pieces),
                tuple(jnp.zeros((rs.stop - rs.start, 1), F32) for rs, _ in pieces), jnp.zeros((T, dh), F32))
        carry = lax.fori_loop(0, n_tiles - 1, lambda j, c: tile(j, c, False), init)
        m, l, acc = tile(n_tiles - 1, carry, True)
        m, l = jnp.concatenate(m, axis=0), jnp.concatenate(l, axis=0)
        o_ref[0, 0] = acc / l
        lse_ref[0, 0] = m + jnp.log(l)

    full = lambda w: pl.BlockSpec((1, 1, S, w), lambda b, h, i: (b, h, 0, 0))
    blk = lambda w: pl.BlockSpec((1, 1, T, w), lambda b, h, i: (b, h, i, 0))
    return pl.pallas_call(
        body, name=_unique(name), grid=(B, H, S // T),
        in_specs=[blk(dh), full(dh), full(dh), blk(1), pl.BlockSpec((1, 1, S // TK, TK), lambda b, h, i: (b, h, 0, 0))],
        out_specs=[blk(dh), blk(1)],
        out_shape=[jax.ShapeDtypeStruct((B, H, S, dh), F32), jax.ShapeDtypeStruct((B, H, S, 1), F32)],
        scratch_shapes=[pltpu.VMEM((T, TK), F32), pltpu.VMEM((T, TK), BF16)],
        compiler_params=_params(("parallel", "parallel", "parallel")))(q, k, v, fq, fk)


def _fox_bwd(name, q, k, v, fq, fk, do, lse):
    B, H, S, dh = q.shape
    T, TK = _att_tiles(S, FOX_KEY_BLOCK)
    nt, nkt = S // T, S // TK
    scale = np.float32(dh ** -0.5)

    pieces = _pieces(T, TK)

    def body(q_ref, k_ref, v_ref, fq_ref, fk_ref, do_ref, lse_ref, dq_ref, dk_ref, dv_ref, dfk_ref,
             p_buf, dp_buf, s_scr, ds_scr, p16_scr):
        dk_ref[...] = jnp.zeros_like(dk_ref)
        dv_ref[...] = jnp.zeros_like(dv_ref)
        dfk_ref[...] = jnp.zeros_like(dfk_ref)

        def qloop(i, _):
            qs = pl.ds(pl.multiple_of(i * T, T), T)
            qb, dob16 = q_ref[0, 0, qs, :], do_ref[0, 0, qs, :].astype(BF16)
            n_tiles = _causal_tiles(i, T, TK)
            row_at = lambda rs: pl.ds(pl.multiple_of(i * T + rs.start, SUBLANES), rs.stop - rs.start)
            fq_c = [fq_ref[0, 0, row_at(rs), :] for rs, _ in pieces]
            lse_c = [lse_ref[0, 0, row_at(rs), :] for rs, _ in pieces]

            def sweep1(j, delta, diag):
                ks = pl.ds(pl.multiple_of(j * TK, TK), TK)
                s_scr[...] = lax.dot_general(qb, k_ref[0, 0, ks, :], _NT, preferred_element_type=F32) * scale
                dp_buf[j] = lax.dot_general(dob16, v_ref[0, 0, ks, :], _NT, preferred_element_type=F32)
                fkj = fk_ref[0, 0, pl.ds(j, 1), :]
                out = []
                for rc, (rs, segs) in enumerate(pieces):
                    pdp = None
                    for cs in segs:
                        p = jnp.exp(((s_scr[rs, cs] + fq_c[rc]) - fkj[:, cs]) - lse_c[rc])
                        if diag:
                            p = jnp.where(_piece_keep(i * T, j * TK, rs, cs, False), p, 0.0)
                        p_buf[j, rs, cs] = p
                        pdp = p * dp_buf[j, rs, cs] if pdp is None else pdp + p * dp_buf[j, rs, cs]
                    out.append(delta[rc] + jnp.sum(pdp, axis=1, keepdims=True))
                return tuple(out)

            zeros = tuple(jnp.zeros((rs.stop - rs.start, 1), F32) for rs, _ in pieces)
            delta = lax.fori_loop(0, n_tiles - 1, lambda j, d: sweep1(j, d, False), zeros)
            delta = sweep1(n_tiles - 1, delta, True)

            def sweep2(j, dq):
                ks = pl.ds(pl.multiple_of(j * TK, TK), TK)
                col = [None] * len(pieces[0][1])
                for rc, (rs, segs) in enumerate(pieces):
                    for sg, cs in enumerate(segs):
                        p = p_buf[j, rs, cs]
                        ds = p * (dp_buf[j, rs, cs] - delta[rc])
                        ds_scr[rs, cs] = ds.astype(BF16)
                        p16_scr[rs, cs] = p.astype(BF16)
                        col[sg] = ds if col[sg] is None else col[sg] + ds
                dfk_ref[0, 0, pl.ds(j, 1), :] -= jnp.concatenate([_csum(c) for c in col], axis=1)
                ds16 = ds_scr[...]
                dk_ref[0, 0, ks, :] += lax.dot_general(ds16, qb, _TN, preferred_element_type=F32)
                dv_ref[0, 0, ks, :] += lax.dot_general(p16_scr[...], dob16, _TN, preferred_element_type=F32)
                return dq + jnp.dot(ds16, k_ref[0, 0, ks, :], preferred_element_type=F32)

            dq_ref[0, 0, qs, :] = lax.fori_loop(0, n_tiles, sweep2, jnp.zeros((T, dh), F32)) * scale
            return 0

        lax.fori_loop(0, nt, qloop, 0)
        dk_ref[...] = dk_ref[...] * scale

    full = lambda w: pl.BlockSpec((1, 1, S, w), lambda b, h: (b, h, 0, 0))
    fks = pl.BlockSpec((1, 1, nkt, TK), lambda b, h: (b, h, 0, 0))
    return pl.pallas_call(
        body, name=_unique(name), grid=(B, H),
        in_specs=[full(dh), full(dh), full(dh), full(1), fks, full(dh), full(1)],
        out_specs=[full(dh), full(dh), full(dh), fks],
        out_shape=[jax.ShapeDtypeStruct((B, H, S, dh), F32)] * 3 + [jax.ShapeDtypeStruct((B, H, nkt, TK), F32)],
        scratch_shapes=[pltpu.VMEM((nkt, T, TK), F32), pltpu.VMEM((nkt, T, TK), F32), pltpu.VMEM((T, TK), F32),
                        pltpu.VMEM((T, TK), BF16), pltpu.VMEM((T, TK), BF16)],
        compiler_params=_params(("parallel", "parallel")))(q, k, v, fq, fk, do, lse)


def _sb_terms(z, with_sigmoids=True):
    t = jnp.exp(-jnp.abs(z))
    lp = jnp.log(1.0 + t)
    lb, l1 = jnp.minimum(z, 0.0) - lp, jnp.minimum(-z, 0.0) - lp
    if not with_sigmoids:
        return lb, l1, None, None
    return lb, l1, jnp.exp(lb), jnp.exp(l1)


SCAN_RADIX = 4


def _lane_scan(x, reverse):
    lane = _iota2(x.shape, 1)
    y, d = x, 1
    while d < LANES:
        step = y
        for m in range(1, SCAN_RADIX):
            if m * d < LANES:
                if reverse:
                    step = step + jnp.where(lane + m * d < LANES, pltpu.roll(y, LANES - m * d, 1), 0.0)
                else:
                    step = step + jnp.where(lane >= m * d, pltpu.roll(y, m * d, 1), 0.0)
        y, d = step, d * SCAN_RADIX
    return y


def _chunk_scan(xs, reverse):
    n = len(xs)
    within, acc = [None] * n, None
    for s in (range(n - 1, -1, -1) if reverse else range(n)):
        acc = xs[s] if acc is None else acc + xs[s]
        within[s] = acc
    lanes = _lane_scan(acc, reverse)
    beyond = lanes - acc
    return [w + beyond for w in within], (lanes[:, 0:1] if reverse else lanes[:, LANES - 1:LANES])


SB_DEAD = 110.0


def _sb_fwd(name, q, k, v):
    B, H, S, dh = q.shape
    T, TK = _att_tiles(S, SB_KEY_BLOCK)
    scale = np.float32(dh ** -0.5)
    pieces = _pieces(T, TK)

    def body(q_ref, k_ref, v_ref, o_ref, lt_ref, first_ref, z_scr, a_scr):
        i = pl.program_id(2)
        qb = q_ref[0, 0]
        n_tiles = _causal_tiles(i, T, TK)

        def tile(j, carry, diag):
            runs, acc = carry
            ks = pl.ds(pl.multiple_of(j * TK, TK), TK)
            z_scr[...] = lax.dot_general(qb, k_ref[0, 0, ks, :], _NT, preferred_element_type=F32) * scale
            new_runs = []
            for rc, (rs, segs) in enumerate(pieces):
                terms = [_sb_terms(z_scr[rs, cs], False) for cs in segs]
                keep = [_piece_keep(i * T, j * TK, rs, cs, True, len(segs)) if diag else None for cs in segs]
                l1 = [jnp.where(kp, t[1], 0.0) if diag else t[1] for kp, t in zip(keep, terms)]
                right_of, total = _chunk_scan(l1, True)
                for cs, kp, t, x, r in zip(segs, keep, terms, l1, right_of):
                    a = jnp.exp(t[0] + ((r - x) + runs[rc]))
                    a_scr[rs, cs] = (jnp.where(kp, a, 0.0) if diag else a).astype(BF16)
                new_runs.append(runs[rc] + total)
            return tuple(new_runs), acc + jnp.dot(a_scr[...], v_ref[0, 0, ks, :], preferred_element_type=F32)

        init = (tuple(jnp.zeros((rs.stop - rs.start, 1), F32) for rs, _ in pieces), jnp.zeros((T, dh), F32))
        def some_row_alive(runs):
            worst = runs[0]
            for r in runs[1:]:
                worst = jnp.maximum(worst, r)
            return jnp.max(worst) > -SB_DEAD

        def step(c):
            runs, acc = tile(n_tiles - 1 - c[0], (c[1], c[2]), False)
            return c[0] + 1, runs, acc

        visited, runs, acc = lax.while_loop(lambda c: jnp.logical_and(c[0] < n_tiles, some_row_alive(c[1])), step,
                                            (jnp.int32(1), *tile(n_tiles - 1, init, True)))
        o_ref[0, 0] = acc
        lt_ref[0, 0] = jnp.concatenate(runs, axis=0)
        first_ref[pl.program_id(0), pl.program_id(1), i] = (n_tiles - visited).astype(F32)

    full = lambda w: pl.BlockSpec((1, 1, S, w), lambda b, h, i: (b, h, 0, 0))
    blk = lambda w: pl.BlockSpec((1, 1, T, w), lambda b, h, i: (b, h, i, 0))
    return pl.pallas_call(
        body, name=_unique(name), grid=(B, H, S // T),
        in_specs=[blk(dh), full(dh), full(dh)], out_specs=[blk(dh), blk(1), pl.BlockSpec(memory_space=pltpu.SMEM)],
        out_shape=[jax.ShapeDtypeStruct((B, H, S, dh), F32), jax.ShapeDtypeStruct((B, H, S, 1), F32),
                   jax.ShapeDtypeStruct((B, H, S // T), F32)],
        scratch_shapes=[pltpu.VMEM((T, TK), F32), pltpu.VMEM((T, TK), BF16)],
        compiler_params=_params(("arbitrary", "arbitrary", "arbitrary")))(q, k, v)


def _sb_bwd(name, q, k, v, do, lt, first):
    B, H, S, dh = q.shape
    T, TK = _att_tiles(S, SB_KEY_BLOCK)
    nt = S // T
    scale = np.float32(dh ** -0.5)

    pieces = _pieces(T, TK)

    def body(first_ref, q_ref, k_ref, v_ref, do_ref, lt_ref, dq_ref, dk_ref, dv_ref, z_scr, da_scr, dz_scr, a_scr):
        dk_ref[...] = jnp.zeros_like(dk_ref)
        dv_ref[...] = jnp.zeros_like(dv_ref)
        b, h = pl.program_id(0), pl.program_id(1)

        def qloop(i, _):
            qs = pl.ds(pl.multiple_of(i * T, T), T)
            qb, dob16 = q_ref[0, 0, qs, :], do_ref[0, 0, qs, :].astype(BF16)
            n_tiles = _causal_tiles(i, T, TK)
            first_tile = jnp.clip(first_ref[b, h, i].astype(jnp.int32), 0, n_tiles - 1)
            lt_c = [lt_ref[0, 0, pl.ds(pl.multiple_of(i * T + rs.start, SUBLANES), rs.stop - rs.start), :] for rs, _ in pieces]

            def tile(j, carry, diag):
                sums_l, sums_e, dq = carry
                ks = pl.ds(pl.multiple_of(j * TK, TK), TK)
                kb, vb = k_ref[0, 0, ks, :], v_ref[0, 0, ks, :]
                z_scr[...] = lax.dot_general(qb, kb, _NT, preferred_element_type=F32) * scale
                da_scr[...] = lax.dot_general(dob16, vb, _NT, preferred_element_type=F32)
                new_l, new_e = [], []
                for rc, (rs, segs) in enumerate(pieces):
                    terms = [_sb_terms(z_scr[rs, cs]) for cs in segs]
                    keep = [_piece_keep(i * T, j * TK, rs, cs, True, len(segs)) if diag else None for cs in segs]
                    l1 = [jnp.where(kp, t[1], 0.0) if diag else t[1] for kp, t in zip(keep, terms)]
                    upto, total_l = _chunk_scan(l1, False)
                    es = []
                    for cs, kp, t, u in zip(segs, keep, terms, upto):
                        a = jnp.exp(t[0] + (lt_c[rc] - (u + sums_l[rc])))
                        if diag:
                            a = jnp.where(kp, a, 0.0)
                        a_scr[rs, cs] = a.astype(BF16)
                        es.append(da_scr[rs, cs] * a)
                    e_upto, total_e = _chunk_scan(es, False)
                    for cs, kp, t, e, eu in zip(segs, keep, terms, es, e_upto):
                        dz = e * t[3] - ((eu - e) + sums_e[rc]) * t[2]
                        dz_scr[rs, cs] = (jnp.where(kp, dz, 0.0) if diag else dz).astype(BF16)
                    new_l.append(sums_l[rc] + total_l)
                    new_e.append(sums_e[rc] + total_e)
                dz16 = dz_scr[...]
                dk_ref[0, 0, ks, :] += lax.dot_general(dz16, qb, _TN, preferred_element_type=F32)
                dv_ref[0, 0, ks, :] += lax.dot_general(a_scr[...], dob16, _TN, preferred_element_type=F32)
                return tuple(new_l), tuple(new_e), dq + jnp.dot(dz16, kb, preferred_element_type=F32)

            zeros = tuple(jnp.zeros((rs.stop - rs.start, 1), F32) for rs, _ in pieces)
            carry = lax.fori_loop(first_tile, n_tiles - 1, lambda j, c: tile(j, c, False), (zeros, zeros, jnp.zeros((T, dh), F32)))
            dq_ref[0, 0, qs, :] = tile(n_tiles - 1, carry, True)[2] * scale
            return 0

        lax.fori_loop(0, nt, qloop, 0)
        dk_ref[...] = dk_ref[...] * scale

    full = lambda w: pl.BlockSpec((1, 1, S, w), lambda b, h: (b, h, 0, 0))
    return pl.pallas_call(
        body, name=_unique(name), grid=(B, H),
        in_specs=[pl.BlockSpec(memory_space=pltpu.SMEM), full(dh), full(dh), full(dh), full(dh), full(1)], out_specs=[full(dh)] * 3,
        out_shape=[jax.ShapeDtypeStruct((B, H, S, dh), F32)] * 3,
        scratch_shapes=[pltpu.VMEM((T, TK), F32), pltpu.VMEM((T, TK), F32), pltpu.VMEM((T, TK), BF16), pltpu.VMEM((T, TK), BF16)],
        compiler_params=_params(("parallel", "parallel")))(first, q, k, v, do, lt)


def _adamw(name, w, g, m, v):
    shape = w.shape
    n = w.size
    cols = shape[-1] if (w.ndim >= 2 and (shape[-1] % LANES == 0 or n // shape[-1] >= LANES)) else 0
    if cols:
        prep = lambda t: t.reshape(1, n // cols, cols)
    else:
        cols = LANES
        pad = (-n) % (SUBLANES * LANES)
        prep = lambda t: jnp.pad(t.reshape(-1), (0, pad), constant_values=1.0).reshape(1, (n + pad) // cols, cols)

    def fn(r, bv, cv):
        w_, g_, m_, v_ = r
        m2 = ADAM_B1 * m_ + (1.0 - ADAM_B1) * g_
        v2 = ADAM_B2 * v_ + (1.0 - ADAM_B2) * (g_ * g_)
        m_hat = m2 / (1.0 - ADAM_B1 ** ADAM_STEP)
        v_hat = v2 / (1.0 - ADAM_B2 ** ADAM_STEP)
        return [-ADAM_LR * (m_hat / (jnp.sqrt(v_hat) + ADAM_EPS) + ADAM_WD * w_), m2, v2], [], []
    outs = _rowwise(name, fn, [prep(w), prep(g), prep(m), prep(v)], out_rows=[(cols, F32)] * 3, tm=512)
    return [o.reshape(-1)[:n].reshape(shape) for o in outs]


def _sum8(name, parts):
    def fn(r, bv, cv):
        s = r[0]
        for t in r[1:]:
            s = s + t
        return [s], [], []
    rows = [parts[i][None] for i in range(parts.shape[0])]
    return _rowwise(name, fn, rows, out_rows=[(parts.shape[2], F32)])[0][0]


def _pack(arrs, cols, dtype, row_mult):
    flat = jnp.concatenate([a.reshape(-1).astype(dtype) for a in arrs])
    pad = (-flat.size) % (cols * row_mult)
    return jnp.pad(flat, (0, pad)).reshape(-1, cols)


def _unpack(flat, shapes):
    out, off = [], 0
    for s in shapes:
        n = int(np.prod(s))
        out.append(flat[off:off + n].reshape(s))
        off += n
    return out


def _heads(t, H):
    B, S, W = t.shape
    return t.reshape(B, S, H, W // H).transpose(0, 2, 1, 3)


def _unheads(t):
    B, H, S, dh = t.shape
    return t.transpose(0, 2, 1, 3).reshape(B, S, H * dh)


def kernel(*args):
    _names_used.clear()
    p = dict(zip(ARGS, args))
    x, target = p['x'], p['loss_target']
    B, S, D = x.shape
    T = B * S
    depth = p['ln1_g'].shape[0]
    H = D // HEAD_DIM
    alpha = np.float32((2.0 * depth) ** 0.25)
    cx, cy, cc = _mesh_pos()
    my_q = 2 * cx + cy
    axes = ("x", "y", "c")
    two = lambda t: t.reshape(T, t.shape[-1])
    three = lambda t: t.reshape(B, S, t.shape[-1])

    small_in = [p['c']] + [p[n] for n in SMALL_SPLIT]
    g1 = _all_gather8("ag_small", [_pack(small_in, LANES, F32, SUBLANES)])[0]
    g1 = g1.reshape(N_DEV, -1)
    c_all = g1[:, :B * D].reshape(N_DEV * B, D)
    per_chip = [_unpack(g1[2 * q], [a.shape for a in small_in])[1:] for q in range(N_CHIPS)]
    small = {n: jnp.concatenate([per_chip[q][i] for q in range(N_CHIPS)], axis=-1) for i, n in enumerate(SMALL_SPLIT)}
    for n in SMALL_REPL:
        small[n] = p[n]

    n_seq = N_DEV * B
    seq_pad = -(-n_seq // LANES) * LANES
    c_act = _rowwise("c_act", lambda r, bv, cv: ([_silu(r[0])], [], []),
                     [jnp.pad(c_all, ((0, seq_pad - n_seq), (0, 0)))[None]], out_rows=[(D, F32)])[0][0]
    mod_cols = p['mod_w'].shape[2]
    mod_part = jnp.stack([_mm(f"mod_fwd{l}", c_act, p['mod_w'][l])[:n_seq] for l in range(depth)])
    half_layers = depth // 2
    mod_half = lax.dynamic_slice_in_dim(mod_part, cc * half_layers, half_layers, axis=0)
    gm_ = _all_gather8("ag_mod", [mod_half.reshape(half_layers * n_seq, mod_cols)])[0]
    mod_all = gm_.reshape(N_CHIPS, 2, half_layers, n_seq, mod_cols).transpose(1, 2, 3, 0, 4).reshape(depth, n_seq, 6 * D)
    mod_mine = lax.dynamic_slice_in_dim(mod_all, (2 * my_q + cc) * B, B, axis=1)
    mod = _rowwise("mod_bias", lambda r, bv, cv: ([r[0] + bv[0]], [], []), [mod_mine], [p['mod_b'][:, None, :]],
                   out_rows=[(6 * D, F32)])[0]
    mods = [[mod[l, :, None, i * D:(i + 1) * D] for i in range(6)] for l in range(depth)]

    big_names = list(BIG)
    shard_shapes = [p[n].shape for n in big_names]
    half_rows = [s[0] * s[1] // 2 for s in shard_shapes]
    w_halves = [lax.dynamic_slice_in_dim(p[n].reshape(-1, s[2]), cc * hr, hr, axis=0).astype(BF16)
                for n, s, hr in zip(big_names, shard_shapes, half_rows)]
    W = {}
    for n, s, g in zip(big_names, shard_shapes, _all_gather8("ag_weights", w_halves)):
        seg = g.reshape((N_CHIPS,) + s)
        W[n] = jnp.concatenate([seg[q] for q in range(N_CHIPS)], axis=BIG[n])

    def vec(n, j):
        return small[n][j][None, :]

    def attn_proj(h1, w_in, gate_cols):
        wp = jnp.pad(w_in, ((0, 0), (0, gate_cols))) if gate_cols else w_in
        proj = three(_mm("att_proj", two(h1), wp))
        q, k, v = [_heads(proj[..., i * D:(i + 1) * D].astype(BF16), H) for i in range(3)]
        return wp, proj, q, k, v

    def gm_fwd(j, h1):
        zin = three(_mm("gm_in", two(h1), W['gm_w_in'][j]))
        u, vn = _gm_act("gm_act", zin, vec('gm_b_in', j), vec('gm_ln_g', j), vec('gm_ln_b', j))
        b_sT = small['gm_b_s'][j].T
        yg = _gm_spatial("gm_spatial", u, vn, small['gm_w_s'][j], b_sT)
        return three(_mm("gm_out", two(yg), W['gm_w_out'][j])), (zin, u, vn, b_sT, yg)

    def gm_bwd(j, h1, dy1, cache):
        zin, u, vn, b_sT, yg = cache
        g = {'gm_w_out': _mm("gm_dwout", two(yg), two(dy1), ta=True)}
        dyg = three(_mm("gm_dyg", two(dy1), W['gm_w_out'][j], tb=True))
        du, dvn, dws, dbsT = _gm_spatial_bwd("gm_spatial_bwd", dyg, u, vn, small['gm_w_s'][j], b_sT)
        dzin, g['gm_b_in'], g['gm_ln_g'], g['gm_ln_b'] = _gm_act_bwd("gm_act_bwd", zin, du, dvn, vec('gm_b_in', j), vec('gm_ln_g', j))
        g['gm_w_s'], g['gm_b_s'] = dws, dbsT.T
        g['gm_w_in'] = _mm("gm_dwin", two(h1), two(dzin), ta=True)
        return _mm("gm_dh", two(dzin), W['gm_w_in'][j], tb=True), g

    def fox_fwd(j, h1):
        wp, proj, q, k, v = attn_proj(h1, W['fox_w_in'][j], 3 * D + FOX_GATE_COLS - W['fox_w_in'].shape[2])
        fl = proj[..., 3 * D:]
        bf = jnp.pad(small['fox_b_f'][j][None, :], ((0, 0), (0, FOX_GATE_COLS - H)))
        Fh = _fox_gate_cumsum("fox_gate", fl, bf)[..., :H].transpose(0, 2, 1)
        fq, fk = Fh[..., None], Fh.reshape(B, H, -1, _att_tiles(S, FOX_KEY_BLOCK)[1])
        o, lse = _fox_fwd("fox_fwd", q, k, v, fq, fk)
        o2 = _unheads(o)
        return three(_mm("fox_out", two(o2), W['fox_w_out'][j])), (wp, q, k, v, fl, bf, fq, fk, lse, o2)

    def fox_bwd(j, h1, dy1, cache):
        wp, q, k, v, fl, bf, fq, fk, lse, o2 = cache
        g = {'fox_w_out': _mm("fox_dwout", two(o2), two(dy1), ta=True)}
        do = _heads(three(_mm("fox_do", two(dy1), W['fox_w_out'][j], tb=True)), H)
        dq, dk, dv, dfk = _fox_bwd("fox_bwd", q, k, v, fq, fk, do, lse)
        dF = jnp.pad(dfk.reshape(B, H, S).transpose(0, 2, 1), ((0, 0), (0, 0), (0, FOX_GATE_COLS - H)))
        dfl, dbf = _fox_gate_bwd("fox_gate_bwd", dF, fl, bf, H)
        dproj = jnp.concatenate([_unheads(dq).astype(BF16), _unheads(dk).astype(BF16), _unheads(dv).astype(BF16),
                                 dfl.astype(BF16)], axis=-1)
        g['fox_w_in'] = _mm("fox_dwin", two(h1), two(dproj), ta=True)[:, :W['fox_w_in'].shape[2]]
        g['fox_b_f'] = dbf[0, :H]
        return _mm("fox_dh", two(dproj), wp, tb=True), g

    def sb_fwd(j, h1):
        wp, proj, q, k, v = attn_proj(h1, W['sb_w_in'][j], 0)
        k, v = _lane_major(k, _att_tiles(S, SB_KEY_BLOCK)[1]), _lane_major(v, _att_tiles(S, SB_KEY_BLOCK)[1])
        o, lt, first = _sb_fwd("sb_fwd", q, k, v)
        o2 = _unheads(o)
        return three(_mm("sb_out", two(o2), W['sb_w_out'][j])), (q, k, v, lt, first, o2)

    def sb_bwd(j, h1, dy1, cache):
        q, k, v, lt, first, o2 = cache
        g = {'sb_w_out': _mm("sb_dwout", two(o2), two(dy1), ta=True)}
        do = _heads(three(_mm("sb_do", two(dy1), W['sb_w_out'][j], tb=True)), H)
        dq, dk, dv = _sb_bwd("sb_bwd", q, k, v, do, lt, first)
        dk, dv = _lane_major_inverse(dk, _att_tiles(S, SB_KEY_BLOCK)[1]), _lane_major_inverse(dv, _att_tiles(S, SB_KEY_BLOCK)[1])
        dproj = jnp.concatenate([_unheads(dq).astype(BF16), _unheads(dk).astype(BF16), _unheads(dv).astype(BF16)], axis=-1)
        g['sb_w_in'] = _mm("sb_dwin", two(h1), two(dproj), ta=True)
        return _mm("sb_dh", two(dproj), W['sb_w_in'][j], tb=True), g

    def cv_fwd(j, h1):
        pw = three(_mm("cv_in", two(h1), W['cv_w_in'][j]))
        ygl = _cv_glu("cv_glu", pw, vec('cv_b_in', j))
        dw = jnp.pad(small['cv_dw'][j], ((0, CONV_HALO - CONV_WIDTH), (0, 0)))
        yc = _dwconv("cv_dwconv", ygl, dw, vec('cv_dw_b', j))
        ys = _cv_ln_act("cv_ln_act", yc, vec('cv_ln_g', j), vec('cv_ln_b', j))
        return three(_mm("cv_out", two(ys), W['cv_w_out'][j])), (pw, ygl, dw, yc, ys)

    def cv_bwd(j, h1, dy1, cache):
        pw, ygl, dw, yc, ys = cache
        g = {'cv_w_out': _mm("cv_dwout", two(ys), two(dy1), ta=True)}
        dys = three(_mm("cv_dys", two(dy1), W['cv_w_out'][j], tb=True))
        dyc, g['cv_ln_g'], g['cv_ln_b'], g['cv_dw_b'] = _cv_ln_act_bwd("cv_ln_act_bwd", yc, dys, vec('cv_ln_g', j), vec('cv_ln_b', j))
        dygl, ddw = _dwconv_bwd("cv_dwconv_bwd", dyc, ygl, dw)
        g['cv_dw'] = ddw[:CONV_WIDTH]
        dpw, g['cv_b_in'] = _cv_glu_bwd("cv_glu_bwd", pw, dygl, vec('cv_b_in', j))
        g['cv_w_in'] = _mm("cv_dwin", two(h1), two(dpw), ta=True)
        return _mm("cv_dh", two(dpw), W['cv_w_in'][j], tb=True), g

    mixers = [(gm_fwd, gm_bwd), (fox_fwd, fox_bwd), (sb_fwd, sb_bwd), (cv_fwd, cv_bwd)]
    n_mix = len(mixers)

    saved = []
    h1 = _modulate("mod1", x, mods[0][1], mods[0][0])
    for l in range(depth):
        m, j = l % n_mix, l // n_mix
        sh1, sc1, g1_, sh2, sc2, g2_ = mods[l]
        ybias = vec('cv_b_out', j) if m == 3 else None
        y1, cache = mixers[m][0](j, h1)
        xm, h2 = _resid_ln("resid_ln1", alpha, x, y1, g1_, small['ln1_g'][l][None], small['ln1_b'][l][None], ybias, then=(sc2, sh2))
        *z, a = _ffn_in_act("ffn_in", two(h2), W['ffn_w_in'][l])
        y2 = three(_mm("ffn_out", a, W['ffn_w_out'][l]))
        nxt = (mods[l + 1][1], mods[l + 1][0]) if l + 1 < depth else None
        xo, *h_next = _resid_ln("resid_ln2", alpha, xm, y2, g2_, small['ln2_g'][l][None], small['ln2_b'][l][None], then=nxt)
        saved.append((x, h1, y1, cache, xm, h2, z, a, y2, ybias))
        x, h1 = xo, (h_next[0] if h_next else None)

    dx, sq = _loss_head("loss_head", x, target)
    loss = lax.psum(jnp.sum(sq) * np.float32(0.5 / D), axes)

    grads = {n: [None] * p[n].shape[0] for n in WEIGHTS}
    parts = [dict() for _ in range(depth)]
    after = None
    for l in reversed(range(depth)):
        m, j = l % n_mix, l // n_mix
        sh1, sc1, g1_, sh2, sc2, g2_ = mods[l]
        x_in, h1, y1, cache, xm, h2, z, a, y2, ybias = saved[l]
        ln2 = (small['ln2_g'][l][None], small['ln2_b'][l][None])
        if after is None:
            dr2, dy2, parts[l]['g2'], grads['ln2_g'][l], grads['ln2_b'][l], _ = _resid_ln_bwd("resid_ln2_bwd", alpha, dx, xm, y2, g2_, *ln2)
        else:
            (dr2, dy2, parts[l]['g2'], parts[l + 1]['sc1'], parts[l + 1]['sh1'], grads['ln2_g'][l], grads['ln2_b'][l], _) = _resid_ln_bwd(
                "resid_ln2_bwd", alpha, after[0], xm, y2, g2_, *ln2, then=after[1:])
        grads['ffn_w_out'][l] = _mm("ffn_dwout", a, two(dy2), ta=True)
        dz = _ffn_out_bwd_act("ffn_da", two(dy2), W['ffn_w_out'][l], *z)
        grads['ffn_w_in'][l] = jnp.concatenate([_mm("ffn_dwin", two(h2), t, ta=True) for t in dz], axis=1)
        dh2 = three(_ffn_in_bwd("ffn_dh", *dz, W['ffn_w_in'][l]))
        (dr1, dy1, parts[l]['g1'], parts[l]['sc2'], parts[l]['sh2'], grads['ln1_g'][l], grads['ln1_b'][l], dyb) = _resid_ln_bwd(
            "resid_ln1_bwd", alpha, dr2, x_in, y1, g1_, small['ln1_g'][l][None], small['ln1_b'][l][None], ybias, then=(dh2, sc2))
        dh1, mg = mixers[m][1](j, h1, dy1, cache)
        if m == 3:
            mg['cv_b_out'] = dyb
        for n, gval in mg.items():
            grads[n][j] = gval
        after = (dr1, three(dh1), sc1)
    grad_x, parts[0]['sc1'], parts[0]['sh1'] = _modulate_bwd("mod1_bwd", alpha, after[1], after[0], saved[0][0], after[2])
    dmod = [jnp.concatenate([pt[k] for k in ('sh1', 'sc1', 'g1', 'sh2', 'sc2', 'g2')], axis=-1)[:, 0, :] for pt in parts]
    dmod = jnp.stack(dmod)
    grads['mod_b'] = [jnp.sum(dmod[l], axis=0) for l in range(depth)]
    full_shape = {n: tuple(t.shape) for n, t in small.items()}

    small_names = SMALL_REPL + SMALL_SPLIT
    small_parts = [jnp.stack([gv.reshape(full_shape[n][1:]) for gv in grads[n]]) for n in small_names]
    pack_a = _pack([dmod], LANES, F32, SUBLANES)
    pack_b = _pack(small_parts, LANES, F32, SUBLANES)
    g2 = _all_gather8("ag_grads_small", [jnp.concatenate([pack_a, pack_b], axis=0)])[0]
    rows_a = pack_a.shape[0]
    dmod_all = g2[:, :rows_a].reshape(N_DEV, -1)[:, :dmod.size].reshape(N_DEV, depth, B, 6 * D)
    dmod_all = dmod_all.transpose(1, 0, 2, 3).reshape(depth, n_seq, 6 * D)
    small_sum = _sum8("sum_grads_small", g2[:, rows_a:]).reshape(-1)
    g_small = dict(zip(small_names, _unpack(small_sum, [full_shape[n] for n in small_names])))
    for n in SMALL_SPLIT:
        w = p[n].shape[-1]
        g_small[n] = lax.dynamic_slice_in_dim(g_small[n], my_q * w, w, axis=g_small[n].ndim - 1)

    dm_cols = lax.dynamic_slice_in_dim(dmod_all, my_q * mod_cols, mod_cols, axis=2)
    dm_cols = jnp.pad(dm_cols, ((0, 0), (0, seq_pad - n_seq), (0, 0)))
    g_mod_w = jnp.stack([_mm(f"mod_dw{l}", c_act, dm_cols[l], ta=True) for l in range(depth)])

    keep, give = [], []
    for n, s, hr in zip(big_names, shard_shapes, half_rows):
        gfull = jnp.stack(grads[n])
        g4 = jnp.stack(jnp.split(gfull, N_CHIPS, axis=BIG[n])).reshape(N_CHIPS, 2 * hr, s[2])
        keep.append(lax.dynamic_slice_in_dim(g4, cc * hr, hr, axis=1))
        give.append(lax.dynamic_slice_in_dim(g4, (1 - cc) * hr, hr, axis=1))
    got = _sibling_exchange("rs_sibling", give)
    chip_sum = [_rowwise("rs_add_sibling", lambda r, bv, cv: ([r[0] + r[1]], [], []),
                         [a.reshape(1, -1, a.shape[2]), b.reshape(1, -1, a.shape[2])],
                         out_rows=[(a.shape[2], BF16)], tm=512)[0].reshape(a.shape) for a, b in zip(keep, got)]
    from_chips = _chip_all_to_all("rs_chips", chip_sum)
    half_sum = [_rowwise("rs_add_chips", lambda r, bv, cv: ([((r[0] + r[1]) + r[2]) + r[3]], [], []),
                         [t[q][None] for q in range(N_CHIPS)], out_rows=[(t.shape[2], F32)], tm=512)[0][0]
                for t in from_chips]
    other = _sibling_exchange("rs_share", half_sum)
    g_big = {n: jnp.concatenate([jnp.where(cc == 0, a, b), jnp.where(cc == 0, b, a)], axis=0).reshape(s)
             for n, s, a, b in zip(big_names, shard_shapes, half_sum, other)}

    g_out = {**g_small, **g_big, 'mod_w': g_mod_w}
    upd = {n: _adamw("adamw_" + n, p[n], g_out[n], p['m_' + n], p['v_' + n]) for n in WEIGHTS}
    return (loss, grad_x, *[g_out[n] for n in WEIGHTS], *[upd[n][0] for n in WEIGHTS],
            *[upd[n][1] for n in WEIGHTS], *[upd[n][2] for n in WEIGHTS])
```

```python
import math

import jax
import jax.numpy as jnp
import numpy as np
from jax import lax
from jax.experimental import pallas as pl
from jax.experimental.pallas import tpu as pltpu

F32, BF16 = jnp.float32, jnp.bfloat16

HEAD_DIM = 64
CONV_WIDTH = 31
LN_EPS = 1e-5
NEG_INF = -1e30
ADAM_LR, ADAM_B1, ADAM_B2, ADAM_EPS, ADAM_WD, ADAM_STEP = 0.001, 0.9, 0.999, 1e-08, 0.01, 10

LANES = 128
SUBLANES = 8
VMEM_LIMIT_BYTES = 56 * 1024 * 1024
N_CHIPS = 4
N_DEV = 8

WEIGHTS = ['mod_w', 'mod_b', 'ln1_g', 'ln1_b', 'ln2_g', 'ln2_b', 'ffn_w_in', 'ffn_w_out', 'gm_w_in', 'gm_b_in',
           'gm_ln_g', 'gm_ln_b', 'gm_w_s', 'gm_b_s', 'gm_w_out', 'fox_w_in', 'fox_b_f', 'fox_w_out', 'sb_w_in',
           'sb_w_out', 'cv_w_in', 'cv_b_in', 'cv_dw', 'cv_dw_b', 'cv_ln_g', 'cv_ln_b', 'cv_w_out', 'cv_b_out']
ARGS = ['x', 'c'] + WEIGHTS + ['loss_target'] + ['m_' + n for n in WEIGHTS] + ['v_' + n for n in WEIGHTS]
BIG = {'ffn_w_in': 2, 'ffn_w_out': 1, 'gm_w_in': 2, 'gm_w_out': 1, 'fox_w_in': 2, 'fox_w_out': 1,
       'sb_w_in': 2, 'sb_w_out': 1, 'cv_w_in': 2, 'cv_w_out': 1}
SMALL_SPLIT = ['cv_b_in', 'cv_dw', 'cv_dw_b', 'cv_ln_g', 'cv_ln_b', 'cv_b_out']
SMALL_REPL = ['mod_b', 'ln1_g', 'ln1_b', 'ln2_g', 'ln2_b', 'gm_b_in', 'gm_ln_g', 'gm_ln_b', 'gm_w_s', 'gm_b_s', 'fox_b_f']
PACK_COLS = 1024


_names_used = {}


def _unique(name):
    k = _names_used.get(name, 0)
    _names_used[name] = k + 1
    return name if k == 0 else f"{name}_{k}"


def _params(sem):
    return pltpu.CompilerParams(dimension_semantics=sem, vmem_limit_bytes=VMEM_LIMIT_BYTES)


def _pick(dim, pref, mult=LANES):
    if dim <= pref:
        return dim
    best = 0
    for t in range(mult, pref + 1, mult):
        if dim % t == 0:
            best = t
    assert best, (dim, pref)
    return best


def _mesh_pos():
    return lax.axis_index("x"), lax.axis_index("y"), lax.axis_index("c")


AG_COPIES = 7
A2A_COPIES = 3


def _comm_call(name, body, blks, out_shapes, n_sems):
    n = len(blks)
    hbm = pl.BlockSpec(memory_space=pl.ANY)
    return pl.pallas_call(
        body, name=_unique(name), out_shape=out_shapes, in_specs=[hbm] * n, out_specs=[hbm] * n,
        scratch_shapes=[pltpu.SemaphoreType.DMA((n_sems * n,)), pltpu.SemaphoreType.DMA((n_sems * n,)),
                        pltpu.SemaphoreType.DMA((n,))],
    )(*blks)


def _all_gather8(name, blks):
    n = len(blks)

    def body(*refs):
        x_refs, out_refs, (send_sems, recv_sems, local_sems) = refs[:n], refs[n:2 * n], refs[2 * n:]
        x, y, c = _mesh_pos()
        me, sibling = (x, y, c), (x, y, 1 - c)
        chips = [(1 - x, y), (x, 1 - y), (1 - x, 1 - y)]

        def copy(a, k, block, to, from_input=False):
            px, py, pc = block
            slot = out_refs[a].at[4 * px + 2 * py + pc]
            return pltpu.make_async_remote_copy(
                src_ref=x_refs[a] if from_input else slot, dst_ref=slot,
                send_sem=send_sems.at[AG_COPIES * a + k], recv_sem=recv_sems.at[AG_COPIES * a + k],
                device_id=to, device_id_type=pl.DeviceIdType.MESH)

        local, sent = [], []
        for a in range(n):
            local.append(pltpu.make_async_copy(x_refs[a], out_refs[a].at[4 * x + 2 * y + c], local_sems.at[a]))
            local[-1].start()
            first = [copy(a, 0, me, sibling, True)] + [copy(a, 1 + j, me, (*chip, c), True) for j, chip in enumerate(chips)]
            for cp in first:
                cp.start()
            sent += first
        for a in range(n):
            for j, chip in enumerate(chips):
                copy(a, 1 + j, (*chip, c), me).wait_recv()
                sent.append(copy(a, 4 + j, (*chip, c), sibling))
                sent[-1].start()
        for a in range(n):
            copy(a, 0, sibling, me).wait_recv()
            for j, chip in enumerate(chips):
                copy(a, 4 + j, (*chip, 1 - c), me).wait_recv()
        for cp in sent:
            cp.wait_send()
        for cp in local:
            cp.wait()

    return _comm_call(name, body, blks, [jax.ShapeDtypeStruct((N_DEV,) + b.shape, b.dtype) for b in blks], AG_COPIES)


def _sibling_exchange(name, blks):
    n = len(blks)

    def body(*refs):
        x_refs, out_refs, (send_sems, recv_sems, _) = refs[:n], refs[n:2 * n], refs[2 * n:]
        x, y, c = _mesh_pos()
        cps = [pltpu.make_async_remote_copy(src_ref=x_refs[a], dst_ref=out_refs[a], send_sem=send_sems.at[a],
                                            recv_sem=recv_sems.at[a], device_id=(x, y, 1 - c),
                                            device_id_type=pl.DeviceIdType.MESH) for a in range(n)]
        for cp in cps:
            cp.start()
        for cp in cps:
            cp.wait()

    return _comm_call(name, body, blks, [jax.ShapeDtypeStruct(b.shape, b.dtype) for b in blks], 1)


def _chip_all_to_all(name, blks):
    n = len(blks)

    def body(*refs):
        x_refs, out_refs, (send_sems, recv_sems, local_sems) = refs[:n], refs[n:2 * n], refs[2 * n:]
        x, y, c = _mesh_pos()
        chips = [(1 - x, y), (x, 1 - y), (1 - x, 1 - y)]
        my_q = 2 * x + y

        def copy(a, j, src_q, dst_q):
            px, py = chips[j]
            return pltpu.make_async_remote_copy(
                src_ref=x_refs[a].at[src_q], dst_ref=out_refs[a].at[dst_q],
                send_sem=send_sems.at[A2A_COPIES * a + j], recv_sem=recv_sems.at[A2A_COPIES * a + j],
                device_id=(px, py, c), device_id_type=pl.DeviceIdType.MESH)

        local, sent = [], []
        for a in range(n):
            local.append(pltpu.make_async_copy(x_refs[a].at[my_q], out_refs[a].at[my_q], local_sems.at[a]))
            local[-1].start()
            sent += [copy(a, j, 2 * px + py, my_q) for j, (px, py) in enumerate(chips)]
            for cp in sent[-A2A_COPIES:]:
                cp.start()
        for a in range(n):
            for j, (px, py) in enumerate(chips):
                copy(a, j, my_q, 2 * px + py).wait_recv()
        for cp in sent:
            cp.wait_send()
        for cp in local:
            cp.wait()

    return _comm_call(name, body, blks, [jax.ShapeDtypeStruct(b.shape, b.dtype) for b in blks], A2A_COPIES)


def _rowwise(name, fn, rows, bvecs=(), cvecs=(), out_rows=(), out_bsums=(), out_tsums=(), tm=256):
    B, S = rows[0].shape[:2]
    tm = _pick(S, tm, SUBLANES)
    n_r, n_b, n_c = len(rows), len(bvecs), len(cvecs)
    n_or, n_ob = len(out_rows), len(out_bsums)

    def body(*refs):
        ins, outs = refs[:n_r + n_b + n_c], refs[n_r + n_b + n_c:]
        r = [ref[0] for ref in ins[:n_r]]
        bv = [ref[0] for ref in ins[n_r:n_r + n_b]]
        cv = [ref[...] for ref in ins[n_r + n_b:]]
        o_rows, o_bsums, o_tsums = fn(r, bv, cv)
        b, i = pl.program_id(0), pl.program_id(1)
        for ref, val in zip(outs[:n_or], o_rows):
            ref[0] = val.astype(ref.dtype)
        for ref, val in zip(outs[n_or:n_or + n_ob], o_bsums):
            @pl.when(i == 0)
            def _(ref=ref, val=val):
                ref[0] = val

            @pl.when(i > 0)
            def _(ref=ref, val=val):
                ref[0] += val
        for ref, val in zip(outs[n_or + n_ob:], o_tsums):
            first = jnp.logical_and(b == 0, i == 0)

            @pl.when(first)
            def _(ref=ref, val=val):
                ref[...] = val

            @pl.when(jnp.logical_not(first))
            def _(ref=ref, val=val):
                ref[...] += val

    in_specs = [pl.BlockSpec((1, tm, a.shape[2]), lambda b, i: (b, i, 0)) for a in rows]
    in_specs += [pl.BlockSpec((1, 1, a.shape[2]), lambda b, i: (b, 0, 0)) for a in bvecs]
    in_specs += [pl.BlockSpec((1, a.shape[1]), lambda b, i: (0, 0)) for a in cvecs]
    out_shape = [jax.ShapeDtypeStruct((B, S, cdim), dt) for cdim, dt in out_rows]
    out_specs = [pl.BlockSpec((1, tm, cdim), lambda b, i: (b, i, 0)) for cdim, _ in out_rows]
    out_shape += [jax.ShapeDtypeStruct((B, 1, cdim), F32) for cdim in out_bsums]
    out_specs += [pl.BlockSpec((1, 1, cdim), lambda b, i: (b, 0, 0)) for cdim in out_bsums]
    out_shape += [jax.ShapeDtypeStruct((1, cdim), F32) for cdim in out_tsums]
    out_specs += [pl.BlockSpec((1, cdim), lambda b, i: (0, 0)) for cdim in out_tsums]
    sem = ("arbitrary", "arbitrary") if out_tsums else ("parallel", "arbitrary")
    res = pl.pallas_call(body, name=_unique(name), grid=(B, S // tm), in_specs=in_specs, out_specs=out_specs,
                         out_shape=out_shape, compiler_params=_params(sem))(*rows, *bvecs, *cvecs)
    return list(res)


MM_TILE = 1536
MM_ROWS = 512
MM_WEIGHT_TILE_BYTES = 12 * 1024 * 1024


def _mm(name, a, b, ta=False, tb=False, out_dtype=F32):
    M, K = (a.shape[1], a.shape[0]) if ta else a.shape
    N = b.shape[0] if tb else b.shape[1]
    assert (b.shape[1] if tb else b.shape[0]) == K, (a.shape, b.shape, ta, tb)
    tn = _pick(N, MM_TILE)
    if ta:
        tm, tk = _pick(M, MM_TILE), _pick(K, 2 * MM_ROWS, LANES if tb else SUBLANES)
    else:
        tm = _pick(M, MM_ROWS, SUBLANES)
        tk = K if K * tn * 2 <= MM_WEIGHT_TILE_BYTES else _pick(K, MM_TILE)
    nk = K // tk
    dims = (((0 if ta else 1,), (1 if tb else 0,)), ((), ()))

    def body(a_ref, b_ref, o_ref, acc_ref):
        k = pl.program_id(2)
        p = lax.dot_general(a_ref[...].astype(BF16), b_ref[...].astype(BF16), dims, preferred_element_type=F32)
        if nk == 1:
            o_ref[...] = p.astype(o_ref.dtype)
        else:
            @pl.when(k == 0)
            def _():
                acc_ref[...] = p

            @pl.when(k > 0)
            def _():
                acc_ref[...] += p

            @pl.when(k == nk - 1)
            def _():
                o_ref[...] = acc_ref[...].astype(o_ref.dtype)

    a_spec = pl.BlockSpec((tk, tm), lambda j, i, k: (k, i)) if ta else pl.BlockSpec((tm, tk), lambda j, i, k: (i, k))
    b_spec = pl.BlockSpec((tn, tk), lambda j, i, k: (j, k)) if tb else pl.BlockSpec((tk, tn), lambda j, i, k: (k, j))
    return pl.pallas_call(
        body, name=_unique(name), grid=(N // tn, M // tm, nk), in_specs=[a_spec, b_spec],
        out_specs=pl.BlockSpec((tm, tn), lambda j, i, k: (i, j)),
        out_shape=jax.ShapeDtypeStruct((M, N), out_dtype),
        scratch_shapes=[pltpu.VMEM((tm, tn) if nk > 1 else (SUBLANES, LANES), F32)],
        compiler_params=_params(("parallel", "parallel", "arbitrary")))(a, b)


def _silu(x):
    return x * _sigmoid(x)


def _sigmoid(x):
    return 1.0 / (1.0 + jnp.exp(-x))


def _dsilu(x):
    s = _sigmoid(x)
    return s * (1.0 + x * (1.0 - s))


def _gelu(x):
    return 0.5 * x * (1.0 + lax.erf(x * np.float32(math.sqrt(0.5))))


def _dgelu(x):
    cdf = 0.5 * (1.0 + lax.erf(x * np.float32(math.sqrt(0.5))))
    pdf = jnp.exp(-0.5 * x * x) * np.float32(1.0 / math.sqrt(2.0 * math.pi))
    return cdf + x * pdf


def _ln_stats(r):
    mu = jnp.mean(r, axis=-1, keepdims=True)
    xc = r - mu
    var = jnp.mean(xc * xc, axis=-1, keepdims=True)
    rstd = lax.rsqrt(var + LN_EPS)
    return xc * rstd, rstd


def _ln_bwd(dxhat, xhat, rstd):
    m1 = jnp.mean(dxhat, axis=-1, keepdims=True)
    m2 = jnp.mean(dxhat * xhat, axis=-1, keepdims=True)
    return rstd * (dxhat - m1 - xhat * m2)


def _csum(v):
    return jnp.sum(v, axis=0, keepdims=True)


def _split_dot(x, m01, lhs01=False, terms=2):
    acc, rem = None, x
    for _ in range(terms):
        part = rem.astype(BF16)
        rem = rem - part.astype(F32)
        d = jnp.dot(m01, part, preferred_element_type=F32) if lhs01 else jnp.dot(part, m01, preferred_element_type=F32)
        acc = d if acc is None else acc + d
    return acc


def _iota2(shape, dim):
    return lax.broadcasted_iota(jnp.int32, shape, dim)


def _modulate(name, x, sc, sh):
    D = x.shape[2]
    return _rowwise(name, lambda r, bv, cv: ([r[0] * (1.0 + bv[0]) + bv[1]], [], []),
                    [x], [sc, sh], [], out_rows=[(D, BF16)])[0]


def _resid_ln(name, alpha, x, y, g, ln_g, ln_b, ybias=None, then=None):
    D = x.shape[2]

    def fn(r, bv, cv):
        yy = r[1] if ybias is None else r[1] + cv[2]
        xhat, _ = _ln_stats(alpha * r[0] + (1.0 + bv[0]) * yy)
        xn = xhat * cv[0] + cv[1]
        return [xn] + ([xn * (1.0 + bv[1]) + bv[2]] if then else []), [], []
    cvecs = [ln_g, ln_b] + ([] if ybias is None else [ybias])
    return _rowwise(name, fn, [x, y], [g] + list(then or ()), cvecs, out_rows=[(D, F32)] + ([(D, BF16)] if then else []))


def _resid_ln_bwd(name, alpha, dxn, x, y, g, ln_g, ln_b, ybias=None, then=None):
    D = x.shape[2]

    def fn(r, bv, cv):
        yy = r[2] if ybias is None else r[2] + cv[2]
        xhat, rstd = _ln_stats(alpha * r[1] + (1.0 + bv[0]) * yy)
        d, sums = r[0], []
        if then:
            d = alpha * d + r[3] * (1.0 + bv[1])
            sums = [_csum(r[3] * (xhat * cv[0] + cv[1])), _csum(r[3])]
        dr = _ln_bwd(d * cv[0], xhat, rstd)
        dy = (1.0 + bv[0]) * dr
        return [dr, dy], [_csum(dr * yy)] + sums, [_csum(d * xhat), _csum(d), _csum(dy)]
    cvecs = [ln_g, ln_b] + ([] if ybias is None else [ybias])
    rows, bvecs = [dxn, x, y] + ([then[0]] if then else []), [g] + ([then[1]] if then else [])
    return _rowwise(name, fn, rows, bvecs, cvecs, out_rows=[(D, F32), (D, BF16)], out_bsums=[D] * (3 if then else 1),
                    out_tsums=[D, D, D])


def _modulate_bwd(name, alpha, dh, dr, x, sc):
    D = x.shape[2]

    def fn(r, bv, cv):
        return [alpha * r[1] + r[0] * (1.0 + bv[0])], [_csum(r[0] * r[2]), _csum(r[0])], []
    return _rowwise(name, fn, [dh, dr, x], [sc], [], out_rows=[(D, F32)], out_bsums=[D, D])


def _loss_head(name, y, target):
    D = y.shape[2]

    def fn(r, bv, cv):
        e = r[0] - r[1]
        return [e * np.float32(1.0 / D)], [_csum(e * e)], []
    return _rowwise(name, fn, [y, target], [], [], out_rows=[(D, F32)], out_bsums=[D])


def _ffn_in_act(name, h, w_in):
    M, K = h.shape
    Hd = w_in.shape[1] // 2
    tm, tn = _pick(M, MM_ROWS, SUBLANES), _pick(Hd, MM_TILE)
    nj = Hd // tn

    def body(h_ref, wg_ref, wu_ref, zg_ref, zu_ref, act_ref):
        hv = h_ref[...].astype(BF16)
        zg = jnp.dot(hv, wg_ref[...].astype(BF16), preferred_element_type=F32)
        zu = jnp.dot(hv, wu_ref[...].astype(BF16), preferred_element_type=F32)
        zg_ref[...] = zg
        zu_ref[...] = zu
        act_ref[...] = (_silu(zg) * zu).astype(act_ref.dtype)

    tile = pl.BlockSpec((tm, tn), lambda j, i: (i, j))
    return pl.pallas_call(
        body, name=_unique(name), grid=(nj, M // tm),
        in_specs=[pl.BlockSpec((tm, K), lambda j, i: (i, 0)), pl.BlockSpec((K, tn), lambda j, i: (0, j)),
                  pl.BlockSpec((K, tn), lambda j, i: (0, j + nj))],
        out_specs=[tile, tile, tile],
        out_shape=[jax.ShapeDtypeStruct((M, Hd), F32), jax.ShapeDtypeStruct((M, Hd), F32), jax.ShapeDtypeStruct((M, Hd), BF16)],
        compiler_params=_params(("parallel", "parallel")))(h, w_in, w_in)


def _ffn_out_bwd_act(name, dy, w_out, zg, zu):
    M, K = dy.shape
    Hd = w_out.shape[0]
    tm, tn = _pick(M, MM_ROWS, SUBLANES), _pick(Hd, MM_TILE)

    def body(dy_ref, w_ref, zg_ref, zu_ref, dzg_ref, dzu_ref):
        da = lax.dot_general(dy_ref[...].astype(BF16), w_ref[...].astype(BF16), _NT, preferred_element_type=F32)
        gg = zg_ref[...]
        dzg_ref[...] = (da * zu_ref[...] * _dsilu(gg)).astype(dzg_ref.dtype)
        dzu_ref[...] = (da * _silu(gg)).astype(dzu_ref.dtype)

    tile = pl.BlockSpec((tm, tn), lambda j, i: (i, j))
    return pl.pallas_call(
        body, name=_unique(name), grid=(Hd // tn, M // tm),
        in_specs=[pl.BlockSpec((tm, K), lambda j, i: (i, 0)), pl.BlockSpec((tn, K), lambda j, i: (j, 0)), tile, tile],
        out_specs=[tile, tile],
        out_shape=[jax.ShapeDtypeStruct((M, Hd), BF16), jax.ShapeDtypeStruct((M, Hd), BF16)],
        compiler_params=_params(("parallel", "parallel")))(dy, w_out, zg, zu)


def _ffn_in_bwd(name, dzg, dzu, w_in):
    M, Hd = dzg.shape
    N = w_in.shape[0]
    tm, tn = _pick(M, MM_ROWS, SUBLANES), _pick(N, MM_TILE)

    def body(g_ref, u_ref, wg_ref, wu_ref, o_ref):
        o_ref[...] = (lax.dot_general(g_ref[...], wg_ref[...].astype(BF16), _NT, preferred_element_type=F32)
                      + lax.dot_general(u_ref[...], wu_ref[...].astype(BF16), _NT, preferred_element_type=F32))

    rows = pl.BlockSpec((tm, Hd), lambda j, i: (i, 0))
    return pl.pallas_call(
        body, name=_unique(name), grid=(N // tn, M // tm),
        in_specs=[rows, rows, pl.BlockSpec((tn, Hd), lambda j, i: (j, 0)), pl.BlockSpec((tn, Hd), lambda j, i: (j, 1))],
        out_specs=pl.BlockSpec((tm, tn), lambda j, i: (i, j)),
        out_shape=jax.ShapeDtypeStruct((M, N), F32),
        compiler_params=_params(("parallel", "parallel")))(dzg, dzu, w_in, w_in)


def _gm_act(name, zin, b_in, ln_g, ln_b):
    W = zin.shape[2] // 2

    def fn(r, bv, cv):
        z = _gelu(r[0] + cv[0])
        vhat, _ = _ln_stats(z[:, W:])
        return [z[:, :W], vhat * cv[1] + cv[2]], [], []
    return _rowwise(name, fn, [zin], [], [b_in, ln_g, ln_b], out_rows=[(W, F32), (W, BF16)])


def _gm_act_bwd(name, zin, du, dvn, b_in, ln_g):
    W = zin.shape[2] // 2

    def fn(r, bv, cv):
        zz = r[0] + cv[0]
        z = _gelu(zz)
        vhat, rstd = _ln_stats(z[:, W:])
        dv = _ln_bwd(r[2] * cv[1], vhat, rstd)
        dzin = jnp.concatenate([r[1], dv], axis=1) * _dgelu(zz)
        return [dzin], [], [_csum(dzin), _csum(r[2] * vhat), _csum(r[2])]
    return _rowwise(name, fn, [zin, du, dvn], [], [b_in, ln_g], out_rows=[(2 * W, BF16)], out_tsums=[2 * W, W, W])


def _gm_causal_w(ws_ref, g):
    T = ws_ref.shape[1]
    return jnp.where(_iota2((T, T), 1) <= _iota2((T, T), 0), ws_ref[g], 0.0).astype(BF16)


def _gm_spatial(name, u, vn, w_s, b_sT):
    B, S, W = u.shape
    G, T = w_s.shape[0], w_s.shape[1]
    assert W == G * T, "a head group is as wide as a chunk is long"

    def body(u_ref, vn_ref, ws_ref, bs_ref, y_ref):
        for g in range(G):
            cs = slice(g * T, (g + 1) * T)
            sv = jnp.dot(_gm_causal_w(ws_ref, g), vn_ref[0, :, cs], preferred_element_type=F32) + bs_ref[:, g:g + 1]
            y_ref[0, :, cs] = (u_ref[0, :, cs] * sv).astype(y_ref.dtype)

    row = pl.BlockSpec((1, T, W), lambda b, i: (b, i, 0))
    return pl.pallas_call(
        body, name=_unique(name), grid=(B, S // T),
        in_specs=[row, row, pl.BlockSpec((G, T, T), lambda b, i: (0, 0, 0)), pl.BlockSpec((T, G), lambda b, i: (0, 0))],
        out_specs=row, out_shape=jax.ShapeDtypeStruct((B, S, W), BF16),
        compiler_params=_params(("parallel", "parallel")))(u, vn, w_s, b_sT)


def _gm_spatial_bwd(name, dyg, u, vn, w_s, b_sT):
    B, S, W = u.shape
    G, T = w_s.shape[0], w_s.shape[1]
    assert W == G * T, "a head group is as wide as a chunk is long"

    def body(dy_ref, u_ref, vn_ref, ws_ref, bs_ref, du_ref, dvn_ref, dws_ref, dbs_ref):
        first = jnp.logical_and(pl.program_id(0) == 0, pl.program_id(1) == 0)

        @pl.when(first)
        def _():
            dws_ref[...] = jnp.zeros_like(dws_ref)
            dbs_ref[...] = jnp.zeros_like(dbs_ref)

        tril = _iota2((T, T), 1) <= _iota2((T, T), 0)
        for g in range(G):
            cs = slice(g * T, (g + 1) * T)
            wm = _gm_causal_w(ws_ref, g)
            vng = vn_ref[0, :, cs]
            sv = jnp.dot(wm, vng, preferred_element_type=F32) + bs_ref[:, g:g + 1]
            dy = dy_ref[0, :, cs]
            du_ref[0, :, cs] = dy * sv
            dsv = dy * u_ref[0, :, cs]
            dsv16 = dsv.astype(BF16)
            dvn_ref[0, :, cs] = lax.dot_general(wm, dsv16, (((0,), (0,)), ((), ())), preferred_element_type=F32)
            dw = lax.dot_general(dsv16, vng, (((1,), (1,)), ((), ())), preferred_element_type=F32)
            dws_ref[g] += jnp.where(tril, dw, 0.0)
            dbs_ref[:, g:g + 1] += jnp.sum(dsv, axis=1, keepdims=True)

    row = pl.BlockSpec((1, T, W), lambda b, i: (b, i, 0))
    return pl.pallas_call(
        body, name=_unique(name), grid=(B, S // T),
        in_specs=[row, row, row, pl.BlockSpec((G, T, T), lambda b, i: (0, 0, 0)), pl.BlockSpec((T, G), lambda b, i: (0, 0))],
        out_specs=[row, row, pl.BlockSpec((G, T, T), lambda b, i: (0, 0, 0)), pl.BlockSpec((T, G), lambda b, i: (0, 0))],
        out_shape=[jax.ShapeDtypeStruct((B, S, W), F32), jax.ShapeDtypeStruct((B, S, W), F32),
                   jax.ShapeDtypeStruct((G, T, T), F32), jax.ShapeDtypeStruct((T, G), F32)],
        compiler_params=_params(("arbitrary", "arbitrary")))(dyg, u, vn, w_s, b_sT)


def _cv_glu(name, pw, b_in):
    W = pw.shape[2] // 2

    def fn(r, bv, cv):
        z = r[0] + cv[0]
        return [z[:, :W] * _sigmoid(z[:, W:])], [], []
    return _rowwise(name, fn, [pw], [], [b_in], out_rows=[(W, F32)])[0]


def _cv_glu_bwd(name, pw, dyg, b_in):
    W = pw.shape[2] // 2

    def fn(r, bv, cv):
        z = r[0] + cv[0]
        a, s = z[:, :W], _sigmoid(z[:, W:])
        dpw = jnp.concatenate([r[1] * s, r[1] * a * s * (1.0 - s)], axis=1)
        return [dpw], [], [_csum(dpw)]
    return _rowwise(name, fn, [pw, dyg], [], [b_in], out_rows=[(2 * W, BF16)], out_tsums=[2 * W])


def _cv_ln_act(name, yc, ln_g, ln_b):
    D = yc.shape[2]

    def fn(r, bv, cv):
        xhat, _ = _ln_stats(r[0])
        return [_silu(xhat * cv[0] + cv[1])], [], []
    return _rowwise(name, fn, [yc], [], [ln_g, ln_b], out_rows=[(D, BF16)])[0]


def _cv_ln_act_bwd(name, yc, dys, ln_g, ln_b):
    D = yc.shape[2]

    def fn(r, bv, cv):
        xhat, rstd = _ln_stats(r[0])
        dyn = r[1] * _dsilu(xhat * cv[0] + cv[1])
        dyc = _ln_bwd(dyn * cv[0], xhat, rstd)
        return [dyc], [], [_csum(dyn * xhat), _csum(dyn), _csum(dyc)]
    return _rowwise(name, fn, [yc, dys], [], [ln_g, ln_b], out_rows=[(D, F32)], out_tsums=[D, D, D])


CONV_HALO = 32
CONV_TS, CONV_TC = 256, 128


def _dwconv(name, y, dw, dw_b):
    B, S, D = y.shape
    ts, tc, halo, K = _pick(S, CONV_TS, SUBLANES), _pick(D, CONV_TC), CONV_HALO, CONV_WIDTH

    def body(cur_ref, prev_ref, dw_ref, b_ref, o_ref, buf):
        i = pl.program_id(1)
        buf[pl.ds(0, halo), :] = jnp.where(i > 0, prev_ref[0, pl.ds(ts - halo, halo), :], 0.0)
        buf[pl.ds(halo, ts), :] = cur_ref[0]
        acc = jnp.zeros((ts, tc), F32) + b_ref[...]
        for k in range(K):
            acc = acc + dw_ref[k:k + 1, :] * buf[pl.ds(halo - (K - 1) + k, ts), :]
        o_ref[0] = acc

    return pl.pallas_call(
        body, name=_unique(name), grid=(B, S // ts, D // tc),
        in_specs=[pl.BlockSpec((1, ts, tc), lambda b, i, j: (b, i, j)),
                  pl.BlockSpec((1, ts, tc), lambda b, i, j: (b, jnp.maximum(i - 1, 0), j)),
                  pl.BlockSpec((halo, tc), lambda b, i, j: (0, j)), pl.BlockSpec((1, tc), lambda b, i, j: (0, j))],
        out_specs=pl.BlockSpec((1, ts, tc), lambda b, i, j: (b, i, j)),
        out_shape=jax.ShapeDtypeStruct((B, S, D), F32),
        scratch_shapes=[pltpu.VMEM((halo + ts, tc), F32)],
        compiler_params=_params(("parallel", "parallel", "parallel")))(y, y, dw, dw_b)


def _dwconv_bwd(name, dyc, y, dw):
    B, S, D = y.shape
    ts, tc, halo, K = _pick(S, CONV_TS, SUBLANES), _pick(D, CONV_TC), CONV_HALO, CONV_WIDTH
    nt = S // ts

    def body(g_ref, gnext_ref, y_ref, yprev_ref, dw_ref, dy_ref, ddw_ref, gbuf, ybuf):
        b, i = pl.program_id(1), pl.program_id(2)
        first = jnp.logical_and(b == 0, i == 0)

        @pl.when(first)
        def _():
            ddw_ref[...] = jnp.zeros_like(ddw_ref)

        g = g_ref[0]
        gbuf[pl.ds(0, ts), :] = g
        gbuf[pl.ds(ts, halo), :] = jnp.where(i < nt - 1, gnext_ref[0, pl.ds(0, halo), :], 0.0)
        ybuf[pl.ds(0, halo), :] = jnp.where(i > 0, yprev_ref[0, pl.ds(ts - halo, halo), :], 0.0)
        ybuf[pl.ds(halo, ts), :] = y_ref[0]
        acc = jnp.zeros((ts, tc), F32)
        for k in range(K):
            acc = acc + dw_ref[k:k + 1, :] * gbuf[pl.ds(K - 1 - k, ts), :]
            ddw_ref[k:k + 1, :] += _csum(g * ybuf[pl.ds(halo - (K - 1) + k, ts), :])
        dy_ref[0] = acc

    tile = lambda f: pl.BlockSpec((1, ts, tc), f)
    return pl.pallas_call(
        body, name=_unique(name), grid=(D // tc, B, nt),
        in_specs=[tile(lambda j, b, i: (b, i, j)), tile(lambda j, b, i: (b, jnp.minimum(i + 1, nt - 1), j)),
                  tile(lambda j, b, i: (b, i, j)), tile(lambda j, b, i: (b, jnp.maximum(i - 1, 0), j)),
                  pl.BlockSpec((halo, tc), lambda j, b, i: (0, j))],
        out_specs=[tile(lambda j, b, i: (b, i, j)), pl.BlockSpec((halo, tc), lambda j, b, i: (0, j))],
        out_shape=[jax.ShapeDtypeStruct((B, S, D), F32), jax.ShapeDtypeStruct((halo, D), F32)],
        scratch_shapes=[pltpu.VMEM((ts + halo, tc), F32), pltpu.VMEM((halo + ts, tc), F32)],
        compiler_params=_params(("parallel", "arbitrary", "arbitrary")))(dyc, dyc, y, y, dw)


ATT_BLOCK = 128
FOX_TILE = (512, 1024)
SB_TILE = (256, 512)
ATT_PIECE_ROWS = 32
FOX_GATE_COLS = 128


def _att_tiles(S, tile):
    return _pick(S, tile[0], SUBLANES), _pick(S, tile[1], LANES)


def _pieces(T, TK):
    R = min(T, ATT_PIECE_ROWS)
    segs = [slice(c, c + LANES) for c in range(0, TK, LANES)]
    return [(slice(r, r + R), segs) for r in range(0, T, R)]


def _piece_keep(row0, col0, rs, cs, strict, lane_major_of=0):
    shape = (rs.stop - rs.start, cs.stop - cs.start)
    lane = _iota2(shape, 1)
    key = col0 + (lane * lane_major_of + cs.start // LANES if lane_major_of else cs.start + lane)
    qry = row0 + rs.start + _iota2(shape, 0)
    return key < qry if strict else key <= qry


def _causal_tiles(i, tq, tk):
    return (i * tq + tq + tk - 1) // tk


def _lane_major(t, tk):
    B, H, S, dh = t.shape
    return t.reshape(B, H, S // tk, LANES, tk // LANES, dh).swapaxes(3, 4).reshape(B, H, S, dh)


def _lane_major_inverse(t, tk):
    B, H, S, dh = t.shape
    return t.reshape(B, H, S // tk, tk // LANES, LANES, dh).swapaxes(3, 4).reshape(B, H, S, dh)


def _log_sigmoid(x):
    return jnp.minimum(x, 0.0) - jnp.log(1.0 + jnp.exp(-jnp.abs(x)))


def _fox_gate_cumsum(name, fl, b_f):
    B, S, C = fl.shape
    T = _pick(S, ATT_BLOCK, SUBLANES)

    def body(fl_ref, bf_ref, f_ref, carry):
        @pl.when(pl.program_id(1) == 0)
        def _():
            carry[...] = jnp.zeros_like(carry)
        lf = _log_sigmoid(fl_ref[0] + bf_ref[...])
        lower = (_iota2((T, T), 1) <= _iota2((T, T), 0)).astype(BF16)
        f = _split_dot(lf, lower, lhs01=True, terms=3) + carry[...]
        f_ref[0] = f
        carry[...] = f[T - 1:T, :]

    return pl.pallas_call(
        body, name=_unique(name), grid=(B, S // T),
        in_specs=[pl.BlockSpec((1, T, C), lambda b, i: (b, i, 0)), pl.BlockSpec((1, C), lambda b, i: (0, 0))],
        out_specs=pl.BlockSpec((1, T, C), lambda b, i: (b, i, 0)),
        out_shape=jax.ShapeDtypeStruct((B, S, C), F32),
        scratch_shapes=[pltpu.VMEM((1, C), F32)],
        compiler_params=_params(("arbitrary", "arbitrary")))(fl, b_f)


def _fox_gate_bwd(name, dF, fl, b_f, n_heads):
    B, S, C = fl.shape
    T = _pick(S, ATT_BLOCK, SUBLANES)
    nt = S // T

    def body(df_ref, fl_ref, bf_ref, dfl_ref, dbf_ref, carry):
        first = jnp.logical_and(pl.program_id(0) == 0, pl.program_id(1) == 0)

        @pl.when(pl.program_id(1) == 0)
        def _():
            carry[...] = jnp.zeros_like(carry)

        @pl.when(first)
        def _():
            dbf_ref[...] = jnp.zeros_like(dbf_ref)

        upper = (_iota2((T, T), 1) >= _iota2((T, T), 0)).astype(BF16)
        dlf = _split_dot(df_ref[0], upper, lhs01=True, terms=3) + carry[...]
        carry[...] = dlf[0:1, :]
        x = fl_ref[0] + bf_ref[...]
        dfl = jnp.where(_iota2((T, C), 1) < n_heads, dlf * _sigmoid(-x), 0.0)
        dfl_ref[0] = dfl
        dbf_ref[...] += _csum(dfl)

    rev = lambda b, i: (b, nt - 1 - i, 0)
    return pl.pallas_call(
        body, name=_unique(name), grid=(B, nt),
        in_specs=[pl.BlockSpec((1, T, C), rev), pl.BlockSpec((1, T, C), rev), pl.BlockSpec((1, C), lambda b, i: (0, 0))],
        out_specs=[pl.BlockSpec((1, T, C), rev), pl.BlockSpec((1, C), lambda b, i: (0, 0))],
        out_shape=[jax.ShapeDtypeStruct((B, S, C), F32), jax.ShapeDtypeStruct((1, C), F32)],
        scratch_shapes=[pltpu.VMEM((1, C), F32)],
        compiler_params=_params(("arbitrary", "arbitrary")))(dF, fl, b_f)


_NT = (((1,), (1,)), ((), ()))
_TN = (((0,), (0,)), ((), ()))


def _fox_fwd(name, q, k, v, fq, fk):
    B, H, S, dh = q.shape
    T, TK = _att_tiles(S, FOX_TILE)
    scale = np.float32(dh ** -0.5)

    pieces = _pieces(T, TK)

    def body(q_ref, k_ref, v_ref, fq_ref, fk_ref, o_ref, lse_ref, s_scr, p_scr):
        i = pl.program_id(2)
        qb = q_ref[0, 0]
        n_tiles = _causal_tiles(i, T, TK)

        def tile(j, carry, diag):
            m, l, acc = carry
            ks = pl.ds(pl.multiple_of(j * TK, TK), TK)
            s_scr[...] = lax.dot_general(qb, k_ref[0, 0, ks, :], _NT, preferred_element_type=F32) * scale
            fkj = fk_ref[0, 0, pl.ds(j, 1), :]
            m_new = []
            for rc, (rs, segs) in enumerate(pieces):
                fq_c, mx = fq_ref[0, 0, rs, :], None
                for cs in segs:
                    s = s_scr[rs, cs] + fq_c - fkj[:, cs]
                    if diag:
                        s = jnp.where(_piece_keep(i * T, j * TK, rs, cs, False), s, NEG_INF)
                    s_scr[rs, cs] = s
                    mx = s if mx is None else jnp.maximum(mx, s)
                m_new.append(jnp.maximum(m[rc], jnp.max(mx, axis=1, keepdims=True)))
            alpha, l_new = [], []
            for rc, (rs, segs) in enumerate(pieces):
                alpha.append(jnp.exp(m[rc] - m_new[rc]))
                psum = None
                for cs in segs:
                    p = jnp.exp(s_scr[rs, cs] - m_new[rc])
                    p_scr[rs, cs] = p.astype(BF16)
                    psum = p if psum is None else psum + p
                l_new.append(alpha[rc] * l[rc] + jnp.sum(psum, axis=1, keepdims=True))
            acc = jnp.concatenate(alpha, axis=0) * acc + jnp.dot(p_scr[...], v_ref[0, 0, ks, :], preferred_element_type=F32)
            return tuple(m_new), tuple(l_new), acc

        init = (tuple(jnp.full((rs.stop - rs.start, 1), NEG_INF, F32) for rs, _ in pieces),
                tuple(jnp.zeros((rs.stop - rs.start, 1), F32) for rs, _ in pieces), jnp.zeros((T, dh), F32))
        carry = lax.fori_loop(0, n_tiles - 1, lambda j, c: tile(j, c, False), init)
        m, l, acc = tile(n_tiles - 1, carry, True)
        m, l = jnp.concatenate(m, axis=0), jnp.concatenate(l, axis=0)
        o_ref[0, 0] = (acc / l).astype(o_ref.dtype)
        lse_ref[0, 0] = m + jnp.log(l)

    full = lambda w: pl.BlockSpec((1, 1, S, w), lambda b, h, i: (b, h, 0, 0))
    blk = lambda w: pl.BlockSpec((1, 1, T, w), lambda b, h, i: (b, h, i, 0))
    return pl.pallas_call(
        body, name=_unique(name), grid=(B, H, S // T),
        in_specs=[blk(dh), full(dh), full(dh), blk(1), pl.BlockSpec((1, 1, S // TK, TK), lambda b, h, i: (b, h, 0, 0))],
        out_specs=[blk(dh), blk(1)],
        out_shape=[jax.ShapeDtypeStruct((B, H, S, dh), BF16), jax.ShapeDtypeStruct((B, H, S, 1), F32)],
        scratch_shapes=[pltpu.VMEM((T, TK), F32), pltpu.VMEM((T, TK), BF16)],
        compiler_params=_params(("parallel", "parallel", "parallel")))(q, k, v, fq, fk)


def _fox_bwd(name, q, k, v, fq, fk, do, lse):
    B, H, S, dh = q.shape
    T, TK = _att_tiles(S, FOX_TILE)
    nt, nkt = S // T, S // TK
    scale = np.float32(dh ** -0.5)

    pieces = _pieces(T, TK)

    def body(q_ref, k_ref, v_ref, fq_ref, fk_ref, do_ref, lse_ref, dq_ref, dk_ref, dv_ref, dfk_ref,
             p_buf, dp_buf, s_scr, ds_scr, p16_scr, dk_acc, dv_acc):
        dk_acc[...] = jnp.zeros_like(dk_acc)
        dv_acc[...] = jnp.zeros_like(dv_acc)
        dfk_ref[...] = jnp.zeros_like(dfk_ref)

        def qloop(i, _):
            qs = pl.ds(pl.multiple_of(i * T, T), T)
            qb, dob16 = q_ref[0, 0, qs, :], do_ref[0, 0, qs, :].astype(BF16)
            n_tiles = _causal_tiles(i, T, TK)
            row_at = lambda rs: pl.ds(pl.multiple_of(i * T + rs.start, SUBLANES), rs.stop - rs.start)
            fq_c = [fq_ref[0, 0, row_at(rs), :] for rs, _ in pieces]
            lse_c = [lse_ref[0, 0, row_at(rs), :] for rs, _ in pieces]

            def sweep1(j, delta, diag):
                ks = pl.ds(pl.multiple_of(j * TK, TK), TK)
                s_scr[...] = lax.dot_general(qb, k_ref[0, 0, ks, :], _NT, preferred_element_type=F32) * scale
                dp_buf[j] = lax.dot_general(dob16, v_ref[0, 0, ks, :], _NT, preferred_element_type=F32)
                fkj = fk_ref[0, 0, pl.ds(j, 1), :]
                out = []
                for rc, (rs, segs) in enumerate(pieces):
                    pdp = None
                    for cs in segs:
                        p = jnp.exp(((s_scr[rs, cs] + fq_c[rc]) - fkj[:, cs]) - lse_c[rc])
                        if diag:
                            p = jnp.where(_piece_keep(i * T, j * TK, rs, cs, False), p, 0.0)
                        p_buf[j, rs, cs] = p
                        pdp = p * dp_buf[j, rs, cs] if pdp is None else pdp + p * dp_buf[j, rs, cs]
                    out.append(delta[rc] + jnp.sum(pdp, axis=1, keepdims=True))
                return tuple(out)

            zeros = tuple(jnp.zeros((rs.stop - rs.start, 1), F32) for rs, _ in pieces)
            delta = lax.fori_loop(0, n_tiles - 1, lambda j, d: sweep1(j, d, False), zeros)
            delta = sweep1(n_tiles - 1, delta, True)

            def sweep2(j, dq):
                ks = pl.ds(pl.multiple_of(j * TK, TK), TK)
                col = [None] * len(pieces[0][1])
                for rc, (rs, segs) in enumerate(pieces):
                    for sg, cs in enumerate(segs):
                        p = p_buf[j, rs, cs]
                        ds = p * (dp_buf[j, rs, cs] - delta[rc])
                        ds_scr[rs, cs] = ds.astype(BF16)
                        p16_scr[rs, cs] = p.astype(BF16)
                        col[sg] = ds if col[sg] is None else col[sg] + ds
                dfk_ref[0, 0, pl.ds(j, 1), :] -= jnp.concatenate([_csum(c) for c in col], axis=1)
                ds16 = ds_scr[...]
                dk_acc[ks, :] += lax.dot_general(ds16, qb, _TN, preferred_element_type=F32)
                dv_acc[ks, :] += lax.dot_general(p16_scr[...], dob16, _TN, preferred_element_type=F32)
                return dq + jnp.dot(ds16, k_ref[0, 0, ks, :], preferred_element_type=F32)

            dq_ref[0, 0, qs, :] = (lax.fori_loop(0, n_tiles, sweep2, jnp.zeros((T, dh), F32)) * scale).astype(dq_ref.dtype)
            return 0

        lax.fori_loop(0, nt, qloop, 0)
        dk_ref[0, 0] = (dk_acc[...] * scale).astype(dk_ref.dtype)
        dv_ref[0, 0] = dv_acc[...].astype(dv_ref.dtype)

    full = lambda w: pl.BlockSpec((1, 1, S, w), lambda b, h: (b, h, 0, 0))
    fks = pl.BlockSpec((1, 1, nkt, TK), lambda b, h: (b, h, 0, 0))
    return pl.pallas_call(
        body, name=_unique(name), grid=(B, H),
        in_specs=[full(dh), full(dh), full(dh), full(1), fks, full(dh), full(1)],
        out_specs=[full(dh), full(dh), full(dh), fks],
        out_shape=[jax.ShapeDtypeStruct((B, H, S, dh), BF16)] * 3 + [jax.ShapeDtypeStruct((B, H, nkt, TK), F32)],
        scratch_shapes=[pltpu.VMEM((nkt, T, TK), F32), pltpu.VMEM((nkt, T, TK), F32), pltpu.VMEM((T, TK), F32),
                        pltpu.VMEM((T, TK), BF16), pltpu.VMEM((T, TK), BF16), pltpu.VMEM((S, dh), F32), pltpu.VMEM((S, dh), F32)],
        compiler_params=_params(("parallel", "parallel")))(q, k, v, fq, fk, do, lse)


def _sb_terms(z, with_sigmoids=True):
    t = jnp.exp(-jnp.abs(z))
    lp = jnp.log(1.0 + t)
    lb, l1 = jnp.minimum(z, 0.0) - lp, jnp.minimum(-z, 0.0) - lp
    if not with_sigmoids:
        return lb, l1, None, None
    return lb, l1, jnp.exp(lb), jnp.exp(l1)


SCAN_RADIX = 4


def _lane_scan(x, reverse):
    lane = _iota2(x.shape, 1)
    y, d = x, 1
    while d < LANES:
        step = y
        for m in range(1, SCAN_RADIX):
            if m * d < LANES:
                if reverse:
                    step = step + jnp.where(lane + m * d < LANES, pltpu.roll(y, LANES - m * d, 1), 0.0)
                else:
                    step = step + jnp.where(lane >= m * d, pltpu.roll(y, m * d, 1), 0.0)
        y, d = step, d * SCAN_RADIX
    return y


def _chunk_scan(xs, reverse):
    n = len(xs)
    within, acc = [None] * n, None
    for s in (range(n - 1, -1, -1) if reverse else range(n)):
        acc = xs[s] if acc is None else acc + xs[s]
        within[s] = acc
    lanes = _lane_scan(acc, reverse)
    beyond = lanes - acc
    return [w + beyond for w in within], (lanes[:, 0:1] if reverse else lanes[:, LANES - 1:LANES])


SB_DEAD = 110.0


def _sb_fwd(name, q, k, v):
    B, H, S, dh = q.shape
    T, TK = _att_tiles(S, SB_TILE)
    scale = np.float32(dh ** -0.5)
    pieces = _pieces(T, TK)

    def body(q_ref, k_ref, v_ref, o_ref, lt_ref, first_ref, z_scr, a_scr):
        i = pl.program_id(2)
        qb = q_ref[0, 0]
        n_tiles = _causal_tiles(i, T, TK)

        def tile(j, carry, diag):
            runs, acc = carry
            ks = pl.ds(pl.multiple_of(j * TK, TK), TK)
            z_scr[...] = lax.dot_general(qb, k_ref[0, 0, ks, :], _NT, preferred_element_type=F32) * scale
            new_runs = []
            for rc, (rs, segs) in enumerate(pieces):
                terms = [_sb_terms(z_scr[rs, cs], False) for cs in segs]
                keep = [_piece_keep(i * T, j * TK, rs, cs, True, len(segs)) if diag else None for cs in segs]
                l1 = [jnp.where(kp, t[1], 0.0) if diag else t[1] for kp, t in zip(keep, terms)]
                right_of, total = _chunk_scan(l1, True)
                for cs, kp, t, x, r in zip(segs, keep, terms, l1, right_of):
                    a = jnp.exp(t[0] + ((r - x) + runs[rc]))
                    a_scr[rs, cs] = (jnp.where(kp, a, 0.0) if diag else a).astype(BF16)
                new_runs.append(runs[rc] + total)
            return tuple(new_runs), acc + jnp.dot(a_scr[...], v_ref[0, 0, ks, :], preferred_element_type=F32)

        init = (tuple(jnp.zeros((rs.stop - rs.start, 1), F32) for rs, _ in pieces), jnp.zeros((T, dh), F32))
        def some_row_alive(runs):
            worst = runs[0]
            for r in runs[1:]:
                worst = jnp.maximum(worst, r)
            return jnp.max(worst) > -SB_DEAD

        def step(c):
            runs, acc = tile(n_tiles - 1 - c[0], (c[1], c[2]), False)
            return c[0] + 1, runs, acc

        visited, runs, acc = lax.while_loop(lambda c: jnp.logical_and(c[0] < n_tiles, some_row_alive(c[1])), step,
                                            (jnp.int32(1), *tile(n_tiles - 1, init, True)))
        o_ref[0, 0] = acc.astype(o_ref.dtype)
        lt_ref[0, 0] = jnp.concatenate(runs, axis=0)
        first_ref[pl.program_id(0), pl.program_id(1), i] = (n_tiles - visited).astype(F32)

    full = lambda w: pl.BlockSpec((1, 1, S, w), lambda b, h, i: (b, h, 0, 0))
    blk = lambda w: pl.BlockSpec((1, 1, T, w), lambda b, h, i: (b, h, i, 0))
    return pl.pallas_call(
        body, name=_unique(name), grid=(B, H, S // T),
        in_specs=[blk(dh), full(dh), full(dh)], out_specs=[blk(dh), blk(1), pl.BlockSpec(memory_space=pltpu.SMEM)],
        out_shape=[jax.ShapeDtypeStruct((B, H, S, dh), BF16), jax.ShapeDtypeStruct((B, H, S, 1), F32),
                   jax.ShapeDtypeStruct((B, H, S // T), F32)],
        scratch_shapes=[pltpu.VMEM((T, TK), F32), pltpu.VMEM((T, TK), BF16)],
        compiler_params=_params(("arbitrary", "arbitrary", "arbitrary")))(q, k, v)


def _sb_bwd(name, q, k, v, do, lt, first):
    B, H, S, dh = q.shape
    T, TK = _att_tiles(S, SB_TILE)
    nt = S // T
    scale = np.float32(dh ** -0.5)

    pieces = _pieces(T, TK)

    def body(first_ref, q_ref, k_ref, v_ref, do_ref, lt_ref, dq_ref, dk_ref, dv_ref, z_scr, da_scr, dz_scr, a_scr, dk_acc, dv_acc):
        dk_acc[...] = jnp.zeros_like(dk_acc)
        dv_acc[...] = jnp.zeros_like(dv_acc)
        b, h = pl.program_id(0), pl.program_id(1)

        def qloop(i, _):
            qs = pl.ds(pl.multiple_of(i * T, T), T)
            qb, dob16 = q_ref[0, 0, qs, :], do_ref[0, 0, qs, :].astype(BF16)
            n_tiles = _causal_tiles(i, T, TK)
            first_tile = jnp.clip(first_ref[b, h, i].astype(jnp.int32), 0, n_tiles - 1)
            lt_c = [lt_ref[0, 0, pl.ds(pl.multiple_of(i * T + rs.start, SUBLANES), rs.stop - rs.start), :] for rs, _ in pieces]

            def tile(j, carry, diag):
                sums_l, sums_e, dq = carry
                ks = pl.ds(pl.multiple_of(j * TK, TK), TK)
                kb, vb = k_ref[0, 0, ks, :], v_ref[0, 0, ks, :]
                z_scr[...] = lax.dot_general(qb, kb, _NT, preferred_element_type=F32) * scale
                da_scr[...] = lax.dot_general(dob16, vb, _NT, preferred_element_type=F32)
                new_l, new_e = [], []
                for rc, (rs, segs) in enumerate(pieces):
                    terms = [_sb_terms(z_scr[rs, cs]) for cs in segs]
                    keep = [_piece_keep(i * T, j * TK, rs, cs, True, len(segs)) if diag else None for cs in segs]
                    l1 = [jnp.where(kp, t[1], 0.0) if diag else t[1] for kp, t in zip(keep, terms)]
                    upto, total_l = _chunk_scan(l1, False)
                    es = []
                    for cs, kp, t, u in zip(segs, keep, terms, upto):
                        a = jnp.exp(t[0] + (lt_c[rc] - (u + sums_l[rc])))
                        if diag:
                            a = jnp.where(kp, a, 0.0)
                        a_scr[rs, cs] = a.astype(BF16)
                        es.append(da_scr[rs, cs] * a)
                    e_upto, total_e = _chunk_scan(es, False)
                    for cs, kp, t, e, eu in zip(segs, keep, terms, es, e_upto):
                        dz = e * t[3] - ((eu - e) + sums_e[rc]) * t[2]
                        dz_scr[rs, cs] = (jnp.where(kp, dz, 0.0) if diag else dz).astype(BF16)
                    new_l.append(sums_l[rc] + total_l)
                    new_e.append(sums_e[rc] + total_e)
                dz16 = dz_scr[...]
                dk_acc[ks, :] += lax.dot_general(dz16, qb, _TN, preferred_element_type=F32)
                dv_acc[ks, :] += lax.dot_general(a_scr[...], dob16, _TN, preferred_element_type=F32)
                return tuple(new_l), tuple(new_e), dq + jnp.dot(dz16, kb, preferred_element_type=F32)

            zeros = tuple(jnp.zeros((rs.stop - rs.start, 1), F32) for rs, _ in pieces)
            carry = lax.fori_loop(first_tile, n_tiles - 1, lambda j, c: tile(j, c, False), (zeros, zeros, jnp.zeros((T, dh), F32)))
            dq_ref[0, 0, qs, :] = (tile(n_tiles - 1, carry, True)[2] * scale).astype(dq_ref.dtype)
            return 0

        lax.fori_loop(0, nt, qloop, 0)
        dk_ref[0, 0] = (dk_acc[...] * scale).astype(dk_ref.dtype)
        dv_ref[0, 0] = dv_acc[...].astype(dv_ref.dtype)

    full = lambda w: pl.BlockSpec((1, 1, S, w), lambda b, h: (b, h, 0, 0))
    return pl.pallas_call(
        body, name=_unique(name), grid=(B, H),
        in_specs=[pl.BlockSpec(memory_space=pltpu.SMEM), full(dh), full(dh), full(dh), full(dh), full(1)], out_specs=[full(dh)] * 3,
        out_shape=[jax.ShapeDtypeStruct((B, H, S, dh), BF16)] * 3,
        scratch_shapes=[pltpu.VMEM((T, TK), F32), pltpu.VMEM((T, TK), F32), pltpu.VMEM((T, TK), BF16), pltpu.VMEM((T, TK), BF16),
                        pltpu.VMEM((S, dh), F32), pltpu.VMEM((S, dh), F32)],
        compiler_params=_params(("parallel", "parallel")))(first, q, k, v, do, lt)


def _adamw(name, w, g, m, v):
    shape = w.shape
    n = w.size
    cols = shape[-1] if (w.ndim >= 2 and (shape[-1] % LANES == 0 or n // shape[-1] >= LANES)) else 0
    if cols:
        prep = lambda t: t.reshape(1, n // cols, cols)
    else:
        cols = LANES
        pad = (-n) % (SUBLANES * LANES)
        prep = lambda t: jnp.pad(t.reshape(-1), (0, pad), constant_values=1.0).reshape(1, (n + pad) // cols, cols)

    def fn(r, bv, cv):
        w_, g_, m_, v_ = r
        m2 = ADAM_B1 * m_ + (1.0 - ADAM_B1) * g_
        v2 = ADAM_B2 * v_ + (1.0 - ADAM_B2) * (g_ * g_)
        m_hat = m2 / (1.0 - ADAM_B1 ** ADAM_STEP)
        v_hat = v2 / (1.0 - ADAM_B2 ** ADAM_STEP)
        return [-ADAM_LR * (m_hat / (jnp.sqrt(v_hat) + ADAM_EPS) + ADAM_WD * w_), m2, v2], [], []
    outs = _rowwise(name, fn, [prep(w), prep(g), prep(m), prep(v)], out_rows=[(cols, F32)] * 3, tm=512)
    return [o.reshape(-1)[:n].reshape(shape) for o in outs]


def _sum8(name, parts):
    def fn(r, bv, cv):
        s = r[0]
        for t in r[1:]:
            s = s + t
        return [s], [], []
    rows = [parts[i][None] for i in range(parts.shape[0])]
    return _rowwise(name, fn, rows, out_rows=[(parts.shape[2], F32)])[0][0]


def _pack(arrs, cols, dtype, row_mult):
    flat = jnp.concatenate([a.reshape(-1).astype(dtype) for a in arrs])
    pad = (-flat.size) % (cols * row_mult)
    return jnp.pad(flat, (0, pad)).reshape(-1, cols)


def _unpack(flat, shapes):
    out, off = [], 0
    for s in shapes:
        n = int(np.prod(s))
        out.append(flat[off:off + n].reshape(s))
        off += n
    return out


def _heads(t, H):
    B, S, W = t.shape
    return t.reshape(B, S, H, W // H).transpose(0, 2, 1, 3)


def _unheads(t):
    B, H, S, dh = t.shape
    return t.transpose(0, 2, 1, 3).reshape(B, S, H * dh)


def kernel(*args):
    _names_used.clear()
    p = dict(zip(ARGS, args))
    x, target = p['x'], p['loss_target']
    B, S, D = x.shape
    T = B * S
    depth = p['ln1_g'].shape[0]
    H = D // HEAD_DIM
    alpha = np.float32((2.0 * depth) ** 0.25)
    cx, cy, cc = _mesh_pos()
    my_q = 2 * cx + cy
    axes = ("x", "y", "c")
    two = lambda t: t.reshape(T, t.shape[-1])
    three = lambda t: t.reshape(B, S, t.shape[-1])

    small_in = [p['c']] + [p[n] for n in SMALL_SPLIT]
    g1 = _all_gather8("ag_small", [_pack(small_in, LANES, F32, SUBLANES)])[0]
    g1 = g1.reshape(N_DEV, -1)
    c_all = g1[:, :B * D].reshape(N_DEV * B, D)
    per_chip = [_unpack(g1[2 * q], [a.shape for a in small_in])[1:] for q in range(N_CHIPS)]
    small = {n: jnp.concatenate([per_chip[q][i] for q in range(N_CHIPS)], axis=-1) for i, n in enumerate(SMALL_SPLIT)}
    for n in SMALL_REPL:
        small[n] = p[n]

    n_seq = N_DEV * B
    seq_pad = -(-n_seq // LANES) * LANES
    c_act = _rowwise("c_act", lambda r, bv, cv: ([_silu(r[0])], [], []),
                     [jnp.pad(c_all, ((0, seq_pad - n_seq), (0, 0)))[None]], out_rows=[(D, F32)])[0][0]
    mod_cols = p['mod_w'].shape[2]
    mod_part = jnp.stack([_mm(f"mod_fwd{l}", c_act, p['mod_w'][l])[:n_seq] for l in range(depth)])
    half_layers = depth // 2
    mod_half = lax.dynamic_slice_in_dim(mod_part, cc * half_layers, half_layers, axis=0)
    gm_ = _all_gather8("ag_mod", [mod_half.reshape(half_layers * n_seq, mod_cols)])[0]
    mod_all = gm_.reshape(N_CHIPS, 2, half_layers, n_seq, mod_cols).transpose(1, 2, 3, 0, 4).reshape(depth, n_seq, 6 * D)
    mod_mine = lax.dynamic_slice_in_dim(mod_all, (2 * my_q + cc) * B, B, axis=1)
    mod = _rowwise("mod_bias", lambda r, bv, cv: ([r[0] + bv[0]], [], []), [mod_mine], [p['mod_b'][:, None, :]],
                   out_rows=[(6 * D, F32)])[0]
    mods = [[mod[l, :, None, i * D:(i + 1) * D] for i in range(6)] for l in range(depth)]

    big_names = list(BIG)
    shard_shapes = [p[n].shape for n in big_names]
    half_rows = [s[0] * s[1] // 2 for s in shard_shapes]
    w_halves = [lax.dynamic_slice_in_dim(p[n].reshape(-1, s[2]), cc * hr, hr, axis=0).astype(BF16)
                for n, s, hr in zip(big_names, shard_shapes, half_rows)]
    W = {}
    for n, s, g in zip(big_names, shard_shapes, _all_gather8("ag_weights", w_halves)):
        seg = g.reshape((N_CHIPS,) + s)
        W[n] = jnp.concatenate([seg[q] for q in range(N_CHIPS)], axis=BIG[n])

    def vec(n, j):
        return small[n][j][None, :]

    def attn_proj(h1, w_in, gate_cols):
        wp = jnp.pad(w_in, ((0, 0), (0, gate_cols))) if gate_cols else w_in
        proj = three(_mm("att_proj", two(h1), wp))
        q, k, v = [_heads(proj[..., i * D:(i + 1) * D].astype(BF16), H) for i in range(3)]
        return wp, proj, q, k, v

    def gm_fwd(j, h1):
        zin = three(_mm("gm_in", two(h1), W['gm_w_in'][j]))
        u, vn = _gm_act("gm_act", zin, vec('gm_b_in', j), vec('gm_ln_g', j), vec('gm_ln_b', j))
        b_sT = small['gm_b_s'][j].T
        yg = _gm_spatial("gm_spatial", u, vn, small['gm_w_s'][j], b_sT)
        return three(_mm("gm_out", two(yg), W['gm_w_out'][j])), (zin, u, vn, b_sT, yg)

    def gm_bwd(j, h1, dy1, cache):
        zin, u, vn, b_sT, yg = cache
        g = {'gm_w_out': _mm("gm_dwout", two(yg), two(dy1), ta=True)}
        dyg = three(_mm("gm_dyg", two(dy1), W['gm_w_out'][j], tb=True))
        du, dvn, dws, dbsT = _gm_spatial_bwd("gm_spatial_bwd", dyg, u, vn, small['gm_w_s'][j], b_sT)
        dzin, g['gm_b_in'], g['gm_ln_g'], g['gm_ln_b'] = _gm_act_bwd("gm_act_bwd", zin, du, dvn, vec('gm_b_in', j), vec('gm_ln_g', j))
        g['gm_w_s'], g['gm_b_s'] = dws, dbsT.T
        g['gm_w_in'] = _mm("gm_dwin", two(h1), two(dzin), ta=True)
        return _mm("gm_dh", two(dzin), W['gm_w_in'][j], tb=True), g

    def fox_fwd(j, h1):
        wp, proj, q, k, v = attn_proj(h1, W['fox_w_in'][j], 3 * D + FOX_GATE_COLS - W['fox_w_in'].shape[2])
        fl = proj[..., 3 * D:]
        bf = jnp.pad(small['fox_b_f'][j][None, :], ((0, 0), (0, FOX_GATE_COLS - H)))
        Fh = _fox_gate_cumsum("fox_gate", fl, bf)[..., :H].transpose(0, 2, 1)
        fq, fk = Fh[..., None], Fh.reshape(B, H, -1, _att_tiles(S, FOX_TILE)[1])
        o, lse = _fox_fwd("fox_fwd", q, k, v, fq, fk)
        o2 = _unheads(o)
        return three(_mm("fox_out", two(o2), W['fox_w_out'][j])), (wp, q, k, v, fl, bf, fq, fk, lse, o2)

    def fox_bwd(j, h1, dy1, cache):
        wp, q, k, v, fl, bf, fq, fk, lse, o2 = cache
        g = {'fox_w_out': _mm("fox_dwout", two(o2), two(dy1), ta=True)}
        do = _heads(three(_mm("fox_do", two(dy1), W['fox_w_out'][j], tb=True, out_dtype=BF16)), H)
        dq, dk, dv, dfk = _fox_bwd("fox_bwd", q, k, v, fq, fk, do, lse)
        dF = jnp.pad(dfk.reshape(B, H, S).transpose(0, 2, 1), ((0, 0), (0, 0), (0, FOX_GATE_COLS - H)))
        dfl, dbf = _fox_gate_bwd("fox_gate_bwd", dF, fl, bf, H)
        dproj = jnp.concatenate([_unheads(dq).astype(BF16), _unheads(dk).astype(BF16), _unheads(dv).astype(BF16),
                                 dfl.astype(BF16)], axis=-1)
        g['fox_w_in'] = _mm("fox_dwin", two(h1), two(dproj), ta=True)[:, :W['fox_w_in'].shape[2]]
        g['fox_b_f'] = dbf[0, :H]
        return _mm("fox_dh", two(dproj), wp, tb=True), g

    def sb_fwd(j, h1):
        wp, proj, q, k, v = attn_proj(h1, W['sb_w_in'][j], 0)
        k, v = _lane_major(k, _att_tiles(S, SB_TILE)[1]), _lane_major(v, _att_tiles(S, SB_TILE)[1])
        o, lt, first = _sb_fwd("sb_fwd", q, k, v)
        o2 = _unheads(o)
        return three(_mm("sb_out", two(o2), W['sb_w_out'][j])), (q, k, v, lt, first, o2)

    def sb_bwd(j, h1, dy1, cache):
        q, k, v, lt, first, o2 = cache
        g = {'sb_w_out': _mm("sb_dwout", two(o2), two(dy1), ta=True)}
        do = _heads(three(_mm("sb_do", two(dy1), W['sb_w_out'][j], tb=True, out_dtype=BF16)), H)
        dq, dk, dv = _sb_bwd("sb_bwd", q, k, v, do, lt, first)
        dk, dv = _lane_major_inverse(dk, _att_tiles(S, SB_TILE)[1]), _lane_major_inverse(dv, _att_tiles(S, SB_TILE)[1])
        dproj = jnp.concatenate([_unheads(dq).astype(BF16), _unheads(dk).astype(BF16), _unheads(dv).astype(BF16)], axis=-1)
        g['sb_w_in'] = _mm("sb_dwin", two(h1), two(dproj), ta=True)
        return _mm("sb_dh", two(dproj), W['sb_w_in'][j], tb=True), g

    def cv_fwd(j, h1):
        pw = three(_mm("cv_in", two(h1), W['cv_w_in'][j]))
        ygl = _cv_glu("cv_glu", pw, vec('cv_b_in', j))
        dw = jnp.pad(small['cv_dw'][j], ((0, CONV_HALO - CONV_WIDTH), (0, 0)))
        yc = _dwconv("cv_dwconv", ygl, dw, vec('cv_dw_b', j))
        ys = _cv_ln_act("cv_ln_act", yc, vec('cv_ln_g', j), vec('cv_ln_b', j))
        return three(_mm("cv_out", two(ys), W['cv_w_out'][j])), (pw, ygl, dw, yc, ys)

    def cv_bwd(j, h1, dy1, cache):
        pw, ygl, dw, yc, ys = cache
        g = {'cv_w_out': _mm("cv_dwout", two(ys), two(dy1), ta=True)}
        dys = three(_mm("cv_dys", two(dy1), W['cv_w_out'][j], tb=True))
        dyc, g['cv_ln_g'], g['cv_ln_b'], g['cv_dw_b'] = _cv_ln_act_bwd("cv_ln_act_bwd", yc, dys, vec('cv_ln_g', j), vec('cv_ln_b', j))
        dygl, ddw = _dwconv_bwd("cv_dwconv_bwd", dyc, ygl, dw)
        g['cv_dw'] = ddw[:CONV_WIDTH]
        dpw, g['cv_b_in'] = _cv_glu_bwd("cv_glu_bwd", pw, dygl, vec('cv_b_in', j))
        g['cv_w_in'] = _mm("cv_dwin", two(h1), two(dpw), ta=True)
        return _mm("cv_dh", two(dpw), W['cv_w_in'][j], tb=True), g

    mixers = [(gm_fwd, gm_bwd), (fox_fwd, fox_bwd), (sb_fwd, sb_bwd), (cv_fwd, cv_bwd)]
    n_mix = len(mixers)

    saved = []
    h1 = _modulate("mod1", x, mods[0][1], mods[0][0])
    for l in range(depth):
        m, j = l % n_mix, l // n_mix
        sh1, sc1, g1_, sh2, sc2, g2_ = mods[l]
        ybias = vec('cv_b_out', j) if m == 3 else None
        y1, cache = mixers[m][0](j, h1)
        xm, h2 = _resid_ln("resid_ln1", alpha, x, y1, g1_, small['ln1_g'][l][None], small['ln1_b'][l][None], ybias, then=(sc2, sh2))
        *z, a = _ffn_in_act("ffn_in", two(h2), W['ffn_w_in'][l])
        y2 = three(_mm("ffn_out", a, W['ffn_w_out'][l]))
        nxt = (mods[l + 1][1], mods[l + 1][0]) if l + 1 < depth else None
        xo, *h_next = _resid_ln("resid_ln2", alpha, xm, y2, g2_, small['ln2_g'][l][None], small['ln2_b'][l][None], then=nxt)
        saved.append((x, h1, y1, cache, xm, h2, z, a, y2, ybias))
        x, h1 = xo, (h_next[0] if h_next else None)

    dx, sq = _loss_head("loss_head", x, target)
    loss = lax.psum(jnp.sum(sq) * np.float32(0.5 / D), axes)

    grads = {n: [None] * p[n].shape[0] for n in WEIGHTS}
    parts = [dict() for _ in range(depth)]
    after = None
    for l in reversed(range(depth)):
        m, j = l % n_mix, l // n_mix
        sh1, sc1, g1_, sh2, sc2, g2_ = mods[l]
        x_in, h1, y1, cache, xm, h2, z, a, y2, ybias = saved[l]
        ln2 = (small['ln2_g'][l][None], small['ln2_b'][l][None])
        if after is None:
            dr2, dy2, parts[l]['g2'], grads['ln2_g'][l], grads['ln2_b'][l], _ = _resid_ln_bwd("resid_ln2_bwd", alpha, dx, xm, y2, g2_, *ln2)
        else:
            (dr2, dy2, parts[l]['g2'], parts[l + 1]['sc1'], parts[l + 1]['sh1'], grads['ln2_g'][l], grads['ln2_b'][l], _) = _resid_ln_bwd(
                "resid_ln2_bwd", alpha, after[0], xm, y2, g2_, *ln2, then=after[1:])
        grads['ffn_w_out'][l] = _mm("ffn_dwout", a, two(dy2), ta=True)
        dz = _ffn_out_bwd_act("ffn_da", two(dy2), W['ffn_w_out'][l], *z)
        grads['ffn_w_in'][l] = jnp.concatenate([_mm("ffn_dwin", two(h2), t, ta=True) for t in dz], axis=1)
        dh2 = three(_ffn_in_bwd("ffn_dh", *dz, W['ffn_w_in'][l]))
        (dr1, dy1, parts[l]['g1'], parts[l]['sc2'], parts[l]['sh2'], grads['ln1_g'][l], grads['ln1_b'][l], dyb) = _resid_ln_bwd(
            "resid_ln1_bwd", alpha, dr2, x_in, y1, g1_, small['ln1_g'][l][None], small['ln1_b'][l][None], ybias, then=(dh2, sc2))
        dh1, mg = mixers[m][1](j, h1, dy1, cache)
        if m == 3:
            mg['cv_b_out'] = dyb
        for n, gval in mg.items():
            grads[n][j] = gval
        after = (dr1, three(dh1), sc1)
    grad_x, parts[0]['sc1'], parts[0]['sh1'] = _modulate_bwd("mod1_bwd", alpha, after[1], after[0], saved[0][0], after[2])
    dmod = [jnp.concatenate([pt[k] for k in ('sh1', 'sc1', 'g1', 'sh2', 'sc2', 'g2')], axis=-1)[:, 0, :] for pt in parts]
    dmod = jnp.stack(dmod)
    grads['mod_b'] = [jnp.sum(dmod[l], axis=0) for l in range(depth)]
    full_shape = {n: tuple(t.shape) for n, t in small.items()}

    small_names = SMALL_REPL + SMALL_SPLIT
    small_parts = [jnp.stack([gv.reshape(full_shape[n][1:]) for gv in grads[n]]) for n in small_names]
    pack_a = _pack([dmod], LANES, F32, SUBLANES)
    pack_b = _pack(small_parts, LANES, F32, SUBLANES)
    g2 = _all_gather8("ag_grads_small", [jnp.concatenate([pack_a, pack_b], axis=0)])[0]
    rows_a = pack_a.shape[0]
    dmod_all = g2[:, :rows_a].reshape(N_DEV, -1)[:, :dmod.size].reshape(N_DEV, depth, B, 6 * D)
    dmod_all = dmod_all.transpose(1, 0, 2, 3).reshape(depth, n_seq, 6 * D)
    small_sum = _sum8("sum_grads_small", g2[:, rows_a:]).reshape(-1)
    g_small = dict(zip(small_names, _unpack(small_sum, [full_shape[n] for n in small_names])))
    for n in SMALL_SPLIT:
        w = p[n].shape[-1]
        g_small[n] = lax.dynamic_slice_in_dim(g_small[n], my_q * w, w, axis=g_small[n].ndim - 1)

    dm_cols = lax.dynamic_slice_in_dim(dmod_all, my_q * mod_cols, mod_cols, axis=2)
    dm_cols = jnp.pad(dm_cols, ((0, 0), (0, seq_pad - n_seq), (0, 0)))
    g_mod_w = jnp.stack([_mm(f"mod_dw{l}", c_act, dm_cols[l], ta=True) for l in range(depth)])

    keep, give = [], []
    for n, s, hr in zip(big_names, shard_shapes, half_rows):
        gfull = jnp.stack(grads[n])
        g4 = jnp.stack(jnp.split(gfull, N_CHIPS, axis=BIG[n])).reshape(N_CHIPS, 2 * hr, s[2])
        keep.append(lax.dynamic_slice_in_dim(g4, cc * hr, hr, axis=1))
        give.append(lax.dynamic_slice_in_dim(g4, (1 - cc) * hr, hr, axis=1))
    got = _sibling_exchange("rs_sibling", give)
    chip_sum = [_rowwise("rs_add_sibling", lambda r, bv, cv: ([r[0] + r[1]], [], []),
                         [a.reshape(1, -1, a.shape[2]), b.reshape(1, -1, a.shape[2])],
                         out_rows=[(a.shape[2], BF16)], tm=512)[0].reshape(a.shape) for a, b in zip(keep, got)]
    from_chips = _chip_all_to_all("rs_chips", chip_sum)
    half_sum = [_rowwise("rs_add_chips", lambda r, bv, cv: ([((r[0] + r[1]) + r[2]) + r[3]], [], []),
                         [t[q][None] for q in range(N_CHIPS)], out_rows=[(t.shape[2], F32)], tm=512)[0][0]
                for t in from_chips]
    other = _sibling_exchange("rs_share", half_sum)
    g_big = {n: jnp.concatenate([jnp.where(cc == 0, a, b), jnp.where(cc == 0, b, a)], axis=0).reshape(s)
             for n, s, a, b in zip(big_names, shard_shapes, half_sum, other)}

    g_out = {**g_small, **g_big, 'mod_w': g_mod_w}
    upd = {n: _adamw("adamw_" + n, p[n], g_out[n], p['m_' + n], p['v_' + n]) for n in WEIGHTS}
    return (loss, grad_x, *[g_out[n] for n in WEIGHTS], *[upd[n][0] for n in WEIGHTS],
            *[upd[n][1] for n in WEIGHTS], *[upd[n][2] for n in WEIGHTS])
```

```python
import math

import jax
import jax.numpy as jnp
import numpy as np
from jax import lax
from jax.experimental import pallas as pl
from jax.experimental.pallas import tpu as pltpu

F32, BF16 = jnp.float32, jnp.bfloat16

HEAD_DIM = 64
CONV_WIDTH = 31
LN_EPS = 1e-5
NEG_INF = -1e30
ADAM_LR, ADAM_B1, ADAM_B2, ADAM_EPS, ADAM_WD, ADAM_STEP = 0.001, 0.9, 0.999, 1e-08, 0.01, 10

LANES = 128
SUBLANES = 8
VMEM_LIMIT_BYTES = 56 * 1024 * 1024
N_CHIPS = 4
N_DEV = 8

WEIGHTS = ['mod_w', 'mod_b', 'ln1_g', 'ln1_b', 'ln2_g', 'ln2_b', 'ffn_w_in', 'ffn_w_out', 'gm_w_in', 'gm_b_in',
           'gm_ln_g', 'gm_ln_b', 'gm_w_s', 'gm_b_s', 'gm_w_out', 'fox_w_in', 'fox_b_f', 'fox_w_out', 'sb_w_in',
           'sb_w_out', 'cv_w_in', 'cv_b_in', 'cv_dw', 'cv_dw_b', 'cv_ln_g', 'cv_ln_b', 'cv_w_out', 'cv_b_out']
ARGS = ['x', 'c'] + WEIGHTS + ['loss_target'] + ['m_' + n for n in WEIGHTS] + ['v_' + n for n in WEIGHTS]
BIG = {'ffn_w_in': 2, 'ffn_w_out': 1, 'gm_w_in': 2, 'gm_w_out': 1, 'fox_w_in': 2, 'fox_w_out': 1,
       'sb_w_in': 2, 'sb_w_out': 1, 'cv_w_in': 2, 'cv_w_out': 1}
SMALL_SPLIT = ['cv_b_in', 'cv_dw', 'cv_dw_b', 'cv_ln_g', 'cv_ln_b', 'cv_b_out']
SMALL_REPL = ['mod_b', 'ln1_g', 'ln1_b', 'ln2_g', 'ln2_b', 'gm_b_in', 'gm_ln_g', 'gm_ln_b', 'gm_w_s', 'gm_b_s', 'fox_b_f']
PACK_COLS = 1024


_names_used = {}


def _unique(name):
    k = _names_used.get(name, 0)
    _names_used[name] = k + 1
    return name if k == 0 else f"{name}_{k}"


def _params(sem):
    return pltpu.CompilerParams(dimension_semantics=sem, vmem_limit_bytes=VMEM_LIMIT_BYTES)


def _pick(dim, pref, mult=LANES):
    if dim <= pref:
        return dim
    best = 0
    for t in range(mult, pref + 1, mult):
        if dim % t == 0:
            best = t
    assert best, (dim, pref)
    return best


def _mesh_pos():
    return lax.axis_index("x"), lax.axis_index("y"), lax.axis_index("c")


AG_COPIES = 7
A2A_COPIES = 3


def _comm_call(name, body, blks, out_shapes, n_sems):
    n = len(blks)
    hbm = pl.BlockSpec(memory_space=pl.ANY)
    return pl.pallas_call(
        body, name=_unique(name), out_shape=out_shapes, in_specs=[hbm] * n, out_specs=[hbm] * n,
        scratch_shapes=[pltpu.SemaphoreType.DMA((n_sems * n,)), pltpu.SemaphoreType.DMA((n_sems * n,)),
                        pltpu.SemaphoreType.DMA((n,))],
    )(*blks)


def _all_gather8(name, blks):
    n = len(blks)

    def body(*refs):
        x_refs, out_refs, (send_sems, recv_sems, local_sems) = refs[:n], refs[n:2 * n], refs[2 * n:]
        x, y, c = _mesh_pos()
        me, sibling = (x, y, c), (x, y, 1 - c)
        chips = [(1 - x, y), (x, 1 - y), (1 - x, 1 - y)]

        def copy(a, k, block, to, from_input=False):
            px, py, pc = block
            slot = out_refs[a].at[4 * px + 2 * py + pc]
            return pltpu.make_async_remote_copy(
                src_ref=x_refs[a] if from_input else slot, dst_ref=slot,
                send_sem=send_sems.at[AG_COPIES * a + k], recv_sem=recv_sems.at[AG_COPIES * a + k],
                device_id=to, device_id_type=pl.DeviceIdType.MESH)

        local, sent = [], []
        for a in range(n):
            local.append(pltpu.make_async_copy(x_refs[a], out_refs[a].at[4 * x + 2 * y + c], local_sems.at[a]))
            local[-1].start()
            first = [copy(a, 0, me, sibling, True)] + [copy(a, 1 + j, me, (*chip, c), True) for j, chip in enumerate(chips)]
            for cp in first:
                cp.start()
            sent += first
        for a in range(n):
            for j, chip in enumerate(chips):
                copy(a, 1 + j, (*chip, c), me).wait_recv()
                sent.append(copy(a, 4 + j, (*chip, c), sibling))
                sent[-1].start()
        for a in range(n):
            copy(a, 0, sibling, me).wait_recv()
            for j, chip in enumerate(chips):
                copy(a, 4 + j, (*chip, 1 - c), me).wait_recv()
        for cp in sent:
            cp.wait_send()
        for cp in local:
            cp.wait()

    return _comm_call(name, body, blks, [jax.ShapeDtypeStruct((N_DEV,) + b.shape, b.dtype) for b in blks], AG_COPIES)


def _sibling_exchange(name, blks):
    n = len(blks)

    def body(*refs):
        x_refs, out_refs, (send_sems, recv_sems, _) = refs[:n], refs[n:2 * n], refs[2 * n:]
        x, y, c = _mesh_pos()
        cps = [pltpu.make_async_remote_copy(src_ref=x_refs[a], dst_ref=out_refs[a], send_sem=send_sems.at[a],
                                            recv_sem=recv_sems.at[a], device_id=(x, y, 1 - c),
                                            device_id_type=pl.DeviceIdType.MESH) for a in range(n)]
        for cp in cps:
            cp.start()
        for cp in cps:
            cp.wait()

    return _comm_call(name, body, blks, [jax.ShapeDtypeStruct(b.shape, b.dtype) for b in blks], 1)


def _chip_all_to_all(name, blks):
    n = len(blks)

    def body(*refs):
        x_refs, out_refs, (send_sems, recv_sems, local_sems) = refs[:n], refs[n:2 * n], refs[2 * n:]
        x, y, c = _mesh_pos()
        chips = [(1 - x, y), (x, 1 - y), (1 - x, 1 - y)]
        my_q = 2 * x + y

        def copy(a, j, src_q, dst_q):
            px, py = chips[j]
            return pltpu.make_async_remote_copy(
                src_ref=x_refs[a].at[src_q], dst_ref=out_refs[a].at[dst_q],
                send_sem=send_sems.at[A2A_COPIES * a + j], recv_sem=recv_sems.at[A2A_COPIES * a + j],
                device_id=(px, py, c), device_id_type=pl.DeviceIdType.MESH)

        local, sent = [], []
        for a in range(n):
            local.append(pltpu.make_async_copy(x_refs[a].at[my_q], out_refs[a].at[my_q], local_sems.at[a]))
            local[-1].start()
            sent += [copy(a, j, 2 * px + py, my_q) for j, (px, py) in enumerate(chips)]
            for cp in sent[-A2A_COPIES:]:
                cp.start()
        for a in range(n):
            for j, (px, py) in enumerate(chips):
                copy(a, j, my_q, 2 * px + py).wait_recv()
        for cp in sent:
            cp.wait_send()
        for cp in local:
            cp.wait()

    return _comm_call(name, body, blks, [jax.ShapeDtypeStruct(b.shape, b.dtype) for b in blks], A2A_COPIES)


def _rowwise(name, fn, rows, bvecs=(), cvecs=(), out_rows=(), out_bsums=(), out_tsums=(), tm=256):
    B, S = rows[0].shape[:2]
    tm = _pick(S, tm, SUBLANES)
    n_r, n_b, n_c = len(rows), len(bvecs), len(cvecs)
    n_or, n_ob = len(out_rows), len(out_bsums)

    def body(*refs):
        ins, outs = refs[:n_r + n_b + n_c], refs[n_r + n_b + n_c:]
        r = [ref[0] for ref in ins[:n_r]]
        bv = [ref[0] for ref in ins[n_r:n_r + n_b]]
        cv = [ref[...] for ref in ins[n_r + n_b:]]
        o_rows, o_bsums, o_tsums = fn(r, bv, cv)
        b, i = pl.program_id(0), pl.program_id(1)
        for ref, val in zip(outs[:n_or], o_rows):
            ref[0] = val.astype(ref.dtype)
        for ref, val in zip(outs[n_or:n_or + n_ob], o_bsums):
            @pl.when(i == 0)
            def _(ref=ref, val=val):
                ref[0] = val

            @pl.when(i > 0)
            def _(ref=ref, val=val):
                ref[0] += val
        for ref, val in zip(outs[n_or + n_ob:], o_tsums):
            first = jnp.logical_and(b == 0, i == 0)

            @pl.when(first)
            def _(ref=ref, val=val):
                ref[...] = val

            @pl.when(jnp.logical_not(first))
            def _(ref=ref, val=val):
                ref[...] += val

    in_specs = [pl.BlockSpec((1, tm, a.shape[2]), lambda b, i: (b, i, 0)) for a in rows]
    in_specs += [pl.BlockSpec((1, 1, a.shape[2]), lambda b, i: (b, 0, 0)) for a in bvecs]
    in_specs += [pl.BlockSpec((1, a.shape[1]), lambda b, i: (0, 0)) for a in cvecs]
    out_shape = [jax.ShapeDtypeStruct((B, S, cdim), dt) for cdim, dt in out_rows]
    out_specs = [pl.BlockSpec((1, tm, cdim), lambda b, i: (b, i, 0)) for cdim, _ in out_rows]
    out_shape += [jax.ShapeDtypeStruct((B, 1, cdim), F32) for cdim in out_bsums]
    out_specs += [pl.BlockSpec((1, 1, cdim), lambda b, i: (b, 0, 0)) for cdim in out_bsums]
    out_shape += [jax.ShapeDtypeStruct((1, cdim), F32) for cdim in out_tsums]
    out_specs += [pl.BlockSpec((1, cdim), lambda b, i: (0, 0)) for cdim in out_tsums]
    sem = ("arbitrary", "arbitrary") if out_tsums else ("parallel", "arbitrary")
    res = pl.pallas_call(body, name=_unique(name), grid=(B, S // tm), in_specs=in_specs, out_specs=out_specs,
                         out_shape=out_shape, compiler_params=_params(sem))(*rows, *bvecs, *cvecs)
    return list(res)


MM_TILE = 1536
MM_ROWS = 512
MM_WEIGHT_TILE_BYTES = 12 * 1024 * 1024


def _mm(name, a, b, ta=False, tb=False, out_dtype=F32):
    M, K = (a.shape[1], a.shape[0]) if ta else a.shape
    N = b.shape[0] if tb else b.shape[1]
    assert (b.shape[1] if tb else b.shape[0]) == K, (a.shape, b.shape, ta, tb)
    tn = _pick(N, MM_TILE)
    if ta:
        tm, tk = _pick(M, MM_TILE), _pick(K, 2 * MM_ROWS, LANES if tb else SUBLANES)
    else:
        tm = _pick(M, MM_ROWS, SUBLANES)
        tk = K if K * tn * 2 <= MM_WEIGHT_TILE_BYTES else _pick(K, MM_TILE)
    nk = K // tk
    dims = (((0 if ta else 1,), (1 if tb else 0,)), ((), ()))

    def body(a_ref, b_ref, o_ref, acc_ref):
        k = pl.program_id(2)
        p = lax.dot_general(a_ref[...].astype(BF16), b_ref[...].astype(BF16), dims, preferred_element_type=F32)
        if nk == 1:
            o_ref[...] = p.astype(o_ref.dtype)
        else:
            @pl.when(k == 0)
            def _():
                acc_ref[...] = p

            @pl.when(k > 0)
            def _():
                acc_ref[...] += p

            @pl.when(k == nk - 1)
            def _():
                o_ref[...] = acc_ref[...].astype(o_ref.dtype)

    a_spec = pl.BlockSpec((tk, tm), lambda j, i, k: (k, i)) if ta else pl.BlockSpec((tm, tk), lambda j, i, k: (i, k))
    b_spec = pl.BlockSpec((tn, tk), lambda j, i, k: (j, k)) if tb else pl.BlockSpec((tk, tn), lambda j, i, k: (k, j))
    return pl.pallas_call(
        body, name=_unique(name), grid=(N // tn, M // tm, nk), in_specs=[a_spec, b_spec],
        out_specs=pl.BlockSpec((tm, tn), lambda j, i, k: (i, j)),
        out_shape=jax.ShapeDtypeStruct((M, N), out_dtype),
        scratch_shapes=[pltpu.VMEM((tm, tn) if nk > 1 else (SUBLANES, LANES), F32)],
        compiler_params=_params(("parallel", "parallel", "arbitrary")))(a, b)


def _silu(x):
    return x * _sigmoid(x)


def _sigmoid(x):
    return 1.0 / (1.0 + jnp.exp(-x))


def _dsilu(x):
    s = _sigmoid(x)
    return s * (1.0 + x * (1.0 - s))


def _gelu(x):
    return 0.5 * x * (1.0 + lax.erf(x * np.float32(math.sqrt(0.5))))


def _dgelu(x):
    cdf = 0.5 * (1.0 + lax.erf(x * np.float32(math.sqrt(0.5))))
    pdf = jnp.exp(-0.5 * x * x) * np.float32(1.0 / math.sqrt(2.0 * math.pi))
    return cdf + x * pdf


def _ln_stats(r):
    mu = jnp.mean(r, axis=-1, keepdims=True)
    xc = r - mu
    var = jnp.mean(xc * xc, axis=-1, keepdims=True)
    rstd = lax.rsqrt(var + LN_EPS)
    return xc * rstd, rstd


def _ln_bwd(dxhat, xhat, rstd):
    m1 = jnp.mean(dxhat, axis=-1, keepdims=True)
    m2 = jnp.mean(dxhat * xhat, axis=-1, keepdims=True)
    return rstd * (dxhat - m1 - xhat * m2)


def _csum(v):
    return jnp.sum(v, axis=0, keepdims=True)


def _split_dot(x, m01, lhs01=False, terms=2):
    acc, rem = None, x
    for _ in range(terms):
        part = rem.astype(BF16)
        rem = rem - part.astype(F32)
        d = jnp.dot(m01, part, preferred_element_type=F32) if lhs01 else jnp.dot(part, m01, preferred_element_type=F32)
        acc = d if acc is None else acc + d
    return acc


def _iota2(shape, dim):
    return lax.broadcasted_iota(jnp.int32, shape, dim)


def _modulate(name, x, sc, sh):
    D = x.shape[2]
    return _rowwise(name, lambda r, bv, cv: ([r[0] * (1.0 + bv[0]) + bv[1]], [], []),
                    [x], [sc, sh], [], out_rows=[(D, BF16)])[0]


def _resid_ln(name, alpha, x, y, g, ln_g, ln_b, ybias=None, then=None):
    D = x.shape[2]

    def fn(r, bv, cv):
        yy = r[1] if ybias is None else r[1] + cv[2]
        xhat, _ = _ln_stats(alpha * r[0] + (1.0 + bv[0]) * yy)
        xn = xhat * cv[0] + cv[1]
        return [xn] + ([xn * (1.0 + bv[1]) + bv[2]] if then else []), [], []
    cvecs = [ln_g, ln_b] + ([] if ybias is None else [ybias])
    return _rowwise(name, fn, [x, y], [g] + list(then or ()), cvecs, out_rows=[(D, F32)] + ([(D, BF16)] if then else []))


def _resid_ln_bwd(name, alpha, dxn, x, y, g, ln_g, ln_b, ybias=None, then=None):
    D = x.shape[2]

    def fn(r, bv, cv):
        yy = r[2] if ybias is None else r[2] + cv[2]
        xhat, rstd = _ln_stats(alpha * r[1] + (1.0 + bv[0]) * yy)
        d, sums = r[0], []
        if then:
            d = alpha * d + r[3] * (1.0 + bv[1])
            sums = [_csum(r[3] * (xhat * cv[0] + cv[1])), _csum(r[3])]
        dr = _ln_bwd(d * cv[0], xhat, rstd)
        dy = (1.0 + bv[0]) * dr
        return [dr, dy], [_csum(dr * yy)] + sums, [_csum(d * xhat), _csum(d), _csum(dy)]
    cvecs = [ln_g, ln_b] + ([] if ybias is None else [ybias])
    rows, bvecs = [dxn, x, y] + ([then[0]] if then else []), [g] + ([then[1]] if then else [])
    return _rowwise(name, fn, rows, bvecs, cvecs, out_rows=[(D, F32), (D, BF16)], out_bsums=[D] * (3 if then else 1),
                    out_tsums=[D, D, D])


def _modulate_bwd(name, alpha, dh, dr, x, sc):
    D = x.shape[2]

    def fn(r, bv, cv):
        return [alpha * r[1] + r[0] * (1.0 + bv[0])], [_csum(r[0] * r[2]), _csum(r[0])], []
    return _rowwise(name, fn, [dh, dr, x], [sc], [], out_rows=[(D, F32)], out_bsums=[D, D])


def _loss_head(name, y, target):
    D = y.shape[2]

    def fn(r, bv, cv):
        e = r[0] - r[1]
        return [e * np.float32(1.0 / D)], [_csum(e * e)], []
    return _rowwise(name, fn, [y, target], [], [], out_rows=[(D, F32)], out_bsums=[D])


def _ffn_in_act(name, h, w_in):
    M, K = h.shape
    Hd = w_in.shape[1] // 2
    tm, tn = _pick(M, MM_ROWS, SUBLANES), _pick(Hd, MM_TILE)
    nj = Hd // tn

    def body(h_ref, wg_ref, wu_ref, zg_ref, zu_ref, act_ref):
        hv = h_ref[...].astype(BF16)
        zg = jnp.dot(hv, wg_ref[...].astype(BF16), preferred_element_type=F32)
        zu = jnp.dot(hv, wu_ref[...].astype(BF16), preferred_element_type=F32)
        zg_ref[...] = zg
        zu_ref[...] = zu
        act_ref[...] = (_silu(zg) * zu).astype(act_ref.dtype)

    tile = pl.BlockSpec((tm, tn), lambda j, i: (i, j))
    return pl.pallas_call(
        body, name=_unique(name), grid=(nj, M // tm),
        in_specs=[pl.BlockSpec((tm, K), lambda j, i: (i, 0)), pl.BlockSpec((K, tn), lambda j, i: (0, j)),
                  pl.BlockSpec((K, tn), lambda j, i: (0, j + nj))],
        out_specs=[tile, tile, tile],
        out_shape=[jax.ShapeDtypeStruct((M, Hd), F32), jax.ShapeDtypeStruct((M, Hd), F32), jax.ShapeDtypeStruct((M, Hd), BF16)],
        compiler_params=_params(("parallel", "parallel")))(h, w_in, w_in)


def _ffn_out_bwd_act(name, dy, w_out, zg, zu):
    M, K = dy.shape
    Hd = w_out.shape[0]
    tm, tn = _pick(M, MM_ROWS, SUBLANES), _pick(Hd, MM_TILE)

    def body(dy_ref, w_ref, zg_ref, zu_ref, dzg_ref, dzu_ref):
        da = lax.dot_general(dy_ref[...].astype(BF16), w_ref[...].astype(BF16), _NT, preferred_element_type=F32)
        gg = zg_ref[...]
        sg = _sigmoid(gg)
        dzg_ref[...] = (da * zu_ref[...] * (sg * (1.0 + gg * (1.0 - sg)))).astype(dzg_ref.dtype)
        dzu_ref[...] = (da * (gg * sg)).astype(dzu_ref.dtype)

    tile = pl.BlockSpec((tm, tn), lambda j, i: (i, j))
    return pl.pallas_call(
        body, name=_unique(name), grid=(Hd // tn, M // tm),
        in_specs=[pl.BlockSpec((tm, K), lambda j, i: (i, 0)), pl.BlockSpec((tn, K), lambda j, i: (j, 0)), tile, tile],
        out_specs=[tile, tile],
        out_shape=[jax.ShapeDtypeStruct((M, Hd), BF16), jax.ShapeDtypeStruct((M, Hd), BF16)],
        compiler_params=_params(("parallel", "parallel")))(dy, w_out, zg, zu)


def _ffn_in_bwd(name, dzg, dzu, w_in):
    M, Hd = dzg.shape
    N = w_in.shape[0]
    tm, tn = _pick(M, MM_ROWS, SUBLANES), _pick(N, MM_TILE)

    def body(g_ref, u_ref, wg_ref, wu_ref, o_ref):
        o_ref[...] = (lax.dot_general(g_ref[...], wg_ref[...].astype(BF16), _NT, preferred_element_type=F32)
                      + lax.dot_general(u_ref[...], wu_ref[...].astype(BF16), _NT, preferred_element_type=F32))

    rows = pl.BlockSpec((tm, Hd), lambda j, i: (i, 0))
    return pl.pallas_call(
        body, name=_unique(name), grid=(N // tn, M // tm),
        in_specs=[rows, rows, pl.BlockSpec((tn, Hd), lambda j, i: (j, 0)), pl.BlockSpec((tn, Hd), lambda j, i: (j, 1))],
        out_specs=pl.BlockSpec((tm, tn), lambda j, i: (i, j)),
        out_shape=jax.ShapeDtypeStruct((M, N), F32),
        compiler_params=_params(("parallel", "parallel")))(dzg, dzu, w_in, w_in)


def _gm_act(name, zin, b_in, ln_g, ln_b):
    W = zin.shape[2] // 2

    def fn(r, bv, cv):
        z = _gelu(r[0] + cv[0])
        vhat, _ = _ln_stats(z[:, W:])
        return [z[:, :W], vhat * cv[1] + cv[2]], [], []
    return _rowwise(name, fn, [zin], [], [b_in, ln_g, ln_b], out_rows=[(W, F32), (W, BF16)])


def _gm_act_bwd(name, zin, du, dvn, b_in, ln_g):
    W = zin.shape[2] // 2

    def fn(r, bv, cv):
        zz = r[0] + cv[0]
        z = _gelu(zz)
        vhat, rstd = _ln_stats(z[:, W:])
        dv = _ln_bwd(r[2] * cv[1], vhat, rstd)
        dzin = jnp.concatenate([r[1], dv], axis=1) * _dgelu(zz)
        return [dzin], [], [_csum(dzin), _csum(r[2] * vhat), _csum(r[2])]
    return _rowwise(name, fn, [zin, du, dvn], [], [b_in, ln_g], out_rows=[(2 * W, BF16)], out_tsums=[2 * W, W, W])


def _gm_causal_w(ws_ref, g):
    T = ws_ref.shape[1]
    return jnp.where(_iota2((T, T), 1) <= _iota2((T, T), 0), ws_ref[g], 0.0).astype(BF16)


def _gm_spatial(name, u, vn, w_s, b_sT):
    B, S, W = u.shape
    G, T = w_s.shape[0], w_s.shape[1]
    assert W == G * T, "a head group is as wide as a chunk is long"

    def body(u_ref, vn_ref, ws_ref, bs_ref, y_ref):
        for g in range(G):
            cs = slice(g * T, (g + 1) * T)
            sv = jnp.dot(_gm_causal_w(ws_ref, g), vn_ref[0, :, cs], preferred_element_type=F32) + bs_ref[:, g:g + 1]
            y_ref[0, :, cs] = (u_ref[0, :, cs] * sv).astype(y_ref.dtype)

    row = pl.BlockSpec((1, T, W), lambda b, i: (b, i, 0))
    return pl.pallas_call(
        body, name=_unique(name), grid=(B, S // T),
        in_specs=[row, row, pl.BlockSpec((G, T, T), lambda b, i: (0, 0, 0)), pl.BlockSpec((T, G), lambda b, i: (0, 0))],
        out_specs=row, out_shape=jax.ShapeDtypeStruct((B, S, W), BF16),
        compiler_params=_params(("parallel", "parallel")))(u, vn, w_s, b_sT)


def _gm_spatial_bwd(name, dyg, u, vn, w_s, b_sT):
    B, S, W = u.shape
    G, T = w_s.shape[0], w_s.shape[1]
    assert W == G * T, "a head group is as wide as a chunk is long"

    def body(dy_ref, u_ref, vn_ref, ws_ref, bs_ref, du_ref, dvn_ref, dws_ref, dbs_ref):
        first = jnp.logical_and(pl.program_id(0) == 0, pl.program_id(1) == 0)

        @pl.when(first)
        def _():
            dws_ref[...] = jnp.zeros_like(dws_ref)
            dbs_ref[...] = jnp.zeros_like(dbs_ref)

        tril = _iota2((T, T), 1) <= _iota2((T, T), 0)
        for g in range(G):
            cs = slice(g * T, (g + 1) * T)
            wm = _gm_causal_w(ws_ref, g)
            vng = vn_ref[0, :, cs]
            sv = jnp.dot(wm, vng, preferred_element_type=F32) + bs_ref[:, g:g + 1]
            dy = dy_ref[0, :, cs]
            du_ref[0, :, cs] = dy * sv
            dsv = dy * u_ref[0, :, cs]
            dsv16 = dsv.astype(BF16)
            dvn_ref[0, :, cs] = lax.dot_general(wm, dsv16, (((0,), (0,)), ((), ())), preferred_element_type=F32)
            dw = lax.dot_general(dsv16, vng, (((1,), (1,)), ((), ())), preferred_element_type=F32)
            dws_ref[g] += jnp.where(tril, dw, 0.0)
            dbs_ref[:, g:g + 1] += jnp.sum(dsv, axis=1, keepdims=True)

    row = pl.BlockSpec((1, T, W), lambda b, i: (b, i, 0))
    return pl.pallas_call(
        body, name=_unique(name), grid=(B, S // T),
        in_specs=[row, row, row, pl.BlockSpec((G, T, T), lambda b, i: (0, 0, 0)), pl.BlockSpec((T, G), lambda b, i: (0, 0))],
        out_specs=[row, row, pl.BlockSpec((G, T, T), lambda b, i: (0, 0, 0)), pl.BlockSpec((T, G), lambda b, i: (0, 0))],
        out_shape=[jax.ShapeDtypeStruct((B, S, W), F32), jax.ShapeDtypeStruct((B, S, W), F32),
                   jax.ShapeDtypeStruct((G, T, T), F32), jax.ShapeDtypeStruct((T, G), F32)],
        compiler_params=_params(("arbitrary", "arbitrary")))(dyg, u, vn, w_s, b_sT)


def _cv_glu(name, pw, b_in):
    W = pw.shape[2] // 2

    def fn(r, bv, cv):
        z = r[0] + cv[0]
        return [z[:, :W] * _sigmoid(z[:, W:])], [], []
    return _rowwise(name, fn, [pw], [], [b_in], out_rows=[(W, F32)])[0]


def _cv_glu_bwd(name, pw, dyg, b_in):
    W = pw.shape[2] // 2

    def fn(r, bv, cv):
        z = r[0] + cv[0]
        a, s = z[:, :W], _sigmoid(z[:, W:])
        dpw = jnp.concatenate([r[1] * s, r[1] * a * s * (1.0 - s)], axis=1)
        return [dpw], [], [_csum(dpw)]
    return _rowwise(name, fn, [pw, dyg], [], [b_in], out_rows=[(2 * W, BF16)], out_tsums=[2 * W])


def _cv_ln_act(name, yc, ln_g, ln_b):
    D = yc.shape[2]

    def fn(r, bv, cv):
        xhat, _ = _ln_stats(r[0])
        return [_silu(xhat * cv[0] + cv[1])], [], []
    return _rowwise(name, fn, [yc], [], [ln_g, ln_b], out_rows=[(D, BF16)])[0]


def _cv_ln_act_bwd(name, yc, dys, ln_g, ln_b):
    D = yc.shape[2]

    def fn(r, bv, cv):
        xhat, rstd = _ln_stats(r[0])
        dyn = r[1] * _dsilu(xhat * cv[0] + cv[1])
        dyc = _ln_bwd(dyn * cv[0], xhat, rstd)
        return [dyc], [], [_csum(dyn * xhat), _csum(dyn), _csum(dyc)]
    return _rowwise(name, fn, [yc, dys], [], [ln_g, ln_b], out_rows=[(D, F32)], out_tsums=[D, D, D])


CONV_HALO = 32
CONV_TS, CONV_TC = 256, 1024


def _dwconv(name, y, dw, dw_b):
    B, S, D = y.shape
    ts, tc, halo, K = _pick(S, CONV_TS, SUBLANES), _pick(D, CONV_TC), CONV_HALO, CONV_WIDTH

    def body(cur_ref, prev_ref, dw_ref, b_ref, o_ref, buf):
        i = pl.program_id(1)
        buf[pl.ds(0, halo), :] = jnp.where(i > 0, prev_ref[0, pl.ds(ts - halo, halo), :], 0.0)
        buf[pl.ds(halo, ts), :] = cur_ref[0]
        for c0 in range(0, tc, LANES):
            cs = slice(c0, c0 + LANES)
            acc = jnp.zeros((ts, LANES), F32) + b_ref[:, cs]
            for k in range(K):
                acc = acc + dw_ref[k:k + 1, cs] * buf[pl.ds(halo - (K - 1) + k, ts), cs]
            o_ref[0, :, cs] = acc

    return pl.pallas_call(
        body, name=_unique(name), grid=(B, S // ts, D // tc),
        in_specs=[pl.BlockSpec((1, ts, tc), lambda b, i, j: (b, i, j)),
                  pl.BlockSpec((1, ts, tc), lambda b, i, j: (b, jnp.maximum(i - 1, 0), j)),
                  pl.BlockSpec((halo, tc), lambda b, i, j: (0, j)), pl.BlockSpec((1, tc), lambda b, i, j: (0, j))],
        out_specs=pl.BlockSpec((1, ts, tc), lambda b, i, j: (b, i, j)),
        out_shape=jax.ShapeDtypeStruct((B, S, D), F32),
        scratch_shapes=[pltpu.VMEM((halo + ts, tc), F32)],
        compiler_params=_params(("parallel", "parallel", "parallel")))(y, y, dw, dw_b)


def _dwconv_bwd(name, dyc, y, dw):
    B, S, D = y.shape
    ts, tc, halo, K = _pick(S, CONV_TS, SUBLANES), _pick(D, CONV_TC), CONV_HALO, CONV_WIDTH
    nt = S // ts

    def body(g_ref, gnext_ref, y_ref, yprev_ref, dw_ref, dy_ref, ddw_ref, gbuf, ybuf):
        b, i = pl.program_id(1), pl.program_id(2)
        first = jnp.logical_and(b == 0, i == 0)

        @pl.when(first)
        def _():
            ddw_ref[...] = jnp.zeros_like(ddw_ref)

        g = g_ref[0]
        gbuf[pl.ds(0, ts), :] = g
        gbuf[pl.ds(ts, halo), :] = jnp.where(i < nt - 1, gnext_ref[0, pl.ds(0, halo), :], 0.0)
        ybuf[pl.ds(0, halo), :] = jnp.where(i > 0, yprev_ref[0, pl.ds(ts - halo, halo), :], 0.0)
        ybuf[pl.ds(halo, ts), :] = y_ref[0]
        for c0 in range(0, tc, LANES):
            cs = slice(c0, c0 + LANES)
            acc, gc = jnp.zeros((ts, LANES), F32), g[:, cs]
            for k in range(K):
                acc = acc + dw_ref[k:k + 1, cs] * gbuf[pl.ds(K - 1 - k, ts), cs]
                ddw_ref[k:k + 1, cs] += _csum(gc * ybuf[pl.ds(halo - (K - 1) + k, ts), cs])
            dy_ref[0, :, cs] = acc

    tile = lambda f: pl.BlockSpec((1, ts, tc), f)
    return pl.pallas_call(
        body, name=_unique(name), grid=(D // tc, B, nt),
        in_specs=[tile(lambda j, b, i: (b, i, j)), tile(lambda j, b, i: (b, jnp.minimum(i + 1, nt - 1), j)),
                  tile(lambda j, b, i: (b, i, j)), tile(lambda j, b, i: (b, jnp.maximum(i - 1, 0), j)),
                  pl.BlockSpec((halo, tc), lambda j, b, i: (0, j))],
        out_specs=[tile(lambda j, b, i: (b, i, j)), pl.BlockSpec((halo, tc), lambda j, b, i: (0, j))],
        out_shape=[jax.ShapeDtypeStruct((B, S, D), F32), jax.ShapeDtypeStruct((halo, D), F32)],
        scratch_shapes=[pltpu.VMEM((ts + halo, tc), F32), pltpu.VMEM((halo + ts, tc), F32)],
        compiler_params=_params(("parallel", "arbitrary", "arbitrary")))(dyc, dyc, y, y, dw)


ATT_BLOCK = 128
FOX_TILE = (512, 1024)
SB_TILE = (256, 512)
ATT_PIECE_ROWS = 32
FOX_GATE_COLS = 128


def _att_tiles(S, tile):
    return _pick(S, tile[0], SUBLANES), _pick(S, tile[1], LANES)


def _pieces(T, TK):
    R = min(T, ATT_PIECE_ROWS)
    segs = [slice(c, c + LANES) for c in range(0, TK, LANES)]
    return [(slice(r, r + R), segs) for r in range(0, T, R)]


def _piece_keep(row0, col0, rs, cs, strict, lane_major_of=0):
    shape = (rs.stop - rs.start, cs.stop - cs.start)
    lane = _iota2(shape, 1)
    key = col0 + (lane * lane_major_of + cs.start // LANES if lane_major_of else cs.start + lane)
    qry = row0 + rs.start + _iota2(shape, 0)
    return key < qry if strict else key <= qry


def _causal_tiles(i, tq, tk):
    return (i * tq + tq + tk - 1) // tk


def _lane_major(t, tk):
    B, H, S, dh = t.shape
    return t.reshape(B, H, S // tk, LANES, tk // LANES, dh).swapaxes(3, 4).reshape(B, H, S, dh)


def _lane_major_inverse(t, tk):
    B, H, S, dh = t.shape
    return t.reshape(B, H, S // tk, tk // LANES, LANES, dh).swapaxes(3, 4).reshape(B, H, S, dh)


def _log_sigmoid(x):
    return jnp.minimum(x, 0.0) - jnp.log(1.0 + jnp.exp(-jnp.abs(x)))


def _fox_gate_cumsum(name, fl, b_f):
    B, S, C = fl.shape
    T = _pick(S, ATT_BLOCK, SUBLANES)

    def body(fl_ref, bf_ref, f_ref, carry):
        @pl.when(pl.program_id(1) == 0)
        def _():
            carry[...] = jnp.zeros_like(carry)
        lf = _log_sigmoid(fl_ref[0] + bf_ref[...])
        lower = (_iota2((T, T), 1) <= _iota2((T, T), 0)).astype(BF16)
        f = _split_dot(lf, lower, lhs01=True, terms=3) + carry[...]
        f_ref[0] = f
        carry[...] = f[T - 1:T, :]

    return pl.pallas_call(
        body, name=_unique(name), grid=(B, S // T),
        in_specs=[pl.BlockSpec((1, T, C), lambda b, i: (b, i, 0)), pl.BlockSpec((1, C), lambda b, i: (0, 0))],
        out_specs=pl.BlockSpec((1, T, C), lambda b, i: (b, i, 0)),
        out_shape=jax.ShapeDtypeStruct((B, S, C), F32),
        scratch_shapes=[pltpu.VMEM((1, C), F32)],
        compiler_params=_params(("arbitrary", "arbitrary")))(fl, b_f)


def _fox_gate_bwd(name, dF, fl, b_f, n_heads):
    B, S, C = fl.shape
    T = _pick(S, ATT_BLOCK, SUBLANES)
    nt = S // T

    def body(df_ref, fl_ref, bf_ref, dfl_ref, dbf_ref, carry):
        first = jnp.logical_and(pl.program_id(0) == 0, pl.program_id(1) == 0)

        @pl.when(pl.program_id(1) == 0)
        def _():
            carry[...] = jnp.zeros_like(carry)

        @pl.when(first)
        def _():
            dbf_ref[...] = jnp.zeros_like(dbf_ref)

        upper = (_iota2((T, T), 1) >= _iota2((T, T), 0)).astype(BF16)
        dlf = _split_dot(df_ref[0], upper, lhs01=True, terms=3) + carry[...]
        carry[...] = dlf[0:1, :]
        x = fl_ref[0] + bf_ref[...]
        dfl = jnp.where(_iota2((T, C), 1) < n_heads, dlf * _sigmoid(-x), 0.0)
        dfl_ref[0] = dfl
        dbf_ref[...] += _csum(dfl)

    rev = lambda b, i: (b, nt - 1 - i, 0)
    return pl.pallas_call(
        body, name=_unique(name), grid=(B, nt),
        in_specs=[pl.BlockSpec((1, T, C), rev), pl.BlockSpec((1, T, C), rev), pl.BlockSpec((1, C), lambda b, i: (0, 0))],
        out_specs=[pl.BlockSpec((1, T, C), rev), pl.BlockSpec((1, C), lambda b, i: (0, 0))],
        out_shape=[jax.ShapeDtypeStruct((B, S, C), F32), jax.ShapeDtypeStruct((1, C), F32)],
        scratch_shapes=[pltpu.VMEM((1, C), F32)],
        compiler_params=_params(("arbitrary", "arbitrary")))(dF, fl, b_f)


_NT = (((1,), (1,)), ((), ()))
_TN = (((0,), (0,)), ((), ()))


def _fox_fwd(name, q, k, v, fq, fk):
    B, H, S, dh = q.shape
    T, TK = _att_tiles(S, FOX_TILE)
    scale = np.float32(dh ** -0.5)

    pieces = _pieces(T, TK)

    def body(q_ref, k_ref, v_ref, fq_ref, fk_ref, o_ref, lse_ref, s_scr, p_scr):
        i = pl.program_id(2)
        qb = q_ref[0, 0]
        n_tiles = _causal_tiles(i, T, TK)

        def tile(j, carry, diag):
            m, l, acc = carry
            ks = pl.ds(pl.multiple_of(j * TK, TK), TK)
            s_scr[...] = lax.dot_general(qb, k_ref[0, 0, ks, :], _NT, preferred_element_type=F32) * scale
            fkj = fk_ref[0, 0, pl.ds(j, 1), :]
            m_new = []
            for rc, (rs, segs) in enumerate(pieces):
                fq_c, mx = fq_ref[0, 0, rs, :], None
                for cs in segs:
                    s = s_scr[rs, cs] + fq_c - fkj[:, cs]
                    if diag:
                        s = jnp.where(_piece_keep(i * T, j * TK, rs, cs, False), s, NEG_INF)
                    s_scr[rs, cs] = s
                    mx = s if mx is None else jnp.maximum(mx, s)
                m_new.append(jnp.maximum(m[rc], jnp.max(mx, axis=1, keepdims=True)))
            alpha, l_new = [], []
            for rc, (rs, segs) in enumerate(pieces):
                alpha.append(jnp.exp(m[rc] - m_new[rc]))
                psum = None
                for cs in segs:
                    p = jnp.exp(s_scr[rs, cs] - m_new[rc])
                    p_scr[rs, cs] = p.astype(BF16)
                    psum = p if psum is None else psum + p
                l_new.append(alpha[rc] * l[rc] + jnp.sum(psum, axis=1, keepdims=True))
            acc = jnp.concatenate(alpha, axis=0) * acc + jnp.dot(p_scr[...], v_ref[0, 0, ks, :], preferred_element_type=F32)
            return tuple(m_new), tuple(l_new), acc

        init = (tuple(jnp.full((rs.stop - rs.start, 1), NEG_INF, F32) for rs, _ in pieces),
                tuple(jnp.zeros((rs.stop - rs.start, 1), F32) for rs, _ in pieces), jnp.zeros((T, dh), F32))
        carry = lax.fori_loop(0, n_tiles - 1, lambda j, c: tile(j, c, False), init)
        m, l, acc = tile(n_tiles - 1, carry, True)
        m, l = jnp.concatenate(m, axis=0), jnp.concatenate(l, axis=0)
        o_ref[0, 0] = (acc / l).astype(o_ref.dtype)
        lse_ref[0, 0] = m + jnp.log(l)

    full = lambda w: pl.BlockSpec((1, 1, S, w), lambda b, h, i: (b, h, 0, 0))
    blk = lambda w: pl.BlockSpec((1, 1, T, w), lambda b, h, i: (b, h, i, 0))
    return pl.pallas_call(
        body, name=_unique(name), grid=(B, H, S // T),
        in_specs=[blk(dh), full(dh), full(dh), blk(1), pl.BlockSpec((1, 1, S // TK, TK), lambda b, h, i: (b, h, 0, 0))],
        out_specs=[blk(dh), blk(1)],
        out_shape=[jax.ShapeDtypeStruct((B, H, S, dh), BF16), jax.ShapeDtypeStruct((B, H, S, 1), F32)],
        scratch_shapes=[pltpu.VMEM((T, TK), F32), pltpu.VMEM((T, TK), BF16)],
        compiler_params=_params(("parallel", "parallel", "parallel")))(q, k, v, fq, fk)


def _fox_bwd(name, q, k, v, fq, fk, do, lse):
    B, H, S, dh = q.shape
    T, TK = _att_tiles(S, FOX_TILE)
    nt, nkt = S // T, S // TK
    scale = np.float32(dh ** -0.5)

    pieces = _pieces(T, TK)

    def body(q_ref, k_ref, v_ref, fq_ref, fk_ref, do_ref, lse_ref, dq_ref, dk_ref, dv_ref, dfk_ref,
             p_buf, dp_buf, s_scr, ds_scr, p16_scr, dk_acc, dv_acc):
        dk_acc[...] = jnp.zeros_like(dk_acc)
        dv_acc[...] = jnp.zeros_like(dv_acc)
        dfk_ref[...] = jnp.zeros_like(dfk_ref)

        def qloop(i, _):
            qs = pl.ds(pl.multiple_of(i * T, T), T)
            qb, dob16 = q_ref[0, 0, qs, :], do_ref[0, 0, qs, :].astype(BF16)
            n_tiles = _causal_tiles(i, T, TK)
            row_at = lambda rs: pl.ds(pl.multiple_of(i * T + rs.start, SUBLANES), rs.stop - rs.start)
            fq_c = [fq_ref[0, 0, row_at(rs), :] for rs, _ in pieces]
            lse_c = [lse_ref[0, 0, row_at(rs), :] for rs, _ in pieces]

            def sweep1(j, delta, diag):
                ks = pl.ds(pl.multiple_of(j * TK, TK), TK)
                s_scr[...] = lax.dot_general(qb, k_ref[0, 0, ks, :], _NT, preferred_element_type=F32) * scale
                dp_buf[j] = lax.dot_general(dob16, v_ref[0, 0, ks, :], _NT, preferred_element_type=F32)
                fkj = fk_ref[0, 0, pl.ds(j, 1), :]
                out = []
                for rc, (rs, segs) in enumerate(pieces):
                    pdp = None
                    for cs in segs:
                        p = jnp.exp(((s_scr[rs, cs] + fq_c[rc]) - fkj[:, cs]) - lse_c[rc])
                        if diag:
                            p = jnp.where(_piece_keep(i * T, j * TK, rs, cs, False), p, 0.0)
                        p_buf[j, rs, cs] = p
                        pdp = p * dp_buf[j, rs, cs] if pdp is None else pdp + p * dp_buf[j, rs, cs]
                    out.append(delta[rc] + jnp.sum(pdp, axis=1, keepdims=True))
                return tuple(out)

            zeros = tuple(jnp.zeros((rs.stop - rs.start, 1), F32) for rs, _ in pieces)
            delta = lax.fori_loop(0, n_tiles - 1, lambda j, d: sweep1(j, d, False), zeros)
            delta = sweep1(n_tiles - 1, delta, True)

            def sweep2(j, dq):
                ks = pl.ds(pl.multiple_of(j * TK, TK), TK)
                col = [None] * len(pieces[0][1])
                for rc, (rs, segs) in enumerate(pieces):
                    for sg, cs in enumerate(segs):
                        p = p_buf[j, rs, cs]
                        ds = p * (dp_buf[j, rs, cs] - delta[rc])
                        ds_scr[rs, cs] = ds.astype(BF16)
                        p16_scr[rs, cs] = p.astype(BF16)
                        col[sg] = ds if col[sg] is None else col[sg] + ds
                dfk_ref[0, 0, pl.ds(j, 1), :] -= jnp.concatenate([_csum(c) for c in col], axis=1)
                ds16 = ds_scr[...]
                dk_acc[ks, :] += lax.dot_general(ds16, qb, _TN, preferred_element_type=F32)
                dv_acc[ks, :] += lax.dot_general(p16_scr[...], dob16, _TN, preferred_element_type=F32)
                return dq + jnp.dot(ds16, k_ref[0, 0, ks, :], preferred_element_type=F32)

            dq_ref[0, 0, qs, :] = (lax.fori_loop(0, n_tiles, sweep2, jnp.zeros((T, dh), F32)) * scale).astype(dq_ref.dtype)
            return 0

        lax.fori_loop(0, nt, qloop, 0)
        dk_ref[0, 0] = (dk_acc[...] * scale).astype(dk_ref.dtype)
        dv_ref[0, 0] = dv_acc[...].astype(dv_ref.dtype)

    full = lambda w: pl.BlockSpec((1, 1, S, w), lambda b, h: (b, h, 0, 0))
    fks = pl.BlockSpec((1, 1, nkt, TK), lambda b, h: (b, h, 0, 0))
    return pl.pallas_call(
        body, name=_unique(name), grid=(B, H),
        in_specs=[full(dh), full(dh), full(dh), full(1), fks, full(dh), full(1)],
        out_specs=[full(dh), full(dh), full(dh), fks],
        out_shape=[jax.ShapeDtypeStruct((B, H, S, dh), BF16)] * 3 + [jax.ShapeDtypeStruct((B, H, nkt, TK), F32)],
        scratch_shapes=[pltpu.VMEM((nkt, T, TK), F32), pltpu.VMEM((nkt, T, TK), F32), pltpu.VMEM((T, TK), F32),
                        pltpu.VMEM((T, TK), BF16), pltpu.VMEM((T, TK), BF16), pltpu.VMEM((S, dh), F32), pltpu.VMEM((S, dh), F32)],
        compiler_params=_params(("parallel", "parallel")))(q, k, v, fq, fk, do, lse)


def _sb_terms(z, with_sigmoids=True):
    t = jnp.exp(-jnp.abs(z))
    lp = jnp.log(1.0 + t)
    lb, l1 = jnp.minimum(z, 0.0) - lp, jnp.minimum(-z, 0.0) - lp
    if not with_sigmoids:
        return lb, l1, None, None
    return lb, l1, jnp.exp(lb), jnp.exp(l1)


SCAN_RADIX = 4


def _lane_scan(x, reverse):
    lane = _iota2(x.shape, 1)
    y, d = x, 1
    while d < LANES:
        step = y
        for m in range(1, SCAN_RADIX):
            if m * d < LANES:
                if reverse:
                    step = step + jnp.where(lane + m * d < LANES, pltpu.roll(y, LANES - m * d, 1), 0.0)
                else:
                    step = step + jnp.where(lane >= m * d, pltpu.roll(y, m * d, 1), 0.0)
        y, d = step, d * SCAN_RADIX
    return y


def _chunk_scan(xs, reverse):
    n = len(xs)
    within, acc = [None] * n, None
    for s in (range(n - 1, -1, -1) if reverse else range(n)):
        acc = xs[s] if acc is None else acc + xs[s]
        within[s] = acc
    lanes = _lane_scan(acc, reverse)
    beyond = lanes - acc
    return [w + beyond for w in within], (lanes[:, 0:1] if reverse else lanes[:, LANES - 1:LANES])


SB_DEAD = 110.0


def _sb_fwd(name, q, k, v):
    B, H, S, dh = q.shape
    T, TK = _att_tiles(S, SB_TILE)
    scale = np.float32(dh ** -0.5)
    pieces = _pieces(T, TK)

    def body(q_ref, k_ref, v_ref, o_ref, lt_ref, first_ref, z_scr, a_scr):
        i = pl.program_id(2)
        qb = q_ref[0, 0]
        n_tiles = _causal_tiles(i, T, TK)

        def tile(j, carry, diag):
            runs, acc = carry
            ks = pl.ds(pl.multiple_of(j * TK, TK), TK)
            z_scr[...] = lax.dot_general(qb, k_ref[0, 0, ks, :], _NT, preferred_element_type=F32) * scale
            new_runs = []
            for rc, (rs, segs) in enumerate(pieces):
                terms = [_sb_terms(z_scr[rs, cs], False) for cs in segs]
                keep = [_piece_keep(i * T, j * TK, rs, cs, True, len(segs)) if diag else None for cs in segs]
                l1 = [jnp.where(kp, t[1], 0.0) if diag else t[1] for kp, t in zip(keep, terms)]
                right_of, total = _chunk_scan(l1, True)
                for cs, kp, t, x, r in zip(segs, keep, terms, l1, right_of):
                    a = jnp.exp(t[0] + ((r - x) + runs[rc]))
                    a_scr[rs, cs] = (jnp.where(kp, a, 0.0) if diag else a).astype(BF16)
                new_runs.append(runs[rc] + total)
            return tuple(new_runs), acc + jnp.dot(a_scr[...], v_ref[0, 0, ks, :], preferred_element_type=F32)

        init = (tuple(jnp.zeros((rs.stop - rs.start, 1), F32) for rs, _ in pieces), jnp.zeros((T, dh), F32))
        def some_row_alive(runs):
            worst = runs[0]
            for r in runs[1:]:
                worst = jnp.maximum(worst, r)
            return jnp.max(worst) > -SB_DEAD

        def step(c):
            runs, acc = tile(n_tiles - 1 - c[0], (c[1], c[2]), False)
            return c[0] + 1, runs, acc

        visited, runs, acc = lax.while_loop(lambda c: jnp.logical_and(c[0] < n_tiles, some_row_alive(c[1])), step,
                                            (jnp.int32(1), *tile(n_tiles - 1, init, True)))
        o_ref[0, 0] = acc.astype(o_ref.dtype)
        lt_ref[0, 0] = jnp.concatenate(runs, axis=0)
        first_ref[pl.program_id(0), pl.program_id(1), i] = (n_tiles - visited).astype(F32)

    full = lambda w: pl.BlockSpec((1, 1, S, w), lambda b, h, i: (b, h, 0, 0))
    blk = lambda w: pl.BlockSpec((1, 1, T, w), lambda b, h, i: (b, h, i, 0))
    return pl.pallas_call(
        body, name=_unique(name), grid=(B, H, S // T),
        in_specs=[blk(dh), full(dh), full(dh)], out_specs=[blk(dh), blk(1), pl.BlockSpec(memory_space=pltpu.SMEM)],
        out_shape=[jax.ShapeDtypeStruct((B, H, S, dh), BF16), jax.ShapeDtypeStruct((B, H, S, 1), F32),
                   jax.ShapeDtypeStruct((B, H, S // T), F32)],
        scratch_shapes=[pltpu.VMEM((T, TK), F32), pltpu.VMEM((T, TK), BF16)],
        compiler_params=_params(("arbitrary", "arbitrary", "arbitrary")))(q, k, v)


def _sb_bwd(name, q, k, v, do, lt, first):
    B, H, S, dh = q.shape
    T, TK = _att_tiles(S, SB_TILE)
    nt = S // T
    scale = np.float32(dh ** -0.5)

    pieces = _pieces(T, TK)

    def body(first_ref, q_ref, k_ref, v_ref, do_ref, lt_ref, dq_ref, dk_ref, dv_ref, z_scr, da_scr, dz_scr, a_scr, dk_acc, dv_acc):
        dk_acc[...] = jnp.zeros_like(dk_acc)
        dv_acc[...] = jnp.zeros_like(dv_acc)
        b, h = pl.program_id(0), pl.program_id(1)

        def qloop(i, _):
            qs = pl.ds(pl.multiple_of(i * T, T), T)
            qb, dob16 = q_ref[0, 0, qs, :], do_ref[0, 0, qs, :].astype(BF16)
            n_tiles = _causal_tiles(i, T, TK)
            first_tile = jnp.clip(first_ref[b, h, i].astype(jnp.int32), 0, n_tiles - 1)
            lt_c = [lt_ref[0, 0, pl.ds(pl.multiple_of(i * T + rs.start, SUBLANES), rs.stop - rs.start), :] for rs, _ in pieces]

            def tile(j, carry, diag):
                sums_l, sums_e, dq = carry
                ks = pl.ds(pl.multiple_of(j * TK, TK), TK)
                kb, vb = k_ref[0, 0, ks, :], v_ref[0, 0, ks, :]
                z_scr[...] = lax.dot_general(qb, kb, _NT, preferred_element_type=F32) * scale
                da_scr[...] = lax.dot_general(dob16, vb, _NT, preferred_element_type=F32)
                new_l, new_e = [], []
                for rc, (rs, segs) in enumerate(pieces):
                    terms = [_sb_terms(z_scr[rs, cs]) for cs in segs]
                    keep = [_piece_keep(i * T, j * TK, rs, cs, True, len(segs)) if diag else None for cs in segs]
                    l1 = [jnp.where(kp, t[1], 0.0) if diag else t[1] for kp, t in zip(keep, terms)]
                    upto, total_l = _chunk_scan(l1, False)
                    es = []
                    for cs, kp, t, u in zip(segs, keep, terms, upto):
                        a = jnp.exp(t[0] + (lt_c[rc] - (u + sums_l[rc])))
                        if diag:
                            a = jnp.where(kp, a, 0.0)
                        a_scr[rs, cs] = a.astype(BF16)
                        es.append(da_scr[rs, cs] * a)
                    e_upto, total_e = _chunk_scan(es, False)
                    for cs, kp, t, e, eu in zip(segs, keep, terms, es, e_upto):
                        dz = e * t[3] - ((eu - e) + sums_e[rc]) * t[2]
                        dz_scr[rs, cs] = (jnp.where(kp, dz, 0.0) if diag else dz).astype(BF16)
                    new_l.append(sums_l[rc] + total_l)
                    new_e.append(sums_e[rc] + total_e)
                dz16 = dz_scr[...]
                dk_acc[ks, :] += lax.dot_general(dz16, qb, _TN, preferred_element_type=F32)
                dv_acc[ks, :] += lax.dot_general(a_scr[...], dob16, _TN, preferred_element_type=F32)
                return tuple(new_l), tuple(new_e), dq + jnp.dot(dz16, kb, preferred_element_type=F32)

            zeros = tuple(jnp.zeros((rs.stop - rs.start, 1), F32) for rs, _ in pieces)
            carry = lax.fori_loop(first_tile, n_tiles - 1, lambda j, c: tile(j, c, False), (zeros, zeros, jnp.zeros((T, dh), F32)))
            dq_ref[0, 0, qs, :] = (tile(n_tiles - 1, carry, True)[2] * scale).astype(dq_ref.dtype)
            return 0

        lax.fori_loop(0, nt, qloop, 0)
        dk_ref[0, 0] = (dk_acc[...] * scale).astype(dk_ref.dtype)
        dv_ref[0, 0] = dv_acc[...].astype(dv_ref.dtype)

    full = lambda w: pl.BlockSpec((1, 1, S, w), lambda b, h: (b, h, 0, 0))
    return pl.pallas_call(
        body, name=_unique(name), grid=(B, H),
        in_specs=[pl.BlockSpec(memory_space=pltpu.SMEM), full(dh), full(dh), full(dh), full(dh), full(1)], out_specs=[full(dh)] * 3,
        out_shape=[jax.ShapeDtypeStruct((B, H, S, dh), BF16)] * 3,
        scratch_shapes=[pltpu.VMEM((T, TK), F32), pltpu.VMEM((T, TK), F32), pltpu.VMEM((T, TK), BF16), pltpu.VMEM((T, TK), BF16),
                        pltpu.VMEM((S, dh), F32), pltpu.VMEM((S, dh), F32)],
        compiler_params=_params(("parallel", "parallel")))(first, q, k, v, do, lt)


def _adamw(name, w, g, m, v):
    shape = w.shape
    n = w.size
    cols = shape[-1] if (w.ndim >= 2 and (shape[-1] % LANES == 0 or n // shape[-1] >= LANES)) else 0
    if cols:
        prep = lambda t: t.reshape(1, n // cols, cols)
    else:
        cols = LANES
        pad = (-n) % (SUBLANES * LANES)
        prep = lambda t: jnp.pad(t.reshape(-1), (0, pad), constant_values=1.0).reshape(1, (n + pad) // cols, cols)

    def fn(r, bv, cv):
        w_, g_, m_, v_ = r
        m2 = ADAM_B1 * m_ + (1.0 - ADAM_B1) * g_
        v2 = ADAM_B2 * v_ + (1.0 - ADAM_B2) * (g_ * g_)
        m_hat = m2 / (1.0 - ADAM_B1 ** ADAM_STEP)
        v_hat = v2 / (1.0 - ADAM_B2 ** ADAM_STEP)
        return [-ADAM_LR * (m_hat / (jnp.sqrt(v_hat) + ADAM_EPS) + ADAM_WD * w_), m2, v2], [], []
    outs = _rowwise(name, fn, [prep(w), prep(g), prep(m), prep(v)], out_rows=[(cols, F32)] * 3, tm=512)
    return [o.reshape(-1)[:n].reshape(shape) for o in outs]


def _sum8(name, parts):
    def fn(r, bv, cv):
        s = r[0]
        for t in r[1:]:
            s = s + t
        return [s], [], []
    rows = [parts[i][None] for i in range(parts.shape[0])]
    return _rowwise(name, fn, rows, out_rows=[(parts.shape[2], F32)])[0][0]


def _pack(arrs, cols, dtype, row_mult):
    flat = jnp.concatenate([a.reshape(-1).astype(dtype) for a in arrs])
    pad = (-flat.size) % (cols * row_mult)
    return jnp.pad(flat, (0, pad)).reshape(-1, cols)


def _unpack(flat, shapes):
    out, off = [], 0
    for s in shapes:
        n = int(np.prod(s))
        out.append(flat[off:off + n].reshape(s))
        off += n
    return out


def _heads(t, H):
    B, S, W = t.shape
    return t.reshape(B, S, H, W // H).transpose(0, 2, 1, 3)


def _unheads(t):
    B, H, S, dh = t.shape
    return t.transpose(0, 2, 1, 3).reshape(B, S, H * dh)


def kernel(*args):
    _names_used.clear()
    p = dict(zip(ARGS, args))
    x, target = p['x'], p['loss_target']
    B, S, D = x.shape
    T = B * S
    depth = p['ln1_g'].shape[0]
    H = D // HEAD_DIM
    alpha = np.float32((2.0 * depth) ** 0.25)
    cx, cy, cc = _mesh_pos()
    my_q = 2 * cx + cy
    axes = ("x", "y", "c")
    two = lambda t: t.reshape(T, t.shape[-1])
    three = lambda t: t.reshape(B, S, t.shape[-1])

    small_in = [p['c']] + [p[n] for n in SMALL_SPLIT]
    g1 = _all_gather8("ag_small", [_pack(small_in, LANES, F32, SUBLANES)])[0]
    g1 = g1.reshape(N_DEV, -1)
    c_all = g1[:, :B * D].reshape(N_DEV * B, D)
    per_chip = [_unpack(g1[2 * q], [a.shape for a in small_in])[1:] for q in range(N_CHIPS)]
    small = {n: jnp.concatenate([per_chip[q][i] for q in range(N_CHIPS)], axis=-1) for i, n in enumerate(SMALL_SPLIT)}
    for n in SMALL_REPL:
        small[n] = p[n]

    n_seq = N_DEV * B
    seq_pad = -(-n_seq // LANES) * LANES
    c_act = _rowwise("c_act", lambda r, bv, cv: ([_silu(r[0])], [], []),
                     [jnp.pad(c_all, ((0, seq_pad - n_seq), (0, 0)))[None]], out_rows=[(D, F32)])[0][0]
    mod_cols = p['mod_w'].shape[2]
    mod_part = jnp.stack([_mm(f"mod_fwd{l}", c_act, p['mod_w'][l])[:n_seq] for l in range(depth)])
    half_layers = depth // 2
    mod_half = lax.dynamic_slice_in_dim(mod_part, cc * half_layers, half_layers, axis=0)
    gm_ = _all_gather8("ag_mod", [mod_half.reshape(half_layers * n_seq, mod_cols)])[0]
    mod_all = gm_.reshape(N_CHIPS, 2, half_layers, n_seq, mod_cols).transpose(1, 2, 3, 0, 4).reshape(depth, n_seq, 6 * D)
    mod_mine = lax.dynamic_slice_in_dim(mod_all, (2 * my_q + cc) * B, B, axis=1)
    mod = _rowwise("mod_bias", lambda r, bv, cv: ([r[0] + bv[0]], [], []), [mod_mine], [p['mod_b'][:, None, :]],
                   out_rows=[(6 * D, F32)])[0]
    mods = [[mod[l, :, None, i * D:(i + 1) * D] for i in range(6)] for l in range(depth)]

    big_names = list(BIG)
    shard_shapes = [p[n].shape for n in big_names]
    half_rows = [s[0] * s[1] // 2 for s in shard_shapes]
    w_halves = [lax.dynamic_slice_in_dim(p[n].reshape(-1, s[2]), cc * hr, hr, axis=0).astype(BF16)
                for n, s, hr in zip(big_names, shard_shapes, half_rows)]
    W = {}
    for n, s, g in zip(big_names, shard_shapes, _all_gather8("ag_weights", w_halves)):
        seg = g.reshape((N_CHIPS,) + s)
        W[n] = jnp.concatenate([seg[q] for q in range(N_CHIPS)], axis=BIG[n])

    def vec(n, j):
        return small[n][j][None, :]

    def attn_proj(h1, w_in, gate_cols):
        wp = jnp.pad(w_in, ((0, 0), (0, gate_cols))) if gate_cols else w_in
        proj = three(_mm("att_proj", two(h1), wp))
        q, k, v = [_heads(proj[..., i * D:(i + 1) * D].astype(BF16), H) for i in range(3)]
        return wp, proj, q, k, v

    def gm_fwd(j, h1):
        zin = three(_mm("gm_in", two(h1), W['gm_w_in'][j]))
        u, vn = _gm_act("gm_act", zin, vec('gm_b_in', j), vec('gm_ln_g', j), vec('gm_ln_b', j))
        b_sT = small['gm_b_s'][j].T
        yg = _gm_spatial("gm_spatial", u, vn, small['gm_w_s'][j], b_sT)
        return three(_mm("gm_out", two(yg), W['gm_w_out'][j])), (zin, u, vn, b_sT, yg)

    def gm_bwd(j, h1, dy1, cache):
        zin, u, vn, b_sT, yg = cache
        g = {'gm_w_out': _mm("gm_dwout", two(yg), two(dy1), ta=True)}
        dyg = three(_mm("gm_dyg", two(dy1), W['gm_w_out'][j], tb=True))
        du, dvn, dws, dbsT = _gm_spatial_bwd("gm_spatial_bwd", dyg, u, vn, small['gm_w_s'][j], b_sT)
        dzin, g['gm_b_in'], g['gm_ln_g'], g['gm_ln_b'] = _gm_act_bwd("gm_act_bwd", zin, du, dvn, vec('gm_b_in', j), vec('gm_ln_g', j))
        g['gm_w_s'], g['gm_b_s'] = dws, dbsT.T
        g['gm_w_in'] = _mm("gm_dwin", two(h1), two(dzin), ta=True)
        return _mm("gm_dh", two(dzin), W['gm_w_in'][j], tb=True), g

    def fox_fwd(j, h1):
        wp, proj, q, k, v = attn_proj(h1, W['fox_w_in'][j], 3 * D + FOX_GATE_COLS - W['fox_w_in'].shape[2])
        fl = proj[..., 3 * D:]
        bf = jnp.pad(small['fox_b_f'][j][None, :], ((0, 0), (0, FOX_GATE_COLS - H)))
        Fh = _fox_gate_cumsum("fox_gate", fl, bf)[..., :H].transpose(0, 2, 1)
        fq, fk = Fh[..., None], Fh.reshape(B, H, -1, _att_tiles(S, FOX_TILE)[1])
        o, lse = _fox_fwd("fox_fwd", q, k, v, fq, fk)
        o2 = _unheads(o)
        return three(_mm("fox_out", two(o2), W['fox_w_out'][j])), (wp, q, k, v, fl, bf, fq, fk, lse, o2)

    def fox_bwd(j, h1, dy1, cache):
        wp, q, k, v, fl, bf, fq, fk, lse, o2 = cache
        g = {'fox_w_out': _mm("fox_dwout", two(o2), two(dy1), ta=True)}
        do = _heads(three(_mm("fox_do", two(dy1), W['fox_w_out'][j], tb=True, out_dtype=BF16)), H)
        dq, dk, dv, dfk = _fox_bwd("fox_bwd", q, k, v, fq, fk, do, lse)
        dF = jnp.pad(dfk.reshape(B, H, S).transpose(0, 2, 1), ((0, 0), (0, 0), (0, FOX_GATE_COLS - H)))
        dfl, dbf = _fox_gate_bwd("fox_gate_bwd", dF, fl, bf, H)
        dproj = jnp.concatenate([_unheads(dq).astype(BF16), _unheads(dk).astype(BF16), _unheads(dv).astype(BF16),
                                 dfl.astype(BF16)], axis=-1)
        g['fox_w_in'] = _mm("fox_dwin", two(h1), two(dproj), ta=True)[:, :W['fox_w_in'].shape[2]]
        g['fox_b_f'] = dbf[0, :H]
        return _mm("fox_dh", two(dproj), wp, tb=True), g

    def sb_fwd(j, h1):
        wp, proj, q, k, v = attn_proj(h1, W['sb_w_in'][j], 0)
        k, v = _lane_major(k, _att_tiles(S, SB_TILE)[1]), _lane_major(v, _att_tiles(S, SB_TILE)[1])
        o, lt, first = _sb_fwd("sb_fwd", q, k, v)
        o2 = _unheads(o)
        return three(_mm("sb_out", two(o2), W['sb_w_out'][j])), (q, k, v, lt, first, o2)

    def sb_bwd(j, h1, dy1, cache):
        q, k, v, lt, first, o2 = cache
        g = {'sb_w_out': _mm("sb_dwout", two(o2), two(dy1), ta=True)}
        do = _heads(three(_mm("sb_do", two(dy1), W['sb_w_out'][j], tb=True, out_dtype=BF16)), H)
        dq, dk, dv = _sb_bwd("sb_bwd", q, k, v, do, lt, first)
        dk, dv = _lane_major_inverse(dk, _att_tiles(S, SB_TILE)[1]), _lane_major_inverse(dv, _att_tiles(S, SB_TILE)[1])
        dproj = jnp.concatenate([_unheads(dq).astype(BF16), _unheads(dk).astype(BF16), _unheads(dv).astype(BF16)], axis=-1)
        g['sb_w_in'] = _mm("sb_dwin", two(h1), two(dproj), ta=True)
        return _mm("sb_dh", two(dproj), W['sb_w_in'][j], tb=True), g

    def cv_fwd(j, h1):
        pw = three(_mm("cv_in", two(h1), W['cv_w_in'][j]))
        ygl = _cv_glu("cv_glu", pw, vec('cv_b_in', j))
        dw = jnp.pad(small['cv_dw'][j], ((0, CONV_HALO - CONV_WIDTH), (0, 0)))
        yc = _dwconv("cv_dwconv", ygl, dw, vec('cv_dw_b', j))
        ys = _cv_ln_act("cv_ln_act", yc, vec('cv_ln_g', j), vec('cv_ln_b', j))
        return three(_mm("cv_out", two(ys), W['cv_w_out'][j])), (pw, ygl, dw, yc, ys)

    def cv_bwd(j, h1, dy1, cache):
        pw, ygl, dw, yc, ys = cache
        g = {'cv_w_out': _mm("cv_dwout", two(ys), two(dy1), ta=True)}
        dys = three(_mm("cv_dys", two(dy1), W['cv_w_out'][j], tb=True))
        dyc, g['cv_ln_g'], g['cv_ln_b'], g['cv_dw_b'] = _cv_ln_act_bwd("cv_ln_act_bwd", yc, dys, vec('cv_ln_g', j), vec('cv_ln_b', j))
        dygl, ddw = _dwconv_bwd("cv_dwconv_bwd", dyc, ygl, dw)
        g['cv_dw'] = ddw[:CONV_WIDTH]
        dpw, g['cv_b_in'] = _cv_glu_bwd("cv_glu_bwd", pw, dygl, vec('cv_b_in', j))
        g['cv_w_in'] = _mm("cv_dwin", two(h1), two(dpw), ta=True)
        return _mm("cv_dh", two(dpw), W['cv_w_in'][j], tb=True), g

    mixers = [(gm_fwd, gm_bwd), (fox_fwd, fox_bwd), (sb_fwd, sb_bwd), (cv_fwd, cv_bwd)]
    n_mix = len(mixers)

    saved = []
    h1 = _modulate("mod1", x, mods[0][1], mods[0][0])
    for l in range(depth):
        m, j = l % n_mix, l // n_mix
        sh1, sc1, g1_, sh2, sc2, g2_ = mods[l]
        ybias = vec('cv_b_out', j) if m == 3 else None
        y1, cache = mixers[m][0](j, h1)
        xm, h2 = _resid_ln("resid_ln1", alpha, x, y1, g1_, small['ln1_g'][l][None], small['ln1_b'][l][None], ybias, then=(sc2, sh2))
        *z, a = _ffn_in_act("ffn_in", two(h2), W['ffn_w_in'][l])
        y2 = three(_mm("ffn_out", a, W['ffn_w_out'][l]))
        nxt = (mods[l + 1][1], mods[l + 1][0]) if l + 1 < depth else None
        xo, *h_next = _resid_ln("resid_ln2", alpha, xm, y2, g2_, small['ln2_g'][l][None], small['ln2_b'][l][None], then=nxt)
        saved.append((x, h1, y1, cache, xm, h2, z, a, y2, ybias))
        x, h1 = xo, (h_next[0] if h_next else None)

    dx, sq = _loss_head("loss_head", x, target)
    loss = lax.psum(jnp.sum(sq) * np.float32(0.5 / D), axes)

    grads = {n: [None] * p[n].shape[0] for n in WEIGHTS}
    parts = [dict() for _ in range(depth)]
    after = None
    for l in reversed(range(depth)):
        m, j = l % n_mix, l // n_mix
        sh1, sc1, g1_, sh2, sc2, g2_ = mods[l]
        x_in, h1, y1, cache, xm, h2, z, a, y2, ybias = saved[l]
        ln2 = (small['ln2_g'][l][None], small['ln2_b'][l][None])
        if after is None:
            dr2, dy2, parts[l]['g2'], grads['ln2_g'][l], grads['ln2_b'][l], _ = _resid_ln_bwd("resid_ln2_bwd", alpha, dx, xm, y2, g2_, *ln2)
        else:
            (dr2, dy2, parts[l]['g2'], parts[l + 1]['sc1'], parts[l + 1]['sh1'], grads['ln2_g'][l], grads['ln2_b'][l], _) = _resid_ln_bwd(
                "resid_ln2_bwd", alpha, after[0], xm, y2, g2_, *ln2, then=after[1:])
        grads['ffn_w_out'][l] = _mm("ffn_dwout", a, two(dy2), ta=True)
        dz = _ffn_out_bwd_act("ffn_da", two(dy2), W['ffn_w_out'][l], *z)
        grads['ffn_w_in'][l] = jnp.concatenate([_mm("ffn_dwin", two(h2), t, ta=True) for t in dz], axis=1)
        dh2 = three(_ffn_in_bwd("ffn_dh", *dz, W['ffn_w_in'][l]))
        (dr1, dy1, parts[l]['g1'], parts[l]['sc2'], parts[l]['sh2'], grads['ln1_g'][l], grads['ln1_b'][l], dyb) = _resid_ln_bwd(
            "resid_ln1_bwd", alpha, dr2, x_in, y1, g1_, small['ln1_g'][l][None], small['ln1_b'][l][None], ybias, then=(dh2, sc2))
        dh1, mg = mixers[m][1](j, h1, dy1, cache)
        if m == 3:
            mg['cv_b_out'] = dyb
        for n, gval in mg.items():
            grads[n][j] = gval
        after = (dr1, three(dh1), sc1)
    grad_x, parts[0]['sc1'], parts[0]['sh1'] = _modulate_bwd("mod1_bwd", alpha, after[1], after[0], saved[0][0], after[2])
    dmod = [jnp.concatenate([pt[k] for k in ('sh1', 'sc1', 'g1', 'sh2', 'sc2', 'g2')], axis=-1)[:, 0, :] for pt in parts]
    dmod = jnp.stack(dmod)
    grads['mod_b'] = [jnp.sum(dmod[l], axis=0) for l in range(depth)]
    full_shape = {n: tuple(t.shape) for n, t in small.items()}

    small_names = SMALL_REPL + SMALL_SPLIT
    small_parts = [jnp.stack([gv.reshape(full_shape[n][1:]) for gv in grads[n]]) for n in small_names]
    pack_a = _pack([dmod], LANES, F32, SUBLANES)
    pack_b = _pack(small_parts, LANES, F32, SUBLANES)
    g2 = _all_gather8("ag_grads_small", [jnp.concatenate([pack_a, pack_b], axis=0)])[0]
    rows_a = pack_a.shape[0]
    dmod_all = g2[:, :rows_a].reshape(N_DEV, -1)[:, :dmod.size].reshape(N_DEV, depth, B, 6 * D)
    dmod_all = dmod_all.transpose(1, 0, 2, 3).reshape(depth, n_seq, 6 * D)
    small_sum = _sum8("sum_grads_small", g2[:, rows_a:]).reshape(-1)
    g_small = dict(zip(small_names, _unpack(small_sum, [full_shape[n] for n in small_names])))
    for n in SMALL_SPLIT:
        w = p[n].shape[-1]
        g_small[n] = lax.dynamic_slice_in_dim(g_small[n], my_q * w, w, axis=g_small[n].ndim - 1)

    dm_cols = lax.dynamic_slice_in_dim(dmod_all, my_q * mod_cols, mod_cols, axis=2)
    dm_cols = jnp.pad(dm_cols, ((0, 0), (0, seq_pad - n_seq), (0, 0)))
    g_mod_w = jnp.stack([_mm(f"mod_dw{l}", c_act, dm_cols[l], ta=True) for l in range(depth)])

    keep, give = [], []
    for n, s, hr in zip(big_names, shard_shapes, half_rows):
        gfull = jnp.stack(grads[n])
        g4 = jnp.stack(jnp.split(gfull, N_CHIPS, axis=BIG[n])).reshape(N_CHIPS, 2 * hr, s[2])
        keep.append(lax.dynamic_slice_in_dim(g4, cc * hr, hr, axis=1))
        give.append(lax.dynamic_slice_in_dim(g4, (1 - cc) * hr, hr, axis=1))
    got = _sibling_exchange("rs_sibling", give)
    chip_sum = [_rowwise("rs_add_sibling", lambda r, bv, cv: ([r[0] + r[1]], [], []),
                         [a.reshape(1, -1, a.shape[2]), b.reshape(1, -1, a.shape[2])],
                         out_rows=[(a.shape[2], BF16)], tm=512)[0].reshape(a.shape) for a, b in zip(keep, got)]
    from_chips = _chip_all_to_all("rs_chips", chip_sum)
    half_sum = [_rowwise("rs_add_chips", lambda r, bv, cv: ([((r[0] + r[1]) + r[2]) + r[3]], [], []),
                         [t[q][None] for q in range(N_CHIPS)], out_rows=[(t.shape[2], F32)], tm=512)[0][0]
                for t in from_chips]
    other = _sibling_exchange("rs_share", half_sum)
    g_big = {n: jnp.concatenate([jnp.where(cc == 0, a, b), jnp.where(cc == 0, b, a)], axis=0).reshape(s)
             for n, s, a, b in zip(big_names, shard_shapes, half_sum, other)}

    g_out = {**g_small, **g_big, 'mod_w': g_mod_w}
    upd = {n: _adamw("adamw_" + n, p[n], g_out[n], p['m_' + n], p['v_' + n]) for n in WEIGHTS}
    return (loss, grad_x, *[g_out[n] for n in WEIGHTS], *[upd[n][0] for n in WEIGHTS],
            *[upd[n][1] for n in WEIGHTS], *[upd[n][2] for n in WEIGHTS])
```

```python
import math

import jax
import jax.numpy as jnp
import numpy as np
from jax import lax
from jax.experimental import pallas as pl
from jax.experimental.pallas import tpu as pltpu

F32, BF16 = jnp.float32, jnp.bfloat16

HEAD_DIM = 64
CONV_WIDTH = 31
LN_EPS = 1e-5
NEG_INF = -1e30
ADAM_LR, ADAM_B1, ADAM_B2, ADAM_EPS, ADAM_WD, ADAM_STEP = 0.001, 0.9, 0.999, 1e-08, 0.01, 10

LANES = 128
SUBLANES = 8
VMEM_LIMIT_BYTES = 56 * 1024 * 1024
N_CHIPS = 4
N_DEV = 8

WEIGHTS = ['mod_w', 'mod_b', 'ln1_g', 'ln1_b', 'ln2_g', 'ln2_b', 'ffn_w_in', 'ffn_w_out', 'gm_w_in', 'gm_b_in',
           'gm_ln_g', 'gm_ln_b', 'gm_w_s', 'gm_b_s', 'gm_w_out', 'fox_w_in', 'fox_b_f', 'fox_w_out', 'sb_w_in',
           'sb_w_out', 'cv_w_in', 'cv_b_in', 'cv_dw', 'cv_dw_b', 'cv_ln_g', 'cv_ln_b', 'cv_w_out', 'cv_b_out']
ARGS = ['x', 'c'] + WEIGHTS + ['loss_target'] + ['m_' + n for n in WEIGHTS] + ['v_' + n for n in WEIGHTS]
BIG = {'ffn_w_in': 2, 'ffn_w_out': 1, 'gm_w_in': 2, 'gm_w_out': 1, 'fox_w_in': 2, 'fox_w_out': 1,
       'sb_w_in': 2, 'sb_w_out': 1, 'cv_w_in': 2, 'cv_w_out': 1}
SMALL_SPLIT = ['cv_b_in', 'cv_dw', 'cv_dw_b', 'cv_ln_g', 'cv_ln_b', 'cv_b_out']
SMALL_REPL = ['mod_b', 'ln1_g', 'ln1_b', 'ln2_g', 'ln2_b', 'gm_b_in', 'gm_ln_g', 'gm_ln_b', 'gm_w_s', 'gm_b_s', 'fox_b_f']
PACK_COLS = 1024


_names_used = {}


def _unique(name):
    k = _names_used.get(name, 0)
    _names_used[name] = k + 1
    return name if k == 0 else f"{name}_{k}"


def _params(sem):
    return pltpu.CompilerParams(dimension_semantics=sem, vmem_limit_bytes=VMEM_LIMIT_BYTES)


def _pick(dim, pref, mult=LANES):
    if dim <= pref:
        return dim
    best = 0
    for t in range(mult, pref + 1, mult):
        if dim % t == 0:
            best = t
    assert best, (dim, pref)
    return best


def _mesh_pos():
    return lax.axis_index("x"), lax.axis_index("y"), lax.axis_index("c")


AG_COPIES = 7
A2A_COPIES = 3


def _comm_call(name, body, blks, out_shapes, n_sems):
    n = len(blks)
    hbm = pl.BlockSpec(memory_space=pl.ANY)
    return pl.pallas_call(
        body, name=_unique(name), out_shape=out_shapes, in_specs=[hbm] * n, out_specs=[hbm] * n,
        scratch_shapes=[pltpu.SemaphoreType.DMA((n_sems * n,)), pltpu.SemaphoreType.DMA((n_sems * n,)),
                        pltpu.SemaphoreType.DMA((n,))],
    )(*blks)


def _all_gather8(name, blks):
    n = len(blks)

    def body(*refs):
        x_refs, out_refs, (send_sems, recv_sems, local_sems) = refs[:n], refs[n:2 * n], refs[2 * n:]
        x, y, c = _mesh_pos()
        me, sibling = (x, y, c), (x, y, 1 - c)
        chips = [(1 - x, y), (x, 1 - y), (1 - x, 1 - y)]

        def copy(a, k, block, to, from_input=False):
            px, py, pc = block
            slot = out_refs[a].at[4 * px + 2 * py + pc]
            return pltpu.make_async_remote_copy(
                src_ref=x_refs[a] if from_input else slot, dst_ref=slot,
                send_sem=send_sems.at[AG_COPIES * a + k], recv_sem=recv_sems.at[AG_COPIES * a + k],
                device_id=to, device_id_type=pl.DeviceIdType.MESH)

        local, sent = [], []
        for a in range(n):
            local.append(pltpu.make_async_copy(x_refs[a], out_refs[a].at[4 * x + 2 * y + c], local_sems.at[a]))
            local[-1].start()
            first = [copy(a, 0, me, sibling, True)] + [copy(a, 1 + j, me, (*chip, c), True) for j, chip in enumerate(chips)]
            for cp in first:
                cp.start()
            sent += first
        for a in range(n):
            for j, chip in enumerate(chips):
                copy(a, 1 + j, (*chip, c), me).wait_recv()
                sent.append(copy(a, 4 + j, (*chip, c), sibling))
                sent[-1].start()
        for a in range(n):
            copy(a, 0, sibling, me).wait_recv()
            for j, chip in enumerate(chips):
                copy(a, 4 + j, (*chip, 1 - c), me).wait_recv()
        for cp in sent:
            cp.wait_send()
        for cp in local:
            cp.wait()

    return _comm_call(name, body, blks, [jax.ShapeDtypeStruct((N_DEV,) + b.shape, b.dtype) for b in blks], AG_COPIES)


def _sibling_exchange(name, blks):
    n = len(blks)

    def body(*refs):
        x_refs, out_refs, (send_sems, recv_sems, _) = refs[:n], refs[n:2 * n], refs[2 * n:]
        x, y, c = _mesh_pos()
        cps = [pltpu.make_async_remote_copy(src_ref=x_refs[a], dst_ref=out_refs[a], send_sem=send_sems.at[a],
                                            recv_sem=recv_sems.at[a], device_id=(x, y, 1 - c),
                                            device_id_type=pl.DeviceIdType.MESH) for a in range(n)]
        for cp in cps:
            cp.start()
        for cp in cps:
            cp.wait()

    return _comm_call(name, body, blks, [jax.ShapeDtypeStruct(b.shape, b.dtype) for b in blks], 1)


def _chip_all_to_all(name, blks):
    n = len(blks)

    def body(*refs):
        x_refs, out_refs, (send_sems, recv_sems, local_sems) = refs[:n], refs[n:2 * n], refs[2 * n:]
        x, y, c = _mesh_pos()
        chips = [(1 - x, y), (x, 1 - y), (1 - x, 1 - y)]
        my_q = 2 * x + y

        def copy(a, j, src_q, dst_q):
            px, py = chips[j]
            return pltpu.make_async_remote_copy(
                src_ref=x_refs[a].at[src_q], dst_ref=out_refs[a].at[dst_q],
                send_sem=send_sems.at[A2A_COPIES * a + j], recv_sem=recv_sems.at[A2A_COPIES * a + j],
                device_id=(px, py, c), device_id_type=pl.DeviceIdType.MESH)

        local, sent = [], []
        for a in range(n):
            local.append(pltpu.make_async_copy(x_refs[a].at[my_q], out_refs[a].at[my_q], local_sems.at[a]))
            local[-1].start()
            sent += [copy(a, j, 2 * px + py, my_q) for j, (px, py) in enumerate(chips)]
            for cp in sent[-A2A_COPIES:]:
                cp.start()
        for a in range(n):
            for j, (px, py) in enumerate(chips):
                copy(a, j, my_q, 2 * px + py).wait_recv()
        for cp in sent:
            cp.wait_send()
        for cp in local:
            cp.wait()

    return _comm_call(name, body, blks, [jax.ShapeDtypeStruct(b.shape, b.dtype) for b in blks], A2A_COPIES)


def _rowwise(name, fn, rows, bvecs=(), cvecs=(), out_rows=(), out_bsums=(), out_tsums=(), tm=256):
    B, S = rows[0].shape[:2]
    tm = _pick(S, tm, SUBLANES)
    n_r, n_b, n_c = len(rows), len(bvecs), len(cvecs)
    n_or, n_ob = len(out_rows), len(out_bsums)

    def body(*refs):
        ins, outs = refs[:n_r + n_b + n_c], refs[n_r + n_b + n_c:]
        r = [ref[0] for ref in ins[:n_r]]
        bv = [ref[0] for ref in ins[n_r:n_r + n_b]]
        cv = [ref[...] for ref in ins[n_r + n_b:]]
        o_rows, o_bsums, o_tsums = fn(r, bv, cv)
        b, i = pl.program_id(0), pl.program_id(1)
        for ref, val in zip(outs[:n_or], o_rows):
            ref[0] = val.astype(ref.dtype)
        for ref, val in zip(outs[n_or:n_or + n_ob], o_bsums):
            @pl.when(i == 0)
            def _(ref=ref, val=val):
                ref[0] = val

            @pl.when(i > 0)
            def _(ref=ref, val=val):
                ref[0] += val
        for ref, val in zip(outs[n_or + n_ob:], o_tsums):
            first = jnp.logical_and(b == 0, i == 0)

            @pl.when(first)
            def _(ref=ref, val=val):
                ref[...] = val

            @pl.when(jnp.logical_not(first))
            def _(ref=ref, val=val):
                ref[...] += val

    in_specs = [pl.BlockSpec((1, tm, a.shape[2]), lambda b, i: (b, i, 0)) for a in rows]
    in_specs += [pl.BlockSpec((1, 1, a.shape[2]), lambda b, i: (b, 0, 0)) for a in bvecs]
    in_specs += [pl.BlockSpec((1, a.shape[1]), lambda b, i: (0, 0)) for a in cvecs]
    out_shape = [jax.ShapeDtypeStruct((B, S, cdim), dt) for cdim, dt in out_rows]
    out_specs = [pl.BlockSpec((1, tm, cdim), lambda b, i: (b, i, 0)) for cdim, _ in out_rows]
    out_shape += [jax.ShapeDtypeStruct((B, 1, cdim), F32) for cdim in out_bsums]
    out_specs += [pl.BlockSpec((1, 1, cdim), lambda b, i: (b, 0, 0)) for cdim in out_bsums]
    out_shape += [jax.ShapeDtypeStruct((1, cdim), F32) for cdim in out_tsums]
    out_specs += [pl.BlockSpec((1, cdim), lambda b, i: (0, 0)) for cdim in out_tsums]
    sem = ("arbitrary", "arbitrary") if out_tsums else ("parallel", "arbitrary")
    res = pl.pallas_call(body, name=_unique(name), grid=(B, S // tm), in_specs=in_specs, out_specs=out_specs,
                         out_shape=out_shape, compiler_params=_params(sem))(*rows, *bvecs, *cvecs)
    return list(res)


MM_TILE = 1536
MM_ROWS = 512
MM_WEIGHT_TILE_BYTES = 12 * 1024 * 1024


def _mm(name, a, b, ta=False, tb=False, out_dtype=F32):
    M, K = (a.shape[1], a.shape[0]) if ta else a.shape
    N = b.shape[0] if tb else b.shape[1]
    assert (b.shape[1] if tb else b.shape[0]) == K, (a.shape, b.shape, ta, tb)
    tn = _pick(N, MM_TILE)
    if ta:
        tm, tk = _pick(M, MM_TILE), _pick(K, 2 * MM_ROWS, LANES if tb else SUBLANES)
    else:
        tm = _pick(M, MM_ROWS, SUBLANES)
        tk = K if K * tn * 2 <= MM_WEIGHT_TILE_BYTES else _pick(K, MM_TILE)
    nk = K // tk
    dims = (((0 if ta else 1,), (1 if tb else 0,)), ((), ()))

    def body(a_ref, b_ref, o_ref, acc_ref):
        k = pl.program_id(2)
        p = lax.dot_general(a_ref[...].astype(BF16), b_ref[...].astype(BF16), dims, preferred_element_type=F32)
        if nk == 1:
            o_ref[...] = p.astype(o_ref.dtype)
        else:
            @pl.when(k == 0)
            def _():
                acc_ref[...] = p

            @pl.when(k > 0)
            def _():
                acc_ref[...] += p

            @pl.when(k == nk - 1)
            def _():
                o_ref[...] = acc_ref[...].astype(o_ref.dtype)

    a_spec = pl.BlockSpec((tk, tm), lambda j, i, k: (k, i)) if ta else pl.BlockSpec((tm, tk), lambda j, i, k: (i, k))
    b_spec = pl.BlockSpec((tn, tk), lambda j, i, k: (j, k)) if tb else pl.BlockSpec((tk, tn), lambda j, i, k: (k, j))
    return pl.pallas_call(
        body, name=_unique(name), grid=(N // tn, M // tm, nk), in_specs=[a_spec, b_spec],
        out_specs=pl.BlockSpec((tm, tn), lambda j, i, k: (i, j)),
        out_shape=jax.ShapeDtypeStruct((M, N), out_dtype),
        scratch_shapes=[pltpu.VMEM((tm, tn) if nk > 1 else (SUBLANES, LANES), F32)],
        compiler_params=_params(("parallel", "parallel", "arbitrary")))(a, b)


def _silu(x):
    return x * _sigmoid(x)


def _sigmoid(x):
    return 1.0 / (1.0 + jnp.exp(-x))


def _dsilu(x):
    s = _sigmoid(x)
    return s * (1.0 + x * (1.0 - s))


def _gelu(x):
    return 0.5 * x * (1.0 + lax.erf(x * np.float32(math.sqrt(0.5))))


def _dgelu(x):
    cdf = 0.5 * (1.0 + lax.erf(x * np.float32(math.sqrt(0.5))))
    pdf = jnp.exp(-0.5 * x * x) * np.float32(1.0 / math.sqrt(2.0 * math.pi))
    return cdf + x * pdf


def _ln_stats(r):
    mu = jnp.mean(r, axis=-1, keepdims=True)
    xc = r - mu
    var = jnp.mean(xc * xc, axis=-1, keepdims=True)
    rstd = lax.rsqrt(var + LN_EPS)
    return xc * rstd, rstd


def _ln_bwd(dxhat, xhat, rstd):
    m1 = jnp.mean(dxhat, axis=-1, keepdims=True)
    m2 = jnp.mean(dxhat * xhat, axis=-1, keepdims=True)
    return rstd * (dxhat - m1 - xhat * m2)


def _csum(v):
    return jnp.sum(v, axis=0, keepdims=True)


def _split_dot(x, m01, lhs01=False, terms=2):
    acc, rem = None, x
    for _ in range(terms):
        part = rem.astype(BF16)
        rem = rem - part.astype(F32)
        d = jnp.dot(m01, part, preferred_element_type=F32) if lhs01 else jnp.dot(part, m01, preferred_element_type=F32)
        acc = d if acc is None else acc + d
    return acc


def _iota2(shape, dim):
    return lax.broadcasted_iota(jnp.int32, shape, dim)


def _modulate(name, x, sc, sh):
    D = x.shape[2]
    return _rowwise(name, lambda r, bv, cv: ([r[0] * (1.0 + bv[0]) + bv[1]], [], []),
                    [x], [sc, sh], [], out_rows=[(D, BF16)])[0]


def _resid_ln(name, alpha, x, y, g, ln_g, ln_b, ybias=None, then=None):
    D = x.shape[2]

    def fn(r, bv, cv):
        yy = r[1] if ybias is None else r[1] + cv[2]
        xhat, _ = _ln_stats(alpha * r[0] + (1.0 + bv[0]) * yy)
        xn = xhat * cv[0] + cv[1]
        return [xn] + ([xn * (1.0 + bv[1]) + bv[2]] if then else []), [], []
    cvecs = [ln_g, ln_b] + ([] if ybias is None else [ybias])
    return _rowwise(name, fn, [x, y], [g] + list(then or ()), cvecs, out_rows=[(D, F32)] + ([(D, BF16)] if then else []))


def _resid_ln_bwd(name, alpha, dxn, x, y, g, ln_g, ln_b, ybias=None, then=None):
    D = x.shape[2]

    def fn(r, bv, cv):
        yy = r[2] if ybias is None else r[2] + cv[2]
        xhat, rstd = _ln_stats(alpha * r[1] + (1.0 + bv[0]) * yy)
        d, sums = r[0], []
        if then:
            d = alpha * d + r[3] * (1.0 + bv[1])
            sums = [_csum(r[3] * (xhat * cv[0] + cv[1])), _csum(r[3])]
        dr = _ln_bwd(d * cv[0], xhat, rstd)
        dy = (1.0 + bv[0]) * dr
        return [dr, dy], [_csum(dr * yy)] + sums, [_csum(d * xhat), _csum(d), _csum(dy)]
    cvecs = [ln_g, ln_b] + ([] if ybias is None else [ybias])
    rows, bvecs = [dxn, x, y] + ([then[0]] if then else []), [g] + ([then[1]] if then else [])
    return _rowwise(name, fn, rows, bvecs, cvecs, out_rows=[(D, F32), (D, BF16)], out_bsums=[D] * (3 if then else 1),
                    out_tsums=[D, D, D])


def _modulate_bwd(name, alpha, dh, dr, x, sc):
    D = x.shape[2]

    def fn(r, bv, cv):
        return [alpha * r[1] + r[0] * (1.0 + bv[0])], [_csum(r[0] * r[2]), _csum(r[0])], []
    return _rowwise(name, fn, [dh, dr, x], [sc], [], out_rows=[(D, F32)], out_bsums=[D, D])


def _loss_head(name, y, target):
    D = y.shape[2]

    def fn(r, bv, cv):
        e = r[0] - r[1]
        return [e * np.float32(1.0 / D)], [_csum(e * e)], []
    return _rowwise(name, fn, [y, target], [], [], out_rows=[(D, F32)], out_bsums=[D])


def _ffn_in_act(name, h, w_in):
    M, K = h.shape
    Hd = w_in.shape[1] // 2
    tm, tn = _pick(M, MM_ROWS, SUBLANES), _pick(Hd, MM_TILE)
    nj = Hd // tn

    def body(h_ref, wg_ref, wu_ref, zg_ref, zu_ref, act_ref):
        hv = h_ref[...].astype(BF16)
        zg = jnp.dot(hv, wg_ref[...].astype(BF16), preferred_element_type=F32)
        zu = jnp.dot(hv, wu_ref[...].astype(BF16), preferred_element_type=F32)
        zg_ref[...] = zg
        zu_ref[...] = zu
        act_ref[...] = (_silu(zg) * zu).astype(act_ref.dtype)

    tile = pl.BlockSpec((tm, tn), lambda j, i: (i, j))
    return pl.pallas_call(
        body, name=_unique(name), grid=(nj, M // tm),
        in_specs=[pl.BlockSpec((tm, K), lambda j, i: (i, 0)), pl.BlockSpec((K, tn), lambda j, i: (0, j)),
                  pl.BlockSpec((K, tn), lambda j, i: (0, j + nj))],
        out_specs=[tile, tile, tile],
        out_shape=[jax.ShapeDtypeStruct((M, Hd), F32), jax.ShapeDtypeStruct((M, Hd), F32), jax.ShapeDtypeStruct((M, Hd), BF16)],
        compiler_params=_params(("parallel", "parallel")))(h, w_in, w_in)


def _ffn_out_bwd_act(name, dy, w_out, zg, zu):
    M, K = dy.shape
    Hd = w_out.shape[0]
    tm, tn = _pick(M, MM_ROWS, SUBLANES), _pick(Hd, MM_TILE)

    def body(dy_ref, w_ref, zg_ref, zu_ref, dzg_ref, dzu_ref):
        da = lax.dot_general(dy_ref[...].astype(BF16), w_ref[...].astype(BF16), _NT, preferred_element_type=F32)
        gg = zg_ref[...]
        sg = _sigmoid(gg)
        dzg_ref[...] = (da * zu_ref[...] * (sg * (1.0 + gg * (1.0 - sg)))).astype(dzg_ref.dtype)
        dzu_ref[...] = (da * (gg * sg)).astype(dzu_ref.dtype)

    tile = pl.BlockSpec((tm, tn), lambda j, i: (i, j))
    return pl.pallas_call(
        body, name=_unique(name), grid=(Hd // tn, M // tm),
        in_specs=[pl.BlockSpec((tm, K), lambda j, i: (i, 0)), pl.BlockSpec((tn, K), lambda j, i: (j, 0)), tile, tile],
        out_specs=[tile, tile],
        out_shape=[jax.ShapeDtypeStruct((M, Hd), BF16), jax.ShapeDtypeStruct((M, Hd), BF16)],
        compiler_params=_params(("parallel", "parallel")))(dy, w_out, zg, zu)


def _ffn_in_bwd(name, dzg, dzu, w_in):
    M, Hd = dzg.shape
    N = w_in.shape[0]
    tm, tn = _pick(M, MM_ROWS, SUBLANES), _pick(N, MM_TILE)

    def body(g_ref, u_ref, wg_ref, wu_ref, o_ref):
        o_ref[...] = (lax.dot_general(g_ref[...], wg_ref[...].astype(BF16), _NT, preferred_element_type=F32)
                      + lax.dot_general(u_ref[...], wu_ref[...].astype(BF16), _NT, preferred_element_type=F32))

    rows = pl.BlockSpec((tm, Hd), lambda j, i: (i, 0))
    return pl.pallas_call(
        body, name=_unique(name), grid=(N // tn, M // tm),
        in_specs=[rows, rows, pl.BlockSpec((tn, Hd), lambda j, i: (j, 0)), pl.BlockSpec((tn, Hd), lambda j, i: (j, 1))],
        out_specs=pl.BlockSpec((tm, tn), lambda j, i: (i, j)),
        out_shape=jax.ShapeDtypeStruct((M, N), F32),
        compiler_params=_params(("parallel", "parallel")))(dzg, dzu, w_in, w_in)


def _gm_act(name, zin, b_in, ln_g, ln_b):
    W = zin.shape[2] // 2

    def fn(r, bv, cv):
        z = _gelu(r[0] + cv[0])
        vhat, _ = _ln_stats(z[:, W:])
        return [z[:, :W], vhat * cv[1] + cv[2]], [], []
    return _rowwise(name, fn, [zin], [], [b_in, ln_g, ln_b], out_rows=[(W, F32), (W, BF16)])


def _gm_act_bwd(name, zin, du, dvn, b_in, ln_g):
    W = zin.shape[2] // 2

    def fn(r, bv, cv):
        zz = r[0] + cv[0]
        z = _gelu(zz)
        vhat, rstd = _ln_stats(z[:, W:])
        dv = _ln_bwd(r[2] * cv[1], vhat, rstd)
        dzin = jnp.concatenate([r[1], dv], axis=1) * _dgelu(zz)
        return [dzin], [], [_csum(dzin), _csum(r[2] * vhat), _csum(r[2])]
    return _rowwise(name, fn, [zin, du, dvn], [], [b_in, ln_g], out_rows=[(2 * W, BF16)], out_tsums=[2 * W, W, W])


def _gm_causal_w(ws_ref, g):
    T = ws_ref.shape[1]
    return jnp.where(_iota2((T, T), 1) <= _iota2((T, T), 0), ws_ref[g], 0.0).astype(BF16)


def _gm_spatial(name, u, vn, w_s, b_sT):
    B, S, W = u.shape
    G, T = w_s.shape[0], w_s.shape[1]
    assert W == G * T, "a head group is as wide as a chunk is long"

    def body(u_ref, vn_ref, ws_ref, bs_ref, y_ref):
        for g in range(G):
            cs = slice(g * T, (g + 1) * T)
            sv = jnp.dot(_gm_causal_w(ws_ref, g), vn_ref[0, :, cs], preferred_element_type=F32) + bs_ref[:, g:g + 1]
            y_ref[0, :, cs] = (u_ref[0, :, cs] * sv).astype(y_ref.dtype)

    row = pl.BlockSpec((1, T, W), lambda b, i: (b, i, 0))
    return pl.pallas_call(
        body, name=_unique(name), grid=(B, S // T),
        in_specs=[row, row, pl.BlockSpec((G, T, T), lambda b, i: (0, 0, 0)), pl.BlockSpec((T, G), lambda b, i: (0, 0))],
        out_specs=row, out_shape=jax.ShapeDtypeStruct((B, S, W), BF16),
        compiler_params=_params(("parallel", "parallel")))(u, vn, w_s, b_sT)


def _gm_spatial_bwd(name, dyg, u, vn, w_s, b_sT):
    B, S, W = u.shape
    G, T = w_s.shape[0], w_s.shape[1]
    assert W == G * T, "a head group is as wide as a chunk is long"

    def body(dy_ref, u_ref, vn_ref, ws_ref, bs_ref, du_ref, dvn_ref, dws_ref, dbs_ref):
        first = jnp.logical_and(pl.program_id(0) == 0, pl.program_id(1) == 0)

        @pl.when(first)
        def _():
            dws_ref[...] = jnp.zeros_like(dws_ref)
            dbs_ref[...] = jnp.zeros_like(dbs_ref)

        tril = _iota2((T, T), 1) <= _iota2((T, T), 0)
        for g in range(G):
            cs = slice(g * T, (g + 1) * T)
            wm = _gm_causal_w(ws_ref, g)
            vng = vn_ref[0, :, cs]
            sv = jnp.dot(wm, vng, preferred_element_type=F32) + bs_ref[:, g:g + 1]
            dy = dy_ref[0, :, cs]
            du_ref[0, :, cs] = dy * sv
            dsv = dy * u_ref[0, :, cs]
            dsv16 = dsv.astype(BF16)
            dvn_ref[0, :, cs] = lax.dot_general(wm, dsv16, (((0,), (0,)), ((), ())), preferred_element_type=F32)
            dw = lax.dot_general(dsv16, vng, (((1,), (1,)), ((), ())), preferred_element_type=F32)
            dws_ref[g] += jnp.where(tril, dw, 0.0)
            dbs_ref[:, g:g + 1] += jnp.sum(dsv, axis=1, keepdims=True)

    row = pl.BlockSpec((1, T, W), lambda b, i: (b, i, 0))
    return pl.pallas_call(
        body, name=_unique(name), grid=(B, S // T),
        in_specs=[row, row, row, pl.BlockSpec((G, T, T), lambda b, i: (0, 0, 0)), pl.BlockSpec((T, G), lambda b, i: (0, 0))],
        out_specs=[row, row, pl.BlockSpec((G, T, T), lambda b, i: (0, 0, 0)), pl.BlockSpec((T, G), lambda b, i: (0, 0))],
        out_shape=[jax.ShapeDtypeStruct((B, S, W), F32), jax.ShapeDtypeStruct((B, S, W), F32),
                   jax.ShapeDtypeStruct((G, T, T), F32), jax.ShapeDtypeStruct((T, G), F32)],
        compiler_params=_params(("arbitrary", "arbitrary")))(dyg, u, vn, w_s, b_sT)


def _cv_glu(name, pw, b_in):
    W = pw.shape[2] // 2

    def fn(r, bv, cv):
        z = r[0] + cv[0]
        return [z[:, :W] * _sigmoid(z[:, W:])], [], []
    return _rowwise(name, fn, [pw], [], [b_in], out_rows=[(W, F32)])[0]


def _cv_glu_bwd(name, pw, dyg, b_in):
    W = pw.shape[2] // 2

    def fn(r, bv, cv):
        z = r[0] + cv[0]
        a, s = z[:, :W], _sigmoid(z[:, W:])
        dpw = jnp.concatenate([r[1] * s, r[1] * a * s * (1.0 - s)], axis=1)
        return [dpw], [], [_csum(dpw)]
    return _rowwise(name, fn, [pw, dyg], [], [b_in], out_rows=[(2 * W, BF16)], out_tsums=[2 * W])


def _cv_ln_act(name, yc, ln_g, ln_b):
    D = yc.shape[2]

    def fn(r, bv, cv):
        xhat, _ = _ln_stats(r[0])
        return [_silu(xhat * cv[0] + cv[1])], [], []
    return _rowwise(name, fn, [yc], [], [ln_g, ln_b], out_rows=[(D, BF16)])[0]


def _cv_ln_act_bwd(name, yc, dys, ln_g, ln_b):
    D = yc.shape[2]

    def fn(r, bv, cv):
        xhat, rstd = _ln_stats(r[0])
        dyn = r[1] * _dsilu(xhat * cv[0] + cv[1])
        dyc = _ln_bwd(dyn * cv[0], xhat, rstd)
        return [dyc], [], [_csum(dyn * xhat), _csum(dyn), _csum(dyc)]
    return _rowwise(name, fn, [yc, dys], [], [ln_g, ln_b], out_rows=[(D, F32)], out_tsums=[D, D, D])


CONV_HALO = 32
CONV_TS, CONV_TC = 256, 128


def _dwconv(name, y, dw, dw_b):
    B, S, D = y.shape
    ts, tc, halo, K = _pick(S, CONV_TS, SUBLANES), _pick(D, CONV_TC), CONV_HALO, CONV_WIDTH

    def body(cur_ref, prev_ref, dw_ref, b_ref, o_ref, buf):
        i = pl.program_id(1)
        buf[pl.ds(0, halo), :] = jnp.where(i > 0, prev_ref[0, pl.ds(ts - halo, halo), :], 0.0)
        buf[pl.ds(halo, ts), :] = cur_ref[0]
        acc = jnp.zeros((ts, tc), F32) + b_ref[...]
        for k in range(K):
            acc = acc + dw_ref[k:k + 1, :] * buf[pl.ds(halo - (K - 1) + k, ts), :]
        o_ref[0] = acc

    return pl.pallas_call(
        body, name=_unique(name), grid=(B, S // ts, D // tc),
        in_specs=[pl.BlockSpec((1, ts, tc), lambda b, i, j: (b, i, j)),
                  pl.BlockSpec((1, ts, tc), lambda b, i, j: (b, jnp.maximum(i - 1, 0), j)),
                  pl.BlockSpec((halo, tc), lambda b, i, j: (0, j)), pl.BlockSpec((1, tc), lambda b, i, j: (0, j))],
        out_specs=pl.BlockSpec((1, ts, tc), lambda b, i, j: (b, i, j)),
        out_shape=jax.ShapeDtypeStruct((B, S, D), F32),
        scratch_shapes=[pltpu.VMEM((halo + ts, tc), F32)],
        compiler_params=_params(("parallel", "parallel", "parallel")))(y, y, dw, dw_b)


def _dwconv_bwd(name, dyc, y, dw):
    B, S, D = y.shape
    ts, tc, halo, K = _pick(S, CONV_TS, SUBLANES), _pick(D, CONV_TC), CONV_HALO, CONV_WIDTH
    nt = S // ts

    def body(g_ref, gnext_ref, y_ref, yprev_ref, dw_ref, dy_ref, ddw_ref, gbuf, ybuf):
        b, i = pl.program_id(1), pl.program_id(2)
        first = jnp.logical_and(b == 0, i == 0)

        @pl.when(first)
        def _():
            ddw_ref[...] = jnp.zeros_like(ddw_ref)

        g = g_ref[0]
        gbuf[pl.ds(0, ts), :] = g
        gbuf[pl.ds(ts, halo), :] = jnp.where(i < nt - 1, gnext_ref[0, pl.ds(0, halo), :], 0.0)
        ybuf[pl.ds(0, halo), :] = jnp.where(i > 0, yprev_ref[0, pl.ds(ts - halo, halo), :], 0.0)
        ybuf[pl.ds(halo, ts), :] = y_ref[0]
        acc = jnp.zeros((ts, tc), F32)
        for k in range(K):
            acc = acc + dw_ref[k:k + 1, :] * gbuf[pl.ds(K - 1 - k, ts), :]
            ddw_ref[k:k + 1, :] += _csum(g * ybuf[pl.ds(halo - (K - 1) + k, ts), :])
        dy_ref[0] = acc

    tile = lambda f: pl.BlockSpec((1, ts, tc), f)
    return pl.pallas_call(
        body, name=_unique(name), grid=(D // tc, B, nt),
        in_specs=[tile(lambda j, b, i: (b, i, j)), tile(lambda j, b, i: (b, jnp.minimum(i + 1, nt - 1), j)),
                  tile(lambda j, b, i: (b, i, j)), tile(lambda j, b, i: (b, jnp.maximum(i - 1, 0), j)),
                  pl.BlockSpec((halo, tc), lambda j, b, i: (0, j))],
        out_specs=[tile(lambda j, b, i: (b, i, j)), pl.BlockSpec((halo, tc), lambda j, b, i: (0, j))],
        out_shape=[jax.ShapeDtypeStruct((B, S, D), F32), jax.ShapeDtypeStruct((halo, D), F32)],
        scratch_shapes=[pltpu.VMEM((ts + halo, tc), F32), pltpu.VMEM((halo + ts, tc), F32)],
        compiler_params=_params(("parallel", "arbitrary", "arbitrary")))(dyc, dyc, y, y, dw)


ATT_BLOCK = 128
FOX_TILE = (512, 1024)
SB_TILE = (256, 512)
ATT_PIECE_ROWS = 32
FOX_GATE_COLS = 128


def _att_tiles(S, tile):
    return _pick(S, tile[0], SUBLANES), _pick(S, tile[1], LANES)


def _pieces(T, TK):
    R = min(T, ATT_PIECE_ROWS)
    segs = [slice(c, c + LANES) for c in range(0, TK, LANES)]
    return [(slice(r, r + R), segs) for r in range(0, T, R)]


def _piece_keep(row0, col0, rs, cs, strict, lane_major_of=0):
    shape = (rs.stop - rs.start, cs.stop - cs.start)
    lane = _iota2(shape, 1)
    key = col0 + (lane * lane_major_of + cs.start // LANES if lane_major_of else cs.start + lane)
    qry = row0 + rs.start + _iota2(shape, 0)
    return key < qry if strict else key <= qry


def _causal_tiles(i, tq, tk):
    return (i * tq + tq + tk - 1) // tk


def _lane_major(t, tk):
    B, H, S, dh = t.shape
    return t.reshape(B, H, S // tk, LANES, tk // LANES, dh).swapaxes(3, 4).reshape(B, H, S, dh)


def _lane_major_inverse(t, tk):
    B, H, S, dh = t.shape
    return t.reshape(B, H, S // tk, tk // LANES, LANES, dh).swapaxes(3, 4).reshape(B, H, S, dh)


def _log_sigmoid(x):
    return jnp.minimum(x, 0.0) - jnp.log(1.0 + jnp.exp(-jnp.abs(x)))


def _fox_gate_cumsum(name, fl, b_f):
    B, S, C = fl.shape
    T = _pick(S, ATT_BLOCK, SUBLANES)

    def body(fl_ref, bf_ref, f_ref, carry):
        @pl.when(pl.program_id(1) == 0)
        def _():
            carry[...] = jnp.zeros_like(carry)
        lf = _log_sigmoid(fl_ref[0] + bf_ref[...])
        lower = (_iota2((T, T), 1) <= _iota2((T, T), 0)).astype(BF16)
        f = _split_dot(lf, lower, lhs01=True, terms=3) + carry[...]
        f_ref[0] = f
        carry[...] = f[T - 1:T, :]

    return pl.pallas_call(
        body, name=_unique(name), grid=(B, S // T),
        in_specs=[pl.BlockSpec((1, T, C), lambda b, i: (b, i, 0)), pl.BlockSpec((1, C), lambda b, i: (0, 0))],
        out_specs=pl.BlockSpec((1, T, C), lambda b, i: (b, i, 0)),
        out_shape=jax.ShapeDtypeStruct((B, S, C), F32),
        scratch_shapes=[pltpu.VMEM((1, C), F32)],
        compiler_params=_params(("arbitrary", "arbitrary")))(fl, b_f)


def _fox_gate_bwd(name, dF, fl, b_f, n_heads):
    B, S, C = fl.shape
    T = _pick(S, ATT_BLOCK, SUBLANES)
    nt = S // T

    def body(df_ref, fl_ref, bf_ref, dfl_ref, dbf_ref, carry):
        first = jnp.logical_and(pl.program_id(0) == 0, pl.program_id(1) == 0)

        @pl.when(pl.program_id(1) == 0)
        def _():
            carry[...] = jnp.zeros_like(carry)

        @pl.when(first)
        def _():
            dbf_ref[...] = jnp.zeros_like(dbf_ref)

        upper = (_iota2((T, T), 1) >= _iota2((T, T), 0)).astype(BF16)
        dlf = _split_dot(df_ref[0], upper, lhs01=True, terms=3) + carry[...]
        carry[...] = dlf[0:1, :]
        x = fl_ref[0] + bf_ref[...]
        dfl = jnp.where(_iota2((T, C), 1) < n_heads, dlf * _sigmoid(-x), 0.0)
        dfl_ref[0] = dfl
        dbf_ref[...] += _csum(dfl)

    rev = lambda b, i: (b, nt - 1 - i, 0)
    return pl.pallas_call(
        body, name=_unique(name), grid=(B, nt),
        in_specs=[pl.BlockSpec((1, T, C), rev), pl.BlockSpec((1, T, C), rev), pl.BlockSpec((1, C), lambda b, i: (0, 0))],
        out_specs=[pl.BlockSpec((1, T, C), rev), pl.BlockSpec((1, C), lambda b, i: (0, 0))],
        out_shape=[jax.ShapeDtypeStruct((B, S, C), F32), jax.ShapeDtypeStruct((1, C), F32)],
        scratch_shapes=[pltpu.VMEM((1, C), F32)],
        compiler_params=_params(("arbitrary", "arbitrary")))(dF, fl, b_f)


_NT = (((1,), (1,)), ((), ()))
_TN = (((0,), (0,)), ((), ()))


def _fox_fwd(name, q, k, v, fq, fk):
    B, H, S, dh = q.shape
    T, TK = _att_tiles(S, FOX_TILE)
    scale = np.float32(dh ** -0.5)

    pieces = _pieces(T, TK)

    def body(q_ref, k_ref, v_ref, fq_ref, fk_ref, o_ref, lse_ref, s_scr, p_scr):
        i = pl.program_id(2)
        qb = q_ref[0, 0]
        n_tiles = _causal_tiles(i, T, TK)

        def tile(j, carry, diag):
            m, l, acc = carry
            ks = pl.ds(pl.multiple_of(j * TK, TK), TK)
            s_scr[...] = lax.dot_general(qb, k_ref[0, 0, ks, :], _NT, preferred_element_type=F32) * scale
            fkj = fk_ref[0, 0, pl.ds(j, 1), :]
            m_new = []
            for rc, (rs, segs) in enumerate(pieces):
                fq_c, mx = fq_ref[0, 0, rs, :], None
                for cs in segs:
                    s = s_scr[rs, cs] + fq_c - fkj[:, cs]
                    if diag:
                        s = jnp.where(_piece_keep(i * T, j * TK, rs, cs, False), s, NEG_INF)
                    s_scr[rs, cs] = s
                    mx = s if mx is None else jnp.maximum(mx, s)
                m_new.append(jnp.maximum(m[rc], jnp.max(mx, axis=1, keepdims=True)))
            alpha, l_new = [], []
            for rc, (rs, segs) in enumerate(pieces):
                alpha.append(jnp.exp(m[rc] - m_new[rc]))
                psum = None
                for cs in segs:
                    p = jnp.exp(s_scr[rs, cs] - m_new[rc])
                    p_scr[rs, cs] = p.astype(BF16)
                    psum = p if psum is None else psum + p
                l_new.append(alpha[rc] * l[rc] + jnp.sum(psum, axis=1, keepdims=True))
            acc = jnp.concatenate(alpha, axis=0) * acc + jnp.dot(p_scr[...], v_ref[0, 0, ks, :], preferred_element_type=F32)
            return tuple(m_new), tuple(l_new), acc

        init = (tuple(jnp.full((rs.stop - rs.start, 1), NEG_INF, F32) for rs, _ in pieces),
                tuple(jnp.zeros((rs.stop - rs.start, 1), F32) for rs, _ in pieces), jnp.zeros((T, dh), F32))
        carry = lax.fori_loop(0, n_tiles - 1, lambda j, c: tile(j, c, False), init)
        m, l, acc = tile(n_tiles - 1, carry, True)
        m, l = jnp.concatenate(m, axis=0), jnp.concatenate(l, axis=0)
        o_ref[0, 0] = (acc / l).astype(o_ref.dtype)
        lse_ref[0, 0] = m + jnp.log(l)

    full = lambda w: pl.BlockSpec((1, 1, S, w), lambda b, h, i: (b, h, 0, 0))
    blk = lambda w: pl.BlockSpec((1, 1, T, w), lambda b, h, i: (b, h, i, 0))
    return pl.pallas_call(
        body, name=_unique(name), grid=(B, H, S // T),
        in_specs=[blk(dh), full(dh), full(dh), blk(1), pl.BlockSpec((1, 1, S // TK, TK), lambda b, h, i: (b, h, 0, 0))],
        out_specs=[blk(dh), blk(1)],
        out_shape=[jax.ShapeDtypeStruct((B, H, S, dh), BF16), jax.ShapeDtypeStruct((B, H, S, 1), F32)],
        scratch_shapes=[pltpu.VMEM((T, TK), F32), pltpu.VMEM((T, TK), BF16)],
        compiler_params=_params(("parallel", "parallel", "parallel")))(q, k, v, fq, fk)


def _fox_bwd(name, q, k, v, fq, fk, do, lse):
    B, H, S, dh = q.shape
    T, TK = _att_tiles(S, FOX_TILE)
    nt, nkt = S // T, S // TK
    scale = np.float32(dh ** -0.5)

    pieces = _pieces(T, TK)

    def body(q_ref, k_ref, v_ref, fq_ref, fk_ref, do_ref, lse_ref, dq_ref, dk_ref, dv_ref, dfk_ref,
             p_buf, dp_buf, s_scr, ds_scr, p16_scr, dk_acc, dv_acc):
        dk_acc[...] = jnp.zeros_like(dk_acc)
        dv_acc[...] = jnp.zeros_like(dv_acc)
        dfk_ref[...] = jnp.zeros_like(dfk_ref)

        def qloop(i, _):
            qs = pl.ds(pl.multiple_of(i * T, T), T)
            qb, dob16 = q_ref[0, 0, qs, :], do_ref[0, 0, qs, :].astype(BF16)
            n_tiles = _causal_tiles(i, T, TK)
            row_at = lambda rs: pl.ds(pl.multiple_of(i * T + rs.start, SUBLANES), rs.stop - rs.start)
            fq_c = [fq_ref[0, 0, row_at(rs), :] for rs, _ in pieces]
            lse_c = [lse_ref[0, 0, row_at(rs), :] for rs, _ in pieces]

            def sweep1(j, delta, diag):
                ks = pl.ds(pl.multiple_of(j * TK, TK), TK)
                s_scr[...] = lax.dot_general(qb, k_ref[0, 0, ks, :], _NT, preferred_element_type=F32) * scale
                dp_buf[j] = lax.dot_general(dob16, v_ref[0, 0, ks, :], _NT, preferred_element_type=F32)
                fkj = fk_ref[0, 0, pl.ds(j, 1), :]
                out = []
                for rc, (rs, segs) in enumerate(pieces):
                    pdp = None
                    for cs in segs:
                        p = jnp.exp(((s_scr[rs, cs] + fq_c[rc]) - fkj[:, cs]) - lse_c[rc])
                        if diag:
                            p = jnp.where(_piece_keep(i * T, j * TK, rs, cs, False), p, 0.0)
                        p_buf[j, rs, cs] = p
                        pdp = p * dp_buf[j, rs, cs] if pdp is None else pdp + p * dp_buf[j, rs, cs]
                    out.append(delta[rc] + jnp.sum(pdp, axis=1, keepdims=True))
                return tuple(out)

            zeros = tuple(jnp.zeros((rs.stop - rs.start, 1), F32) for rs, _ in pieces)
            delta = lax.fori_loop(0, n_tiles - 1, lambda j, d: sweep1(j, d, False), zeros)
            delta = sweep1(n_tiles - 1, delta, True)

            def sweep2(j, dq):
                ks = pl.ds(pl.multiple_of(j * TK, TK), TK)
                col = [None] * len(pieces[0][1])
                for rc, (rs, segs) in enumerate(pieces):
                    for sg, cs in enumerate(segs):
                        p = p_buf[j, rs, cs]
                        ds = p * (dp_buf[j, rs, cs] - delta[rc])
                        ds_scr[rs, cs] = ds.astype(BF16)
                        p16_scr[rs, cs] = p.astype(BF16)
                        col[sg] = ds if col[sg] is None else col[sg] + ds
                dfk_ref[0, 0, pl.ds(j, 1), :] -= jnp.concatenate([_csum(c) for c in col], axis=1)
                ds16 = ds_scr[...]
                dk_acc[ks, :] += lax.dot_general(ds16, qb, _TN, preferred_element_type=F32)
                dv_acc[ks, :] += lax.dot_general(p16_scr[...], dob16, _TN, preferred_element_type=F32)
                return dq + jnp.dot(ds16, k_ref[0, 0, ks, :], preferred_element_type=F32)

            dq_ref[0, 0, qs, :] = (lax.fori_loop(0, n_tiles, sweep2, jnp.zeros((T, dh), F32)) * scale).astype(dq_ref.dtype)
            return 0

        lax.fori_loop(0, nt, qloop, 0)
        dk_ref[0, 0] = (dk_acc[...] * scale).astype(dk_ref.dtype)
        dv_ref[0, 0] = dv_acc[...].astype(dv_ref.dtype)

    full = lambda w: pl.BlockSpec((1, 1, S, w), lambda b, h: (b, h, 0, 0))
    fks = pl.BlockSpec((1, 1, nkt, TK), lambda b, h: (b, h, 0, 0))
    return pl.pallas_call(
        body, name=_unique(name), grid=(B, H),
        in_specs=[full(dh), full(dh), full(dh), full(1), fks, full(dh), full(1)],
        out_specs=[full(dh), full(dh), full(dh), fks],
        out_shape=[jax.ShapeDtypeStruct((B, H, S, dh), BF16)] * 3 + [jax.ShapeDtypeStruct((B, H, nkt, TK), F32)],
        scratch_shapes=[pltpu.VMEM((nkt, T, TK), F32), pltpu.VMEM((nkt, T, TK), F32), pltpu.VMEM((T, TK), F32),
                        pltpu.VMEM((T, TK), BF16), pltpu.VMEM((T, TK), BF16), pltpu.VMEM((S, dh), F32), pltpu.VMEM((S, dh), F32)],
        compiler_params=_params(("parallel", "parallel")))(q, k, v, fq, fk, do, lse)


def _sb_terms(z, with_sigmoids=True):
    t = jnp.exp(-jnp.abs(z))
    lp = jnp.log(1.0 + t)
    lb, l1 = jnp.minimum(z, 0.0) - lp, jnp.minimum(-z, 0.0) - lp
    if not with_sigmoids:
        return lb, l1, None, None
    return lb, l1, jnp.exp(lb), jnp.exp(l1)


SCAN_RADIX = 4


def _lane_scan(x, reverse):
    lane = _iota2(x.shape, 1)
    y, d = x, 1
    while d < LANES:
        step = y
        for m in range(1, SCAN_RADIX):
            if m * d < LANES:
                if reverse:
                    step = step + jnp.where(lane + m * d < LANES, pltpu.roll(y, LANES - m * d, 1), 0.0)
                else:
                    step = step + jnp.where(lane >= m * d, pltpu.roll(y, m * d, 1), 0.0)
        y, d = step, d * SCAN_RADIX
    return y


def _chunk_scan(xs, reverse):
    n = len(xs)
    within, acc = [None] * n, None
    for s in (range(n - 1, -1, -1) if reverse else range(n)):
        acc = xs[s] if acc is None else acc + xs[s]
        within[s] = acc
    lanes = _lane_scan(acc, reverse)
    beyond = lanes - acc
    return [w + beyond for w in within], (lanes[:, 0:1] if reverse else lanes[:, LANES - 1:LANES])


SB_DEAD = 110.0


def _sb_fwd(name, q, k, v):
    B, H, S, dh = q.shape
    T, TK = _att_tiles(S, SB_TILE)
    scale = np.float32(dh ** -0.5)
    pieces = _pieces(T, TK)

    def body(q_ref, k_ref, v_ref, o_ref, lt_ref, first_ref, z_scr, a_scr):
        i = pl.program_id(2)
        qb = q_ref[0, 0]
        n_tiles = _causal_tiles(i, T, TK)

        def tile(j, carry, diag):
            runs, acc = carry
            ks = pl.ds(pl.multiple_of(j * TK, TK), TK)
            z_scr[...] = lax.dot_general(qb, k_ref[0, 0, ks, :], _NT, preferred_element_type=F32) * scale
            new_runs = []
            for rc, (rs, segs) in enumerate(pieces):
                terms = [_sb_terms(z_scr[rs, cs], False) for cs in segs]
                keep = [_piece_keep(i * T, j * TK, rs, cs, True, len(segs)) if diag else None for cs in segs]
                l1 = [jnp.where(kp, t[1], 0.0) if diag else t[1] for kp, t in zip(keep, terms)]
                right_of, total = _chunk_scan(l1, True)
                for cs, kp, t, x, r in zip(segs, keep, terms, l1, right_of):
                    a = jnp.exp(t[0] + ((r - x) + runs[rc]))
                    a_scr[rs, cs] = (jnp.where(kp, a, 0.0) if diag else a).astype(BF16)
                new_runs.append(runs[rc] + total)
            return tuple(new_runs), acc + jnp.dot(a_scr[...], v_ref[0, 0, ks, :], preferred_element_type=F32)

        init = (tuple(jnp.zeros((rs.stop - rs.start, 1), F32) for rs, _ in pieces), jnp.zeros((T, dh), F32))
        def some_row_alive(runs):
            worst = runs[0]
            for r in runs[1:]:
                worst = jnp.maximum(worst, r)
            return jnp.max(worst) > -SB_DEAD

        def step(c):
            runs, acc = tile(n_tiles - 1 - c[0], (c[1], c[2]), False)
            return c[0] + 1, runs, acc

        visited, runs, acc = lax.while_loop(lambda c: jnp.logical_and(c[0] < n_tiles, some_row_alive(c[1])), step,
                                            (jnp.int32(1), *tile(n_tiles - 1, init, True)))
        o_ref[0, 0] = acc.astype(o_ref.dtype)
        lt_ref[0, 0] = jnp.concatenate(runs, axis=0)
        first_ref[pl.program_id(0), pl.program_id(1), i] = (n_tiles - visited).astype(F32)

    full = lambda w: pl.BlockSpec((1, 1, S, w), lambda b, h, i: (b, h, 0, 0))
    blk = lambda w: pl.BlockSpec((1, 1, T, w), lambda b, h, i: (b, h, i, 0))
    return pl.pallas_call(
        body, name=_unique(name), grid=(B, H, S // T),
        in_specs=[blk(dh), full(dh), full(dh)], out_specs=[blk(dh), blk(1), pl.BlockSpec(memory_space=pltpu.SMEM)],
        out_shape=[jax.ShapeDtypeStruct((B, H, S, dh), BF16), jax.ShapeDtypeStruct((B, H, S, 1), F32),
                   jax.ShapeDtypeStruct((B, H, S // T), F32)],
        scratch_shapes=[pltpu.VMEM((T, TK), F32), pltpu.VMEM((T, TK), BF16)],
        compiler_params=_params(("arbitrary", "arbitrary", "arbitrary")))(q, k, v)


def _sb_bwd(name, q, k, v, do, lt, first):
    B, H, S, dh = q.shape
    T, TK = _att_tiles(S, SB_TILE)
    nt = S // T
    scale = np.float32(dh ** -0.5)

    pieces = _pieces(T, TK)

    def body(first_ref, q_ref, k_ref, v_ref, do_ref, lt_ref, dq_ref, dk_ref, dv_ref, z_scr, da_scr, dz_scr, a_scr, dk_acc, dv_acc):
        dk_acc[...] = jnp.zeros_like(dk_acc)
        dv_acc[...] = jnp.zeros_like(dv_acc)
        b, h = pl.program_id(0), pl.program_id(1)

        def qloop(i, _):
            qs = pl.ds(pl.multiple_of(i * T, T), T)
            qb, dob16 = q_ref[0, 0, qs, :], do_ref[0, 0, qs, :].astype(BF16)
            n_tiles = _causal_tiles(i, T, TK)
            first_tile = jnp.clip(first_ref[b, h, i].astype(jnp.int32), 0, n_tiles - 1)
            lt_c = [lt_ref[0, 0, pl.ds(pl.multiple_of(i * T + rs.start, SUBLANES), rs.stop - rs.start), :] for rs, _ in pieces]

            def tile(j, carry, diag):
                sums_l, sums_e, dq = carry
                ks = pl.ds(pl.multiple_of(j * TK, TK), TK)
                kb, vb = k_ref[0, 0, ks, :], v_ref[0, 0, ks, :]
                z_scr[...] = lax.dot_general(qb, kb, _NT, preferred_element_type=F32) * scale
                da_scr[...] = lax.dot_general(dob16, vb, _NT, preferred_element_type=F32)
                new_l, new_e = [], []
                for rc, (rs, segs) in enumerate(pieces):
                    terms = [_sb_terms(z_scr[rs, cs]) for cs in segs]
                    keep = [_piece_keep(i * T, j * TK, rs, cs, True, len(segs)) if diag else None for cs in segs]
                    l1 = [jnp.where(kp, t[1], 0.0) if diag else t[1] for kp, t in zip(keep, terms)]
                    upto, total_l = _chunk_scan(l1, False)
                    es = []
                    for cs, kp, t, u in zip(segs, keep, terms, upto):
                        a = jnp.exp(t[0] + (lt_c[rc] - (u + sums_l[rc])))
                        if diag:
                            a = jnp.where(kp, a, 0.0)
                        a_scr[rs, cs] = a.astype(BF16)
                        es.append(da_scr[rs, cs] * a)
                    e_upto, total_e = _chunk_scan(es, False)
                    for cs, kp, t, e, eu in zip(segs, keep, terms, es, e_upto):
                        dz = e * t[3] - ((eu - e) + sums_e[rc]) * t[2]
                        dz_scr[rs, cs] = (jnp.where(kp, dz, 0.0) if diag else dz).astype(BF16)
                    new_l.append(sums_l[rc] + total_l)
                    new_e.append(sums_e[rc] + total_e)
                dz16 = dz_scr[...]
                dk_acc[ks, :] += lax.dot_general(dz16, qb, _TN, preferred_element_type=F32)
                dv_acc[ks, :] += lax.dot_general(a_scr[...], dob16, _TN, preferred_element_type=F32)
                return tuple(new_l), tuple(new_e), dq + jnp.dot(dz16, kb, preferred_element_type=F32)

            zeros = tuple(jnp.zeros((rs.stop - rs.start, 1), F32) for rs, _ in pieces)
            carry = lax.fori_loop(first_tile, n_tiles - 1, lambda j, c: tile(j, c, False), (zeros, zeros, jnp.zeros((T, dh), F32)))
            dq_ref[0, 0, qs, :] = (tile(n_tiles - 1, carry, True)[2] * scale).astype(dq_ref.dtype)
            return 0

        lax.fori_loop(0, nt, qloop, 0)
        dk_ref[0, 0] = (dk_acc[...] * scale).astype(dk_ref.dtype)
        dv_ref[0, 0] = dv_acc[...].astype(dv_ref.dtype)

    full = lambda w: pl.BlockSpec((1, 1, S, w), lambda b, h: (b, h, 0, 0))
    return pl.pallas_call(
        body, name=_unique(name), grid=(B, H),
        in_specs=[pl.BlockSpec(memory_space=pltpu.SMEM), full(dh), full(dh), full(dh), full(dh), full(1)], out_specs=[full(dh)] * 3,
        out_shape=[jax.ShapeDtypeStruct((B, H, S, dh), BF16)] * 3,
        scratch_shapes=[pltpu.VMEM((T, TK), F32), pltpu.VMEM((T, TK), F32), pltpu.VMEM((T, TK), BF16), pltpu.VMEM((T, TK), BF16),
                        pltpu.VMEM((S, dh), F32), pltpu.VMEM((S, dh), F32)],
        compiler_params=_params(("parallel", "parallel")))(first, q, k, v, do, lt)


def _adamw(name, w, g, m, v):
    shape = w.shape
    n = w.size
    cols = shape[-1] if (w.ndim >= 2 and (shape[-1] % LANES == 0 or n // shape[-1] >= LANES)) else 0
    if cols:
        prep = lambda t: t.reshape(1, n // cols, cols)
    else:
        cols = LANES
        pad = (-n) % (SUBLANES * LANES)
        prep = lambda t: jnp.pad(t.reshape(-1), (0, pad), constant_values=1.0).reshape(1, (n + pad) // cols, cols)

    def fn(r, bv, cv):
        w_, g_, m_, v_ = r
        m2 = ADAM_B1 * m_ + (1.0 - ADAM_B1) * g_
        v2 = ADAM_B2 * v_ + (1.0 - ADAM_B2) * (g_ * g_)
        m_hat = m2 / (1.0 - ADAM_B1 ** ADAM_STEP)
        v_hat = v2 / (1.0 - ADAM_B2 ** ADAM_STEP)
        return [-ADAM_LR * (m_hat / (jnp.sqrt(v_hat) + ADAM_EPS) + ADAM_WD * w_), m2, v2], [], []
    outs = _rowwise(name, fn, [prep(w), prep(g), prep(m), prep(v)], out_rows=[(cols, F32)] * 3, tm=512)
    return [o.reshape(-1)[:n].reshape(shape) for o in outs]


def _sum8(name, parts):
    def fn(r, bv, cv):
        s = r[0]
        for t in r[1:]:
            s = s + t
        return [s], [], []
    rows = [parts[i][None] for i in range(parts.shape[0])]
    return _rowwise(name, fn, rows, out_rows=[(parts.shape[2], F32)])[0][0]


def _pack(arrs, cols, dtype, row_mult):
    flat = jnp.concatenate([a.reshape(-1).astype(dtype) for a in arrs])
    pad = (-flat.size) % (cols * row_mult)
    return jnp.pad(flat, (0, pad)).reshape(-1, cols)


def _unpack(flat, shapes):
    out, off = [], 0
    for s in shapes:
        n = int(np.prod(s))
        out.append(flat[off:off + n].reshape(s))
        off += n
    return out


HEAD_ROWS = 512


def _heads(name, t, H, col0=0):
    B, S, _ = t.shape
    per = LANES // HEAD_DIM
    ts = _pick(S, HEAD_ROWS, 2 * SUBLANES)

    def body(x_ref, o_ref):
        x = x_ref[0].astype(o_ref.dtype)
        for u in range(per):
            o_ref[0, u] = x[:, u * HEAD_DIM:(u + 1) * HEAD_DIM]

    return pl.pallas_call(
        body, name=_unique(name), grid=(B, H // per, S // ts),
        in_specs=[pl.BlockSpec((1, ts, LANES), lambda b, g, i: (b, i, col0 // LANES + g))],
        out_specs=pl.BlockSpec((1, per, ts, HEAD_DIM), lambda b, g, i: (b, g, i, 0)),
        out_shape=jax.ShapeDtypeStruct((B, H, S, HEAD_DIM), BF16),
        compiler_params=_params(("parallel", "parallel", "parallel")))(t)


def _unheads(name, t):
    B, H, S, dh = t.shape
    per = LANES // dh
    ts = _pick(S, HEAD_ROWS, 2 * SUBLANES)

    def body(x_ref, o_ref):
        o_ref[0] = jnp.concatenate([x_ref[0, u] for u in range(per)], axis=1)

    return pl.pallas_call(
        body, name=_unique(name), grid=(B, H // per, S // ts),
        in_specs=[pl.BlockSpec((1, per, ts, dh), lambda b, g, i: (b, g, i, 0))],
        out_specs=pl.BlockSpec((1, ts, LANES), lambda b, g, i: (b, i, g)),
        out_shape=jax.ShapeDtypeStruct((B, S, H * dh), t.dtype),
        compiler_params=_params(("parallel", "parallel", "parallel")))(t)


def kernel(*args):
    _names_used.clear()
    p = dict(zip(ARGS, args))
    x, target = p['x'], p['loss_target']
    B, S, D = x.shape
    T = B * S
    depth = p['ln1_g'].shape[0]
    H = D // HEAD_DIM
    alpha = np.float32((2.0 * depth) ** 0.25)
    cx, cy, cc = _mesh_pos()
    my_q = 2 * cx + cy
    axes = ("x", "y", "c")
    two = lambda t: t.reshape(T, t.shape[-1])
    three = lambda t: t.reshape(B, S, t.shape[-1])

    small_in = [p['c']] + [p[n] for n in SMALL_SPLIT]
    g1 = _all_gather8("ag_small", [_pack(small_in, LANES, F32, SUBLANES)])[0]
    g1 = g1.reshape(N_DEV, -1)
    c_all = g1[:, :B * D].reshape(N_DEV * B, D)
    per_chip = [_unpack(g1[2 * q], [a.shape for a in small_in])[1:] for q in range(N_CHIPS)]
    small = {n: jnp.concatenate([per_chip[q][i] for q in range(N_CHIPS)], axis=-1) for i, n in enumerate(SMALL_SPLIT)}
    for n in SMALL_REPL:
        small[n] = p[n]

    n_seq = N_DEV * B
    seq_pad = -(-n_seq // LANES) * LANES
    c_act = _rowwise("c_act", lambda r, bv, cv: ([_silu(r[0])], [], []),
                     [jnp.pad(c_all, ((0, seq_pad - n_seq), (0, 0)))[None]], out_rows=[(D, F32)])[0][0]
    mod_cols = p['mod_w'].shape[2]
    mod_part = jnp.stack([_mm(f"mod_fwd{l}", c_act, p['mod_w'][l])[:n_seq] for l in range(depth)])
    half_layers = depth // 2
    mod_half = lax.dynamic_slice_in_dim(mod_part, cc * half_layers, half_layers, axis=0)
    gm_ = _all_gather8("ag_mod", [mod_half.reshape(half_layers * n_seq, mod_cols)])[0]
    mod_all = gm_.reshape(N_CHIPS, 2, half_layers, n_seq, mod_cols).transpose(1, 2, 3, 0, 4).reshape(depth, n_seq, 6 * D)
    mod_mine = lax.dynamic_slice_in_dim(mod_all, (2 * my_q + cc) * B, B, axis=1)
    mod = _rowwise("mod_bias", lambda r, bv, cv: ([r[0] + bv[0]], [], []), [mod_mine], [p['mod_b'][:, None, :]],
                   out_rows=[(6 * D, F32)])[0]
    mods = [[mod[l, :, None, i * D:(i + 1) * D] for i in range(6)] for l in range(depth)]

    big_names = list(BIG)
    shard_shapes = [p[n].shape for n in big_names]
    half_rows = [s[0] * s[1] // 2 for s in shard_shapes]
    w_halves = [lax.dynamic_slice_in_dim(p[n].reshape(-1, s[2]), cc * hr, hr, axis=0).astype(BF16)
                for n, s, hr in zip(big_names, shard_shapes, half_rows)]
    W = {}
    for n, s, g in zip(big_names, shard_shapes, _all_gather8("ag_weights", w_halves)):
        seg = g.reshape((N_CHIPS,) + s)
        W[n] = jnp.concatenate([seg[q] for q in range(N_CHIPS)], axis=BIG[n])

    def vec(n, j):
        return small[n][j][None, :]

    def attn_proj(h1, w_in, gate_cols):
        wp = jnp.pad(w_in, ((0, 0), (0, gate_cols))) if gate_cols else w_in
        proj = three(_mm("att_proj", two(h1), wp))
        q, k, v = [_heads("att_heads", proj, H, col0=i * D) for i in range(3)]
        return wp, proj, q, k, v

    def gm_fwd(j, h1):
        zin = three(_mm("gm_in", two(h1), W['gm_w_in'][j]))
        u, vn = _gm_act("gm_act", zin, vec('gm_b_in', j), vec('gm_ln_g', j), vec('gm_ln_b', j))
        b_sT = small['gm_b_s'][j].T
        yg = _gm_spatial("gm_spatial", u, vn, small['gm_w_s'][j], b_sT)
        return three(_mm("gm_out", two(yg), W['gm_w_out'][j])), (zin, u, vn, b_sT, yg)

    def gm_bwd(j, h1, dy1, cache):
        zin, u, vn, b_sT, yg = cache
        g = {'gm_w_out': _mm("gm_dwout", two(yg), two(dy1), ta=True)}
        dyg = three(_mm("gm_dyg", two(dy1), W['gm_w_out'][j], tb=True))
        du, dvn, dws, dbsT = _gm_spatial_bwd("gm_spatial_bwd", dyg, u, vn, small['gm_w_s'][j], b_sT)
        dzin, g['gm_b_in'], g['gm_ln_g'], g['gm_ln_b'] = _gm_act_bwd("gm_act_bwd", zin, du, dvn, vec('gm_b_in', j), vec('gm_ln_g', j))
        g['gm_w_s'], g['gm_b_s'] = dws, dbsT.T
        g['gm_w_in'] = _mm("gm_dwin", two(h1), two(dzin), ta=True)
        return _mm("gm_dh", two(dzin), W['gm_w_in'][j], tb=True), g

    def fox_fwd(j, h1):
        wp, proj, q, k, v = attn_proj(h1, W['fox_w_in'][j], 3 * D + FOX_GATE_COLS - W['fox_w_in'].shape[2])
        fl = proj[..., 3 * D:]
        bf = jnp.pad(small['fox_b_f'][j][None, :], ((0, 0), (0, FOX_GATE_COLS - H)))
        Fh = _fox_gate_cumsum("fox_gate", fl, bf)[..., :H].transpose(0, 2, 1)
        fq, fk = Fh[..., None], Fh.reshape(B, H, -1, _att_tiles(S, FOX_TILE)[1])
        o, lse = _fox_fwd("fox_fwd", q, k, v, fq, fk)
        o2 = _unheads("att_merge_o", o)
        return three(_mm("fox_out", two(o2), W['fox_w_out'][j])), (wp, q, k, v, fl, bf, fq, fk, lse, o2)

    def fox_bwd(j, h1, dy1, cache):
        wp, q, k, v, fl, bf, fq, fk, lse, o2 = cache
        g = {'fox_w_out': _mm("fox_dwout", two(o2), two(dy1), ta=True)}
        do = _heads("att_heads_do", three(_mm("fox_do", two(dy1), W['fox_w_out'][j], tb=True, out_dtype=BF16)), H)
        dq, dk, dv, dfk = _fox_bwd("fox_bwd", q, k, v, fq, fk, do, lse)
        dF = jnp.pad(dfk.reshape(B, H, S).transpose(0, 2, 1), ((0, 0), (0, 0), (0, FOX_GATE_COLS - H)))
        dfl, dbf = _fox_gate_bwd("fox_gate_bwd", dF, fl, bf, H)
        dproj = jnp.concatenate([_unheads("att_merge_d", t) for t in (dq, dk, dv)] + [dfl.astype(BF16)], axis=-1)
        g['fox_w_in'] = _mm("fox_dwin", two(h1), two(dproj), ta=True)[:, :W['fox_w_in'].shape[2]]
        g['fox_b_f'] = dbf[0, :H]
        return _mm("fox_dh", two(dproj), wp, tb=True), g

    def sb_fwd(j, h1):
        wp, proj, q, k, v = attn_proj(h1, W['sb_w_in'][j], 0)
        k, v = _lane_major(k, _att_tiles(S, SB_TILE)[1]), _lane_major(v, _att_tiles(S, SB_TILE)[1])
        o, lt, first = _sb_fwd("sb_fwd", q, k, v)
        o2 = _unheads("att_merge_o", o)
        return three(_mm("sb_out", two(o2), W['sb_w_out'][j])), (q, k, v, lt, first, o2)

    def sb_bwd(j, h1, dy1, cache):
        q, k, v, lt, first, o2 = cache
        g = {'sb_w_out': _mm("sb_dwout", two(o2), two(dy1), ta=True)}
        do = _heads("att_heads_do", three(_mm("sb_do", two(dy1), W['sb_w_out'][j], tb=True, out_dtype=BF16)), H)
        dq, dk, dv = _sb_bwd("sb_bwd", q, k, v, do, lt, first)
        dk, dv = _lane_major_inverse(dk, _att_tiles(S, SB_TILE)[1]), _lane_major_inverse(dv, _att_tiles(S, SB_TILE)[1])
        dproj = jnp.concatenate([_unheads("att_merge_d", t) for t in (dq, dk, dv)], axis=-1)
        g['sb_w_in'] = _mm("sb_dwin", two(h1), two(dproj), ta=True)
        return _mm("sb_dh", two(dproj), W['sb_w_in'][j], tb=True), g

    def cv_fwd(j, h1):
        pw = three(_mm("cv_in", two(h1), W['cv_w_in'][j]))
        ygl = _cv_glu("cv_glu", pw, vec('cv_b_in', j))
        dw = jnp.pad(small['cv_dw'][j], ((0, CONV_HALO - CONV_WIDTH), (0, 0)))
        yc = _dwconv("cv_dwconv", ygl, dw, vec('cv_dw_b', j))
        ys = _cv_ln_act("cv_ln_act", yc, vec('cv_ln_g', j), vec('cv_ln_b', j))
        return three(_mm("cv_out", two(ys), W['cv_w_out'][j])), (pw, ygl, dw, yc, ys)

    def cv_bwd(j, h1, dy1, cache):
        pw, ygl, dw, yc, ys = cache
        g = {'cv_w_out': _mm("cv_dwout", two(ys), two(dy1), ta=True)}
        dys = three(_mm("cv_dys", two(dy1), W['cv_w_out'][j], tb=True))
        dyc, g['cv_ln_g'], g['cv_ln_b'], g['cv_dw_b'] = _cv_ln_act_bwd("cv_ln_act_bwd", yc, dys, vec('cv_ln_g', j), vec('cv_ln_b', j))
        dygl, ddw = _dwconv_bwd("cv_dwconv_bwd", dyc, ygl, dw)
        g['cv_dw'] = ddw[:CONV_WIDTH]
        dpw, g['cv_b_in'] = _cv_glu_bwd("cv_glu_bwd", pw, dygl, vec('cv_b_in', j))
        g['cv_w_in'] = _mm("cv_dwin", two(h1), two(dpw), ta=True)
        return _mm("cv_dh", two(dpw), W['cv_w_in'][j], tb=True), g

    mixers = [(gm_fwd, gm_bwd), (fox_fwd, fox_bwd), (sb_fwd, sb_bwd), (cv_fwd, cv_bwd)]
    n_mix = len(mixers)

    saved = []
    h1 = _modulate("mod1", x, mods[0][1], mods[0][0])
    for l in range(depth):
        m, j = l % n_mix, l // n_mix
        sh1, sc1, g1_, sh2, sc2, g2_ = mods[l]
        ybias = vec('cv_b_out', j) if m == 3 else None
        y1, cache = mixers[m][0](j, h1)
        xm, h2 = _resid_ln("resid_ln1", alpha, x, y1, g1_, small['ln1_g'][l][None], small['ln1_b'][l][None], ybias, then=(sc2, sh2))
        *z, a = _ffn_in_act("ffn_in", two(h2), W['ffn_w_in'][l])
        y2 = three(_mm("ffn_out", a, W['ffn_w_out'][l]))
        nxt = (mods[l + 1][1], mods[l + 1][0]) if l + 1 < depth else None
        xo, *h_next = _resid_ln("resid_ln2", alpha, xm, y2, g2_, small['ln2_g'][l][None], small['ln2_b'][l][None], then=nxt)
        saved.append((x, h1, y1, cache, xm, h2, z, a, y2, ybias))
        x, h1 = xo, (h_next[0] if h_next else None)

    dx, sq = _loss_head("loss_head", x, target)
    loss = lax.psum(jnp.sum(sq) * np.float32(0.5 / D), axes)

    grads = {n: [None] * p[n].shape[0] for n in WEIGHTS}
    parts = [dict() for _ in range(depth)]
    after = None
    for l in reversed(range(depth)):
        m, j = l % n_mix, l // n_mix
        sh1, sc1, g1_, sh2, sc2, g2_ = mods[l]
        x_in, h1, y1, cache, xm, h2, z, a, y2, ybias = saved[l]
        ln2 = (small['ln2_g'][l][None], small['ln2_b'][l][None])
        if after is None:
            dr2, dy2, parts[l]['g2'], grads['ln2_g'][l], grads['ln2_b'][l], _ = _resid_ln_bwd("resid_ln2_bwd", alpha, dx, xm, y2, g2_, *ln2)
        else:
            (dr2, dy2, parts[l]['g2'], parts[l + 1]['sc1'], parts[l + 1]['sh1'], grads['ln2_g'][l], grads['ln2_b'][l], _) = _resid_ln_bwd(
                "resid_ln2_bwd", alpha, after[0], xm, y2, g2_, *ln2, then=after[1:])
        grads['ffn_w_out'][l] = _mm("ffn_dwout", a, two(dy2), ta=True)
        dz = _ffn_out_bwd_act("ffn_da", two(dy2), W['ffn_w_out'][l], *z)
        grads['ffn_w_in'][l] = jnp.concatenate([_mm("ffn_dwin", two(h2), t, ta=True) for t in dz], axis=1)
        dh2 = three(_ffn_in_bwd("ffn_dh", *dz, W['ffn_w_in'][l]))
        (dr1, dy1, parts[l]['g1'], parts[l]['sc2'], parts[l]['sh2'], grads['ln1_g'][l], grads['ln1_b'][l], dyb) = _resid_ln_bwd(
            "resid_ln1_bwd", alpha, dr2, x_in, y1, g1_, small['ln1_g'][l][None], small['ln1_b'][l][None], ybias, then=(dh2, sc2))
        dh1, mg = mixers[m][1](j, h1, dy1, cache)
        if m == 3:
            mg['cv_b_out'] = dyb
        for n, gval in mg.items():
            grads[n][j] = gval
        after = (dr1, three(dh1), sc1)
    grad_x, parts[0]['sc1'], parts[0]['sh1'] = _modulate_bwd("mod1_bwd", alpha, after[1], after[0], saved[0][0], after[2])
    dmod = [jnp.concatenate([pt[k] for k in ('sh1', 'sc1', 'g1', 'sh2', 'sc2', 'g2')], axis=-1)[:, 0, :] for pt in parts]
    dmod = jnp.stack(dmod)
    grads['mod_b'] = [jnp.sum(dmod[l], axis=0) for l in range(depth)]
    full_shape = {n: tuple(t.shape) for n, t in small.items()}

    small_names = SMALL_REPL + SMALL_SPLIT
    small_parts = [jnp.stack([gv.reshape(full_shape[n][1:]) for gv in grads[n]]) for n in small_names]
    pack_a = _pack([dmod], LANES, F32, SUBLANES)
    pack_b = _pack(small_parts, LANES, F32, SUBLANES)
    g2 = _all_gather8("ag_grads_small", [jnp.concatenate([pack_a, pack_b], axis=0)])[0]
    rows_a = pack_a.shape[0]
    dmod_all = g2[:, :rows_a].reshape(N_DEV, -1)[:, :dmod.size].reshape(N_DEV, depth, B, 6 * D)
    dmod_all = dmod_all.transpose(1, 0, 2, 3).reshape(depth, n_seq, 6 * D)
    small_sum = _sum8("sum_grads_small", g2[:, rows_a:]).reshape(-1)
    g_small = dict(zip(small_names, _unpack(small_sum, [full_shape[n] for n in small_names])))
    for n in SMALL_SPLIT:
        w = p[n].shape[-1]
        g_small[n] = lax.dynamic_slice_in_dim(g_small[n], my_q * w, w, axis=g_small[n].ndim - 1)

    dm_cols = lax.dynamic_slice_in_dim(dmod_all, my_q * mod_cols, mod_cols, axis=2)
    dm_cols = jnp.pad(dm_cols, ((0, 0), (0, seq_pad - n_seq), (0, 0)))
    g_mod_w = jnp.stack([_mm(f"mod_dw{l}", c_act, dm_cols[l], ta=True) for l in range(depth)])

    keep, give = [], []
    for n, s, hr in zip(big_names, shard_shapes, half_rows):
        gfull = jnp.stack(grads[n])
        g4 = jnp.stack(jnp.split(gfull, N_CHIPS, axis=BIG[n])).reshape(N_CHIPS, 2 * hr, s[2])
        keep.append(lax.dynamic_slice_in_dim(g4, cc * hr, hr, axis=1))
        give.append(lax.dynamic_slice_in_dim(g4, (1 - cc) * hr, hr, axis=1))
    got = _sibling_exchange("rs_sibling", give)
    chip_sum = [_rowwise("rs_add_sibling", lambda r, bv, cv: ([r[0] + r[1]], [], []),
                         [a.reshape(1, -1, a.shape[2]), b.reshape(1, -1, a.shape[2])],
                         out_rows=[(a.shape[2], BF16)], tm=512)[0].reshape(a.shape) for a, b in zip(keep, got)]
    from_chips = _chip_all_to_all("rs_chips", chip_sum)
    half_sum = [_rowwise("rs_add_chips", lambda r, bv, cv: ([((r[0] + r[1]) + r[2]) + r[3]], [], []),
                         [t[q][None] for q in range(N_CHIPS)], out_rows=[(t.shape[2], F32)], tm=512)[0][0]
                for t in from_chips]
    other = _sibling_exchange("rs_share", half_sum)
    g_big = {n: jnp.concatenate([jnp.where(cc == 0, a, b), jnp.where(cc == 0, b, a)], axis=0).reshape(s)
             for n, s, a, b in zip(big_names, shard_shapes, half_sum, other)}

    g_out = {**g_small, **g_big, 'mod_w': g_mod_w}
    upd = {n: _adamw("adamw_" + n, p[n], g_out[n], p['m_' + n], p['v_' + n]) for n in WEIGHTS}
    return (loss, grad_x, *[g_out[n] for n in WEIGHTS], *[upd[n][0] for n in WEIGHTS],
            *[upd[n][1] for n in WEIGHTS], *[upd[n][2] for n in WEIGHTS])
```

```python
import math

import jax
import jax.numpy as jnp
import numpy as np
from jax import lax
from jax.experimental import pallas as pl
from jax.experimental.pallas import tpu as pltpu

F32, BF16 = jnp.float32, jnp.bfloat16

HEAD_DIM = 64
CONV_WIDTH = 31
LN_EPS = 1e-5
NEG_INF = -1e30
ADAM_LR, ADAM_B1, ADAM_B2, ADAM_EPS, ADAM_WD, ADAM_STEP = 0.001, 0.9, 0.999, 1e-08, 0.01, 10

LANES = 128
SUBLANES = 8
VMEM_LIMIT_BYTES = 56 * 1024 * 1024
N_CHIPS = 4
N_DEV = 8

WEIGHTS = ['mod_w', 'mod_b', 'ln1_g', 'ln1_b', 'ln2_g', 'ln2_b', 'ffn_w_in', 'ffn_w_out', 'gm_w_in', 'gm_b_in',
           'gm_ln_g', 'gm_ln_b', 'gm_w_s', 'gm_b_s', 'gm_w_out', 'fox_w_in', 'fox_b_f', 'fox_w_out', 'sb_w_in',
           'sb_w_out', 'cv_w_in', 'cv_b_in', 'cv_dw', 'cv_dw_b', 'cv_ln_g', 'cv_ln_b', 'cv_w_out', 'cv_b_out']
ARGS = ['x', 'c'] + WEIGHTS + ['loss_target'] + ['m_' + n for n in WEIGHTS] + ['v_' + n for n in WEIGHTS]
BIG = {'ffn_w_in': 2, 'ffn_w_out': 1, 'gm_w_in': 2, 'gm_w_out': 1, 'fox_w_in': 2, 'fox_w_out': 1,
       'sb_w_in': 2, 'sb_w_out': 1, 'cv_w_in': 2, 'cv_w_out': 1}
SMALL_SPLIT = ['cv_b_in', 'cv_dw', 'cv_dw_b', 'cv_ln_g', 'cv_ln_b', 'cv_b_out']
SMALL_REPL = ['mod_b', 'ln1_g', 'ln1_b', 'ln2_g', 'ln2_b', 'gm_b_in', 'gm_ln_g', 'gm_ln_b', 'gm_w_s', 'gm_b_s', 'fox_b_f']
PACK_COLS = 1024


_names_used = {}


def _unique(name):
    k = _names_used.get(name, 0)
    _names_used[name] = k + 1
    return name if k == 0 else f"{name}_{k}"


def _params(sem):
    return pltpu.CompilerParams(dimension_semantics=sem, vmem_limit_bytes=VMEM_LIMIT_BYTES)


def _pick(dim, pref, mult=LANES):
    if dim <= pref:
        return dim
    best = 0
    for t in range(mult, pref + 1, mult):
        if dim % t == 0:
            best = t
    assert best, (dim, pref)
    return best


def _mesh_pos():
    return lax.axis_index("x"), lax.axis_index("y"), lax.axis_index("c")


AG_COPIES = 7
A2A_COPIES = 3


def _comm_call(name, body, blks, out_shapes, n_sems):
    n = len(blks)
    hbm = pl.BlockSpec(memory_space=pl.ANY)
    return pl.pallas_call(
        body, name=_unique(name), out_shape=out_shapes, in_specs=[hbm] * n, out_specs=[hbm] * n,
        scratch_shapes=[pltpu.SemaphoreType.DMA((n_sems * n,)), pltpu.SemaphoreType.DMA((n_sems * n,)),
                        pltpu.SemaphoreType.DMA((n,))],
    )(*blks)


def _all_gather8(name, blks):
    n = len(blks)

    def body(*refs):
        x_refs, out_refs, (send_sems, recv_sems, local_sems) = refs[:n], refs[n:2 * n], refs[2 * n:]
        x, y, c = _mesh_pos()
        me, sibling = (x, y, c), (x, y, 1 - c)
        chips = [(1 - x, y), (x, 1 - y), (1 - x, 1 - y)]

        def copy(a, k, block, to, from_input=False):
            px, py, pc = block
            slot = out_refs[a].at[4 * px + 2 * py + pc]
            return pltpu.make_async_remote_copy(
                src_ref=x_refs[a] if from_input else slot, dst_ref=slot,
                send_sem=send_sems.at[AG_COPIES * a + k], recv_sem=recv_sems.at[AG_COPIES * a + k],
                device_id=to, device_id_type=pl.DeviceIdType.MESH)

        local, sent = [], []
        for a in range(n):
            local.append(pltpu.make_async_copy(x_refs[a], out_refs[a].at[4 * x + 2 * y + c], local_sems.at[a]))
            local[-1].start()
            first = [copy(a, 0, me, sibling, True)] + [copy(a, 1 + j, me, (*chip, c), True) for j, chip in enumerate(chips)]
            for cp in first:
                cp.start()
            sent += first
        for a in range(n):
            for j, chip in enumerate(chips):
                copy(a, 1 + j, (*chip, c), me).wait_recv()
                sent.append(copy(a, 4 + j, (*chip, c), sibling))
                sent[-1].start()
        for a in range(n):
            copy(a, 0, sibling, me).wait_recv()
            for j, chip in enumerate(chips):
                copy(a, 4 + j, (*chip, 1 - c), me).wait_recv()
        for cp in sent:
            cp.wait_send()
        for cp in local:
            cp.wait()

    return _comm_call(name, body, blks, [jax.ShapeDtypeStruct((N_DEV,) + b.shape, b.dtype) for b in blks], AG_COPIES)


def _sibling_exchange(name, blks):
    n = len(blks)

    def body(*refs):
        x_refs, out_refs, (send_sems, recv_sems, _) = refs[:n], refs[n:2 * n], refs[2 * n:]
        x, y, c = _mesh_pos()
        cps = [pltpu.make_async_remote_copy(src_ref=x_refs[a], dst_ref=out_refs[a], send_sem=send_sems.at[a],
                                            recv_sem=recv_sems.at[a], device_id=(x, y, 1 - c),
                                            device_id_type=pl.DeviceIdType.MESH) for a in range(n)]
        for cp in cps:
            cp.start()
        for cp in cps:
            cp.wait()

    return _comm_call(name, body, blks, [jax.ShapeDtypeStruct(b.shape, b.dtype) for b in blks], 1)


def _chip_all_to_all(name, blks):
    n = len(blks)

    def body(*refs):
        x_refs, out_refs, (send_sems, recv_sems, local_sems) = refs[:n], refs[n:2 * n], refs[2 * n:]
        x, y, c = _mesh_pos()
        chips = [(1 - x, y), (x, 1 - y), (1 - x, 1 - y)]
        my_q = 2 * x + y

        def copy(a, j, src_q, dst_q):
            px, py = chips[j]
            return pltpu.make_async_remote_copy(
                src_ref=x_refs[a].at[src_q], dst_ref=out_refs[a].at[dst_q],
                send_sem=send_sems.at[A2A_COPIES * a + j], recv_sem=recv_sems.at[A2A_COPIES * a + j],
                device_id=(px, py, c), device_id_type=pl.DeviceIdType.MESH)

        local, sent = [], []
        for a in range(n):
            local.append(pltpu.make_async_copy(x_refs[a].at[my_q], out_refs[a].at[my_q], local_sems.at[a]))
            local[-1].start()
            sent += [copy(a, j, 2 * px + py, my_q) for j, (px, py) in enumerate(chips)]
            for cp in sent[-A2A_COPIES:]:
                cp.start()
        for a in range(n):
            for j, (px, py) in enumerate(chips):
                copy(a, j, my_q, 2 * px + py).wait_recv()
        for cp in sent:
            cp.wait_send()
        for cp in local:
            cp.wait()

    return _comm_call(name, body, blks, [jax.ShapeDtypeStruct(b.shape, b.dtype) for b in blks], A2A_COPIES)


def _rowwise(name, fn, rows, bvecs=(), cvecs=(), out_rows=(), out_bsums=(), out_tsums=(), tm=256):
    B, S = rows[0].shape[:2]
    tm = _pick(S, tm, SUBLANES)
    n_r, n_b, n_c = len(rows), len(bvecs), len(cvecs)
    n_or, n_ob = len(out_rows), len(out_bsums)

    def body(*refs):
        ins, outs = refs[:n_r + n_b + n_c], refs[n_r + n_b + n_c:]
        r = [ref[0] for ref in ins[:n_r]]
        bv = [ref[0] for ref in ins[n_r:n_r + n_b]]
        cv = [ref[...] for ref in ins[n_r + n_b:]]
        o_rows, o_bsums, o_tsums = fn(r, bv, cv)
        b, i = pl.program_id(0), pl.program_id(1)
        for ref, val in zip(outs[:n_or], o_rows):
            ref[0] = val.astype(ref.dtype)
        for ref, val in zip(outs[n_or:n_or + n_ob], o_bsums):
            @pl.when(i == 0)
            def _(ref=ref, val=val):
                ref[0] = val

            @pl.when(i > 0)
            def _(ref=ref, val=val):
                ref[0] += val
        for ref, val in zip(outs[n_or + n_ob:], o_tsums):
            first = jnp.logical_and(b == 0, i == 0)

            @pl.when(first)
            def _(ref=ref, val=val):
                ref[...] = val

            @pl.when(jnp.logical_not(first))
            def _(ref=ref, val=val):
                ref[...] += val

    in_specs = [pl.BlockSpec((1, tm, a.shape[2]), lambda b, i: (b, i, 0)) for a in rows]
    in_specs += [pl.BlockSpec((1, 1, a.shape[2]), lambda b, i: (b, 0, 0)) for a in bvecs]
    in_specs += [pl.BlockSpec((1, a.shape[1]), lambda b, i: (0, 0)) for a in cvecs]
    out_shape = [jax.ShapeDtypeStruct((B, S, cdim), dt) for cdim, dt in out_rows]
    out_specs = [pl.BlockSpec((1, tm, cdim), lambda b, i: (b, i, 0)) for cdim, _ in out_rows]
    out_shape += [jax.ShapeDtypeStruct((B, 1, cdim), F32) for cdim in out_bsums]
    out_specs += [pl.BlockSpec((1, 1, cdim), lambda b, i: (b, 0, 0)) for cdim in out_bsums]
    out_shape += [jax.ShapeDtypeStruct((1, cdim), F32) for cdim in out_tsums]
    out_specs += [pl.BlockSpec((1, cdim), lambda b, i: (0, 0)) for cdim in out_tsums]
    sem = ("arbitrary", "arbitrary") if out_tsums else ("parallel", "arbitrary")
    res = pl.pallas_call(body, name=_unique(name), grid=(B, S // tm), in_specs=in_specs, out_specs=out_specs,
                         out_shape=out_shape, compiler_params=_params(sem))(*rows, *bvecs, *cvecs)
    return list(res)


MM_TILE = 1536
MM_ROWS = 512
MM_WEIGHT_TILE_BYTES = 12 * 1024 * 1024


def _mm(name, a, b, ta=False, tb=False, out_dtype=F32):
    M, K = (a.shape[1], a.shape[0]) if ta else a.shape
    N = b.shape[0] if tb else b.shape[1]
    assert (b.shape[1] if tb else b.shape[0]) == K, (a.shape, b.shape, ta, tb)
    tn = _pick(N, MM_TILE)
    if ta:
        tm, tk = _pick(M, MM_TILE), _pick(K, 2 * MM_ROWS, LANES if tb else SUBLANES)
    else:
        tm = _pick(M, MM_ROWS, SUBLANES)
        tk = K if K * tn * 2 <= MM_WEIGHT_TILE_BYTES else _pick(K, MM_TILE)
    nk = K // tk
    dims = (((0 if ta else 1,), (1 if tb else 0,)), ((), ()))

    def body(a_ref, b_ref, o_ref, acc_ref):
        k = pl.program_id(2)
        p = lax.dot_general(a_ref[...].astype(BF16), b_ref[...].astype(BF16), dims, preferred_element_type=F32)
        if nk == 1:
            o_ref[...] = p.astype(o_ref.dtype)
        else:
            @pl.when(k == 0)
            def _():
                acc_ref[...] = p

            @pl.when(k > 0)
            def _():
                acc_ref[...] += p

            @pl.when(k == nk - 1)
            def _():
                o_ref[...] = acc_ref[...].astype(o_ref.dtype)

    a_spec = pl.BlockSpec((tk, tm), lambda j, i, k: (k, i)) if ta else pl.BlockSpec((tm, tk), lambda j, i, k: (i, k))
    b_spec = pl.BlockSpec((tn, tk), lambda j, i, k: (j, k)) if tb else pl.BlockSpec((tk, tn), lambda j, i, k: (k, j))
    return pl.pallas_call(
        body, name=_unique(name), grid=(N // tn, M // tm, nk), in_specs=[a_spec, b_spec],
        out_specs=pl.BlockSpec((tm, tn), lambda j, i, k: (i, j)),
        out_shape=jax.ShapeDtypeStruct((M, N), out_dtype),
        scratch_shapes=[pltpu.VMEM((tm, tn) if nk > 1 else (SUBLANES, LANES), F32)],
        compiler_params=_params(("parallel", "parallel", "arbitrary")))(a, b)


def _silu(x):
    return x * _sigmoid(x)


def _sigmoid(x):
    return 1.0 / (1.0 + jnp.exp(-x))


def _dsilu(x):
    s = _sigmoid(x)
    return s * (1.0 + x * (1.0 - s))


def _gelu(x):
    return 0.5 * x * (1.0 + lax.erf(x * np.float32(math.sqrt(0.5))))


def _dgelu(x):
    cdf = 0.5 * (1.0 + lax.erf(x * np.float32(math.sqrt(0.5))))
    pdf = jnp.exp(-0.5 * x * x) * np.float32(1.0 / math.sqrt(2.0 * math.pi))
    return cdf + x * pdf


def _ln_stats(r):
    mu = jnp.mean(r, axis=-1, keepdims=True)
    xc = r - mu
    var = jnp.mean(xc * xc, axis=-1, keepdims=True)
    rstd = lax.rsqrt(var + LN_EPS)
    return xc * rstd, rstd


def _ln_bwd(dxhat, xhat, rstd):
    m1 = jnp.mean(dxhat, axis=-1, keepdims=True)
    m2 = jnp.mean(dxhat * xhat, axis=-1, keepdims=True)
    return rstd * (dxhat - m1 - xhat * m2)


def _csum(v):
    return jnp.sum(v, axis=0, keepdims=True)


def _split_dot(x, m01, lhs01=False, terms=2):
    acc, rem = None, x
    for _ in range(terms):
        part = rem.astype(BF16)
        rem = rem - part.astype(F32)
        d = jnp.dot(m01, part, preferred_element_type=F32) if lhs01 else jnp.dot(part, m01, preferred_element_type=F32)
        acc = d if acc is None else acc + d
    return acc


def _iota2(shape, dim):
    return lax.broadcasted_iota(jnp.int32, shape, dim)


def _modulate(name, x, sc, sh):
    D = x.shape[2]
    return _rowwise(name, lambda r, bv, cv: ([r[0] * (1.0 + bv[0]) + bv[1]], [], []),
                    [x], [sc, sh], [], out_rows=[(D, BF16)])[0]


def _resid_ln(name, alpha, x, y, g, ln_g, ln_b, ybias=None, then=None):
    D = x.shape[2]

    def fn(r, bv, cv):
        yy = r[1] if ybias is None else r[1] + cv[2]
        xhat, _ = _ln_stats(alpha * r[0] + (1.0 + bv[0]) * yy)
        xn = xhat * cv[0] + cv[1]
        return [xn] + ([xn * (1.0 + bv[1]) + bv[2]] if then else []), [], []
    cvecs = [ln_g, ln_b] + ([] if ybias is None else [ybias])
    return _rowwise(name, fn, [x, y], [g] + list(then or ()), cvecs, out_rows=[(D, F32)] + ([(D, BF16)] if then else []))


def _resid_ln_bwd(name, alpha, dxn, x, y, g, ln_g, ln_b, ybias=None, then=None):
    D = x.shape[2]

    def fn(r, bv, cv):
        yy = r[2] if ybias is None else r[2] + cv[2]
        xhat, rstd = _ln_stats(alpha * r[1] + (1.0 + bv[0]) * yy)
        d, sums = r[0], []
        if then:
            d = alpha * d + r[3] * (1.0 + bv[1])
            sums = [_csum(r[3] * (xhat * cv[0] + cv[1])), _csum(r[3])]
        dr = _ln_bwd(d * cv[0], xhat, rstd)
        dy = (1.0 + bv[0]) * dr
        return [dr, dy], [_csum(dr * yy)] + sums, [_csum(d * xhat), _csum(d), _csum(dy)]
    cvecs = [ln_g, ln_b] + ([] if ybias is None else [ybias])
    rows, bvecs = [dxn, x, y] + ([then[0]] if then else []), [g] + ([then[1]] if then else [])
    return _rowwise(name, fn, rows, bvecs, cvecs, out_rows=[(D, F32), (D, BF16)], out_bsums=[D] * (3 if then else 1),
                    out_tsums=[D, D, D])


def _modulate_bwd(name, alpha, dh, dr, x, sc):
    D = x.shape[2]

    def fn(r, bv, cv):
        return [alpha * r[1] + r[0] * (1.0 + bv[0])], [_csum(r[0] * r[2]), _csum(r[0])], []
    return _rowwise(name, fn, [dh, dr, x], [sc], [], out_rows=[(D, F32)], out_bsums=[D, D])


def _loss_head(name, y, target):
    D = y.shape[2]

    def fn(r, bv, cv):
        e = r[0] - r[1]
        return [e * np.float32(1.0 / D)], [_csum(e * e)], []
    return _rowwise(name, fn, [y, target], [], [], out_rows=[(D, F32)], out_bsums=[D])


def _ffn_in_act(name, h, w_in):
    M, K = h.shape
    Hd = w_in.shape[1] // 2
    tm, tn = _pick(M, MM_ROWS, SUBLANES), _pick(Hd, MM_TILE)
    nj = Hd // tn

    def body(h_ref, wg_ref, wu_ref, zg_ref, zu_ref, act_ref):
        hv = h_ref[...].astype(BF16)
        zg = jnp.dot(hv, wg_ref[...].astype(BF16), preferred_element_type=F32)
        zu = jnp.dot(hv, wu_ref[...].astype(BF16), preferred_element_type=F32)
        zg_ref[...] = zg
        zu_ref[...] = zu
        act_ref[...] = (_silu(zg) * zu).astype(act_ref.dtype)

    tile = pl.BlockSpec((tm, tn), lambda j, i: (i, j))
    return pl.pallas_call(
        body, name=_unique(name), grid=(nj, M // tm),
        in_specs=[pl.BlockSpec((tm, K), lambda j, i: (i, 0)), pl.BlockSpec((K, tn), lambda j, i: (0, j)),
                  pl.BlockSpec((K, tn), lambda j, i: (0, j + nj))],
        out_specs=[tile, tile, tile],
        out_shape=[jax.ShapeDtypeStruct((M, Hd), F32), jax.ShapeDtypeStruct((M, Hd), F32), jax.ShapeDtypeStruct((M, Hd), BF16)],
        compiler_params=_params(("parallel", "parallel")))(h, w_in, w_in)


def _ffn_out_bwd_act(name, dy, w_out, zg, zu):
    M, K = dy.shape
    Hd = w_out.shape[0]
    tm, tn = _pick(M, MM_ROWS, SUBLANES), _pick(Hd, MM_TILE)

    def body(dy_ref, w_ref, zg_ref, zu_ref, dzg_ref, dzu_ref):
        da = lax.dot_general(dy_ref[...].astype(BF16), w_ref[...].astype(BF16), _NT, preferred_element_type=F32)
        gg = zg_ref[...]
        dzg_ref[...] = (da * zu_ref[...] * _dsilu(gg)).astype(dzg_ref.dtype)
        dzu_ref[...] = (da * _silu(gg)).astype(dzu_ref.dtype)

    tile = pl.BlockSpec((tm, tn), lambda j, i: (i, j))
    return pl.pallas_call(
        body, name=_unique(name), grid=(Hd // tn, M // tm),
        in_specs=[pl.BlockSpec((tm, K), lambda j, i: (i, 0)), pl.BlockSpec((tn, K), lambda j, i: (j, 0)), tile, tile],
        out_specs=[tile, tile],
        out_shape=[jax.ShapeDtypeStruct((M, Hd), BF16), jax.ShapeDtypeStruct((M, Hd), BF16)],
        compiler_params=_params(("parallel", "parallel")))(dy, w_out, zg, zu)


def _ffn_in_bwd(name, dzg, dzu, w_in):
    M, Hd = dzg.shape
    N = w_in.shape[0]
    tm, tn = _pick(M, MM_ROWS, SUBLANES), _pick(N, MM_TILE)

    def body(g_ref, u_ref, wg_ref, wu_ref, o_ref):
        o_ref[...] = (lax.dot_general(g_ref[...], wg_ref[...].astype(BF16), _NT, preferred_element_type=F32)
                      + lax.dot_general(u_ref[...], wu_ref[...].astype(BF16), _NT, preferred_element_type=F32))

    rows = pl.BlockSpec((tm, Hd), lambda j, i: (i, 0))
    return pl.pallas_call(
        body, name=_unique(name), grid=(N // tn, M // tm),
        in_specs=[rows, rows, pl.BlockSpec((tn, Hd), lambda j, i: (j, 0)), pl.BlockSpec((tn, Hd), lambda j, i: (j, 1))],
        out_specs=pl.BlockSpec((tm, tn), lambda j, i: (i, j)),
        out_shape=jax.ShapeDtypeStruct((M, N), F32),
        compiler_params=_params(("parallel", "parallel")))(dzg, dzu, w_in, w_in)


def _gm_act(name, zin, b_in, ln_g, ln_b):
    W = zin.shape[2] // 2

    def fn(r, bv, cv):
        z = _gelu(r[0] + cv[0])
        vhat, _ = _ln_stats(z[:, W:])
        return [z[:, :W], vhat * cv[1] + cv[2]], [], []
    return _rowwise(name, fn, [zin], [], [b_in, ln_g, ln_b], out_rows=[(W, F32), (W, BF16)])


def _gm_act_bwd(name, zin, du, dvn, b_in, ln_g):
    W = zin.shape[2] // 2

    def fn(r, bv, cv):
        zz = r[0] + cv[0]
        z = _gelu(zz)
        vhat, rstd = _ln_stats(z[:, W:])
        dv = _ln_bwd(r[2] * cv[1], vhat, rstd)
        dzin = jnp.concatenate([r[1], dv], axis=1) * _dgelu(zz)
        return [dzin], [], [_csum(dzin), _csum(r[2] * vhat), _csum(r[2])]
    return _rowwise(name, fn, [zin, du, dvn], [], [b_in, ln_g], out_rows=[(2 * W, BF16)], out_tsums=[2 * W, W, W])


def _gm_causal_w(ws_ref, g):
    T = ws_ref.shape[1]
    return jnp.where(_iota2((T, T), 1) <= _iota2((T, T), 0), ws_ref[g], 0.0).astype(BF16)


def _gm_spatial(name, u, vn, w_s, b_sT):
    B, S, W = u.shape
    G, T = w_s.shape[0], w_s.shape[1]
    assert W == G * T, "a head group is as wide as a chunk is long"

    def body(u_ref, vn_ref, ws_ref, bs_ref, y_ref):
        for g in range(G):
            cs = slice(g * T, (g + 1) * T)
            sv = jnp.dot(_gm_causal_w(ws_ref, g), vn_ref[0, :, cs], preferred_element_type=F32) + bs_ref[:, g:g + 1]
            y_ref[0, :, cs] = (u_ref[0, :, cs] * sv).astype(y_ref.dtype)

    row = pl.BlockSpec((1, T, W), lambda b, i: (b, i, 0))
    return pl.pallas_call(
        body, name=_unique(name), grid=(B, S // T),
        in_specs=[row, row, pl.BlockSpec((G, T, T), lambda b, i: (0, 0, 0)), pl.BlockSpec((T, G), lambda b, i: (0, 0))],
        out_specs=row, out_shape=jax.ShapeDtypeStruct((B, S, W), BF16),
        compiler_params=_params(("parallel", "parallel")))(u, vn, w_s, b_sT)


def _gm_spatial_bwd(name, dyg, u, vn, w_s, b_sT):
    B, S, W = u.shape
    G, T = w_s.shape[0], w_s.shape[1]
    assert W == G * T, "a head group is as wide as a chunk is long"

    def body(dy_ref, u_ref, vn_ref, ws_ref, bs_ref, du_ref, dvn_ref, dws_ref, dbs_ref):
        first = jnp.logical_and(pl.program_id(0) == 0, pl.program_id(1) == 0)

        @pl.when(first)
        def _():
            dws_ref[...] = jnp.zeros_like(dws_ref)
            dbs_ref[...] = jnp.zeros_like(dbs_ref)

        tril = _iota2((T, T), 1) <= _iota2((T, T), 0)
        for g in range(G):
            cs = slice(g * T, (g + 1) * T)
            wm = _gm_causal_w(ws_ref, g)
            vng = vn_ref[0, :, cs]
            sv = jnp.dot(wm, vng, preferred_element_type=F32) + bs_ref[:, g:g + 1]
            dy = dy_ref[0, :, cs]
            du_ref[0, :, cs] = dy * sv
            dsv = dy * u_ref[0, :, cs]
            dsv16 = dsv.astype(BF16)
            dvn_ref[0, :, cs] = lax.dot_general(wm, dsv16, (((0,), (0,)), ((), ())), preferred_element_type=F32)
            dw = lax.dot_general(dsv16, vng, (((1,), (1,)), ((), ())), preferred_element_type=F32)
            dws_ref[g] += jnp.where(tril, dw, 0.0)
            dbs_ref[:, g:g + 1] += jnp.sum(dsv, axis=1, keepdims=True)

    row = pl.BlockSpec((1, T, W), lambda b, i: (b, i, 0))
    return pl.pallas_call(
        body, name=_unique(name), grid=(B, S // T),
        in_specs=[row, row, row, pl.BlockSpec((G, T, T), lambda b, i: (0, 0, 0)), pl.BlockSpec((T, G), lambda b, i: (0, 0))],
        out_specs=[row, row, pl.BlockSpec((G, T, T), lambda b, i: (0, 0, 0)), pl.BlockSpec((T, G), lambda b, i: (0, 0))],
        out_shape=[jax.ShapeDtypeStruct((B, S, W), F32), jax.ShapeDtypeStruct((B, S, W), F32),
                   jax.ShapeDtypeStruct((G, T, T), F32), jax.ShapeDtypeStruct((T, G), F32)],
        compiler_params=_params(("arbitrary", "arbitrary")))(dyg, u, vn, w_s, b_sT)


def _cv_glu(name, pw, b_in):
    W = pw.shape[2] // 2

    def fn(r, bv, cv):
        z = r[0] + cv[0]
        return [z[:, :W] * _sigmoid(z[:, W:])], [], []
    return _rowwise(name, fn, [pw], [], [b_in], out_rows=[(W, F32)])[0]


def _cv_glu_bwd(name, pw, dyg, b_in):
    W = pw.shape[2] // 2

    def fn(r, bv, cv):
        z = r[0] + cv[0]
        a, s = z[:, :W], _sigmoid(z[:, W:])
        dpw = jnp.concatenate([r[1] * s, r[1] * a * s * (1.0 - s)], axis=1)
        return [dpw], [], [_csum(dpw)]
    return _rowwise(name, fn, [pw, dyg], [], [b_in], out_rows=[(2 * W, BF16)], out_tsums=[2 * W])


def _cv_ln_act(name, yc, ln_g, ln_b):
    D = yc.shape[2]

    def fn(r, bv, cv):
        xhat, _ = _ln_stats(r[0])
        return [_silu(xhat * cv[0] + cv[1])], [], []
    return _rowwise(name, fn, [yc], [], [ln_g, ln_b], out_rows=[(D, BF16)])[0]


def _cv_ln_act_bwd(name, yc, dys, ln_g, ln_b):
    D = yc.shape[2]

    def fn(r, bv, cv):
        xhat, rstd = _ln_stats(r[0])
        dyn = r[1] * _dsilu(xhat * cv[0] + cv[1])
        dyc = _ln_bwd(dyn * cv[0], xhat, rstd)
        return [dyc], [], [_csum(dyn * xhat), _csum(dyn), _csum(dyc)]
    return _rowwise(name, fn, [yc, dys], [], [ln_g, ln_b], out_rows=[(D, F32)], out_tsums=[D, D, D])


CONV_HALO = 32
CONV_TS, CONV_TC = 256, 128


def _dwconv(name, y, dw, dw_b):
    B, S, D = y.shape
    ts, tc, halo, K = _pick(S, CONV_TS, SUBLANES), _pick(D, CONV_TC), CONV_HALO, CONV_WIDTH

    def body(cur_ref, prev_ref, dw_ref, b_ref, o_ref, buf):
        i = pl.program_id(1)
        buf[pl.ds(0, halo), :] = jnp.where(i > 0, prev_ref[0, pl.ds(ts - halo, halo), :], 0.0)
        buf[pl.ds(halo, ts), :] = cur_ref[0]
        acc = jnp.zeros((ts, tc), F32) + b_ref[...]
        for k in range(K):
            acc = acc + dw_ref[k:k + 1, :] * buf[pl.ds(halo - (K - 1) + k, ts), :]
        o_ref[0] = acc

    return pl.pallas_call(
        body, name=_unique(name), grid=(B, S // ts, D // tc),
        in_specs=[pl.BlockSpec((1, ts, tc), lambda b, i, j: (b, i, j)),
                  pl.BlockSpec((1, ts, tc), lambda b, i, j: (b, jnp.maximum(i - 1, 0), j)),
                  pl.BlockSpec((halo, tc), lambda b, i, j: (0, j)), pl.BlockSpec((1, tc), lambda b, i, j: (0, j))],
        out_specs=pl.BlockSpec((1, ts, tc), lambda b, i, j: (b, i, j)),
        out_shape=jax.ShapeDtypeStruct((B, S, D), F32),
        scratch_shapes=[pltpu.VMEM((halo + ts, tc), F32)],
        compiler_params=_params(("parallel", "parallel", "parallel")))(y, y, dw, dw_b)


def _dwconv_bwd(name, dyc, y, dw):
    B, S, D = y.shape
    ts, tc, halo, K = _pick(S, CONV_TS, SUBLANES), _pick(D, CONV_TC), CONV_HALO, CONV_WIDTH
    nt = S // ts

    def body(g_ref, gnext_ref, y_ref, yprev_ref, dw_ref, dy_ref, ddw_ref, gbuf, ybuf):
        b, i = pl.program_id(1), pl.program_id(2)
        first = jnp.logical_and(b == 0, i == 0)

        @pl.when(first)
        def _():
            ddw_ref[...] = jnp.zeros_like(ddw_ref)

        g = g_ref[0]
        gbuf[pl.ds(0, ts), :] = g
        gbuf[pl.ds(ts, halo), :] = jnp.where(i < nt - 1, gnext_ref[0, pl.ds(0, halo), :], 0.0)
        ybuf[pl.ds(0, halo), :] = jnp.where(i > 0, yprev_ref[0, pl.ds(ts - halo, halo), :], 0.0)
        ybuf[pl.ds(halo, ts), :] = y_ref[0]
        acc = jnp.zeros((ts, tc), F32)
        for k in range(K):
            acc = acc + dw_ref[k:k + 1, :] * gbuf[pl.ds(K - 1 - k, ts), :]
            ddw_ref[k:k + 1, :] += _csum(g * ybuf[pl.ds(halo - (K - 1) + k, ts), :])
        dy_ref[0] = acc

    tile = lambda f: pl.BlockSpec((1, ts, tc), f)
    return pl.pallas_call(
        body, name=_unique(name), grid=(D // tc, B, nt),
        in_specs=[tile(lambda j, b, i: (b, i, j)), tile(lambda j, b, i: (b, jnp.minimum(i + 1, nt - 1), j)),
                  tile(lambda j, b, i: (b, i, j)), tile(lambda j, b, i: (b, jnp.maximum(i - 1, 0), j)),
                  pl.BlockSpec((halo, tc), lambda j, b, i: (0, j))],
        out_specs=[tile(lambda j, b, i: (b, i, j)), pl.BlockSpec((halo, tc), lambda j, b, i: (0, j))],
        out_shape=[jax.ShapeDtypeStruct((B, S, D), F32), jax.ShapeDtypeStruct((halo, D), F32)],
        scratch_shapes=[pltpu.VMEM((ts + halo, tc), F32), pltpu.VMEM((halo + ts, tc), F32)],
        compiler_params=_params(("parallel", "arbitrary", "arbitrary")))(dyc, dyc, y, y, dw)


ATT_BLOCK = 128
FOX_TILE = (512, 1024)
SB_TILE = (256, 512)
ATT_PIECE_ROWS = 32
FOX_GATE_COLS = 128


def _att_tiles(S, tile):
    return _pick(S, tile[0], SUBLANES), _pick(S, tile[1], LANES)


def _pieces(T, TK):
    R = min(T, ATT_PIECE_ROWS)
    segs = [slice(c, c + LANES) for c in range(0, TK, LANES)]
    return [(slice(r, r + R), segs) for r in range(0, T, R)]


def _piece_keep(row0, col0, rs, cs, strict, lane_major_of=0):
    shape = (rs.stop - rs.start, cs.stop - cs.start)
    lane = _iota2(shape, 1)
    key = col0 + (lane * lane_major_of + cs.start // LANES if lane_major_of else cs.start + lane)
    qry = row0 + rs.start + _iota2(shape, 0)
    return key < qry if strict else key <= qry


def _causal_tiles(i, tq, tk):
    return (i * tq + tq + tk - 1) // tk


def _lane_major(t, tk):
    B, H, S, dh = t.shape
    return t.reshape(B, H, S // tk, LANES, tk // LANES, dh).swapaxes(3, 4).reshape(B, H, S, dh)


def _lane_major_inverse(t, tk):
    B, H, S, dh = t.shape
    return t.reshape(B, H, S // tk, tk // LANES, LANES, dh).swapaxes(3, 4).reshape(B, H, S, dh)


def _log_sigmoid(x):
    return jnp.minimum(x, 0.0) - jnp.log(1.0 + jnp.exp(-jnp.abs(x)))


def _fox_gate_cumsum(name, fl, b_f):
    B, S, C = fl.shape
    T = _pick(S, ATT_BLOCK, SUBLANES)

    def body(fl_ref, bf_ref, f_ref, carry):
        @pl.when(pl.program_id(1) == 0)
        def _():
            carry[...] = jnp.zeros_like(carry)
        lf = _log_sigmoid(fl_ref[0] + bf_ref[...])
        lower = (_iota2((T, T), 1) <= _iota2((T, T), 0)).astype(BF16)
        f = _split_dot(lf, lower, lhs01=True, terms=3) + carry[...]
        f_ref[0] = f
        carry[...] = f[T - 1:T, :]

    return pl.pallas_call(
        body, name=_unique(name), grid=(B, S // T),
        in_specs=[pl.BlockSpec((1, T, C), lambda b, i: (b, i, 0)), pl.BlockSpec((1, C), lambda b, i: (0, 0))],
        out_specs=pl.BlockSpec((1, T, C), lambda b, i: (b, i, 0)),
        out_shape=jax.ShapeDtypeStruct((B, S, C), F32),
        scratch_shapes=[pltpu.VMEM((1, C), F32)],
        compiler_params=_params(("arbitrary", "arbitrary")))(fl, b_f)


def _fox_gate_bwd(name, dF, fl, b_f, n_heads):
    B, S, C = fl.shape
    T = _pick(S, ATT_BLOCK, SUBLANES)
    nt = S // T

    def body(df_ref, fl_ref, bf_ref, dfl_ref, dbf_ref, carry):
        first = jnp.logical_and(pl.program_id(0) == 0, pl.program_id(1) == 0)

        @pl.when(pl.program_id(1) == 0)
        def _():
            carry[...] = jnp.zeros_like(carry)

        @pl.when(first)
        def _():
            dbf_ref[...] = jnp.zeros_like(dbf_ref)

        upper = (_iota2((T, T), 1) >= _iota2((T, T), 0)).astype(BF16)
        dlf = _split_dot(df_ref[0], upper, lhs01=True, terms=3) + carry[...]
        carry[...] = dlf[0:1, :]
        x = fl_ref[0] + bf_ref[...]
        dfl = jnp.where(_iota2((T, C), 1) < n_heads, dlf * _sigmoid(-x), 0.0)
        dfl_ref[0] = dfl
        dbf_ref[...] += _csum(dfl)

    rev = lambda b, i: (b, nt - 1 - i, 0)
    return pl.pallas_call(
        body, name=_unique(name), grid=(B, nt),
        in_specs=[pl.BlockSpec((1, T, C), rev), pl.BlockSpec((1, T, C), rev), pl.BlockSpec((1, C), lambda b, i: (0, 0))],
        out_specs=[pl.BlockSpec((1, T, C), rev), pl.BlockSpec((1, C), lambda b, i: (0, 0))],
        out_shape=[jax.ShapeDtypeStruct((B, S, C), F32), jax.ShapeDtypeStruct((1, C), F32)],
        scratch_shapes=[pltpu.VMEM((1, C), F32)],
        compiler_params=_params(("arbitrary", "arbitrary")))(dF, fl, b_f)


_NT = (((1,), (1,)), ((), ()))
_TN = (((0,), (0,)), ((), ()))


def _fox_fwd(name, q, k, v, fq, fk):
    B, H, S, dh = q.shape
    T, TK = _att_tiles(S, FOX_TILE)
    scale = np.float32(dh ** -0.5)

    pieces = _pieces(T, TK)

    def body(q_ref, k_ref, v_ref, fq_ref, fk_ref, o_ref, lse_ref, s_scr, p_scr):
        i = pl.program_id(2)
        qb = q_ref[0, 0]
        n_tiles = _causal_tiles(i, T, TK)

        def tile(j, carry, diag):
            m, l, acc = carry
            ks = pl.ds(pl.multiple_of(j * TK, TK), TK)
            s_scr[...] = lax.dot_general(qb, k_ref[0, 0, ks, :], _NT, preferred_element_type=F32) * scale
            fkj = fk_ref[0, 0, pl.ds(j, 1), :]
            m_new = []
            for rc, (rs, segs) in enumerate(pieces):
                fq_c, mx = fq_ref[0, 0, rs, :], None
                for cs in segs:
                    s = s_scr[rs, cs] + fq_c - fkj[:, cs]
                    if diag:
                        s = jnp.where(_piece_keep(i * T, j * TK, rs, cs, False), s, NEG_INF)
                    s_scr[rs, cs] = s
                    mx = s if mx is None else jnp.maximum(mx, s)
                m_new.append(jnp.maximum(m[rc], jnp.max(mx, axis=1, keepdims=True)))
            alpha, l_new = [], []
            for rc, (rs, segs) in enumerate(pieces):
                alpha.append(jnp.exp(m[rc] - m_new[rc]))
                psum = None
                for cs in segs:
                    p = jnp.exp(s_scr[rs, cs] - m_new[rc])
                    p_scr[rs, cs] = p.astype(BF16)
                    psum = p if psum is None else psum + p
                l_new.append(alpha[rc] * l[rc] + jnp.sum(psum, axis=1, keepdims=True))
            acc = jnp.concatenate(alpha, axis=0) * acc + jnp.dot(p_scr[...], v_ref[0, 0, ks, :], preferred_element_type=F32)
            return tuple(m_new), tuple(l_new), acc

        init = (tuple(jnp.full((rs.stop - rs.start, 1), NEG_INF, F32) for rs, _ in pieces),
                tuple(jnp.zeros((rs.stop - rs.start, 1), F32) for rs, _ in pieces), jnp.zeros((T, dh), F32))
        carry = lax.fori_loop(0, n_tiles - 1, lambda j, c: tile(j, c, False), init)
        m, l, acc = tile(n_tiles - 1, carry, True)
        m, l = jnp.concatenate(m, axis=0), jnp.concatenate(l, axis=0)
        o_ref[0, 0] = (acc / l).astype(o_ref.dtype)
        lse_ref[0, 0] = m + jnp.log(l)

    full = lambda w: pl.BlockSpec((1, 1, S, w), lambda b, h, i: (b, h, 0, 0))
    blk = lambda w: pl.BlockSpec((1, 1, T, w), lambda b, h, i: (b, h, i, 0))
    return pl.pallas_call(
        body, name=_unique(name), grid=(B, H, S // T),
        in_specs=[blk(dh), full(dh), full(dh), blk(1), pl.BlockSpec((1, 1, S // TK, TK), lambda b, h, i: (b, h, 0, 0))],
        out_specs=[blk(dh), blk(1)],
        out_shape=[jax.ShapeDtypeStruct((B, H, S, dh), BF16), jax.ShapeDtypeStruct((B, H, S, 1), F32)],
        scratch_shapes=[pltpu.VMEM((T, TK), F32), pltpu.VMEM((T, TK), BF16)],
        compiler_params=_params(("parallel", "parallel", "parallel")))(q, k, v, fq, fk)


def _fox_bwd(name, q, k, v, fq, fk, do, lse):
    B, H, S, dh = q.shape
    T, TK = _att_tiles(S, FOX_TILE)
    nt, nkt = S // T, S // TK
    scale = np.float32(dh ** -0.5)

    pieces = _pieces(T, TK)

    def body(q_ref, k_ref, v_ref, fq_ref, fk_ref, do_ref, lse_ref, dq_ref, dk_ref, dv_ref, dfk_ref,
             p_buf, dp_buf, s_scr, ds_scr, p16_scr, dk_acc, dv_acc):
        dk_acc[...] = jnp.zeros_like(dk_acc)
        dv_acc[...] = jnp.zeros_like(dv_acc)
        dfk_ref[...] = jnp.zeros_like(dfk_ref)

        def qloop(i, _):
            qs = pl.ds(pl.multiple_of(i * T, T), T)
            qb, dob16 = q_ref[0, 0, qs, :], do_ref[0, 0, qs, :].astype(BF16)
            n_tiles = _causal_tiles(i, T, TK)
            row_at = lambda rs: pl.ds(pl.multiple_of(i * T + rs.start, SUBLANES), rs.stop - rs.start)
            fq_c = [fq_ref[0, 0, row_at(rs), :] for rs, _ in pieces]
            lse_c = [lse_ref[0, 0, row_at(rs), :] for rs, _ in pieces]

            def sweep1(j, delta, diag):
                ks = pl.ds(pl.multiple_of(j * TK, TK), TK)
                s_scr[...] = lax.dot_general(qb, k_ref[0, 0, ks, :], _NT, preferred_element_type=F32) * scale
                dp_buf[j] = lax.dot_general(dob16, v_ref[0, 0, ks, :], _NT, preferred_element_type=F32)
                fkj = fk_ref[0, 0, pl.ds(j, 1), :]
                out = []
                for rc, (rs, segs) in enumerate(pieces):
                    pdp = None
                    for cs in segs:
                        p = jnp.exp(((s_scr[rs, cs] + fq_c[rc]) - fkj[:, cs]) - lse_c[rc])
                        if diag:
                            p = jnp.where(_piece_keep(i * T, j * TK, rs, cs, False), p, 0.0)
                        p_buf[j, rs, cs] = p
                        pdp = p * dp_buf[j, rs, cs] if pdp is None else pdp + p * dp_buf[j, rs, cs]
                    out.append(delta[rc] + jnp.sum(pdp, axis=1, keepdims=True))
                return tuple(out)

            zeros = tuple(jnp.zeros((rs.stop - rs.start, 1), F32) for rs, _ in pieces)
            delta = lax.fori_loop(0, n_tiles - 1, lambda j, d: sweep1(j, d, False), zeros)
            delta = sweep1(n_tiles - 1, delta, True)

            def sweep2(j, dq):
                ks = pl.ds(pl.multiple_of(j * TK, TK), TK)
                col = [None] * len(pieces[0][1])
                for rc, (rs, segs) in enumerate(pieces):
                    for sg, cs in enumerate(segs):
                        p = p_buf[j, rs, cs]
                        ds = p * (dp_buf[j, rs, cs] - delta[rc])
                        ds_scr[rs, cs] = ds.astype(BF16)
                        p16_scr[rs, cs] = p.astype(BF16)
                        col[sg] = ds if col[sg] is None else col[sg] + ds
                dfk_ref[0, 0, pl.ds(j, 1), :] -= jnp.concatenate([_csum(c) for c in col], axis=1)
                ds16 = ds_scr[...]
                dk_acc[ks, :] += lax.dot_general(ds16, qb, _TN, preferred_element_type=F32)
                dv_acc[ks, :] += lax.dot_general(p16_scr[...], dob16, _TN, preferred_element_type=F32)
                return dq + jnp.dot(ds16, k_ref[0, 0, ks, :], preferred_element_type=F32)

            dq_ref[0, 0, qs, :] = (lax.fori_loop(0, n_tiles, sweep2, jnp.zeros((T, dh), F32)) * scale).astype(dq_ref.dtype)
            return 0

        lax.fori_loop(0, nt, qloop, 0)
        dk_ref[0, 0] = (dk_acc[...] * scale).astype(dk_ref.dtype)
        dv_ref[0, 0] = dv_acc[...].astype(dv_ref.dtype)

    full = lambda w: pl.BlockSpec((1, 1, S, w), lambda b, h: (b, h, 0, 0))
    fks = pl.BlockSpec((1, 1, nkt, TK), lambda b, h: (b, h, 0, 0))
    return pl.pallas_call(
        body, name=_unique(name), grid=(B, H),
        in_specs=[full(dh), full(dh), full(dh), full(1), fks, full(dh), full(1)],
        out_specs=[full(dh), full(dh), full(dh), fks],
        out_shape=[jax.ShapeDtypeStruct((B, H, S, dh), BF16)] * 3 + [jax.ShapeDtypeStruct((B, H, nkt, TK), F32)],
        scratch_shapes=[pltpu.VMEM((nkt, T, TK), F32), pltpu.VMEM((nkt, T, TK), F32), pltpu.VMEM((T, TK), F32),
                        pltpu.VMEM((T, TK), BF16), pltpu.VMEM((T, TK), BF16), pltpu.VMEM((S, dh), F32), pltpu.VMEM((S, dh), F32)],
        compiler_params=_params(("parallel", "parallel")))(q, k, v, fq, fk, do, lse)


def _sb_terms(z, with_sigmoids=True):
    t = jnp.exp(-jnp.abs(z))
    lp = jnp.log(1.0 + t)
    lb, l1 = jnp.minimum(z, 0.0) - lp, jnp.minimum(-z, 0.0) - lp
    if not with_sigmoids:
        return lb, l1, None, None
    return lb, l1, jnp.exp(lb), jnp.exp(l1)


SCAN_RADIX = 4


def _lane_scan(x, reverse):
    lane = _iota2(x.shape, 1)
    y, d = x, 1
    while d < LANES:
        step = y
        for m in range(1, SCAN_RADIX):
            if m * d < LANES:
                if reverse:
                    step = step + jnp.where(lane + m * d < LANES, pltpu.roll(y, LANES - m * d, 1), 0.0)
                else:
                    step = step + jnp.where(lane >= m * d, pltpu.roll(y, m * d, 1), 0.0)
        y, d = step, d * SCAN_RADIX
    return y


def _chunk_scan(xs, reverse):
    n = len(xs)
    within, acc = [None] * n, None
    for s in (range(n - 1, -1, -1) if reverse else range(n)):
        acc = xs[s] if acc is None else acc + xs[s]
        within[s] = acc
    lanes = _lane_scan(acc, reverse)
    beyond = lanes - acc
    return [w + beyond for w in within], (lanes[:, 0:1] if reverse else lanes[:, LANES - 1:LANES])


SB_DEAD = 110.0


def _sb_fwd(name, q, k, v):
    B, H, S, dh = q.shape
    T, TK = _att_tiles(S, SB_TILE)
    scale = np.float32(dh ** -0.5)
    pieces = _pieces(T, TK)

    def body(q_ref, k_ref, v_ref, o_ref, lt_ref, first_ref, z_scr, a_scr):
        i = pl.program_id(2)
        qb = q_ref[0, 0]
        n_tiles = _causal_tiles(i, T, TK)

        def tile(j, carry, diag):
            runs, acc = carry
            ks = pl.ds(pl.multiple_of(j * TK, TK), TK)
            z_scr[...] = lax.dot_general(qb, k_ref[0, 0, ks, :], _NT, preferred_element_type=F32) * scale
            new_runs = []
            for rc, (rs, segs) in enumerate(pieces):
                terms = [_sb_terms(z_scr[rs, cs], False) for cs in segs]
                keep = [_piece_keep(i * T, j * TK, rs, cs, True, len(segs)) if diag else None for cs in segs]
                l1 = [jnp.where(kp, t[1], 0.0) if diag else t[1] for kp, t in zip(keep, terms)]
                right_of, total = _chunk_scan(l1, True)
                for cs, kp, t, x, r in zip(segs, keep, terms, l1, right_of):
                    a = jnp.exp(t[0] + ((r - x) + runs[rc]))
                    a_scr[rs, cs] = (jnp.where(kp, a, 0.0) if diag else a).astype(BF16)
                new_runs.append(runs[rc] + total)
            return tuple(new_runs), acc + jnp.dot(a_scr[...], v_ref[0, 0, ks, :], preferred_element_type=F32)

        init = (tuple(jnp.zeros((rs.stop - rs.start, 1), F32) for rs, _ in pieces), jnp.zeros((T, dh), F32))
        def some_row_alive(runs):
            worst = runs[0]
            for r in runs[1:]:
                worst = jnp.maximum(worst, r)
            return jnp.max(worst) > -SB_DEAD

        def step(c):
            runs, acc = tile(n_tiles - 1 - c[0], (c[1], c[2]), False)
            return c[0] + 1, runs, acc

        visited, runs, acc = lax.while_loop(lambda c: jnp.logical_and(c[0] < n_tiles, some_row_alive(c[1])), step,
                                            (jnp.int32(1), *tile(n_tiles - 1, init, True)))
        o_ref[0, 0] = acc.astype(o_ref.dtype)
        lt_ref[0, 0] = jnp.concatenate(runs, axis=0)
        first_ref[pl.program_id(0), pl.program_id(1), i] = (n_tiles - visited).astype(F32)

    full = lambda w: pl.BlockSpec((1, 1, S, w), lambda b, h, i: (b, h, 0, 0))
    blk = lambda w: pl.BlockSpec((1, 1, T, w), lambda b, h, i: (b, h, i, 0))
    return pl.pallas_call(
        body, name=_unique(name), grid=(B, H, S // T),
        in_specs=[blk(dh), full(dh), full(dh)], out_specs=[blk(dh), blk(1), pl.BlockSpec(memory_space=pltpu.SMEM)],
        out_shape=[jax.ShapeDtypeStruct((B, H, S, dh), BF16), jax.ShapeDtypeStruct((B, H, S, 1), F32),
                   jax.ShapeDtypeStruct((B, H, S // T), F32)],
        scratch_shapes=[pltpu.VMEM((T, TK), F32), pltpu.VMEM((T, TK), BF16)],
        compiler_params=_params(("arbitrary", "arbitrary", "arbitrary")))(q, k, v)


def _sb_bwd(name, q, k, v, do, lt, first):
    B, H, S, dh = q.shape
    T, TK = _att_tiles(S, SB_TILE)
    nt = S // T
    scale = np.float32(dh ** -0.5)

    pieces = _pieces(T, TK)

    def body(first_ref, q_ref, k_ref, v_ref, do_ref, lt_ref, dq_ref, dk_ref, dv_ref, z_scr, da_scr, dz_scr, a_scr, dk_acc, dv_acc):
        dk_acc[...] = jnp.zeros_like(dk_acc)
        dv_acc[...] = jnp.zeros_like(dv_acc)
        b, h = pl.program_id(0), pl.program_id(1)

        def qloop(i, _):
            qs = pl.ds(pl.multiple_of(i * T, T), T)
            qb, dob16 = q_ref[0, 0, qs, :], do_ref[0, 0, qs, :].astype(BF16)
            n_tiles = _causal_tiles(i, T, TK)
            first_tile = jnp.clip(first_ref[b, h, i].astype(jnp.int32), 0, n_tiles - 1)
            lt_c = [lt_ref[0, 0, pl.ds(pl.multiple_of(i * T + rs.start, SUBLANES), rs.stop - rs.start), :] for rs, _ in pieces]

            def tile(j, carry, diag):
                sums_l, sums_e, dq = carry
                ks = pl.ds(pl.multiple_of(j * TK, TK), TK)
                kb, vb = k_ref[0, 0, ks, :], v_ref[0, 0, ks, :]
                z_scr[...] = lax.dot_general(qb, kb, _NT, preferred_element_type=F32) * scale
                da_scr[...] = lax.dot_general(dob16, vb, _NT, preferred_element_type=F32)
                new_l, new_e = [], []
                for rc, (rs, segs) in enumerate(pieces):
                    terms = [_sb_terms(z_scr[rs, cs]) for cs in segs]
                    keep = [_piece_keep(i * T, j * TK, rs, cs, True, len(segs)) if diag else None for cs in segs]
                    l1 = [jnp.where(kp, t[1], 0.0) if diag else t[1] for kp, t in zip(keep, terms)]
                    upto, total_l = _chunk_scan(l1, False)
                    es = []
                    for cs, kp, t, u in zip(segs, keep, terms, upto):
                        a = jnp.exp(t[0] + (lt_c[rc] - (u + sums_l[rc])))
                        if diag:
                            a = jnp.where(kp, a, 0.0)
                        a_scr[rs, cs] = a.astype(BF16)
                        es.append(da_scr[rs, cs] * a)
                    e_upto, total_e = _chunk_scan(es, False)
                    for cs, kp, t, e, eu in zip(segs, keep, terms, es, e_upto):
                        dz = e * t[3] - ((eu - e) + sums_e[rc]) * t[2]
                        dz_scr[rs, cs] = (jnp.where(kp, dz, 0.0) if diag else dz).astype(BF16)
                    new_l.append(sums_l[rc] + total_l)
                    new_e.append(sums_e[rc] + total_e)
                dz16 = dz_scr[...]
                dk_acc[ks, :] += lax.dot_general(dz16, qb, _TN, preferred_element_type=F32)
                dv_acc[ks, :] += lax.dot_general(a_scr[...], dob16, _TN, preferred_element_type=F32)
                return tuple(new_l), tuple(new_e), dq + jnp.dot(dz16, kb, preferred_element_type=F32)

            zeros = tuple(jnp.zeros((rs.stop - rs.start, 1), F32) for rs, _ in pieces)
            carry = lax.fori_loop(first_tile, n_tiles - 1, lambda j, c: tile(j, c, False), (zeros, zeros, jnp.zeros((T, dh), F32)))
            dq_ref[0, 0, qs, :] = (tile(n_tiles - 1, carry, True)[2] * scale).astype(dq_ref.dtype)
            return 0

        lax.fori_loop(0, nt, qloop, 0)
        dk_ref[0, 0] = (dk_acc[...] * scale).astype(dk_ref.dtype)
        dv_ref[0, 0] = dv_acc[...].astype(dv_ref.dtype)

    full = lambda w: pl.BlockSpec((1, 1, S, w), lambda b, h: (b, h, 0, 0))
    return pl.pallas_call(
        body, name=_unique(name), grid=(B, H),
        in_specs=[pl.BlockSpec(memory_space=pltpu.SMEM), full(dh), full(dh), full(dh), full(dh), full(1)], out_specs=[full(dh)] * 3,
        out_shape=[jax.ShapeDtypeStruct((B, H, S, dh), BF16)] * 3,
        scratch_shapes=[pltpu.VMEM((T, TK), F32), pltpu.VMEM((T, TK), F32), pltpu.VMEM((T, TK), BF16), pltpu.VMEM((T, TK), BF16),
                        pltpu.VMEM((S, dh), F32), pltpu.VMEM((S, dh), F32)],
        compiler_params=_params(("parallel", "parallel")))(first, q, k, v, do, lt)


def _adamw(name, w, g, m, v):
    shape = w.shape
    n = w.size
    cols = shape[-1] if (w.ndim >= 2 and (shape[-1] % LANES == 0 or n // shape[-1] >= LANES)) else 0
    if cols:
        prep = lambda t: t.reshape(1, n // cols, cols)
    else:
        cols = LANES
        pad = (-n) % (SUBLANES * LANES)
        prep = lambda t: jnp.pad(t.reshape(-1), (0, pad), constant_values=1.0).reshape(1, (n + pad) // cols, cols)

    def fn(r, bv, cv):
        w_, g_, m_, v_ = r
        m2 = ADAM_B1 * m_ + (1.0 - ADAM_B1) * g_
        v2 = ADAM_B2 * v_ + (1.0 - ADAM_B2) * (g_ * g_)
        m_hat = m2 / (1.0 - ADAM_B1 ** ADAM_STEP)
        v_hat = v2 / (1.0 - ADAM_B2 ** ADAM_STEP)
        return [-ADAM_LR * (m_hat / (jnp.sqrt(v_hat) + ADAM_EPS) + ADAM_WD * w_), m2, v2], [], []
    outs = _rowwise(name, fn, [prep(w), prep(g), prep(m), prep(v)], out_rows=[(cols, F32)] * 3, tm=512)
    return [o.reshape(-1)[:n].reshape(shape) for o in outs]


def _sum8(name, parts):
    def fn(r, bv, cv):
        s = r[0]
        for t in r[1:]:
            s = s + t
        return [s], [], []
    rows = [parts[i][None] for i in range(parts.shape[0])]
    return _rowwise(name, fn, rows, out_rows=[(parts.shape[2], F32)])[0][0]


def _pack(arrs, cols, dtype, row_mult):
    flat = jnp.concatenate([a.reshape(-1).astype(dtype) for a in arrs])
    pad = (-flat.size) % (cols * row_mult)
    return jnp.pad(flat, (0, pad)).reshape(-1, cols)


def _unpack(flat, shapes):
    out, off = [], 0
    for s in shapes:
        n = int(np.prod(s))
        out.append(flat[off:off + n].reshape(s))
        off += n
    return out


def _heads(t, H):
    B, S, W = t.shape
    return t.reshape(B, S, H, W // H).transpose(0, 2, 1, 3)


def _unheads(t):
    B, H, S, dh = t.shape
    return t.transpose(0, 2, 1, 3).reshape(B, S, H * dh)


def kernel(*args):
    _names_used.clear()
    p = dict(zip(ARGS, args))
    x, target = p['x'], p['loss_target']
    B, S, D = x.shape
    T = B * S
    depth = p['ln1_g'].shape[0]
    H = D // HEAD_DIM
    alpha = np.float32((2.0 * depth) ** 0.25)
    cx, cy, cc = _mesh_pos()
    my_q = 2 * cx + cy
    axes = ("x", "y", "c")
    two = lambda t: t.reshape(T, t.shape[-1])
    three = lambda t: t.reshape(B, S, t.shape[-1])

    small_in = [p['c']] + [p[n] for n in SMALL_SPLIT]
    g1 = _all_gather8("ag_small", [_pack(small_in, LANES, F32, SUBLANES)])[0]
    g1 = g1.reshape(N_DEV, -1)
    c_all = g1[:, :B * D].reshape(N_DEV * B, D)
    per_chip = [_unpack(g1[2 * q], [a.shape for a in small_in])[1:] for q in range(N_CHIPS)]
    small = {n: jnp.concatenate([per_chip[q][i] for q in range(N_CHIPS)], axis=-1) for i, n in enumerate(SMALL_SPLIT)}
    for n in SMALL_REPL:
        small[n] = p[n]

    n_seq = N_DEV * B
    seq_pad = -(-n_seq // LANES) * LANES
    c_act = _rowwise("c_act", lambda r, bv, cv: ([_silu(r[0])], [], []),
                     [jnp.pad(c_all, ((0, seq_pad - n_seq), (0, 0)))[None]], out_rows=[(D, F32)])[0][0]
    mod_cols = p['mod_w'].shape[2]
    mod_part = jnp.stack([_mm(f"mod_fwd{l}", c_act, p['mod_w'][l])[:n_seq] for l in range(depth)])
    half_layers = depth // 2
    mod_half = lax.dynamic_slice_in_dim(mod_part, cc * half_layers, half_layers, axis=0)
    gm_ = _all_gather8("ag_mod", [mod_half.reshape(half_layers * n_seq, mod_cols)])[0]
    mod_all = gm_.reshape(N_CHIPS, 2, half_layers, n_seq, mod_cols).transpose(1, 2, 3, 0, 4).reshape(depth, n_seq, 6 * D)
    mod_mine = lax.dynamic_slice_in_dim(mod_all, (2 * my_q + cc) * B, B, axis=1)
    mod = _rowwise("mod_bias", lambda r, bv, cv: ([r[0] + bv[0]], [], []), [mod_mine], [p['mod_b'][:, None, :]],
                   out_rows=[(6 * D, F32)])[0]
    mods = [[mod[l, :, None, i * D:(i + 1) * D] for i in range(6)] for l in range(depth)]

    big_names = list(BIG)
    shard_shapes = [p[n].shape for n in big_names]
    half_rows = [s[0] * s[1] // 2 for s in shard_shapes]
    w_halves = [lax.dynamic_slice_in_dim(p[n].reshape(-1, s[2]), cc * hr, hr, axis=0).astype(BF16)
                for n, s, hr in zip(big_names, shard_shapes, half_rows)]
    W = {}
    for n, s, g in zip(big_names, shard_shapes, _all_gather8("ag_weights", w_halves)):
        seg = g.reshape((N_CHIPS,) + s)
        W[n] = jnp.concatenate([seg[q] for q in range(N_CHIPS)], axis=BIG[n])

    def vec(n, j):
        return small[n][j][None, :]

    def attn_proj(h1, w_in, gate_cols):
        wp = jnp.pad(w_in, ((0, 0), (0, gate_cols))) if gate_cols else w_in
        proj = three(_mm("att_proj", two(h1), wp))
        q, k, v = [_heads(proj[..., i * D:(i + 1) * D].astype(BF16), H) for i in range(3)]
        return wp, proj, q, k, v

    def gm_fwd(j, h1):
        zin = three(_mm("gm_in", two(h1), W['gm_w_in'][j]))
        u, vn = _gm_act("gm_act", zin, vec('gm_b_in', j), vec('gm_ln_g', j), vec('gm_ln_b', j))
        b_sT = small['gm_b_s'][j].T
        yg = _gm_spatial("gm_spatial", u, vn, small['gm_w_s'][j], b_sT)
        return three(_mm("gm_out", two(yg), W['gm_w_out'][j])), (zin, u, vn, b_sT, yg)

    def gm_bwd(j, h1, dy1, cache):
        zin, u, vn, b_sT, yg = cache
        g = {'gm_w_out': _mm("gm_dwout", two(yg), two(dy1), ta=True)}
        dyg = three(_mm("gm_dyg", two(dy1), W['gm_w_out'][j], tb=True))
        du, dvn, dws, dbsT = _gm_spatial_bwd("gm_spatial_bwd", dyg, u, vn, small['gm_w_s'][j], b_sT)
        dzin, g['gm_b_in'], g['gm_ln_g'], g['gm_ln_b'] = _gm_act_bwd("gm_act_bwd", zin, du, dvn, vec('gm_b_in', j), vec('gm_ln_g', j))
        g['gm_w_s'], g['gm_b_s'] = dws, dbsT.T
        g['gm_w_in'] = _mm("gm_dwin", two(h1), two(dzin), ta=True)
        return _mm("gm_dh", two(dzin), W['gm_w_in'][j], tb=True), g

    def fox_fwd(j, h1):
        wp, proj, q, k, v = attn_proj(h1, W['fox_w_in'][j], 3 * D + FOX_GATE_COLS - W['fox_w_in'].shape[2])
        fl = proj[..., 3 * D:]
        bf = jnp.pad(small['fox_b_f'][j][None, :], ((0, 0), (0, FOX_GATE_COLS - H)))
        Fh = _fox_gate_cumsum("fox_gate", fl, bf)[..., :H].transpose(0, 2, 1)
        fq, fk = Fh[..., None], Fh.reshape(B, H, -1, _att_tiles(S, FOX_TILE)[1])
        o, lse = _fox_fwd("fox_fwd", q, k, v, fq, fk)
        o2 = _unheads(o)
        return three(_mm("fox_out", two(o2), W['fox_w_out'][j])), (wp, q, k, v, fl, bf, fq, fk, lse, o2)

    def fox_bwd(j, h1, dy1, cache):
        wp, q, k, v, fl, bf, fq, fk, lse, o2 = cache
        g = {'fox_w_out': _mm("fox_dwout", two(o2), two(dy1), ta=True)}
        do = _heads(three(_mm("fox_do", two(dy1), W['fox_w_out'][j], tb=True, out_dtype=BF16)), H)
        dq, dk, dv, dfk = _fox_bwd("fox_bwd", q, k, v, fq, fk, do, lse)
        dF = jnp.pad(dfk.reshape(B, H, S).transpose(0, 2, 1), ((0, 0), (0, 0), (0, FOX_GATE_COLS - H)))
        dfl, dbf = _fox_gate_bwd("fox_gate_bwd", dF, fl, bf, H)
        dproj = jnp.concatenate([_unheads(dq).astype(BF16), _unheads(dk).astype(BF16), _unheads(dv).astype(BF16),
                                 dfl.astype(BF16)], axis=-1)
        g['fox_w_in'] = _mm("fox_dwin", two(h1), two(dproj), ta=True)[:, :W['fox_w_in'].shape[2]]
        g['fox_b_f'] = dbf[0, :H]
        return _mm("fox_dh", two(dproj), wp, tb=True), g

    def sb_fwd(j, h1):
        wp, proj, q, k, v = attn_proj(h1, W['sb_w_in'][j], 0)
        k, v = _lane_major(k, _att_tiles(S, SB_TILE)[1]), _lane_major(v, _att_tiles(S, SB_TILE)[1])
        o, lt, first = _sb_fwd("sb_fwd", q, k, v)
        o2 = _unheads(o)
        return three(_mm("sb_out", two(o2), W['sb_w_out'][j])), (q, k, v, lt, first, o2)

    def sb_bwd(j, h1, dy1, cache):
        q, k, v, lt, first, o2 = cache
        g = {'sb_w_out': _mm("sb_dwout", two(o2), two(dy1), ta=True)}
        do = _heads(three(_mm("sb_do", two(dy1), W['sb_w_out'][j], tb=True, out_dtype=BF16)), H)
        dq, dk, dv = _sb_bwd("sb_bwd", q, k, v, do, lt, first)
        dk, dv = _lane_major_inverse(dk, _att_tiles(S, SB_TILE)[1]), _lane_major_inverse(dv, _att_tiles(S, SB_TILE)[1])
        dproj = jnp.concatenate([_unheads(dq).astype(BF16), _unheads(dk).astype(BF16), _unheads(dv).astype(BF16)], axis=-1)
        g['sb_w_in'] = _mm("sb_dwin", two(h1), two(dproj), ta=True)
        return _mm("sb_dh", two(dproj), W['sb_w_in'][j], tb=True), g

    def cv_fwd(j, h1):
        pw = three(_mm("cv_in", two(h1), W['cv_w_in'][j]))
        ygl = _cv_glu("cv_glu", pw, vec('cv_b_in', j))
        dw = jnp.pad(small['cv_dw'][j], ((0, CONV_HALO - CONV_WIDTH), (0, 0)))
        yc = _dwconv("cv_dwconv", ygl, dw, vec('cv_dw_b', j))
        ys = _cv_ln_act("cv_ln_act", yc, vec('cv_ln_g', j), vec('cv_ln_b', j))
        return three(_mm("cv_out", two(ys), W['cv_w_out'][j])), (pw, ygl, dw, yc, ys)

    def cv_bwd(j, h1, dy1, cache):
        pw, ygl, dw, yc, ys = cache
        g = {'cv_w_out': _mm("cv_dwout", two(ys), two(dy1), ta=True)}
        dys = three(_mm("cv_dys", two(dy1), W['cv_w_out'][j], tb=True))
        dyc, g['cv_ln_g'], g['cv_ln_b'], g['cv_dw_b'] = _cv_ln_act_bwd("cv_ln_act_bwd", yc, dys, vec('cv_ln_g', j), vec('cv_ln_b', j))
        dygl, ddw = _dwconv_bwd("cv_dwconv_bwd", dyc, ygl, dw)
        g['cv_dw'] = ddw[:CONV_WIDTH]
        dpw, g['cv_b_in'] = _cv_glu_bwd("cv_glu_bwd", pw, dygl, vec('cv_b_in', j))
        g['cv_w_in'] = _mm("cv_dwin", two(h1), two(dpw), ta=True)
        return _mm("cv_dh", two(dpw), W['cv_w_in'][j], tb=True), g

    mixers = [(gm_fwd, gm_bwd), (fox_fwd, fox_bwd), (sb_fwd, sb_bwd), (cv_fwd, cv_bwd)]
    n_mix = len(mixers)

    saved = []
    h1 = _modulate("mod1", x, mods[0][1], mods[0][0])
    for l in range(depth):
        m, j = l % n_mix, l // n_mix
        sh1, sc1, g1_, sh2, sc2, g2_ = mods[l]
        ybias = vec('cv_b_out', j) if m == 3 else None
        y1, cache = mixers[m][0](j, h1)
        xm, h2 = _resid_ln("resid_ln1", alpha, x, y1, g1_, small['ln1_g'][l][None], small['ln1_b'][l][None], ybias, then=(sc2, sh2))
        *z, a = _ffn_in_act("ffn_in", two(h2), W['ffn_w_in'][l])
        y2 = three(_mm("ffn_out", a, W['ffn_w_out'][l]))
        nxt = (mods[l + 1][1], mods[l + 1][0]) if l + 1 < depth else None
        xo, *h_next = _resid_ln("resid_ln2", alpha, xm, y2, g2_, small['ln2_g'][l][None], small['ln2_b'][l][None], then=nxt)
        saved.append((x, h1, y1, cache, xm, h2, z, a, y2, ybias))
        x, h1 = xo, (h_next[0] if h_next else None)

    dx, sq = _loss_head("loss_head", x, target)
    loss = lax.psum(jnp.sum(sq) * np.float32(0.5 / D), axes)

    grads = {n: [None] * p[n].shape[0] for n in WEIGHTS}
    parts = [dict() for _ in range(depth)]
    after = None
    for l in reversed(range(depth)):
        m, j = l % n_mix, l // n_mix
        sh1, sc1, g1_, sh2, sc2, g2_ = mods[l]
        x_in, h1, y1, cache, xm, h2, z, a, y2, ybias = saved[l]
        ln2 = (small['ln2_g'][l][None], small['ln2_b'][l][None])
        if after is None:
            dr2, dy2, parts[l]['g2'], grads['ln2_g'][l], grads['ln2_b'][l], _ = _resid_ln_bwd("resid_ln2_bwd", alpha, dx, xm, y2, g2_, *ln2)
        else:
            (dr2, dy2, parts[l]['g2'], parts[l + 1]['sc1'], parts[l + 1]['sh1'], grads['ln2_g'][l], grads['ln2_b'][l], _) = _resid_ln_bwd(
                "resid_ln2_bwd", alpha, after[0], xm, y2, g2_, *ln2, then=after[1:])
        grads['ffn_w_out'][l] = _mm("ffn_dwout", a, two(dy2), ta=True)
        dz = _ffn_out_bwd_act("ffn_da", two(dy2), W['ffn_w_out'][l], *z)
        grads['ffn_w_in'][l] = jnp.concatenate([_mm("ffn_dwin", two(h2), t, ta=True) for t in dz], axis=1)
        dh2 = three(_ffn_in_bwd("ffn_dh", *dz, W['ffn_w_in'][l]))
        (dr1, dy1, parts[l]['g1'], parts[l]['sc2'], parts[l]['sh2'], grads['ln1_g'][l], grads['ln1_b'][l], dyb) = _resid_ln_bwd(
            "resid_ln1_bwd", alpha, dr2, x_in, y1, g1_, small['ln1_g'][l][None], small['ln1_b'][l][None], ybias, then=(dh2, sc2))
        dh1, mg = mixers[m][1](j, h1, dy1, cache)
        if m == 3:
            mg['cv_b_out'] = dyb
        for n, gval in mg.items():
            grads[n][j] = gval
        after = (dr1, three(dh1), sc1)
    grad_x, parts[0]['sc1'], parts[0]['sh1'] = _modulate_bwd("mod1_bwd", alpha, after[1], after[0], saved[0][0], after[2])
    dmod = [jnp.concatenate([pt[k] for k in ('sh1', 'sc1', 'g1', 'sh2', 'sc2', 'g2')], axis=-1)[:, 0, :] for pt in parts]
    dmod = jnp.stack(dmod)
    grads['mod_b'] = [jnp.sum(dmod[l], axis=0) for l in range(depth)]
    full_shape = {n: tuple(t.shape) for n, t in small.items()}

    small_names = SMALL_REPL + SMALL_SPLIT
    small_parts = [jnp.stack([gv.reshape(full_shape[n][1:]) for gv in grads[n]]) for n in small_names]
    pack_a = _pack([dmod], LANES, F32, SUBLANES)
    pack_b = _pack(small_parts, LANES, F32, SUBLANES)
    g2 = _all_gather8("ag_grads_small", [jnp.concatenate([pack_a, pack_b], axis=0)])[0]
    rows_a = pack_a.shape[0]
    dmod_all = g2[:, :rows_a].reshape(N_DEV, -1)[:, :dmod.size].reshape(N_DEV, depth, B, 6 * D)
    dmod_all = dmod_all.transpose(1, 0, 2, 3).reshape(depth, n_seq, 6 * D)
    small_sum = _sum8("sum_grads_small", g2[:, rows_a:]).reshape(-1)
    g_small = dict(zip(small_names, _unpack(small_sum, [full_shape[n] for n in small_names])))
    for n in SMALL_SPLIT:
        w = p[n].shape[-1]
        g_small[n] = lax.dynamic_slice_in_dim(g_small[n], my_q * w, w, axis=g_small[n].ndim - 1)

    dm_cols = lax.dynamic_slice_in_dim(dmod_all, my_q * mod_cols, mod_cols, axis=2)
    dm_cols = jnp.pad(dm_cols, ((0, 0), (0, seq_pad - n_seq), (0, 0)))
    g_mod_w = jnp.stack([_mm(f"mod_dw{l}", c_act, dm_cols[l], ta=True) for l in range(depth)])

    keep, give = [], []
    for n, s, hr in zip(big_names, shard_shapes, half_rows):
        if s[0] % 2 == 0:
            def half(c, n=n, s=s):
                layers = range(c * s[0] // 2, (c + 1) * s[0] // 2)
                return jnp.stack([jnp.concatenate([jnp.split(grads[n][l], N_CHIPS, axis=BIG[n] - 1)[q] for l in layers], axis=0)
                                  for q in range(N_CHIPS)])
            mine, other = lax.cond(cc == 0, lambda: (half(0), half(1)), lambda: (half(1), half(0)))
            keep.append(mine)
            give.append(other)
            continue
        gfull = jnp.stack(grads[n])
        g4 = jnp.stack(jnp.split(gfull, N_CHIPS, axis=BIG[n])).reshape(N_CHIPS, 2 * hr, s[2])
        keep.append(lax.dynamic_slice_in_dim(g4, cc * hr, hr, axis=1))
        give.append(lax.dynamic_slice_in_dim(g4, (1 - cc) * hr, hr, axis=1))
    got = _sibling_exchange("rs_sibling", give)
    chip_sum = [_rowwise("rs_add_sibling", lambda r, bv, cv: ([r[0] + r[1]], [], []),
                         [a.reshape(1, -1, a.shape[2]), b.reshape(1, -1, a.shape[2])],
                         out_rows=[(a.shape[2], BF16)], tm=512)[0].reshape(a.shape) for a, b in zip(keep, got)]
    from_chips = _chip_all_to_all("rs_chips", chip_sum)
    half_sum = [_rowwise("rs_add_chips", lambda r, bv, cv: ([((r[0] + r[1]) + r[2]) + r[3]], [], []),
                         [t[q][None] for q in range(N_CHIPS)], out_rows=[(t.shape[2], F32)], tm=512)[0][0]
                for t in from_chips]
    other = _sibling_exchange("rs_share", half_sum)
    g_big = {n: jnp.concatenate([jnp.where(cc == 0, a, b), jnp.where(cc == 0, b, a)], axis=0).reshape(s)
             for n, s, a, b in zip(big_names, shard_shapes, half_sum, other)}

    g_out = {**g_small, **g_big, 'mod_w': g_mod_w}
    upd = {n: _adamw("adamw_" + n, p[n], g_out[n], p['m_' + n], p['v_' + n]) for n in WEIGHTS}
    return (loss, grad_x, *[g_out[n] for n in WEIGHTS], *[upd[n][0] for n in WEIGHTS],
            *[upd[n][1] for n in WEIGHTS], *[upd[n][2] for n in WEIGHTS])
```

```python
import math

import jax
import jax.numpy as jnp
import numpy as np
from jax import lax
from jax.experimental import pallas as pl
from jax.experimental.pallas import tpu as pltpu

F32, BF16 = jnp.float32, jnp.bfloat16

HEAD_DIM = 64
CONV_WIDTH = 31
LN_EPS = 1e-5
NEG_INF = -1e30
ADAM_LR, ADAM_B1, ADAM_B2, ADAM_EPS, ADAM_WD, ADAM_STEP = 0.001, 0.9, 0.999, 1e-08, 0.01, 10

LANES = 128
SUBLANES = 8
VMEM_LIMIT_BYTES = 56 * 1024 * 1024
N_CHIPS = 4
N_DEV = 8

WEIGHTS = ['mod_w', 'mod_b', 'ln1_g', 'ln1_b', 'ln2_g', 'ln2_b', 'ffn_w_in', 'ffn_w_out', 'gm_w_in', 'gm_b_in',
           'gm_ln_g', 'gm_ln_b', 'gm_w_s', 'gm_b_s', 'gm_w_out', 'fox_w_in', 'fox_b_f', 'fox_w_out', 'sb_w_in',
           'sb_w_out', 'cv_w_in', 'cv_b_in', 'cv_dw', 'cv_dw_b', 'cv_ln_g', 'cv_ln_b', 'cv_w_out', 'cv_b_out']
ARGS = ['x', 'c'] + WEIGHTS + ['loss_target'] + ['m_' + n for n in WEIGHTS] + ['v_' + n for n in WEIGHTS]
BIG = {'ffn_w_in': 2, 'ffn_w_out': 1, 'gm_w_in': 2, 'gm_w_out': 1, 'fox_w_in': 2, 'fox_w_out': 1,
       'sb_w_in': 2, 'sb_w_out': 1, 'cv_w_in': 2, 'cv_w_out': 1}
SMALL_SPLIT = ['cv_b_in', 'cv_dw', 'cv_dw_b', 'cv_ln_g', 'cv_ln_b', 'cv_b_out']
SMALL_REPL = ['mod_b', 'ln1_g', 'ln1_b', 'ln2_g', 'ln2_b', 'gm_b_in', 'gm_ln_g', 'gm_ln_b', 'gm_w_s', 'gm_b_s', 'fox_b_f']
PACK_COLS = 1024


_names_used = {}


def _unique(name):
    k = _names_used.get(name, 0)
    _names_used[name] = k + 1
    return name if k == 0 else f"{name}_{k}"


def _params(sem):
    return pltpu.CompilerParams(dimension_semantics=sem, vmem_limit_bytes=VMEM_LIMIT_BYTES)


def _pick(dim, pref, mult=LANES):
    if dim <= pref:
        return dim
    best = 0
    for t in range(mult, pref + 1, mult):
        if dim % t == 0:
            best = t
    assert best, (dim, pref)
    return best


def _mesh_pos():
    return lax.axis_index("x"), lax.axis_index("y"), lax.axis_index("c")


AG_COPIES = 7
A2A_COPIES = 3


def _comm_call(name, body, blks, out_shapes, n_sems):
    n = len(blks)
    hbm = pl.BlockSpec(memory_space=pl.ANY)
    return pl.pallas_call(
        body, name=_unique(name), out_shape=out_shapes, in_specs=[hbm] * n, out_specs=[hbm] * n,
        scratch_shapes=[pltpu.SemaphoreType.DMA((n_sems * n,)), pltpu.SemaphoreType.DMA((n_sems * n,)),
                        pltpu.SemaphoreType.DMA((n,))],
    )(*blks)


def _all_gather8(name, blks):
    n = len(blks)

    def body(*refs):
        x_refs, out_refs, (send_sems, recv_sems, local_sems) = refs[:n], refs[n:2 * n], refs[2 * n:]
        x, y, c = _mesh_pos()
        me, sibling = (x, y, c), (x, y, 1 - c)
        chips = [(1 - x, y), (x, 1 - y), (1 - x, 1 - y)]

        def copy(a, k, block, to, from_input=False):
            px, py, pc = block
            slot = out_refs[a].at[4 * px + 2 * py + pc]
            return pltpu.make_async_remote_copy(
                src_ref=x_refs[a] if from_input else slot, dst_ref=slot,
                send_sem=send_sems.at[AG_COPIES * a + k], recv_sem=recv_sems.at[AG_COPIES * a + k],
                device_id=to, device_id_type=pl.DeviceIdType.MESH)

        local, sent = [], []
        for a in range(n):
            local.append(pltpu.make_async_copy(x_refs[a], out_refs[a].at[4 * x + 2 * y + c], local_sems.at[a]))
            local[-1].start()
            first = [copy(a, 0, me, sibling, True)] + [copy(a, 1 + j, me, (*chip, c), True) for j, chip in enumerate(chips)]
            for cp in first:
                cp.start()
            sent += first
        for a in range(n):
            for j, chip in enumerate(chips):
                copy(a, 1 + j, (*chip, c), me).wait_recv()
                sent.append(copy(a, 4 + j, (*chip, c), sibling))
                sent[-1].start()
        for a in range(n):
            copy(a, 0, sibling, me).wait_recv()
            for j, chip in enumerate(chips):
                copy(a, 4 + j, (*chip, 1 - c), me).wait_recv()
        for cp in sent:
            cp.wait_send()
        for cp in local:
            cp.wait()

    return _comm_call(name, body, blks, [jax.ShapeDtypeStruct((N_DEV,) + b.shape, b.dtype) for b in blks], AG_COPIES)


def _sibling_exchange(name, blks):
    n = len(blks)

    def body(*refs):
        x_refs, out_refs, (send_sems, recv_sems, _) = refs[:n], refs[n:2 * n], refs[2 * n:]
        x, y, c = _mesh_pos()
        cps = [pltpu.make_async_remote_copy(src_ref=x_refs[a], dst_ref=out_refs[a], send_sem=send_sems.at[a],
                                            recv_sem=recv_sems.at[a], device_id=(x, y, 1 - c),
                                            device_id_type=pl.DeviceIdType.MESH) for a in range(n)]
        for cp in cps:
            cp.start()
        for cp in cps:
            cp.wait()

    return _comm_call(name, body, blks, [jax.ShapeDtypeStruct(b.shape, b.dtype) for b in blks], 1)


def _chip_all_to_all(name, blks):
    n = len(blks)

    def body(*refs):
        x_refs, out_refs, (send_sems, recv_sems, local_sems) = refs[:n], refs[n:2 * n], refs[2 * n:]
        x, y, c = _mesh_pos()
        chips = [(1 - x, y), (x, 1 - y), (1 - x, 1 - y)]
        my_q = 2 * x + y

        def copy(a, j, src_q, dst_q):
            px, py = chips[j]
            return pltpu.make_async_remote_copy(
                src_ref=x_refs[a].at[src_q], dst_ref=out_refs[a].at[dst_q],
                send_sem=send_sems.at[A2A_COPIES * a + j], recv_sem=recv_sems.at[A2A_COPIES * a + j],
                device_id=(px, py, c), device_id_type=pl.DeviceIdType.MESH)

        local, sent = [], []
        for a in range(n):
            local.append(pltpu.make_async_copy(x_refs[a].at[my_q], out_refs[a].at[my_q], local_sems.at[a]))
            local[-1].start()
            sent += [copy(a, j, 2 * px + py, my_q) for j, (px, py) in enumerate(chips)]
            for cp in sent[-A2A_COPIES:]:
                cp.start()
        for a in range(n):
            for j, (px, py) in enumerate(chips):
                copy(a, j, my_q, 2 * px + py).wait_recv()
        for cp in sent:
            cp.wait_send()
        for cp in local:
            cp.wait()

    return _comm_call(name, body, blks, [jax.ShapeDtypeStruct(b.shape, b.dtype) for b in blks], A2A_COPIES)


def _rowwise(name, fn, rows, bvecs=(), cvecs=(), out_rows=(), out_bsums=(), out_tsums=(), tm=256):
    B, S = rows[0].shape[:2]
    tm = _pick(S, tm, SUBLANES)
    n_r, n_b, n_c = len(rows), len(bvecs), len(cvecs)
    n_or, n_ob = len(out_rows), len(out_bsums)

    def body(*refs):
        ins, outs = refs[:n_r + n_b + n_c], refs[n_r + n_b + n_c:]
        r = [ref[0] for ref in ins[:n_r]]
        bv = [ref[0] for ref in ins[n_r:n_r + n_b]]
        cv = [ref[...] for ref in ins[n_r + n_b:]]
        o_rows, o_bsums, o_tsums = fn(r, bv, cv)
        b, i = pl.program_id(0), pl.program_id(1)
        for ref, val in zip(outs[:n_or], o_rows):
            ref[0] = val.astype(ref.dtype)
        for ref, val in zip(outs[n_or:n_or + n_ob], o_bsums):
            @pl.when(i == 0)
            def _(ref=ref, val=val):
                ref[0] = val

            @pl.when(i > 0)
            def _(ref=ref, val=val):
                ref[0] += val
        for ref, val in zip(outs[n_or + n_ob:], o_tsums):
            first = jnp.logical_and(b == 0, i == 0)

            @pl.when(first)
            def _(ref=ref, val=val):
                ref[...] = val

            @pl.when(jnp.logical_not(first))
            def _(ref=ref, val=val):
                ref[...] += val

    in_specs = [pl.BlockSpec((1, tm, a.shape[2]), lambda b, i: (b, i, 0)) for a in rows]
    in_specs += [pl.BlockSpec((1, 1, a.shape[2]), lambda b, i: (b, 0, 0)) for a in bvecs]
    in_specs += [pl.BlockSpec((1, a.shape[1]), lambda b, i: (0, 0)) for a in cvecs]
    out_shape = [jax.ShapeDtypeStruct((B, S, cdim), dt) for cdim, dt in out_rows]
    out_specs = [pl.BlockSpec((1, tm, cdim), lambda b, i: (b, i, 0)) for cdim, _ in out_rows]
    out_shape += [jax.ShapeDtypeStruct((B, 1, cdim), F32) for cdim in out_bsums]
    out_specs += [pl.BlockSpec((1, 1, cdim), lambda b, i: (b, 0, 0)) for cdim in out_bsums]
    out_shape += [jax.ShapeDtypeStruct((1, cdim), F32) for cdim in out_tsums]
    out_specs += [pl.BlockSpec((1, cdim), lambda b, i: (0, 0)) for cdim in out_tsums]
    sem = ("arbitrary", "arbitrary") if out_tsums else ("parallel", "arbitrary")
    res = pl.pallas_call(body, name=_unique(name), grid=(B, S // tm), in_specs=in_specs, out_specs=out_specs,
                         out_shape=out_shape, compiler_params=_params(sem))(*rows, *bvecs, *cvecs)
    return list(res)


MM_TILE = 1536
MM_ROWS = 512
MM_WEIGHT_TILE_BYTES = 12 * 1024 * 1024


def _mm(name, a, b, ta=False, tb=False, out_dtype=F32):
    M, K = (a.shape[1], a.shape[0]) if ta else a.shape
    N = b.shape[0] if tb else b.shape[1]
    assert (b.shape[1] if tb else b.shape[0]) == K, (a.shape, b.shape, ta, tb)
    tn = _pick(N, MM_TILE)
    if ta:
        tm, tk = _pick(M, MM_TILE), _pick(K, 2 * MM_ROWS, LANES if tb else SUBLANES)
    else:
        tm = _pick(M, MM_ROWS, SUBLANES)
        tk = K if K * tn * 2 <= MM_WEIGHT_TILE_BYTES else _pick(K, MM_TILE)
    nk = K // tk
    dims = (((0 if ta else 1,), (1 if tb else 0,)), ((), ()))

    def body(a_ref, b_ref, o_ref, acc_ref):
        k = pl.program_id(2)
        p = lax.dot_general(a_ref[...].astype(BF16), b_ref[...].astype(BF16), dims, preferred_element_type=F32)
        if nk == 1:
            o_ref[...] = p.astype(o_ref.dtype)
        else:
            @pl.when(k == 0)
            def _():
                acc_ref[...] = p

            @pl.when(k > 0)
            def _():
                acc_ref[...] += p

            @pl.when(k == nk - 1)
            def _():
                o_ref[...] = acc_ref[...].astype(o_ref.dtype)

    a_spec = pl.BlockSpec((tk, tm), lambda j, i, k: (k, i)) if ta else pl.BlockSpec((tm, tk), lambda j, i, k: (i, k))
    b_spec = pl.BlockSpec((tn, tk), lambda j, i, k: (j, k)) if tb else pl.BlockSpec((tk, tn), lambda j, i, k: (k, j))
    return pl.pallas_call(
        body, name=_unique(name), grid=(N // tn, M // tm, nk), in_specs=[a_spec, b_spec],
        out_specs=pl.BlockSpec((tm, tn), lambda j, i, k: (i, j)),
        out_shape=jax.ShapeDtypeStruct((M, N), out_dtype),
        scratch_shapes=[pltpu.VMEM((tm, tn) if nk > 1 else (SUBLANES, LANES), F32)],
        compiler_params=_params(("parallel", "parallel", "arbitrary")))(a, b)


def _silu(x):
    return x * _sigmoid(x)


def _sigmoid(x):
    return 1.0 / (1.0 + jnp.exp(-x))


def _dsilu(x):
    s = _sigmoid(x)
    return s * (1.0 + x * (1.0 - s))


def _gelu(x):
    return 0.5 * x * (1.0 + lax.erf(x * np.float32(math.sqrt(0.5))))


def _dgelu(x):
    cdf = 0.5 * (1.0 + lax.erf(x * np.float32(math.sqrt(0.5))))
    pdf = jnp.exp(-0.5 * x * x) * np.float32(1.0 / math.sqrt(2.0 * math.pi))
    return cdf + x * pdf


def _ln_stats(r):
    mu = jnp.mean(r, axis=-1, keepdims=True)
    xc = r - mu
    var = jnp.mean(xc * xc, axis=-1, keepdims=True)
    rstd = lax.rsqrt(var + LN_EPS)
    return xc * rstd, rstd


def _ln_bwd(dxhat, xhat, rstd):
    m1 = jnp.mean(dxhat, axis=-1, keepdims=True)
    m2 = jnp.mean(dxhat * xhat, axis=-1, keepdims=True)
    return rstd * (dxhat - m1 - xhat * m2)


def _csum(v):
    return jnp.sum(v, axis=0, keepdims=True)


def _split_dot(x, m01, lhs01=False, terms=2):
    acc, rem = None, x
    for _ in range(terms):
        part = rem.astype(BF16)
        rem = rem - part.astype(F32)
        d = jnp.dot(m01, part, preferred_element_type=F32) if lhs01 else jnp.dot(part, m01, preferred_element_type=F32)
        acc = d if acc is None else acc + d
    return acc


def _iota2(shape, dim):
    return lax.broadcasted_iota(jnp.int32, shape, dim)


def _modulate(name, x, sc, sh):
    D = x.shape[2]
    return _rowwise(name, lambda r, bv, cv: ([r[0] * (1.0 + bv[0]) + bv[1]], [], []),
                    [x], [sc, sh], [], out_rows=[(D, BF16)])[0]


def _resid_ln(name, alpha, x, y, g, ln_g, ln_b, ybias=None, then=None):
    D = x.shape[2]

    def fn(r, bv, cv):
        yy = r[1] if ybias is None else r[1] + cv[2]
        xhat, _ = _ln_stats(alpha * r[0] + (1.0 + bv[0]) * yy)
        xn = xhat * cv[0] + cv[1]
        return [xn] + ([xn * (1.0 + bv[1]) + bv[2]] if then else []), [], []
    cvecs = [ln_g, ln_b] + ([] if ybias is None else [ybias])
    return _rowwise(name, fn, [x, y], [g] + list(then or ()), cvecs, out_rows=[(D, F32)] + ([(D, BF16)] if then else []))


def _resid_ln_bwd(name, alpha, dxn, x, y, g, ln_g, ln_b, ybias=None, then=None):
    D = x.shape[2]

    def fn(r, bv, cv):
        yy = r[2] if ybias is None else r[2] + cv[2]
        xhat, rstd = _ln_stats(alpha * r[1] + (1.0 + bv[0]) * yy)
        d, sums = r[0], []
        if then:
            d = alpha * d + r[3] * (1.0 + bv[1])
            sums = [_csum(r[3] * (xhat * cv[0] + cv[1])), _csum(r[3])]
        dr = _ln_bwd(d * cv[0], xhat, rstd)
        dy = (1.0 + bv[0]) * dr
        return [dr, dy], [_csum(dr * yy)] + sums, [_csum(d * xhat), _csum(d), _csum(dy)]
    cvecs = [ln_g, ln_b] + ([] if ybias is None else [ybias])
    rows, bvecs = [dxn, x, y] + ([then[0]] if then else []), [g] + ([then[1]] if then else [])
    return _rowwise(name, fn, rows, bvecs, cvecs, out_rows=[(D, F32), (D, BF16)], out_bsums=[D] * (3 if then else 1),
                    out_tsums=[D, D, D])


def _modulate_bwd(name, alpha, dh, dr, x, sc):
    D = x.shape[2]

    def fn(r, bv, cv):
        return [alpha * r[1] + r[0] * (1.0 + bv[0])], [_csum(r[0] * r[2]), _csum(r[0])], []
    return _rowwise(name, fn, [dh, dr, x], [sc], [], out_rows=[(D, F32)], out_bsums=[D, D])


def _loss_head(name, y, target):
    D = y.shape[2]

    def fn(r, bv, cv):
        e = r[0] - r[1]
        return [e * np.float32(1.0 / D)], [_csum(e * e)], []
    return _rowwise(name, fn, [y, target], [], [], out_rows=[(D, F32)], out_bsums=[D])


def _ffn_in_act(name, h, w_in):
    M, K = h.shape
    Hd = w_in.shape[1] // 2
    tm, tn = _pick(M, MM_ROWS, SUBLANES), _pick(Hd, MM_TILE)
    nj = Hd // tn

    def body(h_ref, wg_ref, wu_ref, zg_ref, zu_ref, act_ref):
        hv = h_ref[...].astype(BF16)
        zg = jnp.dot(hv, wg_ref[...].astype(BF16), preferred_element_type=F32)
        zu = jnp.dot(hv, wu_ref[...].astype(BF16), preferred_element_type=F32)
        zg_ref[...] = zg
        zu_ref[...] = zu
        act_ref[...] = (_silu(zg) * zu).astype(act_ref.dtype)

    tile = pl.BlockSpec((tm, tn), lambda j, i: (i, j))
    return pl.pallas_call(
        body, name=_unique(name), grid=(nj, M // tm),
        in_specs=[pl.BlockSpec((tm, K), lambda j, i: (i, 0)), pl.BlockSpec((K, tn), lambda j, i: (0, j)),
                  pl.BlockSpec((K, tn), lambda j, i: (0, j + nj))],
        out_specs=[tile, tile, tile],
        out_shape=[jax.ShapeDtypeStruct((M, Hd), F32), jax.ShapeDtypeStruct((M, Hd), F32), jax.ShapeDtypeStruct((M, Hd), BF16)],
        compiler_params=_params(("parallel", "parallel")))(h, w_in, w_in)


def _ffn_out_bwd_act(name, dy, w_out, zg, zu):
    M, K = dy.shape
    Hd = w_out.shape[0]
    tm, tn = _pick(M, MM_ROWS, SUBLANES), _pick(Hd, MM_TILE)

    def body(dy_ref, w_ref, zg_ref, zu_ref, dzg_ref, dzu_ref):
        da = lax.dot_general(dy_ref[...].astype(BF16), w_ref[...].astype(BF16), _NT, preferred_element_type=F32)
        gg = zg_ref[...]
        dzg_ref[...] = (da * zu_ref[...] * _dsilu(gg)).astype(dzg_ref.dtype)
        dzu_ref[...] = (da * _silu(gg)).astype(dzu_ref.dtype)

    tile = pl.BlockSpec((tm, tn), lambda j, i: (i, j))
    return pl.pallas_call(
        body, name=_unique(name), grid=(Hd // tn, M // tm),
        in_specs=[pl.BlockSpec((tm, K), lambda j, i: (i, 0)), pl.BlockSpec((tn, K), lambda j, i: (j, 0)), tile, tile],
        out_specs=[tile, tile],
        out_shape=[jax.ShapeDtypeStruct((M, Hd), BF16), jax.ShapeDtypeStruct((M, Hd), BF16)],
        compiler_params=_params(("parallel", "parallel")))(dy, w_out, zg, zu)


def _ffn_in_bwd(name, dzg, dzu, w_in):
    M, Hd = dzg.shape
    N = w_in.shape[0]
    tm, tn = _pick(M, MM_ROWS, SUBLANES), _pick(N, MM_TILE)

    def body(g_ref, u_ref, wg_ref, wu_ref, o_ref):
        o_ref[...] = (lax.dot_general(g_ref[...], wg_ref[...].astype(BF16), _NT, preferred_element_type=F32)
                      + lax.dot_general(u_ref[...], wu_ref[...].astype(BF16), _NT, preferred_element_type=F32))

    rows = pl.BlockSpec((tm, Hd), lambda j, i: (i, 0))
    return pl.pallas_call(
        body, name=_unique(name), grid=(N // tn, M // tm),
        in_specs=[rows, rows, pl.BlockSpec((tn, Hd), lambda j, i: (j, 0)), pl.BlockSpec((tn, Hd), lambda j, i: (j, 1))],
        out_specs=pl.BlockSpec((tm, tn), lambda j, i: (i, j)),
        out_shape=jax.ShapeDtypeStruct((M, N), F32),
        compiler_params=_params(("parallel", "parallel")))(dzg, dzu, w_in, w_in)


def _gm_act(name, zin, b_in, ln_g, ln_b):
    W = zin.shape[2] // 2

    def fn(r, bv, cv):
        z = _gelu(r[0] + cv[0])
        vhat, _ = _ln_stats(z[:, W:])
        return [z[:, :W], vhat * cv[1] + cv[2]], [], []
    return _rowwise(name, fn, [zin], [], [b_in, ln_g, ln_b], out_rows=[(W, F32), (W, BF16)])


def _gm_act_bwd(name, zin, du, dvn, b_in, ln_g):
    W = zin.shape[2] // 2

    def fn(r, bv, cv):
        zz = r[0] + cv[0]
        z = _gelu(zz)
        vhat, rstd = _ln_stats(z[:, W:])
        dv = _ln_bwd(r[2] * cv[1], vhat, rstd)
        dzin = jnp.concatenate([r[1], dv], axis=1) * _dgelu(zz)
        return [dzin], [], [_csum(dzin), _csum(r[2] * vhat), _csum(r[2])]
    return _rowwise(name, fn, [zin, du, dvn], [], [b_in, ln_g], out_rows=[(2 * W, BF16)], out_tsums=[2 * W, W, W])


def _gm_causal_w(ws_ref, g):
    T = ws_ref.shape[1]
    return jnp.where(_iota2((T, T), 1) <= _iota2((T, T), 0), ws_ref[g], 0.0).astype(BF16)


def _gm_spatial(name, u, vn, w_s, b_sT):
    B, S, W = u.shape
    G, T = w_s.shape[0], w_s.shape[1]
    assert W == G * T, "a head group is as wide as a chunk is long"

    def body(u_ref, vn_ref, ws_ref, bs_ref, y_ref):
        for g in range(G):
            cs = slice(g * T, (g + 1) * T)
            sv = jnp.dot(_gm_causal_w(ws_ref, g), vn_ref[0, :, cs], preferred_element_type=F32) + bs_ref[:, g:g + 1]
            y_ref[0, :, cs] = (u_ref[0, :, cs] * sv).astype(y_ref.dtype)

    row = pl.BlockSpec((1, T, W), lambda b, i: (b, i, 0))
    return pl.pallas_call(
        body, name=_unique(name), grid=(B, S // T),
        in_specs=[row, row, pl.BlockSpec((G, T, T), lambda b, i: (0, 0, 0)), pl.BlockSpec((T, G), lambda b, i: (0, 0))],
        out_specs=row, out_shape=jax.ShapeDtypeStruct((B, S, W), BF16),
        compiler_params=_params(("parallel", "parallel")))(u, vn, w_s, b_sT)


def _gm_spatial_bwd(name, dyg, u, vn, w_s, b_sT):
    B, S, W = u.shape
    G, T = w_s.shape[0], w_s.shape[1]
    assert W == G * T, "a head group is as wide as a chunk is long"

    def body(dy_ref, u_ref, vn_ref, ws_ref, bs_ref, du_ref, dvn_ref, dws_ref, dbs_ref):
        first = jnp.logical_and(pl.program_id(0) == 0, pl.program_id(1) == 0)

        @pl.when(first)
        def _():
            dws_ref[...] = jnp.zeros_like(dws_ref)
            dbs_ref[...] = jnp.zeros_like(dbs_ref)

        tril = _iota2((T, T), 1) <= _iota2((T, T), 0)
        for g in range(G):
            cs = slice(g * T, (g + 1) * T)
            wm = _gm_causal_w(ws_ref, g)
            vng = vn_ref[0, :, cs]
            sv = jnp.dot(wm, vng, preferred_element_type=F32) + bs_ref[:, g:g + 1]
            dy = dy_ref[0, :, cs]
            du_ref[0, :, cs] = dy * sv
            dsv = dy * u_ref[0, :, cs]
            dsv16 = dsv.astype(BF16)
            dvn_ref[0, :, cs] = lax.dot_general(wm, dsv16, (((0,), (0,)), ((), ())), preferred_element_type=F32)
            dw = lax.dot_general(dsv16, vng, (((1,), (1,)), ((), ())), preferred_element_type=F32)
            dws_ref[g] += jnp.where(tril, dw, 0.0)
            dbs_ref[:, g:g + 1] += jnp.sum(dsv, axis=1, keepdims=True)

    row = pl.BlockSpec((1, T, W), lambda b, i: (b, i, 0))
    return pl.pallas_call(
        body, name=_unique(name), grid=(B, S // T),
        in_specs=[row, row, row, pl.BlockSpec((G, T, T), lambda b, i: (0, 0, 0)), pl.BlockSpec((T, G), lambda b, i: (0, 0))],
        out_specs=[row, row, pl.BlockSpec((G, T, T), lambda b, i: (0, 0, 0)), pl.BlockSpec((T, G), lambda b, i: (0, 0))],
        out_shape=[jax.ShapeDtypeStruct((B, S, W), F32), jax.ShapeDtypeStruct((B, S, W), F32),
                   jax.ShapeDtypeStruct((G, T, T), F32), jax.ShapeDtypeStruct((T, G), F32)],
        compiler_params=_params(("arbitrary", "arbitrary")))(dyg, u, vn, w_s, b_sT)


def _cv_glu(name, pw, b_in):
    W = pw.shape[2] // 2

    def fn(r, bv, cv):
        z = r[0] + cv[0]
        return [z[:, :W] * _sigmoid(z[:, W:])], [], []
    return _rowwise(name, fn, [pw], [], [b_in], out_rows=[(W, F32)])[0]


def _cv_glu_bwd(name, pw, dyg, b_in):
    W = pw.shape[2] // 2

    def fn(r, bv, cv):
        z = r[0] + cv[0]
        a, s = z[:, :W], _sigmoid(z[:, W:])
        dpw = jnp.concatenate([r[1] * s, r[1] * a * s * (1.0 - s)], axis=1)
        return [dpw], [], [_csum(dpw)]
    return _rowwise(name, fn, [pw, dyg], [], [b_in], out_rows=[(2 * W, BF16)], out_tsums=[2 * W])


def _cv_ln_act(name, yc, ln_g, ln_b):
    D = yc.shape[2]

    def fn(r, bv, cv):
        xhat, _ = _ln_stats(r[0])
        return [_silu(xhat * cv[0] + cv[1])], [], []
    return _rowwise(name, fn, [yc], [], [ln_g, ln_b], out_rows=[(D, BF16)])[0]


def _cv_ln_act_bwd(name, yc, dys, ln_g, ln_b):
    D = yc.shape[2]

    def fn(r, bv, cv):
        xhat, rstd = _ln_stats(r[0])
        dyn = r[1] * _dsilu(xhat * cv[0] + cv[1])
        dyc = _ln_bwd(dyn * cv[0], xhat, rstd)
        return [dyc], [], [_csum(dyn * xhat), _csum(dyn), _csum(dyc)]
    return _rowwise(name, fn, [yc, dys], [], [ln_g, ln_b], out_rows=[(D, F32)], out_tsums=[D, D, D])


CONV_HALO = 32
CONV_TS, CONV_TC = 256, 128


def _dwconv(name, y, dw, dw_b):
    B, S, D = y.shape
    ts, tc, halo, K = _pick(S, CONV_TS, SUBLANES), _pick(D, CONV_TC), CONV_HALO, CONV_WIDTH

    def body(cur_ref, prev_ref, dw_ref, b_ref, o_ref, buf):
        i = pl.program_id(1)
        buf[pl.ds(0, halo), :] = jnp.where(i > 0, prev_ref[0, pl.ds(ts - halo, halo), :], 0.0)
        buf[pl.ds(halo, ts), :] = cur_ref[0]
        acc = jnp.zeros((ts, tc), F32) + b_ref[...]
        for k in range(K):
            acc = acc + dw_ref[k:k + 1, :] * buf[pl.ds(halo - (K - 1) + k, ts), :]
        o_ref[0] = acc

    return pl.pallas_call(
        body, name=_unique(name), grid=(B, S // ts, D // tc),
        in_specs=[pl.BlockSpec((1, ts, tc), lambda b, i, j: (b, i, j)),
                  pl.BlockSpec((1, ts, tc), lambda b, i, j: (b, jnp.maximum(i - 1, 0), j)),
                  pl.BlockSpec((halo, tc), lambda b, i, j: (0, j)), pl.BlockSpec((1, tc), lambda b, i, j: (0, j))],
        out_specs=pl.BlockSpec((1, ts, tc), lambda b, i, j: (b, i, j)),
        out_shape=jax.ShapeDtypeStruct((B, S, D), F32),
        scratch_shapes=[pltpu.VMEM((halo + ts, tc), F32)],
        compiler_params=_params(("parallel", "parallel", "parallel")))(y, y, dw, dw_b)


def _dwconv_bwd(name, dyc, y, dw):
    B, S, D = y.shape
    ts, tc, halo, K = _pick(S, CONV_TS, SUBLANES), _pick(D, CONV_TC), CONV_HALO, CONV_WIDTH
    nt = S // ts

    def body(g_ref, gnext_ref, y_ref, yprev_ref, dw_ref, dy_ref, ddw_ref, gbuf, ybuf):
        b, i = pl.program_id(1), pl.program_id(2)
        first = jnp.logical_and(b == 0, i == 0)

        @pl.when(first)
        def _():
            ddw_ref[...] = jnp.zeros_like(ddw_ref)

        g = g_ref[0]
        gbuf[pl.ds(0, ts), :] = g
        gbuf[pl.ds(ts, halo), :] = jnp.where(i < nt - 1, gnext_ref[0, pl.ds(0, halo), :], 0.0)
        ybuf[pl.ds(0, halo), :] = jnp.where(i > 0, yprev_ref[0, pl.ds(ts - halo, halo), :], 0.0)
        ybuf[pl.ds(halo, ts), :] = y_ref[0]
        acc = jnp.zeros((ts, tc), F32)
        for k in range(K):
            acc = acc + dw_ref[k:k + 1, :] * gbuf[pl.ds(K - 1 - k, ts), :]
            ddw_ref[k:k + 1, :] += _csum(g * ybuf[pl.ds(halo - (K - 1) + k, ts), :])
        dy_ref[0] = acc

    tile = lambda f: pl.BlockSpec((1, ts, tc), f)
    return pl.pallas_call(
        body, name=_unique(name), grid=(D // tc, B, nt),
        in_specs=[tile(lambda j, b, i: (b, i, j)), tile(lambda j, b, i: (b, jnp.minimum(i + 1, nt - 1), j)),
                  tile(lambda j, b, i: (b, i, j)), tile(lambda j, b, i: (b, jnp.maximum(i - 1, 0), j)),
                  pl.BlockSpec((halo, tc), lambda j, b, i: (0, j))],
        out_specs=[tile(lambda j, b, i: (b, i, j)), pl.BlockSpec((halo, tc), lambda j, b, i: (0, j))],
        out_shape=[jax.ShapeDtypeStruct((B, S, D), F32), jax.ShapeDtypeStruct((halo, D), F32)],
        scratch_shapes=[pltpu.VMEM((ts + halo, tc), F32), pltpu.VMEM((halo + ts, tc), F32)],
        compiler_params=_params(("parallel", "arbitrary", "arbitrary")))(dyc, dyc, y, y, dw)


ATT_BLOCK = 128
FOX_TILE = (512, 1024)
SB_TILE = (256, 512)
ATT_PIECE_ROWS = 32
FOX_GATE_COLS = 128


def _att_tiles(S, tile):
    return _pick(S, tile[0], SUBLANES), _pick(S, tile[1], LANES)


def _pieces(T, TK):
    R = min(T, ATT_PIECE_ROWS)
    segs = [slice(c, c + LANES) for c in range(0, TK, LANES)]
    return [(slice(r, r + R), segs) for r in range(0, T, R)]


def _piece_keep(row0, col0, rs, cs, strict, lane_major_of=0):
    shape = (rs.stop - rs.start, cs.stop - cs.start)
    lane = _iota2(shape, 1)
    key = col0 + (lane * lane_major_of + cs.start // LANES if lane_major_of else cs.start + lane)
    qry = row0 + rs.start + _iota2(shape, 0)
    return key < qry if strict else key <= qry


def _causal_tiles(i, tq, tk):
    return (i * tq + tq + tk - 1) // tk


def _lane_major(t, tk):
    B, H, S, dh = t.shape
    return t.reshape(B, H, S // tk, LANES, tk // LANES, dh).swapaxes(3, 4).reshape(B, H, S, dh)


def _lane_major_inverse(t, tk):
    B, H, S, dh = t.shape
    return t.reshape(B, H, S // tk, tk // LANES, LANES, dh).swapaxes(3, 4).reshape(B, H, S, dh)


def _log_sigmoid(x):
    return jnp.minimum(x, 0.0) - jnp.log(1.0 + jnp.exp(-jnp.abs(x)))


def _fox_gate_cumsum(name, fl, b_f):
    B, S, C = fl.shape
    T = _pick(S, ATT_BLOCK, SUBLANES)

    def body(fl_ref, bf_ref, f_ref, carry):
        @pl.when(pl.program_id(1) == 0)
        def _():
            carry[...] = jnp.zeros_like(carry)
        lf = _log_sigmoid(fl_ref[0] + bf_ref[...])
        lower = (_iota2((T, T), 1) <= _iota2((T, T), 0)).astype(BF16)
        f = _split_dot(lf, lower, lhs01=True, terms=3) + carry[...]
        f_ref[0] = f
        carry[...] = f[T - 1:T, :]

    return pl.pallas_call(
        body, name=_unique(name), grid=(B, S // T),
        in_specs=[pl.BlockSpec((1, T, C), lambda b, i: (b, i, 0)), pl.BlockSpec((1, C), lambda b, i: (0, 0))],
        out_specs=pl.BlockSpec((1, T, C), lambda b, i: (b, i, 0)),
        out_shape=jax.ShapeDtypeStruct((B, S, C), F32),
        scratch_shapes=[pltpu.VMEM((1, C), F32)],
        compiler_params=_params(("arbitrary", "arbitrary")))(fl, b_f)


def _fox_gate_bwd(name, dF, fl, b_f, n_heads):
    B, S, C = fl.shape
    T = _pick(S, ATT_BLOCK, SUBLANES)
    nt = S // T

    def body(df_ref, fl_ref, bf_ref, dfl_ref, dbf_ref, carry):
        first = jnp.logical_and(pl.program_id(0) == 0, pl.program_id(1) == 0)

        @pl.when(pl.program_id(1) == 0)
        def _():
            carry[...] = jnp.zeros_like(carry)

        @pl.when(first)
        def _():
            dbf_ref[...] = jnp.zeros_like(dbf_ref)

        upper = (_iota2((T, T), 1) >= _iota2((T, T), 0)).astype(BF16)
        dlf = _split_dot(df_ref[0], upper, lhs01=True, terms=3) + carry[...]
        carry[...] = dlf[0:1, :]
        x = fl_ref[0] + bf_ref[...]
        dfl = jnp.where(_iota2((T, C), 1) < n_heads, dlf * _sigmoid(-x), 0.0)
        dfl_ref[0] = dfl
        dbf_ref[...] += _csum(dfl)

    rev = lambda b, i: (b, nt - 1 - i, 0)
    return pl.pallas_call(
        body, name=_unique(name), grid=(B, nt),
        in_specs=[pl.BlockSpec((1, T, C), rev), pl.BlockSpec((1, T, C), rev), pl.BlockSpec((1, C), lambda b, i: (0, 0))],
        out_specs=[pl.BlockSpec((1, T, C), rev), pl.BlockSpec((1, C), lambda b, i: (0, 0))],
        out_shape=[jax.ShapeDtypeStruct((B, S, C), F32), jax.ShapeDtypeStruct((1, C), F32)],
        scratch_shapes=[pltpu.VMEM((1, C), F32)],
        compiler_params=_params(("arbitrary", "arbitrary")))(dF, fl, b_f)


_NT = (((1,), (1,)), ((), ()))
_TN = (((0,), (0,)), ((), ()))


def _fox_fwd(name, q, k, v, fq, fk):
    B, H, S, dh = q.shape
    T, TK = _att_tiles(S, FOX_TILE)
    scale = np.float32(dh ** -0.5)

    pieces = _pieces(T, TK)

    def body(q_ref, k_ref, v_ref, fq_ref, fk_ref, o_ref, lse_ref, s_scr, p_scr):
        i = pl.program_id(2)
        qb = q_ref[0, 0]
        n_tiles = _causal_tiles(i, T, TK)

        def tile(j, carry, diag):
            m, l, acc = carry
            ks = pl.ds(pl.multiple_of(j * TK, TK), TK)
            s_scr[...] = lax.dot_general(qb, k_ref[0, 0, ks, :], _NT, preferred_element_type=F32) * scale
            fkj = fk_ref[0, 0, pl.ds(j, 1), :]
            m_new = []
            for rc, (rs, segs) in enumerate(pieces):
                fq_c, mx = fq_ref[0, 0, rs, :], None
                for cs in segs:
                    s = s_scr[rs, cs] + fq_c - fkj[:, cs]
                    if diag:
                        s = jnp.where(_piece_keep(i * T, j * TK, rs, cs, False), s, NEG_INF)
                    s_scr[rs, cs] = s
                    mx = s if mx is None else jnp.maximum(mx, s)
                m_new.append(jnp.maximum(m[rc], jnp.max(mx, axis=1, keepdims=True)))
            alpha, l_new = [], []
            for rc, (rs, segs) in enumerate(pieces):
                alpha.append(jnp.exp(m[rc] - m_new[rc]))
                psum = None
                for cs in segs:
                    p = jnp.exp(s_scr[rs, cs] - m_new[rc])
                    p_scr[rs, cs] = p.astype(BF16)
                    psum = p if psum is None else psum + p
                l_new.append(alpha[rc] * l[rc] + jnp.sum(psum, axis=1, keepdims=True))
            acc = jnp.concatenate(alpha, axis=0) * acc + jnp.dot(p_scr[...], v_ref[0, 0, ks, :], preferred_element_type=F32)
            return tuple(m_new), tuple(l_new), acc

        init = (tuple(jnp.full((rs.stop - rs.start, 1), NEG_INF, F32) for rs, _ in pieces),
                tuple(jnp.zeros((rs.stop - rs.start, 1), F32) for rs, _ in pieces), jnp.zeros((T, dh), F32))
        carry = lax.fori_loop(0, n_tiles - 1, lambda j, c: tile(j, c, False), init)
        m, l, acc = tile(n_tiles - 1, carry, True)
        m, l = jnp.concatenate(m, axis=0), jnp.concatenate(l, axis=0)
        o_ref[0, 0] = (acc / l).astype(o_ref.dtype)
        lse_ref[0, 0] = m + jnp.log(l)

    full = lambda w: pl.BlockSpec((1, 1, S, w), lambda b, h, i: (b, h, 0, 0))
    blk = lambda w: pl.BlockSpec((1, 1, T, w), lambda b, h, i: (b, h, i, 0))
    return pl.pallas_call(
        body, name=_unique(name), grid=(B, H, S // T),
        in_specs=[blk(dh), full(dh), full(dh), blk(1), pl.BlockSpec((1, 1, S // TK, TK), lambda b, h, i: (b, h, 0, 0))],
        out_specs=[blk(dh), blk(1)],
        out_shape=[jax.ShapeDtypeStruct((B, H, S, dh), BF16), jax.ShapeDtypeStruct((B, H, S, 1), F32)],
        scratch_shapes=[pltpu.VMEM((T, TK), F32), pltpu.VMEM((T, TK), BF16)],
        compiler_params=_params(("parallel", "parallel", "parallel")))(q, k, v, fq, fk)


def _fox_bwd(name, q, k, v, fq, fk, do, lse):
    B, H, S, dh = q.shape
    T, TK = _att_tiles(S, FOX_TILE)
    nt, nkt = S // T, S // TK
    scale = np.float32(dh ** -0.5)

    pieces = _pieces(T, TK)

    def body(q_ref, k_ref, v_ref, fq_ref, fk_ref, do_ref, lse_ref, dq_ref, dk_ref, dv_ref, dfk_ref,
             p_buf, dp_buf, s_scr, ds_scr, p16_scr, dk_acc, dv_acc):
        dk_acc[...] = jnp.zeros_like(dk_acc)
        dv_acc[...] = jnp.zeros_like(dv_acc)
        dfk_ref[...] = jnp.zeros_like(dfk_ref)

        def qloop(i, _):
            qs = pl.ds(pl.multiple_of(i * T, T), T)
            qb, dob16 = q_ref[0, 0, qs, :], do_ref[0, 0, qs, :].astype(BF16)
            n_tiles = _causal_tiles(i, T, TK)
            row_at = lambda rs: pl.ds(pl.multiple_of(i * T + rs.start, SUBLANES), rs.stop - rs.start)
            fq_c = [fq_ref[0, 0, row_at(rs), :] for rs, _ in pieces]
            lse_c = [lse_ref[0, 0, row_at(rs), :] for rs, _ in pieces]

            def sweep1(j, delta, diag):
                ks = pl.ds(pl.multiple_of(j * TK, TK), TK)
                s_scr[...] = lax.dot_general(qb, k_ref[0, 0, ks, :], _NT, preferred_element_type=F32) * scale
                dp_buf[j] = lax.dot_general(dob16, v_ref[0, 0, ks, :], _NT, preferred_element_type=F32)
                fkj = fk_ref[0, 0, pl.ds(j, 1), :]
                out = []
                for rc, (rs, segs) in enumerate(pieces):
                    pdp = None
                    for cs in segs:
                        p = jnp.exp(((s_scr[rs, cs] + fq_c[rc]) - fkj[:, cs]) - lse_c[rc])
                        if diag:
                            p = jnp.where(_piece_keep(i * T, j * TK, rs, cs, False), p, 0.0)
                        p_buf[j, rs, cs] = p
                        pdp = p * dp_buf[j, rs, cs] if pdp is None else pdp + p * dp_buf[j, rs, cs]
                    out.append(delta[rc] + jnp.sum(pdp, axis=1, keepdims=True))
                return tuple(out)

            zeros = tuple(jnp.zeros((rs.stop - rs.start, 1), F32) for rs, _ in pieces)
            delta = lax.fori_loop(0, n_tiles - 1, lambda j, d: sweep1(j, d, False), zeros)
            delta = sweep1(n_tiles - 1, delta, True)

            def sweep2(j, dq):
                ks = pl.ds(pl.multiple_of(j * TK, TK), TK)
                col = [None] * len(pieces[0][1])
                for rc, (rs, segs) in enumerate(pieces):
                    for sg, cs in enumerate(segs):
                        p = p_buf[j, rs, cs]
                        ds = p * (dp_buf[j, rs, cs] - delta[rc])
                        ds_scr[rs, cs] = ds.astype(BF16)
                        p16_scr[rs, cs] = p.astype(BF16)
                        col[sg] = ds if col[sg] is None else col[sg] + ds
                dfk_ref[0, 0, pl.ds(j, 1), :] -= jnp.concatenate([_csum(c) for c in col], axis=1)
                ds16 = ds_scr[...]
                dk_acc[ks, :] += lax.dot_general(ds16, qb, _TN, preferred_element_type=F32)
                dv_acc[ks, :] += lax.dot_general(p16_scr[...], dob16, _TN, preferred_element_type=F32)
                return dq + jnp.dot(ds16, k_ref[0, 0, ks, :], preferred_element_type=F32)

            dq_ref[0, 0, qs, :] = (lax.fori_loop(0, n_tiles, sweep2, jnp.zeros((T, dh), F32)) * scale).astype(dq_ref.dtype)
            return 0

        lax.fori_loop(0, nt, qloop, 0)
        dk_ref[0, 0] = (dk_acc[...] * scale).astype(dk_ref.dtype)
        dv_ref[0, 0] = dv_acc[...].astype(dv_ref.dtype)

    full = lambda w: pl.BlockSpec((1, 1, S, w), lambda b, h: (b, h, 0, 0))
    fks = pl.BlockSpec((1, 1, nkt, TK), lambda b, h: (b, h, 0, 0))
    return pl.pallas_call(
        body, name=_unique(name), grid=(B, H),
        in_specs=[full(dh), full(dh), full(dh), full(1), fks, full(dh), full(1)],
        out_specs=[full(dh), full(dh), full(dh), fks],
        out_shape=[jax.ShapeDtypeStruct((B, H, S, dh), BF16)] * 3 + [jax.ShapeDtypeStruct((B, H, nkt, TK), F32)],
        scratch_shapes=[pltpu.VMEM((nkt, T, TK), F32), pltpu.VMEM((nkt, T, TK), F32), pltpu.VMEM((T, TK), F32),
                        pltpu.VMEM((T, TK), BF16), pltpu.VMEM((T, TK), BF16), pltpu.VMEM((S, dh), F32), pltpu.VMEM((S, dh), F32)],
        compiler_params=_params(("parallel", "parallel")))(q, k, v, fq, fk, do, lse)


def _sb_terms(z, with_sigmoids=True):
    t = jnp.exp(-jnp.abs(z))
    lp = jnp.log(1.0 + t)
    lb, l1 = jnp.minimum(z, 0.0) - lp, jnp.minimum(-z, 0.0) - lp
    if not with_sigmoids:
        return lb, l1, None, None
    return lb, l1, jnp.exp(lb), jnp.exp(l1)


SCAN_RADIX = 4


def _lane_scan(x, reverse):
    lane = _iota2(x.shape, 1)
    y, d = x, 1
    while d < LANES:
        step = y
        for m in range(1, SCAN_RADIX):
            if m * d < LANES:
                if reverse:
                    step = step + jnp.where(lane + m * d < LANES, pltpu.roll(y, LANES - m * d, 1), 0.0)
                else:
                    step = step + jnp.where(lane >= m * d, pltpu.roll(y, m * d, 1), 0.0)
        y, d = step, d * SCAN_RADIX
    return y


def _chunk_scan(xs, reverse):
    n = len(xs)
    within, acc = [None] * n, None
    for s in (range(n - 1, -1, -1) if reverse else range(n)):
        acc = xs[s] if acc is None else acc + xs[s]
        within[s] = acc
    lanes = _lane_scan(acc, reverse)
    beyond = lanes - acc
    return [w + beyond for w in within], (lanes[:, 0:1] if reverse else lanes[:, LANES - 1:LANES])


SB_DEAD = 110.0


def _sb_fwd(name, q, k, v):
    B, H, S, dh = q.shape
    T, TK = _att_tiles(S, SB_TILE)
    scale = np.float32(dh ** -0.5)
    pieces = _pieces(T, TK)

    def body(q_ref, k_ref, v_ref, o_ref, lt_ref, first_ref, z_scr, a_scr):
        i = pl.program_id(2)
        qb = q_ref[0, 0]
        n_tiles = _causal_tiles(i, T, TK)

        def tile(j, carry, diag):
            runs, acc = carry
            ks = pl.ds(pl.multiple_of(j * TK, TK), TK)
            z_scr[...] = lax.dot_general(qb, k_ref[0, 0, ks, :], _NT, preferred_element_type=F32) * scale
            new_runs = []
            for rc, (rs, segs) in enumerate(pieces):
                terms = [_sb_terms(z_scr[rs, cs], False) for cs in segs]
                keep = [_piece_keep(i * T, j * TK, rs, cs, True, len(segs)) if diag else None for cs in segs]
                l1 = [jnp.where(kp, t[1], 0.0) if diag else t[1] for kp, t in zip(keep, terms)]
                right_of, total = _chunk_scan(l1, True)
                for cs, kp, t, x, r in zip(segs, keep, terms, l1, right_of):
                    a = jnp.exp(t[0] + ((r - x) + runs[rc]))
                    a_scr[rs, cs] = (jnp.where(kp, a, 0.0) if diag else a).astype(BF16)
                new_runs.append(runs[rc] + total)
            return tuple(new_runs), acc + jnp.dot(a_scr[...], v_ref[0, 0, ks, :], preferred_element_type=F32)

        init = (tuple(jnp.zeros((rs.stop - rs.start, 1), F32) for rs, _ in pieces), jnp.zeros((T, dh), F32))
        def some_row_alive(runs):
            worst = runs[0]
            for r in runs[1:]:
                worst = jnp.maximum(worst, r)
            return jnp.max(worst) > -SB_DEAD

        def step(c):
            runs, acc = tile(n_tiles - 1 - c[0], (c[1], c[2]), False)
            return c[0] + 1, runs, acc

        visited, runs, acc = lax.while_loop(lambda c: jnp.logical_and(c[0] < n_tiles, some_row_alive(c[1])), step,
                                            (jnp.int32(1), *tile(n_tiles - 1, init, True)))
        o_ref[0, 0] = acc.astype(o_ref.dtype)
        lt_ref[0, 0] = jnp.concatenate(runs, axis=0)
        first_ref[pl.program_id(0), pl.program_id(1), i] = (n_tiles - visited).astype(F32)

    full = lambda w: pl.BlockSpec((1, 1, S, w), lambda b, h, i: (b, h, 0, 0))
    blk = lambda w: pl.BlockSpec((1, 1, T, w), lambda b, h, i: (b, h, i, 0))
    return pl.pallas_call(
        body, name=_unique(name), grid=(B, H, S // T),
        in_specs=[blk(dh), full(dh), full(dh)], out_specs=[blk(dh), blk(1), pl.BlockSpec(memory_space=pltpu.SMEM)],
        out_shape=[jax.ShapeDtypeStruct((B, H, S, dh), BF16), jax.ShapeDtypeStruct((B, H, S, 1), F32),
                   jax.ShapeDtypeStruct((B, H, S // T), F32)],
        scratch_shapes=[pltpu.VMEM((T, TK), F32), pltpu.VMEM((T, TK), BF16)],
        compiler_params=_params(("arbitrary", "arbitrary", "arbitrary")))(q, k, v)


def _sb_bwd(name, q, k, v, do, lt, first):
    B, H, S, dh = q.shape
    T, TK = _att_tiles(S, SB_TILE)
    nt = S // T
    scale = np.float32(dh ** -0.5)

    pieces = _pieces(T, TK)

    def body(first_ref, q_ref, k_ref, v_ref, do_ref, lt_ref, dq_ref, dk_ref, dv_ref, z_scr, da_scr, dz_scr, a_scr, dk_acc, dv_acc):
        dk_acc[...] = jnp.zeros_like(dk_acc)
        dv_acc[...] = jnp.zeros_like(dv_acc)
        b, h = pl.program_id(0), pl.program_id(1)

        def qloop(i, _):
            qs = pl.ds(pl.multiple_of(i * T, T), T)
            qb, dob16 = q_ref[0, 0, qs, :], do_ref[0, 0, qs, :].astype(BF16)
            n_tiles = _causal_tiles(i, T, TK)
            first_tile = jnp.clip(first_ref[b, h, i].astype(jnp.int32), 0, n_tiles - 1)
            lt_c = [lt_ref[0, 0, pl.ds(pl.multiple_of(i * T + rs.start, SUBLANES), rs.stop - rs.start), :] for rs, _ in pieces]

            def tile(j, carry, diag):
                sums_l, sums_e, dq = carry
                ks = pl.ds(pl.multiple_of(j * TK, TK), TK)
                kb, vb = k_ref[0, 0, ks, :], v_ref[0, 0, ks, :]
                z_scr[...] = lax.dot_general(qb, kb, _NT, preferred_element_type=F32) * scale
                da_scr[...] = lax.dot_general(dob16, vb, _NT, preferred_element_type=F32)
                new_l, new_e = [], []
                for rc, (rs, segs) in enumerate(pieces):
                    terms = [_sb_terms(z_scr[rs, cs]) for cs in segs]
                    keep = [_piece_keep(i * T, j * TK, rs, cs, True, len(segs)) if diag else None for cs in segs]
                    l1 = [jnp.where(kp, t[1], 0.0) if diag else t[1] for kp, t in zip(keep, terms)]
                    upto, total_l = _chunk_scan(l1, False)
                    es = []
                    for cs, kp, t, u in zip(segs, keep, terms, upto):
                        a = jnp.exp(t[0] + (lt_c[rc] - (u + sums_l[rc])))
                        if diag:
                            a = jnp.where(kp, a, 0.0)
                        a_scr[rs, cs] = a.astype(BF16)
                        es.append(da_scr[rs, cs] * a)
                    e_upto, total_e = _chunk_scan(es, False)
                    for cs, kp, t, e, eu in zip(segs, keep, terms, es, e_upto):
                        dz = e * t[3] - ((eu - e) + sums_e[rc]) * t[2]
                        dz_scr[rs, cs] = (jnp.where(kp, dz, 0.0) if diag else dz).astype(BF16)
                    new_l.append(sums_l[rc] + total_l)
                    new_e.append(sums_e[rc] + total_e)
                dz16 = dz_scr[...]
                dk_acc[ks, :] += lax.dot_general(dz16, qb, _TN, preferred_element_type=F32)
                dv_acc[ks, :] += lax.dot_general(a_scr[...], dob16, _TN, preferred_element_type=F32)
                return tuple(new_l), tuple(new_e), dq + jnp.dot(dz16, kb, preferred_element_type=F32)

            zeros = tuple(jnp.zeros((rs.stop - rs.start, 1), F32) for rs, _ in pieces)
            carry = lax.fori_loop(first_tile, n_tiles - 1, lambda j, c: tile(j, c, False), (zeros, zeros, jnp.zeros((T, dh), F32)))
            dq_ref[0, 0, qs, :] = (tile(n_tiles - 1, carry, True)[2] * scale).astype(dq_ref.dtype)
            return 0

        lax.fori_loop(0, nt, qloop, 0)
        dk_ref[0, 0] = (dk_acc[...] * scale).astype(dk_ref.dtype)
        dv_ref[0, 0] = dv_acc[...].astype(dv_ref.dtype)

    full = lambda w: pl.BlockSpec((1, 1, S, w), lambda b, h: (b, h, 0, 0))
    return pl.pallas_call(
        body, name=_unique(name), grid=(B, H),
        in_specs=[pl.BlockSpec(memory_space=pltpu.SMEM), full(dh), full(dh), full(dh), full(dh), full(1)], out_specs=[full(dh)] * 3,
        out_shape=[jax.ShapeDtypeStruct((B, H, S, dh), BF16)] * 3,
        scratch_shapes=[pltpu.VMEM((T, TK), F32), pltpu.VMEM((T, TK), F32), pltpu.VMEM((T, TK), BF16), pltpu.VMEM((T, TK), BF16),
                        pltpu.VMEM((S, dh), F32), pltpu.VMEM((S, dh), F32)],
        compiler_params=_params(("parallel", "parallel")))(first, q, k, v, do, lt)


def _adamw(name, w, g, m, v):
    shape = w.shape
    n = w.size
    cols = shape[-1] if (w.ndim >= 2 and (shape[-1] % LANES == 0 or n // shape[-1] >= LANES)) else 0
    if cols:
        prep = lambda t: t.reshape(1, n // cols, cols)
    else:
        cols = LANES
        pad = (-n) % (SUBLANES * LANES)
        prep = lambda t: jnp.pad(t.reshape(-1), (0, pad), constant_values=1.0).reshape(1, (n + pad) // cols, cols)

    def fn(r, bv, cv):
        w_, g_, m_, v_ = r
        m2 = ADAM_B1 * m_ + (1.0 - ADAM_B1) * g_
        v2 = ADAM_B2 * v_ + (1.0 - ADAM_B2) * (g_ * g_)
        m_hat = m2 / (1.0 - ADAM_B1 ** ADAM_STEP)
        v_hat = v2 / (1.0 - ADAM_B2 ** ADAM_STEP)
        return [-ADAM_LR * (m_hat / (jnp.sqrt(v_hat) + ADAM_EPS) + ADAM_WD * w_), m2, v2], [], []
    outs = _rowwise(name, fn, [prep(w), prep(g), prep(m), prep(v)], out_rows=[(cols, F32)] * 3, tm=512)
    return [o.reshape(-1)[:n].reshape(shape) for o in outs]


def _sum8(name, parts):
    def fn(r, bv, cv):
        s = r[0]
        for t in r[1:]:
            s = s + t
        return [s], [], []
    rows = [parts[i][None] for i in range(parts.shape[0])]
    return _rowwise(name, fn, rows, out_rows=[(parts.shape[2], F32)])[0][0]


def _pack(arrs, cols, dtype, row_mult):
    flat = jnp.concatenate([a.reshape(-1).astype(dtype) for a in arrs])
    pad = (-flat.size) % (cols * row_mult)
    return jnp.pad(flat, (0, pad)).reshape(-1, cols)


def _unpack(flat, shapes):
    out, off = [], 0
    for s in shapes:
        n = int(np.prod(s))
        out.append(flat[off:off + n].reshape(s))
        off += n
    return out


def _heads(t, H):
    B, S, W = t.shape
    return t.reshape(B, S, H, W // H).transpose(0, 2, 1, 3)


def _unheads(t):
    B, H, S, dh = t.shape
    return t.transpose(0, 2, 1, 3).reshape(B, S, H * dh)


def kernel(*args):
    _names_used.clear()
    p = dict(zip(ARGS, args))
    x, target = p['x'], p['loss_target']
    B, S, D = x.shape
    T = B * S
    depth = p['ln1_g'].shape[0]
    H = D // HEAD_DIM
    alpha = np.float32((2.0 * depth) ** 0.25)
    cx, cy, cc = _mesh_pos()
    my_q = 2 * cx + cy
    axes = ("x", "y", "c")
    two = lambda t: t.reshape(T, t.shape[-1])
    three = lambda t: t.reshape(B, S, t.shape[-1])

    small_in = [p['c']] + [p[n] for n in SMALL_SPLIT]
    g1 = _all_gather8("ag_small", [_pack(small_in, LANES, F32, SUBLANES)])[0]
    g1 = g1.reshape(N_DEV, -1)
    c_all = g1[:, :B * D].reshape(N_DEV * B, D)
    per_chip = [_unpack(g1[2 * q], [a.shape for a in small_in])[1:] for q in range(N_CHIPS)]
    small = {n: jnp.concatenate([per_chip[q][i] for q in range(N_CHIPS)], axis=-1) for i, n in enumerate(SMALL_SPLIT)}
    for n in SMALL_REPL:
        small[n] = p[n]

    n_seq = N_DEV * B
    seq_pad = -(-n_seq // LANES) * LANES
    c_act = _rowwise("c_act", lambda r, bv, cv: ([_silu(r[0])], [], []),
                     [jnp.pad(c_all, ((0, seq_pad - n_seq), (0, 0)))[None]], out_rows=[(D, F32)])[0][0]
    mod_cols = p['mod_w'].shape[2]
    mod_part = jnp.stack([_mm(f"mod_fwd{l}", c_act, p['mod_w'][l])[:n_seq] for l in range(depth)])
    half_layers = depth // 2
    mod_half = lax.dynamic_slice_in_dim(mod_part, cc * half_layers, half_layers, axis=0)
    gm_ = _all_gather8("ag_mod", [mod_half.reshape(half_layers * n_seq, mod_cols)])[0]
    mod_all = gm_.reshape(N_CHIPS, 2, half_layers, n_seq, mod_cols).transpose(1, 2, 3, 0, 4).reshape(depth, n_seq, 6 * D)
    mod_mine = lax.dynamic_slice_in_dim(mod_all, (2 * my_q + cc) * B, B, axis=1)
    mod = _rowwise("mod_bias", lambda r, bv, cv: ([r[0] + bv[0]], [], []), [mod_mine], [p['mod_b'][:, None, :]],
                   out_rows=[(6 * D, F32)])[0]
    mods = [[mod[l, :, None, i * D:(i + 1) * D] for i in range(6)] for l in range(depth)]

    big_names = list(BIG)
    shard_shapes = [p[n].shape for n in big_names]
    half_rows = [s[0] * s[1] // 2 for s in shard_shapes]
    w_halves = [lax.dynamic_slice_in_dim(p[n].reshape(-1, s[2]), cc * hr, hr, axis=0).astype(BF16)
                for n, s, hr in zip(big_names, shard_shapes, half_rows)]
    W = {}
    for n, s, g in zip(big_names, shard_shapes, _all_gather8("ag_weights", w_halves)):
        seg = g.reshape((N_CHIPS,) + s)
        W[n] = jnp.concatenate([seg[q] for q in range(N_CHIPS)], axis=BIG[n])

    def vec(n, j):
        return small[n][j][None, :]

    def attn_proj(h1, w_in, gate_cols):
        wp = jnp.pad(w_in, ((0, 0), (0, gate_cols))) if gate_cols else w_in
        proj = three(_mm("att_proj", two(h1), wp))
        q, k, v = [_heads(proj[..., i * D:(i + 1) * D].astype(BF16), H) for i in range(3)]
        return wp, proj, q, k, v

    def gm_fwd(j, h1):
        zin = three(_mm("gm_in", two(h1), W['gm_w_in'][j]))
        u, vn = _gm_act("gm_act", zin, vec('gm_b_in', j), vec('gm_ln_g', j), vec('gm_ln_b', j))
        b_sT = small['gm_b_s'][j].T
        yg = _gm_spatial("gm_spatial", u, vn, small['gm_w_s'][j], b_sT)
        return three(_mm("gm_out", two(yg), W['gm_w_out'][j])), (zin, u, vn, b_sT, yg)

    def gm_bwd(j, h1, dy1, cache):
        zin, u, vn, b_sT, yg = cache
        g = {'gm_w_out': _mm("gm_dwout", two(yg), two(dy1), ta=True)}
        dyg = three(_mm("gm_dyg", two(dy1), W['gm_w_out'][j], tb=True))
        du, dvn, dws, dbsT = _gm_spatial_bwd("gm_spatial_bwd", dyg, u, vn, small['gm_w_s'][j], b_sT)
        dzin, g['gm_b_in'], g['gm_ln_g'], g['gm_ln_b'] = _gm_act_bwd("gm_act_bwd", zin, du, dvn, vec('gm_b_in', j), vec('gm_ln_g', j))
        g['gm_w_s'], g['gm_b_s'] = dws, dbsT.T
        g['gm_w_in'] = _mm("gm_dwin", two(h1), two(dzin), ta=True)
        return _mm("gm_dh", two(dzin), W['gm_w_in'][j], tb=True), g

    def fox_fwd(j, h1):
        wp, proj, q, k, v = attn_proj(h1, W['fox_w_in'][j], 3 * D + FOX_GATE_COLS - W['fox_w_in'].shape[2])
        fl = proj[..., 3 * D:]
        bf = jnp.pad(small['fox_b_f'][j][None, :], ((0, 0), (0, FOX_GATE_COLS - H)))
        Fh = _fox_gate_cumsum("fox_gate", fl, bf)[..., :H].transpose(0, 2, 1)
        fq, fk = Fh[..., None], Fh.reshape(B, H, -1, _att_tiles(S, FOX_TILE)[1])
        o, lse = _fox_fwd("fox_fwd", q, k, v, fq, fk)
        o2 = _unheads(o)
        return three(_mm("fox_out", two(o2), W['fox_w_out'][j])), (wp, q, k, v, fl, bf, fq, fk, lse, o2)

    def fox_bwd(j, h1, dy1, cache):
        wp, q, k, v, fl, bf, fq, fk, lse, o2 = cache
        g = {'fox_w_out': _mm("fox_dwout", two(o2), two(dy1), ta=True)}
        do = _heads(three(_mm("fox_do", two(dy1), W['fox_w_out'][j], tb=True, out_dtype=BF16)), H)
        dq, dk, dv, dfk = _fox_bwd("fox_bwd", q, k, v, fq, fk, do, lse)
        dF = jnp.pad(dfk.reshape(B, H, S).transpose(0, 2, 1), ((0, 0), (0, 0), (0, FOX_GATE_COLS - H)))
        dfl, dbf = _fox_gate_bwd("fox_gate_bwd", dF, fl, bf, H)
        dproj = jnp.concatenate([_unheads(dq).astype(BF16), _unheads(dk).astype(BF16), _unheads(dv).astype(BF16),
                                 dfl.astype(BF16)], axis=-1)
        g['fox_w_in'] = _mm("fox_dwin", two(h1), two(dproj), ta=True)[:, :W['fox_w_in'].shape[2]]
        g['fox_b_f'] = dbf[0, :H]
        return _mm("fox_dh", two(dproj), wp, tb=True), g

    def sb_fwd(j, h1):
        wp, proj, q, k, v = attn_proj(h1, W['sb_w_in'][j], 0)
        k, v = _lane_major(k, _att_tiles(S, SB_TILE)[1]), _lane_major(v, _att_tiles(S, SB_TILE)[1])
        o, lt, first = _sb_fwd("sb_fwd", q, k, v)
        o2 = _unheads(o)
        return three(_mm("sb_out", two(o2), W['sb_w_out'][j])), (q, k, v, lt, first, o2)

    def sb_bwd(j, h1, dy1, cache):
        q, k, v, lt, first, o2 = cache
        g = {'sb_w_out': _mm("sb_dwout", two(o2), two(dy1), ta=True)}
        do = _heads(three(_mm("sb_do", two(dy1), W['sb_w_out'][j], tb=True, out_dtype=BF16)), H)
        dq, dk, dv = _sb_bwd("sb_bwd", q, k, v, do, lt, first)
        dk, dv = _lane_major_inverse(dk, _att_tiles(S, SB_TILE)[1]), _lane_major_inverse(dv, _att_tiles(S, SB_TILE)[1])
        dproj = jnp.concatenate([_unheads(dq).astype(BF16), _unheads(dk).astype(BF16), _unheads(dv).astype(BF16)], axis=-1)
        g['sb_w_in'] = _mm("sb_dwin", two(h1), two(dproj), ta=True)
        return _mm("sb_dh", two(dproj), W['sb_w_in'][j], tb=True), g

    def cv_fwd(j, h1):
        pw = three(_mm("cv_in", two(h1), W['cv_w_in'][j]))
        ygl = _cv_glu("cv_glu", pw, vec('cv_b_in', j))
        dw = jnp.pad(small['cv_dw'][j], ((0, CONV_HALO - CONV_WIDTH), (0, 0)))
        yc = _dwconv("cv_dwconv", ygl, dw, vec('cv_dw_b', j))
        ys = _cv_ln_act("cv_ln_act", yc, vec('cv_ln_g', j), vec('cv_ln_b', j))
        return three(_mm("cv_out", two(ys), W['cv_w_out'][j])), (pw, ygl, dw, yc, ys)

    def cv_bwd(j, h1, dy1, cache):
        pw, ygl, dw, yc, ys = cache
        g = {'cv_w_out': _mm("cv_dwout", two(ys), two(dy1), ta=True)}
        dys = three(_mm("cv_dys", two(dy1), W['cv_w_out'][j], tb=True))
        dyc, g['cv_ln_g'], g['cv_ln_b'], g['cv_dw_b'] = _cv_ln_act_bwd("cv_ln_act_bwd", yc, dys, vec('cv_ln_g', j), vec('cv_ln_b', j))
        dygl, ddw = _dwconv_bwd("cv_dwconv_bwd", dyc, ygl, dw)
        g['cv_dw'] = ddw[:CONV_WIDTH]
        dpw, g['cv_b_in'] = _cv_glu_bwd("cv_glu_bwd", pw, dygl, vec('cv_b_in', j))
        g['cv_w_in'] = _mm("cv_dwin", two(h1), two(dpw), ta=True)
        return _mm("cv_dh", two(dpw), W['cv_w_in'][j], tb=True), g

    mixers = [(gm_fwd, gm_bwd), (fox_fwd, fox_bwd), (sb_fwd, sb_bwd), (cv_fwd, cv_bwd)]
    n_mix = len(mixers)

    saved = []
    h1 = _modulate("mod1", x, mods[0][1], mods[0][0])
    for l in range(depth):
        m, j = l % n_mix, l // n_mix
        sh1, sc1, g1_, sh2, sc2, g2_ = mods[l]
        ybias = vec('cv_b_out', j) if m == 3 else None
        y1, cache = mixers[m][0](j, h1)
        xm, h2 = _resid_ln("resid_ln1", alpha, x, y1, g1_, small['ln1_g'][l][None], small['ln1_b'][l][None], ybias, then=(sc2, sh2))
        *z, a = _ffn_in_act("ffn_in", two(h2), W['ffn_w_in'][l])
        y2 = three(_mm("ffn_out", a, W['ffn_w_out'][l]))
        nxt = (mods[l + 1][1], mods[l + 1][0]) if l + 1 < depth else None
        xo, *h_next = _resid_ln("resid_ln2", alpha, xm, y2, g2_, small['ln2_g'][l][None], small['ln2_b'][l][None], then=nxt)
        saved.append((x, h1, y1, cache, xm, h2, z, a, y2, ybias))
        x, h1 = xo, (h_next[0] if h_next else None)

    dx, sq = _loss_head("loss_head", x, target)
    loss = lax.psum(jnp.sum(sq) * np.float32(0.5 / D), axes)

    grads = {n: [None] * p[n].shape[0] for n in WEIGHTS}
    parts = [dict() for _ in range(depth)]
    after = None
    for l in reversed(range(depth)):
        m, j = l % n_mix, l // n_mix
        sh1, sc1, g1_, sh2, sc2, g2_ = mods[l]
        x_in, h1, y1, cache, xm, h2, z, a, y2, ybias = saved[l]
        ln2 = (small['ln2_g'][l][None], small['ln2_b'][l][None])
        if after is None:
            dr2, dy2, parts[l]['g2'], grads['ln2_g'][l], grads['ln2_b'][l], _ = _resid_ln_bwd("resid_ln2_bwd", alpha, dx, xm, y2, g2_, *ln2)
        else:
            (dr2, dy2, parts[l]['g2'], parts[l + 1]['sc1'], parts[l + 1]['sh1'], grads['ln2_g'][l], grads['ln2_b'][l], _) = _resid_ln_bwd(
                "resid_ln2_bwd", alpha, after[0], xm, y2, g2_, *ln2, then=after[1:])
        grads['ffn_w_out'][l] = _mm("ffn_dwout", a, two(dy2), ta=True)
        dz = _ffn_out_bwd_act("ffn_da", two(dy2), W['ffn_w_out'][l], *z)
        grads['ffn_w_in'][l] = jnp.concatenate([_mm("ffn_dwin", two(h2), t, ta=True) for t in dz], axis=1)
        dh2 = three(_ffn_in_bwd("ffn_dh", *dz, W['ffn_w_in'][l]))
        (dr1, dy1, parts[l]['g1'], parts[l]['sc2'], parts[l]['sh2'], grads['ln1_g'][l], grads['ln1_b'][l], dyb) = _resid_ln_bwd(
            "resid_ln1_bwd", alpha, dr2, x_in, y1, g1_, small['ln1_g'][l][None], small['ln1_b'][l][None], ybias, then=(dh2, sc2))
        dh1, mg = mixers[m][1](j, h1, dy1, cache)
        if m == 3:
            mg['cv_b_out'] = dyb
        for n, gval in mg.items():
            grads[n][j] = gval
        after = (dr1, three(dh1), sc1)
    grad_x, parts[0]['sc1'], parts[0]['sh1'] = _modulate_bwd("mod1_bwd", alpha, after[1], after[0], saved[0][0], after[2])
    dmod = [jnp.concatenate([pt[k] for k in ('sh1', 'sc1', 'g1', 'sh2', 'sc2', 'g2')], axis=-1)[:, 0, :] for pt in parts]
    dmod = jnp.stack(dmod)
    grads['mod_b'] = [jnp.sum(dmod[l], axis=0) for l in range(depth)]
    full_shape = {n: tuple(t.shape) for n, t in small.items()}

    small_names = SMALL_REPL + SMALL_SPLIT
    small_parts = [jnp.stack([gv.reshape(full_shape[n][1:]) for gv in grads[n]]) for n in small_names]
    pack_a = _pack([dmod], LANES, F32, SUBLANES)
    pack_b = _pack(small_parts, LANES, F32, SUBLANES)
    g2 = _all_gather8("ag_grads_small", [jnp.concatenate([pack_a, pack_b], axis=0)])[0]
    rows_a = pack_a.shape[0]
    dmod_all = g2[:, :rows_a].reshape(N_DEV, -1)[:, :dmod.size].reshape(N_DEV, depth, B, 6 * D)
    dmod_all = dmod_all.transpose(1, 0, 2, 3).reshape(depth, n_seq, 6 * D)
    small_sum = _sum8("sum_grads_small", g2[:, rows_a:]).reshape(-1)
    g_small = dict(zip(small_names, _unpack(small_sum, [full_shape[n] for n in small_names])))
    for n in SMALL_SPLIT:
        w = p[n].shape[-1]
        g_small[n] = lax.dynamic_slice_in_dim(g_small[n], my_q * w, w, axis=g_small[n].ndim - 1)

    dm_cols = lax.dynamic_slice_in_dim(dmod_all, my_q * mod_cols, mod_cols, axis=2)
    dm_cols = jnp.pad(dm_cols, ((0, 0), (0, seq_pad - n_seq), (0, 0)))
    g_mod_w = jnp.stack([_mm(f"mod_dw{l}", c_act, dm_cols[l], ta=True) for l in range(depth)])

    keep, give = [], []
    for n, s, hr in zip(big_names, shard_shapes, half_rows):
        if s[0] % 2 == 0:
            def half(c, n=n, s=s):
                layers = range(c * s[0] // 2, (c + 1) * s[0] // 2)
                return jnp.stack([jnp.concatenate([jnp.split(grads[n][l], N_CHIPS, axis=BIG[n] - 1)[q] for l in layers], axis=0)
                                  for q in range(N_CHIPS)])
            mine, other = lax.cond(cc == 0, lambda: (half(0), half(1)), lambda: (half(1), half(0)))
            keep.append(mine)
            give.append(other)
            continue
        gfull = jnp.stack(grads[n])
        g4 = jnp.stack(jnp.split(gfull, N_CHIPS, axis=BIG[n])).reshape(N_CHIPS, 2 * hr, s[2])
        keep.append(lax.dynamic_slice_in_dim(g4, cc * hr, hr, axis=1))
        give.append(lax.dynamic_slice_in_dim(g4, (1 - cc) * hr, hr, axis=1))
    got = _sibling_exchange("rs_sibling", give)
    chip_sum = [_rowwise("rs_add_sibling", lambda r, bv, cv: ([r[0] + r[1]], [], []),
                         [a.reshape(1, -1, a.shape[2]), b.reshape(1, -1, a.shape[2])],
                         out_rows=[(a.shape[2], BF16)], tm=512)[0].reshape(a.shape) for a, b in zip(keep, got)]
    from_chips = _chip_all_to_all("rs_chips", chip_sum)
    half_sum = [_rowwise("rs_add_chips", lambda r, bv, cv: ([((r[0] + r[1]) + r[2]) + r[3]], [], []),
                         [t[q][None] for q in range(N_CHIPS)], out_rows=[(t.shape[2], F32)], tm=512)[0][0]
                for t in from_chips]
    other = _sibling_exchange("rs_share", half_sum)
    g_big = {n: jnp.concatenate([jnp.where(cc == 0, a, b), jnp.where(cc == 0, b, a)], axis=0).reshape(s)
             for n, s, a, b in zip(big_names, shard_shapes, half_sum, other)}

    g_out = {**g_small, **g_big, 'mod_w': g_mod_w}
    tiny = [n for n in WEIGHTS if n not in BIG and n != 'mod_w']
    flat = lambda pre, src: jnp.concatenate([src[pre + n].reshape(-1) for n in tiny])
    tiny_upd = [_unpack(t, [p[n].shape for n in tiny])
                for t in _adamw("adamw_small", flat('', p), flat('', g_out), flat('m_', p), flat('v_', p))]
    upd = {n: [part[i] for part in tiny_upd] for i, n in enumerate(tiny)}
    upd.update({n: _adamw("adamw_" + n, p[n], g_out[n], p['m_' + n], p['v_' + n]) for n in WEIGHTS if n not in tiny})
    return (loss, grad_x, *[g_out[n] for n in WEIGHTS], *[upd[n][0] for n in WEIGHTS],
            *[upd[n][1] for n in WEIGHTS], *[upd[n][2] for n in WEIGHTS])
```
